```python
import jax, jax.numpy as jnp
from jax import lax
import numpy as np

D_MODEL = 1024
BATCH = 8
SEQ = 8192
DEPTH = 4

N_MIXERS = 2
EPS = 1e-6

SSM_EXPAND = 2
SSM_D_INNER = SSM_EXPAND * D_MODEL
SSM_HEAD_DIM = 64
SSM_HEADS = SSM_D_INNER // SSM_HEAD_DIM
SSM_GROUPS = 8
SSM_HEADS_PER_GROUP = SSM_HEADS // SSM_GROUPS
SSM_STATE = 128
SSM_CONV = 4
SSM_CHUNK = 128
SSM_BC_DIM = SSM_GROUPS * SSM_STATE
SSM_CONV_DIM = SSM_D_INNER + 2 * SSM_BC_DIM
SSM_IN_DIM = SSM_D_INNER + SSM_CONV_DIM + SSM_HEADS

ATT_HEAD_DIM = 64
ATT_Q_HEADS = D_MODEL // ATT_HEAD_DIM
ATT_KV_HEADS = 4
ATT_GQA = ATT_Q_HEADS // ATT_KV_HEADS
ATT_WIDTH = ATT_Q_HEADS * ATT_HEAD_DIM
ATT_KV_WIDTH = ATT_KV_HEADS * ATT_HEAD_DIM
ATT_IN_DIM = 2 * ATT_WIDTH + 2 * ATT_KV_WIDTH
WINDOW = 128
ATT_BLOCK = 128
ROPE_THETA = 500000.0
ROPE_DIM = ATT_HEAD_DIM // 4

N_SSM_LAYERS = (DEPTH + 1) // 2
N_ATT_LAYERS = DEPTH // 2

kernel_name = "hybrid_ssd_swa_sink_trunk"


def rmsnorm(x, w):
    xf = x.astype(jnp.float32)
    y = xf * lax.rsqrt(jnp.mean(xf * xf, axis=-1, keepdims=True) + EPS)
    return (y * w.astype(jnp.float32)).astype(x.dtype)


def causal_depthwise_conv(u, w, b):
    c = u.shape[-1]
    out = lax.conv_general_dilated(
        u, w[:, None, :].astype(u.dtype), window_strides=(1,),
        padding=((SSM_CONV - 1, 0),), dimension_numbers=('NWC', 'WIO', 'NWC'),
        feature_group_count=c)
    return out + b.astype(u.dtype)


def ssd_chunked(xs, dt, A, Bm, Cm):
    b, L = xs.shape[:2]
    c, l = L // SSM_CHUNK, SSM_CHUNK
    G, R, P, N = SSM_GROUPS, SSM_HEADS_PER_GROUP, SSM_HEAD_DIM, SSM_STATE
    x = (xs * dt[..., None]).reshape(b, c, l, G, R, P)
    a = jnp.moveaxis((dt * A).reshape(b, c, l, G, R), 2, -1)
    a_cs = jnp.cumsum(a, axis=-1)
    Bc = Bm.reshape(b, c, l, G, N)
    Cc = Cm.reshape(b, c, l, G, N)
    causal = jnp.tril(jnp.ones((l, l), dtype=bool))
    seg = a_cs[..., :, None] - a_cs[..., None, :]
    decay = jnp.exp(jnp.where(causal, seg, -jnp.inf))
    cb = jnp.einsum('bclgn,bcsgn->bcgls', Cc, Bc)
    y_diag = jnp.einsum('bcgrls,bcsgrp->bclgrp', cb[:, :, :, None] * decay, x)
    decay_to_end = jnp.exp(a_cs[..., -1:] - a_cs)
    states = jnp.einsum('bclgn,bcgrl,bclgrp->bcgrpn', Bc, decay_to_end, x)
    chunk_decay = jnp.exp(a_cs[..., -1])

    def step(h, inp):
        s, d = inp
        return h * d[..., None, None] + s, h

    h0 = jnp.zeros_like(states[:, 0])
    _, h_in = lax.scan(step, h0, (jnp.moveaxis(states, 1, 0), jnp.moveaxis(chunk_decay, 1, 0)))
    h_in = jnp.moveaxis(h_in, 0, 1)
    y_off = jnp.einsum('bclgn,bcgrpn,bcgrl->bclgrp', Cc, h_in, jnp.exp(a_cs))
    return (y_diag + y_off).reshape(b, L, SSM_HEADS, P)


def mamba2_mixer(h, w_in, conv_w, conv_b, dt_bias, a_log, d_skip, gate_norm, w_out):
    b, L, _ = h.shape
    f32 = jnp.float32
    z, xbc, dt = jnp.split(h @ w_in, [SSM_D_INNER, SSM_D_INNER + SSM_CONV_DIM], axis=-1)
    xbc = jax.nn.silu(causal_depthwise_conv(xbc, conv_w, conv_b))
    xs, Bm, Cm = jnp.split(xbc, [SSM_D_INNER, SSM_D_INNER + SSM_BC_DIM], axis=-1)
    xs = xs.reshape(b, L, SSM_HEADS, SSM_HEAD_DIM).astype(f32)
    Bm = Bm.reshape(b, L, SSM_GROUPS, SSM_STATE).astype(f32)
    Cm = Cm.reshape(b, L, SSM_GROUPS, SSM_STATE).astype(f32)
    dt = jax.nn.softplus(dt.astype(f32) + dt_bias.astype(f32))
    A = -jnp.exp(a_log.astype(f32))
    y = ssd_chunked(xs, dt, A, Bm, Cm) + d_skip.astype(f32)[:, None] * xs
    y = y.reshape(b, L, SSM_D_INNER) * jax.nn.silu(z.astype(f32))
    y = rmsnorm(y, gate_norm)
    return y.astype(h.dtype) @ w_out


def rope_tables(positions):
    inv = ROPE_THETA ** (-jnp.arange(0, ROPE_DIM, 2, dtype=jnp.float32) / ROPE_DIM)
    ang = positions.astype(jnp.float32)[..., None] * inv
    return jnp.cos(ang)[:, :, None, :], jnp.sin(ang)[:, :, None, :]


def apply_partial_rope(t, cos, sin):
    half = ROPE_DIM // 2
    t1, t2, rest = t[..., :half], t[..., half:ROPE_DIM], t[..., ROPE_DIM:]
    return jnp.concatenate([t1 * cos - t2 * sin, t2 * cos + t1 * sin, rest], axis=-1)


def swa_sink_mixer(h, cos, sin, w_in, sinks, w_out):
    b, L, _ = h.shape
    f32 = jnp.float32
    Hk, D, BLK = ATT_KV_HEADS, ATT_HEAD_DIM, ATT_BLOCK
    q, k, v, gate = jnp.split(
        h @ w_in, [ATT_WIDTH, ATT_WIDTH + ATT_KV_WIDTH, ATT_WIDTH + 2 * ATT_KV_WIDTH], axis=-1)
    q = apply_partial_rope(q.reshape(b, L, ATT_Q_HEADS, D).astype(f32), cos, sin)
    k = apply_partial_rope(k.reshape(b, L, Hk, D).astype(f32), cos, sin)
    v = v.reshape(b, L, Hk, D).astype(f32)
    nb = L // BLK
    qb = q.reshape(b, nb, BLK, Hk, ATT_GQA, D)

    def band(t):
        tp = jnp.pad(t, ((0, 0), (BLK, 0), (0, 0), (0, 0))).reshape(b, nb + 1, BLK, Hk, D)
        return jnp.concatenate([tp[:, :-1], tp[:, 1:]], axis=2)

    kb, vb = band(k), band(v)
    s = jnp.einsum('bnikgd,bnjkd->bnkgij', qb, kb) * (D ** -0.5)
    qi = jnp.arange(BLK)[:, None]
    kj = jnp.arange(2 * BLK)[None, :]
    dist = qi + BLK - kj
    blk = jnp.arange(nb)[:, None, None]
    valid = (dist >= 0) & (dist < WINDOW) & ((blk - 1) * BLK + kj >= 0)
    s = jnp.where(valid[None, :, None, None], s, -jnp.inf)
    sink = sinks.astype(f32).reshape(Hk, ATT_GQA)[None, None, :, :, None, None]
    m = jnp.maximum(jnp.max(s, axis=-1, keepdims=True), sink)
    p = jnp.exp(s - m)
    denom = jnp.sum(p, axis=-1, keepdims=True) + jnp.exp(sink - m)
    o = jnp.einsum('bnkgij,bnjkd->bnikgd', p / denom, vb)
    o = o.reshape(b, L, ATT_WIDTH) * jax.nn.silu(gate.astype(f32))
    return o.astype(h.dtype) @ w_out


def _fwd_setup_inputs(seed: int = 0) -> dict:
    key = jax.random.key(seed)
    ks = jax.random.split(key, 16)
    f32 = jnp.float32
    nS, nA = N_SSM_LAYERS, N_ATT_LAYERS
    x = jax.random.normal(ks[0], (BATCH, SEQ, D_MODEL), f32)
    positions = jnp.broadcast_to(jnp.arange(SEQ, dtype=jnp.int32), (BATCH, SEQ))
    pre_norm = 1.0 + 0.1 * jax.random.normal(ks[1], (DEPTH, D_MODEL), f32)
    post_norm = 1.0 + 0.1 * jax.random.normal(ks[2], (DEPTH, D_MODEL), f32)
    ssm_w_in = jax.random.normal(ks[3], (nS, D_MODEL, SSM_IN_DIM), f32) * D_MODEL ** -0.5
    ssm_conv_w = jax.random.normal(ks[4], (nS, SSM_CONV, SSM_CONV_DIM), f32) * SSM_CONV ** -0.5
    ssm_conv_b = 0.02 * jax.random.normal(ks[5], (nS, SSM_CONV_DIM), f32)
    dt0 = jnp.exp(jax.random.uniform(ks[6], (nS, SSM_HEADS), f32)
                  * (np.log(0.1) - np.log(0.001)) + np.log(0.001)).astype(f32)
    ssm_dt_bias = dt0 + jnp.log(-jnp.expm1(-dt0))
    ssm_a_log = jnp.log(jax.random.uniform(ks[7], (nS, SSM_HEADS), f32, 1.0, 16.0))
    ssm_d = 1.0 + 0.1 * jax.random.normal(ks[8], (nS, SSM_HEADS), f32)
    ssm_gate_norm = 1.0 + 0.1 * jax.random.normal(ks[9], (nS, SSM_D_INNER), f32)
    ssm_w_out = jax.random.normal(ks[10], (nS, SSM_D_INNER, D_MODEL), f32) * SSM_D_INNER ** -0.5
    att_w_in = jax.random.normal(ks[11], (nA, D_MODEL, ATT_IN_DIM), f32) * D_MODEL ** -0.5
    att_sinks = 0.5 * jax.random.normal(ks[12], (nA, ATT_Q_HEADS), f32)
    att_w_out = jax.random.normal(ks[13], (nA, ATT_WIDTH, D_MODEL), f32) * ATT_WIDTH ** -0.5
    return {"x": x, "positions": positions, "pre_norm": pre_norm, "post_norm": post_norm,
            "ssm_w_in": ssm_w_in, "ssm_conv_w": ssm_conv_w, "ssm_conv_b": ssm_conv_b,
            "ssm_dt_bias": ssm_dt_bias, "ssm_a_log": ssm_a_log, "ssm_d": ssm_d,
            "ssm_gate_norm": ssm_gate_norm, "ssm_w_out": ssm_w_out,
            "att_w_in": att_w_in, "att_sinks": att_sinks, "att_w_out": att_w_out}


def _fwd_reference(x, positions, pre_norm, post_norm, ssm_w_in, ssm_conv_w, ssm_conv_b,
              ssm_dt_bias, ssm_a_log, ssm_d, ssm_gate_norm, ssm_w_out,
              att_w_in, att_sinks, att_w_out):
    cos, sin = rope_tables(positions)
    for i in range(DEPTH):
        h = rmsnorm(x, pre_norm[i])
        j = i // N_MIXERS
        if i % N_MIXERS == 0:
            y = mamba2_mixer(h, ssm_w_in[j], ssm_conv_w[j], ssm_conv_b[j], ssm_dt_bias[j],
                             ssm_a_log[j], ssm_d[j], ssm_gate_norm[j], ssm_w_out[j])
        else:
            y = swa_sink_mixer(h, cos, sin, att_w_in[j], att_sinks[j], att_w_out[j])
        x = x + rmsnorm(y, post_norm[i])
    return x


import jax as _jax
import jax.numpy as _jnp

TWIN_FORMAT = 'train_step'
FWD_PARAMS = ['x', 'positions', 'pre_norm', 'post_norm', 'ssm_w_in', 'ssm_conv_w', 'ssm_conv_b', 'ssm_dt_bias', 'ssm_a_log', 'ssm_d', 'ssm_gate_norm', 'ssm_w_out', 'att_w_in', 'att_sinks', 'att_w_out']
TWIN_WEIGHTS = ['pre_norm', 'post_norm', 'ssm_w_in', 'ssm_conv_w', 'ssm_conv_b', 'ssm_dt_bias', 'ssm_a_log', 'ssm_d', 'ssm_gate_norm', 'ssm_w_out', 'att_w_in', 'att_sinks', 'att_w_out']
TWIN_DIFF_INPUT = 'x'
TWIN_INPUTS = ['x', 'positions', 'pre_norm', 'post_norm', 'ssm_w_in', 'ssm_conv_w', 'ssm_conv_b', 'ssm_dt_bias', 'ssm_a_log', 'ssm_d', 'ssm_gate_norm', 'ssm_w_out', 'att_w_in', 'att_sinks', 'att_w_out', 'loss_target', 'm_pre_norm', 'm_post_norm', 'm_ssm_w_in', 'm_ssm_conv_w', 'm_ssm_conv_b', 'm_ssm_dt_bias', 'm_ssm_a_log', 'm_ssm_d', 'm_ssm_gate_norm', 'm_ssm_w_out', 'm_att_w_in', 'm_att_sinks', 'm_att_w_out', 'v_pre_norm', 'v_post_norm', 'v_ssm_w_in', 'v_ssm_conv_w', 'v_ssm_conv_b', 'v_ssm_dt_bias', 'v_ssm_a_log', 'v_ssm_d', 'v_ssm_gate_norm', 'v_ssm_w_out', 'v_att_w_in', 'v_att_sinks', 'v_att_w_out']
TWIN_OUTPUTS = ['loss', 'grad_x', 'grad_pre_norm', 'grad_post_norm', 'grad_ssm_w_in', 'grad_ssm_conv_w', 'grad_ssm_conv_b', 'grad_ssm_dt_bias', 'grad_ssm_a_log', 'grad_ssm_d', 'grad_ssm_gate_norm', 'grad_ssm_w_out', 'grad_att_w_in', 'grad_att_sinks', 'grad_att_w_out', 'delta_pre_norm', 'delta_post_norm', 'delta_ssm_w_in', 'delta_ssm_conv_w', 'delta_ssm_conv_b', 'delta_ssm_dt_bias', 'delta_ssm_a_log', 'delta_ssm_d', 'delta_ssm_gate_norm', 'delta_ssm_w_out', 'delta_att_w_in', 'delta_att_sinks', 'delta_att_w_out', 'new_m_pre_norm', 'new_m_post_norm', 'new_m_ssm_w_in', 'new_m_ssm_conv_w', 'new_m_ssm_conv_b', 'new_m_ssm_dt_bias', 'new_m_ssm_a_log', 'new_m_ssm_d', 'new_m_ssm_gate_norm', 'new_m_ssm_w_out', 'new_m_att_w_in', 'new_m_att_sinks', 'new_m_att_w_out', 'new_v_pre_norm', 'new_v_post_norm', 'new_v_ssm_w_in', 'new_v_ssm_conv_w', 'new_v_ssm_conv_b', 'new_v_ssm_dt_bias', 'new_v_ssm_a_log', 'new_v_ssm_d', 'new_v_ssm_gate_norm', 'new_v_ssm_w_out', 'new_v_att_w_in', 'new_v_att_sinks', 'new_v_att_w_out']
TWIN_LEAF_KINDS = {'loss': 'loss', 'grad_x': 'grad_x', 'grad_pre_norm': 'grad_w', 'grad_post_norm': 'grad_w', 'grad_ssm_w_in': 'grad_w', 'grad_ssm_conv_w': 'grad_w', 'grad_ssm_conv_b': 'grad_w', 'grad_ssm_dt_bias': 'grad_w', 'grad_ssm_a_log': 'grad_w', 'grad_ssm_d': 'grad_w', 'grad_ssm_gate_norm': 'grad_w', 'grad_ssm_w_out': 'grad_w', 'grad_att_w_in': 'grad_w', 'grad_att_sinks': 'grad_w', 'grad_att_w_out': 'grad_w', 'delta_pre_norm': 'delta_w', 'delta_post_norm': 'delta_w', 'delta_ssm_w_in': 'delta_w', 'delta_ssm_conv_w': 'delta_w', 'delta_ssm_conv_b': 'delta_w', 'delta_ssm_dt_bias': 'delta_w', 'delta_ssm_a_log': 'delta_w', 'delta_ssm_d': 'delta_w', 'delta_ssm_gate_norm': 'delta_w', 'delta_ssm_w_out': 'delta_w', 'delta_att_w_in': 'delta_w', 'delta_att_sinks': 'delta_w', 'delta_att_w_out': 'delta_w', 'new_m_pre_norm': 'new_m', 'new_m_post_norm': 'new_m', 'new_m_ssm_w_in': 'new_m', 'new_m_ssm_conv_w': 'new_m', 'new_m_ssm_conv_b': 'new_m', 'new_m_ssm_dt_bias': 'new_m', 'new_m_ssm_a_log': 'new_m', 'new_m_ssm_d': 'new_m', 'new_m_ssm_gate_norm': 'new_m', 'new_m_ssm_w_out': 'new_m', 'new_m_att_w_in': 'new_m', 'new_m_att_sinks': 'new_m', 'new_m_att_w_out': 'new_m', 'new_v_pre_norm': 'new_v', 'new_v_post_norm': 'new_v', 'new_v_ssm_w_in': 'new_v', 'new_v_ssm_conv_w': 'new_v', 'new_v_ssm_conv_b': 'new_v', 'new_v_ssm_dt_bias': 'new_v', 'new_v_ssm_a_log': 'new_v', 'new_v_ssm_d': 'new_v', 'new_v_ssm_gate_norm': 'new_v', 'new_v_ssm_w_out': 'new_v', 'new_v_att_w_in': 'new_v', 'new_v_att_sinks': 'new_v', 'new_v_att_w_out': 'new_v'}


def _forward(args):
    return _fwd_reference(*[args[k] for k in FWD_PARAMS])


def _output_shape():
    def fwd():
        inp = _fwd_setup_inputs(0)
        return _fwd_reference(*[inp[k] for k in FWD_PARAMS])
    out = _jax.eval_shape(fwd)
    return out.shape, out.dtype

N_MICROBATCH = 1
ADAM_LR = 0.001
ADAM_B1 = 0.9
ADAM_B2 = 0.999
ADAM_EPS = 1e-08
ADAM_WD = 0.01
ADAM_STEP = 10
PER_EXAMPLE_BATCH_AXIS = {'x': 0, 'positions': 0, 'loss_target': 0}
SHARED_INPUTS = []
_WEIGHT_DTYPES = {'pre_norm': _jnp.float32, 'post_norm': _jnp.float32, 'ssm_w_in': _jnp.float32, 'ssm_conv_w': _jnp.float32, 'ssm_conv_b': _jnp.float32, 'ssm_dt_bias': _jnp.float32, 'ssm_a_log': _jnp.float32, 'ssm_d': _jnp.float32, 'ssm_gate_norm': _jnp.float32, 'ssm_w_out': _jnp.float32, 'att_w_in': _jnp.float32, 'att_sinks': _jnp.float32, 'att_w_out': _jnp.float32}
MOMENT_SCALE = {'pre_norm': 5.196824e+00, 'post_norm': 6.523940e+01, 'ssm_w_in': 1.093300e+00, 'ssm_conv_w': 2.973658e+00, 'ssm_conv_b': 9.790397e+00, 'ssm_dt_bias': 2.822953e+00, 'ssm_a_log': 2.204634e+01, 'ssm_d': 2.661335e+01, 'ssm_gate_norm': 6.844428e+00, 'ssm_w_out': 1.019491e+01, 'att_w_in': 4.928191e+00, 'att_sinks': 3.006804e-01, 'att_w_out': 7.931713e+00}


def _to_microbatches(a, axis):
    t = _jnp.moveaxis(a, axis, 0)
    t = t.reshape((N_MICROBATCH, t.shape[0] // N_MICROBATCH) + t.shape[1:])
    return _jnp.moveaxis(t, 1, axis + 1)


def setup_inputs(seed: int = 0) -> dict:
    inp = _fwd_setup_inputs(seed)
    key = _jax.random.fold_in(_jax.random.key(seed), 7919)
    shape, _ = _output_shape()
    out = dict(inp)
    out["loss_target"] = _jax.random.normal(_jax.random.fold_in(key, 0), shape, _jnp.float32)
    for i, name in enumerate(TWIN_WEIGHTS):
        w = inp[name].astype(_jnp.float32)
        if MOMENT_SCALE is None:
            s = _jnp.sqrt(_jnp.mean(_jnp.square(w)) + 1e-30)
        else:
            s = MOMENT_SCALE[name]
        km, kv = _jax.random.split(_jax.random.fold_in(key, i + 1))
        out[name] = w
        out["m_" + name] = s * _jax.random.normal(km, w.shape, _jnp.float32)
        out["v_" + name] = (s * s) * _jax.random.uniform(kv, w.shape, _jnp.float32, 0.5, 1.5)
    if N_MICROBATCH > 1:
        for name, axis in PER_EXAMPLE_BATCH_AXIS.items():
            out[name] = _to_microbatches(out[name], axis)
    return {'x': out['x'], 'positions': out['positions'], 'pre_norm': out['pre_norm'], 'post_norm': out['post_norm'], 'ssm_w_in': out['ssm_w_in'], 'ssm_conv_w': out['ssm_conv_w'], 'ssm_conv_b': out['ssm_conv_b'], 'ssm_dt_bias': out['ssm_dt_bias'], 'ssm_a_log': out['ssm_a_log'], 'ssm_d': out['ssm_d'], 'ssm_gate_norm': out['ssm_gate_norm'], 'ssm_w_out': out['ssm_w_out'], 'att_w_in': out['att_w_in'], 'att_sinks': out['att_sinks'], 'att_w_out': out['att_w_out'], 'loss_target': out['loss_target'], 'm_pre_norm': out['m_pre_norm'], 'm_post_norm': out['m_post_norm'], 'm_ssm_w_in': out['m_ssm_w_in'], 'm_ssm_conv_w': out['m_ssm_conv_w'], 'm_ssm_conv_b': out['m_ssm_conv_b'], 'm_ssm_dt_bias': out['m_ssm_dt_bias'], 'm_ssm_a_log': out['m_ssm_a_log'], 'm_ssm_d': out['m_ssm_d'], 'm_ssm_gate_norm': out['m_ssm_gate_norm'], 'm_ssm_w_out': out['m_ssm_w_out'], 'm_att_w_in': out['m_att_w_in'], 'm_att_sinks': out['m_att_sinks'], 'm_att_w_out': out['m_att_w_out'], 'v_pre_norm': out['v_pre_norm'], 'v_post_norm': out['v_post_norm'], 'v_ssm_w_in': out['v_ssm_w_in'], 'v_ssm_conv_w': out['v_ssm_conv_w'], 'v_ssm_conv_b': out['v_ssm_conv_b'], 'v_ssm_dt_bias': out['v_ssm_dt_bias'], 'v_ssm_a_log': out['v_ssm_a_log'], 'v_ssm_d': out['v_ssm_d'], 'v_ssm_gate_norm': out['v_ssm_gate_norm'], 'v_ssm_w_out': out['v_ssm_w_out'], 'v_att_w_in': out['v_att_w_in'], 'v_att_sinks': out['v_att_sinks'], 'v_att_w_out': out['v_att_w_out']}


def _loss(weights, diff, rest, loss_target):
    with _jax.named_scope("forward"):
        args = {**rest, TWIN_DIFF_INPUT: diff, **{k: w.astype(_WEIGHT_DTYPES[k]) for k, w in weights.items()}}
        y = _forward(args)
    with _jax.named_scope("loss_head"):
        err = _jnp.square(y.astype(_jnp.float32) - loss_target)
        return 0.5 * _jnp.sum(_jnp.mean(err, axis=-1)) if err.ndim else 0.5 * err


def _adamw(w, g, m, v):
    m = ADAM_B1 * m + (1.0 - ADAM_B1) * g
    v = ADAM_B2 * v + (1.0 - ADAM_B2) * _jnp.square(g)
    m_hat = m / (1.0 - ADAM_B1 ** ADAM_STEP)
    v_hat = v / (1.0 - ADAM_B2 ** ADAM_STEP)
    delta = -ADAM_LR * (m_hat / (_jnp.sqrt(v_hat) + ADAM_EPS) + ADAM_WD * w)
    return delta, m, v


def reference(x, positions, pre_norm, post_norm, ssm_w_in, ssm_conv_w, ssm_conv_b, ssm_dt_bias, ssm_a_log, ssm_d, ssm_gate_norm, ssm_w_out, att_w_in, att_sinks, att_w_out, loss_target, m_pre_norm, m_post_norm, m_ssm_w_in, m_ssm_conv_w, m_ssm_conv_b, m_ssm_dt_bias, m_ssm_a_log, m_ssm_d, m_ssm_gate_norm, m_ssm_w_out, m_att_w_in, m_att_sinks, m_att_w_out, v_pre_norm, v_post_norm, v_ssm_w_in, v_ssm_conv_w, v_ssm_conv_b, v_ssm_dt_bias, v_ssm_a_log, v_ssm_d, v_ssm_gate_norm, v_ssm_w_out, v_att_w_in, v_att_sinks, v_att_w_out):
    given = dict(x=x, positions=positions, pre_norm=pre_norm, post_norm=post_norm, ssm_w_in=ssm_w_in, ssm_conv_w=ssm_conv_w, ssm_conv_b=ssm_conv_b, ssm_dt_bias=ssm_dt_bias, ssm_a_log=ssm_a_log, ssm_d=ssm_d, ssm_gate_norm=ssm_gate_norm, ssm_w_out=ssm_w_out, att_w_in=att_w_in, att_sinks=att_sinks, att_w_out=att_w_out, loss_target=loss_target, m_pre_norm=m_pre_norm, m_post_norm=m_post_norm, m_ssm_w_in=m_ssm_w_in, m_ssm_conv_w=m_ssm_conv_w, m_ssm_conv_b=m_ssm_conv_b, m_ssm_dt_bias=m_ssm_dt_bias, m_ssm_a_log=m_ssm_a_log, m_ssm_d=m_ssm_d, m_ssm_gate_norm=m_ssm_gate_norm, m_ssm_w_out=m_ssm_w_out, m_att_w_in=m_att_w_in, m_att_sinks=m_att_sinks, m_att_w_out=m_att_w_out, v_pre_norm=v_pre_norm, v_post_norm=v_post_norm, v_ssm_w_in=v_ssm_w_in, v_ssm_conv_w=v_ssm_conv_w, v_ssm_conv_b=v_ssm_conv_b, v_ssm_dt_bias=v_ssm_dt_bias, v_ssm_a_log=v_ssm_a_log, v_ssm_d=v_ssm_d, v_ssm_gate_norm=v_ssm_gate_norm, v_ssm_w_out=v_ssm_w_out, v_att_w_in=v_att_w_in, v_att_sinks=v_att_sinks, v_att_w_out=v_att_w_out)
    weights = {n: given[n] for n in TWIN_WEIGHTS}
    shared = {n: given[n] for n in SHARED_INPUTS}
    per_example = {n: given[n] for n in ['x', 'positions']}
    grad_fn = _jax.value_and_grad(_loss, argnums=(0, 1))

    def one_microbatch(ex, loss_target):
        ex = dict(ex)
        diff = ex.pop(TWIN_DIFF_INPUT)
        return grad_fn(weights, diff, {**shared, **ex}, loss_target)

    if N_MICROBATCH == 1:
        loss, (grad_w, grad_x) = one_microbatch(per_example, given["loss_target"])
    else:
        def body(carry, xs):
            loss_sum, grad_sum = carry
            l_k, (gw_k, gx_k) = one_microbatch(xs[0], xs[1])
            with _jax.named_scope("update"):
                return (loss_sum + l_k, _jax.tree.map(_jnp.add, grad_sum, gw_k)), gx_k

        init = (_jnp.zeros((), _jnp.float32), _jax.tree.map(_jnp.zeros_like, weights))
        (loss, grad_w), grad_x = _jax.lax.scan(body, init, (per_example, given["loss_target"]))
    with _jax.named_scope("update"):
        delta_w, new_m, new_v = {}, {}, {}
        for n in TWIN_WEIGHTS:
            delta_w[n], new_m[n], new_v[n] = _adamw(weights[n], grad_w[n], given["m_" + n], given["v_" + n])
    return (loss, grad_x, *[grad_w[n] for n in TWIN_WEIGHTS], *[delta_w[n] for n in TWIN_WEIGHTS],
            *[new_m[n] for n in TWIN_WEIGHTS], *[new_v[n] for n in TWIN_WEIGHTS])
```

```python
import functools

import jax
import jax.numpy as jnp
from jax import lax
from jax.experimental import pallas as pl
from jax.experimental.pallas import tpu as pltpu

f32 = jnp.float32
MXU_DTYPE = jnp.bfloat16
SDS = jax.ShapeDtypeStruct
MESH = pl.DeviceIdType.MESH

D_MODEL = 1024
EPS = 1e-6
TILE = 256
CHUNK = 128
SSM_HEADS = 32
SSM_GROUPS = 8
SSM_P = 64
SSM_N = 128
SSM_INNER = 2048
SSM_IN = 6176
SSM_NT = 25
SSM_KT = 8
ATT_NT = 10
ATT_KT = 4
ATT_IN = 2560
ROPE_THETA = 500000.0
N_DEV = 8
VMEM_LIMIT = 56 * 1024 * 1024

ADAM_LR = 0.001
ADAM_B1 = 0.9
ADAM_B2 = 0.999
ADAM_EPS = 1e-08
ADAM_WD = 0.01
ADAM_STEP = 10

ROWS_SSM_IN = 2 * 1024 * 772 // 1024
ROWS_SSM_OUT = 2 * 256
ROWS_ATT_IN = 2 * 1024 * 320 // 1024
ROWS_ATT_OUT = 2 * 128
ROWS_CONV = 8
ROWS_BIG = ROWS_SSM_IN + ROWS_SSM_OUT + ROWS_ATT_IN + ROWS_ATT_OUT + ROWS_CONV
ROWS_SMALL = 40


def _pc(body, name, **kw):
    return pl.pallas_call(body, name=name, **kw)


def _params(sem):
    return pltpu.CompilerParams(dimension_semantics=sem, vmem_limit_bytes=VMEM_LIMIT)


def _sigmoid(x):
    return 1.0 / (1.0 + jnp.exp(-x))


def _silu(x):
    return x * _sigmoid(x)


def _softplus(x):
    return jnp.maximum(x, 0.0) + jnp.log(1.0 + jnp.exp(-jnp.abs(x)))


def _mx(x):
    return x.astype(MXU_DTYPE)


def _dot(a, b):
    return jnp.dot(_mx(a), _mx(b), preferred_element_type=f32)


def _dot_nt(a, b):
    return lax.dot_general(_mx(a), _mx(b), (((1,), (1,)), ((), ())), preferred_element_type=f32)


def _dot_tn(a, b):
    return lax.dot_general(_mx(a), _mx(b), (((0,), (0,)), ((), ())), preferred_element_type=f32)


def _rms_fwd(x, w):
    r = lax.rsqrt(jnp.mean(x * x, axis=-1, keepdims=True) + EPS)
    return x * r * w


def _rms_bwd(x, w, dy):
    r = lax.rsqrt(jnp.mean(x * x, axis=-1, keepdims=True) + EPS)
    xh = x * r
    dw = jnp.sum(dy * xh, axis=0, keepdims=True)
    g = dy * w
    dx = r * (g - xh * jnp.mean(g * xh, axis=-1, keepdims=True))
    return dx, dw


def _mm_in(x, wn, w, ntb, tm, name):
    t_len, d = x.shape
    nt = w.shape[1] // TILE

    def body(x_ref, wn_ref, w_ref, u_ref, h_ref):
        @pl.when(pl.program_id(1) == 0)
        def _():
            h_ref[...] = _rms_fwd(x_ref[...], wn_ref[...]).astype(h_ref.dtype)
        h = h_ref[...]
        for t in range(ntb):
            u_ref[t] = jnp.dot(h, w_ref[:, TILE * t:TILE * (t + 1)], preferred_element_type=f32)

    return _pc(
        body, name, grid=(t_len // tm, nt // ntb),
        in_specs=[pl.BlockSpec((tm, d), lambda i, j: (i, 0)),
                  pl.BlockSpec((1, d), lambda i, j: (0, 0)),
                  pl.BlockSpec((d, ntb * TILE), lambda i, j: (0, j))],
        out_specs=[pl.BlockSpec((ntb, tm, TILE), lambda i, j: (j, i, 0)),
                   pl.BlockSpec((tm, d), lambda i, j: (i, 0))],
        out_shape=[SDS((nt, t_len, TILE), f32), SDS((t_len, d), MXU_DTYPE)],
        compiler_params=_params(("arbitrary", "arbitrary")),
    )(x, wn, w)


def _mm_dout(y, dxn, wn, wt, ntb, tm, name):
    t_len, d = y.shape
    nt = wt.shape[1] // TILE

    def body(y_ref, dxn_ref, wn_ref, w_ref, da_ref, dy_ref, dwn_ref):
        i, j = pl.program_id(0), pl.program_id(1)

        @pl.when((i == 0) & (j == 0))
        def _():
            dwn_ref[...] = jnp.zeros_like(dwn_ref)

        @pl.when(j == 0)
        def _():
            dy, dw = _rms_bwd(y_ref[...], wn_ref[...], dxn_ref[...])
            dy_ref[...] = dy.astype(dy_ref.dtype)
            dwn_ref[...] += dw
        dy = dy_ref[...]
        for t in range(ntb):
            da_ref[t] = jnp.dot(dy, w_ref[:, TILE * t:TILE * (t + 1)], preferred_element_type=f32)

    return _pc(
        body, name, grid=(t_len // tm, nt // ntb),
        in_specs=[pl.BlockSpec((tm, d), lambda i, j: (i, 0)),
                  pl.BlockSpec((tm, d), lambda i, j: (i, 0)),
                  pl.BlockSpec((1, d), lambda i, j: (0, 0)),
                  pl.BlockSpec((d, ntb * TILE), lambda i, j: (0, j))],
        out_specs=[pl.BlockSpec((ntb, tm, TILE), lambda i, j: (j, i, 0)),
                   pl.BlockSpec((tm, d), lambda i, j: (i, 0)),
                   pl.BlockSpec((1, d), lambda i, j: (0, 0))],
        out_shape=[SDS((nt, t_len, TILE), f32), SDS((t_len, d), MXU_DTYPE), SDS((1, d), f32)],
        compiler_params=_params(("arbitrary", "arbitrary")),
    )(y, dxn, wn, wt)


def _mm_out(a, w, x, wn, ktb, tm, name):
    kt, t_len, _ = a.shape
    d = w.shape[1]
    nk = kt // ktb

    def body(a_ref, w_ref, x_ref, wn_ref, y_ref, xn_ref, acc):
        k = pl.program_id(1)

        @pl.when(k == 0)
        def _():
            acc[...] = jnp.zeros_like(acc)
        s = acc[...]
        for t in range(ktb):
            s = s + jnp.dot(a_ref[t], w_ref[TILE * t:TILE * (t + 1), :], preferred_element_type=f32)
        acc[...] = s

        @pl.when(k == nk - 1)
        def _():
            y = acc[...]
            y_ref[...] = y
            xn_ref[...] = x_ref[...] + _rms_fwd(y, wn_ref[...])

    return _pc(
        body, name, grid=(t_len // tm, nk),
        in_specs=[pl.BlockSpec((ktb, tm, TILE), lambda i, k: (k, i, 0)),
                  pl.BlockSpec((ktb * TILE, d), lambda i, k: (k, 0)),
                  pl.BlockSpec((tm, d), lambda i, k: (i, 0)),
                  pl.BlockSpec((1, d), lambda i, k: (0, 0))],
        out_specs=[pl.BlockSpec((tm, d), lambda i, k: (i, 0)),
                   pl.BlockSpec((tm, d), lambda i, k: (i, 0))],
        out_shape=[SDS((t_len, d), f32), SDS((t_len, d), f32)],
        scratch_shapes=[pltpu.VMEM((tm, d), f32)],
        compiler_params=_params(("arbitrary", "arbitrary")),
    )(a, w, x, wn)


def _mm_dh(du, wt, x, dxn, wn, ktb, tm, name):
    kt, t_len, _ = du.shape
    d = wt.shape[1]
    nk = kt // ktb

    def body(du_ref, w_ref, x_ref, dxn_ref, wn_ref, dx_ref, dwn_ref, acc):
        i, k = pl.program_id(0), pl.program_id(1)

        @pl.when((i == 0) & (k == 0))
        def _():
            dwn_ref[...] = jnp.zeros_like(dwn_ref)

        @pl.when(k == 0)
        def _():
            acc[...] = jnp.zeros_like(acc)
        s = acc[...]
        for t in range(ktb):
            s = s + jnp.dot(_mx(du_ref[t]), w_ref[TILE * t:TILE * (t + 1), :], preferred_element_type=f32)
        acc[...] = s

        @pl.when(k == nk - 1)
        def _():
            dxp, dw = _rms_bwd(x_ref[...], wn_ref[...], acc[...])
            dx_ref[...] = dxn_ref[...] + dxp
            dwn_ref[...] += dw

    return _pc(
        body, name, grid=(t_len // tm, nk),
        in_specs=[pl.BlockSpec((ktb, tm, TILE), lambda i, k: (k, i, 0)),
                  pl.BlockSpec((ktb * TILE, d), lambda i, k: (k, 0)),
                  pl.BlockSpec((tm, d), lambda i, k: (i, 0)),
                  pl.BlockSpec((tm, d), lambda i, k: (i, 0)),
                  pl.BlockSpec((1, d), lambda i, k: (0, 0))],
        out_specs=[pl.BlockSpec((tm, d), lambda i, k: (i, 0)),
                   pl.BlockSpec((1, d), lambda i, k: (0, 0))],
        out_shape=[SDS((t_len, d), f32), SDS((1, d), f32)],
        scratch_shapes=[pltpu.VMEM((tm, d), f32)],
        compiler_params=_params(("arbitrary", "arbitrary")),
    )(du, wt, x, dxn, wn)


def _dw_cols(a, b, ntb, tk, name):
    t_len, kdim = a.shape
    nt = b.shape[0]

    def body(a_ref, b_ref, o_ref):
        @pl.when(pl.program_id(1) == 0)
        def _():
            o_ref[...] = jnp.zeros_like(o_ref)
        av = a_ref[...]
        for s in range(ntb):
            o_ref[:, TILE * s:TILE * (s + 1)] += _dot_tn(av, b_ref[s])

    return _pc(
        body, name, grid=(nt // ntb, t_len // tk),
        in_specs=[pl.BlockSpec((tk, kdim), lambda j, t: (t, 0)),
                  pl.BlockSpec((ntb, tk, TILE), lambda j, t: (j, t, 0))],
        out_specs=pl.BlockSpec((kdim, ntb * TILE), lambda j, t: (0, j)),
        out_shape=SDS((kdim, nt * TILE), f32),
        compiler_params=_params(("arbitrary", "arbitrary")),
    )(a, b)


def _dw_rows(a, b, ktb, tk, name):
    kt, t_len, _ = a.shape
    d = b.shape[1]

    def body(a_ref, b_ref, o_ref):
        @pl.when(pl.program_id(1) == 0)
        def _():
            o_ref[...] = jnp.zeros_like(o_ref)
        bv = b_ref[...]
        for s in range(ktb):
            o_ref[TILE * s:TILE * (s + 1), :] += _dot_tn(a_ref[s], bv)

    return _pc(
        body, name, grid=(kt // ktb, t_len // tk),
        in_specs=[pl.BlockSpec((ktb, tk, TILE), lambda k, t: (k, t, 0)),
                  pl.BlockSpec((tk, d), lambda k, t: (t, 0))],
        out_specs=pl.BlockSpec((ktb * TILE, d), lambda k, t: (k, 0)),
        out_shape=SDS((kt * TILE, d), f32),
        compiler_params=_params(("arbitrary", "arbitrary")),
    )(a, b)


def _loss_grad(x, tgt, tm):
    t_len, d = x.shape

    def body(x_ref, t_ref, dx_ref, l_ref):
        @pl.when(pl.program_id(0) == 0)
        def _():
            l_ref[...] = jnp.zeros_like(l_ref)
        e = x_ref[...] - t_ref[...]
        dx_ref[...] = e * (1.0 / d)
        row = jnp.mean(e * e, axis=-1, keepdims=True)
        l_ref[...] += 0.5 * jnp.sum(row, axis=0, keepdims=True)

    return _pc(
        body, "loss_grad", grid=(t_len // tm,),
        in_specs=[pl.BlockSpec((tm, d), lambda i: (i, 0)), pl.BlockSpec((tm, d), lambda i: (i, 0))],
        out_specs=[pl.BlockSpec((tm, d), lambda i: (i, 0)), pl.BlockSpec((1, 128), lambda i: (0, 0))],
        out_shape=[SDS((t_len, d), f32), SDS((1, 128), f32)],
        compiler_params=_params(("arbitrary",)),
    )(x, tgt)


def _tri(lower):
    r = lax.broadcasted_iota(jnp.int32, (CHUNK, CHUNK), 0)
    c = lax.broadcasted_iota(jnp.int32, (CHUNK, CHUNK), 1)
    return ((c <= r) if lower else (c >= r)).astype(f32)


def _dot_hi(a, b, dims):
    return lax.dot_general(a, b, (dims, ((), ())), precision=lax.Precision.HIGHEST, preferred_element_type=f32)


def _dt_path(dt_raw, dtb, alog):
    dtr = dt_raw + dtb
    dt = _softplus(dtr)
    a_neg = -jnp.exp(alog)
    a = dt * a_neg
    acs = _dot_hi(_tri(True), a, ((1,), (0,)))
    acs_t = _dot_hi(a, _tri(False), ((0,), (0,)))
    return dtr, dt, a_neg, acs, acs_t


def _conv_silu(x0, x1, x2, x3, w0, w1, w2, w3, b):
    acc = b + w0 * x0 + w1 * x1 + w2 * x2 + w3 * x3
    return _silu(acc)


def _ssd_group(xs, bm, cm, dtc, ac, ar, dh, hp):
    row = lax.broadcasted_iota(jnp.int32, (CHUNK, CHUNK), 0)
    col = lax.broadcasted_iota(jnp.int32, (CHUNK, CHUNK), 1)
    causal = col <= row
    last = (lax.broadcasted_iota(jnp.int32, (1, CHUNK), 1) == CHUNK - 1).astype(f32)
    cb = _dot_nt(cm, bm)
    ys, hns = [], []
    for r in range(4):
        xt = xs[r] * dtc[r]
        decay = jnp.exp(jnp.where(causal, ac[r] - ar[r], -jnp.inf))
        y_diag = _dot(cb * decay, xt)
        a_end = jnp.sum(ar[r] * last, axis=1, keepdims=True)
        y_off = _dot_nt(cm, hp[r]) * jnp.exp(ac[r])
        st = _dot_tn(xt * jnp.exp(a_end - ac[r]), bm)
        hns.append(hp[r] * jnp.exp(a_end) + st)
        ys.append(y_diag + y_off + dh[r] * xs[r])
    return tuple(ys), tuple(hns)


def _head_cols(g, r, dt, acs, acs_t, dsk):
    lane = lax.broadcasted_iota(jnp.int32, (1, SSM_HEADS), 1)
    sub = lax.broadcasted_iota(jnp.int32, (SSM_HEADS, 1), 0)
    h = 4 * g + r
    oh_l = (lane == h).astype(f32)
    oh_s = (sub == h).astype(f32)
    dtc = jnp.sum(dt * oh_l, axis=1, keepdims=True)
    ac = jnp.sum(acs * oh_l, axis=1, keepdims=True)
    ar = jnp.sum(acs_t * oh_s, axis=0, keepdims=True)
    dh = jnp.sum(dsk * oh_l, axis=1, keepdims=True)
    return h, oh_l, oh_s, dtc, ac, ar, dh


def _ssm_in_specs(nc, rev):
    def cidx(c):
        return (nc - 1 - c) if rev else c
    return [
        pl.BlockSpec((SSM_NT, CHUNK, TILE), lambda c: (0, cidx(c), 0)),
        pl.BlockSpec((SSM_NT, 8, TILE), lambda c: (0, jnp.maximum(cidx(c) * (CHUNK // 8) - 1, 0), 0)),
        pl.BlockSpec((16, 4, TILE), lambda c: (0, 0, 0)),
        pl.BlockSpec((16, 1, TILE), lambda c: (0, 0, 0)),
        pl.BlockSpec((1, SSM_HEADS), lambda c: (0, 0)),
        pl.BlockSpec((1, SSM_HEADS), lambda c: (0, 0)),
        pl.BlockSpec((1, SSM_HEADS), lambda c: (0, 0)),
        pl.BlockSpec((SSM_KT, 1, TILE), lambda c: (0, 0, 0)),
    ]


def _ssm_fwd(u, cw, cb, dtb, alog, dsk, gn, name):
    t_len = u.shape[1]
    nc = t_len // CHUNK

    def body(u_ref, halo_ref, cw_ref, cb_ref, dtb_ref, alog_ref, dsk_ref, gn_ref,
             a3_ref, yp_ref, hst_ref, h_s, win_s, xs_s, bc_s):
        c = pl.program_id(0)

        @pl.when(c == 0)
        def _():
            h_s[...] = jnp.zeros_like(h_s)
        _, dt, _, acs, acs_t = _dt_path(u_ref[SSM_NT - 1, :, 0:SSM_HEADS], dtb_ref[...], alog_ref[...])
        dsk = dsk_ref[...]

        def conv_tile(j, p):
            win_s[0:8, :] = jnp.where(c > 0, halo_ref[j], 0.0)
            win_s[8:8 + CHUNK, :] = u_ref[j]
            return _conv_silu(win_s[5:5 + CHUNK, :], win_s[6:6 + CHUNK, :], win_s[7:7 + CHUNK, :],
                              win_s[8:8 + CHUNK, :], cw_ref[p, 0:1, :], cw_ref[p, 1:2, :], cw_ref[p, 2:3, :],
                              cw_ref[p, 3:4, :], cb_ref[p])

        def group(g, s1):
            xs_s[...] = conv_tile(8 + g, g)
            bc_s[...] = conv_tile(16 + g, 8 + g)
            bm, cm = bc_s[:, 0:SSM_N], bc_s[:, SSM_N:2 * SSM_N]
            hs, xs, dtc, ac, ar, dh, hp = [], [], [], [], [], [], []
            for r in range(4):
                h, _, _, dtc_r, ac_r, ar_r, dh_r = _head_cols(g, r, dt, acs, acs_t, dsk)
                hs.append(h); dtc.append(dtc_r); ac.append(ac_r); ar.append(ar_r); dh.append(dh_r)
                xs.append(xs_s[:, SSM_P * r:SSM_P * (r + 1)])
                hp.append(h_s[h])
            ys, hns = _ssd_group(xs, bm, cm, dtc, ac, ar, dh, hp)
            for r in range(4):
                hst_ref[0, hs[r]] = hp[r]
                h_s[hs[r]] = hns[r]
                yp_ref[g, :, SSM_P * r:SSM_P * (r + 1)] = ys[r]
            y2 = yp_ref[g] * _silu(u_ref[g])
            return s1 + jnp.sum(y2 * y2, axis=1, keepdims=True)

        s1 = lax.fori_loop(0, SSM_GROUPS, group, jnp.zeros((CHUNK, 1), f32))
        rinv = lax.rsqrt(s1 * (1.0 / SSM_INNER) + EPS)

        def gate(g, carry):
            y2 = yp_ref[g] * _silu(u_ref[g])
            a3_ref[g] = (y2 * rinv * gn_ref[g]).astype(a3_ref.dtype)
            return carry

        lax.fori_loop(0, SSM_GROUPS, gate, 0)

    return _pc(
        body, name, grid=(nc,),
        in_specs=_ssm_in_specs(nc, False),
        out_specs=[pl.BlockSpec((SSM_KT, CHUNK, TILE), lambda c: (0, c, 0)),
                   pl.BlockSpec((SSM_KT, CHUNK, TILE), lambda c: (0, c, 0)),
                   pl.BlockSpec((1, SSM_HEADS, SSM_P, SSM_N), lambda c: (c, 0, 0, 0))],
        out_shape=[SDS((SSM_KT, t_len, TILE), MXU_DTYPE), SDS((SSM_KT, t_len, TILE), f32),
                   SDS((nc, SSM_HEADS, SSM_P, SSM_N), f32)],
        scratch_shapes=[pltpu.VMEM((SSM_HEADS, SSM_P, SSM_N), f32), pltpu.VMEM((8 + CHUNK, TILE), f32),
                        pltpu.VMEM((CHUNK, TILE), f32), pltpu.VMEM((CHUNK, TILE), f32)],
        compiler_params=_params(("arbitrary",)),
    )(u, u, cw, cb, dtb, alog, dsk, gn)


def _ssm_bwd(u, yp, hst, da3, cw, cb, dtb, alog, dsk, gn, name):
    t_len = u.shape[1]
    nc = t_len // CHUNK

    def body(u_ref, halo_ref, cw_ref, cb_ref, dtb_ref, alog_ref, dsk_ref, gn_ref, yp_ref, hst_ref, da3_ref,
             du_ref, dcw_ref, dcb_ref, ddtb_ref, dalog_ref, ddsk_ref, dgn_ref,
             dh_s, carry_s, win_s, dwin_s, xs_s, bc_s, dy_s, dxs_s, dbc_s):
        step = pl.program_id(0)
        cc = nc - 1 - step

        @pl.when(step == 0)
        def _():
            dh_s[...] = jnp.zeros_like(dh_s)
            carry_s[...] = jnp.zeros_like(carry_s)
            dcw_ref[...] = jnp.zeros_like(dcw_ref)
            dcb_ref[...] = jnp.zeros_like(dcb_ref)
            ddtb_ref[...] = jnp.zeros_like(ddtb_ref)
            dalog_ref[...] = jnp.zeros_like(dalog_ref)
            ddsk_ref[...] = jnp.zeros_like(ddsk_ref)
            dgn_ref[...] = jnp.zeros_like(dgn_ref)
        dtr, dt, a_neg, acs, acs_t = _dt_path(u_ref[SSM_NT - 1, :, 0:SSM_HEADS], dtb_ref[...], alog_ref[...])
        dsk = dsk_ref[...]

        def sums(g, carry):
            s1, s2 = carry
            y2 = yp_ref[g] * _silu(u_ref[g])
            g3 = da3_ref[g] * gn_ref[g]
            return (s1 + jnp.sum(y2 * y2, axis=1, keepdims=True), s2 + jnp.sum(g3 * y2, axis=1, keepdims=True))

        zcol = jnp.zeros((CHUNK, 1), f32)
        s1, s2 = lax.fori_loop(0, SSM_GROUPS, sums, (zcol, zcol))
        rinv = lax.rsqrt(s1 * (1.0 / SSM_INNER) + EPS)
        m2 = s2 * rinv * rinv * rinv * (1.0 / SSM_INNER)

        def windows(j):
            win_s[0:8, :] = jnp.where(cc > 0, halo_ref[j], 0.0)
            win_s[8:8 + CHUNK, :] = u_ref[j]
            return [win_s[5 + k:5 + k + CHUNK, :] for k in range(4)]

        def taps(p):
            return [cw_ref[p, k:k + 1, :] for k in range(4)]

        def conv_bwd(j, p, vjp_fn, dout):
            dx0, dx1, dx2, dx3, dw0, dw1, dw2, dw3, db = vjp_fn(dout)
            dwin_s[0:CHUNK, :] = jnp.zeros((CHUNK, TILE), f32)
            dwin_s[CHUNK:CHUNK + 8, :] = carry_s[j - 8]
            for k, dxk in enumerate((dx0, dx1, dx2, dx3)):
                dwin_s[5 + k:5 + k + CHUNK, :] += dxk
            du_ref[j] = dwin_s[8:8 + CHUNK, :]
            carry_s[j - 8] = dwin_s[0:8, :]
            for k, dwk in enumerate((dw0, dw1, dw2, dw3)):
                dcw_ref[p, k:k + 1, :] += dwk
            dcb_ref[p] += db

        def group(g, carry):
            ddt, dacs, dacs_t, ddsk = carry
            z = u_ref[g]
            sg = _sigmoid(z)
            sz = z * sg
            y = yp_ref[g]
            y2 = y * sz
            da3 = da3_ref[g]
            dgn_ref[g] += jnp.sum(da3 * y2 * rinv, axis=0, keepdims=True)
            dy2 = rinv * (da3 * gn_ref[g]) - y2 * m2
            dy_s[...] = dy2 * sz
            du_ref[g] = dy2 * y * (sg * (1.0 + z * (1.0 - sg)))
            xs_v, vjp_xs = jax.vjp(_conv_silu, *windows(8 + g), *taps(g), cb_ref[g])
            xs_s[...] = xs_v
            bc_v, vjp_bc = jax.vjp(_conv_silu, *windows(16 + g), *taps(8 + g), cb_ref[8 + g])
            bc_s[...] = bc_v
            bm, cm = bc_s[:, 0:SSM_N], bc_s[:, SSM_N:2 * SSM_N]
            hs, ohl, ohs, xs, dtc, ac, ar, dh, hp, dys, dhn = [], [], [], [], [], [], [], [], [], [], []
            for r in range(4):
                h, oh_l, oh_s, dtc_r, ac_r, ar_r, dh_r = _head_cols(g, r, dt, acs, acs_t, dsk)
                hs.append(h); ohl.append(oh_l); ohs.append(oh_s)
                dtc.append(dtc_r); ac.append(ac_r); ar.append(ar_r); dh.append(dh_r)
                xs.append(xs_s[:, SSM_P * r:SSM_P * (r + 1)])
                hp.append(hst_ref[0, h])
                dys.append(dy_s[:, SSM_P * r:SSM_P * (r + 1)])
                dhn.append(dh_s[h])
            _, vjp_g = jax.vjp(_ssd_group, xs, bm, cm, dtc, ac, ar, dh, hp)
            dxs, dbm, dcm, ddtc, dac, dar, ddh, dhp = vjp_g((tuple(dys), tuple(dhn)))
            for r in range(4):
                dxs_s[:, SSM_P * r:SSM_P * (r + 1)] = dxs[r]
                dh_s[hs[r]] = dhp[r]
                ddt = ddt + ddtc[r] * ohl[r]
                dacs = dacs + dac[r] * ohl[r]
                dacs_t = dacs_t + ohs[r] * dar[r]
                ddsk = ddsk + ddh[r] * ohl[r]
            dbc_s[:, 0:SSM_N] = dbm
            dbc_s[:, SSM_N:2 * SSM_N] = dcm
            conv_bwd(8 + g, g, vjp_xs, dxs_s[...])
            conv_bwd(16 + g, 8 + g, vjp_bc, dbc_s[...])
            return ddt, dacs, dacs_t, ddsk

        init = (jnp.zeros((CHUNK, SSM_HEADS), f32), jnp.zeros((CHUNK, SSM_HEADS), f32),
                jnp.zeros((SSM_HEADS, CHUNK), f32), jnp.zeros((1, SSM_HEADS), f32))
        ddt, dacs, dacs_t, ddsk = lax.fori_loop(0, SSM_GROUPS, group, init)
        upper = _tri(False)
        da = _dot_hi(upper, dacs, ((1,), (0,))) + _dot_hi(upper, dacs_t, ((1,), (1,)))
        ddt = ddt + da * a_neg
        dalog_ref[...] += jnp.sum(da * dt, axis=0, keepdims=True) * a_neg
        ddtr = ddt * _sigmoid(dtr)
        ddtb_ref[...] += jnp.sum(ddtr, axis=0, keepdims=True)
        ddsk_ref[...] += ddsk
        du_ref[SSM_NT - 1] = jnp.zeros((CHUNK, TILE), f32)
        du_ref[SSM_NT - 1, :, 0:SSM_HEADS] = ddtr

    def rc(c):
        return nc - 1 - c

    small = [pl.BlockSpec((16, 4, TILE), lambda c: (0, 0, 0)),
             pl.BlockSpec((16, 1, TILE), lambda c: (0, 0, 0)),
             pl.BlockSpec((1, SSM_HEADS), lambda c: (0, 0)),
             pl.BlockSpec((1, SSM_HEADS), lambda c: (0, 0)),
             pl.BlockSpec((1, SSM_HEADS), lambda c: (0, 0)),
             pl.BlockSpec((SSM_KT, 1, TILE), lambda c: (0, 0, 0))]
    return _pc(
        body, name, grid=(nc,),
        in_specs=_ssm_in_specs(nc, True) + [
            pl.BlockSpec((SSM_KT, CHUNK, TILE), lambda c: (0, rc(c), 0)),
            pl.BlockSpec((1, SSM_HEADS, SSM_P, SSM_N), lambda c: (rc(c), 0, 0, 0)),
            pl.BlockSpec((SSM_KT, CHUNK, TILE), lambda c: (0, rc(c), 0))],
        out_specs=[pl.BlockSpec((SSM_NT, CHUNK, TILE), lambda c: (0, rc(c), 0))] + small,
        out_shape=[SDS((SSM_NT, t_len, TILE), f32), SDS((16, 4, TILE), f32), SDS((16, 1, TILE), f32),
                   SDS((1, SSM_HEADS), f32), SDS((1, SSM_HEADS), f32), SDS((1, SSM_HEADS), f32),
                   SDS((SSM_KT, 1, TILE), f32)],
        scratch_shapes=[pltpu.VMEM((SSM_HEADS, SSM_P, SSM_N), f32), pltpu.VMEM((16, 8, TILE), f32),
                        pltpu.VMEM((8 + CHUNK, TILE), f32), pltpu.VMEM((8 + CHUNK, TILE), f32)]
                       + [pltpu.VMEM((CHUNK, TILE), f32)] * 5,
        compiler_params=_params(("arbitrary",)),
    )(u, u, cw, cb, dtb, alog, dsk, gn, yp, hst, da3)


def _swap16(t):
    lane = lax.broadcasted_iota(jnp.int32, t.shape, 1) % 64
    return jnp.where(lane < 8, pltpu.roll(t, TILE - 8, 1), jnp.where(lane < 16, pltpu.roll(t, 8, 1), 0.0))


def _rope(t, cos_t, sin_t):
    return t * cos_t + _swap16(t) * sin_t


def _rope_bwd(g, cos_t, sin_t):
    return g * cos_t + _swap16(g * sin_t)


def _att_head(qh, kp, kc, vp, vc, sink, mask_p, mask_c):
    sp = jnp.where(mask_p, _dot_nt(qh, kp) * 0.125, -jnp.inf)
    sc = jnp.where(mask_c, _dot_nt(qh, kc) * 0.125, -jnp.inf)
    m = jnp.maximum(jnp.maximum(jnp.max(sp, axis=-1, keepdims=True), jnp.max(sc, axis=-1, keepdims=True)), sink)
    m = lax.stop_gradient(m)
    pp = jnp.exp(sp - m)
    pc = jnp.exp(sc - m)
    den = jnp.sum(pp, axis=-1, keepdims=True) + jnp.sum(pc, axis=-1, keepdims=True) + jnp.exp(sink - m)
    inv = 1.0 / den
    return _dot(pp * inv, vp) + _dot(pc * inv, vc)


def _att_masks(has_prev):
    row = lax.broadcasted_iota(jnp.int32, (CHUNK, CHUNK), 0)
    col = lax.broadcasted_iota(jnp.int32, (CHUNK, CHUNK), 1)
    return (col > row) & has_prev, col <= row


def _att_in_specs(nb, rev):
    def bidx(n):
        return (nb - 1 - n) if rev else n
    return [
        pl.BlockSpec((ATT_NT, CHUNK, TILE), lambda n: (0, bidx(n), 0)),
        pl.BlockSpec((2, CHUNK, TILE), lambda n: (2, jnp.maximum(bidx(n) - 1, 0), 0)),
        pl.BlockSpec((CHUNK, TILE), lambda n: (bidx(n), 0)),
        pl.BlockSpec((CHUNK, TILE), lambda n: (bidx(n), 0)),
        pl.BlockSpec((CHUNK, TILE), lambda n: (jnp.maximum(bidx(n) - 1, 0), 0)),
        pl.BlockSpec((CHUNK, TILE), lambda n: (jnp.maximum(bidx(n) - 1, 0), 0)),
        pl.BlockSpec((1, 16), lambda n: (0, 0)),
    ]


def _att_fwd(u, cos_t, sin_t, sinks, name):
    t_len = u.shape[1]
    nb = t_len // CHUNK

    def body(u_ref, prev_ref, cc_ref, sc_ref, cp_ref, sp_ref, sink_ref, a_ref,
             q_s, kp_s, kc_s, vp_s, vc_s, o_s):
        n = pl.program_id(0)
        mask_p, mask_c = _att_masks(n > 0)
        cos_c, sin_c = cc_ref[...], sc_ref[...]
        kc_s[...] = _rope(u_ref[4], cos_c, sin_c)
        kp_s[...] = _rope(prev_ref[0], cp_ref[...], sp_ref[...])
        vc_s[...] = u_ref[5]
        vp_s[...] = prev_ref[1]
        sinks = sink_ref[...]
        for g in range(4):
            q_s[...] = _rope(u_ref[g], cos_c, sin_c)
            kv = slice(64 * g, 64 * (g + 1))
            for r in range(4):
                hd = slice(64 * r, 64 * (r + 1))
                h = 4 * g + r
                o_s[:, hd] = _att_head(q_s[:, hd], kp_s[:, kv], kc_s[:, kv], vp_s[:, kv], vc_s[:, kv],
                                       sinks[:, h:h + 1], mask_p, mask_c)
            a_ref[g] = (o_s[...] * _silu(u_ref[6 + g])).astype(a_ref.dtype)

    return _pc(
        body, name, grid=(nb,),
        in_specs=_att_in_specs(nb, False),
        out_specs=pl.BlockSpec((ATT_KT, CHUNK, TILE), lambda n: (0, n, 0)),
        out_shape=SDS((ATT_KT, t_len, TILE), MXU_DTYPE),
        scratch_shapes=[pltpu.VMEM((CHUNK, TILE), f32)] * 6,
        compiler_params=_params(("arbitrary",)),
    )(u, u, cos_t, sin_t, cos_t, sin_t, sinks)


def _att_bwd(u, cos_t, sin_t, sinks, da, name):
    t_len = u.shape[1]
    nb = t_len // CHUNK

    def body(u_ref, prev_ref, cc_ref, sc_ref, cp_ref, sp_ref, sink_ref, da_ref, du_ref, dsink_ref,
             ck_s, cv_s, q_s, kp_s, kc_s, vp_s, vc_s, o_s, do_s, dq_s, dkp_s, dkc_s, dvp_s, dvc_s):
        step = pl.program_id(0)
        nn = nb - 1 - step

        @pl.when(step == 0)
        def _():
            ck_s[...] = jnp.zeros_like(ck_s)
            cv_s[...] = jnp.zeros_like(cv_s)
            dsink_ref[...] = jnp.zeros_like(dsink_ref)
        mask_p, mask_c = _att_masks(nn > 0)
        cos_c, sin_c = cc_ref[...], sc_ref[...]
        cos_p, sin_p = cp_ref[...], sp_ref[...]
        kc_s[...] = _rope(u_ref[4], cos_c, sin_c)
        kp_s[...] = _rope(prev_ref[0], cos_p, sin_p)
        vc_s[...] = u_ref[5]
        vp_s[...] = prev_ref[1]
        sinks = sink_ref[...]
        lane16 = lax.broadcasted_iota(jnp.int32, (1, 16), 1)
        dsink = jnp.zeros((1, 16), f32)
        att = functools.partial(_att_head, mask_p=mask_p, mask_c=mask_c)
        for g in range(4):
            q_s[...] = _rope(u_ref[g], cos_c, sin_c)
            gate = u_ref[6 + g]
            sg = _sigmoid(gate)
            dav = da_ref[g]
            do_s[...] = dav * (gate * sg)
            kv = slice(64 * g, 64 * (g + 1))
            dkp = jnp.zeros((CHUNK, 64), f32)
            dkc = jnp.zeros((CHUNK, 64), f32)
            dvp = jnp.zeros((CHUNK, 64), f32)
            dvc = jnp.zeros((CHUNK, 64), f32)
            for r in range(4):
                hd = slice(64 * r, 64 * (r + 1))
                h = 4 * g + r
                o_h, vjp_fn = jax.vjp(att, q_s[:, hd], kp_s[:, kv], kc_s[:, kv], vp_s[:, kv], vc_s[:, kv],
                                      sinks[:, h:h + 1])
                dq_h, dkp_h, dkc_h, dvp_h, dvc_h, ds_h = vjp_fn(do_s[:, hd])
                o_s[:, hd] = o_h
                dq_s[:, hd] = dq_h
                dkp, dkc, dvp, dvc = dkp + dkp_h, dkc + dkc_h, dvp + dvp_h, dvc + dvc_h
                dsink = dsink + ds_h * (lane16 == h).astype(f32)
            du_ref[6 + g] = dav * o_s[...] * (sg * (1.0 + gate * (1.0 - sg)))
            du_ref[g] = _rope_bwd(dq_s[...], cos_c, sin_c)
            dkp_s[:, kv] = dkp
            dkc_s[:, kv] = dkc
            dvp_s[:, kv] = dvp
            dvc_s[:, kv] = dvc
        du_ref[4] = _rope_bwd(dkc_s[...], cos_c, sin_c) + ck_s[...]
        du_ref[5] = dvc_s[...] + cv_s[...]
        ck_s[...] = _rope_bwd(dkp_s[...], cos_p, sin_p)
        cv_s[...] = dvp_s[...]
        dsink_ref[...] += dsink

    def rb(n):
        return nb - 1 - n

    return _pc(
        body, name, grid=(nb,),
        in_specs=_att_in_specs(nb, True) + [pl.BlockSpec((ATT_KT, CHUNK, TILE), lambda n: (0, rb(n), 0))],
        out_specs=[pl.BlockSpec((ATT_NT, CHUNK, TILE), lambda n: (0, rb(n), 0)),
                   pl.BlockSpec((1, 16), lambda n: (0, 0))],
        out_shape=[SDS((ATT_NT, t_len, TILE), f32), SDS((1, 16), f32)],
        scratch_shapes=[pltpu.VMEM((CHUNK, TILE), f32)] * 14,
        compiler_params=_params(("arbitrary",)),
    )(u, u, cos_t, sin_t, cos_t, sin_t, sinks, da)


_HBM = pl.BlockSpec(memory_space=pltpu.HBM)


def _all_gather_big(shard):
    rows, width = shard.shape

    def body(x_ref, out_ref, send_sems, recv_sems, local_sem):
        x, y, c = lax.axis_index("x"), lax.axis_index("y"), lax.axis_index("c")
        me, sibling = (x, y, c), (x, y, 1 - c)
        chips = [(1 - x, y), (x, 1 - y), (1 - x, 1 - y)]

        def slot(px, py, pc):
            return out_ref.at[4 * px + 2 * py + pc]

        def copy(k, block, to, src=None):
            return pltpu.make_async_remote_copy(
                src_ref=slot(*block) if src is None else src, dst_ref=slot(*block),
                send_sem=send_sems.at[k], recv_sem=recv_sems.at[k], device_id=to, device_id_type=MESH)

        mine = pltpu.make_async_copy(x_ref, slot(*me), local_sem)
        mine.start()
        first = [copy(0, me, sibling, src=x_ref)]
        first += [copy(1 + j, me, (*chip, c), src=x_ref) for j, chip in enumerate(chips)]
        for cp in first:
            cp.start()
        passed = [copy(4 + j, (*chip, c), sibling) for j, chip in enumerate(chips)]
        for j, chip in enumerate(chips):
            copy(1 + j, (*chip, c), me).wait_recv()
            passed[j].start()
        copy(0, sibling, me).wait_recv()
        for j, chip in enumerate(chips):
            copy(4 + j, (*chip, 1 - c), me).wait_recv()
        for cp in first + passed:
            cp.wait_send()
        mine.wait()

    return _pc(
        body, "all_gather_big",
        in_specs=[_HBM], out_specs=_HBM,
        out_shape=SDS((N_DEV, rows, width), shard.dtype),
        scratch_shapes=[pltpu.SemaphoreType.DMA((7,)), pltpu.SemaphoreType.DMA((7,)), pltpu.SemaphoreType.DMA],
    )(shard)


def _all_gather_direct(block, name):
    rows, width = block.shape

    def body(x_ref, out_ref, send_sems, recv_sems, local_sem):
        x, y, c = lax.axis_index("x"), lax.axis_index("y"), lax.axis_index("c")
        my_slot = 4 * x + 2 * y + c

        def peer(k):
            return (1 - x if k & 4 else x, 1 - y if k & 2 else y, 1 - c if k & 1 else c)

        def copy(k):
            px, py, pc = peer(k)
            return pltpu.make_async_remote_copy(
                src_ref=x_ref, dst_ref=out_ref.at[my_slot], send_sem=send_sems.at[k - 1], recv_sem=recv_sems.at[k - 1],
                device_id=(px, py, pc), device_id_type=MESH)

        def arrival(k):
            px, py, pc = peer(k)
            return pltpu.make_async_remote_copy(
                src_ref=x_ref, dst_ref=out_ref.at[4 * px + 2 * py + pc], send_sem=send_sems.at[k - 1],
                recv_sem=recv_sems.at[k - 1], device_id=(px, py, pc), device_id_type=MESH)

        mine = pltpu.make_async_copy(x_ref, out_ref.at[my_slot], local_sem)
        mine.start()
        for k in range(1, N_DEV):
            copy(k).start()
        for k in range(1, N_DEV):
            arrival(k).wait_recv()
        for k in range(1, N_DEV):
            copy(k).wait_send()
        mine.wait()

    return _pc(
        body, name,
        in_specs=[_HBM], out_specs=_HBM,
        out_shape=SDS((N_DEV, rows, width), block.dtype),
        scratch_shapes=[pltpu.SemaphoreType.DMA((7,)), pltpu.SemaphoreType.DMA((7,)), pltpu.SemaphoreType.DMA],
    )(block)


def _exchange_sibling(g2):
    _, nchip, rows, width = g2.shape

    def body(g_ref, out_ref, send_sem, recv_sem):
        x, y, c = lax.axis_index("x"), lax.axis_index("y"), lax.axis_index("c")
        cp = pltpu.make_async_remote_copy(
            src_ref=g_ref.at[1 - c], dst_ref=out_ref, send_sem=send_sem, recv_sem=recv_sem,
            device_id=(x, y, 1 - c), device_id_type=MESH)
        cp.start()
        cp.wait()

    return _pc(
        body, "rs_sibling",
        in_specs=[_HBM], out_specs=_HBM,
        out_shape=SDS((nchip, rows, width), g2.dtype),
        scratch_shapes=[pltpu.SemaphoreType.DMA, pltpu.SemaphoreType.DMA],
    )(g2)


def _pair_sum(g2, r1, cidx, tr):
    _, nchip, rows, width = g2.shape

    def body(c_ref, g_ref, r_ref, o_ref):
        o_ref[...] = g_ref[0] + r_ref[...]

    return pl.pallas_call(
        body, name="rs_pair_sum",
        grid_spec=pltpu.PrefetchScalarGridSpec(
            num_scalar_prefetch=1, grid=(nchip, rows // tr),
            in_specs=[pl.BlockSpec((1, 1, tr, width), lambda k, i, c_ref: (c_ref[0], k, i, 0)),
                      pl.BlockSpec((1, tr, width), lambda k, i, c_ref: (k, i, 0))],
            out_specs=pl.BlockSpec((1, tr, width), lambda k, i, c_ref: (k, i, 0))),
        out_shape=SDS((nchip, rows, width), g2.dtype),
        compiler_params=_params(("arbitrary", "arbitrary")),
    )(cidx, g2, r1)


def _exchange_chips(p):
    nchip, rows, width = p.shape

    def body(p_ref, out_ref, send_sems, recv_sems, local_sem):
        x, y, c = lax.axis_index("x"), lax.axis_index("y"), lax.axis_index("c")
        my_chip = 2 * x + y
        chips = [(1 - x, y), (x, 1 - y), (1 - x, 1 - y)]

        def copy(j):
            px, py = chips[j]
            return pltpu.make_async_remote_copy(
                src_ref=p_ref.at[2 * px + py], dst_ref=out_ref.at[my_chip], send_sem=send_sems.at[j],
                recv_sem=recv_sems.at[j], device_id=(px, py, c), device_id_type=MESH)

        def arrival(j):
            px, py = chips[j]
            return pltpu.make_async_remote_copy(
                src_ref=p_ref.at[my_chip], dst_ref=out_ref.at[2 * px + py], send_sem=send_sems.at[j],
                recv_sem=recv_sems.at[j], device_id=(px, py, c), device_id_type=MESH)

        mine = pltpu.make_async_copy(p_ref.at[my_chip], out_ref.at[my_chip], local_sem)
        mine.start()
        for j in range(3):
            copy(j).start()
        for j in range(3):
            arrival(j).wait_recv()
        for j in range(3):
            copy(j).wait_send()
        mine.wait()

    return _pc(
        body, "rs_chips",
        in_specs=[_HBM], out_specs=_HBM,
        out_shape=SDS((nchip, rows, width), p.dtype),
        scratch_shapes=[pltpu.SemaphoreType.DMA((3,)), pltpu.SemaphoreType.DMA((3,)), pltpu.SemaphoreType.DMA],
    )(p)


def _adamw(parts, w, m, v, tr, name):
    n, rows, width = parts.shape
    c1 = 1.0 / (1.0 - ADAM_B1 ** ADAM_STEP)
    c2 = 1.0 / (1.0 - ADAM_B2 ** ADAM_STEP)

    def body(p_ref, w_ref, m_ref, v_ref, g_ref, d_ref, mo_ref, vo_ref):
        g = p_ref[0]
        for k in range(1, n):
            g = g + p_ref[k]
        mn = ADAM_B1 * m_ref[...] + (1.0 - ADAM_B1) * g
        vn = ADAM_B2 * v_ref[...] + (1.0 - ADAM_B2) * (g * g)
        g_ref[...] = g
        mo_ref[...] = mn
        vo_ref[...] = vn
        d_ref[...] = -ADAM_LR * ((mn * c1) / (jnp.sqrt(vn * c2) + ADAM_EPS) + ADAM_WD * w_ref[...])

    blk = pl.BlockSpec((tr, width), lambda i: (i, 0))
    return _pc(
        body, name, grid=(rows // tr,),
        in_specs=[pl.BlockSpec((n, tr, width), lambda i: (0, i, 0)), blk, blk, blk],
        out_specs=[blk, blk, blk, blk],
        out_shape=[SDS((rows, width), f32)] * 4,
        compiler_params=_params(("arbitrary",)),
    )(parts, w, m, v)


def _pack_big(ssm_w_in, ssm_w_out, att_w_in, att_w_out, conv_w=None):
    parts = [ssm_w_in.reshape(ROWS_SSM_IN, 1024), ssm_w_out.reshape(ROWS_SSM_OUT, 1024),
             att_w_in.reshape(ROWS_ATT_IN, 1024), att_w_out.reshape(ROWS_ATT_OUT, 1024)]
    if conv_w is None:
        parts.append(jnp.zeros((ROWS_CONV, 1024), f32))
    else:
        parts.append(jnp.concatenate([conv_w.reshape(4, 1024), jnp.zeros((4, 1024), f32)], axis=0))
    return jnp.concatenate(parts, axis=0)


def _unpack_big(p):
    o = 0
    out = []
    for rows, shape in ((ROWS_SSM_IN, (2, 1024, 772)), (ROWS_SSM_OUT, (2, 256, 1024)),
                        (ROWS_ATT_IN, (2, 1024, 320)), (ROWS_ATT_OUT, (2, 128, 1024))):
        out.append(p[o:o + rows].reshape(shape))
        o += rows
    out.append(p[o:o + 4].reshape(2, 4, 512))
    return out


def _by_class(a):
    return jnp.swapaxes(a.reshape((4, 2) + a.shape[1:]), 0, 1)


def _pack_grads(d_ssm_w_in, d_ssm_w_out, d_att_w_in, d_att_w_out, d_conv_w):
    a = jnp.transpose(d_ssm_w_in.reshape(2, 1024, 8, 772), (2, 0, 1, 3)).reshape(8, ROWS_SSM_IN, 1024)
    b = jnp.transpose(d_ssm_w_out.reshape(2, 8, 256, 1024), (1, 0, 2, 3)).reshape(8, ROWS_SSM_OUT, 1024)
    c = jnp.transpose(d_att_w_in.reshape(2, 1024, 8, 320), (2, 0, 1, 3)).reshape(8, ROWS_ATT_IN, 1024)
    d = jnp.transpose(d_att_w_out.reshape(2, 8, 128, 1024), (1, 0, 2, 3)).reshape(8, ROWS_ATT_OUT, 1024)
    e = jnp.transpose(d_conv_w.reshape(2, 4, 8, 512), (2, 0, 1, 3)).reshape(8, 4, 1024)
    e = jnp.concatenate([e, jnp.zeros((8, 4, 1024), f32)], axis=1)
    return _by_class(jnp.concatenate([a, b, c, d, e], axis=1))


def _pad8(a):
    return jnp.pad(a, ((0, 8 - a.shape[0]), (0, 0)))


def _pack_small(pre_norm, post_norm, conv_b, gate_norm, dt_bias, a_log, d_skip, sinks, extra=None):
    row = jnp.concatenate([dt_bias.reshape(1, 64), a_log.reshape(1, 64), d_skip.reshape(1, 64), sinks.reshape(1, 32),
                           jnp.zeros((1, 1024 - 224), f32)], axis=1)
    if extra is not None:
        row = row + jnp.pad(extra.reshape(1, 1), ((0, 0), (224, 1024 - 225)))
    return jnp.concatenate([_pad8(pre_norm.reshape(4, 1024)), _pad8(post_norm.reshape(4, 1024)),
                            conv_b.reshape(8, 1024), _pad8(gate_norm.reshape(4, 1024)), _pad8(row)], axis=0)


def _unpack_small(p):
    row = p[32]
    return (p[0:4], p[8:12], p[16:24].reshape(2, 4096), row[0:64].reshape(2, 32), row[64:128].reshape(2, 32),
            row[128:192].reshape(2, 32), p[24:28].reshape(2, 2048), row[192:224].reshape(2, 16))


def _ssm_w_in_tiles(w):
    wb = w[:, 4096:5120].reshape(1024, 8, 128)
    wc = w[:, 5120:6144].reshape(1024, 8, 128)
    wbc = jnp.concatenate([wb, wc], axis=2).reshape(1024, 2048)
    return jnp.concatenate([w[:, 0:4096], wbc, w[:, 6144:6176], jnp.zeros((1024, 224), w.dtype)], axis=1)


def _ssm_w_in_untile(dw):
    dbc = dw[:, 4096:6144].reshape(1024, 8, 256)
    return jnp.concatenate([dw[:, 0:4096], dbc[:, :, 0:128].reshape(1024, 1024), dbc[:, :, 128:256].reshape(1024, 1024),
                            dw[:, 6144:6176]], axis=1)


def _conv_tiles(cw):
    k = cw.shape[0]
    xs = jnp.transpose(cw[:, 0:2048].reshape(k, 8, 256), (1, 0, 2))
    b = cw[:, 2048:3072].reshape(k, 8, 128)
    c = cw[:, 3072:4096].reshape(k, 8, 128)
    bc = jnp.transpose(jnp.concatenate([b, c], axis=2), (1, 0, 2))
    return jnp.concatenate([xs, bc], axis=0)


def _conv_untile(t):
    k = t.shape[1]
    xs = jnp.transpose(t[0:8], (1, 0, 2)).reshape(k, 2048)
    bc = jnp.transpose(t[8:16], (1, 0, 2))
    return jnp.concatenate([xs, bc[:, :, 0:128].reshape(k, 1024), bc[:, :, 128:256].reshape(k, 1024)], axis=1)


def _rope_tables(positions):
    inv = ROPE_THETA ** (-jnp.arange(0, 16, 2, dtype=f32) / 16)
    ang = positions.astype(f32).reshape(-1, 1) * inv
    cos, sin = jnp.cos(ang), jnp.sin(ang)
    t_len = ang.shape[0]
    cos64 = jnp.concatenate([cos, cos, jnp.ones((t_len, 48), f32)], axis=1)
    sin64 = jnp.concatenate([-sin, sin, jnp.zeros((t_len, 48), f32)], axis=1)
    return jnp.tile(cos64, (1, 4)), jnp.tile(sin64, (1, 4))


def kernel(x, positions, pre_norm, post_norm, ssm_w_in, ssm_conv_w, ssm_conv_b, ssm_dt_bias, ssm_a_log, ssm_d, ssm_gate_norm, ssm_w_out, att_w_in, att_sinks, att_w_out, loss_target, m_pre_norm, m_post_norm, m_ssm_w_in, m_ssm_conv_w, m_ssm_conv_b, m_ssm_dt_bias, m_ssm_a_log, m_ssm_d, m_ssm_gate_norm, m_ssm_w_out, m_att_w_in, m_att_sinks, m_att_w_out, v_pre_norm, v_post_norm, v_ssm_w_in, v_ssm_conv_w, v_ssm_conv_b, v_ssm_dt_bias, v_ssm_a_log, v_ssm_d, v_ssm_gate_norm, v_ssm_w_out, v_att_w_in, v_att_sinks, v_att_w_out):
    t_len = x.shape[1]
    tm = min(512, t_len)
    xin = x.reshape(t_len, D_MODEL)
    tgt = loss_target.reshape(t_len, D_MODEL)
    cidx = lax.axis_index("c").astype(jnp.int32).reshape(1)

    w_local = _pack_big(ssm_w_in, ssm_w_out, att_w_in, att_w_out)
    gathered = _all_gather_big(w_local.astype(MXU_DTYPE))
    conv_local = jnp.concatenate([ssm_conv_w.reshape(4, 1024), jnp.zeros((4, 1024), f32)], axis=0)
    conv_all = _all_gather_direct(conv_local, "all_gather_conv")[:, 0:4]
    o = 0
    g_ssm_in = gathered[:, o:o + ROWS_SSM_IN].reshape(8, 2, 1024, 772); o += ROWS_SSM_IN
    g_ssm_out = gathered[:, o:o + ROWS_SSM_OUT].reshape(8, 2, 256, 1024); o += ROWS_SSM_OUT
    g_att_in = gathered[:, o:o + ROWS_ATT_IN].reshape(8, 2, 1024, 320); o += ROWS_ATT_IN
    g_att_out = gathered[:, o:o + ROWS_ATT_OUT].reshape(8, 2, 128, 1024)
    w_ssm_in = jnp.transpose(g_ssm_in, (1, 2, 0, 3)).reshape(2, 1024, SSM_IN)
    w_ssm_out = jnp.transpose(g_ssm_out, (1, 0, 2, 3)).reshape(2, SSM_INNER, 1024)
    w_att_in = jnp.transpose(g_att_in, (1, 2, 0, 3)).reshape(2, 1024, ATT_IN)
    w_att_out = jnp.transpose(g_att_out, (1, 0, 2, 3)).reshape(2, 1024, 1024)
    conv_w = jnp.transpose(conv_all.reshape(8, 2, 4, 512), (1, 2, 0, 3)).reshape(2, 4, 4096)
    cos_t, sin_t = _rope_tables(positions)

    saved = []
    xc = xin
    for i in range(4):
        j = i // 2
        wn_pre, wn_post = pre_norm[i].reshape(1, D_MODEL), post_norm[i].reshape(1, D_MODEL)
        if i % 2 == 0:
            w_in = _ssm_w_in_tiles(w_ssm_in[j])
            cw, cb = _conv_tiles(conv_w[j]), _conv_tiles(ssm_conv_b[j].reshape(1, 4096))
            dtb, alog, dsk = ssm_dt_bias[j].reshape(1, 32), ssm_a_log[j].reshape(1, 32), ssm_d[j].reshape(1, 32)
            gn = ssm_gate_norm[j].reshape(SSM_KT, 1, TILE)
            u, h = _mm_in(xc, wn_pre, w_in, 5, tm, f"ssm_in_{j}")
            a3, yp, hst = _ssm_fwd(u, cw, cb, dtb, alog, dsk, gn, f"ssm_core_{j}")
            y, xn = _mm_out(a3, w_ssm_out[j], xc, wn_post, 4, tm, f"ssm_out_{j}")
            saved.append(dict(x=xc, u=u, h=h, a=a3, yp=yp, hst=hst, y=y, w_in=w_in, cw=cw, cb=cb, dtb=dtb, alog=alog,
                              dsk=dsk, gn=gn))
        else:
            sinks = att_sinks[j].reshape(1, 16)
            u, h = _mm_in(xc, wn_pre, w_att_in[j], 5, tm, f"att_in_{j}")
            a = _att_fwd(u, cos_t, sin_t, sinks, f"att_core_{j}")
            y, xn = _mm_out(a, w_att_out[j], xc, wn_post, 4, tm, f"att_out_{j}")
            saved.append(dict(x=xc, u=u, h=h, a=a, y=y, sinks=sinks))
        xc = xn

    dx, loss_part = _loss_grad(xc, tgt, tm)

    d_pre, d_post = [None] * 4, [None] * 4
    d_ssm_in, d_ssm_out, d_att_in, d_att_out = [None] * 2, [None] * 2, [None] * 2, [None] * 2
    d_cw, d_cb, d_dtb, d_alog, d_dsk, d_gn, d_sinks = ([None] * 2 for _ in range(7))
    for i in reversed(range(4)):
        j = i // 2
        s = saved[i]
        wn_pre, wn_post = pre_norm[i].reshape(1, D_MODEL), post_norm[i].reshape(1, D_MODEL)
        if i % 2 == 0:
            da3, dy, d_post[i] = _mm_dout(s["y"], dx, wn_post, w_ssm_out[j].T, 4, tm, f"ssm_dout_{j}")
            d_ssm_out[j] = _dw_rows(s["a"], dy, 4, tm, f"ssm_dwout_{j}")
            du, dcw, dcb, d_dtb[j], d_alog[j], d_dsk[j], dgn = _ssm_bwd(
                s["u"], s["yp"], s["hst"], da3, s["cw"], s["cb"], s["dtb"], s["alog"], s["dsk"], s["gn"],
                f"ssm_core_bwd_{j}")
            d_cw[j], d_cb[j], d_gn[j] = _conv_untile(dcw), _conv_untile(dcb), dgn.reshape(1, SSM_INNER)
            d_ssm_in[j] = _ssm_w_in_untile(_dw_cols(s["h"], du, 5, tm, f"ssm_dwin_{j}"))
            dx, d_pre[i] = _mm_dh(du, s["w_in"].T, s["x"], dx, wn_pre, 5, tm, f"ssm_dh_{j}")
        else:
            da, dy, d_post[i] = _mm_dout(s["y"], dx, wn_post, w_att_out[j].T, 4, tm, f"att_dout_{j}")
            d_att_out[j] = _dw_rows(s["a"], dy, 4, tm, f"att_dwout_{j}")
            du, d_sinks[j] = _att_bwd(s["u"], cos_t, sin_t, s["sinks"], da, f"att_core_bwd_{j}")
            d_att_in[j] = _dw_cols(s["h"], du, 5, tm, f"att_dwin_{j}")
            dx, d_pre[i] = _mm_dh(du, w_att_in[j].T, s["x"], dx, wn_pre, 5, tm, f"att_dh_{j}")

    g2 = _pack_grads(jnp.stack(d_ssm_in), jnp.stack(d_ssm_out), jnp.stack(d_att_in), jnp.stack(d_att_out),
                     jnp.stack(d_cw))
    r1 = _exchange_sibling(g2)
    pair = _pair_sum(g2, r1, cidx, ROWS_BIG // 10)
    parts = _exchange_chips(pair)
    w_p = _pack_big(ssm_w_in, ssm_w_out, att_w_in, att_w_out, ssm_conv_w)
    m_p = _pack_big(m_ssm_w_in, m_ssm_w_out, m_att_w_in, m_att_w_out, m_ssm_conv_w)
    v_p = _pack_big(v_ssm_w_in, v_ssm_w_out, v_att_w_in, v_att_w_out, v_ssm_conv_w)
    big = [_unpack_big(t) for t in _adamw(parts, w_p, m_p, v_p, ROWS_BIG // 10, "adamw_big")]

    small_local = _pack_small(jnp.concatenate(d_pre, axis=0), jnp.concatenate(d_post, axis=0),
                              jnp.concatenate(d_cb, axis=0), jnp.concatenate(d_gn, axis=0),
                              jnp.concatenate(d_dtb, axis=0), jnp.concatenate(d_alog, axis=0),
                              jnp.concatenate(d_dsk, axis=0), jnp.concatenate(d_sinks, axis=0), loss_part[0, 0])
    small_all = _all_gather_direct(small_local, "all_gather_small")
    ws = _pack_small(pre_norm, post_norm, ssm_conv_b, ssm_gate_norm, ssm_dt_bias, ssm_a_log, ssm_d, att_sinks)
    ms = _pack_small(m_pre_norm, m_post_norm, m_ssm_conv_b, m_ssm_gate_norm, m_ssm_dt_bias, m_ssm_a_log, m_ssm_d,
                     m_att_sinks)
    vs = _pack_small(v_pre_norm, v_post_norm, v_ssm_conv_b, v_ssm_gate_norm, v_ssm_dt_bias, v_ssm_a_log, v_ssm_d,
                     v_att_sinks)
    small4 = _adamw(small_all, ws, ms, vs, ROWS_SMALL, "adamw_small")
    loss = small4[0][32, 224]
    small = [_unpack_small(t) for t in small4]

    outs = [loss, dx.reshape(1, t_len, D_MODEL)]
    for k in range(4):
        b_ssm_in, b_ssm_out, b_att_in, b_att_out, b_conv = big[k]
        s_pre, s_post, s_cb, s_dtb, s_alog, s_d, s_gn, s_sinks = small[k]
        outs += [s_pre, s_post, b_ssm_in, b_conv, s_cb, s_dtb, s_alog, s_d, s_gn, b_ssm_out, b_att_in, s_sinks,
                 b_att_out]
    return tuple(outs)
```

```python
import functools

import jax
import jax.numpy as jnp
from jax import lax
from jax.experimental import pallas as pl
from jax.experimental.pallas import tpu as pltpu

f32 = jnp.float32
MXU_DTYPE = jnp.bfloat16
SDS = jax.ShapeDtypeStruct
MESH = pl.DeviceIdType.MESH

D_MODEL = 1024
EPS = 1e-6
TILE = 256
CHUNK = 128
SSM_HEADS = 32
SSM_GROUPS = 8
SSM_P = 64
SSM_N = 128
SSM_INNER = 2048
SSM_IN = 6176
SSM_NT = 25
SSM_KT = 8
ATT_NT = 10
ATT_KT = 4
ATT_IN = 2560
ROPE_THETA = 500000.0
N_DEV = 8
VMEM_LIMIT = 56 * 1024 * 1024

ADAM_LR = 0.001
ADAM_B1 = 0.9
ADAM_B2 = 0.999
ADAM_EPS = 1e-08
ADAM_WD = 0.01
ADAM_STEP = 10

ROWS_SSM_IN = 2 * 1024 * 772 // 1024
ROWS_SSM_OUT = 2 * 256
ROWS_ATT_IN = 2 * 1024 * 320 // 1024
ROWS_ATT_OUT = 2 * 128
ROWS_CONV = 8
ROWS_BIG = ROWS_SSM_IN + ROWS_SSM_OUT + ROWS_ATT_IN + ROWS_ATT_OUT + ROWS_CONV
ROWS_SMALL = 40


def _pc(body, name, **kw):
    return pl.pallas_call(body, name=name, **kw)


def _params(sem):
    return pltpu.CompilerParams(dimension_semantics=sem, vmem_limit_bytes=VMEM_LIMIT)


def _sigmoid(x):
    return 1.0 / (1.0 + jnp.exp(-x))


def _silu(x):
    return x * _sigmoid(x)


def _softplus(x):
    return jnp.maximum(x, 0.0) + jnp.log(1.0 + jnp.exp(-jnp.abs(x)))


def _mx(x):
    return x.astype(MXU_DTYPE)


def _dot(a, b):
    return jnp.dot(_mx(a), _mx(b), preferred_element_type=f32)


def _dot_nt(a, b):
    return lax.dot_general(_mx(a), _mx(b), (((1,), (1,)), ((), ())), preferred_element_type=f32)


def _dot_tn(a, b):
    return lax.dot_general(_mx(a), _mx(b), (((0,), (0,)), ((), ())), preferred_element_type=f32)


def _rms_fwd(x, w):
    r = lax.rsqrt(jnp.mean(x * x, axis=-1, keepdims=True) + EPS)
    return x * r * w


def _rms_bwd(x, w, dy):
    r = lax.rsqrt(jnp.mean(x * x, axis=-1, keepdims=True) + EPS)
    xh = x * r
    dw = jnp.sum(dy * xh, axis=0, keepdims=True)
    g = dy * w
    dx = r * (g - xh * jnp.mean(g * xh, axis=-1, keepdims=True))
    return dx, dw


def _mm_in(x, wn, w, ntb, tm, name):
    t_len, d = x.shape
    nt = w.shape[1] // TILE

    def body(x_ref, wn_ref, w_ref, u_ref, h_ref):
        @pl.when(pl.program_id(1) == 0)
        def _():
            h_ref[...] = _rms_fwd(x_ref[...], wn_ref[...]).astype(h_ref.dtype)
        h = h_ref[...]
        for t in range(ntb):
            u_ref[t] = jnp.dot(h, w_ref[:, TILE * t:TILE * (t + 1)], preferred_element_type=f32)

    return _pc(
        body, name, grid=(t_len // tm, nt // ntb),
        in_specs=[pl.BlockSpec((tm, d), lambda i, j: (i, 0)),
                  pl.BlockSpec((1, d), lambda i, j: (0, 0)),
                  pl.BlockSpec((d, ntb * TILE), lambda i, j: (0, j))],
        out_specs=[pl.BlockSpec((ntb, tm, TILE), lambda i, j: (j, i, 0)),
                   pl.BlockSpec((tm, d), lambda i, j: (i, 0))],
        out_shape=[SDS((nt, t_len, TILE), f32), SDS((t_len, d), MXU_DTYPE)],
        compiler_params=_params(("arbitrary", "arbitrary")),
    )(x, wn, w)


def _mm_dout(y, dxn, wn, wt, ntb, tm, name):
    t_len, d = y.shape
    nt = wt.shape[1] // TILE

    def body(y_ref, dxn_ref, wn_ref, w_ref, da_ref, dy_ref, dwn_ref):
        i, j = pl.program_id(0), pl.program_id(1)

        @pl.when((i == 0) & (j == 0))
        def _():
            dwn_ref[...] = jnp.zeros_like(dwn_ref)

        @pl.when(j == 0)
        def _():
            dy, dw = _rms_bwd(y_ref[...], wn_ref[...], dxn_ref[...])
            dy_ref[...] = dy.astype(dy_ref.dtype)
            dwn_ref[...] += dw
        dy = dy_ref[...]
        for t in range(ntb):
            da_ref[t] = jnp.dot(dy, w_ref[:, TILE * t:TILE * (t + 1)], preferred_element_type=f32)

    return _pc(
        body, name, grid=(t_len // tm, nt // ntb),
        in_specs=[pl.BlockSpec((tm, d), lambda i, j: (i, 0)),
                  pl.BlockSpec((tm, d), lambda i, j: (i, 0)),
                  pl.BlockSpec((1, d), lambda i, j: (0, 0)),
                  pl.BlockSpec((d, ntb * TILE), lambda i, j: (0, j))],
        out_specs=[pl.BlockSpec((ntb, tm, TILE), lambda i, j: (j, i, 0)),
                   pl.BlockSpec((tm, d), lambda i, j: (i, 0)),
                   pl.BlockSpec((1, d), lambda i, j: (0, 0))],
        out_shape=[SDS((nt, t_len, TILE), f32), SDS((t_len, d), MXU_DTYPE), SDS((1, d), f32)],
        compiler_params=_params(("arbitrary", "arbitrary")),
    )(y, dxn, wn, wt)


def _mm_out(a, w, x, wn, ktb, tm, name):
    kt, t_len, _ = a.shape
    d = w.shape[1]
    nk = kt // ktb

    def body(a_ref, w_ref, x_ref, wn_ref, y_ref, xn_ref, acc):
        k = pl.program_id(1)

        @pl.when(k == 0)
        def _():
            acc[...] = jnp.zeros_like(acc)
        s = acc[...]
        for t in range(ktb):
            s = s + jnp.dot(a_ref[t], w_ref[TILE * t:TILE * (t + 1), :], preferred_element_type=f32)
        acc[...] = s

        @pl.when(k == nk - 1)
        def _():
            y = acc[...]
            y_ref[...] = y
            xn_ref[...] = x_ref[...] + _rms_fwd(y, wn_ref[...])

    return _pc(
        body, name, grid=(t_len // tm, nk),
        in_specs=[pl.BlockSpec((ktb, tm, TILE), lambda i, k: (k, i, 0)),
                  pl.BlockSpec((ktb * TILE, d), lambda i, k: (k, 0)),
                  pl.BlockSpec((tm, d), lambda i, k: (i, 0)),
                  pl.BlockSpec((1, d), lambda i, k: (0, 0))],
        out_specs=[pl.BlockSpec((tm, d), lambda i, k: (i, 0)),
                   pl.BlockSpec((tm, d), lambda i, k: (i, 0))],
        out_shape=[SDS((t_len, d), f32), SDS((t_len, d), f32)],
        scratch_shapes=[pltpu.VMEM((tm, d), f32)],
        compiler_params=_params(("arbitrary", "arbitrary")),
    )(a, w, x, wn)


def _mm_dh(du, wt, x, dxn, wn, ktb, tm, name):
    kt, t_len, _ = du.shape
    d = wt.shape[1]
    nk = kt // ktb

    def body(du_ref, w_ref, x_ref, dxn_ref, wn_ref, dx_ref, dwn_ref, acc):
        i, k = pl.program_id(0), pl.program_id(1)

        @pl.when((i == 0) & (k == 0))
        def _():
            dwn_ref[...] = jnp.zeros_like(dwn_ref)

        @pl.when(k == 0)
        def _():
            acc[...] = jnp.zeros_like(acc)
        s = acc[...]
        for t in range(ktb):
            s = s + jnp.dot(_mx(du_ref[t]), w_ref[TILE * t:TILE * (t + 1), :], preferred_element_type=f32)
        acc[...] = s

        @pl.when(k == nk - 1)
        def _():
            dxp, dw = _rms_bwd(x_ref[...], wn_ref[...], acc[...])
            dx_ref[...] = dxn_ref[...] + dxp
            dwn_ref[...] += dw

    return _pc(
        body, name, grid=(t_len // tm, nk),
        in_specs=[pl.BlockSpec((ktb, tm, TILE), lambda i, k: (k, i, 0)),
                  pl.BlockSpec((ktb * TILE, d), lambda i, k: (k, 0)),
                  pl.BlockSpec((tm, d), lambda i, k: (i, 0)),
                  pl.BlockSpec((tm, d), lambda i, k: (i, 0)),
                  pl.BlockSpec((1, d), lambda i, k: (0, 0))],
        out_specs=[pl.BlockSpec((tm, d), lambda i, k: (i, 0)),
                   pl.BlockSpec((1, d), lambda i, k: (0, 0))],
        out_shape=[SDS((t_len, d), f32), SDS((1, d), f32)],
        scratch_shapes=[pltpu.VMEM((tm, d), f32)],
        compiler_params=_params(("arbitrary", "arbitrary")),
    )(du, wt, x, dxn, wn)


def _dw_cols(a, b, ntb, tk, name):
    t_len, kdim = a.shape
    nt = b.shape[0]

    def body(a_ref, b_ref, o_ref):
        @pl.when(pl.program_id(1) == 0)
        def _():
            o_ref[...] = jnp.zeros_like(o_ref)
        av = a_ref[...]
        for s in range(ntb):
            o_ref[:, TILE * s:TILE * (s + 1)] += _dot_tn(av, b_ref[s])

    return _pc(
        body, name, grid=(nt // ntb, t_len // tk),
        in_specs=[pl.BlockSpec((tk, kdim), lambda j, t: (t, 0)),
                  pl.BlockSpec((ntb, tk, TILE), lambda j, t: (j, t, 0))],
        out_specs=pl.BlockSpec((kdim, ntb * TILE), lambda j, t: (0, j)),
        out_shape=SDS((kdim, nt * TILE), f32),
        compiler_params=_params(("arbitrary", "arbitrary")),
    )(a, b)


def _dw_rows(a, b, ktb, tk, name):
    kt, t_len, _ = a.shape
    d = b.shape[1]

    def body(a_ref, b_ref, o_ref):
        @pl.when(pl.program_id(1) == 0)
        def _():
            o_ref[...] = jnp.zeros_like(o_ref)
        bv = b_ref[...]
        for s in range(ktb):
            o_ref[TILE * s:TILE * (s + 1), :] += _dot_tn(a_ref[s], bv)

    return _pc(
        body, name, grid=(kt // ktb, t_len // tk),
        in_specs=[pl.BlockSpec((ktb, tk, TILE), lambda k, t: (k, t, 0)),
                  pl.BlockSpec((tk, d), lambda k, t: (t, 0))],
        out_specs=pl.BlockSpec((ktb * TILE, d), lambda k, t: (k, 0)),
        out_shape=SDS((kt * TILE, d), f32),
        compiler_params=_params(("arbitrary", "arbitrary")),
    )(a, b)


def _loss_grad(x, tgt, tm):
    t_len, d = x.shape

    def body(x_ref, t_ref, dx_ref, l_ref):
        @pl.when(pl.program_id(0) == 0)
        def _():
            l_ref[...] = jnp.zeros_like(l_ref)
        e = x_ref[...] - t_ref[...]
        dx_ref[...] = e * (1.0 / d)
        row = jnp.mean(e * e, axis=-1, keepdims=True)
        l_ref[...] += 0.5 * jnp.sum(row, axis=0, keepdims=True)

    return _pc(
        body, "loss_grad", grid=(t_len // tm,),
        in_specs=[pl.BlockSpec((tm, d), lambda i: (i, 0)), pl.BlockSpec((tm, d), lambda i: (i, 0))],
        out_specs=[pl.BlockSpec((tm, d), lambda i: (i, 0)), pl.BlockSpec((1, 128), lambda i: (0, 0))],
        out_shape=[SDS((t_len, d), f32), SDS((1, 128), f32)],
        compiler_params=_params(("arbitrary",)),
    )(x, tgt)


def _tri(lower):
    r = lax.broadcasted_iota(jnp.int32, (CHUNK, CHUNK), 0)
    c = lax.broadcasted_iota(jnp.int32, (CHUNK, CHUNK), 1)
    return ((c <= r) if lower else (c >= r)).astype(f32)


def _dot_hi(a, b, dims):
    return lax.dot_general(a, b, (dims, ((), ())), precision=lax.Precision.HIGHEST, preferred_element_type=f32)


def _split(x, n):
    parts = []
    for _ in range(n):
        p = x.astype(jnp.bfloat16)
        parts.append(p)
        x = x - p.astype(f32)
    return parts


def _dot_exact(a, b, dims, split_a, n=3):
    out = None
    if split_a:
        b = b.astype(jnp.bfloat16)
        for p in _split(a, n):
            t = lax.dot_general(p, b, (dims, ((), ())), preferred_element_type=f32)
            out = t if out is None else out + t
    else:
        a = a.astype(jnp.bfloat16)
        for p in _split(b, n):
            t = lax.dot_general(a, p, (dims, ((), ())), preferred_element_type=f32)
            out = t if out is None else out + t
    return out


def _dt_path(dt_raw, dtb, alog):
    dtr = dt_raw + dtb
    dt = _softplus(dtr)
    a_neg = -jnp.exp(alog)
    a = dt * a_neg
    acs = _dot_exact(_tri(True), a, ((1,), (0,)), False)
    acs_t = _dot_exact(a, _tri(False), ((0,), (0,)), True)
    return dtr, dt, a_neg, acs, acs_t


def _conv_silu(x0, x1, x2, x3, w0, w1, w2, w3, b):
    acc = b + w0 * x0 + w1 * x1 + w2 * x2 + w3 * x3
    return _silu(acc)


def _ssd_group(xs, bm, cm, dtc, ac, ar, dh, hp):
    row = lax.broadcasted_iota(jnp.int32, (CHUNK, CHUNK), 0)
    col = lax.broadcasted_iota(jnp.int32, (CHUNK, CHUNK), 1)
    causal = col <= row
    last = (lax.broadcasted_iota(jnp.int32, (1, CHUNK), 1) == CHUNK - 1).astype(f32)
    cb = _dot_nt(cm, bm)
    ys, hns = [], []
    for r in range(4):
        xt = xs[r] * dtc[r]
        decay = jnp.exp(jnp.where(causal, ac[r] - ar[r], -jnp.inf))
        y_diag = _dot(cb * decay, xt)
        a_end = jnp.sum(ar[r] * last, axis=1, keepdims=True)
        y_off = _dot_nt(cm, hp[r]) * jnp.exp(ac[r])
        st = _dot_tn(xt * jnp.exp(a_end - ac[r]), bm)
        hns.append(hp[r] * jnp.exp(a_end) + st)
        ys.append(y_diag + y_off + dh[r] * xs[r])
    return tuple(ys), tuple(hns)


def _head_cols(g, r, dt, acs, acs_t, dsk):
    lane = lax.broadcasted_iota(jnp.int32, (1, SSM_HEADS), 1)
    sub = lax.broadcasted_iota(jnp.int32, (SSM_HEADS, 1), 0)
    h = 4 * g + r
    oh_l = (lane == h).astype(f32)
    oh_s = (sub == h).astype(f32)
    dtc = jnp.sum(dt * oh_l, axis=1, keepdims=True)
    ac = jnp.sum(acs * oh_l, axis=1, keepdims=True)
    ar = jnp.sum(acs_t * oh_s, axis=0, keepdims=True)
    dh = jnp.sum(dsk * oh_l, axis=1, keepdims=True)
    return h, oh_l, oh_s, dtc, ac, ar, dh


def _ssm_in_specs(nc, rev):
    def cidx(c):
        return (nc - 1 - c) if rev else c
    return [
        pl.BlockSpec((SSM_NT, CHUNK, TILE), lambda c: (0, cidx(c), 0)),
        pl.BlockSpec((SSM_NT, 8, TILE), lambda c: (0, jnp.maximum(cidx(c) * (CHUNK // 8) - 1, 0), 0)),
        pl.BlockSpec((16, 4, TILE), lambda c: (0, 0, 0)),
        pl.BlockSpec((16, 1, TILE), lambda c: (0, 0, 0)),
        pl.BlockSpec((1, SSM_HEADS), lambda c: (0, 0)),
        pl.BlockSpec((1, SSM_HEADS), lambda c: (0, 0)),
        pl.BlockSpec((1, SSM_HEADS), lambda c: (0, 0)),
        pl.BlockSpec((SSM_KT, 1, TILE), lambda c: (0, 0, 0)),
    ]


def _ssm_fwd(u, cw, cb, dtb, alog, dsk, gn, name):
    t_len = u.shape[1]
    nc = t_len // CHUNK

    def body(u_ref, halo_ref, cw_ref, cb_ref, dtb_ref, alog_ref, dsk_ref, gn_ref,
             a3_ref, yp_ref, hst_ref, h_s, win_s, xs_s, bc_s):
        c = pl.program_id(0)

        @pl.when(c == 0)
        def _():
            h_s[...] = jnp.zeros_like(h_s)
        _, dt, _, acs, acs_t = _dt_path(u_ref[SSM_NT - 1, :, 0:SSM_HEADS], dtb_ref[...], alog_ref[...])
        dsk = dsk_ref[...]

        def conv_tile(j, p):
            win_s[0:8, :] = jnp.where(c > 0, halo_ref[j], 0.0)
            win_s[8:8 + CHUNK, :] = u_ref[j]
            return _conv_silu(win_s[5:5 + CHUNK, :], win_s[6:6 + CHUNK, :], win_s[7:7 + CHUNK, :],
                              win_s[8:8 + CHUNK, :], cw_ref[p, 0:1, :], cw_ref[p, 1:2, :], cw_ref[p, 2:3, :],
                              cw_ref[p, 3:4, :], cb_ref[p])

        def group(g, s1):
            xs_s[...] = conv_tile(8 + g, g)
            bc_s[...] = conv_tile(16 + g, 8 + g)
            bm, cm = bc_s[:, 0:SSM_N], bc_s[:, SSM_N:2 * SSM_N]
            hs, xs, dtc, ac, ar, dh, hp = [], [], [], [], [], [], []
            for r in range(4):
                h, _, _, dtc_r, ac_r, ar_r, dh_r = _head_cols(g, r, dt, acs, acs_t, dsk)
                hs.append(h); dtc.append(dtc_r); ac.append(ac_r); ar.append(ar_r); dh.append(dh_r)
                xs.append(xs_s[:, SSM_P * r:SSM_P * (r + 1)])
                hp.append(h_s[h])
            ys, hns = _ssd_group(xs, bm, cm, dtc, ac, ar, dh, hp)
            for r in range(4):
                hst_ref[0, hs[r]] = hp[r]
                h_s[hs[r]] = hns[r]
                yp_ref[g, :, SSM_P * r:SSM_P * (r + 1)] = ys[r]
            y2 = yp_ref[g] * _silu(u_ref[g])
            return s1 + jnp.sum(y2 * y2, axis=1, keepdims=True)

        s1 = lax.fori_loop(0, SSM_GROUPS, group, jnp.zeros((CHUNK, 1), f32))
        rinv = lax.rsqrt(s1 * (1.0 / SSM_INNER) + EPS)

        def gate(g, carry):
            y2 = yp_ref[g] * _silu(u_ref[g])
            a3_ref[g] = (y2 * rinv * gn_ref[g]).astype(a3_ref.dtype)
            return carry

        lax.fori_loop(0, SSM_GROUPS, gate, 0)

    return _pc(
        body, name, grid=(nc,),
        in_specs=_ssm_in_specs(nc, False),
        out_specs=[pl.BlockSpec((SSM_KT, CHUNK, TILE), lambda c: (0, c, 0)),
                   pl.BlockSpec((SSM_KT, CHUNK, TILE), lambda c: (0, c, 0)),
                   pl.BlockSpec((1, SSM_HEADS, SSM_P, SSM_N), lambda c: (c, 0, 0, 0))],
        out_shape=[SDS((SSM_KT, t_len, TILE), MXU_DTYPE), SDS((SSM_KT, t_len, TILE), f32),
                   SDS((nc, SSM_HEADS, SSM_P, SSM_N), f32)],
        scratch_shapes=[pltpu.VMEM((SSM_HEADS, SSM_P, SSM_N), f32), pltpu.VMEM((8 + CHUNK, TILE), f32),
                        pltpu.VMEM((CHUNK, TILE), f32), pltpu.VMEM((CHUNK, TILE), f32)],
        compiler_params=_params(("arbitrary",)),
    )(u, u, cw, cb, dtb, alog, dsk, gn)


def _ssm_bwd(u, yp, hst, da3, cw, cb, dtb, alog, dsk, gn, name):
    t_len = u.shape[1]
    nc = t_len // CHUNK

    def body(u_ref, halo_ref, cw_ref, cb_ref, dtb_ref, alog_ref, dsk_ref, gn_ref, yp_ref, hst_ref, da3_ref,
             du_ref, dcw_ref, dcb_ref, ddtb_ref, dalog_ref, ddsk_ref, dgn_ref,
             dh_s, carry_s, win_s, dwin_s, xs_s, bc_s, dy_s, dxs_s, dbc_s):
        step = pl.program_id(0)
        cc = nc - 1 - step

        @pl.when(step == 0)
        def _():
            dh_s[...] = jnp.zeros_like(dh_s)
            carry_s[...] = jnp.zeros_like(carry_s)
            dcw_ref[...] = jnp.zeros_like(dcw_ref)
            dcb_ref[...] = jnp.zeros_like(dcb_ref)
            ddtb_ref[...] = jnp.zeros_like(ddtb_ref)
            dalog_ref[...] = jnp.zeros_like(dalog_ref)
            ddsk_ref[...] = jnp.zeros_like(ddsk_ref)
            dgn_ref[...] = jnp.zeros_like(dgn_ref)
        dtr, dt, a_neg, acs, acs_t = _dt_path(u_ref[SSM_NT - 1, :, 0:SSM_HEADS], dtb_ref[...], alog_ref[...])
        dsk = dsk_ref[...]

        def sums(g, carry):
            s1, s2 = carry
            y2 = yp_ref[g] * _silu(u_ref[g])
            g3 = da3_ref[g] * gn_ref[g]
            return (s1 + jnp.sum(y2 * y2, axis=1, keepdims=True), s2 + jnp.sum(g3 * y2, axis=1, keepdims=True))

        zcol = jnp.zeros((CHUNK, 1), f32)
        s1, s2 = lax.fori_loop(0, SSM_GROUPS, sums, (zcol, zcol))
        rinv = lax.rsqrt(s1 * (1.0 / SSM_INNER) + EPS)
        m2 = s2 * rinv * rinv * rinv * (1.0 / SSM_INNER)

        def windows(j):
            win_s[0:8, :] = jnp.where(cc > 0, halo_ref[j], 0.0)
            win_s[8:8 + CHUNK, :] = u_ref[j]
            return [win_s[5 + k:5 + k + CHUNK, :] for k in range(4)]

        def taps(p):
            return [cw_ref[p, k:k + 1, :] for k in range(4)]

        def conv_bwd(j, p, vjp_fn, dout):
            dx0, dx1, dx2, dx3, dw0, dw1, dw2, dw3, db = vjp_fn(dout)
            dwin_s[0:CHUNK, :] = jnp.zeros((CHUNK, TILE), f32)
            dwin_s[CHUNK:CHUNK + 8, :] = carry_s[j - 8]
            for k, dxk in enumerate((dx0, dx1, dx2, dx3)):
                dwin_s[5 + k:5 + k + CHUNK, :] += dxk
            du_ref[j] = dwin_s[8:8 + CHUNK, :]
            carry_s[j - 8] = dwin_s[0:8, :]
            for k, dwk in enumerate((dw0, dw1, dw2, dw3)):
                dcw_ref[p, k:k + 1, :] += dwk
            dcb_ref[p] += db

        def group(g, carry):
            ddt, dacs, dacs_t, ddsk = carry
            z = u_ref[g]
            sg = _sigmoid(z)
            sz = z * sg
            y = yp_ref[g]
            y2 = y * sz
            da3 = da3_ref[g]
            dgn_ref[g] += jnp.sum(da3 * y2 * rinv, axis=0, keepdims=True)
            dy2 = rinv * (da3 * gn_ref[g]) - y2 * m2
            dy_s[...] = dy2 * sz
            du_ref[g] = dy2 * y * (sg * (1.0 + z * (1.0 - sg)))
            xs_v, vjp_xs = jax.vjp(_conv_silu, *windows(8 + g), *taps(g), cb_ref[g])
            xs_s[...] = xs_v
            bc_v, vjp_bc = jax.vjp(_conv_silu, *windows(16 + g), *taps(8 + g), cb_ref[8 + g])
            bc_s[...] = bc_v
            bm, cm = bc_s[:, 0:SSM_N], bc_s[:, SSM_N:2 * SSM_N]
            hs, ohl, ohs, xs, dtc, ac, ar, dh, hp, dys, dhn = [], [], [], [], [], [], [], [], [], [], []
            for r in range(4):
                h, oh_l, oh_s, dtc_r, ac_r, ar_r, dh_r = _head_cols(g, r, dt, acs, acs_t, dsk)
                hs.append(h); ohl.append(oh_l); ohs.append(oh_s)
                dtc.append(dtc_r); ac.append(ac_r); ar.append(ar_r); dh.append(dh_r)
                xs.append(xs_s[:, SSM_P * r:SSM_P * (r + 1)])
                hp.append(hst_ref[0, h])
                dys.append(dy_s[:, SSM_P * r:SSM_P * (r + 1)])
                dhn.append(dh_s[h])
            _, vjp_g = jax.vjp(_ssd_group, xs, bm, cm, dtc, ac, ar, dh, hp)
            dxs, dbm, dcm, ddtc, dac, dar, ddh, dhp = vjp_g((tuple(dys), tuple(dhn)))
            for r in range(4):
                dxs_s[:, SSM_P * r:SSM_P * (r + 1)] = dxs[r]
                dh_s[hs[r]] = dhp[r]
                ddt = ddt + ddtc[r] * ohl[r]
                dacs = dacs + dac[r] * ohl[r]
                dacs_t = dacs_t + ohs[r] * dar[r]
                ddsk = ddsk + ddh[r] * ohl[r]
            dbc_s[:, 0:SSM_N] = dbm
            dbc_s[:, SSM_N:2 * SSM_N] = dcm
            conv_bwd(8 + g, g, vjp_xs, dxs_s[...])
            conv_bwd(16 + g, 8 + g, vjp_bc, dbc_s[...])
            return ddt, dacs, dacs_t, ddsk

        init = (jnp.zeros((CHUNK, SSM_HEADS), f32), jnp.zeros((CHUNK, SSM_HEADS), f32),
                jnp.zeros((SSM_HEADS, CHUNK), f32), jnp.zeros((1, SSM_HEADS), f32))
        ddt, dacs, dacs_t, ddsk = lax.fori_loop(0, SSM_GROUPS, group, init)
        upper = _tri(False)
        da = _dot_hi(upper, dacs, ((1,), (0,))) + _dot_hi(upper, dacs_t, ((1,), (1,)))
        ddt = ddt + da * a_neg
        dalog_ref[...] += jnp.sum(da * dt, axis=0, keepdims=True) * a_neg
        ddtr = ddt * _sigmoid(dtr)
        ddtb_ref[...] += jnp.sum(ddtr, axis=0, keepdims=True)
        ddsk_ref[...] += ddsk
        du_ref[SSM_NT - 1] = jnp.zeros((CHUNK, TILE), f32)
        du_ref[SSM_NT - 1, :, 0:SSM_HEADS] = ddtr

    def rc(c):
        return nc - 1 - c

    small = [pl.BlockSpec((16, 4, TILE), lambda c: (0, 0, 0)),
             pl.BlockSpec((16, 1, TILE), lambda c: (0, 0, 0)),
             pl.BlockSpec((1, SSM_HEADS), lambda c: (0, 0)),
             pl.BlockSpec((1, SSM_HEADS), lambda c: (0, 0)),
             pl.BlockSpec((1, SSM_HEADS), lambda c: (0, 0)),
             pl.BlockSpec((SSM_KT, 1, TILE), lambda c: (0, 0, 0))]
    return _pc(
        body, name, grid=(nc,),
        in_specs=_ssm_in_specs(nc, True) + [
            pl.BlockSpec((SSM_KT, CHUNK, TILE), lambda c: (0, rc(c), 0)),
            pl.BlockSpec((1, SSM_HEADS, SSM_P, SSM_N), lambda c: (rc(c), 0, 0, 0)),
            pl.BlockSpec((SSM_KT, CHUNK, TILE), lambda c: (0, rc(c), 0))],
        out_specs=[pl.BlockSpec((SSM_NT, CHUNK, TILE), lambda c: (0, rc(c), 0))] + small,
        out_shape=[SDS((SSM_NT, t_len, TILE), f32), SDS((16, 4, TILE), f32), SDS((16, 1, TILE), f32),
                   SDS((1, SSM_HEADS), f32), SDS((1, SSM_HEADS), f32), SDS((1, SSM_HEADS), f32),
                   SDS((SSM_KT, 1, TILE), f32)],
        scratch_shapes=[pltpu.VMEM((SSM_HEADS, SSM_P, SSM_N), f32), pltpu.VMEM((16, 8, TILE), f32),
                        pltpu.VMEM((8 + CHUNK, TILE), f32), pltpu.VMEM((8 + CHUNK, TILE), f32)]
                       + [pltpu.VMEM((CHUNK, TILE), f32)] * 5,
        compiler_params=_params(("arbitrary",)),
    )(u, u, cw, cb, dtb, alog, dsk, gn, yp, hst, da3)


CONV_ROWS = 1024
CONV_SUB = 32


def _conv_specs(nb, rows, rev):
    def ridx(i):
        return (nb - 1 - i) if rev else i
    return [
        pl.BlockSpec((1, rows, TILE), lambda p, i: (8 + p, ridx(i), 0)),
        pl.BlockSpec((1, 8, TILE), lambda p, i: (8 + p, jnp.maximum(ridx(i) * (rows // 8) - 1, 0), 0)),
        pl.BlockSpec((1, 4, TILE), lambda p, i: (p, 0, 0)),
        pl.BlockSpec((1, 1, TILE), lambda p, i: (p, 0, 0)),
    ]


def _conv_fwd(u, cw, cb, name):
    t_len = u.shape[1]
    rows = min(CONV_ROWS, t_len)
    nb = t_len // rows

    def body(u_ref, halo_ref, cw_ref, cb_ref, o_ref, win_s):
        i = pl.program_id(1)
        win_s[0:8, :] = jnp.where(i > 0, halo_ref[0], 0.0)
        win_s[8:8 + rows, :] = u_ref[0]
        w = [cw_ref[0, k:k + 1, :] for k in range(4)]
        b = cb_ref[0]
        for s in range(rows // CONV_SUB):
            o = CONV_SUB * s
            acc = b
            for k in range(4):
                acc = acc + w[k] * win_s[5 + k + o:5 + k + o + CONV_SUB, :]
            o_ref[0, o:o + CONV_SUB, :] = _silu(acc)

    return _pc(
        body, name, grid=(16, nb),
        in_specs=_conv_specs(nb, rows, False),
        out_specs=pl.BlockSpec((1, rows, TILE), lambda p, i: (p, i, 0)),
        out_shape=SDS((16, t_len, TILE), f32),
        scratch_shapes=[pltpu.VMEM((8 + rows, TILE), f32)],
        compiler_params=_params(("arbitrary", "arbitrary")),
    )(u, u, cw, cb)


def _conv_bwd(u, du, cw, cb, name):
    t_len = u.shape[1]
    rows = min(CONV_ROWS, t_len)
    nb = t_len // rows

    def body(u_ref, halo_ref, cw_ref, cb_ref, d_ref, o_ref, dcw_ref, dcb_ref, carry_s, win_s, dp_s):
        i = pl.program_id(1)
        ri = nb - 1 - i

        @pl.when(i == 0)
        def _():
            carry_s[...] = jnp.zeros_like(carry_s)
            dcw_ref[...] = jnp.zeros_like(dcw_ref)
            dcb_ref[...] = jnp.zeros_like(dcb_ref)
        win_s[0:8, :] = jnp.where(ri > 0, halo_ref[0], 0.0)
        win_s[8:8 + rows, :] = u_ref[0]
        w = [cw_ref[0, k:k + 1, :] for k in range(4)]
        b = cb_ref[0]
        dw = [jnp.zeros((1, TILE), f32)] * 4
        db = jnp.zeros((1, TILE), f32)
        for s in range(rows // CONV_SUB):
            o = CONV_SUB * s
            xk = [win_s[5 + k + o:5 + k + o + CONV_SUB, :] for k in range(4)]
            pre = b
            for k in range(4):
                pre = pre + w[k] * xk[k]
            sg = _sigmoid(pre)
            dpre = d_ref[0, o:o + CONV_SUB, :] * (sg * (1.0 + pre * (1.0 - sg)))
            dp_s[o:o + CONV_SUB, :] = dpre
            dw = [dw[k] + jnp.sum(dpre * xk[k], axis=0, keepdims=True) for k in range(4)]
            db = db + jnp.sum(dpre, axis=0, keepdims=True)
        dp_s[rows:rows + 8, :] = carry_s[...]
        for s in range(rows // CONV_SUB):
            o = CONV_SUB * s
            acc = w[0] * dp_s[3 + o:3 + o + CONV_SUB, :]
            for k in range(1, 4):
                acc = acc + w[k] * dp_s[3 - k + o:3 - k + o + CONV_SUB, :]
            o_ref[0, o:o + CONV_SUB, :] = acc
        carry_s[...] = dp_s[0:8, :]
        for k in range(4):
            dcw_ref[0, k:k + 1, :] += dw[k]
        dcb_ref[0] += db

    return _pc(
        body, name, grid=(16, nb),
        in_specs=_conv_specs(nb, rows, True) + [pl.BlockSpec((1, rows, TILE), lambda p, i: (8 + p, nb - 1 - i, 0))],
        out_specs=[pl.BlockSpec((1, rows, TILE), lambda p, i: (8 + p, nb - 1 - i, 0)),
                   pl.BlockSpec((1, 4, TILE), lambda p, i: (p, 0, 0)),
                   pl.BlockSpec((1, 1, TILE), lambda p, i: (p, 0, 0))],
        out_shape=[SDS(du.shape, f32), SDS((16, 4, TILE), f32), SDS((16, 1, TILE), f32)],
        input_output_aliases={4: 0},
        scratch_shapes=[pltpu.VMEM((8, TILE), f32), pltpu.VMEM((8 + rows, TILE), f32),
                        pltpu.VMEM((rows + 8, TILE), f32)],
        compiler_params=_params(("arbitrary", "arbitrary")),
    )(u, u, cw, cb, du)


def _collapse_matrix(g):
    r = lax.broadcasted_iota(jnp.int32, (SSM_HEADS, TILE), 0)
    c = lax.broadcasted_iota(jnp.int32, (SSM_HEADS, TILE), 1)
    return ((c // SSM_P) + 4 * g == r).astype(jnp.bfloat16)


def _ssd_prelude(dt_raw, dtb, alog, dsk, colx_s, scx_s):
    dtr, dt, a_neg, acs, acs_t = _dt_path(dt_raw, dtb, alog)
    a_end = acs[CHUNK - 1:CHUNK, :]
    lane = lax.broadcasted_iota(jnp.int32, (1, 2 * SSM_P), 1)
    lane4 = lax.broadcasted_iota(jnp.int32, (1, TILE), 1)
    sub8 = lax.broadcasted_iota(jnp.int32, (8, 1), 0)

    def row4(v, g):
        e = [v[:, 4 * g + r:4 * g + r + 1] for r in range(4)]
        return jnp.where(lane4 < 64, e[0], jnp.where(lane4 < 128, e[1], jnp.where(lane4 < 192, e[2], e[3])))

    for g in range(SSM_GROUPS):
        for k, arr in enumerate((dt, acs)):
            for half in range(2):
                h0 = 4 * g + 2 * half
                colx_s[k, g, :, 128 * half:128 * (half + 1)] = jnp.where(
                    lane < SSM_P, arr[:, h0:h0 + 1], arr[:, h0 + 1:h0 + 2])
        scx_s[g] = jnp.where(sub8 == 0, row4(dsk, g), jnp.where(sub8 == 1, row4(a_end, g), 0.0))
    return dtr, dt, a_neg, acs_t


def _ssm_core_specs(nc, rev):
    def cidx(c):
        return (nc - 1 - c) if rev else c
    return [
        pl.BlockSpec((SSM_KT, CHUNK, TILE), lambda c: (0, cidx(c), 0)),
        pl.BlockSpec((1, CHUNK, TILE), lambda c: (SSM_NT - 1, cidx(c), 0)),
        pl.BlockSpec((16, CHUNK, TILE), lambda c: (0, cidx(c), 0)),
        pl.BlockSpec((1, SSM_HEADS), lambda c: (0, 0)),
        pl.BlockSpec((1, SSM_HEADS), lambda c: (0, 0)),
        pl.BlockSpec((1, SSM_HEADS), lambda c: (0, 0)),
        pl.BlockSpec((SSM_KT, 1, TILE), lambda c: (0, 0, 0)),
    ]


def _causal():
    row = lax.broadcasted_iota(jnp.int32, (CHUNK, CHUNK), 0)
    col = lax.broadcasted_iota(jnp.int32, (CHUNK, CHUNK), 1)
    return col <= row


def _ssm2_fwd(u, xbc, dtb, alog, dsk, gn, name):
    t_len = u.shape[1]
    nc = t_len // CHUNK

    def body(z_ref, dt_ref, x_ref, dtb_ref, alog_ref, dsk_ref, gn_ref, a3_ref, yp_ref, hst_ref,
             h_s, colx_s, scx_s, xt_s, yd_s):
        c = pl.program_id(0)

        @pl.when(c == 0)
        def _():
            h_s[...] = jnp.zeros_like(h_s)
        _, _, _, acs_t = _ssd_prelude(dt_ref[0, :, 0:SSM_HEADS], dtb_ref[...], alog_ref[...], dsk_ref[...],
                                      colx_s, scx_s)
        causal = _causal()

        def group(g, s1):
            xs = x_ref[g]
            bm, cm = x_ref[8 + g, :, 0:SSM_N], x_ref[8 + g, :, SSM_N:2 * SSM_N]
            cb = _dot_nt(cm, bm)
            sc = scx_s[g]
            a_end = sc[1:2, :]
            rows = pltpu.roll(acs_t, (SSM_HEADS - 4 * g) % SSM_HEADS, 0)
            hp = h_s[g]
            xt = xs * colx_s[0, g]
            xt_s[...] = xt
            acx = colx_s[1, g]
            for r in range(4):
                hd = slice(SSM_P * r, SSM_P * (r + 1))
                lam = jnp.exp(jnp.where(causal, acx[:, SSM_P * r:SSM_P * r + 1] - rows[r:r + 1, :], -jnp.inf))
                yd_s[:, hd] = _dot(cb * lam, xt_s[:, hd])
            yp_ref[g] = yd_s[...] + _dot(cm, hp) * jnp.exp(acx) + sc[0:1, :] * xs
            hst_ref[0, g] = hp
            h_s[g] = hp * jnp.exp(a_end) + _dot_tn(bm, xt * jnp.exp(a_end - acx))
            y2 = yp_ref[g] * _silu(z_ref[g])
            return s1 + jnp.sum(y2 * y2, axis=1, keepdims=True)

        s1 = lax.fori_loop(0, SSM_GROUPS, group, jnp.zeros((CHUNK, 1), f32))
        rinv = lax.rsqrt(s1 * (1.0 / SSM_INNER) + EPS)

        def gate(g, carry):
            y2 = yp_ref[g] * _silu(z_ref[g])
            a3_ref[g] = (y2 * rinv * gn_ref[g]).astype(a3_ref.dtype)
            return carry

        lax.fori_loop(0, SSM_GROUPS, gate, 0)

    return _pc(
        body, name, grid=(nc,),
        in_specs=_ssm_core_specs(nc, False),
        out_specs=[pl.BlockSpec((SSM_KT, CHUNK, TILE), lambda c: (0, c, 0)),
                   pl.BlockSpec((SSM_KT, CHUNK, TILE), lambda c: (0, c, 0)),
                   pl.BlockSpec((1, SSM_GROUPS, SSM_N, TILE), lambda c: (c, 0, 0, 0))],
        out_shape=[SDS((SSM_KT, t_len, TILE), MXU_DTYPE), SDS((SSM_KT, t_len, TILE), f32),
                   SDS((nc, SSM_GROUPS, SSM_N, TILE), f32)],
        scratch_shapes=[pltpu.VMEM((SSM_GROUPS, SSM_N, TILE), f32), pltpu.VMEM((2, SSM_GROUPS, CHUNK, TILE), f32),
                        pltpu.VMEM((SSM_GROUPS, 8, TILE), f32), pltpu.VMEM((CHUNK, TILE), f32),
                        pltpu.VMEM((CHUNK, TILE), f32)],
        compiler_params=_params(("arbitrary",)),
    )(u, u, xbc, dtb, alog, dsk, gn)


def _ssm2_bwd(u, xbc, yp, hst, da3, dtb, alog, dsk, gn, name):
    t_len = u.shape[1]
    nc = t_len // CHUNK

    def body(z_ref, dt_ref, x_ref, dtb_ref, alog_ref, dsk_ref, gn_ref, yp_ref, hst_ref, da3_ref,
             du_ref, ddtb_ref, dalog_ref, ddsk_ref, dgn_ref,
             dh_s, colx_s, scx_s, xt_s, dy_s, dxt_s, ddtx_s, dacx_s, ddx_s, drow_s):
        step = pl.program_id(0)

        @pl.when(step == 0)
        def _():
            dh_s[...] = jnp.zeros_like(dh_s)
            ddtb_ref[...] = jnp.zeros_like(ddtb_ref)
            dalog_ref[...] = jnp.zeros_like(dalog_ref)
            ddsk_ref[...] = jnp.zeros_like(ddsk_ref)
            dgn_ref[...] = jnp.zeros_like(dgn_ref)
        dtr, dt, a_neg, acs_t = _ssd_prelude(dt_ref[0, :, 0:SSM_HEADS], dtb_ref[...], alog_ref[...], dsk_ref[...],
                                             colx_s, scx_s)
        causal = _causal()
        causal_t = (lax.broadcasted_iota(jnp.int32, (CHUNK, CHUNK), 1)
                    >= lax.broadcasted_iota(jnp.int32, (CHUNK, CHUNK), 0))
        last = (lax.broadcasted_iota(jnp.int32, (1, CHUNK), 1) == CHUNK - 1).astype(f32)
        lane = lax.broadcasted_iota(jnp.int32, (1, TILE), 1)
        sub32 = lax.broadcasted_iota(jnp.int32, (SSM_HEADS, 1), 0)
        drow_s[...] = jnp.zeros_like(drow_s)

        def sums(g, carry):
            s1, s2 = carry
            y2 = yp_ref[g] * _silu(z_ref[g])
            g3 = da3_ref[g] * gn_ref[g]
            return (s1 + jnp.sum(y2 * y2, axis=1, keepdims=True), s2 + jnp.sum(g3 * y2, axis=1, keepdims=True))

        zcol = jnp.zeros((CHUNK, 1), f32)
        s1, s2 = lax.fori_loop(0, SSM_GROUPS, sums, (zcol, zcol))
        rinv = lax.rsqrt(s1 * (1.0 / SSM_INNER) + EPS)
        m2 = s2 * rinv * rinv * rinv * (1.0 / SSM_INNER)

        def group(g, carry):
            z = z_ref[g]
            sg = _sigmoid(z)
            sz = z * sg
            y = yp_ref[g]
            y2 = y * sz
            da3 = da3_ref[g]
            dgn_ref[g] += jnp.sum(da3 * y2 * rinv, axis=0, keepdims=True)
            dy2 = rinv * (da3 * gn_ref[g]) - y2 * m2
            dy = dy2 * sz
            dy_s[...] = dy
            du_ref[g] = dy2 * y * (sg * (1.0 + z * (1.0 - sg)))
            xs = x_ref[g]
            bm, cm = x_ref[8 + g, :, 0:SSM_N], x_ref[8 + g, :, SSM_N:2 * SSM_N]
            cb = _dot_nt(cm, bm)
            cbt = _dot_nt(bm, cm)
            dtx, acx = colx_s[0, g], colx_s[1, g]
            sc = scx_s[g]
            a_end = sc[1:2, :]
            ex = jnp.exp(acx)
            wdx = jnp.exp(a_end - acx)
            eend = jnp.exp(a_end)
            rows = pltpu.roll(acs_t, (SSM_HEADS - 4 * g) % SSM_HEADS, 0)
            hp = hst_ref[0, g]
            dhn = dh_s[g]
            xt = xs * dtx
            xt_s[...] = xt
            ch = _dot(cm, hp)
            gy = dy * ex
            dcm = _dot_nt(gy, hp)
            dh_s[g] = _dot_tn(cm, gy) + dhn * eend
            q = _dot(bm, dhn)
            dbm = _dot_nt(xt * wdx, dhn)
            qx = q * xt * wdx
            v_end = jnp.sum(dhn * hp, axis=0, keepdims=True) * eend + jnp.sum(qx, axis=0, keepdims=True)
            dcb = jnp.zeros((CHUNK, CHUNK), f32)
            drows = jnp.zeros((SSM_HEADS, CHUNK), f32)
            for r in range(4):
                hd = slice(SSM_P * r, SSM_P * (r + 1))
                in_head = (lane >= SSM_P * r) & (lane < SSM_P * (r + 1))
                ac = acx[:, SSM_P * r:SSM_P * r + 1]
                ar = rows[r:r + 1, :]
                lam = jnp.exp(jnp.where(causal, ac - ar, -jnp.inf))
                lam_t = jnp.exp(jnp.where(causal_t, ar - ac, -jnp.inf))
                m = cb * lam
                m_t = cbt * lam_t
                dm = _dot_nt(dy_s[:, hd], xt_s[:, hd])
                dm_t = _dot_nt(xt_s[:, hd], dy_s[:, hd])
                dxt_s[:, hd] = _dot(m_t, dy_s[:, hd])
                dcb = dcb + dm * lam
                d_ac = jnp.sum(dm_t * m_t, axis=0, keepdims=True)
                d_ar = jnp.sum(dm * m, axis=0, keepdims=True)
                d_aend = jnp.sum(jnp.where(in_head, v_end, 0.0), axis=1, keepdims=True)
                drows = drows + jnp.where(sub32 == r, d_ac - d_ar + last * d_aend, 0.0)
            dxt = dxt_s[...] + q * wdx
            du_ref[8 + g] = sc[0:1, :] * dy + dxt * dtx
            ddtx_s[g] = dxt * xs
            dacx_s[g] = dy * ch * ex - qx
            ddx_s[g] = jnp.broadcast_to(jnp.sum(dy * xs, axis=0, keepdims=True), (8, TILE))
            du_ref[16 + g, :, 0:SSM_N] = dbm + _dot_tn(dcb, cm)
            du_ref[16 + g, :, SSM_N:2 * SSM_N] = dcm + _dot(dcb, bm)
            drow_s[...] += pltpu.roll(drows, (4 * g) % SSM_HEADS, 0)
            return carry

        lax.fori_loop(0, SSM_GROUPS, group, 0)
        ddt = jnp.zeros((CHUNK, SSM_HEADS), f32)
        dacs = jnp.zeros((CHUNK, SSM_HEADS), f32)
        ddsk = jnp.zeros((8, SSM_HEADS), f32)
        for g in range(SSM_GROUPS):
            col_g = _collapse_matrix(g)
            ddt = ddt + _dot_exact(ddtx_s[g], col_g, ((1,), (1,)), True, 2)
            dacs = dacs + _dot_exact(dacx_s[g], col_g, ((1,), (1,)), True, 2)
            ddsk = ddsk + _dot_exact(ddx_s[g], col_g, ((1,), (1,)), True, 2)
        upper = _tri(False)
        da = _dot_exact(upper, dacs, ((1,), (0,)), False) + _dot_exact(upper, drow_s[...], ((1,), (1,)), False)
        ddt = ddt + da * a_neg
        dalog_ref[...] += jnp.sum(da * dt, axis=0, keepdims=True) * a_neg
        ddtr = ddt * _sigmoid(dtr)
        ddtb_ref[...] += jnp.sum(ddtr, axis=0, keepdims=True)
        ddsk_ref[...] += ddsk[0:1, :]
        du_ref[SSM_NT - 1] = jnp.zeros((CHUNK, TILE), f32)
        du_ref[SSM_NT - 1, :, 0:SSM_HEADS] = ddtr

    def rc(c):
        return nc - 1 - c

    vec = pl.BlockSpec((1, SSM_HEADS), lambda c: (0, 0))
    return _pc(
        body, name, grid=(nc,),
        in_specs=_ssm_core_specs(nc, True) + [
            pl.BlockSpec((SSM_KT, CHUNK, TILE), lambda c: (0, rc(c), 0)),
            pl.BlockSpec((1, SSM_GROUPS, SSM_N, TILE), lambda c: (rc(c), 0, 0, 0)),
            pl.BlockSpec((SSM_KT, CHUNK, TILE), lambda c: (0, rc(c), 0))],
        out_specs=[pl.BlockSpec((SSM_NT, CHUNK, TILE), lambda c: (0, rc(c), 0)), vec, vec, vec,
                   pl.BlockSpec((SSM_KT, 1, TILE), lambda c: (0, 0, 0))],
        out_shape=[SDS((SSM_NT, t_len, TILE), f32), SDS((1, SSM_HEADS), f32), SDS((1, SSM_HEADS), f32),
                   SDS((1, SSM_HEADS), f32), SDS((SSM_KT, 1, TILE), f32)],
        scratch_shapes=[pltpu.VMEM((SSM_GROUPS, SSM_N, TILE), f32), pltpu.VMEM((2, SSM_GROUPS, CHUNK, TILE), f32),
                        pltpu.VMEM((SSM_GROUPS, 8, TILE), f32), pltpu.VMEM((CHUNK, TILE), f32),
                        pltpu.VMEM((CHUNK, TILE), f32), pltpu.VMEM((CHUNK, TILE), f32),
                        pltpu.VMEM((SSM_GROUPS, CHUNK, TILE), f32), pltpu.VMEM((SSM_GROUPS, CHUNK, TILE), f32),
                        pltpu.VMEM((SSM_GROUPS, 8, TILE), f32), pltpu.VMEM((SSM_HEADS, CHUNK), f32)],
        compiler_params=_params(("arbitrary",)),
    )(u, u, xbc, dtb, alog, dsk, gn, yp, hst, da3)


def _swap16(t):
    lane = lax.broadcasted_iota(jnp.int32, t.shape, 1) % 64
    return jnp.where(lane < 8, pltpu.roll(t, TILE - 8, 1), jnp.where(lane < 16, pltpu.roll(t, 8, 1), 0.0))


def _rope(t, cos_t, sin_t):
    return t * cos_t + _swap16(t) * sin_t


def _rope_bwd(g, cos_t, sin_t):
    return g * cos_t + _swap16(g * sin_t)


def _att_head(qh, kp, kc, vp, vc, sink, mask_p, mask_c):
    sp = jnp.where(mask_p, _dot_nt(qh, kp) * 0.125, -jnp.inf)
    sc = jnp.where(mask_c, _dot_nt(qh, kc) * 0.125, -jnp.inf)
    m = jnp.maximum(jnp.maximum(jnp.max(sp, axis=-1, keepdims=True), jnp.max(sc, axis=-1, keepdims=True)), sink)
    m = lax.stop_gradient(m)
    pp = jnp.exp(sp - m)
    pc = jnp.exp(sc - m)
    den = jnp.sum(pp, axis=-1, keepdims=True) + jnp.sum(pc, axis=-1, keepdims=True) + jnp.exp(sink - m)
    inv = 1.0 / den
    return _dot(pp * inv, vp) + _dot(pc * inv, vc)


def _att_masks(has_prev):
    row = lax.broadcasted_iota(jnp.int32, (CHUNK, CHUNK), 0)
    col = lax.broadcasted_iota(jnp.int32, (CHUNK, CHUNK), 1)
    return (col > row) & has_prev, col <= row


def _att_in_specs(nb, rev):
    def bidx(n):
        return (nb - 1 - n) if rev else n
    return [
        pl.BlockSpec((ATT_NT, CHUNK, TILE), lambda n: (0, bidx(n), 0)),
        pl.BlockSpec((2, CHUNK, TILE), lambda n: (2, jnp.maximum(bidx(n) - 1, 0), 0)),
        pl.BlockSpec((CHUNK, TILE), lambda n: (bidx(n), 0)),
        pl.BlockSpec((CHUNK, TILE), lambda n: (bidx(n), 0)),
        pl.BlockSpec((CHUNK, TILE), lambda n: (jnp.maximum(bidx(n) - 1, 0), 0)),
        pl.BlockSpec((CHUNK, TILE), lambda n: (jnp.maximum(bidx(n) - 1, 0), 0)),
        pl.BlockSpec((1, 16), lambda n: (0, 0)),
    ]


def _att_fwd(u, cos_t, sin_t, sinks, name):
    t_len = u.shape[1]
    nb = t_len // CHUNK

    def body(u_ref, prev_ref, cc_ref, sc_ref, cp_ref, sp_ref, sink_ref, a_ref,
             q_s, kp_s, kc_s, vp_s, vc_s, o_s):
        n = pl.program_id(0)
        mask_p, mask_c = _att_masks(n > 0)
        cos_c, sin_c = cc_ref[...], sc_ref[...]
        kc_s[...] = _rope(u_ref[4], cos_c, sin_c)
        kp_s[...] = _rope(prev_ref[0], cp_ref[...], sp_ref[...])
        vc_s[...] = u_ref[5]
        vp_s[...] = prev_ref[1]
        sinks = sink_ref[...]
        for g in range(4):
            q_s[...] = _rope(u_ref[g], cos_c, sin_c)
            kv = slice(64 * g, 64 * (g + 1))
            for r in range(4):
                hd = slice(64 * r, 64 * (r + 1))
                h = 4 * g + r
                o_s[:, hd] = _att_head(q_s[:, hd], kp_s[:, kv], kc_s[:, kv], vp_s[:, kv], vc_s[:, kv],
                                       sinks[:, h:h + 1], mask_p, mask_c)
            a_ref[g] = (o_s[...] * _silu(u_ref[6 + g])).astype(a_ref.dtype)

    return _pc(
        body, name, grid=(nb,),
        in_specs=_att_in_specs(nb, False),
        out_specs=pl.BlockSpec((ATT_KT, CHUNK, TILE), lambda n: (0, n, 0)),
        out_shape=SDS((ATT_KT, t_len, TILE), MXU_DTYPE),
        scratch_shapes=[pltpu.VMEM((CHUNK, TILE), f32)] * 6,
        compiler_params=_params(("arbitrary",)),
    )(u, u, cos_t, sin_t, cos_t, sin_t, sinks)


def _att_bwd(u, cos_t, sin_t, sinks, da, name):
    t_len = u.shape[1]
    nb = t_len // CHUNK

    def body(u_ref, prev_ref, cc_ref, sc_ref, cp_ref, sp_ref, sink_ref, da_ref, du_ref, dsink_ref,
             ck_s, cv_s, q_s, kp_s, kc_s, vp_s, vc_s, o_s, do_s, dq_s, dkp_s, dkc_s, dvp_s, dvc_s):
        step = pl.program_id(0)
        nn = nb - 1 - step

        @pl.when(step == 0)
        def _():
            ck_s[...] = jnp.zeros_like(ck_s)
            cv_s[...] = jnp.zeros_like(cv_s)
            dsink_ref[...] = jnp.zeros_like(dsink_ref)
        mask_p, mask_c = _att_masks(nn > 0)
        cos_c, sin_c = cc_ref[...], sc_ref[...]
        cos_p, sin_p = cp_ref[...], sp_ref[...]
        kc_s[...] = _rope(u_ref[4], cos_c, sin_c)
        kp_s[...] = _rope(prev_ref[0], cos_p, sin_p)
        vc_s[...] = u_ref[5]
        vp_s[...] = prev_ref[1]
        sinks = sink_ref[...]
        lane16 = lax.broadcasted_iota(jnp.int32, (1, 16), 1)
        dsink = jnp.zeros((1, 16), f32)
        att = functools.partial(_att_head, mask_p=mask_p, mask_c=mask_c)
        for g in range(4):
            q_s[...] = _rope(u_ref[g], cos_c, sin_c)
            gate = u_ref[6 + g]
            sg = _sigmoid(gate)
            dav = da_ref[g]
            do_s[...] = dav * (gate * sg)
            kv = slice(64 * g, 64 * (g + 1))
            dkp = jnp.zeros((CHUNK, 64), f32)
            dkc = jnp.zeros((CHUNK, 64), f32)
            dvp = jnp.zeros((CHUNK, 64), f32)
            dvc = jnp.zeros((CHUNK, 64), f32)
            for r in range(4):
                hd = slice(64 * r, 64 * (r + 1))
                h = 4 * g + r
                o_h, vjp_fn = jax.vjp(att, q_s[:, hd], kp_s[:, kv], kc_s[:, kv], vp_s[:, kv], vc_s[:, kv],
                                      sinks[:, h:h + 1])
                dq_h, dkp_h, dkc_h, dvp_h, dvc_h, ds_h = vjp_fn(do_s[:, hd])
                o_s[:, hd] = o_h
                dq_s[:, hd] = dq_h
                dkp, dkc, dvp, dvc = dkp + dkp_h, dkc + dkc_h, dvp + dvp_h, dvc + dvc_h
                dsink = dsink + ds_h * (lane16 == h).astype(f32)
            du_ref[6 + g] = dav * o_s[...] * (sg * (1.0 + gate * (1.0 - sg)))
            du_ref[g] = _rope_bwd(dq_s[...], cos_c, sin_c)
            dkp_s[:, kv] = dkp
            dkc_s[:, kv] = dkc
            dvp_s[:, kv] = dvp
            dvc_s[:, kv] = dvc
        du_ref[4] = _rope_bwd(dkc_s[...], cos_c, sin_c) + ck_s[...]
        du_ref[5] = dvc_s[...] + cv_s[...]
        ck_s[...] = _rope_bwd(dkp_s[...], cos_p, sin_p)
        cv_s[...] = dvp_s[...]
        dsink_ref[...] += dsink

    def rb(n):
        return nb - 1 - n

    return _pc(
        body, name, grid=(nb,),
        in_specs=_att_in_specs(nb, True) + [pl.BlockSpec((ATT_KT, CHUNK, TILE), lambda n: (0, rb(n), 0))],
        out_specs=[pl.BlockSpec((ATT_NT, CHUNK, TILE), lambda n: (0, rb(n), 0)),
                   pl.BlockSpec((1, 16), lambda n: (0, 0))],
        out_shape=[SDS((ATT_NT, t_len, TILE), f32), SDS((1, 16), f32)],
        scratch_shapes=[pltpu.VMEM((CHUNK, TILE), f32)] * 14,
        compiler_params=_params(("arbitrary",)),
    )(u, u, cos_t, sin_t, cos_t, sin_t, sinks, da)


_HBM = pl.BlockSpec(memory_space=pltpu.HBM)


def _all_gather_big(shard):
    rows, width = shard.shape

    def body(x_ref, out_ref, send_sems, recv_sems, local_sem):
        x, y, c = lax.axis_index("x"), lax.axis_index("y"), lax.axis_index("c")
        me, sibling = (x, y, c), (x, y, 1 - c)
        chips = [(1 - x, y), (x, 1 - y), (1 - x, 1 - y)]

        def slot(px, py, pc):
            return out_ref.at[4 * px + 2 * py + pc]

        def copy(k, block, to, src=None):
            return pltpu.make_async_remote_copy(
                src_ref=slot(*block) if src is None else src, dst_ref=slot(*block),
                send_sem=send_sems.at[k], recv_sem=recv_sems.at[k], device_id=to, device_id_type=MESH)

        mine = pltpu.make_async_copy(x_ref, slot(*me), local_sem)
        mine.start()
        first = [copy(0, me, sibling, src=x_ref)]
        first += [copy(1 + j, me, (*chip, c), src=x_ref) for j, chip in enumerate(chips)]
        for cp in first:
            cp.start()
        passed = [copy(4 + j, (*chip, c), sibling) for j, chip in enumerate(chips)]
        for j, chip in enumerate(chips):
            copy(1 + j, (*chip, c), me).wait_recv()
            passed[j].start()
        copy(0, sibling, me).wait_recv()
        for j, chip in enumerate(chips):
            copy(4 + j, (*chip, 1 - c), me).wait_recv()
        for cp in first + passed:
            cp.wait_send()
        mine.wait()

    return _pc(
        body, "all_gather_big",
        in_specs=[_HBM], out_specs=_HBM,
        out_shape=SDS((N_DEV, rows, width), shard.dtype),
        scratch_shapes=[pltpu.SemaphoreType.DMA((7,)), pltpu.SemaphoreType.DMA((7,)), pltpu.SemaphoreType.DMA],
    )(shard)


def _all_gather_direct(block, name):
    rows, width = block.shape

    def body(x_ref, out_ref, send_sems, recv_sems, local_sem):
        x, y, c = lax.axis_index("x"), lax.axis_index("y"), lax.axis_index("c")
        my_slot = 4 * x + 2 * y + c

        def peer(k):
            return (1 - x if k & 4 else x, 1 - y if k & 2 else y, 1 - c if k & 1 else c)

        def copy(k):
            px, py, pc = peer(k)
            return pltpu.make_async_remote_copy(
                src_ref=x_ref, dst_ref=out_ref.at[my_slot], send_sem=send_sems.at[k - 1], recv_sem=recv_sems.at[k - 1],
                device_id=(px, py, pc), device_id_type=MESH)

        def arrival(k):
            px, py, pc = peer(k)
            return pltpu.make_async_remote_copy(
                src_ref=x_ref, dst_ref=out_ref.at[4 * px + 2 * py + pc], send_sem=send_sems.at[k - 1],
                recv_sem=recv_sems.at[k - 1], device_id=(px, py, pc), device_id_type=MESH)

        mine = pltpu.make_async_copy(x_ref, out_ref.at[my_slot], local_sem)
        mine.start()
        for k in range(1, N_DEV):
            copy(k).start()
        for k in range(1, N_DEV):
            arrival(k).wait_recv()
        for k in range(1, N_DEV):
            copy(k).wait_send()
        mine.wait()

    return _pc(
        body, name,
        in_specs=[_HBM], out_specs=_HBM,
        out_shape=SDS((N_DEV, rows, width), block.dtype),
        scratch_shapes=[pltpu.SemaphoreType.DMA((7,)), pltpu.SemaphoreType.DMA((7,)), pltpu.SemaphoreType.DMA],
    )(block)


def _exchange_sibling(g2):
    _, nchip, rows, width = g2.shape

    def body(g_ref, out_ref, send_sem, recv_sem):
        x, y, c = lax.axis_index("x"), lax.axis_index("y"), lax.axis_index("c")
        cp = pltpu.make_async_remote_copy(
            src_ref=g_ref.at[1 - c], dst_ref=out_ref, send_sem=send_sem, recv_sem=recv_sem,
            device_id=(x, y, 1 - c), device_id_type=MESH)
        cp.start()
        cp.wait()

    return _pc(
        body, "rs_sibling",
        in_specs=[_HBM], out_specs=_HBM,
        out_shape=SDS((nchip, rows, width), g2.dtype),
        scratch_shapes=[pltpu.SemaphoreType.DMA, pltpu.SemaphoreType.DMA],
    )(g2)


def _pair_sum(g2, r1, cidx, tr):
    _, nchip, rows, width = g2.shape

    def body(c_ref, g_ref, r_ref, o_ref):
        o_ref[...] = g_ref[0] + r_ref[...]

    return pl.pallas_call(
        body, name="rs_pair_sum",
        grid_spec=pltpu.PrefetchScalarGridSpec(
            num_scalar_prefetch=1, grid=(nchip, rows // tr),
            in_specs=[pl.BlockSpec((1, 1, tr, width), lambda k, i, c_ref: (c_ref[0], k, i, 0)),
                      pl.BlockSpec((1, tr, width), lambda k, i, c_ref: (k, i, 0))],
            out_specs=pl.BlockSpec((1, tr, width), lambda k, i, c_ref: (k, i, 0))),
        out_shape=SDS((nchip, rows, width), g2.dtype),
        compiler_params=_params(("arbitrary", "arbitrary")),
    )(cidx, g2, r1)


def _exchange_chips(p):
    nchip, rows, width = p.shape

    def body(p_ref, out_ref, send_sems, recv_sems, local_sem):
        x, y, c = lax.axis_index("x"), lax.axis_index("y"), lax.axis_index("c")
        my_chip = 2 * x + y
        chips = [(1 - x, y), (x, 1 - y), (1 - x, 1 - y)]

        def copy(j):
            px, py = chips[j]
            return pltpu.make_async_remote_copy(
                src_ref=p_ref.at[2 * px + py], dst_ref=out_ref.at[my_chip], send_sem=send_sems.at[j],
                recv_sem=recv_sems.at[j], device_id=(px, py, c), device_id_type=MESH)

        def arrival(j):
            px, py = chips[j]
            return pltpu.make_async_remote_copy(
                src_ref=p_ref.at[my_chip], dst_ref=out_ref.at[2 * px + py], send_sem=send_sems.at[j],
                recv_sem=recv_sems.at[j], device_id=(px, py, c), device_id_type=MESH)

        mine = pltpu.make_async_copy(p_ref.at[my_chip], out_ref.at[my_chip], local_sem)
        mine.start()
        for j in range(3):
            copy(j).start()
        for j in range(3):
            arrival(j).wait_recv()
        for j in range(3):
            copy(j).wait_send()
        mine.wait()

    return _pc(
        body, "rs_chips",
        in_specs=[_HBM], out_specs=_HBM,
        out_shape=SDS((nchip, rows, width), p.dtype),
        scratch_shapes=[pltpu.SemaphoreType.DMA((3,)), pltpu.SemaphoreType.DMA((3,)), pltpu.SemaphoreType.DMA],
    )(p)


def _adamw(parts, w, m, v, tr, name):
    n, rows, width = parts.shape
    c1 = 1.0 / (1.0 - ADAM_B1 ** ADAM_STEP)
    c2 = 1.0 / (1.0 - ADAM_B2 ** ADAM_STEP)

    def body(p_ref, w_ref, m_ref, v_ref, g_ref, d_ref, mo_ref, vo_ref):
        g = p_ref[0]
        for k in range(1, n):
            g = g + p_ref[k]
        mn = ADAM_B1 * m_ref[...] + (1.0 - ADAM_B1) * g
        vn = ADAM_B2 * v_ref[...] + (1.0 - ADAM_B2) * (g * g)
        g_ref[...] = g
        mo_ref[...] = mn
        vo_ref[...] = vn
        d_ref[...] = -ADAM_LR * ((mn * c1) / (jnp.sqrt(vn * c2) + ADAM_EPS) + ADAM_WD * w_ref[...])

    blk = pl.BlockSpec((tr, width), lambda i: (i, 0))
    return _pc(
        body, name, grid=(rows // tr,),
        in_specs=[pl.BlockSpec((n, tr, width), lambda i: (0, i, 0)), blk, blk, blk],
        out_specs=[blk, blk, blk, blk],
        out_shape=[SDS((rows, width), f32)] * 4,
        compiler_params=_params(("arbitrary",)),
    )(parts, w, m, v)


def _pack_big(ssm_w_in, ssm_w_out, att_w_in, att_w_out, conv_w=None):
    parts = [ssm_w_in.reshape(ROWS_SSM_IN, 1024), ssm_w_out.reshape(ROWS_SSM_OUT, 1024),
             att_w_in.reshape(ROWS_ATT_IN, 1024), att_w_out.reshape(ROWS_ATT_OUT, 1024)]
    if conv_w is None:
        parts.append(jnp.zeros((ROWS_CONV, 1024), f32))
    else:
        parts.append(jnp.concatenate([conv_w.reshape(4, 1024), jnp.zeros((4, 1024), f32)], axis=0))
    return jnp.concatenate(parts, axis=0)


def _unpack_big(p):
    o = 0
    out = []
    for rows, shape in ((ROWS_SSM_IN, (2, 1024, 772)), (ROWS_SSM_OUT, (2, 256, 1024)),
                        (ROWS_ATT_IN, (2, 1024, 320)), (ROWS_ATT_OUT, (2, 128, 1024))):
        out.append(p[o:o + rows].reshape(shape))
        o += rows
    out.append(p[o:o + 4].reshape(2, 4, 512))
    return out


def _by_class(a):
    return jnp.swapaxes(a.reshape((4, 2) + a.shape[1:]), 0, 1)


def _pack_grads(d_ssm_w_in, d_ssm_w_out, d_att_w_in, d_att_w_out, d_conv_w):
    a = jnp.transpose(d_ssm_w_in.reshape(2, 1024, 8, 772), (2, 0, 1, 3)).reshape(8, ROWS_SSM_IN, 1024)
    b = jnp.transpose(d_ssm_w_out.reshape(2, 8, 256, 1024), (1, 0, 2, 3)).reshape(8, ROWS_SSM_OUT, 1024)
    c = jnp.transpose(d_att_w_in.reshape(2, 1024, 8, 320), (2, 0, 1, 3)).reshape(8, ROWS_ATT_IN, 1024)
    d = jnp.transpose(d_att_w_out.reshape(2, 8, 128, 1024), (1, 0, 2, 3)).reshape(8, ROWS_ATT_OUT, 1024)
    e = jnp.transpose(d_conv_w.reshape(2, 4, 8, 512), (2, 0, 1, 3)).reshape(8, 4, 1024)
    e = jnp.concatenate([e, jnp.zeros((8, 4, 1024), f32)], axis=1)
    return _by_class(jnp.concatenate([a, b, c, d, e], axis=1))


def _pad8(a):
    return jnp.pad(a, ((0, 8 - a.shape[0]), (0, 0)))


def _pack_small(pre_norm, post_norm, conv_b, gate_norm, dt_bias, a_log, d_skip, sinks, extra=None):
    row = jnp.concatenate([dt_bias.reshape(1, 64), a_log.reshape(1, 64), d_skip.reshape(1, 64), sinks.reshape(1, 32),
                           jnp.zeros((1, 1024 - 224), f32)], axis=1)
    if extra is not None:
        row = row + jnp.pad(extra.reshape(1, 1), ((0, 0), (224, 1024 - 225)))
    return jnp.concatenate([_pad8(pre_norm.reshape(4, 1024)), _pad8(post_norm.reshape(4, 1024)),
                            conv_b.reshape(8, 1024), _pad8(gate_norm.reshape(4, 1024)), _pad8(row)], axis=0)


def _unpack_small(p):
    row = p[32]
    return (p[0:4], p[8:12], p[16:24].reshape(2, 4096), row[0:64].reshape(2, 32), row[64:128].reshape(2, 32),
            row[128:192].reshape(2, 32), p[24:28].reshape(2, 2048), row[192:224].reshape(2, 16))


def _ssm_w_in_tiles(w):
    wb = w[:, 4096:5120].reshape(1024, 8, 128)
    wc = w[:, 5120:6144].reshape(1024, 8, 128)
    wbc = jnp.concatenate([wb, wc], axis=2).reshape(1024, 2048)
    return jnp.concatenate([w[:, 0:4096], wbc, w[:, 6144:6176], jnp.zeros((1024, 224), w.dtype)], axis=1)


def _ssm_w_in_untile(dw):
    dbc = dw[:, 4096:6144].reshape(1024, 8, 256)
    return jnp.concatenate([dw[:, 0:4096], dbc[:, :, 0:128].reshape(1024, 1024), dbc[:, :, 128:256].reshape(1024, 1024),
                            dw[:, 6144:6176]], axis=1)


def _conv_tiles(cw):
    k = cw.shape[0]
    xs = jnp.transpose(cw[:, 0:2048].reshape(k, 8, 256), (1, 0, 2))
    b = cw[:, 2048:3072].reshape(k, 8, 128)
    c = cw[:, 3072:4096].reshape(k, 8, 128)
    bc = jnp.transpose(jnp.concatenate([b, c], axis=2), (1, 0, 2))
    return jnp.concatenate([xs, bc], axis=0)


def _conv_untile(t):
    k = t.shape[1]
    xs = jnp.transpose(t[0:8], (1, 0, 2)).reshape(k, 2048)
    bc = jnp.transpose(t[8:16], (1, 0, 2))
    return jnp.concatenate([xs, bc[:, :, 0:128].reshape(k, 1024), bc[:, :, 128:256].reshape(k, 1024)], axis=1)


def _rope_tables(positions):
    inv = ROPE_THETA ** (-jnp.arange(0, 16, 2, dtype=f32) / 16)
    ang = positions.astype(f32).reshape(-1, 1) * inv
    cos, sin = jnp.cos(ang), jnp.sin(ang)
    t_len = ang.shape[0]
    cos64 = jnp.concatenate([cos, cos, jnp.ones((t_len, 48), f32)], axis=1)
    sin64 = jnp.concatenate([-sin, sin, jnp.zeros((t_len, 48), f32)], axis=1)
    return jnp.tile(cos64, (1, 4)), jnp.tile(sin64, (1, 4))


def kernel(x, positions, pre_norm, post_norm, ssm_w_in, ssm_conv_w, ssm_conv_b, ssm_dt_bias, ssm_a_log, ssm_d, ssm_gate_norm, ssm_w_out, att_w_in, att_sinks, att_w_out, loss_target, m_pre_norm, m_post_norm, m_ssm_w_in, m_ssm_conv_w, m_ssm_conv_b, m_ssm_dt_bias, m_ssm_a_log, m_ssm_d, m_ssm_gate_norm, m_ssm_w_out, m_att_w_in, m_att_sinks, m_att_w_out, v_pre_norm, v_post_norm, v_ssm_w_in, v_ssm_conv_w, v_ssm_conv_b, v_ssm_dt_bias, v_ssm_a_log, v_ssm_d, v_ssm_gate_norm, v_ssm_w_out, v_att_w_in, v_att_sinks, v_att_w_out):
    t_len = x.shape[1]
    tm = min(512, t_len)
    xin = x.reshape(t_len, D_MODEL)
    tgt = loss_target.reshape(t_len, D_MODEL)
    cidx = lax.axis_index("c").astype(jnp.int32).reshape(1)

    w_local = _pack_big(ssm_w_in, ssm_w_out, att_w_in, att_w_out)
    gathered = _all_gather_big(w_local.astype(MXU_DTYPE))
    conv_local = jnp.concatenate([ssm_conv_w.reshape(4, 1024), jnp.zeros((4, 1024), f32)], axis=0)
    conv_all = _all_gather_direct(conv_local, "all_gather_conv")[:, 0:4]
    o = 0
    g_ssm_in = gathered[:, o:o + ROWS_SSM_IN].reshape(8, 2, 1024, 772); o += ROWS_SSM_IN
    g_ssm_out = gathered[:, o:o + ROWS_SSM_OUT].reshape(8, 2, 256, 1024); o += ROWS_SSM_OUT
    g_att_in = gathered[:, o:o + ROWS_ATT_IN].reshape(8, 2, 1024, 320); o += ROWS_ATT_IN
    g_att_out = gathered[:, o:o + ROWS_ATT_OUT].reshape(8, 2, 128, 1024)
    w_ssm_in = jnp.transpose(g_ssm_in, (1, 2, 0, 3)).reshape(2, 1024, SSM_IN)
    w_ssm_out = jnp.transpose(g_ssm_out, (1, 0, 2, 3)).reshape(2, SSM_INNER, 1024)
    w_att_in = jnp.transpose(g_att_in, (1, 2, 0, 3)).reshape(2, 1024, ATT_IN)
    w_att_out = jnp.transpose(g_att_out, (1, 0, 2, 3)).reshape(2, 1024, 1024)
    conv_w = jnp.transpose(conv_all.reshape(8, 2, 4, 512), (1, 2, 0, 3)).reshape(2, 4, 4096)
    cos_t, sin_t = _rope_tables(positions)

    saved = []
    xc = xin
    for i in range(4):
        j = i // 2
        wn_pre, wn_post = pre_norm[i].reshape(1, D_MODEL), post_norm[i].reshape(1, D_MODEL)
        if i % 2 == 0:
            w_in = _ssm_w_in_tiles(w_ssm_in[j])
            cw, cb = _conv_tiles(conv_w[j]), _conv_tiles(ssm_conv_b[j].reshape(1, 4096))
            dtb, alog, dsk = ssm_dt_bias[j].reshape(1, 32), ssm_a_log[j].reshape(1, 32), ssm_d[j].reshape(1, 32)
            gn = ssm_gate_norm[j].reshape(SSM_KT, 1, TILE)
            u, h = _mm_in(xc, wn_pre, w_in, 5, tm, f"ssm_in_{j}")
            xbc = _conv_fwd(u, cw, cb, f"ssm_conv_{j}")
            a3, yp, hst = _ssm2_fwd(u, xbc, dtb, alog, dsk, gn, f"ssm_core_{j}")
            y, xn = _mm_out(a3, w_ssm_out[j], xc, wn_post, 4, tm, f"ssm_out_{j}")
            saved.append(dict(x=xc, u=u, h=h, a=a3, yp=yp, hst=hst, y=y, w_in=w_in, cw=cw, cb=cb, dtb=dtb, alog=alog,
                              dsk=dsk, gn=gn, xbc=xbc))
        else:
            sinks = att_sinks[j].reshape(1, 16)
            u, h = _mm_in(xc, wn_pre, w_att_in[j], 5, tm, f"att_in_{j}")
            a = _att_fwd(u, cos_t, sin_t, sinks, f"att_core_{j}")
            y, xn = _mm_out(a, w_att_out[j], xc, wn_post, 4, tm, f"att_out_{j}")
            saved.append(dict(x=xc, u=u, h=h, a=a, y=y, sinks=sinks))
        xc = xn

    dx, loss_part = _loss_grad(xc, tgt, tm)

    d_pre, d_post = [None] * 4, [None] * 4
    d_ssm_in, d_ssm_out, d_att_in, d_att_out = [None] * 2, [None] * 2, [None] * 2, [None] * 2
    d_cw, d_cb, d_dtb, d_alog, d_dsk, d_gn, d_sinks = ([None] * 2 for _ in range(7))
    for i in reversed(range(4)):
        j = i // 2
        s = saved[i]
        wn_pre, wn_post = pre_norm[i].reshape(1, D_MODEL), post_norm[i].reshape(1, D_MODEL)
        if i % 2 == 0:
            da3, dy, d_post[i] = _mm_dout(s["y"], dx, wn_post, w_ssm_out[j].T, 4, tm, f"ssm_dout_{j}")
            d_ssm_out[j] = _dw_rows(s["a"], dy, 4, tm, f"ssm_dwout_{j}")
            du, d_dtb[j], d_alog[j], d_dsk[j], dgn = _ssm2_bwd(
                s["u"], s["xbc"], s["yp"], s["hst"], da3, s["dtb"], s["alog"], s["dsk"], s["gn"],
                f"ssm_core_bwd_{j}")
            du, dcw, dcb = _conv_bwd(s["u"], du, s["cw"], s["cb"], f"ssm_conv_bwd_{j}")
            d_cw[j], d_cb[j], d_gn[j] = _conv_untile(dcw), _conv_untile(dcb), dgn.reshape(1, SSM_INNER)
            d_ssm_in[j] = _ssm_w_in_untile(_dw_cols(s["h"], du, 5, tm, f"ssm_dwin_{j}"))
            dx, d_pre[i] = _mm_dh(du, s["w_in"].T, s["x"], dx, wn_pre, 5, tm, f"ssm_dh_{j}")
        else:
            da, dy, d_post[i] = _mm_dout(s["y"], dx, wn_post, w_att_out[j].T, 4, tm, f"att_dout_{j}")
            d_att_out[j] = _dw_rows(s["a"], dy, 4, tm, f"att_dwout_{j}")
            du, d_sinks[j] = _att_bwd(s["u"], cos_t, sin_t, s["sinks"], da, f"att_core_bwd_{j}")
            d_att_in[j] = _dw_cols(s["h"], du, 5, tm, f"att_dwin_{j}")
            dx, d_pre[i] = _mm_dh(du, w_att_in[j].T, s["x"], dx, wn_pre, 5, tm, f"att_dh_{j}")

    g2 = _pack_grads(jnp.stack(d_ssm_in), jnp.stack(d_ssm_out), jnp.stack(d_att_in), jnp.stack(d_att_out),
                     jnp.stack(d_cw))
    r1 = _exchange_sibling(g2)
    pair = _pair_sum(g2, r1, cidx, ROWS_BIG // 10)
    parts = _exchange_chips(pair)
    w_p = _pack_big(ssm_w_in, ssm_w_out, att_w_in, att_w_out, ssm_conv_w)
    m_p = _pack_big(m_ssm_w_in, m_ssm_w_out, m_att_w_in, m_att_w_out, m_ssm_conv_w)
    v_p = _pack_big(v_ssm_w_in, v_ssm_w_out, v_att_w_in, v_att_w_out, v_ssm_conv_w)
    big = [_unpack_big(t) for t in _adamw(parts, w_p, m_p, v_p, ROWS_BIG // 10, "adamw_big")]

    small_local = _pack_small(jnp.concatenate(d_pre, axis=0), jnp.concatenate(d_post, axis=0),
                              jnp.concatenate(d_cb, axis=0), jnp.concatenate(d_gn, axis=0),
                              jnp.concatenate(d_dtb, axis=0), jnp.concatenate(d_alog, axis=0),
                              jnp.concatenate(d_dsk, axis=0), jnp.concatenate(d_sinks, axis=0), loss_part[0, 0])
    small_all = _all_gather_direct(small_local, "all_gather_small")
    ws = _pack_small(pre_norm, post_norm, ssm_conv_b, ssm_gate_norm, ssm_dt_bias, ssm_a_log, ssm_d, att_sinks)
    ms = _pack_small(m_pre_norm, m_post_norm, m_ssm_conv_b, m_ssm_gate_norm, m_ssm_dt_bias, m_ssm_a_log, m_ssm_d,
                     m_att_sinks)
    vs = _pack_small(v_pre_norm, v_post_norm, v_ssm_conv_b, v_ssm_gate_norm, v_ssm_dt_bias, v_ssm_a_log, v_ssm_d,
                     v_att_sinks)
    small4 = _adamw(small_all, ws, ms, vs, ROWS_SMALL, "adamw_small")
    loss = small4[0][32, 224]
    small = [_unpack_small(t) for t in small4]

    outs = [loss, dx.reshape(1, t_len, D_MODEL)]
    for k in range(4):
        b_ssm_in, b_ssm_out, b_att_in, b_att_out, b_conv = big[k]
        s_pre, s_post, s_cb, s_dtb, s_alog, s_d, s_gn, s_sinks = small[k]
        outs += [s_pre, s_post, b_ssm_in, b_conv, s_cb, s_dtb, s_alog, s_d, s_gn, b_ssm_out, b_att_in, s_sinks,
                 b_att_out]
    return tuple(outs)
```

```python
import functools

import jax
import jax.numpy as jnp
from jax import lax
from jax.experimental import pallas as pl
from jax.experimental.pallas import tpu as pltpu

f32 = jnp.float32
MXU_DTYPE = jnp.bfloat16
SDS = jax.ShapeDtypeStruct
MESH = pl.DeviceIdType.MESH

D_MODEL = 1024
EPS = 1e-6
TILE = 256
CHUNK = 128
SSM_HEADS = 32
SSM_GROUPS = 8
SSM_P = 64
SSM_N = 128
SSM_INNER = 2048
SSM_IN = 6176
SSM_NT = 25
SSM_KT = 8
ATT_NT = 10
ATT_KT = 4
ATT_IN = 2560
ROPE_THETA = 500000.0
N_DEV = 8
VMEM_LIMIT = 56 * 1024 * 1024

ADAM_LR = 0.001
ADAM_B1 = 0.9
ADAM_B2 = 0.999
ADAM_EPS = 1e-08
ADAM_WD = 0.01
ADAM_STEP = 10

ROWS_SSM_IN = 2 * 1024 * 772 // 1024
ROWS_SSM_OUT = 2 * 256
ROWS_ATT_IN = 2 * 1024 * 320 // 1024
ROWS_ATT_OUT = 2 * 128
ROWS_CONV = 8
ROWS_BIG = ROWS_SSM_IN + ROWS_SSM_OUT + ROWS_ATT_IN + ROWS_ATT_OUT + ROWS_CONV
ROWS_SMALL = 40


def _pc(body, name, **kw):
    return pl.pallas_call(body, name=name, **kw)


def _params(sem):
    return pltpu.CompilerParams(dimension_semantics=sem, vmem_limit_bytes=VMEM_LIMIT)


def _sigmoid(x):
    return 1.0 / (1.0 + jnp.exp(-x))


def _silu(x):
    return x * _sigmoid(x)


def _softplus(x):
    return jnp.maximum(x, 0.0) + jnp.log(1.0 + jnp.exp(-jnp.abs(x)))


def _mx(x):
    return x.astype(MXU_DTYPE)


def _dot(a, b):
    return jnp.dot(_mx(a), _mx(b), preferred_element_type=f32)


def _dot_nt(a, b):
    return lax.dot_general(_mx(a), _mx(b), (((1,), (1,)), ((), ())), preferred_element_type=f32)


def _dot_tn(a, b):
    return lax.dot_general(_mx(a), _mx(b), (((0,), (0,)), ((), ())), preferred_element_type=f32)


def _rms_fwd(x, w):
    r = lax.rsqrt(jnp.mean(x * x, axis=-1, keepdims=True) + EPS)
    return x * r * w


def _rms_bwd(x, w, dy):
    r = lax.rsqrt(jnp.mean(x * x, axis=-1, keepdims=True) + EPS)
    xh = x * r
    dw = jnp.sum(dy * xh, axis=0, keepdims=True)
    g = dy * w
    dx = r * (g - xh * jnp.mean(g * xh, axis=-1, keepdims=True))
    return dx, dw


def _mm_in(x, wn, w, ntb, tm, name):
    t_len, d = x.shape
    nt = w.shape[1] // TILE

    def body(x_ref, wn_ref, w_ref, u_ref, h_ref):
        @pl.when(pl.program_id(1) == 0)
        def _():
            h_ref[...] = _rms_fwd(x_ref[...], wn_ref[...]).astype(h_ref.dtype)
        h = h_ref[...]
        for t in range(ntb):
            u_ref[t] = jnp.dot(h, w_ref[:, TILE * t:TILE * (t + 1)], preferred_element_type=f32)

    return _pc(
        body, name, grid=(t_len // tm, nt // ntb),
        in_specs=[pl.BlockSpec((tm, d), lambda i, j: (i, 0)),
                  pl.BlockSpec((1, d), lambda i, j: (0, 0)),
                  pl.BlockSpec((d, ntb * TILE), lambda i, j: (0, j))],
        out_specs=[pl.BlockSpec((ntb, tm, TILE), lambda i, j: (j, i, 0)),
                   pl.BlockSpec((tm, d), lambda i, j: (i, 0))],
        out_shape=[SDS((nt, t_len, TILE), f32), SDS((t_len, d), MXU_DTYPE)],
        compiler_params=_params(("arbitrary", "arbitrary")),
    )(x, wn, w)


def _mm_dout(y, dxn, wn, wt, ntb, tm, name):
    t_len, d = y.shape
    nt = wt.shape[1] // TILE

    def body(y_ref, dxn_ref, wn_ref, w_ref, da_ref, dy_ref, dwn_ref):
        i, j = pl.program_id(0), pl.program_id(1)

        @pl.when((i == 0) & (j == 0))
        def _():
            dwn_ref[...] = jnp.zeros_like(dwn_ref)

        @pl.when(j == 0)
        def _():
            dy, dw = _rms_bwd(y_ref[...], wn_ref[...], dxn_ref[...])
            dy_ref[...] = dy.astype(dy_ref.dtype)
            dwn_ref[...] += dw
        dy = dy_ref[...]
        for t in range(ntb):
            da_ref[t] = jnp.dot(dy, w_ref[:, TILE * t:TILE * (t + 1)], preferred_element_type=f32)

    return _pc(
        body, name, grid=(t_len // tm, nt // ntb),
        in_specs=[pl.BlockSpec((tm, d), lambda i, j: (i, 0)),
                  pl.BlockSpec((tm, d), lambda i, j: (i, 0)),
                  pl.BlockSpec((1, d), lambda i, j: (0, 0)),
                  pl.BlockSpec((d, ntb * TILE), lambda i, j: (0, j))],
        out_specs=[pl.BlockSpec((ntb, tm, TILE), lambda i, j: (j, i, 0)),
                   pl.BlockSpec((tm, d), lambda i, j: (i, 0)),
                   pl.BlockSpec((1, d), lambda i, j: (0, 0))],
        out_shape=[SDS((nt, t_len, TILE), f32), SDS((t_len, d), MXU_DTYPE), SDS((1, d), f32)],
        compiler_params=_params(("arbitrary", "arbitrary")),
    )(y, dxn, wn, wt)


def _mm_out(a, w, x, wn, ktb, tm, name):
    kt, t_len, _ = a.shape
    d = w.shape[1]
    nk = kt // ktb

    def body(a_ref, w_ref, x_ref, wn_ref, y_ref, xn_ref, acc):
        k = pl.program_id(1)

        @pl.when(k == 0)
        def _():
            acc[...] = jnp.zeros_like(acc)
        s = acc[...]
        for t in range(ktb):
            s = s + jnp.dot(a_ref[t], w_ref[TILE * t:TILE * (t + 1), :], preferred_element_type=f32)
        acc[...] = s

        @pl.when(k == nk - 1)
        def _():
            y = acc[...]
            y_ref[...] = y
            xn_ref[...] = x_ref[...] + _rms_fwd(y, wn_ref[...])

    return _pc(
        body, name, grid=(t_len // tm, nk),
        in_specs=[pl.BlockSpec((ktb, tm, TILE), lambda i, k: (k, i, 0)),
                  pl.BlockSpec((ktb * TILE, d), lambda i, k: (k, 0)),
                  pl.BlockSpec((tm, d), lambda i, k: (i, 0)),
                  pl.BlockSpec((1, d), lambda i, k: (0, 0))],
        out_specs=[pl.BlockSpec((tm, d), lambda i, k: (i, 0)),
                   pl.BlockSpec((tm, d), lambda i, k: (i, 0))],
        out_shape=[SDS((t_len, d), f32), SDS((t_len, d), f32)],
        scratch_shapes=[pltpu.VMEM((tm, d), f32)],
        compiler_params=_params(("arbitrary", "arbitrary")),
    )(a, w, x, wn)


def _mm_dh(du, wt, x, dxn, wn, ktb, tm, name):
    kt, t_len, _ = du.shape
    d = wt.shape[1]
    nk = kt // ktb

    def body(du_ref, w_ref, x_ref, dxn_ref, wn_ref, dx_ref, dwn_ref, acc):
        i, k = pl.program_id(0), pl.program_id(1)

        @pl.when((i == 0) & (k == 0))
        def _():
            dwn_ref[...] = jnp.zeros_like(dwn_ref)

        @pl.when(k == 0)
        def _():
            acc[...] = jnp.zeros_like(acc)
        s = acc[...]
        for t in range(ktb):
            s = s + jnp.dot(_mx(du_ref[t]), w_ref[TILE * t:TILE * (t + 1), :], preferred_element_type=f32)
        acc[...] = s

        @pl.when(k == nk - 1)
        def _():
            dxp, dw = _rms_bwd(x_ref[...], wn_ref[...], acc[...])
            dx_ref[...] = dxn_ref[...] + dxp
            dwn_ref[...] += dw

    return _pc(
        body, name, grid=(t_len // tm, nk),
        in_specs=[pl.BlockSpec((ktb, tm, TILE), lambda i, k: (k, i, 0)),
                  pl.BlockSpec((ktb * TILE, d), lambda i, k: (k, 0)),
                  pl.BlockSpec((tm, d), lambda i, k: (i, 0)),
                  pl.BlockSpec((tm, d), lambda i, k: (i, 0)),
                  pl.BlockSpec((1, d), lambda i, k: (0, 0))],
        out_specs=[pl.BlockSpec((tm, d), lambda i, k: (i, 0)),
                   pl.BlockSpec((1, d), lambda i, k: (0, 0))],
        out_shape=[SDS((t_len, d), f32), SDS((1, d), f32)],
        scratch_shapes=[pltpu.VMEM((tm, d), f32)],
        compiler_params=_params(("arbitrary", "arbitrary")),
    )(du, wt, x, dxn, wn)


def _dw_cols(a, b, ntb, tk, name):
    t_len, kdim = a.shape
    nt = b.shape[0]

    def body(a_ref, b_ref, o_ref):
        @pl.when(pl.program_id(1) == 0)
        def _():
            o_ref[...] = jnp.zeros_like(o_ref)
        av = a_ref[...]
        for s in range(ntb):
            o_ref[:, TILE * s:TILE * (s + 1)] += _dot_tn(av, b_ref[s])

    return _pc(
        body, name, grid=(nt // ntb, t_len // tk),
        in_specs=[pl.BlockSpec((tk, kdim), lambda j, t: (t, 0)),
                  pl.BlockSpec((ntb, tk, TILE), lambda j, t: (j, t, 0))],
        out_specs=pl.BlockSpec((kdim, ntb * TILE), lambda j, t: (0, j)),
        out_shape=SDS((kdim, nt * TILE), f32),
        compiler_params=_params(("arbitrary", "arbitrary")),
    )(a, b)


def _dw_rows(a, b, ktb, tk, name):
    kt, t_len, _ = a.shape
    d = b.shape[1]

    def body(a_ref, b_ref, o_ref):
        @pl.when(pl.program_id(1) == 0)
        def _():
            o_ref[...] = jnp.zeros_like(o_ref)
        bv = b_ref[...]
        for s in range(ktb):
            o_ref[TILE * s:TILE * (s + 1), :] += _dot_tn(a_ref[s], bv)

    return _pc(
        body, name, grid=(kt // ktb, t_len // tk),
        in_specs=[pl.BlockSpec((ktb, tk, TILE), lambda k, t: (k, t, 0)),
                  pl.BlockSpec((tk, d), lambda k, t: (t, 0))],
        out_specs=pl.BlockSpec((ktb * TILE, d), lambda k, t: (k, 0)),
        out_shape=SDS((kt * TILE, d), f32),
        compiler_params=_params(("arbitrary", "arbitrary")),
    )(a, b)


def _loss_grad(x, tgt, tm):
    t_len, d = x.shape

    def body(x_ref, t_ref, dx_ref, l_ref):
        @pl.when(pl.program_id(0) == 0)
        def _():
            l_ref[...] = jnp.zeros_like(l_ref)
        e = x_ref[...] - t_ref[...]
        dx_ref[...] = e * (1.0 / d)
        row = jnp.mean(e * e, axis=-1, keepdims=True)
        l_ref[...] += 0.5 * jnp.sum(row, axis=0, keepdims=True)

    return _pc(
        body, "loss_grad", grid=(t_len // tm,),
        in_specs=[pl.BlockSpec((tm, d), lambda i: (i, 0)), pl.BlockSpec((tm, d), lambda i: (i, 0))],
        out_specs=[pl.BlockSpec((tm, d), lambda i: (i, 0)), pl.BlockSpec((1, 128), lambda i: (0, 0))],
        out_shape=[SDS((t_len, d), f32), SDS((1, 128), f32)],
        compiler_params=_params(("arbitrary",)),
    )(x, tgt)


def _tri(lower):
    r = lax.broadcasted_iota(jnp.int32, (CHUNK, CHUNK), 0)
    c = lax.broadcasted_iota(jnp.int32, (CHUNK, CHUNK), 1)
    return ((c <= r) if lower else (c >= r)).astype(f32)


def _dot_hi(a, b, dims):
    return lax.dot_general(a, b, (dims, ((), ())), precision=lax.Precision.HIGHEST, preferred_element_type=f32)


def _split(x, n):
    parts = []
    for _ in range(n):
        p = x.astype(jnp.bfloat16)
        parts.append(p)
        x = x - p.astype(f32)
    return parts


def _dot_exact(a, b, dims, split_a, n=3):
    out = None
    if split_a:
        b = b.astype(jnp.bfloat16)
        for p in _split(a, n):
            t = lax.dot_general(p, b, (dims, ((), ())), preferred_element_type=f32)
            out = t if out is None else out + t
    else:
        a = a.astype(jnp.bfloat16)
        for p in _split(b, n):
            t = lax.dot_general(a, p, (dims, ((), ())), preferred_element_type=f32)
            out = t if out is None else out + t
    return out


def _dt_path(dt_raw, dtb, alog):
    dtr = dt_raw + dtb
    dt = _softplus(dtr)
    a_neg = -jnp.exp(alog)
    a = dt * a_neg
    acs = _dot_exact(_tri(True), a, ((1,), (0,)), False)
    acs_t = _dot_exact(a, _tri(False), ((0,), (0,)), True)
    return dtr, dt, a_neg, acs, acs_t


def _conv_silu(x0, x1, x2, x3, w0, w1, w2, w3, b):
    acc = b + w0 * x0 + w1 * x1 + w2 * x2 + w3 * x3
    return _silu(acc)


def _ssd_group(xs, bm, cm, dtc, ac, ar, dh, hp):
    row = lax.broadcasted_iota(jnp.int32, (CHUNK, CHUNK), 0)
    col = lax.broadcasted_iota(jnp.int32, (CHUNK, CHUNK), 1)
    causal = col <= row
    last = (lax.broadcasted_iota(jnp.int32, (1, CHUNK), 1) == CHUNK - 1).astype(f32)
    cb = _dot_nt(cm, bm)
    ys, hns = [], []
    for r in range(4):
        xt = xs[r] * dtc[r]
        decay = jnp.exp(jnp.where(causal, ac[r] - ar[r], -jnp.inf))
        y_diag = _dot(cb * decay, xt)
        a_end = jnp.sum(ar[r] * last, axis=1, keepdims=True)
        y_off = _dot_nt(cm, hp[r]) * jnp.exp(ac[r])
        st = _dot_tn(xt * jnp.exp(a_end - ac[r]), bm)
        hns.append(hp[r] * jnp.exp(a_end) + st)
        ys.append(y_diag + y_off + dh[r] * xs[r])
    return tuple(ys), tuple(hns)


def _head_cols(g, r, dt, acs, acs_t, dsk):
    lane = lax.broadcasted_iota(jnp.int32, (1, SSM_HEADS), 1)
    sub = lax.broadcasted_iota(jnp.int32, (SSM_HEADS, 1), 0)
    h = 4 * g + r
    oh_l = (lane == h).astype(f32)
    oh_s = (sub == h).astype(f32)
    dtc = jnp.sum(dt * oh_l, axis=1, keepdims=True)
    ac = jnp.sum(acs * oh_l, axis=1, keepdims=True)
    ar = jnp.sum(acs_t * oh_s, axis=0, keepdims=True)
    dh = jnp.sum(dsk * oh_l, axis=1, keepdims=True)
    return h, oh_l, oh_s, dtc, ac, ar, dh


def _ssm_in_specs(nc, rev):
    def cidx(c):
        return (nc - 1 - c) if rev else c
    return [
        pl.BlockSpec((SSM_NT, CHUNK, TILE), lambda c: (0, cidx(c), 0)),
        pl.BlockSpec((SSM_NT, 8, TILE), lambda c: (0, jnp.maximum(cidx(c) * (CHUNK // 8) - 1, 0), 0)),
        pl.BlockSpec((16, 4, TILE), lambda c: (0, 0, 0)),
        pl.BlockSpec((16, 1, TILE), lambda c: (0, 0, 0)),
        pl.BlockSpec((1, SSM_HEADS), lambda c: (0, 0)),
        pl.BlockSpec((1, SSM_HEADS), lambda c: (0, 0)),
        pl.BlockSpec((1, SSM_HEADS), lambda c: (0, 0)),
        pl.BlockSpec((SSM_KT, 1, TILE), lambda c: (0, 0, 0)),
    ]


def _ssm_fwd(u, cw, cb, dtb, alog, dsk, gn, name):
    t_len = u.shape[1]
    nc = t_len // CHUNK

    def body(u_ref, halo_ref, cw_ref, cb_ref, dtb_ref, alog_ref, dsk_ref, gn_ref,
             a3_ref, yp_ref, hst_ref, h_s, win_s, xs_s, bc_s):
        c = pl.program_id(0)

        @pl.when(c == 0)
        def _():
            h_s[...] = jnp.zeros_like(h_s)
        _, dt, _, acs, acs_t = _dt_path(u_ref[SSM_NT - 1, :, 0:SSM_HEADS], dtb_ref[...], alog_ref[...])
        dsk = dsk_ref[...]

        def conv_tile(j, p):
            win_s[0:8, :] = jnp.where(c > 0, halo_ref[j], 0.0)
            win_s[8:8 + CHUNK, :] = u_ref[j]
            return _conv_silu(win_s[5:5 + CHUNK, :], win_s[6:6 + CHUNK, :], win_s[7:7 + CHUNK, :],
                              win_s[8:8 + CHUNK, :], cw_ref[p, 0:1, :], cw_ref[p, 1:2, :], cw_ref[p, 2:3, :],
                              cw_ref[p, 3:4, :], cb_ref[p])

        def group(g, s1):
            xs_s[...] = conv_tile(8 + g, g)
            bc_s[...] = conv_tile(16 + g, 8 + g)
            bm, cm = bc_s[:, 0:SSM_N], bc_s[:, SSM_N:2 * SSM_N]
            hs, xs, dtc, ac, ar, dh, hp = [], [], [], [], [], [], []
            for r in range(4):
                h, _, _, dtc_r, ac_r, ar_r, dh_r = _head_cols(g, r, dt, acs, acs_t, dsk)
                hs.append(h); dtc.append(dtc_r); ac.append(ac_r); ar.append(ar_r); dh.append(dh_r)
                xs.append(xs_s[:, SSM_P * r:SSM_P * (r + 1)])
                hp.append(h_s[h])
            ys, hns = _ssd_group(xs, bm, cm, dtc, ac, ar, dh, hp)
            for r in range(4):
                hst_ref[0, hs[r]] = hp[r]
                h_s[hs[r]] = hns[r]
                yp_ref[g, :, SSM_P * r:SSM_P * (r + 1)] = ys[r]
            y2 = yp_ref[g] * _silu(u_ref[g])
            return s1 + jnp.sum(y2 * y2, axis=1, keepdims=True)

        s1 = lax.fori_loop(0, SSM_GROUPS, group, jnp.zeros((CHUNK, 1), f32))
        rinv = lax.rsqrt(s1 * (1.0 / SSM_INNER) + EPS)

        def gate(g, carry):
            y2 = yp_ref[g] * _silu(u_ref[g])
            a3_ref[g] = (y2 * rinv * gn_ref[g]).astype(a3_ref.dtype)
            return carry

        lax.fori_loop(0, SSM_GROUPS, gate, 0)

    return _pc(
        body, name, grid=(nc,),
        in_specs=_ssm_in_specs(nc, False),
        out_specs=[pl.BlockSpec((SSM_KT, CHUNK, TILE), lambda c: (0, c, 0)),
                   pl.BlockSpec((SSM_KT, CHUNK, TILE), lambda c: (0, c, 0)),
                   pl.BlockSpec((1, SSM_HEADS, SSM_P, SSM_N), lambda c: (c, 0, 0, 0))],
        out_shape=[SDS((SSM_KT, t_len, TILE), MXU_DTYPE), SDS((SSM_KT, t_len, TILE), f32),
                   SDS((nc, SSM_HEADS, SSM_P, SSM_N), f32)],
        scratch_shapes=[pltpu.VMEM((SSM_HEADS, SSM_P, SSM_N), f32), pltpu.VMEM((8 + CHUNK, TILE), f32),
                        pltpu.VMEM((CHUNK, TILE), f32), pltpu.VMEM((CHUNK, TILE), f32)],
        compiler_params=_params(("arbitrary",)),
    )(u, u, cw, cb, dtb, alog, dsk, gn)


def _ssm_bwd(u, yp, hst, da3, cw, cb, dtb, alog, dsk, gn, name):
    t_len = u.shape[1]
    nc = t_len // CHUNK

    def body(u_ref, halo_ref, cw_ref, cb_ref, dtb_ref, alog_ref, dsk_ref, gn_ref, yp_ref, hst_ref, da3_ref,
             du_ref, dcw_ref, dcb_ref, ddtb_ref, dalog_ref, ddsk_ref, dgn_ref,
             dh_s, carry_s, win_s, dwin_s, xs_s, bc_s, dy_s, dxs_s, dbc_s):
        step = pl.program_id(0)
        cc = nc - 1 - step

        @pl.when(step == 0)
        def _():
            dh_s[...] = jnp.zeros_like(dh_s)
            carry_s[...] = jnp.zeros_like(carry_s)
            dcw_ref[...] = jnp.zeros_like(dcw_ref)
            dcb_ref[...] = jnp.zeros_like(dcb_ref)
            ddtb_ref[...] = jnp.zeros_like(ddtb_ref)
            dalog_ref[...] = jnp.zeros_like(dalog_ref)
            ddsk_ref[...] = jnp.zeros_like(ddsk_ref)
            dgn_ref[...] = jnp.zeros_like(dgn_ref)
        dtr, dt, a_neg, acs, acs_t = _dt_path(u_ref[SSM_NT - 1, :, 0:SSM_HEADS], dtb_ref[...], alog_ref[...])
        dsk = dsk_ref[...]

        def sums(g, carry):
            s1, s2 = carry
            y2 = yp_ref[g] * _silu(u_ref[g])
            g3 = da3_ref[g] * gn_ref[g]
            return (s1 + jnp.sum(y2 * y2, axis=1, keepdims=True), s2 + jnp.sum(g3 * y2, axis=1, keepdims=True))

        zcol = jnp.zeros((CHUNK, 1), f32)
        s1, s2 = lax.fori_loop(0, SSM_GROUPS, sums, (zcol, zcol))
        rinv = lax.rsqrt(s1 * (1.0 / SSM_INNER) + EPS)
        m2 = s2 * rinv * rinv * rinv * (1.0 / SSM_INNER)

        def windows(j):
            win_s[0:8, :] = jnp.where(cc > 0, halo_ref[j], 0.0)
            win_s[8:8 + CHUNK, :] = u_ref[j]
            return [win_s[5 + k:5 + k + CHUNK, :] for k in range(4)]

        def taps(p):
            return [cw_ref[p, k:k + 1, :] for k in range(4)]

        def conv_bwd(j, p, vjp_fn, dout):
            dx0, dx1, dx2, dx3, dw0, dw1, dw2, dw3, db = vjp_fn(dout)
            dwin_s[0:CHUNK, :] = jnp.zeros((CHUNK, TILE), f32)
            dwin_s[CHUNK:CHUNK + 8, :] = carry_s[j - 8]
            for k, dxk in enumerate((dx0, dx1, dx2, dx3)):
                dwin_s[5 + k:5 + k + CHUNK, :] += dxk
            du_ref[j] = dwin_s[8:8 + CHUNK, :]
            carry_s[j - 8] = dwin_s[0:8, :]
            for k, dwk in enumerate((dw0, dw1, dw2, dw3)):
                dcw_ref[p, k:k + 1, :] += dwk
            dcb_ref[p] += db

        def group(g, carry):
            ddt, dacs, dacs_t, ddsk = carry
            z = u_ref[g]
            sg = _sigmoid(z)
            sz = z * sg
            y = yp_ref[g]
            y2 = y * sz
            da3 = da3_ref[g]
            dgn_ref[g] += jnp.sum(da3 * y2 * rinv, axis=0, keepdims=True)
            dy2 = rinv * (da3 * gn_ref[g]) - y2 * m2
            dy_s[...] = dy2 * sz
            du_ref[g] = dy2 * y * (sg * (1.0 + z * (1.0 - sg)))
            xs_v, vjp_xs = jax.vjp(_conv_silu, *windows(8 + g), *taps(g), cb_ref[g])
            xs_s[...] = xs_v
            bc_v, vjp_bc = jax.vjp(_conv_silu, *windows(16 + g), *taps(8 + g), cb_ref[8 + g])
            bc_s[...] = bc_v
            bm, cm = bc_s[:, 0:SSM_N], bc_s[:, SSM_N:2 * SSM_N]
            hs, ohl, ohs, xs, dtc, ac, ar, dh, hp, dys, dhn = [], [], [], [], [], [], [], [], [], [], []
            for r in range(4):
                h, oh_l, oh_s, dtc_r, ac_r, ar_r, dh_r = _head_cols(g, r, dt, acs, acs_t, dsk)
                hs.append(h); ohl.append(oh_l); ohs.append(oh_s)
                dtc.append(dtc_r); ac.append(ac_r); ar.append(ar_r); dh.append(dh_r)
                xs.append(xs_s[:, SSM_P * r:SSM_P * (r + 1)])
                hp.append(hst_ref[0, h])
                dys.append(dy_s[:, SSM_P * r:SSM_P * (r + 1)])
                dhn.append(dh_s[h])
            _, vjp_g = jax.vjp(_ssd_group, xs, bm, cm, dtc, ac, ar, dh, hp)
            dxs, dbm, dcm, ddtc, dac, dar, ddh, dhp = vjp_g((tuple(dys), tuple(dhn)))
            for r in range(4):
                dxs_s[:, SSM_P * r:SSM_P * (r + 1)] = dxs[r]
                dh_s[hs[r]] = dhp[r]
                ddt = ddt + ddtc[r] * ohl[r]
                dacs = dacs + dac[r] * ohl[r]
                dacs_t = dacs_t + ohs[r] * dar[r]
                ddsk = ddsk + ddh[r] * ohl[r]
            dbc_s[:, 0:SSM_N] = dbm
            dbc_s[:, SSM_N:2 * SSM_N] = dcm
            conv_bwd(8 + g, g, vjp_xs, dxs_s[...])
            conv_bwd(16 + g, 8 + g, vjp_bc, dbc_s[...])
            return ddt, dacs, dacs_t, ddsk

        init = (jnp.zeros((CHUNK, SSM_HEADS), f32), jnp.zeros((CHUNK, SSM_HEADS), f32),
                jnp.zeros((SSM_HEADS, CHUNK), f32), jnp.zeros((1, SSM_HEADS), f32))
        ddt, dacs, dacs_t, ddsk = lax.fori_loop(0, SSM_GROUPS, group, init)
        upper = _tri(False)
        da = _dot_hi(upper, dacs, ((1,), (0,))) + _dot_hi(upper, dacs_t, ((1,), (1,)))
        ddt = ddt + da * a_neg
        dalog_ref[...] += jnp.sum(da * dt, axis=0, keepdims=True) * a_neg
        ddtr = ddt * _sigmoid(dtr)
        ddtb_ref[...] += jnp.sum(ddtr, axis=0, keepdims=True)
        ddsk_ref[...] += ddsk
        du_ref[SSM_NT - 1] = jnp.zeros((CHUNK, TILE), f32)
        du_ref[SSM_NT - 1, :, 0:SSM_HEADS] = ddtr

    def rc(c):
        return nc - 1 - c

    small = [pl.BlockSpec((16, 4, TILE), lambda c: (0, 0, 0)),
             pl.BlockSpec((16, 1, TILE), lambda c: (0, 0, 0)),
             pl.BlockSpec((1, SSM_HEADS), lambda c: (0, 0)),
             pl.BlockSpec((1, SSM_HEADS), lambda c: (0, 0)),
             pl.BlockSpec((1, SSM_HEADS), lambda c: (0, 0)),
             pl.BlockSpec((SSM_KT, 1, TILE), lambda c: (0, 0, 0))]
    return _pc(
        body, name, grid=(nc,),
        in_specs=_ssm_in_specs(nc, True) + [
            pl.BlockSpec((SSM_KT, CHUNK, TILE), lambda c: (0, rc(c), 0)),
            pl.BlockSpec((1, SSM_HEADS, SSM_P, SSM_N), lambda c: (rc(c), 0, 0, 0)),
            pl.BlockSpec((SSM_KT, CHUNK, TILE), lambda c: (0, rc(c), 0))],
        out_specs=[pl.BlockSpec((SSM_NT, CHUNK, TILE), lambda c: (0, rc(c), 0))] + small,
        out_shape=[SDS((SSM_NT, t_len, TILE), f32), SDS((16, 4, TILE), f32), SDS((16, 1, TILE), f32),
                   SDS((1, SSM_HEADS), f32), SDS((1, SSM_HEADS), f32), SDS((1, SSM_HEADS), f32),
                   SDS((SSM_KT, 1, TILE), f32)],
        scratch_shapes=[pltpu.VMEM((SSM_HEADS, SSM_P, SSM_N), f32), pltpu.VMEM((16, 8, TILE), f32),
                        pltpu.VMEM((8 + CHUNK, TILE), f32), pltpu.VMEM((8 + CHUNK, TILE), f32)]
                       + [pltpu.VMEM((CHUNK, TILE), f32)] * 5,
        compiler_params=_params(("arbitrary",)),
    )(u, u, cw, cb, dtb, alog, dsk, gn, yp, hst, da3)


CONV_ROWS = 1024
CONV_SUB = 32


def _conv_specs(nb, rows, rev):
    def ridx(i):
        return (nb - 1 - i) if rev else i
    return [
        pl.BlockSpec((1, rows, TILE), lambda p, i: (8 + p, ridx(i), 0)),
        pl.BlockSpec((1, 8, TILE), lambda p, i: (8 + p, jnp.maximum(ridx(i) * (rows // 8) - 1, 0), 0)),
        pl.BlockSpec((1, 4, TILE), lambda p, i: (p, 0, 0)),
        pl.BlockSpec((1, 1, TILE), lambda p, i: (p, 0, 0)),
    ]


def _conv_fwd(u, cw, cb, name):
    t_len = u.shape[1]
    rows = min(CONV_ROWS, t_len)
    nb = t_len // rows

    def body(u_ref, halo_ref, cw_ref, cb_ref, o_ref, win_s):
        i = pl.program_id(1)
        win_s[0:8, :] = jnp.where(i > 0, halo_ref[0], 0.0)
        win_s[8:8 + rows, :] = u_ref[0]
        w = [cw_ref[0, k:k + 1, :] for k in range(4)]
        b = cb_ref[0]
        for s in range(rows // CONV_SUB):
            o = CONV_SUB * s
            acc = b
            for k in range(4):
                acc = acc + w[k] * win_s[5 + k + o:5 + k + o + CONV_SUB, :]
            o_ref[0, o:o + CONV_SUB, :] = _silu(acc)

    return _pc(
        body, name, grid=(16, nb),
        in_specs=_conv_specs(nb, rows, False),
        out_specs=pl.BlockSpec((1, rows, TILE), lambda p, i: (p, i, 0)),
        out_shape=SDS((16, t_len, TILE), f32),
        scratch_shapes=[pltpu.VMEM((8 + rows, TILE), f32)],
        compiler_params=_params(("arbitrary", "arbitrary")),
    )(u, u, cw, cb)


def _conv_bwd(u, du, cw, cb, name):
    t_len = u.shape[1]
    rows = min(CONV_ROWS, t_len)
    nb = t_len // rows

    def body(u_ref, halo_ref, cw_ref, cb_ref, d_ref, o_ref, dcw_ref, dcb_ref, carry_s, win_s, dp_s):
        i = pl.program_id(1)
        ri = nb - 1 - i

        @pl.when(i == 0)
        def _():
            carry_s[...] = jnp.zeros_like(carry_s)
            dcw_ref[...] = jnp.zeros_like(dcw_ref)
            dcb_ref[...] = jnp.zeros_like(dcb_ref)
        win_s[0:8, :] = jnp.where(ri > 0, halo_ref[0], 0.0)
        win_s[8:8 + rows, :] = u_ref[0]
        w = [cw_ref[0, k:k + 1, :] for k in range(4)]
        b = cb_ref[0]
        dw = [jnp.zeros((1, TILE), f32)] * 4
        db = jnp.zeros((1, TILE), f32)
        for s in range(rows // CONV_SUB):
            o = CONV_SUB * s
            xk = [win_s[5 + k + o:5 + k + o + CONV_SUB, :] for k in range(4)]
            pre = b
            for k in range(4):
                pre = pre + w[k] * xk[k]
            sg = _sigmoid(pre)
            dpre = d_ref[0, o:o + CONV_SUB, :] * (sg * (1.0 + pre * (1.0 - sg)))
            dp_s[o:o + CONV_SUB, :] = dpre
            dw = [dw[k] + jnp.sum(dpre * xk[k], axis=0, keepdims=True) for k in range(4)]
            db = db + jnp.sum(dpre, axis=0, keepdims=True)
        dp_s[rows:rows + 8, :] = carry_s[...]
        for s in range(rows // CONV_SUB):
            o = CONV_SUB * s
            acc = w[0] * dp_s[3 + o:3 + o + CONV_SUB, :]
            for k in range(1, 4):
                acc = acc + w[k] * dp_s[3 - k + o:3 - k + o + CONV_SUB, :]
            o_ref[0, o:o + CONV_SUB, :] = acc
        carry_s[...] = dp_s[0:8, :]
        for k in range(4):
            dcw_ref[0, k:k + 1, :] += dw[k]
        dcb_ref[0] += db

    return _pc(
        body, name, grid=(16, nb),
        in_specs=_conv_specs(nb, rows, True) + [pl.BlockSpec((1, rows, TILE), lambda p, i: (8 + p, nb - 1 - i, 0))],
        out_specs=[pl.BlockSpec((1, rows, TILE), lambda p, i: (8 + p, nb - 1 - i, 0)),
                   pl.BlockSpec((1, 4, TILE), lambda p, i: (p, 0, 0)),
                   pl.BlockSpec((1, 1, TILE), lambda p, i: (p, 0, 0))],
        out_shape=[SDS(du.shape, f32), SDS((16, 4, TILE), f32), SDS((16, 1, TILE), f32)],
        input_output_aliases={4: 0},
        scratch_shapes=[pltpu.VMEM((8, TILE), f32), pltpu.VMEM((8 + rows, TILE), f32),
                        pltpu.VMEM((rows + 8, TILE), f32)],
        compiler_params=_params(("arbitrary", "arbitrary")),
    )(u, u, cw, cb, du)


def _collapse_matrix(g):
    r = lax.broadcasted_iota(jnp.int32, (SSM_HEADS, TILE), 0)
    c = lax.broadcasted_iota(jnp.int32, (SSM_HEADS, TILE), 1)
    return ((c // SSM_P) + 4 * g == r).astype(jnp.bfloat16)


def _ssd_prelude(dt_raw, dtb, alog, dsk, colx_s, scx_s):
    dtr, dt, a_neg, acs, acs_t = _dt_path(dt_raw, dtb, alog)
    a_end = acs[CHUNK - 1:CHUNK, :]
    lane = lax.broadcasted_iota(jnp.int32, (1, 2 * SSM_P), 1)
    lane4 = lax.broadcasted_iota(jnp.int32, (1, TILE), 1)
    sub8 = lax.broadcasted_iota(jnp.int32, (8, 1), 0)

    def row4(v, g):
        e = [v[:, 4 * g + r:4 * g + r + 1] for r in range(4)]
        return jnp.where(lane4 < 64, e[0], jnp.where(lane4 < 128, e[1], jnp.where(lane4 < 192, e[2], e[3])))

    for g in range(SSM_GROUPS):
        for k, arr in enumerate((dt, acs)):
            for half in range(2):
                h0 = 4 * g + 2 * half
                colx_s[k, g, :, 128 * half:128 * (half + 1)] = jnp.where(
                    lane < SSM_P, arr[:, h0:h0 + 1], arr[:, h0 + 1:h0 + 2])
        scx_s[g] = jnp.where(sub8 == 0, row4(dsk, g), jnp.where(sub8 == 1, row4(a_end, g), 0.0))
    return dtr, dt, a_neg, acs_t


def _ssm_core_specs(nc, rev):
    def cidx(c):
        return (nc - 1 - c) if rev else c
    return [
        pl.BlockSpec((SSM_KT, CHUNK, TILE), lambda c: (0, cidx(c), 0)),
        pl.BlockSpec((1, CHUNK, TILE), lambda c: (SSM_NT - 1, cidx(c), 0)),
        pl.BlockSpec((16, CHUNK, TILE), lambda c: (0, cidx(c), 0)),
        pl.BlockSpec((1, SSM_HEADS), lambda c: (0, 0)),
        pl.BlockSpec((1, SSM_HEADS), lambda c: (0, 0)),
        pl.BlockSpec((1, SSM_HEADS), lambda c: (0, 0)),
        pl.BlockSpec((SSM_KT, 1, TILE), lambda c: (0, 0, 0)),
    ]


def _causal():
    row = lax.broadcasted_iota(jnp.int32, (CHUNK, CHUNK), 0)
    col = lax.broadcasted_iota(jnp.int32, (CHUNK, CHUNK), 1)
    return col <= row


def _ssm2_fwd(u, xbc, dtb, alog, dsk, gn, name):
    t_len = u.shape[1]
    nc = t_len // CHUNK

    def body(z_ref, dt_ref, x_ref, dtb_ref, alog_ref, dsk_ref, gn_ref, a3_ref, yp_ref, hst_ref,
             h_s, colx_s, scx_s, xt_s, yd_s):
        c = pl.program_id(0)

        @pl.when(c == 0)
        def _():
            h_s[...] = jnp.zeros_like(h_s)
        _, _, _, acs_t = _ssd_prelude(dt_ref[0, :, 0:SSM_HEADS], dtb_ref[...], alog_ref[...], dsk_ref[...],
                                      colx_s, scx_s)
        causal = _causal()

        def group(g, s1):
            xs = x_ref[g]
            bm, cm = x_ref[8 + g, :, 0:SSM_N], x_ref[8 + g, :, SSM_N:2 * SSM_N]
            cb = _dot_nt(cm, bm)
            sc = scx_s[g]
            a_end = sc[1:2, :]
            rows = pltpu.roll(acs_t, (SSM_HEADS - 4 * g) % SSM_HEADS, 0)
            hp = h_s[g]
            xt = xs * colx_s[0, g]
            xt_s[...] = xt
            acx = colx_s[1, g]
            for r in range(4):
                hd = slice(SSM_P * r, SSM_P * (r + 1))
                lam = jnp.exp(jnp.where(causal, acx[:, SSM_P * r:SSM_P * r + 1] - rows[r:r + 1, :], -jnp.inf))
                yd_s[:, hd] = _dot(cb * lam, xt_s[:, hd])
            yp_ref[g] = yd_s[...] + _dot(cm, hp) * jnp.exp(acx) + sc[0:1, :] * xs
            hst_ref[0, g] = hp
            h_s[g] = hp * jnp.exp(a_end) + _dot_tn(bm, xt * jnp.exp(a_end - acx))
            y2 = yp_ref[g] * _silu(z_ref[g])
            return s1 + jnp.sum(y2 * y2, axis=1, keepdims=True)

        s1 = lax.fori_loop(0, SSM_GROUPS, group, jnp.zeros((CHUNK, 1), f32))
        rinv = lax.rsqrt(s1 * (1.0 / SSM_INNER) + EPS)

        def gate(g, carry):
            y2 = yp_ref[g] * _silu(z_ref[g])
            a3_ref[g] = (y2 * rinv * gn_ref[g]).astype(a3_ref.dtype)
            return carry

        lax.fori_loop(0, SSM_GROUPS, gate, 0)

    return _pc(
        body, name, grid=(nc,),
        in_specs=_ssm_core_specs(nc, False),
        out_specs=[pl.BlockSpec((SSM_KT, CHUNK, TILE), lambda c: (0, c, 0)),
                   pl.BlockSpec((SSM_KT, CHUNK, TILE), lambda c: (0, c, 0)),
                   pl.BlockSpec((1, SSM_GROUPS, SSM_N, TILE), lambda c: (c, 0, 0, 0))],
        out_shape=[SDS((SSM_KT, t_len, TILE), MXU_DTYPE), SDS((SSM_KT, t_len, TILE), f32),
                   SDS((nc, SSM_GROUPS, SSM_N, TILE), f32)],
        scratch_shapes=[pltpu.VMEM((SSM_GROUPS, SSM_N, TILE), f32), pltpu.VMEM((2, SSM_GROUPS, CHUNK, TILE), f32),
                        pltpu.VMEM((SSM_GROUPS, 8, TILE), f32), pltpu.VMEM((CHUNK, TILE), f32),
                        pltpu.VMEM((CHUNK, TILE), f32)],
        compiler_params=_params(("arbitrary",)),
    )(u, u, xbc, dtb, alog, dsk, gn)


def _ssm2_bwd(u, xbc, yp, hst, da3, dtb, alog, dsk, gn, name):
    t_len = u.shape[1]
    nc = t_len // CHUNK

    def body(z_ref, dt_ref, x_ref, dtb_ref, alog_ref, dsk_ref, gn_ref, yp_ref, hst_ref, da3_ref,
             du_ref, ddtb_ref, dalog_ref, ddsk_ref, dgn_ref,
             dh_s, colx_s, scx_s, xt_s, dy_s, dxt_s, ddtx_s, dacx_s, ddx_s, drow_s):
        step = pl.program_id(0)

        @pl.when(step == 0)
        def _():
            dh_s[...] = jnp.zeros_like(dh_s)
            ddtb_ref[...] = jnp.zeros_like(ddtb_ref)
            dalog_ref[...] = jnp.zeros_like(dalog_ref)
            ddsk_ref[...] = jnp.zeros_like(ddsk_ref)
            dgn_ref[...] = jnp.zeros_like(dgn_ref)
        dtr, dt, a_neg, acs_t = _ssd_prelude(dt_ref[0, :, 0:SSM_HEADS], dtb_ref[...], alog_ref[...], dsk_ref[...],
                                             colx_s, scx_s)
        causal = _causal()
        causal_t = (lax.broadcasted_iota(jnp.int32, (CHUNK, CHUNK), 1)
                    >= lax.broadcasted_iota(jnp.int32, (CHUNK, CHUNK), 0))
        last = (lax.broadcasted_iota(jnp.int32, (1, CHUNK), 1) == CHUNK - 1).astype(f32)
        lane = lax.broadcasted_iota(jnp.int32, (1, TILE), 1)
        sub32 = lax.broadcasted_iota(jnp.int32, (SSM_HEADS, 1), 0)
        drow_s[...] = jnp.zeros_like(drow_s)

        def sums(g, carry):
            s1, s2 = carry
            y2 = yp_ref[g] * _silu(z_ref[g])
            g3 = da3_ref[g] * gn_ref[g]
            return (s1 + jnp.sum(y2 * y2, axis=1, keepdims=True), s2 + jnp.sum(g3 * y2, axis=1, keepdims=True))

        zcol = jnp.zeros((CHUNK, 1), f32)
        s1, s2 = lax.fori_loop(0, SSM_GROUPS, sums, (zcol, zcol))
        rinv = lax.rsqrt(s1 * (1.0 / SSM_INNER) + EPS)
        m2 = s2 * rinv * rinv * rinv * (1.0 / SSM_INNER)

        def group(g, carry):
            z = z_ref[g]
            sg = _sigmoid(z)
            sz = z * sg
            y = yp_ref[g]
            y2 = y * sz
            da3 = da3_ref[g]
            dgn_ref[g] += jnp.sum(da3 * y2 * rinv, axis=0, keepdims=True)
            dy2 = rinv * (da3 * gn_ref[g]) - y2 * m2
            dy = dy2 * sz
            dy_s[...] = dy
            du_ref[g] = dy2 * y * (sg * (1.0 + z * (1.0 - sg)))
            xs = x_ref[g]
            bm, cm = x_ref[8 + g, :, 0:SSM_N], x_ref[8 + g, :, SSM_N:2 * SSM_N]
            cb = _dot_nt(cm, bm)
            cbt = _dot_nt(bm, cm)
            dtx, acx = colx_s[0, g], colx_s[1, g]
            sc = scx_s[g]
            a_end = sc[1:2, :]
            ex = jnp.exp(acx)
            wdx = jnp.exp(a_end - acx)
            eend = jnp.exp(a_end)
            rows = pltpu.roll(acs_t, (SSM_HEADS - 4 * g) % SSM_HEADS, 0)
            hp = hst_ref[0, g]
            dhn = dh_s[g]
            xt = xs * dtx
            xt_s[...] = xt
            ch = _dot(cm, hp)
            gy = dy * ex
            dcm = _dot_nt(gy, hp)
            dh_s[g] = _dot_tn(cm, gy) + dhn * eend
            q = _dot(bm, dhn)
            dbm = _dot_nt(xt * wdx, dhn)
            qx = q * xt * wdx
            v_end = jnp.sum(dhn * hp, axis=0, keepdims=True) * eend + jnp.sum(qx, axis=0, keepdims=True)
            dcb = jnp.zeros((CHUNK, CHUNK), f32)
            drows = jnp.zeros((SSM_HEADS, CHUNK), f32)
            for r in range(4):
                hd = slice(SSM_P * r, SSM_P * (r + 1))
                in_head = (lane >= SSM_P * r) & (lane < SSM_P * (r + 1))
                ac = acx[:, SSM_P * r:SSM_P * r + 1]
                ar = rows[r:r + 1, :]
                lam = jnp.exp(jnp.where(causal, ac - ar, -jnp.inf))
                lam_t = jnp.exp(jnp.where(causal_t, ar - ac, -jnp.inf))
                m = cb * lam
                m_t = cbt * lam_t
                dm = _dot_nt(dy_s[:, hd], xt_s[:, hd])
                dm_t = _dot_nt(xt_s[:, hd], dy_s[:, hd])
                dxt_s[:, hd] = _dot(m_t, dy_s[:, hd])
                dcb = dcb + dm * lam
                d_ac = jnp.sum(dm_t * m_t, axis=0, keepdims=True)
                d_ar = jnp.sum(dm * m, axis=0, keepdims=True)
                d_aend = jnp.sum(jnp.where(in_head, v_end, 0.0), axis=1, keepdims=True)
                drows = drows + jnp.where(sub32 == r, d_ac - d_ar + last * d_aend, 0.0)
            dxt = dxt_s[...] + q * wdx
            du_ref[8 + g] = sc[0:1, :] * dy + dxt * dtx
            ddtx_s[g] = dxt * xs
            dacx_s[g] = dy * ch * ex - qx
            ddx_s[g] = jnp.broadcast_to(jnp.sum(dy * xs, axis=0, keepdims=True), (8, TILE))
            du_ref[16 + g, :, 0:SSM_N] = dbm + _dot_tn(dcb, cm)
            du_ref[16 + g, :, SSM_N:2 * SSM_N] = dcm + _dot(dcb, bm)
            drow_s[...] += pltpu.roll(drows, (4 * g) % SSM_HEADS, 0)
            return carry

        lax.fori_loop(0, SSM_GROUPS, group, 0)
        ddt = jnp.zeros((CHUNK, SSM_HEADS), f32)
        dacs = jnp.zeros((CHUNK, SSM_HEADS), f32)
        ddsk = jnp.zeros((8, SSM_HEADS), f32)
        for g in range(SSM_GROUPS):
            col_g = _collapse_matrix(g)
            ddt = ddt + _dot_exact(ddtx_s[g], col_g, ((1,), (1,)), True, 2)
            dacs = dacs + _dot_exact(dacx_s[g], col_g, ((1,), (1,)), True, 2)
            ddsk = ddsk + _dot_exact(ddx_s[g], col_g, ((1,), (1,)), True, 2)
        upper = _tri(False)
        da = _dot_exact(upper, dacs, ((1,), (0,)), False) + _dot_exact(upper, drow_s[...], ((1,), (1,)), False)
        ddt = ddt + da * a_neg
        dalog_ref[...] += jnp.sum(da * dt, axis=0, keepdims=True) * a_neg
        ddtr = ddt * _sigmoid(dtr)
        ddtb_ref[...] += jnp.sum(ddtr, axis=0, keepdims=True)
        ddsk_ref[...] += ddsk[0:1, :]
        du_ref[SSM_NT - 1] = jnp.zeros((CHUNK, TILE), f32)
        du_ref[SSM_NT - 1, :, 0:SSM_HEADS] = ddtr

    def rc(c):
        return nc - 1 - c

    vec = pl.BlockSpec((1, SSM_HEADS), lambda c: (0, 0))
    return _pc(
        body, name, grid=(nc,),
        in_specs=_ssm_core_specs(nc, True) + [
            pl.BlockSpec((SSM_KT, CHUNK, TILE), lambda c: (0, rc(c), 0)),
            pl.BlockSpec((1, SSM_GROUPS, SSM_N, TILE), lambda c: (rc(c), 0, 0, 0)),
            pl.BlockSpec((SSM_KT, CHUNK, TILE), lambda c: (0, rc(c), 0))],
        out_specs=[pl.BlockSpec((SSM_NT, CHUNK, TILE), lambda c: (0, rc(c), 0)), vec, vec, vec,
                   pl.BlockSpec((SSM_KT, 1, TILE), lambda c: (0, 0, 0))],
        out_shape=[SDS((SSM_NT, t_len, TILE), f32), SDS((1, SSM_HEADS), f32), SDS((1, SSM_HEADS), f32),
                   SDS((1, SSM_HEADS), f32), SDS((SSM_KT, 1, TILE), f32)],
        scratch_shapes=[pltpu.VMEM((SSM_GROUPS, SSM_N, TILE), f32), pltpu.VMEM((2, SSM_GROUPS, CHUNK, TILE), f32),
                        pltpu.VMEM((SSM_GROUPS, 8, TILE), f32), pltpu.VMEM((CHUNK, TILE), f32),
                        pltpu.VMEM((CHUNK, TILE), f32), pltpu.VMEM((CHUNK, TILE), f32),
                        pltpu.VMEM((SSM_GROUPS, CHUNK, TILE), f32), pltpu.VMEM((SSM_GROUPS, CHUNK, TILE), f32),
                        pltpu.VMEM((SSM_GROUPS, 8, TILE), f32), pltpu.VMEM((SSM_HEADS, CHUNK), f32)],
        compiler_params=_params(("arbitrary",)),
    )(u, u, xbc, dtb, alog, dsk, gn, yp, hst, da3)


def _swap16(t):
    lane = lax.broadcasted_iota(jnp.int32, t.shape, 1) % 64
    return jnp.where(lane < 8, pltpu.roll(t, TILE - 8, 1), jnp.where(lane < 16, pltpu.roll(t, 8, 1), 0.0))


def _rope(t, cos_t, sin_t):
    return t * cos_t + _swap16(t) * sin_t


def _rope_bwd(g, cos_t, sin_t):
    return g * cos_t + _swap16(g * sin_t)


def _att_head(qh, kp, kc, vp, vc, sink, mask_p, mask_c):
    sp = jnp.where(mask_p, _dot_nt(qh, kp) * 0.125, -jnp.inf)
    sc = jnp.where(mask_c, _dot_nt(qh, kc) * 0.125, -jnp.inf)
    m = jnp.maximum(jnp.maximum(jnp.max(sp, axis=-1, keepdims=True), jnp.max(sc, axis=-1, keepdims=True)), sink)
    m = lax.stop_gradient(m)
    pp = jnp.exp(sp - m)
    pc = jnp.exp(sc - m)
    den = jnp.sum(pp, axis=-1, keepdims=True) + jnp.sum(pc, axis=-1, keepdims=True) + jnp.exp(sink - m)
    inv = 1.0 / den
    return _dot(pp * inv, vp) + _dot(pc * inv, vc)


def _att_masks(has_prev):
    row = lax.broadcasted_iota(jnp.int32, (CHUNK, CHUNK), 0)
    col = lax.broadcasted_iota(jnp.int32, (CHUNK, CHUNK), 1)
    return (col > row) & has_prev, col <= row


def _att_in_specs(nb, rev):
    def bidx(n):
        return (nb - 1 - n) if rev else n
    return [
        pl.BlockSpec((ATT_NT, CHUNK, TILE), lambda n: (0, bidx(n), 0)),
        pl.BlockSpec((2, CHUNK, TILE), lambda n: (2, jnp.maximum(bidx(n) - 1, 0), 0)),
        pl.BlockSpec((CHUNK, TILE), lambda n: (bidx(n), 0)),
        pl.BlockSpec((CHUNK, TILE), lambda n: (bidx(n), 0)),
        pl.BlockSpec((CHUNK, TILE), lambda n: (jnp.maximum(bidx(n) - 1, 0), 0)),
        pl.BlockSpec((CHUNK, TILE), lambda n: (jnp.maximum(bidx(n) - 1, 0), 0)),
        pl.BlockSpec((1, 16), lambda n: (0, 0)),
    ]


def _att_fwd(u, cos_t, sin_t, sinks, name):
    t_len = u.shape[1]
    nb = t_len // CHUNK

    def body(u_ref, prev_ref, cc_ref, sc_ref, cp_ref, sp_ref, sink_ref, a_ref,
             q_s, kp_s, kc_s, vp_s, vc_s, o_s):
        n = pl.program_id(0)
        mask_p, mask_c = _att_masks(n > 0)
        cos_c, sin_c = cc_ref[...], sc_ref[...]
        kc_s[...] = _rope(u_ref[4], cos_c, sin_c)
        kp_s[...] = _rope(prev_ref[0], cp_ref[...], sp_ref[...])
        vc_s[...] = u_ref[5]
        vp_s[...] = prev_ref[1]
        sinks = sink_ref[...]
        for g in range(4):
            q_s[...] = _rope(u_ref[g], cos_c, sin_c)
            kv = slice(64 * g, 64 * (g + 1))
            for r in range(4):
                hd = slice(64 * r, 64 * (r + 1))
                h = 4 * g + r
                o_s[:, hd] = _att_head(q_s[:, hd], kp_s[:, kv], kc_s[:, kv], vp_s[:, kv], vc_s[:, kv],
                                       sinks[:, h:h + 1], mask_p, mask_c)
            a_ref[g] = (o_s[...] * _silu(u_ref[6 + g])).astype(a_ref.dtype)

    return _pc(
        body, name, grid=(nb,),
        in_specs=_att_in_specs(nb, False),
        out_specs=pl.BlockSpec((ATT_KT, CHUNK, TILE), lambda n: (0, n, 0)),
        out_shape=SDS((ATT_KT, t_len, TILE), MXU_DTYPE),
        scratch_shapes=[pltpu.VMEM((CHUNK, TILE), f32)] * 6,
        compiler_params=_params(("arbitrary",)),
    )(u, u, cos_t, sin_t, cos_t, sin_t, sinks)


def _att_bwd(u, cos_t, sin_t, sinks, da, name):
    t_len = u.shape[1]
    nb = t_len // CHUNK

    def body(u_ref, prev_ref, cc_ref, sc_ref, cp_ref, sp_ref, sink_ref, da_ref, du_ref, dsink_ref,
             ck_s, cv_s, q_s, kp_s, kc_s, vp_s, vc_s, o_s, do_s, dq_s, dkp_s, dkc_s, dvp_s, dvc_s):
        step = pl.program_id(0)
        nn = nb - 1 - step

        @pl.when(step == 0)
        def _():
            ck_s[...] = jnp.zeros_like(ck_s)
            cv_s[...] = jnp.zeros_like(cv_s)
            dsink_ref[...] = jnp.zeros_like(dsink_ref)
        mask_p, mask_c = _att_masks(nn > 0)
        cos_c, sin_c = cc_ref[...], sc_ref[...]
        cos_p, sin_p = cp_ref[...], sp_ref[...]
        kc_s[...] = _rope(u_ref[4], cos_c, sin_c)
        kp_s[...] = _rope(prev_ref[0], cos_p, sin_p)
        vc_s[...] = u_ref[5]
        vp_s[...] = prev_ref[1]
        sinks = sink_ref[...]
        lane16 = lax.broadcasted_iota(jnp.int32, (1, 16), 1)
        dsink = jnp.zeros((1, 16), f32)
        att = functools.partial(_att_head, mask_p=mask_p, mask_c=mask_c)
        for g in range(4):
            q_s[...] = _rope(u_ref[g], cos_c, sin_c)
            gate = u_ref[6 + g]
            sg = _sigmoid(gate)
            dav = da_ref[g]
            do_s[...] = dav * (gate * sg)
            kv = slice(64 * g, 64 * (g + 1))
            dkp = jnp.zeros((CHUNK, 64), f32)
            dkc = jnp.zeros((CHUNK, 64), f32)
            dvp = jnp.zeros((CHUNK, 64), f32)
            dvc = jnp.zeros((CHUNK, 64), f32)
            for r in range(4):
                hd = slice(64 * r, 64 * (r + 1))
                h = 4 * g + r
                o_h, vjp_fn = jax.vjp(att, q_s[:, hd], kp_s[:, kv], kc_s[:, kv], vp_s[:, kv], vc_s[:, kv],
                                      sinks[:, h:h + 1])
                dq_h, dkp_h, dkc_h, dvp_h, dvc_h, ds_h = vjp_fn(do_s[:, hd])
                o_s[:, hd] = o_h
                dq_s[:, hd] = dq_h
                dkp, dkc, dvp, dvc = dkp + dkp_h, dkc + dkc_h, dvp + dvp_h, dvc + dvc_h
                dsink = dsink + ds_h * (lane16 == h).astype(f32)
            du_ref[6 + g] = dav * o_s[...] * (sg * (1.0 + gate * (1.0 - sg)))
            du_ref[g] = _rope_bwd(dq_s[...], cos_c, sin_c)
            dkp_s[:, kv] = dkp
            dkc_s[:, kv] = dkc
            dvp_s[:, kv] = dvp
            dvc_s[:, kv] = dvc
        du_ref[4] = _rope_bwd(dkc_s[...], cos_c, sin_c) + ck_s[...]
        du_ref[5] = dvc_s[...] + cv_s[...]
        ck_s[...] = _rope_bwd(dkp_s[...], cos_p, sin_p)
        cv_s[...] = dvp_s[...]
        dsink_ref[...] += dsink

    def rb(n):
        return nb - 1 - n

    return _pc(
        body, name, grid=(nb,),
        in_specs=_att_in_specs(nb, True) + [pl.BlockSpec((ATT_KT, CHUNK, TILE), lambda n: (0, rb(n), 0))],
        out_specs=[pl.BlockSpec((ATT_NT, CHUNK, TILE), lambda n: (0, rb(n), 0)),
                   pl.BlockSpec((1, 16), lambda n: (0, 0))],
        out_shape=[SDS((ATT_NT, t_len, TILE), f32), SDS((1, 16), f32)],
        scratch_shapes=[pltpu.VMEM((CHUNK, TILE), f32)] * 14,
        compiler_params=_params(("arbitrary",)),
    )(u, u, cos_t, sin_t, cos_t, sin_t, sinks, da)


ATT_SCALE = 0.125
ATT_ROWS = 4 * CHUNK


def _band_mask(has_prev):
    row = lax.broadcasted_iota(jnp.int32, (CHUNK, 2 * CHUNK), 0)
    col = lax.broadcasted_iota(jnp.int32, (CHUNK, 2 * CHUNK), 1)
    return ((col < CHUNK) & (col > row) & has_prev) | ((col >= CHUNK) & (col - CHUNK <= row))


def _stack_heads(ref):
    return jnp.concatenate([ref[:, 64 * r:64 * (r + 1)] for r in range(4)], axis=0)


def _sink_col(sinks, g):
    return [sinks[:, 4 * g + r:4 * g + r + 1] for r in range(4)]


def _softmax_rows(s_s, pn_s, sink, mask):
    ps = []
    for r in range(4):
        rs = slice(CHUNK * r, CHUNK * (r + 1))
        sk = sink[r]
        s = jnp.where(mask, s_s[rs, :] * ATT_SCALE, -jnp.inf)
        m = jnp.maximum(jnp.max(s, axis=-1, keepdims=True), sk)
        p = jnp.exp(s - m)
        e_sink = jnp.exp(sk - m)
        inv = 1.0 / (jnp.sum(p, axis=-1, keepdims=True) + e_sink)
        pn_s[rs, :] = p * inv
        ps.append(e_sink * inv)
    return ps


def _att2_fwd(u, cos_t, sin_t, sinks, name):
    t_len = u.shape[1]
    nb = t_len // CHUNK

    def body(u_ref, prev_ref, cc_ref, sc_ref, cp_ref, sp_ref, sink_ref, a_ref,
             q_s, kp_s, kc_s, vp_s, vc_s, o_s, s_s, pn_s):
        n = pl.program_id(0)
        mask = _band_mask(n > 0)
        cos_c, sin_c = cc_ref[...], sc_ref[...]
        kc_s[...] = _rope(u_ref[4], cos_c, sin_c)
        kp_s[...] = _rope(prev_ref[0], cp_ref[...], sp_ref[...])
        vc_s[...] = u_ref[5]
        vp_s[...] = prev_ref[1]
        sinks = sink_ref[...]
        for g in range(4):
            q_s[...] = _rope(u_ref[g], cos_c, sin_c)
            kv = slice(64 * g, 64 * (g + 1))
            kb = jnp.concatenate([kp_s[:, kv], kc_s[:, kv]], axis=0)
            vb = jnp.concatenate([vp_s[:, kv], vc_s[:, kv]], axis=0)
            s_s[...] = _dot_nt(_stack_heads(q_s), kb)
            _softmax_rows(s_s, pn_s, _sink_col(sinks, g), mask)
            o = _dot(pn_s[...], vb)
            for r in range(4):
                o_s[:, 64 * r:64 * (r + 1)] = o[CHUNK * r:CHUNK * (r + 1), :]
            a_ref[g] = (o_s[...] * _silu(u_ref[6 + g])).astype(a_ref.dtype)

    return _pc(
        body, name, grid=(nb,),
        in_specs=_att_in_specs(nb, False),
        out_specs=pl.BlockSpec((ATT_KT, CHUNK, TILE), lambda n: (0, n, 0)),
        out_shape=SDS((ATT_KT, t_len, TILE), MXU_DTYPE),
        scratch_shapes=[pltpu.VMEM((CHUNK, TILE), f32)] * 6 + [pltpu.VMEM((ATT_ROWS, 2 * CHUNK), f32)] * 2,
        compiler_params=_params(("arbitrary",)),
    )(u, u, cos_t, sin_t, cos_t, sin_t, sinks)


def _att2_bwd(u, cos_t, sin_t, sinks, da, name):
    t_len = u.shape[1]
    nb = t_len // CHUNK

    def body(u_ref, prev_ref, cc_ref, sc_ref, cp_ref, sp_ref, sink_ref, da_ref, du_ref, dsink_ref,
             ck_s, cv_s, q_s, kp_s, kc_s, vp_s, vc_s, o_s, do_s, dq_s, s_s, pn_s, dkt_s, dvt_s):
        step = pl.program_id(0)
        nn = nb - 1 - step

        @pl.when(step == 0)
        def _():
            ck_s[...] = jnp.zeros_like(ck_s)
            cv_s[...] = jnp.zeros_like(cv_s)
            dsink_ref[...] = jnp.zeros_like(dsink_ref)
        mask = _band_mask(nn > 0)
        cos_c, sin_c = cc_ref[...], sc_ref[...]
        cos_p, sin_p = cp_ref[...], sp_ref[...]
        kc_s[...] = _rope(u_ref[4], cos_c, sin_c)
        kp_s[...] = _rope(prev_ref[0], cos_p, sin_p)
        vc_s[...] = u_ref[5]
        vp_s[...] = prev_ref[1]
        sinks = sink_ref[...]
        lane16 = lax.broadcasted_iota(jnp.int32, (1, 16), 1)
        dsink = jnp.zeros((1, 16), f32)
        for g in range(4):
            q_s[...] = _rope(u_ref[g], cos_c, sin_c)
            gate = u_ref[6 + g]
            sg = _sigmoid(gate)
            dav = da_ref[g]
            do_s[...] = dav * (gate * sg)
            kv = slice(64 * g, 64 * (g + 1))
            kb = jnp.concatenate([kp_s[:, kv], kc_s[:, kv]], axis=0)
            vb = jnp.concatenate([vp_s[:, kv], vc_s[:, kv]], axis=0)
            q_st = _stack_heads(q_s)
            do_st = _stack_heads(do_s)
            s_s[...] = _dot_nt(q_st, kb)
            p_sink = _softmax_rows(s_s, pn_s, _sink_col(sinks, g), mask)
            o = _dot(pn_s[...], vb)
            dvt_s[64 * g:64 * (g + 1), :] = _dot_tn(do_st, pn_s[...])
            s_s[...] = _dot_nt(do_st, vb)
            for r in range(4):
                rs = slice(CHUNK * r, CHUNK * (r + 1))
                o_s[:, 64 * r:64 * (r + 1)] = o[rs, :]
                delta = jnp.sum(do_st[rs, :] * o[rs, :], axis=-1, keepdims=True)
                s_s[rs, :] = pn_s[rs, :] * (s_s[rs, :] - delta) * ATT_SCALE
                ds_h = -jnp.sum(p_sink[r] * delta, axis=0, keepdims=True)
                dsink = dsink + ds_h * (lane16 == 4 * g + r).astype(f32)
            ds = s_s[...]
            dq = _dot(ds, kb)
            for r in range(4):
                dq_s[:, 64 * r:64 * (r + 1)] = dq[CHUNK * r:CHUNK * (r + 1), :]
            dkt_s[64 * g:64 * (g + 1), :] = _dot_tn(q_st, ds)
            du_ref[6 + g] = dav * o_s[...] * (sg * (1.0 + gate * (1.0 - sg)))
            du_ref[g] = _rope_bwd(dq_s[...], cos_c, sin_c)
        dk = dkt_s[...].T
        dv = dvt_s[...].T
        du_ref[4] = _rope_bwd(dk[CHUNK:2 * CHUNK, :], cos_c, sin_c) + ck_s[...]
        du_ref[5] = dv[CHUNK:2 * CHUNK, :] + cv_s[...]
        ck_s[...] = _rope_bwd(dk[0:CHUNK, :], cos_p, sin_p)
        cv_s[...] = dv[0:CHUNK, :]
        dsink_ref[...] += dsink

    def rb(n):
        return nb - 1 - n

    return _pc(
        body, name, grid=(nb,),
        in_specs=_att_in_specs(nb, True) + [pl.BlockSpec((ATT_KT, CHUNK, TILE), lambda n: (0, rb(n), 0))],
        out_specs=[pl.BlockSpec((ATT_NT, CHUNK, TILE), lambda n: (0, rb(n), 0)),
                   pl.BlockSpec((1, 16), lambda n: (0, 0))],
        out_shape=[SDS((ATT_NT, t_len, TILE), f32), SDS((1, 16), f32)],
        scratch_shapes=[pltpu.VMEM((CHUNK, TILE), f32)] * 10 + [pltpu.VMEM((ATT_ROWS, 2 * CHUNK), f32)] * 2
                       + [pltpu.VMEM((2 * CHUNK, 2 * CHUNK), f32)] * 2,
        compiler_params=_params(("arbitrary",)),
    )(u, u, cos_t, sin_t, cos_t, sin_t, sinks, da)


_HBM = pl.BlockSpec(memory_space=pltpu.HBM)


def _all_gather_big(shard):
    rows, width = shard.shape

    def body(x_ref, out_ref, send_sems, recv_sems, local_sem):
        x, y, c = lax.axis_index("x"), lax.axis_index("y"), lax.axis_index("c")
        me, sibling = (x, y, c), (x, y, 1 - c)
        chips = [(1 - x, y), (x, 1 - y), (1 - x, 1 - y)]

        def slot(px, py, pc):
            return out_ref.at[4 * px + 2 * py + pc]

        def copy(k, block, to, src=None):
            return pltpu.make_async_remote_copy(
                src_ref=slot(*block) if src is None else src, dst_ref=slot(*block),
                send_sem=send_sems.at[k], recv_sem=recv_sems.at[k], device_id=to, device_id_type=MESH)

        mine = pltpu.make_async_copy(x_ref, slot(*me), local_sem)
        mine.start()
        first = [copy(0, me, sibling, src=x_ref)]
        first += [copy(1 + j, me, (*chip, c), src=x_ref) for j, chip in enumerate(chips)]
        for cp in first:
            cp.start()
        passed = [copy(4 + j, (*chip, c), sibling) for j, chip in enumerate(chips)]
        for j, chip in enumerate(chips):
            copy(1 + j, (*chip, c), me).wait_recv()
            passed[j].start()
        copy(0, sibling, me).wait_recv()
        for j, chip in enumerate(chips):
            copy(4 + j, (*chip, 1 - c), me).wait_recv()
        for cp in first + passed:
            cp.wait_send()
        mine.wait()

    return _pc(
        body, "all_gather_big",
        in_specs=[_HBM], out_specs=_HBM,
        out_shape=SDS((N_DEV, rows, width), shard.dtype),
        scratch_shapes=[pltpu.SemaphoreType.DMA((7,)), pltpu.SemaphoreType.DMA((7,)), pltpu.SemaphoreType.DMA],
    )(shard)


def _all_gather_direct(block, name):
    rows, width = block.shape

    def body(x_ref, out_ref, send_sems, recv_sems, local_sem):
        x, y, c = lax.axis_index("x"), lax.axis_index("y"), lax.axis_index("c")
        my_slot = 4 * x + 2 * y + c

        def peer(k):
            return (1 - x if k & 4 else x, 1 - y if k & 2 else y, 1 - c if k & 1 else c)

        def copy(k):
            px, py, pc = peer(k)
            return pltpu.make_async_remote_copy(
                src_ref=x_ref, dst_ref=out_ref.at[my_slot], send_sem=send_sems.at[k - 1], recv_sem=recv_sems.at[k - 1],
                device_id=(px, py, pc), device_id_type=MESH)

        def arrival(k):
            px, py, pc = peer(k)
            return pltpu.make_async_remote_copy(
                src_ref=x_ref, dst_ref=out_ref.at[4 * px + 2 * py + pc], send_sem=send_sems.at[k - 1],
                recv_sem=recv_sems.at[k - 1], device_id=(px, py, pc), device_id_type=MESH)

        mine = pltpu.make_async_copy(x_ref, out_ref.at[my_slot], local_sem)
        mine.start()
        for k in range(1, N_DEV):
            copy(k).start()
        for k in range(1, N_DEV):
            arrival(k).wait_recv()
        for k in range(1, N_DEV):
            copy(k).wait_send()
        mine.wait()

    return _pc(
        body, name,
        in_specs=[_HBM], out_specs=_HBM,
        out_shape=SDS((N_DEV, rows, width), block.dtype),
        scratch_shapes=[pltpu.SemaphoreType.DMA((7,)), pltpu.SemaphoreType.DMA((7,)), pltpu.SemaphoreType.DMA],
    )(block)


def _exchange_sibling(g2):
    _, nchip, rows, width = g2.shape

    def body(g_ref, out_ref, send_sem, recv_sem):
        x, y, c = lax.axis_index("x"), lax.axis_index("y"), lax.axis_index("c")
        cp = pltpu.make_async_remote_copy(
            src_ref=g_ref.at[1 - c], dst_ref=out_ref, send_sem=send_sem, recv_sem=recv_sem,
            device_id=(x, y, 1 - c), device_id_type=MESH)
        cp.start()
        cp.wait()

    return _pc(
        body, "rs_sibling",
        in_specs=[_HBM], out_specs=_HBM,
        out_shape=SDS((nchip, rows, width), g2.dtype),
        scratch_shapes=[pltpu.SemaphoreType.DMA, pltpu.SemaphoreType.DMA],
    )(g2)


def _pair_sum(g2, r1, cidx, tr):
    _, nchip, rows, width = g2.shape

    def body(c_ref, g_ref, r_ref, o_ref):
        o_ref[...] = (g_ref[0].astype(f32) + r_ref[...].astype(f32)).astype(o_ref.dtype)

    return pl.pallas_call(
        body, name="rs_pair_sum",
        grid_spec=pltpu.PrefetchScalarGridSpec(
            num_scalar_prefetch=1, grid=(nchip, rows // tr),
            in_specs=[pl.BlockSpec((1, 1, tr, width), lambda k, i, c_ref: (c_ref[0], k, i, 0)),
                      pl.BlockSpec((1, tr, width), lambda k, i, c_ref: (k, i, 0))],
            out_specs=pl.BlockSpec((1, tr, width), lambda k, i, c_ref: (k, i, 0))),
        out_shape=SDS((nchip, rows, width), g2.dtype),
        compiler_params=_params(("arbitrary", "arbitrary")),
    )(cidx, g2, r1)


def _exchange_chips(p):
    nchip, rows, width = p.shape

    def body(p_ref, out_ref, send_sems, recv_sems, local_sem):
        x, y, c = lax.axis_index("x"), lax.axis_index("y"), lax.axis_index("c")
        my_chip = 2 * x + y
        chips = [(1 - x, y), (x, 1 - y), (1 - x, 1 - y)]

        def copy(j):
            px, py = chips[j]
            return pltpu.make_async_remote_copy(
                src_ref=p_ref.at[2 * px + py], dst_ref=out_ref.at[my_chip], send_sem=send_sems.at[j],
                recv_sem=recv_sems.at[j], device_id=(px, py, c), device_id_type=MESH)

        def arrival(j):
            px, py = chips[j]
            return pltpu.make_async_remote_copy(
                src_ref=p_ref.at[my_chip], dst_ref=out_ref.at[2 * px + py], send_sem=send_sems.at[j],
                recv_sem=recv_sems.at[j], device_id=(px, py, c), device_id_type=MESH)

        mine = pltpu.make_async_copy(p_ref.at[my_chip], out_ref.at[my_chip], local_sem)
        mine.start()
        for j in range(3):
            copy(j).start()
        for j in range(3):
            arrival(j).wait_recv()
        for j in range(3):
            copy(j).wait_send()
        mine.wait()

    return _pc(
        body, "rs_chips",
        in_specs=[_HBM], out_specs=_HBM,
        out_shape=SDS((nchip, rows, width), p.dtype),
        scratch_shapes=[pltpu.SemaphoreType.DMA((3,)), pltpu.SemaphoreType.DMA((3,)), pltpu.SemaphoreType.DMA],
    )(p)


def _adamw(parts, w, m, v, tr, name):
    n, rows, width = parts.shape
    c1 = 1.0 / (1.0 - ADAM_B1 ** ADAM_STEP)
    c2 = 1.0 / (1.0 - ADAM_B2 ** ADAM_STEP)

    def body(p_ref, w_ref, m_ref, v_ref, g_ref, d_ref, mo_ref, vo_ref):
        g = p_ref[0].astype(f32)
        for k in range(1, n):
            g = g + p_ref[k].astype(f32)
        mn = ADAM_B1 * m_ref[...] + (1.0 - ADAM_B1) * g
        vn = ADAM_B2 * v_ref[...] + (1.0 - ADAM_B2) * (g * g)
        g_ref[...] = g
        mo_ref[...] = mn
        vo_ref[...] = vn
        d_ref[...] = -ADAM_LR * ((mn * c1) / (jnp.sqrt(vn * c2) + ADAM_EPS) + ADAM_WD * w_ref[...])

    blk = pl.BlockSpec((tr, width), lambda i: (i, 0))
    return _pc(
        body, name, grid=(rows // tr,),
        in_specs=[pl.BlockSpec((n, tr, width), lambda i: (0, i, 0)), blk, blk, blk],
        out_specs=[blk, blk, blk, blk],
        out_shape=[SDS((rows, width), f32)] * 4,
        compiler_params=_params(("arbitrary",)),
    )(parts, w, m, v)


def _pack_big(ssm_w_in, ssm_w_out, att_w_in, att_w_out, conv_w=None):
    parts = [ssm_w_in.reshape(ROWS_SSM_IN, 1024), ssm_w_out.reshape(ROWS_SSM_OUT, 1024),
             att_w_in.reshape(ROWS_ATT_IN, 1024), att_w_out.reshape(ROWS_ATT_OUT, 1024)]
    if conv_w is None:
        parts.append(jnp.zeros((ROWS_CONV, 1024), f32))
    else:
        parts.append(jnp.concatenate([conv_w.reshape(4, 1024), jnp.zeros((4, 1024), f32)], axis=0))
    return jnp.concatenate(parts, axis=0)


def _unpack_big(p):
    o = 0
    out = []
    for rows, shape in ((ROWS_SSM_IN, (2, 1024, 772)), (ROWS_SSM_OUT, (2, 256, 1024)),
                        (ROWS_ATT_IN, (2, 1024, 320)), (ROWS_ATT_OUT, (2, 128, 1024))):
        out.append(p[o:o + rows].reshape(shape))
        o += rows
    out.append(p[o:o + 4].reshape(2, 4, 512))
    return out


def _by_class(a):
    return jnp.swapaxes(a.reshape((4, 2) + a.shape[1:]), 0, 1)


def _pack_grads(d_ssm_w_in, d_ssm_w_out, d_att_w_in, d_att_w_out, d_conv_w):
    wire = lambda t: t.astype(MXU_DTYPE)
    a = jnp.transpose(wire(d_ssm_w_in).reshape(2, 1024, 8, 772), (2, 0, 1, 3)).reshape(8, ROWS_SSM_IN, 1024)
    b = jnp.transpose(wire(d_ssm_w_out).reshape(2, 8, 256, 1024), (1, 0, 2, 3)).reshape(8, ROWS_SSM_OUT, 1024)
    c = jnp.transpose(wire(d_att_w_in).reshape(2, 1024, 8, 320), (2, 0, 1, 3)).reshape(8, ROWS_ATT_IN, 1024)
    d = jnp.transpose(wire(d_att_w_out).reshape(2, 8, 128, 1024), (1, 0, 2, 3)).reshape(8, ROWS_ATT_OUT, 1024)
    e = jnp.transpose(wire(d_conv_w).reshape(2, 4, 8, 512), (2, 0, 1, 3)).reshape(8, 4, 1024)
    e = jnp.concatenate([e, jnp.zeros((8, 4, 1024), MXU_DTYPE)], axis=1)
    return _by_class(jnp.concatenate([a, b, c, d, e], axis=1))


def _pad8(a):
    return jnp.pad(a, ((0, 8 - a.shape[0]), (0, 0)))


def _pack_small(pre_norm, post_norm, conv_b, gate_norm, dt_bias, a_log, d_skip, sinks, extra=None):
    row = jnp.concatenate([dt_bias.reshape(1, 64), a_log.reshape(1, 64), d_skip.reshape(1, 64), sinks.reshape(1, 32),
                           jnp.zeros((1, 1024 - 224), f32)], axis=1)
    if extra is not None:
        row = row + jnp.pad(extra.reshape(1, 1), ((0, 0), (224, 1024 - 225)))
    return jnp.concatenate([_pad8(pre_norm.reshape(4, 1024)), _pad8(post_norm.reshape(4, 1024)),
                            conv_b.reshape(8, 1024), _pad8(gate_norm.reshape(4, 1024)), _pad8(row)], axis=0)


def _unpack_small(p):
    row = p[32]
    return (p[0:4], p[8:12], p[16:24].reshape(2, 4096), row[0:64].reshape(2, 32), row[64:128].reshape(2, 32),
            row[128:192].reshape(2, 32), p[24:28].reshape(2, 2048), row[192:224].reshape(2, 16))


def _ssm_w_in_tiles(w):
    wb = w[:, 4096:5120].reshape(1024, 8, 128)
    wc = w[:, 5120:6144].reshape(1024, 8, 128)
    wbc = jnp.concatenate([wb, wc], axis=2).reshape(1024, 2048)
    return jnp.concatenate([w[:, 0:4096], wbc, w[:, 6144:6176], jnp.zeros((1024, 224), w.dtype)], axis=1)


def _ssm_w_in_untile(dw):
    dbc = dw[:, 4096:6144].reshape(1024, 8, 256)
    return jnp.concatenate([dw[:, 0:4096], dbc[:, :, 0:128].reshape(1024, 1024), dbc[:, :, 128:256].reshape(1024, 1024),
                            dw[:, 6144:6176]], axis=1)


def _conv_tiles(cw):
    k = cw.shape[0]
    xs = jnp.transpose(cw[:, 0:2048].reshape(k, 8, 256), (1, 0, 2))
    b = cw[:, 2048:3072].reshape(k, 8, 128)
    c = cw[:, 3072:4096].reshape(k, 8, 128)
    bc = jnp.transpose(jnp.concatenate([b, c], axis=2), (1, 0, 2))
    return jnp.concatenate([xs, bc], axis=0)


def _conv_untile(t):
    k = t.shape[1]
    xs = jnp.transpose(t[0:8], (1, 0, 2)).reshape(k, 2048)
    bc = jnp.transpose(t[8:16], (1, 0, 2))
    return jnp.concatenate([xs, bc[:, :, 0:128].reshape(k, 1024), bc[:, :, 128:256].reshape(k, 1024)], axis=1)


def _rope_tables(positions):
    inv = ROPE_THETA ** (-jnp.arange(0, 16, 2, dtype=f32) / 16)
    ang = positions.astype(f32).reshape(-1, 1) * inv
    cos, sin = jnp.cos(ang), jnp.sin(ang)
    t_len = ang.shape[0]
    cos64 = jnp.concatenate([cos, cos, jnp.ones((t_len, 48), f32)], axis=1)
    sin64 = jnp.concatenate([-sin, sin, jnp.zeros((t_len, 48), f32)], axis=1)
    return jnp.tile(cos64, (1, 4)), jnp.tile(sin64, (1, 4))


def kernel(x, positions, pre_norm, post_norm, ssm_w_in, ssm_conv_w, ssm_conv_b, ssm_dt_bias, ssm_a_log, ssm_d, ssm_gate_norm, ssm_w_out, att_w_in, att_sinks, att_w_out, loss_target, m_pre_norm, m_post_norm, m_ssm_w_in, m_ssm_conv_w, m_ssm_conv_b, m_ssm_dt_bias, m_ssm_a_log, m_ssm_d, m_ssm_gate_norm, m_ssm_w_out, m_att_w_in, m_att_sinks, m_att_w_out, v_pre_norm, v_post_norm, v_ssm_w_in, v_ssm_conv_w, v_ssm_conv_b, v_ssm_dt_bias, v_ssm_a_log, v_ssm_d, v_ssm_gate_norm, v_ssm_w_out, v_att_w_in, v_att_sinks, v_att_w_out):
    t_len = x.shape[1]
    tm = min(512, t_len)
    xin = x.reshape(t_len, D_MODEL)
    tgt = loss_target.reshape(t_len, D_MODEL)
    cidx = lax.axis_index("c").astype(jnp.int32).reshape(1)

    w_local = _pack_big(ssm_w_in, ssm_w_out, att_w_in, att_w_out)
    gathered = _all_gather_big(w_local.astype(MXU_DTYPE))
    conv_local = jnp.concatenate([ssm_conv_w.reshape(4, 1024), jnp.zeros((4, 1024), f32)], axis=0)
    conv_all = _all_gather_direct(conv_local, "all_gather_conv")[:, 0:4]
    o = 0
    g_ssm_in = gathered[:, o:o + ROWS_SSM_IN].reshape(8, 2, 1024, 772); o += ROWS_SSM_IN
    g_ssm_out = gathered[:, o:o + ROWS_SSM_OUT].reshape(8, 2, 256, 1024); o += ROWS_SSM_OUT
    g_att_in = gathered[:, o:o + ROWS_ATT_IN].reshape(8, 2, 1024, 320); o += ROWS_ATT_IN
    g_att_out = gathered[:, o:o + ROWS_ATT_OUT].reshape(8, 2, 128, 1024)
    w_ssm_in = jnp.transpose(g_ssm_in, (1, 2, 0, 3)).reshape(2, 1024, SSM_IN)
    w_ssm_out = jnp.transpose(g_ssm_out, (1, 0, 2, 3)).reshape(2, SSM_INNER, 1024)
    w_att_in = jnp.transpose(g_att_in, (1, 2, 0, 3)).reshape(2, 1024, ATT_IN)
    w_att_out = jnp.transpose(g_att_out, (1, 0, 2, 3)).reshape(2, 1024, 1024)
    conv_w = jnp.transpose(conv_all.reshape(8, 2, 4, 512), (1, 2, 0, 3)).reshape(2, 4, 4096)
    cos_t, sin_t = _rope_tables(positions)

    saved = []
    xc = xin
    for i in range(4):
        j = i // 2
        wn_pre, wn_post = pre_norm[i].reshape(1, D_MODEL), post_norm[i].reshape(1, D_MODEL)
        if i % 2 == 0:
            w_in = _ssm_w_in_tiles(w_ssm_in[j])
            cw, cb = _conv_tiles(conv_w[j]), _conv_tiles(ssm_conv_b[j].reshape(1, 4096))
            dtb, alog, dsk = ssm_dt_bias[j].reshape(1, 32), ssm_a_log[j].reshape(1, 32), ssm_d[j].reshape(1, 32)
            gn = ssm_gate_norm[j].reshape(SSM_KT, 1, TILE)
            u, h = _mm_in(xc, wn_pre, w_in, 5, tm, f"ssm_in_{j}")
            xbc = _conv_fwd(u, cw, cb, f"ssm_conv_{j}")
            a3, yp, hst = _ssm2_fwd(u, xbc, dtb, alog, dsk, gn, f"ssm_core_{j}")
            y, xn = _mm_out(a3, w_ssm_out[j], xc, wn_post, 4, tm, f"ssm_out_{j}")
            saved.append(dict(x=xc, u=u, h=h, a=a3, yp=yp, hst=hst, y=y, w_in=w_in, cw=cw, cb=cb, dtb=dtb, alog=alog,
                              dsk=dsk, gn=gn, xbc=xbc))
        else:
            sinks = att_sinks[j].reshape(1, 16)
            u, h = _mm_in(xc, wn_pre, w_att_in[j], 5, tm, f"att_in_{j}")
            a = _att2_fwd(u, cos_t, sin_t, sinks, f"att_core_{j}")
            y, xn = _mm_out(a, w_att_out[j], xc, wn_post, 4, tm, f"att_out_{j}")
            saved.append(dict(x=xc, u=u, h=h, a=a, y=y, sinks=sinks))
        xc = xn

    dx, loss_part = _loss_grad(xc, tgt, tm)

    d_pre, d_post = [None] * 4, [None] * 4
    d_ssm_in, d_ssm_out, d_att_in, d_att_out = [None] * 2, [None] * 2, [None] * 2, [None] * 2
    d_cw, d_cb, d_dtb, d_alog, d_dsk, d_gn, d_sinks = ([None] * 2 for _ in range(7))
    for i in reversed(range(4)):
        j = i // 2
        s = saved[i]
        wn_pre, wn_post = pre_norm[i].reshape(1, D_MODEL), post_norm[i].reshape(1, D_MODEL)
        if i % 2 == 0:
            da3, dy, d_post[i] = _mm_dout(s["y"], dx, wn_post, w_ssm_out[j].T, 4, tm, f"ssm_dout_{j}")
            d_ssm_out[j] = _dw_rows(s["a"], dy, 4, tm, f"ssm_dwout_{j}")
            du, d_dtb[j], d_alog[j], d_dsk[j], dgn = _ssm2_bwd(
                s["u"], s["xbc"], s["yp"], s["hst"], da3, s["dtb"], s["alog"], s["dsk"], s["gn"],
                f"ssm_core_bwd_{j}")
            du, dcw, dcb = _conv_bwd(s["u"], du, s["cw"], s["cb"], f"ssm_conv_bwd_{j}")
            d_cw[j], d_cb[j], d_gn[j] = _conv_untile(dcw), _conv_untile(dcb), dgn.reshape(1, SSM_INNER)
            d_ssm_in[j] = _ssm_w_in_untile(_dw_cols(s["h"], du, 5, tm, f"ssm_dwin_{j}"))
            dx, d_pre[i] = _mm_dh(du, s["w_in"].T, s["x"], dx, wn_pre, 5, tm, f"ssm_dh_{j}")
        else:
            da, dy, d_post[i] = _mm_dout(s["y"], dx, wn_post, w_att_out[j].T, 4, tm, f"att_dout_{j}")
            d_att_out[j] = _dw_rows(s["a"], dy, 4, tm, f"att_dwout_{j}")
            du, d_sinks[j] = _att2_bwd(s["u"], cos_t, sin_t, s["sinks"], da, f"att_core_bwd_{j}")
            d_att_in[j] = _dw_cols(s["h"], du, 5, tm, f"att_dwin_{j}")
            dx, d_pre[i] = _mm_dh(du, w_att_in[j].T, s["x"], dx, wn_pre, 5, tm, f"att_dh_{j}")

    g2 = _pack_grads(jnp.stack(d_ssm_in), jnp.stack(d_ssm_out), jnp.stack(d_att_in), jnp.stack(d_att_out),
                     jnp.stack(d_cw))
    r1 = _exchange_sibling(g2)
    pair = _pair_sum(g2, r1, cidx, ROWS_BIG // 5)
    parts = _exchange_chips(pair)
    w_p = _pack_big(ssm_w_in, ssm_w_out, att_w_in, att_w_out, ssm_conv_w)
    m_p = _pack_big(m_ssm_w_in, m_ssm_w_out, m_att_w_in, m_att_w_out, m_ssm_conv_w)
    v_p = _pack_big(v_ssm_w_in, v_ssm_w_out, v_att_w_in, v_att_w_out, v_ssm_conv_w)
    big = [_unpack_big(t) for t in _adamw(parts, w_p, m_p, v_p, ROWS_BIG // 5, "adamw_big")]

    small_local = _pack_small(jnp.concatenate(d_pre, axis=0), jnp.concatenate(d_post, axis=0),
                              jnp.concatenate(d_cb, axis=0), jnp.concatenate(d_gn, axis=0),
                              jnp.concatenate(d_dtb, axis=0), jnp.concatenate(d_alog, axis=0),
                              jnp.concatenate(d_dsk, axis=0), jnp.concatenate(d_sinks, axis=0), loss_part[0, 0])
    small_all = _all_gather_direct(small_local, "all_gather_small")
    ws = _pack_small(pre_norm, post_norm, ssm_conv_b, ssm_gate_norm, ssm_dt_bias, ssm_a_log, ssm_d, att_sinks)
    ms = _pack_small(m_pre_norm, m_post_norm, m_ssm_conv_b, m_ssm_gate_norm, m_ssm_dt_bias, m_ssm_a_log, m_ssm_d,
                     m_att_sinks)
    vs = _pack_small(v_pre_norm, v_post_norm, v_ssm_conv_b, v_ssm_gate_norm, v_ssm_dt_bias, v_ssm_a_log, v_ssm_d,
                     v_att_sinks)
    small4 = _adamw(small_all, ws, ms, vs, ROWS_SMALL, "adamw_small")
    loss = small4[0][32, 224]
    small = [_unpack_small(t) for t in small4]

    outs = [loss, dx.reshape(1, t_len, D_MODEL)]
    for k in range(4):
        b_ssm_in, b_ssm_out, b_att_in, b_att_out, b_conv = big[k]
        s_pre, s_post, s_cb, s_dtb, s_alog, s_d, s_gn, s_sinks = small[k]
        outs += [s_pre, s_post, b_ssm_in, b_conv, s_cb, s_dtb, s_alog, s_d, s_gn, b_ssm_out, b_att_in, s_sinks,
                 b_att_out]
    return tuple(outs)
```

```python
import functools

import jax
import jax.numpy as jnp
from jax import lax
from jax.experimental import pallas as pl
from jax.experimental.pallas import tpu as pltpu

f32 = jnp.float32
MXU_DTYPE = jnp.bfloat16
SDS = jax.ShapeDtypeStruct
MESH = pl.DeviceIdType.MESH

D_MODEL = 1024
EPS = 1e-6
TILE = 256
CHUNK = 128
SSM_HEADS = 32
SSM_GROUPS = 8
SSM_P = 64
SSM_N = 128
SSM_INNER = 2048
SSM_IN = 6176
SSM_NT = 25
SSM_KT = 8
ATT_NT = 10
ATT_KT = 4
ATT_IN = 2560
ROPE_THETA = 500000.0
N_DEV = 8
VMEM_LIMIT = 56 * 1024 * 1024

ADAM_LR = 0.001
ADAM_B1 = 0.9
ADAM_B2 = 0.999
ADAM_EPS = 1e-08
ADAM_WD = 0.01
ADAM_STEP = 10

ROWS_SSM_IN = 2 * 1024 * 772 // 1024
ROWS_SSM_OUT = 2 * 256
ROWS_ATT_IN = 2 * 1024 * 320 // 1024
ROWS_ATT_OUT = 2 * 128
ROWS_CONV = 8
ROWS_BIG = ROWS_SSM_IN + ROWS_SSM_OUT + ROWS_ATT_IN + ROWS_ATT_OUT + ROWS_CONV
ROWS_SMALL = 40


def _pc(body, name, **kw):
    return pl.pallas_call(body, name=name, **kw)


def _params(sem):
    return pltpu.CompilerParams(dimension_semantics=sem, vmem_limit_bytes=VMEM_LIMIT)


def _sigmoid(x):
    return 1.0 / (1.0 + jnp.exp(-x))


def _silu(x):
    return x * _sigmoid(x)


def _softplus(x):
    return jnp.maximum(x, 0.0) + jnp.log(1.0 + jnp.exp(-jnp.abs(x)))


def _mx(x):
    return x.astype(MXU_DTYPE)


def _dot(a, b):
    return jnp.dot(_mx(a), _mx(b), preferred_element_type=f32)


def _dot_nt(a, b):
    return lax.dot_general(_mx(a), _mx(b), (((1,), (1,)), ((), ())), preferred_element_type=f32)


def _dot_tn(a, b):
    return lax.dot_general(_mx(a), _mx(b), (((0,), (0,)), ((), ())), preferred_element_type=f32)


def _rms_fwd(x, w):
    r = lax.rsqrt(jnp.mean(x * x, axis=-1, keepdims=True) + EPS)
    return x * r * w


def _rms_bwd(x, w, dy):
    r = lax.rsqrt(jnp.mean(x * x, axis=-1, keepdims=True) + EPS)
    xh = x * r
    dw = jnp.sum(dy * xh, axis=0, keepdims=True)
    g = dy * w
    dx = r * (g - xh * jnp.mean(g * xh, axis=-1, keepdims=True))
    return dx, dw


def _mm_in(x, wn, w, ntb, tm, name):
    t_len, d = x.shape
    nt = w.shape[1] // TILE

    def body(x_ref, wn_ref, w_ref, u_ref, h_ref):
        @pl.when(pl.program_id(1) == 0)
        def _():
            h_ref[...] = _rms_fwd(x_ref[...], wn_ref[...]).astype(h_ref.dtype)
        h = h_ref[...]
        for t in range(ntb):
            u_ref[t] = jnp.dot(h, w_ref[:, TILE * t:TILE * (t + 1)], preferred_element_type=f32)

    return _pc(
        body, name, grid=(t_len // tm, nt // ntb),
        in_specs=[pl.BlockSpec((tm, d), lambda i, j: (i, 0)),
                  pl.BlockSpec((1, d), lambda i, j: (0, 0)),
                  pl.BlockSpec((d, ntb * TILE), lambda i, j: (0, j))],
        out_specs=[pl.BlockSpec((ntb, tm, TILE), lambda i, j: (j, i, 0)),
                   pl.BlockSpec((tm, d), lambda i, j: (i, 0))],
        out_shape=[SDS((nt, t_len, TILE), f32), SDS((t_len, d), MXU_DTYPE)],
        compiler_params=_params(("arbitrary", "arbitrary")),
    )(x, wn, w)


def _mm_dout(y, dxn, wn, w, ntb, tm, name):
    t_len, d = y.shape
    nt = w.shape[0] // TILE

    def body(y_ref, dxn_ref, wn_ref, w_ref, da_ref, dy_ref, dwn_ref):
        i, j = pl.program_id(0), pl.program_id(1)

        @pl.when((i == 0) & (j == 0))
        def _():
            dwn_ref[...] = jnp.zeros_like(dwn_ref)

        @pl.when(j == 0)
        def _():
            dy, dw = _rms_bwd(y_ref[...], wn_ref[...], dxn_ref[...])
            dy_ref[...] = dy.astype(dy_ref.dtype)
            dwn_ref[...] += dw
        dy = dy_ref[...]
        for t in range(ntb):
            da_ref[t] = _dot_nt(dy, w_ref[TILE * t:TILE * (t + 1), :])

    return _pc(
        body, name, grid=(t_len // tm, nt // ntb),
        in_specs=[pl.BlockSpec((tm, d), lambda i, j: (i, 0)),
                  pl.BlockSpec((tm, d), lambda i, j: (i, 0)),
                  pl.BlockSpec((1, d), lambda i, j: (0, 0)),
                  pl.BlockSpec((ntb * TILE, d), lambda i, j: (j, 0))],
        out_specs=[pl.BlockSpec((ntb, tm, TILE), lambda i, j: (j, i, 0)),
                   pl.BlockSpec((tm, d), lambda i, j: (i, 0)),
                   pl.BlockSpec((1, d), lambda i, j: (0, 0))],
        out_shape=[SDS((nt, t_len, TILE), f32), SDS((t_len, d), MXU_DTYPE), SDS((1, d), f32)],
        compiler_params=_params(("arbitrary", "arbitrary")),
    )(y, dxn, wn, w)


def _mm_out(a, w, x, wn, ktb, tm, name):
    kt, t_len, _ = a.shape
    d = w.shape[1]
    nk = kt // ktb

    def body(a_ref, w_ref, x_ref, wn_ref, y_ref, xn_ref, acc):
        k = pl.program_id(1)

        @pl.when(k == 0)
        def _():
            acc[...] = jnp.zeros_like(acc)
        s = acc[...]
        for t in range(ktb):
            s = s + jnp.dot(a_ref[t], w_ref[TILE * t:TILE * (t + 1), :], preferred_element_type=f32)
        acc[...] = s

        @pl.when(k == nk - 1)
        def _():
            y = acc[...]
            y_ref[...] = y
            xn_ref[...] = x_ref[...] + _rms_fwd(y, wn_ref[...])

    return _pc(
        body, name, grid=(t_len // tm, nk),
        in_specs=[pl.BlockSpec((ktb, tm, TILE), lambda i, k: (k, i, 0)),
                  pl.BlockSpec((ktb * TILE, d), lambda i, k: (k, 0)),
                  pl.BlockSpec((tm, d), lambda i, k: (i, 0)),
                  pl.BlockSpec((1, d), lambda i, k: (0, 0))],
        out_specs=[pl.BlockSpec((tm, d), lambda i, k: (i, 0)),
                   pl.BlockSpec((tm, d), lambda i, k: (i, 0))],
        out_shape=[SDS((t_len, d), f32), SDS((t_len, d), f32)],
        scratch_shapes=[pltpu.VMEM((tm, d), f32)],
        compiler_params=_params(("arbitrary", "arbitrary")),
    )(a, w, x, wn)


def _mm_dh(du, w, x, dxn, wn, ktb, tm, name):
    kt, t_len, _ = du.shape
    d = w.shape[0]
    nk = kt // ktb

    def body(du_ref, w_ref, x_ref, dxn_ref, wn_ref, dx_ref, dwn_ref, acc):
        i, k = pl.program_id(0), pl.program_id(1)

        @pl.when((i == 0) & (k == 0))
        def _():
            dwn_ref[...] = jnp.zeros_like(dwn_ref)

        @pl.when(k == 0)
        def _():
            acc[...] = jnp.zeros_like(acc)
        s = acc[...]
        for t in range(ktb):
            s = s + _dot_nt(du_ref[t], w_ref[:, TILE * t:TILE * (t + 1)])
        acc[...] = s

        @pl.when(k == nk - 1)
        def _():
            dxp, dw = _rms_bwd(x_ref[...], wn_ref[...], acc[...])
            dx_ref[...] = dxn_ref[...] + dxp
            dwn_ref[...] += dw

    return _pc(
        body, name, grid=(t_len // tm, nk),
        in_specs=[pl.BlockSpec((ktb, tm, TILE), lambda i, k: (k, i, 0)),
                  pl.BlockSpec((d, ktb * TILE), lambda i, k: (0, k)),
                  pl.BlockSpec((tm, d), lambda i, k: (i, 0)),
                  pl.BlockSpec((tm, d), lambda i, k: (i, 0)),
                  pl.BlockSpec((1, d), lambda i, k: (0, 0))],
        out_specs=[pl.BlockSpec((tm, d), lambda i, k: (i, 0)),
                   pl.BlockSpec((1, d), lambda i, k: (0, 0))],
        out_shape=[SDS((t_len, d), f32), SDS((1, d), f32)],
        scratch_shapes=[pltpu.VMEM((tm, d), f32)],
        compiler_params=_params(("arbitrary", "arbitrary")),
    )(du, w, x, dxn, wn)


def _dw_cols(a, b, ntb, tk, name):
    t_len, kdim = a.shape
    nt = b.shape[0]

    def body(a_ref, b_ref, o_ref):
        @pl.when(pl.program_id(1) == 0)
        def _():
            o_ref[...] = jnp.zeros_like(o_ref)
        av = a_ref[...]
        for s in range(ntb):
            o_ref[:, TILE * s:TILE * (s + 1)] += _dot_tn(av, b_ref[s])

    return _pc(
        body, name, grid=(nt // ntb, t_len // tk),
        in_specs=[pl.BlockSpec((tk, kdim), lambda j, t: (t, 0)),
                  pl.BlockSpec((ntb, tk, TILE), lambda j, t: (j, t, 0))],
        out_specs=pl.BlockSpec((kdim, ntb * TILE), lambda j, t: (0, j)),
        out_shape=SDS((kdim, nt * TILE), f32),
        compiler_params=_params(("arbitrary", "arbitrary")),
    )(a, b)


def _dw_rows(a, b, ktb, tk, name):
    kt, t_len, _ = a.shape
    d = b.shape[1]

    def body(a_ref, b_ref, o_ref):
        @pl.when(pl.program_id(1) == 0)
        def _():
            o_ref[...] = jnp.zeros_like(o_ref)
        bv = b_ref[...]
        for s in range(ktb):
            o_ref[TILE * s:TILE * (s + 1), :] += _dot_tn(a_ref[s], bv)

    return _pc(
        body, name, grid=(kt // ktb, t_len // tk),
        in_specs=[pl.BlockSpec((ktb, tk, TILE), lambda k, t: (k, t, 0)),
                  pl.BlockSpec((tk, d), lambda k, t: (t, 0))],
        out_specs=pl.BlockSpec((ktb * TILE, d), lambda k, t: (k, 0)),
        out_shape=SDS((kt * TILE, d), f32),
        compiler_params=_params(("arbitrary", "arbitrary")),
    )(a, b)


def _loss_grad(x, tgt, tm):
    t_len, d = x.shape

    def body(x_ref, t_ref, dx_ref, l_ref):
        @pl.when(pl.program_id(0) == 0)
        def _():
            l_ref[...] = jnp.zeros_like(l_ref)
        e = x_ref[...] - t_ref[...]
        dx_ref[...] = e * (1.0 / d)
        row = jnp.mean(e * e, axis=-1, keepdims=True)
        l_ref[...] += 0.5 * jnp.sum(row, axis=0, keepdims=True)

    return _pc(
        body, "loss_grad", grid=(t_len // tm,),
        in_specs=[pl.BlockSpec((tm, d), lambda i: (i, 0)), pl.BlockSpec((tm, d), lambda i: (i, 0))],
        out_specs=[pl.BlockSpec((tm, d), lambda i: (i, 0)), pl.BlockSpec((1, 128), lambda i: (0, 0))],
        out_shape=[SDS((t_len, d), f32), SDS((1, 128), f32)],
        compiler_params=_params(("arbitrary",)),
    )(x, tgt)


def _tri(lower):
    r = lax.broadcasted_iota(jnp.int32, (CHUNK, CHUNK), 0)
    c = lax.broadcasted_iota(jnp.int32, (CHUNK, CHUNK), 1)
    return ((c <= r) if lower else (c >= r)).astype(f32)


def _dot_hi(a, b, dims):
    return lax.dot_general(a, b, (dims, ((), ())), precision=lax.Precision.HIGHEST, preferred_element_type=f32)


def _split(x, n):
    parts = []
    for _ in range(n):
        p = x.astype(jnp.bfloat16)
        parts.append(p)
        x = x - p.astype(f32)
    return parts


def _dot_exact(a, b, dims, split_a, n=3):
    out = None
    if split_a:
        b = b.astype(jnp.bfloat16)
        for p in _split(a, n):
            t = lax.dot_general(p, b, (dims, ((), ())), preferred_element_type=f32)
            out = t if out is None else out + t
    else:
        a = a.astype(jnp.bfloat16)
        for p in _split(b, n):
            t = lax.dot_general(a, p, (dims, ((), ())), preferred_element_type=f32)
            out = t if out is None else out + t
    return out


def _dt_path(dt_raw, dtb, alog):
    dtr = dt_raw + dtb
    dt = _softplus(dtr)
    a_neg = -jnp.exp(alog)
    a = dt * a_neg
    acs = _dot_exact(_tri(True), a, ((1,), (0,)), False)
    acs_t = _dot_exact(a, _tri(False), ((0,), (0,)), True)
    return dtr, dt, a_neg, acs, acs_t


def _conv_silu(x0, x1, x2, x3, w0, w1, w2, w3, b):
    acc = b + w0 * x0 + w1 * x1 + w2 * x2 + w3 * x3
    return _silu(acc)


def _ssd_group(xs, bm, cm, dtc, ac, ar, dh, hp):
    row = lax.broadcasted_iota(jnp.int32, (CHUNK, CHUNK), 0)
    col = lax.broadcasted_iota(jnp.int32, (CHUNK, CHUNK), 1)
    causal = col <= row
    last = (lax.broadcasted_iota(jnp.int32, (1, CHUNK), 1) == CHUNK - 1).astype(f32)
    cb = _dot_nt(cm, bm)
    ys, hns = [], []
    for r in range(4):
        xt = xs[r] * dtc[r]
        decay = jnp.exp(jnp.where(causal, ac[r] - ar[r], -jnp.inf))
        y_diag = _dot(cb * decay, xt)
        a_end = jnp.sum(ar[r] * last, axis=1, keepdims=True)
        y_off = _dot_nt(cm, hp[r]) * jnp.exp(ac[r])
        st = _dot_tn(xt * jnp.exp(a_end - ac[r]), bm)
        hns.append(hp[r] * jnp.exp(a_end) + st)
        ys.append(y_diag + y_off + dh[r] * xs[r])
    return tuple(ys), tuple(hns)


def _head_cols(g, r, dt, acs, acs_t, dsk):
    lane = lax.broadcasted_iota(jnp.int32, (1, SSM_HEADS), 1)
    sub = lax.broadcasted_iota(jnp.int32, (SSM_HEADS, 1), 0)
    h = 4 * g + r
    oh_l = (lane == h).astype(f32)
    oh_s = (sub == h).astype(f32)
    dtc = jnp.sum(dt * oh_l, axis=1, keepdims=True)
    ac = jnp.sum(acs * oh_l, axis=1, keepdims=True)
    ar = jnp.sum(acs_t * oh_s, axis=0, keepdims=True)
    dh = jnp.sum(dsk * oh_l, axis=1, keepdims=True)
    return h, oh_l, oh_s, dtc, ac, ar, dh


def _ssm_in_specs(nc, rev):
    def cidx(c):
        return (nc - 1 - c) if rev else c
    return [
        pl.BlockSpec((SSM_NT, CHUNK, TILE), lambda c: (0, cidx(c), 0)),
        pl.BlockSpec((SSM_NT, 8, TILE), lambda c: (0, jnp.maximum(cidx(c) * (CHUNK // 8) - 1, 0), 0)),
        pl.BlockSpec((16, 4, TILE), lambda c: (0, 0, 0)),
        pl.BlockSpec((16, 1, TILE), lambda c: (0, 0, 0)),
        pl.BlockSpec((1, SSM_HEADS), lambda c: (0, 0)),
        pl.BlockSpec((1, SSM_HEADS), lambda c: (0, 0)),
        pl.BlockSpec((1, SSM_HEADS), lambda c: (0, 0)),
        pl.BlockSpec((SSM_KT, 1, TILE), lambda c: (0, 0, 0)),
    ]


def _ssm_fwd(u, cw, cb, dtb, alog, dsk, gn, name):
    t_len = u.shape[1]
    nc = t_len // CHUNK

    def body(u_ref, halo_ref, cw_ref, cb_ref, dtb_ref, alog_ref, dsk_ref, gn_ref,
             a3_ref, yp_ref, hst_ref, h_s, win_s, xs_s, bc_s):
        c = pl.program_id(0)

        @pl.when(c == 0)
        def _():
            h_s[...] = jnp.zeros_like(h_s)
        _, dt, _, acs, acs_t = _dt_path(u_ref[SSM_NT - 1, :, 0:SSM_HEADS], dtb_ref[...], alog_ref[...])
        dsk = dsk_ref[...]

        def conv_tile(j, p):
            win_s[0:8, :] = jnp.where(c > 0, halo_ref[j], 0.0)
            win_s[8:8 + CHUNK, :] = u_ref[j]
            return _conv_silu(win_s[5:5 + CHUNK, :], win_s[6:6 + CHUNK, :], win_s[7:7 + CHUNK, :],
                              win_s[8:8 + CHUNK, :], cw_ref[p, 0:1, :], cw_ref[p, 1:2, :], cw_ref[p, 2:3, :],
                              cw_ref[p, 3:4, :], cb_ref[p])

        def group(g, s1):
            xs_s[...] = conv_tile(8 + g, g)
            bc_s[...] = conv_tile(16 + g, 8 + g)
            bm, cm = bc_s[:, 0:SSM_N], bc_s[:, SSM_N:2 * SSM_N]
            hs, xs, dtc, ac, ar, dh, hp = [], [], [], [], [], [], []
            for r in range(4):
                h, _, _, dtc_r, ac_r, ar_r, dh_r = _head_cols(g, r, dt, acs, acs_t, dsk)
                hs.append(h); dtc.append(dtc_r); ac.append(ac_r); ar.append(ar_r); dh.append(dh_r)
                xs.append(xs_s[:, SSM_P * r:SSM_P * (r + 1)])
                hp.append(h_s[h])
            ys, hns = _ssd_group(xs, bm, cm, dtc, ac, ar, dh, hp)
            for r in range(4):
                hst_ref[0, hs[r]] = hp[r]
                h_s[hs[r]] = hns[r]
                yp_ref[g, :, SSM_P * r:SSM_P * (r + 1)] = ys[r]
            y2 = yp_ref[g] * _silu(u_ref[g])
            return s1 + jnp.sum(y2 * y2, axis=1, keepdims=True)

        s1 = lax.fori_loop(0, SSM_GROUPS, group, jnp.zeros((CHUNK, 1), f32))
        rinv = lax.rsqrt(s1 * (1.0 / SSM_INNER) + EPS)

        def gate(g, carry):
            y2 = yp_ref[g] * _silu(u_ref[g])
            a3_ref[g] = (y2 * rinv * gn_ref[g]).astype(a3_ref.dtype)
            return carry

        lax.fori_loop(0, SSM_GROUPS, gate, 0)

    return _pc(
        body, name, grid=(nc,),
        in_specs=_ssm_in_specs(nc, False),
        out_specs=[pl.BlockSpec((SSM_KT, CHUNK, TILE), lambda c: (0, c, 0)),
                   pl.BlockSpec((SSM_KT, CHUNK, TILE), lambda c: (0, c, 0)),
                   pl.BlockSpec((1, SSM_HEADS, SSM_P, SSM_N), lambda c: (c, 0, 0, 0))],
        out_shape=[SDS((SSM_KT, t_len, TILE), MXU_DTYPE), SDS((SSM_KT, t_len, TILE), f32),
                   SDS((nc, SSM_HEADS, SSM_P, SSM_N), f32)],
        scratch_shapes=[pltpu.VMEM((SSM_HEADS, SSM_P, SSM_N), f32), pltpu.VMEM((8 + CHUNK, TILE), f32),
                        pltpu.VMEM((CHUNK, TILE), f32), pltpu.VMEM((CHUNK, TILE), f32)],
        compiler_params=_params(("arbitrary",)),
    )(u, u, cw, cb, dtb, alog, dsk, gn)


def _ssm_bwd(u, yp, hst, da3, cw, cb, dtb, alog, dsk, gn, name):
    t_len = u.shape[1]
    nc = t_len // CHUNK

    def body(u_ref, halo_ref, cw_ref, cb_ref, dtb_ref, alog_ref, dsk_ref, gn_ref, yp_ref, hst_ref, da3_ref,
             du_ref, dcw_ref, dcb_ref, ddtb_ref, dalog_ref, ddsk_ref, dgn_ref,
             dh_s, carry_s, win_s, dwin_s, xs_s, bc_s, dy_s, dxs_s, dbc_s):
        step = pl.program_id(0)
        cc = nc - 1 - step

        @pl.when(step == 0)
        def _():
            dh_s[...] = jnp.zeros_like(dh_s)
            carry_s[...] = jnp.zeros_like(carry_s)
            dcw_ref[...] = jnp.zeros_like(dcw_ref)
            dcb_ref[...] = jnp.zeros_like(dcb_ref)
            ddtb_ref[...] = jnp.zeros_like(ddtb_ref)
            dalog_ref[...] = jnp.zeros_like(dalog_ref)
            ddsk_ref[...] = jnp.zeros_like(ddsk_ref)
            dgn_ref[...] = jnp.zeros_like(dgn_ref)
        dtr, dt, a_neg, acs, acs_t = _dt_path(u_ref[SSM_NT - 1, :, 0:SSM_HEADS], dtb_ref[...], alog_ref[...])
        dsk = dsk_ref[...]

        def sums(g, carry):
            s1, s2 = carry
            y2 = yp_ref[g] * _silu(u_ref[g])
            g3 = da3_ref[g] * gn_ref[g]
            return (s1 + jnp.sum(y2 * y2, axis=1, keepdims=True), s2 + jnp.sum(g3 * y2, axis=1, keepdims=True))

        zcol = jnp.zeros((CHUNK, 1), f32)
        s1, s2 = lax.fori_loop(0, SSM_GROUPS, sums, (zcol, zcol))
        rinv = lax.rsqrt(s1 * (1.0 / SSM_INNER) + EPS)
        m2 = s2 * rinv * rinv * rinv * (1.0 / SSM_INNER)

        def windows(j):
            win_s[0:8, :] = jnp.where(cc > 0, halo_ref[j], 0.0)
            win_s[8:8 + CHUNK, :] = u_ref[j]
            return [win_s[5 + k:5 + k + CHUNK, :] for k in range(4)]

        def taps(p):
            return [cw_ref[p, k:k + 1, :] for k in range(4)]

        def conv_bwd(j, p, vjp_fn, dout):
            dx0, dx1, dx2, dx3, dw0, dw1, dw2, dw3, db = vjp_fn(dout)
            dwin_s[0:CHUNK, :] = jnp.zeros((CHUNK, TILE), f32)
            dwin_s[CHUNK:CHUNK + 8, :] = carry_s[j - 8]
            for k, dxk in enumerate((dx0, dx1, dx2, dx3)):
                dwin_s[5 + k:5 + k + CHUNK, :] += dxk
            du_ref[j] = dwin_s[8:8 + CHUNK, :]
            carry_s[j - 8] = dwin_s[0:8, :]
            for k, dwk in enumerate((dw0, dw1, dw2, dw3)):
                dcw_ref[p, k:k + 1, :] += dwk
            dcb_ref[p] += db

        def group(g, carry):
            ddt, dacs, dacs_t, ddsk = carry
            z = u_ref[g]
            sg = _sigmoid(z)
            sz = z * sg
            y = yp_ref[g]
            y2 = y * sz
            da3 = da3_ref[g]
            dgn_ref[g] += jnp.sum(da3 * y2 * rinv, axis=0, keepdims=True)
            dy2 = rinv * (da3 * gn_ref[g]) - y2 * m2
            dy_s[...] = dy2 * sz
            du_ref[g] = dy2 * y * (sg * (1.0 + z * (1.0 - sg)))
            xs_v, vjp_xs = jax.vjp(_conv_silu, *windows(8 + g), *taps(g), cb_ref[g])
            xs_s[...] = xs_v
            bc_v, vjp_bc = jax.vjp(_conv_silu, *windows(16 + g), *taps(8 + g), cb_ref[8 + g])
            bc_s[...] = bc_v
            bm, cm = bc_s[:, 0:SSM_N], bc_s[:, SSM_N:2 * SSM_N]
            hs, ohl, ohs, xs, dtc, ac, ar, dh, hp, dys, dhn = [], [], [], [], [], [], [], [], [], [], []
            for r in range(4):
                h, oh_l, oh_s, dtc_r, ac_r, ar_r, dh_r = _head_cols(g, r, dt, acs, acs_t, dsk)
                hs.append(h); ohl.append(oh_l); ohs.append(oh_s)
                dtc.append(dtc_r); ac.append(ac_r); ar.append(ar_r); dh.append(dh_r)
                xs.append(xs_s[:, SSM_P * r:SSM_P * (r + 1)])
                hp.append(hst_ref[0, h])
                dys.append(dy_s[:, SSM_P * r:SSM_P * (r + 1)])
                dhn.append(dh_s[h])
            _, vjp_g = jax.vjp(_ssd_group, xs, bm, cm, dtc, ac, ar, dh, hp)
            dxs, dbm, dcm, ddtc, dac, dar, ddh, dhp = vjp_g((tuple(dys), tuple(dhn)))
            for r in range(4):
                dxs_s[:, SSM_P * r:SSM_P * (r + 1)] = dxs[r]
                dh_s[hs[r]] = dhp[r]
                ddt = ddt + ddtc[r] * ohl[r]
                dacs = dacs + dac[r] * ohl[r]
                dacs_t = dacs_t + ohs[r] * dar[r]
                ddsk = ddsk + ddh[r] * ohl[r]
            dbc_s[:, 0:SSM_N] = dbm
            dbc_s[:, SSM_N:2 * SSM_N] = dcm
            conv_bwd(8 + g, g, vjp_xs, dxs_s[...])
            conv_bwd(16 + g, 8 + g, vjp_bc, dbc_s[...])
            return ddt, dacs, dacs_t, ddsk

        init = (jnp.zeros((CHUNK, SSM_HEADS), f32), jnp.zeros((CHUNK, SSM_HEADS), f32),
                jnp.zeros((SSM_HEADS, CHUNK), f32), jnp.zeros((1, SSM_HEADS), f32))
        ddt, dacs, dacs_t, ddsk = lax.fori_loop(0, SSM_GROUPS, group, init)
        upper = _tri(False)
        da = _dot_hi(upper, dacs, ((1,), (0,))) + _dot_hi(upper, dacs_t, ((1,), (1,)))
        ddt = ddt + da * a_neg
        dalog_ref[...] += jnp.sum(da * dt, axis=0, keepdims=True) * a_neg
        ddtr = ddt * _sigmoid(dtr)
        ddtb_ref[...] += jnp.sum(ddtr, axis=0, keepdims=True)
        ddsk_ref[...] += ddsk
        du_ref[SSM_NT - 1] = jnp.zeros((CHUNK, TILE), f32)
        du_ref[SSM_NT - 1, :, 0:SSM_HEADS] = ddtr

    def rc(c):
        return nc - 1 - c

    small = [pl.BlockSpec((16, 4, TILE), lambda c: (0, 0, 0)),
             pl.BlockSpec((16, 1, TILE), lambda c: (0, 0, 0)),
             pl.BlockSpec((1, SSM_HEADS), lambda c: (0, 0)),
             pl.BlockSpec((1, SSM_HEADS), lambda c: (0, 0)),
             pl.BlockSpec((1, SSM_HEADS), lambda c: (0, 0)),
             pl.BlockSpec((SSM_KT, 1, TILE), lambda c: (0, 0, 0))]
    return _pc(
        body, name, grid=(nc,),
        in_specs=_ssm_in_specs(nc, True) + [
            pl.BlockSpec((SSM_KT, CHUNK, TILE), lambda c: (0, rc(c), 0)),
            pl.BlockSpec((1, SSM_HEADS, SSM_P, SSM_N), lambda c: (rc(c), 0, 0, 0)),
            pl.BlockSpec((SSM_KT, CHUNK, TILE), lambda c: (0, rc(c), 0))],
        out_specs=[pl.BlockSpec((SSM_NT, CHUNK, TILE), lambda c: (0, rc(c), 0))] + small,
        out_shape=[SDS((SSM_NT, t_len, TILE), f32), SDS((16, 4, TILE), f32), SDS((16, 1, TILE), f32),
                   SDS((1, SSM_HEADS), f32), SDS((1, SSM_HEADS), f32), SDS((1, SSM_HEADS), f32),
                   SDS((SSM_KT, 1, TILE), f32)],
        scratch_shapes=[pltpu.VMEM((SSM_HEADS, SSM_P, SSM_N), f32), pltpu.VMEM((16, 8, TILE), f32),
                        pltpu.VMEM((8 + CHUNK, TILE), f32), pltpu.VMEM((8 + CHUNK, TILE), f32)]
                       + [pltpu.VMEM((CHUNK, TILE), f32)] * 5,
        compiler_params=_params(("arbitrary",)),
    )(u, u, cw, cb, dtb, alog, dsk, gn, yp, hst, da3)


CONV_ROWS = 1024
CONV_SUB = 32


def _conv_specs(nb, rows, rev):
    def ridx(i):
        return (nb - 1 - i) if rev else i
    return [
        pl.BlockSpec((1, rows, TILE), lambda p, i: (8 + p, ridx(i), 0)),
        pl.BlockSpec((1, 8, TILE), lambda p, i: (8 + p, jnp.maximum(ridx(i) * (rows // 8) - 1, 0), 0)),
        pl.BlockSpec((1, 4, TILE), lambda p, i: (p, 0, 0)),
        pl.BlockSpec((1, 1, TILE), lambda p, i: (p, 0, 0)),
    ]


def _conv_fwd(u, cw, cb, name):
    t_len = u.shape[1]
    rows = min(CONV_ROWS, t_len)
    nb = t_len // rows

    def body(u_ref, halo_ref, cw_ref, cb_ref, o_ref, win_s):
        i = pl.program_id(1)
        win_s[0:8, :] = jnp.where(i > 0, halo_ref[0], 0.0)
        win_s[8:8 + rows, :] = u_ref[0]
        w = [cw_ref[0, k:k + 1, :] for k in range(4)]
        b = cb_ref[0]
        for s in range(rows // CONV_SUB):
            o = CONV_SUB * s
            acc = b
            for k in range(4):
                acc = acc + w[k] * win_s[5 + k + o:5 + k + o + CONV_SUB, :]
            o_ref[0, o:o + CONV_SUB, :] = _silu(acc)

    return _pc(
        body, name, grid=(16, nb),
        in_specs=_conv_specs(nb, rows, False),
        out_specs=pl.BlockSpec((1, rows, TILE), lambda p, i: (p, i, 0)),
        out_shape=SDS((16, t_len, TILE), f32),
        scratch_shapes=[pltpu.VMEM((8 + rows, TILE), f32)],
        compiler_params=_params(("arbitrary", "arbitrary")),
    )(u, u, cw, cb)


def _conv_bwd(u, du, cw, cb, name):
    t_len = u.shape[1]
    rows = min(CONV_ROWS, t_len)
    nb = t_len // rows

    def body(u_ref, halo_ref, cw_ref, cb_ref, d_ref, o_ref, dcw_ref, dcb_ref, carry_s, win_s, dp_s):
        i = pl.program_id(1)
        ri = nb - 1 - i

        @pl.when(i == 0)
        def _():
            carry_s[...] = jnp.zeros_like(carry_s)
            dcw_ref[...] = jnp.zeros_like(dcw_ref)
            dcb_ref[...] = jnp.zeros_like(dcb_ref)
        win_s[0:8, :] = jnp.where(ri > 0, halo_ref[0], 0.0)
        win_s[8:8 + rows, :] = u_ref[0]
        w = [cw_ref[0, k:k + 1, :] for k in range(4)]
        b = cb_ref[0]
        dw = [jnp.zeros((1, TILE), f32)] * 4
        db = jnp.zeros((1, TILE), f32)
        for s in range(rows // CONV_SUB):
            o = CONV_SUB * s
            xk = [win_s[5 + k + o:5 + k + o + CONV_SUB, :] for k in range(4)]
            pre = b
            for k in range(4):
                pre = pre + w[k] * xk[k]
            sg = _sigmoid(pre)
            dpre = d_ref[0, o:o + CONV_SUB, :] * (sg * (1.0 + pre * (1.0 - sg)))
            dp_s[o:o + CONV_SUB, :] = dpre
            dw = [dw[k] + jnp.sum(dpre * xk[k], axis=0, keepdims=True) for k in range(4)]
            db = db + jnp.sum(dpre, axis=0, keepdims=True)
        dp_s[rows:rows + 8, :] = carry_s[...]
        for s in range(rows // CONV_SUB):
            o = CONV_SUB * s
            acc = w[0] * dp_s[3 + o:3 + o + CONV_SUB, :]
            for k in range(1, 4):
                acc = acc + w[k] * dp_s[3 - k + o:3 - k + o + CONV_SUB, :]
            o_ref[0, o:o + CONV_SUB, :] = acc
        carry_s[...] = dp_s[0:8, :]
        for k in range(4):
            dcw_ref[0, k:k + 1, :] += dw[k]
        dcb_ref[0] += db

    return _pc(
        body, name, grid=(16, nb),
        in_specs=_conv_specs(nb, rows, True) + [pl.BlockSpec((1, rows, TILE), lambda p, i: (8 + p, nb - 1 - i, 0))],
        out_specs=[pl.BlockSpec((1, rows, TILE), lambda p, i: (8 + p, nb - 1 - i, 0)),
                   pl.BlockSpec((1, 4, TILE), lambda p, i: (p, 0, 0)),
                   pl.BlockSpec((1, 1, TILE), lambda p, i: (p, 0, 0))],
        out_shape=[SDS(du.shape, f32), SDS((16, 4, TILE), f32), SDS((16, 1, TILE), f32)],
        input_output_aliases={4: 0},
        scratch_shapes=[pltpu.VMEM((8, TILE), f32), pltpu.VMEM((8 + rows, TILE), f32),
                        pltpu.VMEM((rows + 8, TILE), f32)],
        compiler_params=_params(("arbitrary", "arbitrary")),
    )(u, u, cw, cb, du)


def _collapse_matrix(g):
    r = lax.broadcasted_iota(jnp.int32, (SSM_HEADS, TILE), 0)
    c = lax.broadcasted_iota(jnp.int32, (SSM_HEADS, TILE), 1)
    return ((c // SSM_P) + 4 * g == r).astype(jnp.bfloat16)


def _ssd_prelude(dt_raw, dtb, alog, dsk, colx_s, scx_s):
    dtr, dt, a_neg, acs, acs_t = _dt_path(dt_raw, dtb, alog)
    a_end = acs[CHUNK - 1:CHUNK, :]
    lane = lax.broadcasted_iota(jnp.int32, (1, 2 * SSM_P), 1)
    lane4 = lax.broadcasted_iota(jnp.int32, (1, TILE), 1)
    sub8 = lax.broadcasted_iota(jnp.int32, (8, 1), 0)

    def row4(v, g):
        e = [v[:, 4 * g + r:4 * g + r + 1] for r in range(4)]
        return jnp.where(lane4 < 64, e[0], jnp.where(lane4 < 128, e[1], jnp.where(lane4 < 192, e[2], e[3])))

    for g in range(SSM_GROUPS):
        for k, arr in enumerate((dt, acs)):
            for half in range(2):
                h0 = 4 * g + 2 * half
                colx_s[k, g, :, 128 * half:128 * (half + 1)] = jnp.where(
                    lane < SSM_P, arr[:, h0:h0 + 1], arr[:, h0 + 1:h0 + 2])
        scx_s[g] = jnp.where(sub8 == 0, row4(dsk, g), jnp.where(sub8 == 1, row4(a_end, g), 0.0))
    return dtr, dt, a_neg, acs_t


def _ssm_core_specs(nc, rev):
    def cidx(c):
        return (nc - 1 - c) if rev else c
    return [
        pl.BlockSpec((SSM_KT, CHUNK, TILE), lambda c: (0, cidx(c), 0)),
        pl.BlockSpec((1, CHUNK, TILE), lambda c: (SSM_NT - 1, cidx(c), 0)),
        pl.BlockSpec((16, CHUNK, TILE), lambda c: (0, cidx(c), 0)),
        pl.BlockSpec((1, SSM_HEADS), lambda c: (0, 0)),
        pl.BlockSpec((1, SSM_HEADS), lambda c: (0, 0)),
        pl.BlockSpec((1, SSM_HEADS), lambda c: (0, 0)),
        pl.BlockSpec((SSM_KT, 1, TILE), lambda c: (0, 0, 0)),
    ]


def _causal():
    row = lax.broadcasted_iota(jnp.int32, (CHUNK, CHUNK), 0)
    col = lax.broadcasted_iota(jnp.int32, (CHUNK, CHUNK), 1)
    return col <= row


def _ssm2_fwd(u, xbc, dtb, alog, dsk, gn, name):
    t_len = u.shape[1]
    nc = t_len // CHUNK

    def body(z_ref, dt_ref, x_ref, dtb_ref, alog_ref, dsk_ref, gn_ref, a3_ref, yp_ref, hst_ref,
             h_s, colx_s, scx_s, xt_s, yd_s):
        c = pl.program_id(0)

        @pl.when(c == 0)
        def _():
            h_s[...] = jnp.zeros_like(h_s)
        _, _, _, acs_t = _ssd_prelude(dt_ref[0, :, 0:SSM_HEADS], dtb_ref[...], alog_ref[...], dsk_ref[...],
                                      colx_s, scx_s)
        causal = _causal()

        def group(g, s1):
            xs = x_ref[g]
            bm, cm = x_ref[8 + g, :, 0:SSM_N], x_ref[8 + g, :, SSM_N:2 * SSM_N]
            cb = _dot_nt(cm, bm)
            sc = scx_s[g]
            a_end = sc[1:2, :]
            rows = pltpu.roll(acs_t, (SSM_HEADS - 4 * g) % SSM_HEADS, 0)
            hp = h_s[g]
            xt = xs * colx_s[0, g]
            xt_s[...] = xt
            acx = colx_s[1, g]
            for r in range(4):
                hd = slice(SSM_P * r, SSM_P * (r + 1))
                lam = jnp.exp(jnp.where(causal, acx[:, SSM_P * r:SSM_P * r + 1] - rows[r:r + 1, :], -jnp.inf))
                yd_s[:, hd] = _dot(cb * lam, xt_s[:, hd])
            yp_ref[g] = yd_s[...] + _dot(cm, hp) * jnp.exp(acx) + sc[0:1, :] * xs
            hst_ref[0, g] = hp
            h_s[g] = hp * jnp.exp(a_end) + _dot_tn(bm, xt * jnp.exp(a_end - acx))
            y2 = yp_ref[g] * _silu(z_ref[g])
            return s1 + jnp.sum(y2 * y2, axis=1, keepdims=True)

        s1 = lax.fori_loop(0, SSM_GROUPS, group, jnp.zeros((CHUNK, 1), f32))
        rinv = lax.rsqrt(s1 * (1.0 / SSM_INNER) + EPS)

        def gate(g, carry):
            y2 = yp_ref[g] * _silu(z_ref[g])
            a3_ref[g] = (y2 * rinv * gn_ref[g]).astype(a3_ref.dtype)
            return carry

        lax.fori_loop(0, SSM_GROUPS, gate, 0)

    return _pc(
        body, name, grid=(nc,),
        in_specs=_ssm_core_specs(nc, False),
        out_specs=[pl.BlockSpec((SSM_KT, CHUNK, TILE), lambda c: (0, c, 0)),
                   pl.BlockSpec((SSM_KT, CHUNK, TILE), lambda c: (0, c, 0)),
                   pl.BlockSpec((1, SSM_GROUPS, SSM_N, TILE), lambda c: (c, 0, 0, 0))],
        out_shape=[SDS((SSM_KT, t_len, TILE), MXU_DTYPE), SDS((SSM_KT, t_len, TILE), f32),
                   SDS((nc, SSM_GROUPS, SSM_N, TILE), f32)],
        scratch_shapes=[pltpu.VMEM((SSM_GROUPS, SSM_N, TILE), f32), pltpu.VMEM((2, SSM_GROUPS, CHUNK, TILE), f32),
                        pltpu.VMEM((SSM_GROUPS, 8, TILE), f32), pltpu.VMEM((CHUNK, TILE), f32),
                        pltpu.VMEM((CHUNK, TILE), f32)],
        compiler_params=_params(("arbitrary",)),
    )(u, u, xbc, dtb, alog, dsk, gn)


def _ssm2_bwd(u, xbc, yp, hst, da3, dtb, alog, dsk, gn, name):
    t_len = u.shape[1]
    nc = t_len // CHUNK

    def body(z_ref, dt_ref, x_ref, dtb_ref, alog_ref, dsk_ref, gn_ref, yp_ref, hst_ref, da3_ref,
             du_ref, ddtb_ref, dalog_ref, ddsk_ref, dgn_ref,
             dh_s, colx_s, scx_s, xt_s, dy_s, dxt_s, ddtx_s, dacx_s, ddx_s, drow_s):
        step = pl.program_id(0)

        @pl.when(step == 0)
        def _():
            dh_s[...] = jnp.zeros_like(dh_s)
            ddtb_ref[...] = jnp.zeros_like(ddtb_ref)
            dalog_ref[...] = jnp.zeros_like(dalog_ref)
            ddsk_ref[...] = jnp.zeros_like(ddsk_ref)
            dgn_ref[...] = jnp.zeros_like(dgn_ref)
        dtr, dt, a_neg, acs_t = _ssd_prelude(dt_ref[0, :, 0:SSM_HEADS], dtb_ref[...], alog_ref[...], dsk_ref[...],
                                             colx_s, scx_s)
        causal = _causal()
        causal_t = (lax.broadcasted_iota(jnp.int32, (CHUNK, CHUNK), 1)
                    >= lax.broadcasted_iota(jnp.int32, (CHUNK, CHUNK), 0))
        last = (lax.broadcasted_iota(jnp.int32, (1, CHUNK), 1) == CHUNK - 1).astype(f32)
        lane = lax.broadcasted_iota(jnp.int32, (1, TILE), 1)
        sub32 = lax.broadcasted_iota(jnp.int32, (SSM_HEADS, 1), 0)
        drow_s[...] = jnp.zeros_like(drow_s)

        def sums(g, carry):
            s1, s2 = carry
            y2 = yp_ref[g] * _silu(z_ref[g])
            g3 = da3_ref[g] * gn_ref[g]
            return (s1 + jnp.sum(y2 * y2, axis=1, keepdims=True), s2 + jnp.sum(g3 * y2, axis=1, keepdims=True))

        zcol = jnp.zeros((CHUNK, 1), f32)
        s1, s2 = lax.fori_loop(0, SSM_GROUPS, sums, (zcol, zcol))
        rinv = lax.rsqrt(s1 * (1.0 / SSM_INNER) + EPS)
        m2 = s2 * rinv * rinv * rinv * (1.0 / SSM_INNER)

        def group(g, carry):
            z = z_ref[g]
            sg = _sigmoid(z)
            sz = z * sg
            y = yp_ref[g]
            y2 = y * sz
            da3 = da3_ref[g]
            dgn_ref[g] += jnp.sum(da3 * y2 * rinv, axis=0, keepdims=True)
            dy2 = rinv * (da3 * gn_ref[g]) - y2 * m2
            dy = dy2 * sz
            dy_s[...] = dy
            du_ref[g] = dy2 * y * (sg * (1.0 + z * (1.0 - sg)))
            xs = x_ref[g]
            bm, cm = x_ref[8 + g, :, 0:SSM_N], x_ref[8 + g, :, SSM_N:2 * SSM_N]
            cb = _dot_nt(cm, bm)
            cbt = _dot_nt(bm, cm)
            dtx, acx = colx_s[0, g], colx_s[1, g]
            sc = scx_s[g]
            a_end = sc[1:2, :]
            ex = jnp.exp(acx)
            wdx = jnp.exp(a_end - acx)
            eend = jnp.exp(a_end)
            rows = pltpu.roll(acs_t, (SSM_HEADS - 4 * g) % SSM_HEADS, 0)
            hp = hst_ref[0, g]
            dhn = dh_s[g]
            xt = xs * dtx
            xt_s[...] = xt
            ch = _dot(cm, hp)
            gy = dy * ex
            dcm = _dot_nt(gy, hp)
            dh_s[g] = _dot_tn(cm, gy) + dhn * eend
            q = _dot(bm, dhn)
            dbm = _dot_nt(xt * wdx, dhn)
            qx = q * xt * wdx
            v_end = jnp.sum(dhn * hp, axis=0, keepdims=True) * eend + jnp.sum(qx, axis=0, keepdims=True)
            dcb = jnp.zeros((CHUNK, CHUNK), f32)
            drows = jnp.zeros((SSM_HEADS, CHUNK), f32)
            for r in range(4):
                hd = slice(SSM_P * r, SSM_P * (r + 1))
                in_head = (lane >= SSM_P * r) & (lane < SSM_P * (r + 1))
                ac = acx[:, SSM_P * r:SSM_P * r + 1]
                ar = rows[r:r + 1, :]
                lam = jnp.exp(jnp.where(causal, ac - ar, -jnp.inf))
                lam_t = jnp.exp(jnp.where(causal_t, ar - ac, -jnp.inf))
                m = cb * lam
                m_t = cbt * lam_t
                dm = _dot_nt(dy_s[:, hd], xt_s[:, hd])
                dm_t = _dot_nt(xt_s[:, hd], dy_s[:, hd])
                dxt_s[:, hd] = _dot(m_t, dy_s[:, hd])
                dcb = dcb + dm * lam
                d_ac = jnp.sum(dm_t * m_t, axis=0, keepdims=True)
                d_ar = jnp.sum(dm * m, axis=0, keepdims=True)
                d_aend = jnp.sum(jnp.where(in_head, v_end, 0.0), axis=1, keepdims=True)
                drows = drows + jnp.where(sub32 == r, d_ac - d_ar + last * d_aend, 0.0)
            dxt = dxt_s[...] + q * wdx
            du_ref[8 + g] = sc[0:1, :] * dy + dxt * dtx
            ddtx_s[g] = dxt * xs
            dacx_s[g] = dy * ch * ex - qx
            ddx_s[g] = jnp.broadcast_to(jnp.sum(dy * xs, axis=0, keepdims=True), (8, TILE))
            du_ref[16 + g, :, 0:SSM_N] = dbm + _dot_tn(dcb, cm)
            du_ref[16 + g, :, SSM_N:2 * SSM_N] = dcm + _dot(dcb, bm)
            drow_s[...] += pltpu.roll(drows, (4 * g) % SSM_HEADS, 0)
            return carry

        lax.fori_loop(0, SSM_GROUPS, group, 0)
        ddt = jnp.zeros((CHUNK, SSM_HEADS), f32)
        dacs = jnp.zeros((CHUNK, SSM_HEADS), f32)
        ddsk = jnp.zeros((8, SSM_HEADS), f32)
        for g in range(SSM_GROUPS):
            col_g = _collapse_matrix(g)
            ddt = ddt + _dot_exact(ddtx_s[g], col_g, ((1,), (1,)), True, 2)
            dacs = dacs + _dot_exact(dacx_s[g], col_g, ((1,), (1,)), True, 2)
            ddsk = ddsk + _dot_exact(ddx_s[g], col_g, ((1,), (1,)), True, 2)
        upper = _tri(False)
        da = _dot_exact(upper, dacs, ((1,), (0,)), False) + _dot_exact(upper, drow_s[...], ((1,), (1,)), False)
        ddt = ddt + da * a_neg
        dalog_ref[...] += jnp.sum(da * dt, axis=0, keepdims=True) * a_neg
        ddtr = ddt * _sigmoid(dtr)
        ddtb_ref[...] += jnp.sum(ddtr, axis=0, keepdims=True)
        ddsk_ref[...] += ddsk[0:1, :]
        du_ref[SSM_NT - 1] = jnp.zeros((CHUNK, TILE), f32)
        du_ref[SSM_NT - 1, :, 0:SSM_HEADS] = ddtr

    def rc(c):
        return nc - 1 - c

    vec = pl.BlockSpec((1, SSM_HEADS), lambda c: (0, 0))
    return _pc(
        body, name, grid=(nc,),
        in_specs=_ssm_core_specs(nc, True) + [
            pl.BlockSpec((SSM_KT, CHUNK, TILE), lambda c: (0, rc(c), 0)),
            pl.BlockSpec((1, SSM_GROUPS, SSM_N, TILE), lambda c: (rc(c), 0, 0, 0)),
            pl.BlockSpec((SSM_KT, CHUNK, TILE), lambda c: (0, rc(c), 0))],
        out_specs=[pl.BlockSpec((SSM_NT, CHUNK, TILE), lambda c: (0, rc(c), 0)), vec, vec, vec,
                   pl.BlockSpec((SSM_KT, 1, TILE), lambda c: (0, 0, 0))],
        out_shape=[SDS((SSM_NT, t_len, TILE), f32), SDS((1, SSM_HEADS), f32), SDS((1, SSM_HEADS), f32),
                   SDS((1, SSM_HEADS), f32), SDS((SSM_KT, 1, TILE), f32)],
        scratch_shapes=[pltpu.VMEM((SSM_GROUPS, SSM_N, TILE), f32), pltpu.VMEM((2, SSM_GROUPS, CHUNK, TILE), f32),
                        pltpu.VMEM((SSM_GROUPS, 8, TILE), f32), pltpu.VMEM((CHUNK, TILE), f32),
                        pltpu.VMEM((CHUNK, TILE), f32), pltpu.VMEM((CHUNK, TILE), f32),
                        pltpu.VMEM((SSM_GROUPS, CHUNK, TILE), f32), pltpu.VMEM((SSM_GROUPS, CHUNK, TILE), f32),
                        pltpu.VMEM((SSM_GROUPS, 8, TILE), f32), pltpu.VMEM((SSM_HEADS, CHUNK), f32)],
        compiler_params=_params(("arbitrary",)),
    )(u, u, xbc, dtb, alog, dsk, gn, yp, hst, da3)


def _swap16(t):
    lane = lax.broadcasted_iota(jnp.int32, t.shape, 1) % 64
    return jnp.where(lane < 8, pltpu.roll(t, TILE - 8, 1), jnp.where(lane < 16, pltpu.roll(t, 8, 1), 0.0))


def _rope(t, cos_t, sin_t):
    return t * cos_t + _swap16(t) * sin_t


def _rope_bwd(g, cos_t, sin_t):
    return g * cos_t + _swap16(g * sin_t)


def _att_head(qh, kp, kc, vp, vc, sink, mask_p, mask_c):
    sp = jnp.where(mask_p, _dot_nt(qh, kp) * 0.125, -jnp.inf)
    sc = jnp.where(mask_c, _dot_nt(qh, kc) * 0.125, -jnp.inf)
    m = jnp.maximum(jnp.maximum(jnp.max(sp, axis=-1, keepdims=True), jnp.max(sc, axis=-1, keepdims=True)), sink)
    m = lax.stop_gradient(m)
    pp = jnp.exp(sp - m)
    pc = jnp.exp(sc - m)
    den = jnp.sum(pp, axis=-1, keepdims=True) + jnp.sum(pc, axis=-1, keepdims=True) + jnp.exp(sink - m)
    inv = 1.0 / den
    return _dot(pp * inv, vp) + _dot(pc * inv, vc)


def _att_masks(has_prev):
    row = lax.broadcasted_iota(jnp.int32, (CHUNK, CHUNK), 0)
    col = lax.broadcasted_iota(jnp.int32, (CHUNK, CHUNK), 1)
    return (col > row) & has_prev, col <= row


def _att_in_specs(nb, rev):
    def bidx(n):
        return (nb - 1 - n) if rev else n
    return [
        pl.BlockSpec((ATT_NT, CHUNK, TILE), lambda n: (0, bidx(n), 0)),
        pl.BlockSpec((2, CHUNK, TILE), lambda n: (2, jnp.maximum(bidx(n) - 1, 0), 0)),
        pl.BlockSpec((CHUNK, TILE), lambda n: (bidx(n), 0)),
        pl.BlockSpec((CHUNK, TILE), lambda n: (bidx(n), 0)),
        pl.BlockSpec((CHUNK, TILE), lambda n: (jnp.maximum(bidx(n) - 1, 0), 0)),
        pl.BlockSpec((CHUNK, TILE), lambda n: (jnp.maximum(bidx(n) - 1, 0), 0)),
        pl.BlockSpec((1, 16), lambda n: (0, 0)),
    ]


def _att_fwd(u, cos_t, sin_t, sinks, name):
    t_len = u.shape[1]
    nb = t_len // CHUNK

    def body(u_ref, prev_ref, cc_ref, sc_ref, cp_ref, sp_ref, sink_ref, a_ref,
             q_s, kp_s, kc_s, vp_s, vc_s, o_s):
        n = pl.program_id(0)
        mask_p, mask_c = _att_masks(n > 0)
        cos_c, sin_c = cc_ref[...], sc_ref[...]
        kc_s[...] = _rope(u_ref[4], cos_c, sin_c)
        kp_s[...] = _rope(prev_ref[0], cp_ref[...], sp_ref[...])
        vc_s[...] = u_ref[5]
        vp_s[...] = prev_ref[1]
        sinks = sink_ref[...]
        for g in range(4):
            q_s[...] = _rope(u_ref[g], cos_c, sin_c)
            kv = slice(64 * g, 64 * (g + 1))
            for r in range(4):
                hd = slice(64 * r, 64 * (r + 1))
                h = 4 * g + r
                o_s[:, hd] = _att_head(q_s[:, hd], kp_s[:, kv], kc_s[:, kv], vp_s[:, kv], vc_s[:, kv],
                                       sinks[:, h:h + 1], mask_p, mask_c)
            a_ref[g] = (o_s[...] * _silu(u_ref[6 + g])).astype(a_ref.dtype)

    return _pc(
        body, name, grid=(nb,),
        in_specs=_att_in_specs(nb, False),
        out_specs=pl.BlockSpec((ATT_KT, CHUNK, TILE), lambda n: (0, n, 0)),
        out_shape=SDS((ATT_KT, t_len, TILE), MXU_DTYPE),
        scratch_shapes=[pltpu.VMEM((CHUNK, TILE), f32)] * 6,
        compiler_params=_params(("arbitrary",)),
    )(u, u, cos_t, sin_t, cos_t, sin_t, sinks)


def _att_bwd(u, cos_t, sin_t, sinks, da, name):
    t_len = u.shape[1]
    nb = t_len // CHUNK

    def body(u_ref, prev_ref, cc_ref, sc_ref, cp_ref, sp_ref, sink_ref, da_ref, du_ref, dsink_ref,
             ck_s, cv_s, q_s, kp_s, kc_s, vp_s, vc_s, o_s, do_s, dq_s, dkp_s, dkc_s, dvp_s, dvc_s):
        step = pl.program_id(0)
        nn = nb - 1 - step

        @pl.when(step == 0)
        def _():
            ck_s[...] = jnp.zeros_like(ck_s)
            cv_s[...] = jnp.zeros_like(cv_s)
            dsink_ref[...] = jnp.zeros_like(dsink_ref)
        mask_p, mask_c = _att_masks(nn > 0)
        cos_c, sin_c = cc_ref[...], sc_ref[...]
        cos_p, sin_p = cp_ref[...], sp_ref[...]
        kc_s[...] = _rope(u_ref[4], cos_c, sin_c)
        kp_s[...] = _rope(prev_ref[0], cos_p, sin_p)
        vc_s[...] = u_ref[5]
        vp_s[...] = prev_ref[1]
        sinks = sink_ref[...]
        lane16 = lax.broadcasted_iota(jnp.int32, (1, 16), 1)
        dsink = jnp.zeros((1, 16), f32)
        att = functools.partial(_att_head, mask_p=mask_p, mask_c=mask_c)
        for g in range(4):
            q_s[...] = _rope(u_ref[g], cos_c, sin_c)
            gate = u_ref[6 + g]
            sg = _sigmoid(gate)
            dav = da_ref[g]
            do_s[...] = dav * (gate * sg)
            kv = slice(64 * g, 64 * (g + 1))
            dkp = jnp.zeros((CHUNK, 64), f32)
            dkc = jnp.zeros((CHUNK, 64), f32)
            dvp = jnp.zeros((CHUNK, 64), f32)
            dvc = jnp.zeros((CHUNK, 64), f32)
            for r in range(4):
                hd = slice(64 * r, 64 * (r + 1))
                h = 4 * g + r
                o_h, vjp_fn = jax.vjp(att, q_s[:, hd], kp_s[:, kv], kc_s[:, kv], vp_s[:, kv], vc_s[:, kv],
                                      sinks[:, h:h + 1])
                dq_h, dkp_h, dkc_h, dvp_h, dvc_h, ds_h = vjp_fn(do_s[:, hd])
                o_s[:, hd] = o_h
                dq_s[:, hd] = dq_h
                dkp, dkc, dvp, dvc = dkp + dkp_h, dkc + dkc_h, dvp + dvp_h, dvc + dvc_h
                dsink = dsink + ds_h * (lane16 == h).astype(f32)
            du_ref[6 + g] = dav * o_s[...] * (sg * (1.0 + gate * (1.0 - sg)))
            du_ref[g] = _rope_bwd(dq_s[...], cos_c, sin_c)
            dkp_s[:, kv] = dkp
            dkc_s[:, kv] = dkc
            dvp_s[:, kv] = dvp
            dvc_s[:, kv] = dvc
        du_ref[4] = _rope_bwd(dkc_s[...], cos_c, sin_c) + ck_s[...]
        du_ref[5] = dvc_s[...] + cv_s[...]
        ck_s[...] = _rope_bwd(dkp_s[...], cos_p, sin_p)
        cv_s[...] = dvp_s[...]
        dsink_ref[...] += dsink

    def rb(n):
        return nb - 1 - n

    return _pc(
        body, name, grid=(nb,),
        in_specs=_att_in_specs(nb, True) + [pl.BlockSpec((ATT_KT, CHUNK, TILE), lambda n: (0, rb(n), 0))],
        out_specs=[pl.BlockSpec((ATT_NT, CHUNK, TILE), lambda n: (0, rb(n), 0)),
                   pl.BlockSpec((1, 16), lambda n: (0, 0))],
        out_shape=[SDS((ATT_NT, t_len, TILE), f32), SDS((1, 16), f32)],
        scratch_shapes=[pltpu.VMEM((CHUNK, TILE), f32)] * 14,
        compiler_params=_params(("arbitrary",)),
    )(u, u, cos_t, sin_t, cos_t, sin_t, sinks, da)


ATT_SCALE = 0.125
ATT_ROWS = 4 * CHUNK


def _band_mask(has_prev):
    row = lax.broadcasted_iota(jnp.int32, (CHUNK, 2 * CHUNK), 0)
    col = lax.broadcasted_iota(jnp.int32, (CHUNK, 2 * CHUNK), 1)
    return ((col < CHUNK) & (col > row) & has_prev) | ((col >= CHUNK) & (col - CHUNK <= row))


def _stack_heads(ref):
    return jnp.concatenate([ref[:, 64 * r:64 * (r + 1)] for r in range(4)], axis=0)


def _sink_col(sinks, g):
    return [sinks[:, 4 * g + r:4 * g + r + 1] for r in range(4)]


def _softmax_rows(s_s, pn_s, sink, mask):
    ps = []
    for r in range(4):
        rs = slice(CHUNK * r, CHUNK * (r + 1))
        sk = sink[r]
        s = jnp.where(mask, s_s[rs, :] * ATT_SCALE, -jnp.inf)
        m = jnp.maximum(jnp.max(s, axis=-1, keepdims=True), sk)
        p = jnp.exp(s - m)
        e_sink = jnp.exp(sk - m)
        inv = 1.0 / (jnp.sum(p, axis=-1, keepdims=True) + e_sink)
        pn_s[rs, :] = p * inv
        ps.append(e_sink * inv)
    return ps


def _att2_fwd(u, cos_t, sin_t, sinks, name):
    t_len = u.shape[1]
    nb = t_len // CHUNK

    def body(u_ref, prev_ref, cc_ref, sc_ref, cp_ref, sp_ref, sink_ref, a_ref,
             q_s, kp_s, kc_s, vp_s, vc_s, o_s, s_s, pn_s):
        n = pl.program_id(0)
        mask = _band_mask(n > 0)
        cos_c, sin_c = cc_ref[...], sc_ref[...]
        kc_s[...] = _rope(u_ref[4], cos_c, sin_c)
        kp_s[...] = _rope(prev_ref[0], cp_ref[...], sp_ref[...])
        vc_s[...] = u_ref[5]
        vp_s[...] = prev_ref[1]
        sinks = sink_ref[...]
        for g in range(4):
            q_g, o_g, s_g, pn_g = q_s.at[g], o_s.at[g], s_s.at[g], pn_s.at[g]
            q_g[...] = _rope(u_ref[g], cos_c, sin_c)
            kv = slice(64 * g, 64 * (g + 1))
            kb = jnp.concatenate([kp_s[:, kv], kc_s[:, kv]], axis=0)
            vb = jnp.concatenate([vp_s[:, kv], vc_s[:, kv]], axis=0)
            s_g[...] = _dot_nt(_stack_heads(q_g), kb)
            _softmax_rows(s_g, pn_g, _sink_col(sinks, g), mask)
            o = _dot(pn_g[...], vb)
            for r in range(4):
                o_g[:, 64 * r:64 * (r + 1)] = o[CHUNK * r:CHUNK * (r + 1), :]
            a_ref[g] = (o_g[...] * _silu(u_ref[6 + g])).astype(a_ref.dtype)

    return _pc(
        body, name, grid=(nb,),
        in_specs=_att_in_specs(nb, False),
        out_specs=pl.BlockSpec((ATT_KT, CHUNK, TILE), lambda n: (0, n, 0)),
        out_shape=SDS((ATT_KT, t_len, TILE), MXU_DTYPE),
        scratch_shapes=[pltpu.VMEM((4, CHUNK, TILE), f32)] + [pltpu.VMEM((CHUNK, TILE), f32)] * 4
                       + [pltpu.VMEM((4, CHUNK, TILE), f32)] + [pltpu.VMEM((4, ATT_ROWS, 2 * CHUNK), f32)] * 2,
        compiler_params=_params(("arbitrary",)),
    )(u, u, cos_t, sin_t, cos_t, sin_t, sinks)


def _att2_bwd(u, cos_t, sin_t, sinks, da, name):
    t_len = u.shape[1]
    nb = t_len // CHUNK

    def body(u_ref, prev_ref, cc_ref, sc_ref, cp_ref, sp_ref, sink_ref, da_ref, du_ref, dsink_ref,
             ck_s, cv_s, kp_s, kc_s, vp_s, vc_s, q_s, o_s, do_s, dq_s, s_s, pn_s, dp_s, dkt_s, dvt_s):
        step = pl.program_id(0)
        nn = nb - 1 - step

        @pl.when(step == 0)
        def _():
            ck_s[...] = jnp.zeros_like(ck_s)
            cv_s[...] = jnp.zeros_like(cv_s)
            dsink_ref[...] = jnp.zeros_like(dsink_ref)
        mask = _band_mask(nn > 0)
        cos_c, sin_c = cc_ref[...], sc_ref[...]
        cos_p, sin_p = cp_ref[...], sp_ref[...]
        kc_s[...] = _rope(u_ref[4], cos_c, sin_c)
        kp_s[...] = _rope(prev_ref[0], cos_p, sin_p)
        vc_s[...] = u_ref[5]
        vp_s[...] = prev_ref[1]
        sinks = sink_ref[...]
        lane16 = lax.broadcasted_iota(jnp.int32, (1, 16), 1)
        dsink = jnp.zeros((1, 16), f32)
        for g in range(4):
            q_g, o_g, do_g, dq_g = q_s.at[g], o_s.at[g], do_s.at[g], dq_s.at[g]
            s_g, pn_g, dp_g = s_s.at[g], pn_s.at[g], dp_s.at[g]
            q_g[...] = _rope(u_ref[g], cos_c, sin_c)
            gate = u_ref[6 + g]
            sg = _sigmoid(gate)
            dav = da_ref[g]
            do_g[...] = dav * (gate * sg)
            kv = slice(64 * g, 64 * (g + 1))
            kb = jnp.concatenate([kp_s[:, kv], kc_s[:, kv]], axis=0)
            vb = jnp.concatenate([vp_s[:, kv], vc_s[:, kv]], axis=0)
            q_st = _stack_heads(q_g)
            do_st = _stack_heads(do_g)
            s_g[...] = _dot_nt(q_st, kb)
            p_sink = _softmax_rows(s_g, pn_g, _sink_col(sinks, g), mask)
            o = _dot(pn_g[...], vb)
            dvt_s[64 * g:64 * (g + 1), :] = _dot_tn(do_st, pn_g[...])
            dp_g[...] = _dot_nt(do_st, vb)
            for r in range(4):
                rs = slice(CHUNK * r, CHUNK * (r + 1))
                o_g[:, 64 * r:64 * (r + 1)] = o[rs, :]
                delta = jnp.sum(do_st[rs, :] * o[rs, :], axis=-1, keepdims=True)
                dp_g[rs, :] = pn_g[rs, :] * (dp_g[rs, :] - delta) * ATT_SCALE
                ds_h = -jnp.sum(p_sink[r] * delta, axis=0, keepdims=True)
                dsink = dsink + ds_h * (lane16 == 4 * g + r).astype(f32)
            ds = dp_g[...]
            dq = _dot(ds, kb)
            for r in range(4):
                dq_g[:, 64 * r:64 * (r + 1)] = dq[CHUNK * r:CHUNK * (r + 1), :]
            dkt_s[64 * g:64 * (g + 1), :] = _dot_tn(q_st, ds)
            du_ref[6 + g] = dav * o_g[...] * (sg * (1.0 + gate * (1.0 - sg)))
            du_ref[g] = _rope_bwd(dq_g[...], cos_c, sin_c)
        dk = dkt_s[...].T
        dv = dvt_s[...].T
        du_ref[4] = _rope_bwd(dk[CHUNK:2 * CHUNK, :], cos_c, sin_c) + ck_s[...]
        du_ref[5] = dv[CHUNK:2 * CHUNK, :] + cv_s[...]
        ck_s[...] = _rope_bwd(dk[0:CHUNK, :], cos_p, sin_p)
        cv_s[...] = dv[0:CHUNK, :]
        dsink_ref[...] += dsink

    def rb(n):
        return nb - 1 - n

    return _pc(
        body, name, grid=(nb,),
        in_specs=_att_in_specs(nb, True) + [pl.BlockSpec((ATT_KT, CHUNK, TILE), lambda n: (0, rb(n), 0))],
        out_specs=[pl.BlockSpec((ATT_NT, CHUNK, TILE), lambda n: (0, rb(n), 0)),
                   pl.BlockSpec((1, 16), lambda n: (0, 0))],
        out_shape=[SDS((ATT_NT, t_len, TILE), f32), SDS((1, 16), f32)],
        scratch_shapes=[pltpu.VMEM((CHUNK, TILE), f32)] * 6 + [pltpu.VMEM((4, CHUNK, TILE), f32)] * 4
                       + [pltpu.VMEM((4, ATT_ROWS, 2 * CHUNK), f32)] * 3
                       + [pltpu.VMEM((2 * CHUNK, 2 * CHUNK), f32)] * 2,
        compiler_params=_params(("arbitrary",)),
    )(u, u, cos_t, sin_t, cos_t, sin_t, sinks, da)


_HBM = pl.BlockSpec(memory_space=pltpu.HBM)


def _all_gather_big(shard):
    rows, width = shard.shape

    def body(x_ref, out_ref, send_sems, recv_sems, local_sem):
        x, y, c = lax.axis_index("x"), lax.axis_index("y"), lax.axis_index("c")
        me, sibling = (x, y, c), (x, y, 1 - c)
        chips = [(1 - x, y), (x, 1 - y), (1 - x, 1 - y)]

        def slot(px, py, pc):
            return out_ref.at[4 * px + 2 * py + pc]

        def copy(k, block, to, src=None):
            return pltpu.make_async_remote_copy(
                src_ref=slot(*block) if src is None else src, dst_ref=slot(*block),
                send_sem=send_sems.at[k], recv_sem=recv_sems.at[k], device_id=to, device_id_type=MESH)

        mine = pltpu.make_async_copy(x_ref, slot(*me), local_sem)
        mine.start()
        first = [copy(0, me, sibling, src=x_ref)]
        first += [copy(1 + j, me, (*chip, c), src=x_ref) for j, chip in enumerate(chips)]
        for cp in first:
            cp.start()
        passed = [copy(4 + j, (*chip, c), sibling) for j, chip in enumerate(chips)]
        for j, chip in enumerate(chips):
            copy(1 + j, (*chip, c), me).wait_recv()
            passed[j].start()
        copy(0, sibling, me).wait_recv()
        for j, chip in enumerate(chips):
            copy(4 + j, (*chip, 1 - c), me).wait_recv()
        for cp in first + passed:
            cp.wait_send()
        mine.wait()

    return _pc(
        body, "all_gather_big",
        in_specs=[_HBM], out_specs=_HBM,
        out_shape=SDS((N_DEV, rows, width), shard.dtype),
        scratch_shapes=[pltpu.SemaphoreType.DMA((7,)), pltpu.SemaphoreType.DMA((7,)), pltpu.SemaphoreType.DMA],
    )(shard)


def _all_gather_direct(block, name):
    rows, width = block.shape

    def body(x_ref, out_ref, send_sems, recv_sems, local_sem):
        x, y, c = lax.axis_index("x"), lax.axis_index("y"), lax.axis_index("c")
        my_slot = 4 * x + 2 * y + c

        def peer(k):
            return (1 - x if k & 4 else x, 1 - y if k & 2 else y, 1 - c if k & 1 else c)

        def copy(k):
            px, py, pc = peer(k)
            return pltpu.make_async_remote_copy(
                src_ref=x_ref, dst_ref=out_ref.at[my_slot], send_sem=send_sems.at[k - 1], recv_sem=recv_sems.at[k - 1],
                device_id=(px, py, pc), device_id_type=MESH)

        def arrival(k):
            px, py, pc = peer(k)
            return pltpu.make_async_remote_copy(
                src_ref=x_ref, dst_ref=out_ref.at[4 * px + 2 * py + pc], send_sem=send_sems.at[k - 1],
                recv_sem=recv_sems.at[k - 1], device_id=(px, py, pc), device_id_type=MESH)

        mine = pltpu.make_async_copy(x_ref, out_ref.at[my_slot], local_sem)
        mine.start()
        for k in range(1, N_DEV):
            copy(k).start()
        for k in range(1, N_DEV):
            arrival(k).wait_recv()
        for k in range(1, N_DEV):
            copy(k).wait_send()
        mine.wait()

    return _pc(
        body, name,
        in_specs=[_HBM], out_specs=_HBM,
        out_shape=SDS((N_DEV, rows, width), block.dtype),
        scratch_shapes=[pltpu.SemaphoreType.DMA((7,)), pltpu.SemaphoreType.DMA((7,)), pltpu.SemaphoreType.DMA],
    )(block)


def _exchange_sibling(g):
    _, rows, width = g.shape
    nchip = N_DEV // 2

    def body(g_ref, out_ref, send_sems, recv_sems):
        x, y, c = lax.axis_index("x"), lax.axis_index("y"), lax.axis_index("c")
        cps = [pltpu.make_async_remote_copy(
            src_ref=g_ref.at[2 * k + 1 - c], dst_ref=out_ref.at[k], send_sem=send_sems.at[k], recv_sem=recv_sems.at[k],
            device_id=(x, y, 1 - c), device_id_type=MESH) for k in range(nchip)]
        for cp in cps:
            cp.start()
        for cp in cps:
            cp.wait()

    return _pc(
        body, "rs_sibling",
        in_specs=[_HBM], out_specs=_HBM,
        out_shape=SDS((nchip, rows, width), g.dtype),
        scratch_shapes=[pltpu.SemaphoreType.DMA((nchip,)), pltpu.SemaphoreType.DMA((nchip,))],
    )(g)


def _pair_sum(g, r1, cidx, tr):
    _, rows, width = g.shape
    nchip = N_DEV // 2

    def body(c_ref, g_ref, r_ref, o_ref):
        o_ref[...] = (g_ref[...].astype(f32) + r_ref[...].astype(f32)).astype(o_ref.dtype)

    return pl.pallas_call(
        body, name="rs_pair_sum",
        grid_spec=pltpu.PrefetchScalarGridSpec(
            num_scalar_prefetch=1, grid=(nchip, rows // tr),
            in_specs=[pl.BlockSpec((1, tr, width), lambda k, i, c_ref: (2 * k + c_ref[0], i, 0)),
                      pl.BlockSpec((1, tr, width), lambda k, i, c_ref: (k, i, 0))],
            out_specs=pl.BlockSpec((1, tr, width), lambda k, i, c_ref: (k, i, 0))),
        out_shape=SDS((nchip, rows, width), g.dtype),
        compiler_params=_params(("arbitrary", "arbitrary")),
    )(cidx, g, r1)


def _exchange_chips(p):
    nchip, rows, width = p.shape

    def body(p_ref, out_ref, send_sems, recv_sems, local_sem):
        x, y, c = lax.axis_index("x"), lax.axis_index("y"), lax.axis_index("c")
        my_chip = 2 * x + y
        chips = [(1 - x, y), (x, 1 - y), (1 - x, 1 - y)]

        def copy(j):
            px, py = chips[j]
            return pltpu.make_async_remote_copy(
                src_ref=p_ref.at[2 * px + py], dst_ref=out_ref.at[my_chip], send_sem=send_sems.at[j],
                recv_sem=recv_sems.at[j], device_id=(px, py, c), device_id_type=MESH)

        def arrival(j):
            px, py = chips[j]
            return pltpu.make_async_remote_copy(
                src_ref=p_ref.at[my_chip], dst_ref=out_ref.at[2 * px + py], send_sem=send_sems.at[j],
                recv_sem=recv_sems.at[j], device_id=(px, py, c), device_id_type=MESH)

        mine = pltpu.make_async_copy(p_ref.at[my_chip], out_ref.at[my_chip], local_sem)
        mine.start()
        for j in range(3):
            copy(j).start()
        for j in range(3):
            arrival(j).wait_recv()
        for j in range(3):
            copy(j).wait_send()
        mine.wait()

    return _pc(
        body, "rs_chips",
        in_specs=[_HBM], out_specs=_HBM,
        out_shape=SDS((nchip, rows, width), p.dtype),
        scratch_shapes=[pltpu.SemaphoreType.DMA((3,)), pltpu.SemaphoreType.DMA((3,)), pltpu.SemaphoreType.DMA],
    )(p)


def _adamw(parts, w, m, v, tr, name):
    n, rows, width = parts.shape
    c1 = 1.0 / (1.0 - ADAM_B1 ** ADAM_STEP)
    c2 = 1.0 / (1.0 - ADAM_B2 ** ADAM_STEP)

    def body(p_ref, w_ref, m_ref, v_ref, g_ref, d_ref, mo_ref, vo_ref):
        g = p_ref[0].astype(f32)
        for k in range(1, n):
            g = g + p_ref[k].astype(f32)
        mn = ADAM_B1 * m_ref[...] + (1.0 - ADAM_B1) * g
        vn = ADAM_B2 * v_ref[...] + (1.0 - ADAM_B2) * (g * g)
        g_ref[...] = g
        mo_ref[...] = mn
        vo_ref[...] = vn
        d_ref[...] = -ADAM_LR * ((mn * c1) / (jnp.sqrt(vn * c2) + ADAM_EPS) + ADAM_WD * w_ref[...])

    blk = pl.BlockSpec((tr, width), lambda i: (i, 0))
    return _pc(
        body, name, grid=(rows // tr,),
        in_specs=[pl.BlockSpec((n, tr, width), lambda i: (0, i, 0)), blk, blk, blk],
        out_specs=[blk, blk, blk, blk],
        out_shape=[SDS((rows, width), f32)] * 4,
        compiler_params=_params(("arbitrary",)),
    )(parts, w, m, v)


def _pack_big(ssm_w_in, ssm_w_out, att_w_in, att_w_out, conv_w=None):
    parts = [ssm_w_in.reshape(ROWS_SSM_IN, 1024), ssm_w_out.reshape(ROWS_SSM_OUT, 1024),
             att_w_in.reshape(ROWS_ATT_IN, 1024), att_w_out.reshape(ROWS_ATT_OUT, 1024)]
    if conv_w is None:
        parts.append(jnp.zeros((ROWS_CONV, 1024), f32))
    else:
        parts.append(jnp.concatenate([conv_w.reshape(4, 1024), jnp.zeros((4, 1024), f32)], axis=0))
    return jnp.concatenate(parts, axis=0)


def _unpack_big(p):
    o = 0
    out = []
    for rows, shape in ((ROWS_SSM_IN, (2, 1024, 772)), (ROWS_SSM_OUT, (2, 256, 1024)),
                        (ROWS_ATT_IN, (2, 1024, 320)), (ROWS_ATT_OUT, (2, 128, 1024))):
        out.append(p[o:o + rows].reshape(shape))
        o += rows
    out.append(p[o:o + 4].reshape(2, 4, 512))
    return out


def _pack_grads(d_ssm_w_in, d_ssm_w_out, d_att_w_in, d_att_w_out, d_conv_w):
    wire = lambda t: t.astype(MXU_DTYPE)
    a = jnp.transpose(wire(d_ssm_w_in).reshape(2, 1024, 8, 772), (2, 0, 1, 3)).reshape(8, ROWS_SSM_IN, 1024)
    b = jnp.transpose(wire(d_ssm_w_out).reshape(2, 8, 256, 1024), (1, 0, 2, 3)).reshape(8, ROWS_SSM_OUT, 1024)
    c = jnp.transpose(wire(d_att_w_in).reshape(2, 1024, 8, 320), (2, 0, 1, 3)).reshape(8, ROWS_ATT_IN, 1024)
    d = jnp.transpose(wire(d_att_w_out).reshape(2, 8, 128, 1024), (1, 0, 2, 3)).reshape(8, ROWS_ATT_OUT, 1024)
    e = jnp.transpose(wire(d_conv_w).reshape(2, 4, 8, 512), (2, 0, 1, 3)).reshape(8, 4, 1024)
    e = jnp.concatenate([e, jnp.zeros((8, 4, 1024), MXU_DTYPE)], axis=1)
    return jnp.concatenate([a, b, c, d, e], axis=1)


def _pad8(a):
    return jnp.pad(a, ((0, 8 - a.shape[0]), (0, 0)))


def _pack_small(pre_norm, post_norm, conv_b, gate_norm, dt_bias, a_log, d_skip, sinks, extra=None):
    row = jnp.concatenate([dt_bias.reshape(1, 64), a_log.reshape(1, 64), d_skip.reshape(1, 64), sinks.reshape(1, 32),
                           jnp.zeros((1, 1024 - 224), f32)], axis=1)
    if extra is not None:
        row = row + jnp.pad(extra.reshape(1, 1), ((0, 0), (224, 1024 - 225)))
    return jnp.concatenate([_pad8(pre_norm.reshape(4, 1024)), _pad8(post_norm.reshape(4, 1024)),
                            conv_b.reshape(8, 1024), _pad8(gate_norm.reshape(4, 1024)), _pad8(row)], axis=0)


def _unpack_small(p):
    row = p[32]
    return (p[0:4], p[8:12], p[16:24].reshape(2, 4096), row[0:64].reshape(2, 32), row[64:128].reshape(2, 32),
            row[128:192].reshape(2, 32), p[24:28].reshape(2, 2048), row[192:224].reshape(2, 16))


def _ssm_w_in_tiles(w):
    wb = w[:, 4096:5120].reshape(1024, 8, 128)
    wc = w[:, 5120:6144].reshape(1024, 8, 128)
    wbc = jnp.concatenate([wb, wc], axis=2).reshape(1024, 2048)
    return jnp.concatenate([w[:, 0:4096], wbc, w[:, 6144:6176], jnp.zeros((1024, 224), w.dtype)], axis=1)


def _ssm_w_in_untile(dw):
    dbc = dw[:, 4096:6144].reshape(1024, 8, 256)
    return jnp.concatenate([dw[:, 0:4096], dbc[:, :, 0:128].reshape(1024, 1024), dbc[:, :, 128:256].reshape(1024, 1024),
                            dw[:, 6144:6176]], axis=1)


def _conv_tiles(cw):
    k = cw.shape[0]
    xs = jnp.transpose(cw[:, 0:2048].reshape(k, 8, 256), (1, 0, 2))
    b = cw[:, 2048:3072].reshape(k, 8, 128)
    c = cw[:, 3072:4096].reshape(k, 8, 128)
    bc = jnp.transpose(jnp.concatenate([b, c], axis=2), (1, 0, 2))
    return jnp.concatenate([xs, bc], axis=0)


def _conv_untile(t):
    k = t.shape[1]
    xs = jnp.transpose(t[0:8], (1, 0, 2)).reshape(k, 2048)
    bc = jnp.transpose(t[8:16], (1, 0, 2))
    return jnp.concatenate([xs, bc[:, :, 0:128].reshape(k, 1024), bc[:, :, 128:256].reshape(k, 1024)], axis=1)


def _rope_tables(positions):
    inv = ROPE_THETA ** (-jnp.arange(0, 16, 2, dtype=f32) / 16)
    ang = positions.astype(f32).reshape(-1, 1) * inv
    cos, sin = jnp.cos(ang), jnp.sin(ang)
    t_len = ang.shape[0]
    cos64 = jnp.concatenate([cos, cos, jnp.ones((t_len, 48), f32)], axis=1)
    sin64 = jnp.concatenate([-sin, sin, jnp.zeros((t_len, 48), f32)], axis=1)
    return jnp.tile(cos64, (1, 4)), jnp.tile(sin64, (1, 4))


def kernel(x, positions, pre_norm, post_norm, ssm_w_in, ssm_conv_w, ssm_conv_b, ssm_dt_bias, ssm_a_log, ssm_d, ssm_gate_norm, ssm_w_out, att_w_in, att_sinks, att_w_out, loss_target, m_pre_norm, m_post_norm, m_ssm_w_in, m_ssm_conv_w, m_ssm_conv_b, m_ssm_dt_bias, m_ssm_a_log, m_ssm_d, m_ssm_gate_norm, m_ssm_w_out, m_att_w_in, m_att_sinks, m_att_w_out, v_pre_norm, v_post_norm, v_ssm_w_in, v_ssm_conv_w, v_ssm_conv_b, v_ssm_dt_bias, v_ssm_a_log, v_ssm_d, v_ssm_gate_norm, v_ssm_w_out, v_att_w_in, v_att_sinks, v_att_w_out):
    t_len = x.shape[1]
    tm = min(1024, t_len)
    xin = x.reshape(t_len, D_MODEL)
    tgt = loss_target.reshape(t_len, D_MODEL)
    cidx = lax.axis_index("c").astype(jnp.int32).reshape(1)

    w_local = _pack_big(ssm_w_in, ssm_w_out, att_w_in, att_w_out)
    gathered = _all_gather_big(w_local.astype(MXU_DTYPE))
    conv_local = jnp.concatenate([ssm_conv_w.reshape(4, 1024), jnp.zeros((4, 1024), f32)], axis=0)
    conv_all = _all_gather_direct(conv_local, "all_gather_conv")[:, 0:4]
    o = 0
    g_ssm_in = gathered[:, o:o + ROWS_SSM_IN].reshape(8, 2, 1024, 772); o += ROWS_SSM_IN
    g_ssm_out = gathered[:, o:o + ROWS_SSM_OUT].reshape(8, 2, 256, 1024); o += ROWS_SSM_OUT
    g_att_in = gathered[:, o:o + ROWS_ATT_IN].reshape(8, 2, 1024, 320); o += ROWS_ATT_IN
    g_att_out = gathered[:, o:o + ROWS_ATT_OUT].reshape(8, 2, 128, 1024)
    w_ssm_in = jnp.transpose(g_ssm_in, (1, 2, 0, 3)).reshape(2, 1024, SSM_IN)
    w_ssm_out = jnp.transpose(g_ssm_out, (1, 0, 2, 3)).reshape(2, SSM_INNER, 1024)
    w_att_in = jnp.transpose(g_att_in, (1, 2, 0, 3)).reshape(2, 1024, ATT_IN)
    w_att_out = jnp.transpose(g_att_out, (1, 0, 2, 3)).reshape(2, 1024, 1024)
    conv_w = jnp.transpose(conv_all.reshape(8, 2, 4, 512), (1, 2, 0, 3)).reshape(2, 4, 4096)
    cos_t, sin_t = _rope_tables(positions)

    saved = []
    xc = xin
    for i in range(4):
        j = i // 2
        wn_pre, wn_post = pre_norm[i].reshape(1, D_MODEL), post_norm[i].reshape(1, D_MODEL)
        if i % 2 == 0:
            w_in = _ssm_w_in_tiles(w_ssm_in[j])
            cw, cb = _conv_tiles(conv_w[j]), _conv_tiles(ssm_conv_b[j].reshape(1, 4096))
            dtb, alog, dsk = ssm_dt_bias[j].reshape(1, 32), ssm_a_log[j].reshape(1, 32), ssm_d[j].reshape(1, 32)
            gn = ssm_gate_norm[j].reshape(SSM_KT, 1, TILE)
            u, h = _mm_in(xc, wn_pre, w_in, 5, tm, f"ssm_in_{j}")
            xbc = _conv_fwd(u, cw, cb, f"ssm_conv_{j}")
            a3, yp, hst = _ssm2_fwd(u, xbc, dtb, alog, dsk, gn, f"ssm_core_{j}")
            y, xn = _mm_out(a3, w_ssm_out[j], xc, wn_post, 4, tm, f"ssm_out_{j}")
            saved.append(dict(x=xc, u=u, h=h, a=a3, yp=yp, hst=hst, y=y, w_in=w_in, cw=cw, cb=cb, dtb=dtb, alog=alog,
                              dsk=dsk, gn=gn, xbc=xbc))
        else:
            sinks = att_sinks[j].reshape(1, 16)
            u, h = _mm_in(xc, wn_pre, w_att_in[j], 5, tm, f"att_in_{j}")
            a = _att2_fwd(u, cos_t, sin_t, sinks, f"att_core_{j}")
            y, xn = _mm_out(a, w_att_out[j], xc, wn_post, 4, tm, f"att_out_{j}")
            saved.append(dict(x=xc, u=u, h=h, a=a, y=y, sinks=sinks))
        xc = xn

    dx, loss_part = _loss_grad(xc, tgt, tm)

    d_pre, d_post = [None] * 4, [None] * 4
    d_ssm_in, d_ssm_out, d_att_in, d_att_out = [None] * 2, [None] * 2, [None] * 2, [None] * 2
    d_cw, d_cb, d_dtb, d_alog, d_dsk, d_gn, d_sinks = ([None] * 2 for _ in range(7))
    for i in reversed(range(4)):
        j = i // 2
        s = saved[i]
        wn_pre, wn_post = pre_norm[i].reshape(1, D_MODEL), post_norm[i].reshape(1, D_MODEL)
        if i % 2 == 0:
            da3, dy, d_post[i] = _mm_dout(s["y"], dx, wn_post, w_ssm_out[j], 4, tm, f"ssm_dout_{j}")
            d_ssm_out[j] = _dw_rows(s["a"], dy, 4, tm, f"ssm_dwout_{j}")
            du, d_dtb[j], d_alog[j], d_dsk[j], dgn = _ssm2_bwd(
                s["u"], s["xbc"], s["yp"], s["hst"], da3, s["dtb"], s["alog"], s["dsk"], s["gn"],
                f"ssm_core_bwd_{j}")
            du, dcw, dcb = _conv_bwd(s["u"], du, s["cw"], s["cb"], f"ssm_conv_bwd_{j}")
            d_cw[j], d_cb[j], d_gn[j] = _conv_untile(dcw), _conv_untile(dcb), dgn.reshape(1, SSM_INNER)
            d_ssm_in[j] = _ssm_w_in_untile(_dw_cols(s["h"], du, 5, tm, f"ssm_dwin_{j}"))
            dx, d_pre[i] = _mm_dh(du, s["w_in"], s["x"], dx, wn_pre, 5, tm, f"ssm_dh_{j}")
        else:
            da, dy, d_post[i] = _mm_dout(s["y"], dx, wn_post, w_att_out[j], 4, tm, f"att_dout_{j}")
            d_att_out[j] = _dw_rows(s["a"], dy, 4, tm, f"att_dwout_{j}")
            du, d_sinks[j] = _att2_bwd(s["u"], cos_t, sin_t, s["sinks"], da, f"att_core_bwd_{j}")
            d_att_in[j] = _dw_cols(s["h"], du, 5, tm, f"att_dwin_{j}")
            dx, d_pre[i] = _mm_dh(du, w_att_in[j], s["x"], dx, wn_pre, 5, tm, f"att_dh_{j}")

    g2 = _pack_grads(jnp.stack(d_ssm_in), jnp.stack(d_ssm_out), jnp.stack(d_att_in), jnp.stack(d_att_out),
                     jnp.stack(d_cw))
    r1 = _exchange_sibling(g2)
    pair = _pair_sum(g2, r1, cidx, ROWS_BIG // 5)
    parts = _exchange_chips(pair)
    w_p = _pack_big(ssm_w_in, ssm_w_out, att_w_in, att_w_out, ssm_conv_w)
    m_p = _pack_big(m_ssm_w_in, m_ssm_w_out, m_att_w_in, m_att_w_out, m_ssm_conv_w)
    v_p = _pack_big(v_ssm_w_in, v_ssm_w_out, v_att_w_in, v_att_w_out, v_ssm_conv_w)
    big = [_unpack_big(t) for t in _adamw(parts, w_p, m_p, v_p, ROWS_BIG // 5, "adamw_big")]

    small_local = _pack_small(jnp.concatenate(d_pre, axis=0), jnp.concatenate(d_post, axis=0),
                              jnp.concatenate(d_cb, axis=0), jnp.concatenate(d_gn, axis=0),
                              jnp.concatenate(d_dtb, axis=0), jnp.concatenate(d_alog, axis=0),
                              jnp.concatenate(d_dsk, axis=0), jnp.concatenate(d_sinks, axis=0), loss_part[0, 0])
    small_all = _all_gather_direct(small_local, "all_gather_small")
    ws = _pack_small(pre_norm, post_norm, ssm_conv_b, ssm_gate_norm, ssm_dt_bias, ssm_a_log, ssm_d, att_sinks)
    ms = _pack_small(m_pre_norm, m_post_norm, m_ssm_conv_b, m_ssm_gate_norm, m_ssm_dt_bias, m_ssm_a_log, m_ssm_d,
                     m_att_sinks)
    vs = _pack_small(v_pre_norm, v_post_norm, v_ssm_conv_b, v_ssm_gate_norm, v_ssm_dt_bias, v_ssm_a_log, v_ssm_d,
                     v_att_sinks)
    small4 = _adamw(small_all, ws, ms, vs, ROWS_SMALL, "adamw_small")
    loss = small4[0][32, 224]
    small = [_unpack_small(t) for t in small4]

    outs = [loss, dx.reshape(1, t_len, D_MODEL)]
    for k in range(4):
        b_ssm_in, b_ssm_out, b_att_in, b_att_out, b_conv = big[k]
        s_pre, s_post, s_cb, s_dtb, s_alog, s_d, s_gn, s_sinks = small[k]
        outs += [s_pre, s_post, b_ssm_in, b_conv, s_cb, s_dtb, s_alog, s_d, s_gn, b_ssm_out, b_att_in, s_sinks,
                 b_att_out]
    return tuple(outs)
```

```python
import functools

import jax
import jax.numpy as jnp
from jax import lax
from jax.experimental import pallas as pl
from jax.experimental.pallas import tpu as pltpu

f32 = jnp.float32
MXU_DTYPE = jnp.bfloat16
SDS = jax.ShapeDtypeStruct
MESH = pl.DeviceIdType.MESH

D_MODEL = 1024
EPS = 1e-6
TILE = 256
CHUNK = 128
SSM_HEADS = 32
SSM_GROUPS = 8
SSM_P = 64
SSM_N = 128
SSM_INNER = 2048
SSM_IN = 6176
SSM_NT = 25
SSM_KT = 8
ATT_NT = 10
ATT_KT = 4
ATT_IN = 2560
ROPE_THETA = 500000.0
N_DEV = 8
VMEM_LIMIT = 56 * 1024 * 1024

ADAM_LR = 0.001
ADAM_B1 = 0.9
ADAM_B2 = 0.999
ADAM_EPS = 1e-08
ADAM_WD = 0.01
ADAM_STEP = 10

ROWS_SSM_IN = 2 * 1024 * 772 // 1024
ROWS_SSM_OUT = 2 * 256
ROWS_ATT_IN = 2 * 1024 * 320 // 1024
ROWS_ATT_OUT = 2 * 128
ROWS_CONV = 8
ROWS_BIG = ROWS_SSM_IN + ROWS_SSM_OUT + ROWS_ATT_IN + ROWS_ATT_OUT + ROWS_CONV
ROWS_SMALL = 40


def _pc(body, name, **kw):
    return pl.pallas_call(body, name=name, **kw)


def _params(sem):
    return pltpu.CompilerParams(dimension_semantics=sem, vmem_limit_bytes=VMEM_LIMIT)


def _sigmoid(x):
    return 1.0 / (1.0 + jnp.exp(-x))


def _silu(x):
    return x * _sigmoid(x)


def _softplus(x):
    return jnp.maximum(x, 0.0) + jnp.log(1.0 + jnp.exp(-jnp.abs(x)))


def _mx(x):
    return x.astype(MXU_DTYPE)


def _dot(a, b):
    return jnp.dot(_mx(a), _mx(b), preferred_element_type=f32)


def _dot_nt(a, b):
    return lax.dot_general(_mx(a), _mx(b), (((1,), (1,)), ((), ())), preferred_element_type=f32)


def _dot_tn(a, b):
    return lax.dot_general(_mx(a), _mx(b), (((0,), (0,)), ((), ())), preferred_element_type=f32)


def _rms_fwd(x, w):
    r = lax.rsqrt(jnp.mean(x * x, axis=-1, keepdims=True) + EPS)
    return x * r * w


def _rms_bwd(x, w, dy):
    r = lax.rsqrt(jnp.mean(x * x, axis=-1, keepdims=True) + EPS)
    xh = x * r
    dw = jnp.sum(dy * xh, axis=0, keepdims=True)
    g = dy * w
    dx = r * (g - xh * jnp.mean(g * xh, axis=-1, keepdims=True))
    return dx, dw


def _mm_in(x, wn, w, ntb, tm, name):
    t_len, d = x.shape
    nt = w.shape[1] // TILE

    def body(x_ref, wn_ref, w_ref, u_ref, h_ref):
        @pl.when(pl.program_id(1) == 0)
        def _():
            h_ref[...] = _rms_fwd(x_ref[...], wn_ref[...]).astype(h_ref.dtype)
        h = h_ref[...]
        for t in range(ntb):
            u_ref[t] = jnp.dot(h, w_ref[:, TILE * t:TILE * (t + 1)], preferred_element_type=f32)

    return _pc(
        body, name, grid=(t_len // tm, nt // ntb),
        in_specs=[pl.BlockSpec((tm, d), lambda i, j: (i, 0)),
                  pl.BlockSpec((1, d), lambda i, j: (0, 0)),
                  pl.BlockSpec((d, ntb * TILE), lambda i, j: (0, j))],
        out_specs=[pl.BlockSpec((ntb, tm, TILE), lambda i, j: (j, i, 0)),
                   pl.BlockSpec((tm, d), lambda i, j: (i, 0))],
        out_shape=[SDS((nt, t_len, TILE), f32), SDS((t_len, d), MXU_DTYPE)],
        compiler_params=_params(("arbitrary", "arbitrary")),
    )(x, wn, w)


def _mm_dout(y, dxn, wn, w, ntb, tm, name):
    t_len, d = y.shape
    nt = w.shape[0] // TILE

    def body(y_ref, dxn_ref, wn_ref, w_ref, da_ref, dy_ref, dwn_ref):
        i, j = pl.program_id(0), pl.program_id(1)

        @pl.when((i == 0) & (j == 0))
        def _():
            dwn_ref[...] = jnp.zeros_like(dwn_ref)

        @pl.when(j == 0)
        def _():
            dy, dw = _rms_bwd(y_ref[...], wn_ref[...], dxn_ref[...])
            dy_ref[...] = dy.astype(dy_ref.dtype)
            dwn_ref[...] += dw
        dy = dy_ref[...]
        for t in range(ntb):
            da_ref[t] = _dot_nt(dy, w_ref[TILE * t:TILE * (t + 1), :])

    return _pc(
        body, name, grid=(t_len // tm, nt // ntb),
        in_specs=[pl.BlockSpec((tm, d), lambda i, j: (i, 0)),
                  pl.BlockSpec((tm, d), lambda i, j: (i, 0)),
                  pl.BlockSpec((1, d), lambda i, j: (0, 0)),
                  pl.BlockSpec((ntb * TILE, d), lambda i, j: (j, 0))],
        out_specs=[pl.BlockSpec((ntb, tm, TILE), lambda i, j: (j, i, 0)),
                   pl.BlockSpec((tm, d), lambda i, j: (i, 0)),
                   pl.BlockSpec((1, d), lambda i, j: (0, 0))],
        out_shape=[SDS((nt, t_len, TILE), f32), SDS((t_len, d), MXU_DTYPE), SDS((1, d), f32)],
        compiler_params=_params(("arbitrary", "arbitrary")),
    )(y, dxn, wn, w)


def _mm_out(a, w, x, wn, ktb, tm, name):
    kt, t_len, _ = a.shape
    d = w.shape[1]
    nk = kt // ktb

    def body(a_ref, w_ref, x_ref, wn_ref, y_ref, xn_ref, acc):
        k = pl.program_id(1)

        @pl.when(k == 0)
        def _():
            acc[...] = jnp.zeros_like(acc)
        s = acc[...]
        for t in range(ktb):
            s = s + jnp.dot(a_ref[t], w_ref[TILE * t:TILE * (t + 1), :], preferred_element_type=f32)
        acc[...] = s

        @pl.when(k == nk - 1)
        def _():
            y = acc[...]
            y_ref[...] = y
            xn_ref[...] = x_ref[...] + _rms_fwd(y, wn_ref[...])

    return _pc(
        body, name, grid=(t_len // tm, nk),
        in_specs=[pl.BlockSpec((ktb, tm, TILE), lambda i, k: (k, i, 0)),
                  pl.BlockSpec((ktb * TILE, d), lambda i, k: (k, 0)),
                  pl.BlockSpec((tm, d), lambda i, k: (i, 0)),
                  pl.BlockSpec((1, d), lambda i, k: (0, 0))],
        out_specs=[pl.BlockSpec((tm, d), lambda i, k: (i, 0)),
                   pl.BlockSpec((tm, d), lambda i, k: (i, 0))],
        out_shape=[SDS((t_len, d), f32), SDS((t_len, d), f32)],
        scratch_shapes=[pltpu.VMEM((tm, d), f32)],
        compiler_params=_params(("arbitrary", "arbitrary")),
    )(a, w, x, wn)


def _mm_dh(du, w, x, dxn, wn, ktb, tm, name):
    kt, t_len, _ = du.shape
    d = w.shape[0]
    nk = kt // ktb

    def body(du_ref, w_ref, x_ref, dxn_ref, wn_ref, dx_ref, dwn_ref, acc):
        i, k = pl.program_id(0), pl.program_id(1)

        @pl.when((i == 0) & (k == 0))
        def _():
            dwn_ref[...] = jnp.zeros_like(dwn_ref)

        @pl.when(k == 0)
        def _():
            acc[...] = jnp.zeros_like(acc)
        s = acc[...]
        for t in range(ktb):
            s = s + _dot_nt(du_ref[t], w_ref[:, TILE * t:TILE * (t + 1)])
        acc[...] = s

        @pl.when(k == nk - 1)
        def _():
            dxp, dw = _rms_bwd(x_ref[...], wn_ref[...], acc[...])
            dx_ref[...] = dxn_ref[...] + dxp
            dwn_ref[...] += dw

    return _pc(
        body, name, grid=(t_len // tm, nk),
        in_specs=[pl.BlockSpec((ktb, tm, TILE), lambda i, k: (k, i, 0)),
                  pl.BlockSpec((d, ktb * TILE), lambda i, k: (0, k)),
                  pl.BlockSpec((tm, d), lambda i, k: (i, 0)),
                  pl.BlockSpec((tm, d), lambda i, k: (i, 0)),
                  pl.BlockSpec((1, d), lambda i, k: (0, 0))],
        out_specs=[pl.BlockSpec((tm, d), lambda i, k: (i, 0)),
                   pl.BlockSpec((1, d), lambda i, k: (0, 0))],
        out_shape=[SDS((t_len, d), f32), SDS((1, d), f32)],
        scratch_shapes=[pltpu.VMEM((tm, d), f32)],
        compiler_params=_params(("arbitrary", "arbitrary")),
    )(du, w, x, dxn, wn)


def _dw_cols(a, b, ntb, tk, name):
    t_len, kdim = a.shape
    nt = b.shape[0]

    def body(a_ref, b_ref, o_ref):
        @pl.when(pl.program_id(1) == 0)
        def _():
            o_ref[...] = jnp.zeros_like(o_ref)
        av = a_ref[...]
        for s in range(ntb):
            o_ref[:, TILE * s:TILE * (s + 1)] += _dot_tn(av, b_ref[s])

    return _pc(
        body, name, grid=(nt // ntb, t_len // tk),
        in_specs=[pl.BlockSpec((tk, kdim), lambda j, t: (t, 0)),
                  pl.BlockSpec((ntb, tk, TILE), lambda j, t: (j, t, 0))],
        out_specs=pl.BlockSpec((kdim, ntb * TILE), lambda j, t: (0, j)),
        out_shape=SDS((kdim, nt * TILE), f32),
        compiler_params=_params(("arbitrary", "arbitrary")),
    )(a, b)


def _dw_rows(a, b, ktb, tk, name):
    kt, t_len, _ = a.shape
    d = b.shape[1]

    def body(a_ref, b_ref, o_ref):
        @pl.when(pl.program_id(1) == 0)
        def _():
            o_ref[...] = jnp.zeros_like(o_ref)
        bv = b_ref[...]
        for s in range(ktb):
            o_ref[TILE * s:TILE * (s + 1), :] += _dot_tn(a_ref[s], bv)

    return _pc(
        body, name, grid=(kt // ktb, t_len // tk),
        in_specs=[pl.BlockSpec((ktb, tk, TILE), lambda k, t: (k, t, 0)),
                  pl.BlockSpec((tk, d), lambda k, t: (t, 0))],
        out_specs=pl.BlockSpec((ktb * TILE, d), lambda k, t: (k, 0)),
        out_shape=SDS((kt * TILE, d), f32),
        compiler_params=_params(("arbitrary", "arbitrary")),
    )(a, b)


def _loss_grad(x, tgt, tm):
    t_len, d = x.shape

    def body(x_ref, t_ref, dx_ref, l_ref):
        @pl.when(pl.program_id(0) == 0)
        def _():
            l_ref[...] = jnp.zeros_like(l_ref)
        e = x_ref[...] - t_ref[...]
        dx_ref[...] = e * (1.0 / d)
        row = jnp.mean(e * e, axis=-1, keepdims=True)
        l_ref[...] += 0.5 * jnp.sum(row, axis=0, keepdims=True)

    return _pc(
        body, "loss_grad", grid=(t_len // tm,),
        in_specs=[pl.BlockSpec((tm, d), lambda i: (i, 0)), pl.BlockSpec((tm, d), lambda i: (i, 0))],
        out_specs=[pl.BlockSpec((tm, d), lambda i: (i, 0)), pl.BlockSpec((1, 128), lambda i: (0, 0))],
        out_shape=[SDS((t_len, d), f32), SDS((1, 128), f32)],
        compiler_params=_params(("arbitrary",)),
    )(x, tgt)


def _tri(lower):
    r = lax.broadcasted_iota(jnp.int32, (CHUNK, CHUNK), 0)
    c = lax.broadcasted_iota(jnp.int32, (CHUNK, CHUNK), 1)
    return ((c <= r) if lower else (c >= r)).astype(f32)


def _dot_hi(a, b, dims):
    return lax.dot_general(a, b, (dims, ((), ())), precision=lax.Precision.HIGHEST, preferred_element_type=f32)


def _split(x, n):
    parts = []
    for _ in range(n):
        p = x.astype(jnp.bfloat16)
        parts.append(p)
        x = x - p.astype(f32)
    return parts


def _dot_exact(a, b, dims, split_a, n=3):
    out = None
    if split_a:
        b = b.astype(jnp.bfloat16)
        for p in _split(a, n):
            t = lax.dot_general(p, b, (dims, ((), ())), preferred_element_type=f32)
            out = t if out is None else out + t
    else:
        a = a.astype(jnp.bfloat16)
        for p in _split(b, n):
            t = lax.dot_general(a, p, (dims, ((), ())), preferred_element_type=f32)
            out = t if out is None else out + t
    return out


def _dt_path(dt_raw, dtb, alog):
    dtr = dt_raw + dtb
    dt = _softplus(dtr)
    a_neg = -jnp.exp(alog)
    a = dt * a_neg
    acs = _dot_exact(_tri(True), a, ((1,), (0,)), False)
    acs_t = _dot_exact(a, _tri(False), ((0,), (0,)), True)
    return dtr, dt, a_neg, acs, acs_t


def _conv_silu(x0, x1, x2, x3, w0, w1, w2, w3, b):
    acc = b + w0 * x0 + w1 * x1 + w2 * x2 + w3 * x3
    return _silu(acc)


def _ssd_group(xs, bm, cm, dtc, ac, ar, dh, hp):
    row = lax.broadcasted_iota(jnp.int32, (CHUNK, CHUNK), 0)
    col = lax.broadcasted_iota(jnp.int32, (CHUNK, CHUNK), 1)
    causal = col <= row
    last = (lax.broadcasted_iota(jnp.int32, (1, CHUNK), 1) == CHUNK - 1).astype(f32)
    cb = _dot_nt(cm, bm)
    ys, hns = [], []
    for r in range(4):
        xt = xs[r] * dtc[r]
        decay = jnp.exp(jnp.where(causal, ac[r] - ar[r], -jnp.inf))
        y_diag = _dot(cb * decay, xt)
        a_end = jnp.sum(ar[r] * last, axis=1, keepdims=True)
        y_off = _dot_nt(cm, hp[r]) * jnp.exp(ac[r])
        st = _dot_tn(xt * jnp.exp(a_end - ac[r]), bm)
        hns.append(hp[r] * jnp.exp(a_end) + st)
        ys.append(y_diag + y_off + dh[r] * xs[r])
    return tuple(ys), tuple(hns)


def _head_cols(g, r, dt, acs, acs_t, dsk):
    lane = lax.broadcasted_iota(jnp.int32, (1, SSM_HEADS), 1)
    sub = lax.broadcasted_iota(jnp.int32, (SSM_HEADS, 1), 0)
    h = 4 * g + r
    oh_l = (lane == h).astype(f32)
    oh_s = (sub == h).astype(f32)
    dtc = jnp.sum(dt * oh_l, axis=1, keepdims=True)
    ac = jnp.sum(acs * oh_l, axis=1, keepdims=True)
    ar = jnp.sum(acs_t * oh_s, axis=0, keepdims=True)
    dh = jnp.sum(dsk * oh_l, axis=1, keepdims=True)
    return h, oh_l, oh_s, dtc, ac, ar, dh


def _ssm_in_specs(nc, rev):
    def cidx(c):
        return (nc - 1 - c) if rev else c
    return [
        pl.BlockSpec((SSM_NT, CHUNK, TILE), lambda c: (0, cidx(c), 0)),
        pl.BlockSpec((SSM_NT, 8, TILE), lambda c: (0, jnp.maximum(cidx(c) * (CHUNK // 8) - 1, 0), 0)),
        pl.BlockSpec((16, 4, TILE), lambda c: (0, 0, 0)),
        pl.BlockSpec((16, 1, TILE), lambda c: (0, 0, 0)),
        pl.BlockSpec((1, SSM_HEADS), lambda c: (0, 0)),
        pl.BlockSpec((1, SSM_HEADS), lambda c: (0, 0)),
        pl.BlockSpec((1, SSM_HEADS), lambda c: (0, 0)),
        pl.BlockSpec((SSM_KT, 1, TILE), lambda c: (0, 0, 0)),
    ]


def _ssm_fwd(u, cw, cb, dtb, alog, dsk, gn, name):
    t_len = u.shape[1]
    nc = t_len // CHUNK

    def body(u_ref, halo_ref, cw_ref, cb_ref, dtb_ref, alog_ref, dsk_ref, gn_ref,
             a3_ref, yp_ref, hst_ref, h_s, win_s, xs_s, bc_s):
        c = pl.program_id(0)

        @pl.when(c == 0)
        def _():
            h_s[...] = jnp.zeros_like(h_s)
        _, dt, _, acs, acs_t = _dt_path(u_ref[SSM_NT - 1, :, 0:SSM_HEADS], dtb_ref[...], alog_ref[...])
        dsk = dsk_ref[...]

        def conv_tile(j, p):
            win_s[0:8, :] = jnp.where(c > 0, halo_ref[j], 0.0)
            win_s[8:8 + CHUNK, :] = u_ref[j]
            return _conv_silu(win_s[5:5 + CHUNK, :], win_s[6:6 + CHUNK, :], win_s[7:7 + CHUNK, :],
                              win_s[8:8 + CHUNK, :], cw_ref[p, 0:1, :], cw_ref[p, 1:2, :], cw_ref[p, 2:3, :],
                              cw_ref[p, 3:4, :], cb_ref[p])

        def group(g, s1):
            xs_s[...] = conv_tile(8 + g, g)
            bc_s[...] = conv_tile(16 + g, 8 + g)
            bm, cm = bc_s[:, 0:SSM_N], bc_s[:, SSM_N:2 * SSM_N]
            hs, xs, dtc, ac, ar, dh, hp = [], [], [], [], [], [], []
            for r in range(4):
                h, _, _, dtc_r, ac_r, ar_r, dh_r = _head_cols(g, r, dt, acs, acs_t, dsk)
                hs.append(h); dtc.append(dtc_r); ac.append(ac_r); ar.append(ar_r); dh.append(dh_r)
                xs.append(xs_s[:, SSM_P * r:SSM_P * (r + 1)])
                hp.append(h_s[h])
            ys, hns = _ssd_group(xs, bm, cm, dtc, ac, ar, dh, hp)
            for r in range(4):
                hst_ref[0, hs[r]] = hp[r]
                h_s[hs[r]] = hns[r]
                yp_ref[g, :, SSM_P * r:SSM_P * (r + 1)] = ys[r]
            y2 = yp_ref[g] * _silu(u_ref[g])
            return s1 + jnp.sum(y2 * y2, axis=1, keepdims=True)

        s1 = lax.fori_loop(0, SSM_GROUPS, group, jnp.zeros((CHUNK, 1), f32))
        rinv = lax.rsqrt(s1 * (1.0 / SSM_INNER) + EPS)

        def gate(g, carry):
            y2 = yp_ref[g] * _silu(u_ref[g])
            a3_ref[g] = (y2 * rinv * gn_ref[g]).astype(a3_ref.dtype)
            return carry

        lax.fori_loop(0, SSM_GROUPS, gate, 0)

    return _pc(
        body, name, grid=(nc,),
        in_specs=_ssm_in_specs(nc, False),
        out_specs=[pl.BlockSpec((SSM_KT, CHUNK, TILE), lambda c: (0, c, 0)),
                   pl.BlockSpec((SSM_KT, CHUNK, TILE), lambda c: (0, c, 0)),
                   pl.BlockSpec((1, SSM_HEADS, SSM_P, SSM_N), lambda c: (c, 0, 0, 0))],
        out_shape=[SDS((SSM_KT, t_len, TILE), MXU_DTYPE), SDS((SSM_KT, t_len, TILE), f32),
                   SDS((nc, SSM_HEADS, SSM_P, SSM_N), f32)],
        scratch_shapes=[pltpu.VMEM((SSM_HEADS, SSM_P, SSM_N), f32), pltpu.VMEM((8 + CHUNK, TILE), f32),
                        pltpu.VMEM((CHUNK, TILE), f32), pltpu.VMEM((CHUNK, TILE), f32)],
        compiler_params=_params(("arbitrary",)),
    )(u, u, cw, cb, dtb, alog, dsk, gn)


def _ssm_bwd(u, yp, hst, da3, cw, cb, dtb, alog, dsk, gn, name):
    t_len = u.shape[1]
    nc = t_len // CHUNK

    def body(u_ref, halo_ref, cw_ref, cb_ref, dtb_ref, alog_ref, dsk_ref, gn_ref, yp_ref, hst_ref, da3_ref,
             du_ref, dcw_ref, dcb_ref, ddtb_ref, dalog_ref, ddsk_ref, dgn_ref,
             dh_s, carry_s, win_s, dwin_s, xs_s, bc_s, dy_s, dxs_s, dbc_s):
        step = pl.program_id(0)
        cc = nc - 1 - step

        @pl.when(step == 0)
        def _():
            dh_s[...] = jnp.zeros_like(dh_s)
            carry_s[...] = jnp.zeros_like(carry_s)
            dcw_ref[...] = jnp.zeros_like(dcw_ref)
            dcb_ref[...] = jnp.zeros_like(dcb_ref)
            ddtb_ref[...] = jnp.zeros_like(ddtb_ref)
            dalog_ref[...] = jnp.zeros_like(dalog_ref)
            ddsk_ref[...] = jnp.zeros_like(ddsk_ref)
            dgn_ref[...] = jnp.zeros_like(dgn_ref)
        dtr, dt, a_neg, acs, acs_t = _dt_path(u_ref[SSM_NT - 1, :, 0:SSM_HEADS], dtb_ref[...], alog_ref[...])
        dsk = dsk_ref[...]

        def sums(g, carry):
            s1, s2 = carry
            y2 = yp_ref[g] * _silu(u_ref[g])
            g3 = da3_ref[g] * gn_ref[g]
            return (s1 + jnp.sum(y2 * y2, axis=1, keepdims=True), s2 + jnp.sum(g3 * y2, axis=1, keepdims=True))

        zcol = jnp.zeros((CHUNK, 1), f32)
        s1, s2 = lax.fori_loop(0, SSM_GROUPS, sums, (zcol, zcol))
        rinv = lax.rsqrt(s1 * (1.0 / SSM_INNER) + EPS)
        m2 = s2 * rinv * rinv * rinv * (1.0 / SSM_INNER)

        def windows(j):
            win_s[0:8, :] = jnp.where(cc > 0, halo_ref[j], 0.0)
            win_s[8:8 + CHUNK, :] = u_ref[j]
            return [win_s[5 + k:5 + k + CHUNK, :] for k in range(4)]

        def taps(p):
            return [cw_ref[p, k:k + 1, :] for k in range(4)]

        def conv_bwd(j, p, vjp_fn, dout):
            dx0, dx1, dx2, dx3, dw0, dw1, dw2, dw3, db = vjp_fn(dout)
            dwin_s[0:CHUNK, :] = jnp.zeros((CHUNK, TILE), f32)
            dwin_s[CHUNK:CHUNK + 8, :] = carry_s[j - 8]
            for k, dxk in enumerate((dx0, dx1, dx2, dx3)):
                dwin_s[5 + k:5 + k + CHUNK, :] += dxk
            du_ref[j] = dwin_s[8:8 + CHUNK, :]
            carry_s[j - 8] = dwin_s[0:8, :]
            for k, dwk in enumerate((dw0, dw1, dw2, dw3)):
                dcw_ref[p, k:k + 1, :] += dwk
            dcb_ref[p] += db

        def group(g, carry):
            ddt, dacs, dacs_t, ddsk = carry
            z = u_ref[g]
            sg = _sigmoid(z)
            sz = z * sg
            y = yp_ref[g]
            y2 = y * sz
            da3 = da3_ref[g]
            dgn_ref[g] += jnp.sum(da3 * y2 * rinv, axis=0, keepdims=True)
            dy2 = rinv * (da3 * gn_ref[g]) - y2 * m2
            dy_s[...] = dy2 * sz
            du_ref[g] = dy2 * y * (sg * (1.0 + z * (1.0 - sg)))
            xs_v, vjp_xs = jax.vjp(_conv_silu, *windows(8 + g), *taps(g), cb_ref[g])
            xs_s[...] = xs_v
            bc_v, vjp_bc = jax.vjp(_conv_silu, *windows(16 + g), *taps(8 + g), cb_ref[8 + g])
            bc_s[...] = bc_v
            bm, cm = bc_s[:, 0:SSM_N], bc_s[:, SSM_N:2 * SSM_N]
            hs, ohl, ohs, xs, dtc, ac, ar, dh, hp, dys, dhn = [], [], [], [], [], [], [], [], [], [], []
            for r in range(4):
                h, oh_l, oh_s, dtc_r, ac_r, ar_r, dh_r = _head_cols(g, r, dt, acs, acs_t, dsk)
                hs.append(h); ohl.append(oh_l); ohs.append(oh_s)
                dtc.append(dtc_r); ac.append(ac_r); ar.append(ar_r); dh.append(dh_r)
                xs.append(xs_s[:, SSM_P * r:SSM_P * (r + 1)])
                hp.append(hst_ref[0, h])
                dys.append(dy_s[:, SSM_P * r:SSM_P * (r + 1)])
                dhn.append(dh_s[h])
            _, vjp_g = jax.vjp(_ssd_group, xs, bm, cm, dtc, ac, ar, dh, hp)
            dxs, dbm, dcm, ddtc, dac, dar, ddh, dhp = vjp_g((tuple(dys), tuple(dhn)))
            for r in range(4):
                dxs_s[:, SSM_P * r:SSM_P * (r + 1)] = dxs[r]
                dh_s[hs[r]] = dhp[r]
                ddt = ddt + ddtc[r] * ohl[r]
                dacs = dacs + dac[r] * ohl[r]
                dacs_t = dacs_t + ohs[r] * dar[r]
                ddsk = ddsk + ddh[r] * ohl[r]
            dbc_s[:, 0:SSM_N] = dbm
            dbc_s[:, SSM_N:2 * SSM_N] = dcm
            conv_bwd(8 + g, g, vjp_xs, dxs_s[...])
            conv_bwd(16 + g, 8 + g, vjp_bc, dbc_s[...])
            return ddt, dacs, dacs_t, ddsk

        init = (jnp.zeros((CHUNK, SSM_HEADS), f32), jnp.zeros((CHUNK, SSM_HEADS), f32),
                jnp.zeros((SSM_HEADS, CHUNK), f32), jnp.zeros((1, SSM_HEADS), f32))
        ddt, dacs, dacs_t, ddsk = lax.fori_loop(0, SSM_GROUPS, group, init)
        upper = _tri(False)
        da = _dot_hi(upper, dacs, ((1,), (0,))) + _dot_hi(upper, dacs_t, ((1,), (1,)))
        ddt = ddt + da * a_neg
        dalog_ref[...] += jnp.sum(da * dt, axis=0, keepdims=True) * a_neg
        ddtr = ddt * _sigmoid(dtr)
        ddtb_ref[...] += jnp.sum(ddtr, axis=0, keepdims=True)
        ddsk_ref[...] += ddsk
        du_ref[SSM_NT - 1] = jnp.zeros((CHUNK, TILE), f32)
        du_ref[SSM_NT - 1, :, 0:SSM_HEADS] = ddtr

    def rc(c):
        return nc - 1 - c

    small = [pl.BlockSpec((16, 4, TILE), lambda c: (0, 0, 0)),
             pl.BlockSpec((16, 1, TILE), lambda c: (0, 0, 0)),
             pl.BlockSpec((1, SSM_HEADS), lambda c: (0, 0)),
             pl.BlockSpec((1, SSM_HEADS), lambda c: (0, 0)),
             pl.BlockSpec((1, SSM_HEADS), lambda c: (0, 0)),
             pl.BlockSpec((SSM_KT, 1, TILE), lambda c: (0, 0, 0))]
    return _pc(
        body, name, grid=(nc,),
        in_specs=_ssm_in_specs(nc, True) + [
            pl.BlockSpec((SSM_KT, CHUNK, TILE), lambda c: (0, rc(c), 0)),
            pl.BlockSpec((1, SSM_HEADS, SSM_P, SSM_N), lambda c: (rc(c), 0, 0, 0)),
            pl.BlockSpec((SSM_KT, CHUNK, TILE), lambda c: (0, rc(c), 0))],
        out_specs=[pl.BlockSpec((SSM_NT, CHUNK, TILE), lambda c: (0, rc(c), 0))] + small,
        out_shape=[SDS((SSM_NT, t_len, TILE), f32), SDS((16, 4, TILE), f32), SDS((16, 1, TILE), f32),
                   SDS((1, SSM_HEADS), f32), SDS((1, SSM_HEADS), f32), SDS((1, SSM_HEADS), f32),
                   SDS((SSM_KT, 1, TILE), f32)],
        scratch_shapes=[pltpu.VMEM((SSM_HEADS, SSM_P, SSM_N), f32), pltpu.VMEM((16, 8, TILE), f32),
                        pltpu.VMEM((8 + CHUNK, TILE), f32), pltpu.VMEM((8 + CHUNK, TILE), f32)]
                       + [pltpu.VMEM((CHUNK, TILE), f32)] * 5,
        compiler_params=_params(("arbitrary",)),
    )(u, u, cw, cb, dtb, alog, dsk, gn, yp, hst, da3)


CONV_ROWS = 1024
CONV_SUB = 32


def _conv_specs(nb, rows, rev):
    def ridx(i):
        return (nb - 1 - i) if rev else i
    return [
        pl.BlockSpec((1, rows, TILE), lambda p, i: (8 + p, ridx(i), 0)),
        pl.BlockSpec((1, 8, TILE), lambda p, i: (8 + p, jnp.maximum(ridx(i) * (rows // 8) - 1, 0), 0)),
        pl.BlockSpec((1, 4, TILE), lambda p, i: (p, 0, 0)),
        pl.BlockSpec((1, 1, TILE), lambda p, i: (p, 0, 0)),
    ]


def _conv_fwd(u, cw, cb, name):
    t_len = u.shape[1]
    rows = min(CONV_ROWS, t_len)
    nb = t_len // rows

    def body(u_ref, halo_ref, cw_ref, cb_ref, o_ref, win_s):
        i = pl.program_id(1)
        win_s[0:8, :] = jnp.where(i > 0, halo_ref[0], 0.0)
        win_s[8:8 + rows, :] = u_ref[0]
        w = [cw_ref[0, k:k + 1, :] for k in range(4)]
        b = cb_ref[0]
        for s in range(rows // CONV_SUB):
            o = CONV_SUB * s
            acc = b
            for k in range(4):
                acc = acc + w[k] * win_s[5 + k + o:5 + k + o + CONV_SUB, :]
            o_ref[0, o:o + CONV_SUB, :] = _silu(acc)

    return _pc(
        body, name, grid=(16, nb),
        in_specs=_conv_specs(nb, rows, False),
        out_specs=pl.BlockSpec((1, rows, TILE), lambda p, i: (p, i, 0)),
        out_shape=SDS((16, t_len, TILE), f32),
        scratch_shapes=[pltpu.VMEM((8 + rows, TILE), f32)],
        compiler_params=_params(("arbitrary", "arbitrary")),
    )(u, u, cw, cb)


def _conv_bwd(u, du, cw, cb, name):
    t_len = u.shape[1]
    rows = min(CONV_ROWS, t_len)
    nb = t_len // rows

    def body(u_ref, halo_ref, cw_ref, cb_ref, d_ref, o_ref, dcw_ref, dcb_ref, carry_s, win_s, dp_s):
        i = pl.program_id(1)
        ri = nb - 1 - i

        @pl.when(i == 0)
        def _():
            carry_s[...] = jnp.zeros_like(carry_s)
            dcw_ref[...] = jnp.zeros_like(dcw_ref)
            dcb_ref[...] = jnp.zeros_like(dcb_ref)
        win_s[0:8, :] = jnp.where(ri > 0, halo_ref[0], 0.0)
        win_s[8:8 + rows, :] = u_ref[0]
        w = [cw_ref[0, k:k + 1, :] for k in range(4)]
        b = cb_ref[0]
        dw = [jnp.zeros((1, TILE), f32)] * 4
        db = jnp.zeros((1, TILE), f32)
        for s in range(rows // CONV_SUB):
            o = CONV_SUB * s
            xk = [win_s[5 + k + o:5 + k + o + CONV_SUB, :] for k in range(4)]
            pre = b
            for k in range(4):
                pre = pre + w[k] * xk[k]
            sg = _sigmoid(pre)
            dpre = d_ref[0, o:o + CONV_SUB, :] * (sg * (1.0 + pre * (1.0 - sg)))
            dp_s[o:o + CONV_SUB, :] = dpre
            dw = [dw[k] + jnp.sum(dpre * xk[k], axis=0, keepdims=True) for k in range(4)]
            db = db + jnp.sum(dpre, axis=0, keepdims=True)
        dp_s[rows:rows + 8, :] = carry_s[...]
        for s in range(rows // CONV_SUB):
            o = CONV_SUB * s
            acc = w[0] * dp_s[3 + o:3 + o + CONV_SUB, :]
            for k in range(1, 4):
                acc = acc + w[k] * dp_s[3 - k + o:3 - k + o + CONV_SUB, :]
            o_ref[0, o:o + CONV_SUB, :] = acc
        carry_s[...] = dp_s[0:8, :]
        for k in range(4):
            dcw_ref[0, k:k + 1, :] += dw[k]
        dcb_ref[0] += db

    return _pc(
        body, name, grid=(16, nb),
        in_specs=_conv_specs(nb, rows, True) + [pl.BlockSpec((1, rows, TILE), lambda p, i: (8 + p, nb - 1 - i, 0))],
        out_specs=[pl.BlockSpec((1, rows, TILE), lambda p, i: (8 + p, nb - 1 - i, 0)),
                   pl.BlockSpec((1, 4, TILE), lambda p, i: (p, 0, 0)),
                   pl.BlockSpec((1, 1, TILE), lambda p, i: (p, 0, 0))],
        out_shape=[SDS(du.shape, f32), SDS((16, 4, TILE), f32), SDS((16, 1, TILE), f32)],
        input_output_aliases={4: 0},
        scratch_shapes=[pltpu.VMEM((8, TILE), f32), pltpu.VMEM((8 + rows, TILE), f32),
                        pltpu.VMEM((rows + 8, TILE), f32)],
        compiler_params=_params(("arbitrary", "arbitrary")),
    )(u, u, cw, cb, du)


def _collapse_matrix(g):
    r = lax.broadcasted_iota(jnp.int32, (SSM_HEADS, TILE), 0)
    c = lax.broadcasted_iota(jnp.int32, (SSM_HEADS, TILE), 1)
    return ((c // SSM_P) + 4 * g == r).astype(jnp.bfloat16)


def _ssd_prelude(dt_raw, dtb, alog, dsk, colx_s, scx_s):
    dtr, dt, a_neg, acs, acs_t = _dt_path(dt_raw, dtb, alog)
    a_end = acs[CHUNK - 1:CHUNK, :]
    lane = lax.broadcasted_iota(jnp.int32, (1, 2 * SSM_P), 1)
    lane4 = lax.broadcasted_iota(jnp.int32, (1, TILE), 1)
    sub8 = lax.broadcasted_iota(jnp.int32, (8, 1), 0)

    def row4(v, g):
        e = [v[:, 4 * g + r:4 * g + r + 1] for r in range(4)]
        return jnp.where(lane4 < 64, e[0], jnp.where(lane4 < 128, e[1], jnp.where(lane4 < 192, e[2], e[3])))

    for g in range(SSM_GROUPS):
        for k, arr in enumerate((dt, acs)):
            for half in range(2):
                h0 = 4 * g + 2 * half
                colx_s[k, g, :, 128 * half:128 * (half + 1)] = jnp.where(
                    lane < SSM_P, arr[:, h0:h0 + 1], arr[:, h0 + 1:h0 + 2])
        scx_s[g] = jnp.where(sub8 == 0, row4(dsk, g), jnp.where(sub8 == 1, row4(a_end, g), 0.0))
    return dtr, dt, a_neg, acs_t


def _ssm_core_specs(nc, rev):
    def cidx(c):
        return (nc - 1 - c) if rev else c
    return [
        pl.BlockSpec((SSM_KT, CHUNK, TILE), lambda c: (0, cidx(c), 0)),
        pl.BlockSpec((1, CHUNK, TILE), lambda c: (SSM_NT - 1, cidx(c), 0)),
        pl.BlockSpec((16, CHUNK, TILE), lambda c: (0, cidx(c), 0)),
        pl.BlockSpec((1, SSM_HEADS), lambda c: (0, 0)),
        pl.BlockSpec((1, SSM_HEADS), lambda c: (0, 0)),
        pl.BlockSpec((1, SSM_HEADS), lambda c: (0, 0)),
        pl.BlockSpec((SSM_KT, 1, TILE), lambda c: (0, 0, 0)),
    ]


def _stack_cols_rows(acx, rows):
    ac = jnp.concatenate([acx[:, SSM_P * r:SSM_P * r + 1] for r in range(4)], axis=0)
    ar = jnp.concatenate([jnp.broadcast_to(rows[r:r + 1, :], (CHUNK, CHUNK)) for r in range(4)], axis=0)
    return ac, ar


def _ssm2_fwd(u, xbc, dtb, alog, dsk, gn, name):
    t_len = u.shape[1]
    nc = t_len // CHUNK

    def body(z_ref, dt_ref, x_ref, dtb_ref, alog_ref, dsk_ref, gn_ref, a3_ref, yp_ref, hst_ref,
             h_s, colx_s, scx_s, xt_s, yd_s):
        c = pl.program_id(0)

        @pl.when(c == 0)
        def _():
            h_s[...] = jnp.zeros_like(h_s)
        _, _, _, acs_t = _ssd_prelude(dt_ref[0, :, 0:SSM_HEADS], dtb_ref[...], alog_ref[...], dsk_ref[...],
                                      colx_s, scx_s)
        causal = _tri4()

        def group(g, s1):
            xs = x_ref[g]
            bm, cm = x_ref[8 + g, :, 0:SSM_N], x_ref[8 + g, :, SSM_N:2 * SSM_N]
            cb = _dot_nt(cm, bm)
            sc = scx_s[g]
            a_end = sc[1:2, :]
            rows = pltpu.roll(acs_t, (SSM_HEADS - 4 * g) % SSM_HEADS, 0)
            hp = h_s[g]
            xt = xs * colx_s[0, g]
            xt_s[...] = xt
            acx = colx_s[1, g]
            ac_st, ar_st = _stack_cols_rows(acx, rows)
            m = jnp.concatenate([cb] * 4, axis=0) * jnp.exp(jnp.where(causal, ac_st - ar_st, -jnp.inf))
            for r in range(4):
                hd = slice(SSM_P * r, SSM_P * (r + 1))
                yd_s[:, hd] = _dot(m[CHUNK * r:CHUNK * (r + 1), :], xt_s[:, hd])
            yp_ref[g] = yd_s[...] + _dot(cm, hp) * jnp.exp(acx) + sc[0:1, :] * xs
            hst_ref[0, g] = hp
            h_s[g] = hp * jnp.exp(a_end) + _dot_tn(bm, xt * jnp.exp(a_end - acx))
            y2 = yp_ref[g] * _silu(z_ref[g])
            return s1 + jnp.sum(y2 * y2, axis=1, keepdims=True)

        s1 = lax.fori_loop(0, SSM_GROUPS, group, jnp.zeros((CHUNK, 1), f32))
        rinv = lax.rsqrt(s1 * (1.0 / SSM_INNER) + EPS)

        def gate(g, carry):
            y2 = yp_ref[g] * _silu(z_ref[g])
            a3_ref[g] = (y2 * rinv * gn_ref[g]).astype(a3_ref.dtype)
            return carry

        lax.fori_loop(0, SSM_GROUPS, gate, 0)

    return _pc(
        body, name, grid=(nc,),
        in_specs=_ssm_core_specs(nc, False),
        out_specs=[pl.BlockSpec((SSM_KT, CHUNK, TILE), lambda c: (0, c, 0)),
                   pl.BlockSpec((SSM_KT, CHUNK, TILE), lambda c: (0, c, 0)),
                   pl.BlockSpec((1, SSM_GROUPS, SSM_N, TILE), lambda c: (c, 0, 0, 0))],
        out_shape=[SDS((SSM_KT, t_len, TILE), MXU_DTYPE), SDS((SSM_KT, t_len, TILE), f32),
                   SDS((nc, SSM_GROUPS, SSM_N, TILE), f32)],
        scratch_shapes=[pltpu.VMEM((SSM_GROUPS, SSM_N, TILE), f32), pltpu.VMEM((2, SSM_GROUPS, CHUNK, TILE), f32),
                        pltpu.VMEM((SSM_GROUPS, 8, TILE), f32), pltpu.VMEM((CHUNK, TILE), f32),
                        pltpu.VMEM((CHUNK, TILE), f32)],
        compiler_params=_params(("arbitrary",)),
    )(u, u, xbc, dtb, alog, dsk, gn)


def _ssm2_bwd(u, xbc, yp, hst, da3, dtb, alog, dsk, gn, name):
    t_len = u.shape[1]
    nc = t_len // CHUNK

    def body(z_ref, dt_ref, x_ref, dtb_ref, alog_ref, dsk_ref, gn_ref, yp_ref, hst_ref, da3_ref,
             du_ref, ddtb_ref, dalog_ref, ddsk_ref, dgn_ref,
             dh_s, colx_s, scx_s, xt_s, dy_s, dxt_s, ddtx_s, dacx_s, ddx_s, drow_s, dm_s, dmt_s):
        step = pl.program_id(0)

        @pl.when(step == 0)
        def _():
            dh_s[...] = jnp.zeros_like(dh_s)
            ddtb_ref[...] = jnp.zeros_like(ddtb_ref)
            dalog_ref[...] = jnp.zeros_like(dalog_ref)
            ddsk_ref[...] = jnp.zeros_like(ddsk_ref)
            dgn_ref[...] = jnp.zeros_like(dgn_ref)
        dtr, dt, a_neg, acs_t = _ssd_prelude(dt_ref[0, :, 0:SSM_HEADS], dtb_ref[...], alog_ref[...], dsk_ref[...],
                                             colx_s, scx_s)
        causal = _tri4()
        causal_t = (lax.broadcasted_iota(jnp.int32, (ATT_ROWS, CHUNK), 1)
                    >= lax.broadcasted_iota(jnp.int32, (ATT_ROWS, CHUNK), 0) % CHUNK)
        last = (lax.broadcasted_iota(jnp.int32, (1, CHUNK), 1) == CHUNK - 1).astype(f32)
        lane = lax.broadcasted_iota(jnp.int32, (1, TILE), 1)
        sub32 = lax.broadcasted_iota(jnp.int32, (SSM_HEADS, 1), 0)
        drow_s[...] = jnp.zeros_like(drow_s)

        def sums(g, carry):
            s1, s2 = carry
            y2 = yp_ref[g] * _silu(z_ref[g])
            g3 = da3_ref[g] * gn_ref[g]
            return (s1 + jnp.sum(y2 * y2, axis=1, keepdims=True), s2 + jnp.sum(g3 * y2, axis=1, keepdims=True))

        zcol = jnp.zeros((CHUNK, 1), f32)
        s1, s2 = lax.fori_loop(0, SSM_GROUPS, sums, (zcol, zcol))
        rinv = lax.rsqrt(s1 * (1.0 / SSM_INNER) + EPS)
        m2 = s2 * rinv * rinv * rinv * (1.0 / SSM_INNER)

        def group(g, carry):
            z = z_ref[g]
            sg = _sigmoid(z)
            sz = z * sg
            y = yp_ref[g]
            y2 = y * sz
            da3 = da3_ref[g]
            dgn_ref[g] += jnp.sum(da3 * y2 * rinv, axis=0, keepdims=True)
            dy2 = rinv * (da3 * gn_ref[g]) - y2 * m2
            dy = dy2 * sz
            dy_s[...] = dy
            du_ref[g] = dy2 * y * (sg * (1.0 + z * (1.0 - sg)))
            xs = x_ref[g]
            bm, cm = x_ref[8 + g, :, 0:SSM_N], x_ref[8 + g, :, SSM_N:2 * SSM_N]
            cb = _dot_nt(cm, bm)
            cbt = _dot_nt(bm, cm)
            dtx, acx = colx_s[0, g], colx_s[1, g]
            sc = scx_s[g]
            a_end = sc[1:2, :]
            ex = jnp.exp(acx)
            wdx = jnp.exp(a_end - acx)
            eend = jnp.exp(a_end)
            rows = pltpu.roll(acs_t, (SSM_HEADS - 4 * g) % SSM_HEADS, 0)
            hp = hst_ref[0, g]
            dhn = dh_s[g]
            xt = xs * dtx
            xt_s[...] = xt
            ch = _dot(cm, hp)
            gy = dy * ex
            dcm = _dot_nt(gy, hp)
            dh_s[g] = _dot_tn(cm, gy) + dhn * eend
            q = _dot(bm, dhn)
            dbm = _dot_nt(xt * wdx, dhn)
            qx = q * xt * wdx
            v_end = jnp.sum(dhn * hp, axis=0, keepdims=True) * eend + jnp.sum(qx, axis=0, keepdims=True)
            ac_st, ar_st = _stack_cols_rows(acx, rows)
            lam = jnp.exp(jnp.where(causal, ac_st - ar_st, -jnp.inf))
            lam_t = jnp.exp(jnp.where(causal_t, ar_st - ac_st, -jnp.inf))
            m = jnp.concatenate([cb] * 4, axis=0) * lam
            m_t = jnp.concatenate([cbt] * 4, axis=0) * lam_t
            for r in range(4):
                hd = slice(SSM_P * r, SSM_P * (r + 1))
                rs = slice(CHUNK * r, CHUNK * (r + 1))
                dm_s[rs, :] = _dot_nt(dy_s[:, hd], xt_s[:, hd])
                dmt_s[rs, :] = _dot_nt(xt_s[:, hd], dy_s[:, hd])
                dxt_s[:, hd] = _dot(m_t[rs, :], dy_s[:, hd])
            dm = dm_s[...]
            dl = dm * lam
            dseg = dm * m
            dseg_t = dmt_s[...] * m_t
            dcb = dl[0:CHUNK] + dl[CHUNK:2 * CHUNK] + dl[2 * CHUNK:3 * CHUNK] + dl[3 * CHUNK:4 * CHUNK]
            drows = jnp.zeros((SSM_HEADS, CHUNK), f32)
            for r in range(4):
                rs = slice(CHUNK * r, CHUNK * (r + 1))
                in_head = (lane >= SSM_P * r) & (lane < SSM_P * (r + 1))
                d_ac = jnp.sum(dseg_t[rs, :], axis=0, keepdims=True)
                d_ar = jnp.sum(dseg[rs, :], axis=0, keepdims=True)
                d_aend = jnp.sum(jnp.where(in_head, v_end, 0.0), axis=1, keepdims=True)
                drows = drows + jnp.where(sub32 == r, d_ac - d_ar + last * d_aend, 0.0)
            dxt = dxt_s[...] + q * wdx
            du_ref[8 + g] = sc[0:1, :] * dy + dxt * dtx
            ddtx_s[g] = dxt * xs
            dacx_s[g] = dy * ch * ex - qx
            ddx_s[g] = jnp.broadcast_to(jnp.sum(dy * xs, axis=0, keepdims=True), (8, TILE))
            du_ref[16 + g, :, 0:SSM_N] = dbm + _dot_tn(dcb, cm)
            du_ref[16 + g, :, SSM_N:2 * SSM_N] = dcm + _dot(dcb, bm)
            drow_s[...] += pltpu.roll(drows, (4 * g) % SSM_HEADS, 0)
            return carry

        lax.fori_loop(0, SSM_GROUPS, group, 0)
        ddt = jnp.zeros((CHUNK, SSM_HEADS), f32)
        dacs = jnp.zeros((CHUNK, SSM_HEADS), f32)
        ddsk = jnp.zeros((8, SSM_HEADS), f32)
        for g in range(SSM_GROUPS):
            col_g = _collapse_matrix(g)
            ddt = ddt + _dot_exact(ddtx_s[g], col_g, ((1,), (1,)), True, 2)
            dacs = dacs + _dot_exact(dacx_s[g], col_g, ((1,), (1,)), True, 2)
            ddsk = ddsk + _dot_exact(ddx_s[g], col_g, ((1,), (1,)), True, 2)
        upper = _tri(False)
        da = _dot_exact(upper, dacs, ((1,), (0,)), False) + _dot_exact(upper, drow_s[...], ((1,), (1,)), False)
        ddt = ddt + da * a_neg
        dalog_ref[...] += jnp.sum(da * dt, axis=0, keepdims=True) * a_neg
        ddtr = ddt * _sigmoid(dtr)
        ddtb_ref[...] += jnp.sum(ddtr, axis=0, keepdims=True)
        ddsk_ref[...] += ddsk[0:1, :]
        du_ref[SSM_NT - 1] = jnp.zeros((CHUNK, TILE), f32)
        du_ref[SSM_NT - 1, :, 0:SSM_HEADS] = ddtr

    def rc(c):
        return nc - 1 - c

    vec = pl.BlockSpec((1, SSM_HEADS), lambda c: (0, 0))
    return _pc(
        body, name, grid=(nc,),
        in_specs=_ssm_core_specs(nc, True) + [
            pl.BlockSpec((SSM_KT, CHUNK, TILE), lambda c: (0, rc(c), 0)),
            pl.BlockSpec((1, SSM_GROUPS, SSM_N, TILE), lambda c: (rc(c), 0, 0, 0)),
            pl.BlockSpec((SSM_KT, CHUNK, TILE), lambda c: (0, rc(c), 0))],
        out_specs=[pl.BlockSpec((SSM_NT, CHUNK, TILE), lambda c: (0, rc(c), 0)), vec, vec, vec,
                   pl.BlockSpec((SSM_KT, 1, TILE), lambda c: (0, 0, 0))],
        out_shape=[SDS((SSM_NT, t_len, TILE), f32), SDS((1, SSM_HEADS), f32), SDS((1, SSM_HEADS), f32),
                   SDS((1, SSM_HEADS), f32), SDS((SSM_KT, 1, TILE), f32)],
        scratch_shapes=[pltpu.VMEM((SSM_GROUPS, SSM_N, TILE), f32), pltpu.VMEM((2, SSM_GROUPS, CHUNK, TILE), f32),
                        pltpu.VMEM((SSM_GROUPS, 8, TILE), f32), pltpu.VMEM((CHUNK, TILE), f32),
                        pltpu.VMEM((CHUNK, TILE), f32), pltpu.VMEM((CHUNK, TILE), f32),
                        pltpu.VMEM((SSM_GROUPS, CHUNK, TILE), f32), pltpu.VMEM((SSM_GROUPS, CHUNK, TILE), f32),
                        pltpu.VMEM((SSM_GROUPS, 8, TILE), f32), pltpu.VMEM((SSM_HEADS, CHUNK), f32),
                        pltpu.VMEM((4 * CHUNK, CHUNK), f32), pltpu.VMEM((4 * CHUNK, CHUNK), f32)],
        compiler_params=_params(("arbitrary",)),
    )(u, u, xbc, dtb, alog, dsk, gn, yp, hst, da3)


def _swap16(t):
    lane = lax.broadcasted_iota(jnp.int32, t.shape, 1) % 64
    return jnp.where(lane < 8, pltpu.roll(t, TILE - 8, 1), jnp.where(lane < 16, pltpu.roll(t, 8, 1), 0.0))


def _rope(t, cos_t, sin_t):
    return t * cos_t + _swap16(t) * sin_t


def _rope_bwd(g, cos_t, sin_t):
    return g * cos_t + _swap16(g * sin_t)


def _att_head(qh, kp, kc, vp, vc, sink, mask_p, mask_c):
    sp = jnp.where(mask_p, _dot_nt(qh, kp) * 0.125, -jnp.inf)
    sc = jnp.where(mask_c, _dot_nt(qh, kc) * 0.125, -jnp.inf)
    m = jnp.maximum(jnp.maximum(jnp.max(sp, axis=-1, keepdims=True), jnp.max(sc, axis=-1, keepdims=True)), sink)
    m = lax.stop_gradient(m)
    pp = jnp.exp(sp - m)
    pc = jnp.exp(sc - m)
    den = jnp.sum(pp, axis=-1, keepdims=True) + jnp.sum(pc, axis=-1, keepdims=True) + jnp.exp(sink - m)
    inv = 1.0 / den
    return _dot(pp * inv, vp) + _dot(pc * inv, vc)


def _att_masks(has_prev):
    row = lax.broadcasted_iota(jnp.int32, (CHUNK, CHUNK), 0)
    col = lax.broadcasted_iota(jnp.int32, (CHUNK, CHUNK), 1)
    return (col > row) & has_prev, col <= row


def _att_in_specs(nb, rev):
    def bidx(n):
        return (nb - 1 - n) if rev else n
    return [
        pl.BlockSpec((ATT_NT, CHUNK, TILE), lambda n: (0, bidx(n), 0)),
        pl.BlockSpec((2, CHUNK, TILE), lambda n: (2, jnp.maximum(bidx(n) - 1, 0), 0)),
        pl.BlockSpec((CHUNK, TILE), lambda n: (bidx(n), 0)),
        pl.BlockSpec((CHUNK, TILE), lambda n: (bidx(n), 0)),
        pl.BlockSpec((CHUNK, TILE), lambda n: (jnp.maximum(bidx(n) - 1, 0), 0)),
        pl.BlockSpec((CHUNK, TILE), lambda n: (jnp.maximum(bidx(n) - 1, 0), 0)),
        pl.BlockSpec((1, 16), lambda n: (0, 0)),
    ]


def _att_fwd(u, cos_t, sin_t, sinks, name):
    t_len = u.shape[1]
    nb = t_len // CHUNK

    def body(u_ref, prev_ref, cc_ref, sc_ref, cp_ref, sp_ref, sink_ref, a_ref,
             q_s, kp_s, kc_s, vp_s, vc_s, o_s):
        n = pl.program_id(0)
        mask_p, mask_c = _att_masks(n > 0)
        cos_c, sin_c = cc_ref[...], sc_ref[...]
        kc_s[...] = _rope(u_ref[4], cos_c, sin_c)
        kp_s[...] = _rope(prev_ref[0], cp_ref[...], sp_ref[...])
        vc_s[...] = u_ref[5]
        vp_s[...] = prev_ref[1]
        sinks = sink_ref[...]
        for g in range(4):
            q_s[...] = _rope(u_ref[g], cos_c, sin_c)
            kv = slice(64 * g, 64 * (g + 1))
            for r in range(4):
                hd = slice(64 * r, 64 * (r + 1))
                h = 4 * g + r
                o_s[:, hd] = _att_head(q_s[:, hd], kp_s[:, kv], kc_s[:, kv], vp_s[:, kv], vc_s[:, kv],
                                       sinks[:, h:h + 1], mask_p, mask_c)
            a_ref[g] = (o_s[...] * _silu(u_ref[6 + g])).astype(a_ref.dtype)

    return _pc(
        body, name, grid=(nb,),
        in_specs=_att_in_specs(nb, False),
        out_specs=pl.BlockSpec((ATT_KT, CHUNK, TILE), lambda n: (0, n, 0)),
        out_shape=SDS((ATT_KT, t_len, TILE), MXU_DTYPE),
        scratch_shapes=[pltpu.VMEM((CHUNK, TILE), f32)] * 6,
        compiler_params=_params(("arbitrary",)),
    )(u, u, cos_t, sin_t, cos_t, sin_t, sinks)


def _att_bwd(u, cos_t, sin_t, sinks, da, name):
    t_len = u.shape[1]
    nb = t_len // CHUNK

    def body(u_ref, prev_ref, cc_ref, sc_ref, cp_ref, sp_ref, sink_ref, da_ref, du_ref, dsink_ref,
             ck_s, cv_s, q_s, kp_s, kc_s, vp_s, vc_s, o_s, do_s, dq_s, dkp_s, dkc_s, dvp_s, dvc_s):
        step = pl.program_id(0)
        nn = nb - 1 - step

        @pl.when(step == 0)
        def _():
            ck_s[...] = jnp.zeros_like(ck_s)
            cv_s[...] = jnp.zeros_like(cv_s)
            dsink_ref[...] = jnp.zeros_like(dsink_ref)
        mask_p, mask_c = _att_masks(nn > 0)
        cos_c, sin_c = cc_ref[...], sc_ref[...]
        cos_p, sin_p = cp_ref[...], sp_ref[...]
        kc_s[...] = _rope(u_ref[4], cos_c, sin_c)
        kp_s[...] = _rope(prev_ref[0], cos_p, sin_p)
        vc_s[...] = u_ref[5]
        vp_s[...] = prev_ref[1]
        sinks = sink_ref[...]
        lane16 = lax.broadcasted_iota(jnp.int32, (1, 16), 1)
        dsink = jnp.zeros((1, 16), f32)
        att = functools.partial(_att_head, mask_p=mask_p, mask_c=mask_c)
        for g in range(4):
            q_s[...] = _rope(u_ref[g], cos_c, sin_c)
            gate = u_ref[6 + g]
            sg = _sigmoid(gate)
            dav = da_ref[g]
            do_s[...] = dav * (gate * sg)
            kv = slice(64 * g, 64 * (g + 1))
            dkp = jnp.zeros((CHUNK, 64), f32)
            dkc = jnp.zeros((CHUNK, 64), f32)
            dvp = jnp.zeros((CHUNK, 64), f32)
            dvc = jnp.zeros((CHUNK, 64), f32)
            for r in range(4):
                hd = slice(64 * r, 64 * (r + 1))
                h = 4 * g + r
                o_h, vjp_fn = jax.vjp(att, q_s[:, hd], kp_s[:, kv], kc_s[:, kv], vp_s[:, kv], vc_s[:, kv],
                                      sinks[:, h:h + 1])
                dq_h, dkp_h, dkc_h, dvp_h, dvc_h, ds_h = vjp_fn(do_s[:, hd])
                o_s[:, hd] = o_h
                dq_s[:, hd] = dq_h
                dkp, dkc, dvp, dvc = dkp + dkp_h, dkc + dkc_h, dvp + dvp_h, dvc + dvc_h
                dsink = dsink + ds_h * (lane16 == h).astype(f32)
            du_ref[6 + g] = dav * o_s[...] * (sg * (1.0 + gate * (1.0 - sg)))
            du_ref[g] = _rope_bwd(dq_s[...], cos_c, sin_c)
            dkp_s[:, kv] = dkp
            dkc_s[:, kv] = dkc
            dvp_s[:, kv] = dvp
            dvc_s[:, kv] = dvc
        du_ref[4] = _rope_bwd(dkc_s[...], cos_c, sin_c) + ck_s[...]
        du_ref[5] = dvc_s[...] + cv_s[...]
        ck_s[...] = _rope_bwd(dkp_s[...], cos_p, sin_p)
        cv_s[...] = dvp_s[...]
        dsink_ref[...] += dsink

    def rb(n):
        return nb - 1 - n

    return _pc(
        body, name, grid=(nb,),
        in_specs=_att_in_specs(nb, True) + [pl.BlockSpec((ATT_KT, CHUNK, TILE), lambda n: (0, rb(n), 0))],
        out_specs=[pl.BlockSpec((ATT_NT, CHUNK, TILE), lambda n: (0, rb(n), 0)),
                   pl.BlockSpec((1, 16), lambda n: (0, 0))],
        out_shape=[SDS((ATT_NT, t_len, TILE), f32), SDS((1, 16), f32)],
        scratch_shapes=[pltpu.VMEM((CHUNK, TILE), f32)] * 14,
        compiler_params=_params(("arbitrary",)),
    )(u, u, cos_t, sin_t, cos_t, sin_t, sinks, da)


ATT_SCALE = 0.125
ATT_ROWS = 4 * CHUNK


def _band_mask(has_prev):
    row = lax.broadcasted_iota(jnp.int32, (CHUNK, 2 * CHUNK), 0)
    col = lax.broadcasted_iota(jnp.int32, (CHUNK, 2 * CHUNK), 1)
    return ((col < CHUNK) & (col > row) & has_prev) | ((col >= CHUNK) & (col - CHUNK <= row))


def _tri4():
    row = lax.broadcasted_iota(jnp.int32, (ATT_ROWS, CHUNK), 0) % CHUNK
    col = lax.broadcasted_iota(jnp.int32, (ATT_ROWS, CHUNK), 1)
    return col <= row


def _stack_heads(ref):
    return jnp.concatenate([ref[:, 64 * r:64 * (r + 1)] for r in range(4)], axis=0)


def _sink_col(sinks, g):
    return [sinks[:, 4 * g + r:4 * g + r + 1] for r in range(4)]


def _softmax_rows(s_s, pn_s, pc_s, sink, tri, has_prev):
    sub = lax.broadcasted_iota(jnp.int32, (ATT_ROWS, 1), 0)
    sk = jnp.where(sub < CHUNK, sink[0], jnp.where(sub < 2 * CHUNK, sink[1], jnp.where(sub < 3 * CHUNK, sink[2], sink[3])))
    s = jnp.where(tri, s_s[:, CHUNK:2 * CHUNK], jnp.where(has_prev, s_s[:, 0:CHUNK], -jnp.inf)) * ATT_SCALE
    m = jnp.maximum(jnp.max(s, axis=-1, keepdims=True), sk)
    p = jnp.exp(s - m)
    e_sink = jnp.exp(sk - m)
    inv = 1.0 / (jnp.sum(p, axis=-1, keepdims=True) + e_sink)
    pn = p * inv
    pc_s[...] = pn
    pn_s[:, 0:CHUNK] = jnp.where(tri, 0.0, pn)
    pn_s[:, CHUNK:2 * CHUNK] = jnp.where(tri, pn, 0.0)
    return e_sink * inv


def _att2_fwd(u, cos_t, sin_t, sinks, name):
    t_len = u.shape[1]
    nb = t_len // CHUNK

    def body(u_ref, prev_ref, cc_ref, sc_ref, cp_ref, sp_ref, sink_ref, a_ref,
             q_s, kp_s, kc_s, vp_s, vc_s, o_s, s_s, pn_s, pc_s):
        n = pl.program_id(0)
        tri = _tri4()
        cos_c, sin_c = cc_ref[...], sc_ref[...]
        kc_s[...] = _rope(u_ref[4], cos_c, sin_c)
        kp_s[...] = _rope(prev_ref[0], cp_ref[...], sp_ref[...])
        vc_s[...] = u_ref[5]
        vp_s[...] = prev_ref[1]
        sinks = sink_ref[...]
        for g in range(4):
            q_g, o_g, s_g, pn_g = q_s.at[g], o_s.at[g], s_s.at[g], pn_s.at[g]
            q_g[...] = _rope(u_ref[g], cos_c, sin_c)
            kv = slice(64 * g, 64 * (g + 1))
            kb = jnp.concatenate([kp_s[:, kv], kc_s[:, kv]], axis=0)
            vb = jnp.concatenate([vp_s[:, kv], vc_s[:, kv]], axis=0)
            s_g[...] = _dot_nt(_stack_heads(q_g), kb)
            _softmax_rows(s_g, pn_g, pc_s.at[g], _sink_col(sinks, g), tri, n > 0)
            o = _dot(pn_g[...], vb)
            for r in range(4):
                o_g[:, 64 * r:64 * (r + 1)] = o[CHUNK * r:CHUNK * (r + 1), :]
            a_ref[g] = (o_g[...] * _silu(u_ref[6 + g])).astype(a_ref.dtype)

    return _pc(
        body, name, grid=(nb,),
        in_specs=_att_in_specs(nb, False),
        out_specs=pl.BlockSpec((ATT_KT, CHUNK, TILE), lambda n: (0, n, 0)),
        out_shape=SDS((ATT_KT, t_len, TILE), MXU_DTYPE),
        scratch_shapes=[pltpu.VMEM((4, CHUNK, TILE), f32)] + [pltpu.VMEM((CHUNK, TILE), f32)] * 4
                       + [pltpu.VMEM((4, CHUNK, TILE), f32)] + [pltpu.VMEM((4, ATT_ROWS, 2 * CHUNK), f32)] * 2
                       + [pltpu.VMEM((4, ATT_ROWS, CHUNK), f32)],
        compiler_params=_params(("arbitrary",)),
    )(u, u, cos_t, sin_t, cos_t, sin_t, sinks)


def _att2_bwd(u, cos_t, sin_t, sinks, da, name):
    t_len = u.shape[1]
    nb = t_len // CHUNK

    def body(u_ref, prev_ref, cc_ref, sc_ref, cp_ref, sp_ref, sink_ref, da_ref, du_ref, dsink_ref,
             ck_s, cv_s, kp_s, kc_s, vp_s, vc_s, q_s, o_s, do_s, dq_s, s_s, pn_s, dp_s, dkt_s, dvt_s, pc_s):
        step = pl.program_id(0)
        nn = nb - 1 - step

        @pl.when(step == 0)
        def _():
            ck_s[...] = jnp.zeros_like(ck_s)
            cv_s[...] = jnp.zeros_like(cv_s)
            dsink_ref[...] = jnp.zeros_like(dsink_ref)
        tri = _tri4()
        cos_c, sin_c = cc_ref[...], sc_ref[...]
        cos_p, sin_p = cp_ref[...], sp_ref[...]
        kc_s[...] = _rope(u_ref[4], cos_c, sin_c)
        kp_s[...] = _rope(prev_ref[0], cos_p, sin_p)
        vc_s[...] = u_ref[5]
        vp_s[...] = prev_ref[1]
        sinks = sink_ref[...]
        lane16 = lax.broadcasted_iota(jnp.int32, (1, 16), 1)
        dsink = jnp.zeros((1, 16), f32)
        for g in range(4):
            q_g, o_g, do_g, dq_g = q_s.at[g], o_s.at[g], do_s.at[g], dq_s.at[g]
            s_g, pn_g, dp_g, pc_g = s_s.at[g], pn_s.at[g], dp_s.at[g], pc_s.at[g]
            q_g[...] = _rope(u_ref[g], cos_c, sin_c)
            gate = u_ref[6 + g]
            sg = _sigmoid(gate)
            dav = da_ref[g]
            do_g[...] = dav * (gate * sg)
            kv = slice(64 * g, 64 * (g + 1))
            kb = jnp.concatenate([kp_s[:, kv], kc_s[:, kv]], axis=0)
            vb = jnp.concatenate([vp_s[:, kv], vc_s[:, kv]], axis=0)
            q_st = _stack_heads(q_g)
            do_st = _stack_heads(do_g)
            s_g[...] = _dot_nt(q_st, kb)
            p_sink = _softmax_rows(s_g, pn_g, pc_g, _sink_col(sinks, g), tri, nn > 0)
            o = _dot(pn_g[...], vb)
            dvt_s[64 * g:64 * (g + 1), :] = _dot_tn(do_st, pn_g[...])
            dp_g[...] = _dot_nt(do_st, vb)
            delta = jnp.sum(do_st * o, axis=-1, keepdims=True)
            dpc = jnp.where(tri, dp_g[:, CHUNK:2 * CHUNK], dp_g[:, 0:CHUNK])
            dsc = pc_g[...] * (dpc - delta) * ATT_SCALE
            dp_g[:, 0:CHUNK] = jnp.where(tri, 0.0, dsc)
            dp_g[:, CHUNK:2 * CHUNK] = jnp.where(tri, dsc, 0.0)
            sd = p_sink * delta
            for r in range(4):
                rs = slice(CHUNK * r, CHUNK * (r + 1))
                o_g[:, 64 * r:64 * (r + 1)] = o[rs, :]
                ds_h = -jnp.sum(sd[rs, :], axis=0, keepdims=True)
                dsink = dsink + ds_h * (lane16 == 4 * g + r).astype(f32)
            ds = dp_g[...]
            dq = _dot(ds, kb)
            for r in range(4):
                dq_g[:, 64 * r:64 * (r + 1)] = dq[CHUNK * r:CHUNK * (r + 1), :]
            dkt_s[64 * g:64 * (g + 1), :] = _dot_tn(q_st, ds)
            du_ref[6 + g] = dav * o_g[...] * (sg * (1.0 + gate * (1.0 - sg)))
            du_ref[g] = _rope_bwd(dq_g[...], cos_c, sin_c)
        dk = dkt_s[...].T
        dv = dvt_s[...].T
        du_ref[4] = _rope_bwd(dk[CHUNK:2 * CHUNK, :], cos_c, sin_c) + ck_s[...]
        du_ref[5] = dv[CHUNK:2 * CHUNK, :] + cv_s[...]
        ck_s[...] = _rope_bwd(dk[0:CHUNK, :], cos_p, sin_p)
        cv_s[...] = dv[0:CHUNK, :]
        dsink_ref[...] += dsink

    def rb(n):
        return nb - 1 - n

    return _pc(
        body, name, grid=(nb,),
        in_specs=_att_in_specs(nb, True) + [pl.BlockSpec((ATT_KT, CHUNK, TILE), lambda n: (0, rb(n), 0))],
        out_specs=[pl.BlockSpec((ATT_NT, CHUNK, TILE), lambda n: (0, rb(n), 0)),
                   pl.BlockSpec((1, 16), lambda n: (0, 0))],
        out_shape=[SDS((ATT_NT, t_len, TILE), f32), SDS((1, 16), f32)],
        scratch_shapes=[pltpu.VMEM((CHUNK, TILE), f32)] * 6 + [pltpu.VMEM((4, CHUNK, TILE), f32)] * 4
                       + [pltpu.VMEM((4, ATT_ROWS, 2 * CHUNK), f32)] * 3
                       + [pltpu.VMEM((2 * CHUNK, 2 * CHUNK), f32)] * 2 + [pltpu.VMEM((4, ATT_ROWS, CHUNK), f32)],
        compiler_params=_params(("arbitrary",)),
    )(u, u, cos_t, sin_t, cos_t, sin_t, sinks, da)


_HBM = pl.BlockSpec(memory_space=pltpu.HBM)


def _all_gather_big(shard):
    rows, width = shard.shape

    def body(x_ref, out_ref, send_sems, recv_sems, local_sem):
        x, y, c = lax.axis_index("x"), lax.axis_index("y"), lax.axis_index("c")
        me, sibling = (x, y, c), (x, y, 1 - c)
        chips = [(1 - x, y), (x, 1 - y), (1 - x, 1 - y)]

        def slot(px, py, pc):
            return out_ref.at[4 * px + 2 * py + pc]

        def copy(k, block, to, src=None):
            return pltpu.make_async_remote_copy(
                src_ref=slot(*block) if src is None else src, dst_ref=slot(*block),
                send_sem=send_sems.at[k], recv_sem=recv_sems.at[k], device_id=to, device_id_type=MESH)

        mine = pltpu.make_async_copy(x_ref, slot(*me), local_sem)
        mine.start()
        first = [copy(0, me, sibling, src=x_ref)]
        first += [copy(1 + j, me, (*chip, c), src=x_ref) for j, chip in enumerate(chips)]
        for cp in first:
            cp.start()
        passed = [copy(4 + j, (*chip, c), sibling) for j, chip in enumerate(chips)]
        for j, chip in enumerate(chips):
            copy(1 + j, (*chip, c), me).wait_recv()
            passed[j].start()
        copy(0, sibling, me).wait_recv()
        for j, chip in enumerate(chips):
            copy(4 + j, (*chip, 1 - c), me).wait_recv()
        for cp in first + passed:
            cp.wait_send()
        mine.wait()

    return _pc(
        body, "all_gather_big",
        in_specs=[_HBM], out_specs=_HBM,
        out_shape=SDS((N_DEV, rows, width), shard.dtype),
        scratch_shapes=[pltpu.SemaphoreType.DMA((7,)), pltpu.SemaphoreType.DMA((7,)), pltpu.SemaphoreType.DMA],
    )(shard)


def _all_gather_direct(block, name):
    rows, width = block.shape

    def body(x_ref, out_ref, send_sems, recv_sems, local_sem):
        x, y, c = lax.axis_index("x"), lax.axis_index("y"), lax.axis_index("c")
        my_slot = 4 * x + 2 * y + c

        def peer(k):
            return (1 - x if k & 4 else x, 1 - y if k & 2 else y, 1 - c if k & 1 else c)

        def copy(k):
            px, py, pc = peer(k)
            return pltpu.make_async_remote_copy(
                src_ref=x_ref, dst_ref=out_ref.at[my_slot], send_sem=send_sems.at[k - 1], recv_sem=recv_sems.at[k - 1],
                device_id=(px, py, pc), device_id_type=MESH)

        def arrival(k):
            px, py, pc = peer(k)
            return pltpu.make_async_remote_copy(
                src_ref=x_ref, dst_ref=out_ref.at[4 * px + 2 * py + pc], send_sem=send_sems.at[k - 1],
                recv_sem=recv_sems.at[k - 1], device_id=(px, py, pc), device_id_type=MESH)

        mine = pltpu.make_async_copy(x_ref, out_ref.at[my_slot], local_sem)
        mine.start()
        for k in range(1, N_DEV):
            copy(k).start()
        for k in range(1, N_DEV):
            arrival(k).wait_recv()
        for k in range(1, N_DEV):
            copy(k).wait_send()
        mine.wait()

    return _pc(
        body, name,
        in_specs=[_HBM], out_specs=_HBM,
        out_shape=SDS((N_DEV, rows, width), block.dtype),
        scratch_shapes=[pltpu.SemaphoreType.DMA((7,)), pltpu.SemaphoreType.DMA((7,)), pltpu.SemaphoreType.DMA],
    )(block)


def _exchange_sibling(g):
    _, rows, width = g.shape
    nchip = N_DEV // 2

    def body(g_ref, out_ref, send_sems, recv_sems):
        x, y, c = lax.axis_index("x"), lax.axis_index("y"), lax.axis_index("c")
        cps = [pltpu.make_async_remote_copy(
            src_ref=g_ref.at[2 * k + 1 - c], dst_ref=out_ref.at[k], send_sem=send_sems.at[k], recv_sem=recv_sems.at[k],
            device_id=(x, y, 1 - c), device_id_type=MESH) for k in range(nchip)]
        for cp in cps:
            cp.start()
        for cp in cps:
            cp.wait()

    return _pc(
        body, "rs_sibling",
        in_specs=[_HBM], out_specs=_HBM,
        out_shape=SDS((nchip, rows, width), g.dtype),
        scratch_shapes=[pltpu.SemaphoreType.DMA((nchip,)), pltpu.SemaphoreType.DMA((nchip,))],
    )(g)


def _pair_sum(g, r1, cidx, tr):
    _, rows, width = g.shape
    nchip = N_DEV // 2

    def body(c_ref, g_ref, r_ref, o_ref):
        o_ref[...] = (g_ref[...].astype(f32) + r_ref[...].astype(f32)).astype(o_ref.dtype)

    return pl.pallas_call(
        body, name="rs_pair_sum",
        grid_spec=pltpu.PrefetchScalarGridSpec(
            num_scalar_prefetch=1, grid=(nchip, rows // tr),
            in_specs=[pl.BlockSpec((1, tr, width), lambda k, i, c_ref: (2 * k + c_ref[0], i, 0)),
                      pl.BlockSpec((1, tr, width), lambda k, i, c_ref: (k, i, 0))],
            out_specs=pl.BlockSpec((1, tr, width), lambda k, i, c_ref: (k, i, 0))),
        out_shape=SDS((nchip, rows, width), g.dtype),
        compiler_params=_params(("arbitrary", "arbitrary")),
    )(cidx, g, r1)


def _exchange_chips(p):
    nchip, rows, width = p.shape

    def body(p_ref, out_ref, send_sems, recv_sems, local_sem):
        x, y, c = lax.axis_index("x"), lax.axis_index("y"), lax.axis_index("c")
        my_chip = 2 * x + y
        chips = [(1 - x, y), (x, 1 - y), (1 - x, 1 - y)]

        def copy(j):
            px, py = chips[j]
            return pltpu.make_async_remote_copy(
                src_ref=p_ref.at[2 * px + py], dst_ref=out_ref.at[my_chip], send_sem=send_sems.at[j],
                recv_sem=recv_sems.at[j], device_id=(px, py, c), device_id_type=MESH)

        def arrival(j):
            px, py = chips[j]
            return pltpu.make_async_remote_copy(
                src_ref=p_ref.at[my_chip], dst_ref=out_ref.at[2 * px + py], send_sem=send_sems.at[j],
                recv_sem=recv_sems.at[j], device_id=(px, py, c), device_id_type=MESH)

        mine = pltpu.make_async_copy(p_ref.at[my_chip], out_ref.at[my_chip], local_sem)
        mine.start()
        for j in range(3):
            copy(j).start()
        for j in range(3):
            arrival(j).wait_recv()
        for j in range(3):
            copy(j).wait_send()
        mine.wait()

    return _pc(
        body, "rs_chips",
        in_specs=[_HBM], out_specs=_HBM,
        out_shape=SDS((nchip, rows, width), p.dtype),
        scratch_shapes=[pltpu.SemaphoreType.DMA((3,)), pltpu.SemaphoreType.DMA((3,)), pltpu.SemaphoreType.DMA],
    )(p)


def _adamw(parts, w, m, v, tr, name):
    n, rows, width = parts.shape
    c1 = 1.0 / (1.0 - ADAM_B1 ** ADAM_STEP)
    c2 = 1.0 / (1.0 - ADAM_B2 ** ADAM_STEP)

    def body(p_ref, w_ref, m_ref, v_ref, g_ref, d_ref, mo_ref, vo_ref):
        g = p_ref[0].astype(f32)
        for k in range(1, n):
            g = g + p_ref[k].astype(f32)
        mn = ADAM_B1 * m_ref[...] + (1.0 - ADAM_B1) * g
        vn = ADAM_B2 * v_ref[...] + (1.0 - ADAM_B2) * (g * g)
        g_ref[...] = g
        mo_ref[...] = mn
        vo_ref[...] = vn
        d_ref[...] = -ADAM_LR * ((mn * c1) / (jnp.sqrt(vn * c2) + ADAM_EPS) + ADAM_WD * w_ref[...])

    blk = pl.BlockSpec((tr, width), lambda i: (i, 0))
    return _pc(
        body, name, grid=(rows // tr,),
        in_specs=[pl.BlockSpec((n, tr, width), lambda i: (0, i, 0)), blk, blk, blk],
        out_specs=[blk, blk, blk, blk],
        out_shape=[SDS((rows, width), f32)] * 4,
        compiler_params=_params(("arbitrary",)),
    )(parts, w, m, v)


def _pack_big(ssm_w_in, ssm_w_out, att_w_in, att_w_out, conv_w=None):
    parts = [ssm_w_in.reshape(ROWS_SSM_IN, 1024), ssm_w_out.reshape(ROWS_SSM_OUT, 1024),
             att_w_in.reshape(ROWS_ATT_IN, 1024), att_w_out.reshape(ROWS_ATT_OUT, 1024)]
    if conv_w is None:
        parts.append(jnp.zeros((ROWS_CONV, 1024), f32))
    else:
        parts.append(jnp.concatenate([conv_w.reshape(4, 1024), jnp.zeros((4, 1024), f32)], axis=0))
    return jnp.concatenate(parts, axis=0)


def _unpack_big(p):
    o = 0
    out = []
    for rows, shape in ((ROWS_SSM_IN, (2, 1024, 772)), (ROWS_SSM_OUT, (2, 256, 1024)),
                        (ROWS_ATT_IN, (2, 1024, 320)), (ROWS_ATT_OUT, (2, 128, 1024))):
        out.append(p[o:o + rows].reshape(shape))
        o += rows
    out.append(p[o:o + 4].reshape(2, 4, 512))
    return out


def _pack_grads(d_ssm_w_in, d_ssm_w_out, d_att_w_in, d_att_w_out, d_conv_w):
    wire = lambda t: t.astype(MXU_DTYPE)
    a = jnp.transpose(wire(d_ssm_w_in).reshape(2, 1024, 8, 772), (2, 0, 1, 3)).reshape(8, ROWS_SSM_IN, 1024)
    b = jnp.transpose(wire(d_ssm_w_out).reshape(2, 8, 256, 1024), (1, 0, 2, 3)).reshape(8, ROWS_SSM_OUT, 1024)
    c = jnp.transpose(wire(d_att_w_in).reshape(2, 1024, 8, 320), (2, 0, 1, 3)).reshape(8, ROWS_ATT_IN, 1024)
    d = jnp.transpose(wire(d_att_w_out).reshape(2, 8, 128, 1024), (1, 0, 2, 3)).reshape(8, ROWS_ATT_OUT, 1024)
    e = jnp.transpose(wire(d_conv_w).reshape(2, 4, 8, 512), (2, 0, 1, 3)).reshape(8, 4, 1024)
    e = jnp.concatenate([e, jnp.zeros((8, 4, 1024), MXU_DTYPE)], axis=1)
    return jnp.concatenate([a, b, c, d, e], axis=1)


def _pad8(a):
    return jnp.pad(a, ((0, 8 - a.shape[0]), (0, 0)))


def _pack_small(pre_norm, post_norm, conv_b, gate_norm, dt_bias, a_log, d_skip, sinks, extra=None):
    row = jnp.concatenate([dt_bias.reshape(1, 64), a_log.reshape(1, 64), d_skip.reshape(1, 64), sinks.reshape(1, 32),
                           jnp.zeros((1, 1024 - 224), f32)], axis=1)
    if extra is not None:
        row = row + jnp.pad(extra.reshape(1, 1), ((0, 0), (224, 1024 - 225)))
    return jnp.concatenate([_pad8(pre_norm.reshape(4, 1024)), _pad8(post_norm.reshape(4, 1024)),
                            conv_b.reshape(8, 1024), _pad8(gate_norm.reshape(4, 1024)), _pad8(row)], axis=0)


def _unpack_small(p):
    row = p[32]
    return (p[0:4], p[8:12], p[16:24].reshape(2, 4096), row[0:64].reshape(2, 32), row[64:128].reshape(2, 32),
            row[128:192].reshape(2, 32), p[24:28].reshape(2, 2048), row[192:224].reshape(2, 16))


def _ssm_w_in_tiles(w):
    wb = w[:, 4096:5120].reshape(1024, 8, 128)
    wc = w[:, 5120:6144].reshape(1024, 8, 128)
    wbc = jnp.concatenate([wb, wc], axis=2).reshape(1024, 2048)
    return jnp.concatenate([w[:, 0:4096], wbc, w[:, 6144:6176], jnp.zeros((1024, 224), w.dtype)], axis=1)


def _ssm_w_in_untile(dw):
    dbc = dw[:, 4096:6144].reshape(1024, 8, 256)
    return jnp.concatenate([dw[:, 0:4096], dbc[:, :, 0:128].reshape(1024, 1024), dbc[:, :, 128:256].reshape(1024, 1024),
                            dw[:, 6144:6176]], axis=1)


def _conv_tiles(cw):
    k = cw.shape[0]
    xs = jnp.transpose(cw[:, 0:2048].reshape(k, 8, 256), (1, 0, 2))
    b = cw[:, 2048:3072].reshape(k, 8, 128)
    c = cw[:, 3072:4096].reshape(k, 8, 128)
    bc = jnp.transpose(jnp.concatenate([b, c], axis=2), (1, 0, 2))
    return jnp.concatenate([xs, bc], axis=0)


def _conv_untile(t):
    k = t.shape[1]
    xs = jnp.transpose(t[0:8], (1, 0, 2)).reshape(k, 2048)
    bc = jnp.transpose(t[8:16], (1, 0, 2))
    return jnp.concatenate([xs, bc[:, :, 0:128].reshape(k, 1024), bc[:, :, 128:256].reshape(k, 1024)], axis=1)


def _rope_tables(positions):
    inv = ROPE_THETA ** (-jnp.arange(0, 16, 2, dtype=f32) / 16)
    ang = positions.astype(f32).reshape(-1, 1) * inv
    cos, sin = jnp.cos(ang), jnp.sin(ang)
    t_len = ang.shape[0]
    cos64 = jnp.concatenate([cos, cos, jnp.ones((t_len, 48), f32)], axis=1)
    sin64 = jnp.concatenate([-sin, sin, jnp.zeros((t_len, 48), f32)], axis=1)
    return jnp.tile(cos64, (1, 4)), jnp.tile(sin64, (1, 4))


def kernel(x, positions, pre_norm, post_norm, ssm_w_in, ssm_conv_w, ssm_conv_b, ssm_dt_bias, ssm_a_log, ssm_d, ssm_gate_norm, ssm_w_out, att_w_in, att_sinks, att_w_out, loss_target, m_pre_norm, m_post_norm, m_ssm_w_in, m_ssm_conv_w, m_ssm_conv_b, m_ssm_dt_bias, m_ssm_a_log, m_ssm_d, m_ssm_gate_norm, m_ssm_w_out, m_att_w_in, m_att_sinks, m_att_w_out, v_pre_norm, v_post_norm, v_ssm_w_in, v_ssm_conv_w, v_ssm_conv_b, v_ssm_dt_bias, v_ssm_a_log, v_ssm_d, v_ssm_gate_norm, v_ssm_w_out, v_att_w_in, v_att_sinks, v_att_w_out):
    t_len = x.shape[1]
    tm = min(1024, t_len)
    xin = x.reshape(t_len, D_MODEL)
    tgt = loss_target.reshape(t_len, D_MODEL)
    cidx = lax.axis_index("c").astype(jnp.int32).reshape(1)

    w_local = _pack_big(ssm_w_in, ssm_w_out, att_w_in, att_w_out)
    gathered = _all_gather_big(w_local.astype(MXU_DTYPE))
    conv_local = jnp.concatenate([ssm_conv_w.reshape(4, 1024), jnp.zeros((4, 1024), f32)], axis=0)
    conv_all = _all_gather_direct(conv_local, "all_gather_conv")[:, 0:4]
    o = 0
    g_ssm_in = gathered[:, o:o + ROWS_SSM_IN].reshape(8, 2, 1024, 772); o += ROWS_SSM_IN
    g_ssm_out = gathered[:, o:o + ROWS_SSM_OUT].reshape(8, 2, 256, 1024); o += ROWS_SSM_OUT
    g_att_in = gathered[:, o:o + ROWS_ATT_IN].reshape(8, 2, 1024, 320); o += ROWS_ATT_IN
    g_att_out = gathered[:, o:o + ROWS_ATT_OUT].reshape(8, 2, 128, 1024)
    w_ssm_in = jnp.transpose(g_ssm_in, (1, 2, 0, 3)).reshape(2, 1024, SSM_IN)
    w_ssm_out = jnp.transpose(g_ssm_out, (1, 0, 2, 3)).reshape(2, SSM_INNER, 1024)
    w_att_in = jnp.transpose(g_att_in, (1, 2, 0, 3)).reshape(2, 1024, ATT_IN)
    w_att_out = jnp.transpose(g_att_out, (1, 0, 2, 3)).reshape(2, 1024, 1024)
    conv_w = jnp.transpose(conv_all.reshape(8, 2, 4, 512), (1, 2, 0, 3)).reshape(2, 4, 4096)
    cos_t, sin_t = _rope_tables(positions)

    saved = []
    xc = xin
    for i in range(4):
        j = i // 2
        wn_pre, wn_post = pre_norm[i].reshape(1, D_MODEL), post_norm[i].reshape(1, D_MODEL)
        if i % 2 == 0:
            w_in = _ssm_w_in_tiles(w_ssm_in[j])
            cw, cb = _conv_tiles(conv_w[j]), _conv_tiles(ssm_conv_b[j].reshape(1, 4096))
            dtb, alog, dsk = ssm_dt_bias[j].reshape(1, 32), ssm_a_log[j].reshape(1, 32), ssm_d[j].reshape(1, 32)
            gn = ssm_gate_norm[j].reshape(SSM_KT, 1, TILE)
            u, h = _mm_in(xc, wn_pre, w_in, 5, tm, f"ssm_in_{j}")
            xbc = _conv_fwd(u, cw, cb, f"ssm_conv_{j}")
            a3, yp, hst = _ssm2_fwd(u, xbc, dtb, alog, dsk, gn, f"ssm_core_{j}")
            y, xn = _mm_out(a3, w_ssm_out[j], xc, wn_post, 4, tm, f"ssm_out_{j}")
            saved.append(dict(x=xc, u=u, h=h, a=a3, yp=yp, hst=hst, y=y, w_in=w_in, cw=cw, cb=cb, dtb=dtb, alog=alog,
                              dsk=dsk, gn=gn, xbc=xbc))
        else:
            sinks = att_sinks[j].reshape(1, 16)
            u, h = _mm_in(xc, wn_pre, w_att_in[j], 5, tm, f"att_in_{j}")
            a = _att2_fwd(u, cos_t, sin_t, sinks, f"att_core_{j}")
            y, xn = _mm_out(a, w_att_out[j], xc, wn_post, 4, tm, f"att_out_{j}")
            saved.append(dict(x=xc, u=u, h=h, a=a, y=y, sinks=sinks))
        xc = xn

    dx, loss_part = _loss_grad(xc, tgt, tm)

    d_pre, d_post = [None] * 4, [None] * 4
    d_ssm_in, d_ssm_out, d_att_in, d_att_out = [None] * 2, [None] * 2, [None] * 2, [None] * 2
    d_cw, d_cb, d_dtb, d_alog, d_dsk, d_gn, d_sinks = ([None] * 2 for _ in range(7))
    for i in reversed(range(4)):
        j = i // 2
        s = saved[i]
        wn_pre, wn_post = pre_norm[i].reshape(1, D_MODEL), post_norm[i].reshape(1, D_MODEL)
        if i % 2 == 0:
            da3, dy, d_post[i] = _mm_dout(s["y"], dx, wn_post, w_ssm_out[j], 4, tm, f"ssm_dout_{j}")
            d_ssm_out[j] = _dw_rows(s["a"], dy, 4, tm, f"ssm_dwout_{j}")
            du, d_dtb[j], d_alog[j], d_dsk[j], dgn = _ssm2_bwd(
                s["u"], s["xbc"], s["yp"], s["hst"], da3, s["dtb"], s["alog"], s["dsk"], s["gn"],
                f"ssm_core_bwd_{j}")
            du, dcw, dcb = _conv_bwd(s["u"], du, s["cw"], s["cb"], f"ssm_conv_bwd_{j}")
            d_cw[j], d_cb[j], d_gn[j] = _conv_untile(dcw), _conv_untile(dcb), dgn.reshape(1, SSM_INNER)
            d_ssm_in[j] = _ssm_w_in_untile(_dw_cols(s["h"], du, 5, tm, f"ssm_dwin_{j}"))
            dx, d_pre[i] = _mm_dh(du, s["w_in"], s["x"], dx, wn_pre, 5, tm, f"ssm_dh_{j}")
        else:
            da, dy, d_post[i] = _mm_dout(s["y"], dx, wn_post, w_att_out[j], 4, tm, f"att_dout_{j}")
            d_att_out[j] = _dw_rows(s["a"], dy, 4, tm, f"att_dwout_{j}")
            du, d_sinks[j] = _att2_bwd(s["u"], cos_t, sin_t, s["sinks"], da, f"att_core_bwd_{j}")
            d_att_in[j] = _dw_cols(s["h"], du, 5, tm, f"att_dwin_{j}")
            dx, d_pre[i] = _mm_dh(du, w_att_in[j], s["x"], dx, wn_pre, 5, tm, f"att_dh_{j}")

    g2 = _pack_grads(jnp.stack(d_ssm_in), jnp.stack(d_ssm_out), jnp.stack(d_att_in), jnp.stack(d_att_out),
                     jnp.stack(d_cw))
    r1 = _exchange_sibling(g2)
    pair = _pair_sum(g2, r1, cidx, ROWS_BIG // 5)
    parts = _exchange_chips(pair)
    w_p = _pack_big(ssm_w_in, ssm_w_out, att_w_in, att_w_out, ssm_conv_w)
    m_p = _pack_big(m_ssm_w_in, m_ssm_w_out, m_att_w_in, m_att_w_out, m_ssm_conv_w)
    v_p = _pack_big(v_ssm_w_in, v_ssm_w_out, v_att_w_in, v_att_w_out, v_ssm_conv_w)
    big = [_unpack_big(t) for t in _adamw(parts, w_p, m_p, v_p, ROWS_BIG // 5, "adamw_big")]

    small_local = _pack_small(jnp.concatenate(d_pre, axis=0), jnp.concatenate(d_post, axis=0),
                              jnp.concatenate(d_cb, axis=0), jnp.concatenate(d_gn, axis=0),
                              jnp.concatenate(d_dtb, axis=0), jnp.concatenate(d_alog, axis=0),
                              jnp.concatenate(d_dsk, axis=0), jnp.concatenate(d_sinks, axis=0), loss_part[0, 0])
    small_all = _all_gather_direct(small_local, "all_gather_small")
    ws = _pack_small(pre_norm, post_norm, ssm_conv_b, ssm_gate_norm, ssm_dt_bias, ssm_a_log, ssm_d, att_sinks)
    ms = _pack_small(m_pre_norm, m_post_norm, m_ssm_conv_b, m_ssm_gate_norm, m_ssm_dt_bias, m_ssm_a_log, m_ssm_d,
                     m_att_sinks)
    vs = _pack_small(v_pre_norm, v_post_norm, v_ssm_conv_b, v_ssm_gate_norm, v_ssm_dt_bias, v_ssm_a_log, v_ssm_d,
                     v_att_sinks)
    small4 = _adamw(small_all, ws, ms, vs, ROWS_SMALL, "adamw_small")
    loss = small4[0][32, 224]
    small = [_unpack_small(t) for t in small4]

    outs = [loss, dx.reshape(1, t_len, D_MODEL)]
    for k in range(4):
        b_ssm_in, b_ssm_out, b_att_in, b_att_out, b_conv = big[k]
        s_pre, s_post, s_cb, s_dtb, s_alog, s_d, s_gn, s_sinks = small[k]
        outs += [s_pre, s_post, b_ssm_in, b_conv, s_cb, s_dtb, s_alog, s_d, s_gn, b_ssm_out, b_att_in, s_sinks,
                 b_att_out]
    return tuple(outs)
```

```python
import functools

import jax
import jax.numpy as jnp
from jax import lax
from jax.experimental import pallas as pl
from jax.experimental.pallas import tpu as pltpu

f32 = jnp.float32
MXU_DTYPE = jnp.bfloat16
SDS = jax.ShapeDtypeStruct
MESH = pl.DeviceIdType.MESH

D_MODEL = 1024
EPS = 1e-6
TILE = 256
CHUNK = 128
SSM_HEADS = 32
SSM_GROUPS = 8
SSM_P = 64
SSM_N = 128
SSM_INNER = 2048
SSM_IN = 6176
SSM_NT = 25
SSM_KT = 8
ATT_NT = 10
ATT_KT = 4
ATT_IN = 2560
ROPE_THETA = 500000.0
N_DEV = 8
VMEM_LIMIT = 56 * 1024 * 1024

ADAM_LR = 0.001
ADAM_B1 = 0.9
ADAM_B2 = 0.999
ADAM_EPS = 1e-08
ADAM_WD = 0.01
ADAM_STEP = 10

ROWS_SSM_IN = 2 * 1024 * 772 // 1024
ROWS_SSM_OUT = 2 * 256
ROWS_ATT_IN = 2 * 1024 * 320 // 1024
ROWS_ATT_OUT = 2 * 128
ROWS_CONV = 8
ROWS_BIG = ROWS_SSM_IN + ROWS_SSM_OUT + ROWS_ATT_IN + ROWS_ATT_OUT + ROWS_CONV
ROWS_SMALL = 40


def _pc(body, name, **kw):
    return pl.pallas_call(body, name=name, **kw)


def _params(sem):
    return pltpu.CompilerParams(dimension_semantics=sem, vmem_limit_bytes=VMEM_LIMIT)


def _sigmoid(x):
    return 1.0 / (1.0 + jnp.exp(-x))


def _silu(x):
    return x * _sigmoid(x)


def _softplus(x):
    return jnp.maximum(x, 0.0) + jnp.log(1.0 + jnp.exp(-jnp.abs(x)))


def _mx(x):
    return x.astype(MXU_DTYPE)


def _dot(a, b):
    return jnp.dot(_mx(a), _mx(b), preferred_element_type=f32)


def _dot_nt(a, b):
    return lax.dot_general(_mx(a), _mx(b), (((1,), (1,)), ((), ())), preferred_element_type=f32)


def _dot_tn(a, b):
    return lax.dot_general(_mx(a), _mx(b), (((0,), (0,)), ((), ())), preferred_element_type=f32)


def _rms_fwd(x, w):
    r = lax.rsqrt(jnp.mean(x * x, axis=-1, keepdims=True) + EPS)
    return x * r * w


def _rms_bwd(x, w, dy):
    r = lax.rsqrt(jnp.mean(x * x, axis=-1, keepdims=True) + EPS)
    xh = x * r
    dw = jnp.sum(dy * xh, axis=0, keepdims=True)
    g = dy * w
    dx = r * (g - xh * jnp.mean(g * xh, axis=-1, keepdims=True))
    return dx, dw


def _mm_in(x, wn, w, ntb, tm, name):
    t_len, d = x.shape
    nt = w.shape[1] // TILE

    def body(x_ref, wn_ref, w_ref, u_ref, h_ref):
        @pl.when(pl.program_id(1) == 0)
        def _():
            h_ref[...] = _rms_fwd(x_ref[...], wn_ref[...]).astype(h_ref.dtype)
        h = h_ref[...]
        for t in range(ntb):
            u_ref[t] = jnp.dot(h, w_ref[:, TILE * t:TILE * (t + 1)], preferred_element_type=f32)

    return _pc(
        body, name, grid=(t_len // tm, nt // ntb),
        in_specs=[pl.BlockSpec((tm, d), lambda i, j: (i, 0)),
                  pl.BlockSpec((1, d), lambda i, j: (0, 0)),
                  pl.BlockSpec((d, ntb * TILE), lambda i, j: (0, j))],
        out_specs=[pl.BlockSpec((ntb, tm, TILE), lambda i, j: (j, i, 0)),
                   pl.BlockSpec((tm, d), lambda i, j: (i, 0))],
        out_shape=[SDS((nt, t_len, TILE), f32), SDS((t_len, d), MXU_DTYPE)],
        compiler_params=_params(("arbitrary", "arbitrary")),
    )(x, wn, w)


def _mm_dout(y, dxn, wn, w, ntb, tm, name):
    t_len, d = y.shape
    nt = w.shape[0] // TILE

    def body(y_ref, dxn_ref, wn_ref, w_ref, da_ref, dy_ref, dwn_ref):
        i, j = pl.program_id(0), pl.program_id(1)

        @pl.when((i == 0) & (j == 0))
        def _():
            dwn_ref[...] = jnp.zeros_like(dwn_ref)

        @pl.when(j == 0)
        def _():
            dy, dw = _rms_bwd(y_ref[...], wn_ref[...], dxn_ref[...])
            dy_ref[...] = dy.astype(dy_ref.dtype)
            dwn_ref[...] += dw
        dy = dy_ref[...]
        for t in range(ntb):
            da_ref[t] = _dot_nt(dy, w_ref[TILE * t:TILE * (t + 1), :])

    return _pc(
        body, name, grid=(t_len // tm, nt // ntb),
        in_specs=[pl.BlockSpec((tm, d), lambda i, j: (i, 0)),
                  pl.BlockSpec((tm, d), lambda i, j: (i, 0)),
                  pl.BlockSpec((1, d), lambda i, j: (0, 0)),
                  pl.BlockSpec((ntb * TILE, d), lambda i, j: (j, 0))],
        out_specs=[pl.BlockSpec((ntb, tm, TILE), lambda i, j: (j, i, 0)),
                   pl.BlockSpec((tm, d), lambda i, j: (i, 0)),
                   pl.BlockSpec((1, d), lambda i, j: (0, 0))],
        out_shape=[SDS((nt, t_len, TILE), f32), SDS((t_len, d), MXU_DTYPE), SDS((1, d), f32)],
        compiler_params=_params(("arbitrary", "arbitrary")),
    )(y, dxn, wn, w)


def _mm_out(a, w, x, wn, ktb, tm, name):
    kt, t_len, _ = a.shape
    d = w.shape[1]
    nk = kt // ktb

    def body(a_ref, w_ref, x_ref, wn_ref, y_ref, xn_ref, acc):
        k = pl.program_id(1)

        @pl.when(k == 0)
        def _():
            acc[...] = jnp.zeros_like(acc)
        s = acc[...]
        for t in range(ktb):
            s = s + jnp.dot(a_ref[t], w_ref[TILE * t:TILE * (t + 1), :], preferred_element_type=f32)
        acc[...] = s

        @pl.when(k == nk - 1)
        def _():
            y = acc[...]
            y_ref[...] = y
            xn_ref[...] = x_ref[...] + _rms_fwd(y, wn_ref[...])

    return _pc(
        body, name, grid=(t_len // tm, nk),
        in_specs=[pl.BlockSpec((ktb, tm, TILE), lambda i, k: (k, i, 0)),
                  pl.BlockSpec((ktb * TILE, d), lambda i, k: (k, 0)),
                  pl.BlockSpec((tm, d), lambda i, k: (i, 0)),
                  pl.BlockSpec((1, d), lambda i, k: (0, 0))],
        out_specs=[pl.BlockSpec((tm, d), lambda i, k: (i, 0)),
                   pl.BlockSpec((tm, d), lambda i, k: (i, 0))],
        out_shape=[SDS((t_len, d), f32), SDS((t_len, d), f32)],
        scratch_shapes=[pltpu.VMEM((tm, d), f32)],
        compiler_params=_params(("arbitrary", "arbitrary")),
    )(a, w, x, wn)


def _mm_dh(du, w, x, dxn, wn, ktb, tm, name):
    kt, t_len, _ = du.shape
    d = w.shape[0]
    nk = kt // ktb

    def body(du_ref, w_ref, x_ref, dxn_ref, wn_ref, dx_ref, dwn_ref, acc):
        i, k = pl.program_id(0), pl.program_id(1)

        @pl.when((i == 0) & (k == 0))
        def _():
            dwn_ref[...] = jnp.zeros_like(dwn_ref)

        @pl.when(k == 0)
        def _():
            acc[...] = jnp.zeros_like(acc)
        s = acc[...]
        for t in range(ktb):
            s = s + _dot_nt(du_ref[t], w_ref[:, TILE * t:TILE * (t + 1)])
        acc[...] = s

        @pl.when(k == nk - 1)
        def _():
            dxp, dw = _rms_bwd(x_ref[...], wn_ref[...], acc[...])
            dx_ref[...] = dxn_ref[...] + dxp
            dwn_ref[...] += dw

    return _pc(
        body, name, grid=(t_len // tm, nk),
        in_specs=[pl.BlockSpec((ktb, tm, TILE), lambda i, k: (k, i, 0)),
                  pl.BlockSpec((d, ktb * TILE), lambda i, k: (0, k)),
                  pl.BlockSpec((tm, d), lambda i, k: (i, 0)),
                  pl.BlockSpec((tm, d), lambda i, k: (i, 0)),
                  pl.BlockSpec((1, d), lambda i, k: (0, 0))],
        out_specs=[pl.BlockSpec((tm, d), lambda i, k: (i, 0)),
                   pl.BlockSpec((1, d), lambda i, k: (0, 0))],
        out_shape=[SDS((t_len, d), f32), SDS((1, d), f32)],
        scratch_shapes=[pltpu.VMEM((tm, d), f32)],
        compiler_params=_params(("arbitrary", "arbitrary")),
    )(du, w, x, dxn, wn)


def _dw_cols(a, b, ntb, tk, name):
    t_len, kdim = a.shape
    nt = b.shape[0]

    def body(a_ref, b_ref, o_ref):
        @pl.when(pl.program_id(1) == 0)
        def _():
            o_ref[...] = jnp.zeros_like(o_ref)
        av = a_ref[...]
        for s in range(ntb):
            o_ref[:, TILE * s:TILE * (s + 1)] += _dot_tn(av, b_ref[s])

    return _pc(
        body, name, grid=(nt // ntb, t_len // tk),
        in_specs=[pl.BlockSpec((tk, kdim), lambda j, t: (t, 0)),
                  pl.BlockSpec((ntb, tk, TILE), lambda j, t: (j, t, 0))],
        out_specs=pl.BlockSpec((kdim, ntb * TILE), lambda j, t: (0, j)),
        out_shape=SDS((kdim, nt * TILE), f32),
        compiler_params=_params(("arbitrary", "arbitrary")),
    )(a, b)


def _dw_rows(a, b, ktb, tk, name):
    kt, t_len, _ = a.shape
    d = b.shape[1]

    def body(a_ref, b_ref, o_ref):
        @pl.when(pl.program_id(1) == 0)
        def _():
            o_ref[...] = jnp.zeros_like(o_ref)
        bv = b_ref[...]
        for s in range(ktb):
            o_ref[TILE * s:TILE * (s + 1), :] += _dot_tn(a_ref[s], bv)

    return _pc(
        body, name, grid=(kt // ktb, t_len // tk),
        in_specs=[pl.BlockSpec((ktb, tk, TILE), lambda k, t: (k, t, 0)),
                  pl.BlockSpec((tk, d), lambda k, t: (t, 0))],
        out_specs=pl.BlockSpec((ktb * TILE, d), lambda k, t: (k, 0)),
        out_shape=SDS((kt * TILE, d), f32),
        compiler_params=_params(("arbitrary", "arbitrary")),
    )(a, b)


def _loss_grad(x, tgt, tm):
    t_len, d = x.shape

    def body(x_ref, t_ref, dx_ref, l_ref):
        @pl.when(pl.program_id(0) == 0)
        def _():
            l_ref[...] = jnp.zeros_like(l_ref)
        e = x_ref[...] - t_ref[...]
        dx_ref[...] = e * (1.0 / d)
        row = jnp.mean(e * e, axis=-1, keepdims=True)
        l_ref[...] += 0.5 * jnp.sum(row, axis=0, keepdims=True)

    return _pc(
        body, "loss_grad", grid=(t_len // tm,),
        in_specs=[pl.BlockSpec((tm, d), lambda i: (i, 0)), pl.BlockSpec((tm, d), lambda i: (i, 0))],
        out_specs=[pl.BlockSpec((tm, d), lambda i: (i, 0)), pl.BlockSpec((1, 128), lambda i: (0, 0))],
        out_shape=[SDS((t_len, d), f32), SDS((1, 128), f32)],
        compiler_params=_params(("arbitrary",)),
    )(x, tgt)


def _tri(lower):
    r = lax.broadcasted_iota(jnp.int32, (CHUNK, CHUNK), 0)
    c = lax.broadcasted_iota(jnp.int32, (CHUNK, CHUNK), 1)
    return ((c <= r) if lower else (c >= r)).astype(f32)


def _dot_hi(a, b, dims):
    return lax.dot_general(a, b, (dims, ((), ())), precision=lax.Precision.HIGHEST, preferred_element_type=f32)


def _split(x, n):
    parts = []
    for _ in range(n):
        p = x.astype(jnp.bfloat16)
        parts.append(p)
        x = x - p.astype(f32)
    return parts


def _dot_exact(a, b, dims, split_a, n=3):
    out = None
    if split_a:
        b = b.astype(jnp.bfloat16)
        for p in _split(a, n):
            t = lax.dot_general(p, b, (dims, ((), ())), preferred_element_type=f32)
            out = t if out is None else out + t
    else:
        a = a.astype(jnp.bfloat16)
        for p in _split(b, n):
            t = lax.dot_general(a, p, (dims, ((), ())), preferred_element_type=f32)
            out = t if out is None else out + t
    return out


def _dt_path(dt_raw, dtb, alog):
    dtr = dt_raw + dtb
    dt = _softplus(dtr)
    a_neg = -jnp.exp(alog)
    a = dt * a_neg
    acs = _dot_exact(_tri(True), a, ((1,), (0,)), False)
    acs_t = _dot_exact(a, _tri(False), ((0,), (0,)), True)
    return dtr, dt, a_neg, acs, acs_t


def _conv_silu(x0, x1, x2, x3, w0, w1, w2, w3, b):
    acc = b + w0 * x0 + w1 * x1 + w2 * x2 + w3 * x3
    return _silu(acc)


def _ssd_group(xs, bm, cm, dtc, ac, ar, dh, hp):
    row = lax.broadcasted_iota(jnp.int32, (CHUNK, CHUNK), 0)
    col = lax.broadcasted_iota(jnp.int32, (CHUNK, CHUNK), 1)
    causal = col <= row
    last = (lax.broadcasted_iota(jnp.int32, (1, CHUNK), 1) == CHUNK - 1).astype(f32)
    cb = _dot_nt(cm, bm)
    ys, hns = [], []
    for r in range(4):
        xt = xs[r] * dtc[r]
        decay = jnp.exp(jnp.where(causal, ac[r] - ar[r], -jnp.inf))
        y_diag = _dot(cb * decay, xt)
        a_end = jnp.sum(ar[r] * last, axis=1, keepdims=True)
        y_off = _dot_nt(cm, hp[r]) * jnp.exp(ac[r])
        st = _dot_tn(xt * jnp.exp(a_end - ac[r]), bm)
        hns.append(hp[r] * jnp.exp(a_end) + st)
        ys.append(y_diag + y_off + dh[r] * xs[r])
    return tuple(ys), tuple(hns)


def _head_cols(g, r, dt, acs, acs_t, dsk):
    lane = lax.broadcasted_iota(jnp.int32, (1, SSM_HEADS), 1)
    sub = lax.broadcasted_iota(jnp.int32, (SSM_HEADS, 1), 0)
    h = 4 * g + r
    oh_l = (lane == h).astype(f32)
    oh_s = (sub == h).astype(f32)
    dtc = jnp.sum(dt * oh_l, axis=1, keepdims=True)
    ac = jnp.sum(acs * oh_l, axis=1, keepdims=True)
    ar = jnp.sum(acs_t * oh_s, axis=0, keepdims=True)
    dh = jnp.sum(dsk * oh_l, axis=1, keepdims=True)
    return h, oh_l, oh_s, dtc, ac, ar, dh


def _ssm_in_specs(nc, rev):
    def cidx(c):
        return (nc - 1 - c) if rev else c
    return [
        pl.BlockSpec((SSM_NT, CHUNK, TILE), lambda c: (0, cidx(c), 0)),
        pl.BlockSpec((SSM_NT, 8, TILE), lambda c: (0, jnp.maximum(cidx(c) * (CHUNK // 8) - 1, 0), 0)),
        pl.BlockSpec((16, 4, TILE), lambda c: (0, 0, 0)),
        pl.BlockSpec((16, 1, TILE), lambda c: (0, 0, 0)),
        pl.BlockSpec((1, SSM_HEADS), lambda c: (0, 0)),
        pl.BlockSpec((1, SSM_HEADS), lambda c: (0, 0)),
        pl.BlockSpec((1, SSM_HEADS), lambda c: (0, 0)),
        pl.BlockSpec((SSM_KT, 1, TILE), lambda c: (0, 0, 0)),
    ]


def _ssm_fwd(u, cw, cb, dtb, alog, dsk, gn, name):
    t_len = u.shape[1]
    nc = t_len // CHUNK

    def body(u_ref, halo_ref, cw_ref, cb_ref, dtb_ref, alog_ref, dsk_ref, gn_ref,
             a3_ref, yp_ref, hst_ref, h_s, win_s, xs_s, bc_s):
        c = pl.program_id(0)

        @pl.when(c == 0)
        def _():
            h_s[...] = jnp.zeros_like(h_s)
        _, dt, _, acs, acs_t = _dt_path(u_ref[SSM_NT - 1, :, 0:SSM_HEADS], dtb_ref[...], alog_ref[...])
        dsk = dsk_ref[...]

        def conv_tile(j, p):
            win_s[0:8, :] = jnp.where(c > 0, halo_ref[j], 0.0)
            win_s[8:8 + CHUNK, :] = u_ref[j]
            return _conv_silu(win_s[5:5 + CHUNK, :], win_s[6:6 + CHUNK, :], win_s[7:7 + CHUNK, :],
                              win_s[8:8 + CHUNK, :], cw_ref[p, 0:1, :], cw_ref[p, 1:2, :], cw_ref[p, 2:3, :],
                              cw_ref[p, 3:4, :], cb_ref[p])

        def group(g, s1):
            xs_s[...] = conv_tile(8 + g, g)
            bc_s[...] = conv_tile(16 + g, 8 + g)
            bm, cm = bc_s[:, 0:SSM_N], bc_s[:, SSM_N:2 * SSM_N]
            hs, xs, dtc, ac, ar, dh, hp = [], [], [], [], [], [], []
            for r in range(4):
                h, _, _, dtc_r, ac_r, ar_r, dh_r = _head_cols(g, r, dt, acs, acs_t, dsk)
                hs.append(h); dtc.append(dtc_r); ac.append(ac_r); ar.append(ar_r); dh.append(dh_r)
                xs.append(xs_s[:, SSM_P * r:SSM_P * (r + 1)])
                hp.append(h_s[h])
            ys, hns = _ssd_group(xs, bm, cm, dtc, ac, ar, dh, hp)
            for r in range(4):
                hst_ref[0, hs[r]] = hp[r]
                h_s[hs[r]] = hns[r]
                yp_ref[g, :, SSM_P * r:SSM_P * (r + 1)] = ys[r]
            y2 = yp_ref[g] * _silu(u_ref[g])
            return s1 + jnp.sum(y2 * y2, axis=1, keepdims=True)

        s1 = lax.fori_loop(0, SSM_GROUPS, group, jnp.zeros((CHUNK, 1), f32))
        rinv = lax.rsqrt(s1 * (1.0 / SSM_INNER) + EPS)

        def gate(g, carry):
            y2 = yp_ref[g] * _silu(u_ref[g])
            a3_ref[g] = (y2 * rinv * gn_ref[g]).astype(a3_ref.dtype)
            return carry

        lax.fori_loop(0, SSM_GROUPS, gate, 0)

    return _pc(
        body, name, grid=(nc,),
        in_specs=_ssm_in_specs(nc, False),
        out_specs=[pl.BlockSpec((SSM_KT, CHUNK, TILE), lambda c: (0, c, 0)),
                   pl.BlockSpec((SSM_KT, CHUNK, TILE), lambda c: (0, c, 0)),
                   pl.BlockSpec((1, SSM_HEADS, SSM_P, SSM_N), lambda c: (c, 0, 0, 0))],
        out_shape=[SDS((SSM_KT, t_len, TILE), MXU_DTYPE), SDS((SSM_KT, t_len, TILE), f32),
                   SDS((nc, SSM_HEADS, SSM_P, SSM_N), f32)],
        scratch_shapes=[pltpu.VMEM((SSM_HEADS, SSM_P, SSM_N), f32), pltpu.VMEM((8 + CHUNK, TILE), f32),
                        pltpu.VMEM((CHUNK, TILE), f32), pltpu.VMEM((CHUNK, TILE), f32)],
        compiler_params=_params(("arbitrary",)),
    )(u, u, cw, cb, dtb, alog, dsk, gn)


def _ssm_bwd(u, yp, hst, da3, cw, cb, dtb, alog, dsk, gn, name):
    t_len = u.shape[1]
    nc = t_len // CHUNK

    def body(u_ref, halo_ref, cw_ref, cb_ref, dtb_ref, alog_ref, dsk_ref, gn_ref, yp_ref, hst_ref, da3_ref,
             du_ref, dcw_ref, dcb_ref, ddtb_ref, dalog_ref, ddsk_ref, dgn_ref,
             dh_s, carry_s, win_s, dwin_s, xs_s, bc_s, dy_s, dxs_s, dbc_s):
        step = pl.program_id(0)
        cc = nc - 1 - step

        @pl.when(step == 0)
        def _():
            dh_s[...] = jnp.zeros_like(dh_s)
            carry_s[...] = jnp.zeros_like(carry_s)
            dcw_ref[...] = jnp.zeros_like(dcw_ref)
            dcb_ref[...] = jnp.zeros_like(dcb_ref)
            ddtb_ref[...] = jnp.zeros_like(ddtb_ref)
            dalog_ref[...] = jnp.zeros_like(dalog_ref)
            ddsk_ref[...] = jnp.zeros_like(ddsk_ref)
            dgn_ref[...] = jnp.zeros_like(dgn_ref)
        dtr, dt, a_neg, acs, acs_t = _dt_path(u_ref[SSM_NT - 1, :, 0:SSM_HEADS], dtb_ref[...], alog_ref[...])
        dsk = dsk_ref[...]

        def sums(g, carry):
            s1, s2 = carry
            y2 = yp_ref[g] * _silu(u_ref[g])
            g3 = da3_ref[g] * gn_ref[g]
            return (s1 + jnp.sum(y2 * y2, axis=1, keepdims=True), s2 + jnp.sum(g3 * y2, axis=1, keepdims=True))

        zcol = jnp.zeros((CHUNK, 1), f32)
        s1, s2 = lax.fori_loop(0, SSM_GROUPS, sums, (zcol, zcol))
        rinv = lax.rsqrt(s1 * (1.0 / SSM_INNER) + EPS)
        m2 = s2 * rinv * rinv * rinv * (1.0 / SSM_INNER)

        def windows(j):
            win_s[0:8, :] = jnp.where(cc > 0, halo_ref[j], 0.0)
            win_s[8:8 + CHUNK, :] = u_ref[j]
            return [win_s[5 + k:5 + k + CHUNK, :] for k in range(4)]

        def taps(p):
            return [cw_ref[p, k:k + 1, :] for k in range(4)]

        def conv_bwd(j, p, vjp_fn, dout):
            dx0, dx1, dx2, dx3, dw0, dw1, dw2, dw3, db = vjp_fn(dout)
            dwin_s[0:CHUNK, :] = jnp.zeros((CHUNK, TILE), f32)
            dwin_s[CHUNK:CHUNK + 8, :] = carry_s[j - 8]
            for k, dxk in enumerate((dx0, dx1, dx2, dx3)):
                dwin_s[5 + k:5 + k + CHUNK, :] += dxk
            du_ref[j] = dwin_s[8:8 + CHUNK, :]
            carry_s[j - 8] = dwin_s[0:8, :]
            for k, dwk in enumerate((dw0, dw1, dw2, dw3)):
                dcw_ref[p, k:k + 1, :] += dwk
            dcb_ref[p] += db

        def group(g, carry):
            ddt, dacs, dacs_t, ddsk = carry
            z = u_ref[g]
            sg = _sigmoid(z)
            sz = z * sg
            y = yp_ref[g]
            y2 = y * sz
            da3 = da3_ref[g]
            dgn_ref[g] += jnp.sum(da3 * y2 * rinv, axis=0, keepdims=True)
            dy2 = rinv * (da3 * gn_ref[g]) - y2 * m2
            dy_s[...] = dy2 * sz
            du_ref[g] = dy2 * y * (sg * (1.0 + z * (1.0 - sg)))
            xs_v, vjp_xs = jax.vjp(_conv_silu, *windows(8 + g), *taps(g), cb_ref[g])
            xs_s[...] = xs_v
            bc_v, vjp_bc = jax.vjp(_conv_silu, *windows(16 + g), *taps(8 + g), cb_ref[8 + g])
            bc_s[...] = bc_v
            bm, cm = bc_s[:, 0:SSM_N], bc_s[:, SSM_N:2 * SSM_N]
            hs, ohl, ohs, xs, dtc, ac, ar, dh, hp, dys, dhn = [], [], [], [], [], [], [], [], [], [], []
            for r in range(4):
                h, oh_l, oh_s, dtc_r, ac_r, ar_r, dh_r = _head_cols(g, r, dt, acs, acs_t, dsk)
                hs.append(h); ohl.append(oh_l); ohs.append(oh_s)
                dtc.append(dtc_r); ac.append(ac_r); ar.append(ar_r); dh.append(dh_r)
                xs.append(xs_s[:, SSM_P * r:SSM_P * (r + 1)])
                hp.append(hst_ref[0, h])
                dys.append(dy_s[:, SSM_P * r:SSM_P * (r + 1)])
                dhn.append(dh_s[h])
            _, vjp_g = jax.vjp(_ssd_group, xs, bm, cm, dtc, ac, ar, dh, hp)
            dxs, dbm, dcm, ddtc, dac, dar, ddh, dhp = vjp_g((tuple(dys), tuple(dhn)))
            for r in range(4):
                dxs_s[:, SSM_P * r:SSM_P * (r + 1)] = dxs[r]
                dh_s[hs[r]] = dhp[r]
                ddt = ddt + ddtc[r] * ohl[r]
                dacs = dacs + dac[r] * ohl[r]
                dacs_t = dacs_t + ohs[r] * dar[r]
                ddsk = ddsk + ddh[r] * ohl[r]
            dbc_s[:, 0:SSM_N] = dbm
            dbc_s[:, SSM_N:2 * SSM_N] = dcm
            conv_bwd(8 + g, g, vjp_xs, dxs_s[...])
            conv_bwd(16 + g, 8 + g, vjp_bc, dbc_s[...])
            return ddt, dacs, dacs_t, ddsk

        init = (jnp.zeros((CHUNK, SSM_HEADS), f32), jnp.zeros((CHUNK, SSM_HEADS), f32),
                jnp.zeros((SSM_HEADS, CHUNK), f32), jnp.zeros((1, SSM_HEADS), f32))
        ddt, dacs, dacs_t, ddsk = lax.fori_loop(0, SSM_GROUPS, group, init)
        upper = _tri(False)
        da = _dot_hi(upper, dacs, ((1,), (0,))) + _dot_hi(upper, dacs_t, ((1,), (1,)))
        ddt = ddt + da * a_neg
        dalog_ref[...] += jnp.sum(da * dt, axis=0, keepdims=True) * a_neg
        ddtr = ddt * _sigmoid(dtr)
        ddtb_ref[...] += jnp.sum(ddtr, axis=0, keepdims=True)
        ddsk_ref[...] += ddsk
        du_ref[SSM_NT - 1] = jnp.zeros((CHUNK, TILE), f32)
        du_ref[SSM_NT - 1, :, 0:SSM_HEADS] = ddtr

    def rc(c):
        return nc - 1 - c

    small = [pl.BlockSpec((16, 4, TILE), lambda c: (0, 0, 0)),
             pl.BlockSpec((16, 1, TILE), lambda c: (0, 0, 0)),
             pl.BlockSpec((1, SSM_HEADS), lambda c: (0, 0)),
             pl.BlockSpec((1, SSM_HEADS), lambda c: (0, 0)),
             pl.BlockSpec((1, SSM_HEADS), lambda c: (0, 0)),
             pl.BlockSpec((SSM_KT, 1, TILE), lambda c: (0, 0, 0))]
    return _pc(
        body, name, grid=(nc,),
        in_specs=_ssm_in_specs(nc, True) + [
            pl.BlockSpec((SSM_KT, CHUNK, TILE), lambda c: (0, rc(c), 0)),
            pl.BlockSpec((1, SSM_HEADS, SSM_P, SSM_N), lambda c: (rc(c), 0, 0, 0)),
            pl.BlockSpec((SSM_KT, CHUNK, TILE), lambda c: (0, rc(c), 0))],
        out_specs=[pl.BlockSpec((SSM_NT, CHUNK, TILE), lambda c: (0, rc(c), 0))] + small,
        out_shape=[SDS((SSM_NT, t_len, TILE), f32), SDS((16, 4, TILE), f32), SDS((16, 1, TILE), f32),
                   SDS((1, SSM_HEADS), f32), SDS((1, SSM_HEADS), f32), SDS((1, SSM_HEADS), f32),
                   SDS((SSM_KT, 1, TILE), f32)],
        scratch_shapes=[pltpu.VMEM((SSM_HEADS, SSM_P, SSM_N), f32), pltpu.VMEM((16, 8, TILE), f32),
                        pltpu.VMEM((8 + CHUNK, TILE), f32), pltpu.VMEM((8 + CHUNK, TILE), f32)]
                       + [pltpu.VMEM((CHUNK, TILE), f32)] * 5,
        compiler_params=_params(("arbitrary",)),
    )(u, u, cw, cb, dtb, alog, dsk, gn, yp, hst, da3)


CONV_ROWS = 1024
CONV_SUB = 32


def _conv_specs(nb, rows, rev):
    def ridx(i):
        return (nb - 1 - i) if rev else i
    return [
        pl.BlockSpec((1, rows, TILE), lambda p, i: (8 + p, ridx(i), 0)),
        pl.BlockSpec((1, 8, TILE), lambda p, i: (8 + p, jnp.maximum(ridx(i) * (rows // 8) - 1, 0), 0)),
        pl.BlockSpec((1, 4, TILE), lambda p, i: (p, 0, 0)),
        pl.BlockSpec((1, 1, TILE), lambda p, i: (p, 0, 0)),
    ]


def _conv_fwd(u, cw, cb, name):
    t_len = u.shape[1]
    rows = min(CONV_ROWS, t_len)
    nb = t_len // rows

    def body(u_ref, halo_ref, cw_ref, cb_ref, o_ref, win_s):
        i = pl.program_id(1)
        win_s[0:8, :] = jnp.where(i > 0, halo_ref[0], 0.0)
        win_s[8:8 + rows, :] = u_ref[0]
        w = [cw_ref[0, k:k + 1, :] for k in range(4)]
        b = cb_ref[0]
        for s in range(rows // CONV_SUB):
            o = CONV_SUB * s
            acc = b
            for k in range(4):
                acc = acc + w[k] * win_s[5 + k + o:5 + k + o + CONV_SUB, :]
            o_ref[0, o:o + CONV_SUB, :] = _silu(acc)

    return _pc(
        body, name, grid=(16, nb),
        in_specs=_conv_specs(nb, rows, False),
        out_specs=pl.BlockSpec((1, rows, TILE), lambda p, i: (p, i, 0)),
        out_shape=SDS((16, t_len, TILE), f32),
        scratch_shapes=[pltpu.VMEM((8 + rows, TILE), f32)],
        compiler_params=_params(("arbitrary", "arbitrary")),
    )(u, u, cw, cb)


def _conv_bwd(u, du, cw, cb, name):
    t_len = u.shape[1]
    rows = min(CONV_ROWS, t_len)
    nb = t_len // rows

    def body(u_ref, halo_ref, cw_ref, cb_ref, d_ref, o_ref, dcw_ref, dcb_ref, carry_s, win_s, dp_s):
        i = pl.program_id(1)
        ri = nb - 1 - i

        @pl.when(i == 0)
        def _():
            carry_s[...] = jnp.zeros_like(carry_s)
            dcw_ref[...] = jnp.zeros_like(dcw_ref)
            dcb_ref[...] = jnp.zeros_like(dcb_ref)
        win_s[0:8, :] = jnp.where(ri > 0, halo_ref[0], 0.0)
        win_s[8:8 + rows, :] = u_ref[0]
        w = [cw_ref[0, k:k + 1, :] for k in range(4)]
        b = cb_ref[0]
        dw = [jnp.zeros((1, TILE), f32)] * 4
        db = jnp.zeros((1, TILE), f32)
        for s in range(rows // CONV_SUB):
            o = CONV_SUB * s
            xk = [win_s[5 + k + o:5 + k + o + CONV_SUB, :] for k in range(4)]
            pre = b
            for k in range(4):
                pre = pre + w[k] * xk[k]
            sg = _sigmoid(pre)
            dpre = d_ref[0, o:o + CONV_SUB, :] * (sg * (1.0 + pre * (1.0 - sg)))
            dp_s[o:o + CONV_SUB, :] = dpre
            dw = [dw[k] + jnp.sum(dpre * xk[k], axis=0, keepdims=True) for k in range(4)]
            db = db + jnp.sum(dpre, axis=0, keepdims=True)
        dp_s[rows:rows + 8, :] = carry_s[...]
        for s in range(rows // CONV_SUB):
            o = CONV_SUB * s
            acc = w[0] * dp_s[3 + o:3 + o + CONV_SUB, :]
            for k in range(1, 4):
                acc = acc + w[k] * dp_s[3 - k + o:3 - k + o + CONV_SUB, :]
            o_ref[0, o:o + CONV_SUB, :] = acc
        carry_s[...] = dp_s[0:8, :]
        for k in range(4):
            dcw_ref[0, k:k + 1, :] += dw[k]
        dcb_ref[0] += db

    return _pc(
        body, name, grid=(16, nb),
        in_specs=_conv_specs(nb, rows, True) + [pl.BlockSpec((1, rows, TILE), lambda p, i: (8 + p, nb - 1 - i, 0))],
        out_specs=[pl.BlockSpec((1, rows, TILE), lambda p, i: (8 + p, nb - 1 - i, 0)),
                   pl.BlockSpec((1, 4, TILE), lambda p, i: (p, 0, 0)),
                   pl.BlockSpec((1, 1, TILE), lambda p, i: (p, 0, 0))],
        out_shape=[SDS(du.shape, f32), SDS((16, 4, TILE), f32), SDS((16, 1, TILE), f32)],
        input_output_aliases={4: 0},
        scratch_shapes=[pltpu.VMEM((8, TILE), f32), pltpu.VMEM((8 + rows, TILE), f32),
                        pltpu.VMEM((rows + 8, TILE), f32)],
        compiler_params=_params(("arbitrary", "arbitrary")),
    )(u, u, cw, cb, du)


def _collapse_matrix(g):
    r = lax.broadcasted_iota(jnp.int32, (SSM_HEADS, TILE), 0)
    c = lax.broadcasted_iota(jnp.int32, (SSM_HEADS, TILE), 1)
    return ((c // SSM_P) + 4 * g == r).astype(jnp.bfloat16)


def _ssd_prelude(dt_raw, dtb, alog, dsk, colx_s, scx_s):
    dtr, dt, a_neg, acs, acs_t = _dt_path(dt_raw, dtb, alog)
    a_end = acs[CHUNK - 1:CHUNK, :]
    lane = lax.broadcasted_iota(jnp.int32, (1, 2 * SSM_P), 1)
    lane4 = lax.broadcasted_iota(jnp.int32, (1, TILE), 1)
    sub8 = lax.broadcasted_iota(jnp.int32, (8, 1), 0)

    def row4(v, g):
        e = [v[:, 4 * g + r:4 * g + r + 1] for r in range(4)]
        return jnp.where(lane4 < 64, e[0], jnp.where(lane4 < 128, e[1], jnp.where(lane4 < 192, e[2], e[3])))

    for g in range(SSM_GROUPS):
        for k, arr in enumerate((dt, acs)):
            for half in range(2):
                h0 = 4 * g + 2 * half
                colx_s[k, g, :, 128 * half:128 * (half + 1)] = jnp.where(
                    lane < SSM_P, arr[:, h0:h0 + 1], arr[:, h0 + 1:h0 + 2])
        scx_s[g] = jnp.where(sub8 == 0, row4(dsk, g), jnp.where(sub8 == 1, row4(a_end, g), 0.0))
    return dtr, dt, a_neg, acs_t


def _ssm_core_specs(nc, rev):
    def cidx(c):
        return (nc - 1 - c) if rev else c
    return [
        pl.BlockSpec((SSM_KT, CHUNK, TILE), lambda c: (0, cidx(c), 0)),
        pl.BlockSpec((1, CHUNK, TILE), lambda c: (SSM_NT - 1, cidx(c), 0)),
        pl.BlockSpec((16, CHUNK, TILE), lambda c: (0, cidx(c), 0)),
        pl.BlockSpec((1, SSM_HEADS), lambda c: (0, 0)),
        pl.BlockSpec((1, SSM_HEADS), lambda c: (0, 0)),
        pl.BlockSpec((1, SSM_HEADS), lambda c: (0, 0)),
        pl.BlockSpec((SSM_KT, 1, TILE), lambda c: (0, 0, 0)),
    ]


def _stack_cols_rows(acx, rows):
    ac = jnp.concatenate([acx[:, SSM_P * r:SSM_P * r + 1] for r in range(4)], axis=0)
    ar = jnp.concatenate([jnp.broadcast_to(rows[r:r + 1, :], (CHUNK, CHUNK)) for r in range(4)], axis=0)
    return ac, ar


def _ssm2_fwd(u, xbc, dtb, alog, dsk, gn, name):
    t_len = u.shape[1]
    nc = t_len // CHUNK

    def body(z_ref, dt_ref, x_ref, dtb_ref, alog_ref, dsk_ref, gn_ref, a3_ref, yp_ref, hst_ref,
             h_s, colx_s, scx_s, xt_s, yd_s):
        c = pl.program_id(0)

        @pl.when(c == 0)
        def _():
            h_s[...] = jnp.zeros_like(h_s)
        _, _, _, acs_t = _ssd_prelude(dt_ref[0, :, 0:SSM_HEADS], dtb_ref[...], alog_ref[...], dsk_ref[...],
                                      colx_s, scx_s)
        causal = _tri4()

        def group(g, s1):
            xs = x_ref[g]
            bm, cm = x_ref[8 + g, :, 0:SSM_N], x_ref[8 + g, :, SSM_N:2 * SSM_N]
            cb = _dot_nt(cm, bm)
            sc = scx_s[g]
            a_end = sc[1:2, :]
            rows = pltpu.roll(acs_t, (SSM_HEADS - 4 * g) % SSM_HEADS, 0)
            hp = h_s[g]
            xt = xs * colx_s[0, g]
            xt_s[...] = xt
            acx = colx_s[1, g]
            ac_st, ar_st = _stack_cols_rows(acx, rows)
            m = jnp.concatenate([cb] * 4, axis=0) * jnp.exp(jnp.where(causal, ac_st - ar_st, -jnp.inf))
            for r in range(4):
                hd = slice(SSM_P * r, SSM_P * (r + 1))
                yd_s[:, hd] = _dot(m[CHUNK * r:CHUNK * (r + 1), :], xt_s[:, hd])
            yp_ref[g] = yd_s[...] + _dot(cm, hp) * jnp.exp(acx) + sc[0:1, :] * xs
            hst_ref[0, g] = hp
            h_s[g] = hp * jnp.exp(a_end) + _dot_tn(bm, xt * jnp.exp(a_end - acx))
            y2 = yp_ref[g] * _silu(z_ref[g])
            return s1 + jnp.sum(y2 * y2, axis=1, keepdims=True)

        s1 = lax.fori_loop(0, SSM_GROUPS, group, jnp.zeros((CHUNK, 1), f32))
        rinv = lax.rsqrt(s1 * (1.0 / SSM_INNER) + EPS)

        def gate(g, carry):
            y2 = yp_ref[g] * _silu(z_ref[g])
            a3_ref[g] = (y2 * rinv * gn_ref[g]).astype(a3_ref.dtype)
            return carry

        lax.fori_loop(0, SSM_GROUPS, gate, 0)

    return _pc(
        body, name, grid=(nc,),
        in_specs=_ssm_core_specs(nc, False),
        out_specs=[pl.BlockSpec((SSM_KT, CHUNK, TILE), lambda c: (0, c, 0)),
                   pl.BlockSpec((SSM_KT, CHUNK, TILE), lambda c: (0, c, 0)),
                   pl.BlockSpec((1, SSM_GROUPS, SSM_N, TILE), lambda c: (c, 0, 0, 0))],
        out_shape=[SDS((SSM_KT, t_len, TILE), MXU_DTYPE), SDS((SSM_KT, t_len, TILE), f32),
                   SDS((nc, SSM_GROUPS, SSM_N, TILE), f32)],
        scratch_shapes=[pltpu.VMEM((SSM_GROUPS, SSM_N, TILE), f32), pltpu.VMEM((2, SSM_GROUPS, CHUNK, TILE), f32),
                        pltpu.VMEM((SSM_GROUPS, 8, TILE), f32), pltpu.VMEM((CHUNK, TILE), f32),
                        pltpu.VMEM((CHUNK, TILE), f32)],
        compiler_params=_params(("arbitrary",)),
    )(u, u, xbc, dtb, alog, dsk, gn)


def _ssm2_bwd(u, xbc, yp, hst, da3, dtb, alog, dsk, gn, name):
    t_len = u.shape[1]
    nc = t_len // CHUNK

    def body(z_ref, dt_ref, x_ref, dtb_ref, alog_ref, dsk_ref, gn_ref, yp_ref, hst_ref, da3_ref,
             du_ref, ddtb_ref, dalog_ref, ddsk_ref, dgn_ref,
             dh_s, colx_s, scx_s, xt_s, dy_s, dxt_s, ddtx_s, dacx_s, ddx_s, drow_s, dm_s, dmt_s):
        step = pl.program_id(0)

        @pl.when(step == 0)
        def _():
            dh_s[...] = jnp.zeros_like(dh_s)
            ddtb_ref[...] = jnp.zeros_like(ddtb_ref)
            dalog_ref[...] = jnp.zeros_like(dalog_ref)
            ddsk_ref[...] = jnp.zeros_like(ddsk_ref)
            dgn_ref[...] = jnp.zeros_like(dgn_ref)
        dtr, dt, a_neg, acs_t = _ssd_prelude(dt_ref[0, :, 0:SSM_HEADS], dtb_ref[...], alog_ref[...], dsk_ref[...],
                                             colx_s, scx_s)
        causal = _tri4()
        causal_t = (lax.broadcasted_iota(jnp.int32, (ATT_ROWS, CHUNK), 1)
                    >= lax.broadcasted_iota(jnp.int32, (ATT_ROWS, CHUNK), 0) % CHUNK)
        last = (lax.broadcasted_iota(jnp.int32, (1, CHUNK), 1) == CHUNK - 1).astype(f32)
        lane = lax.broadcasted_iota(jnp.int32, (1, TILE), 1)
        sub32 = lax.broadcasted_iota(jnp.int32, (SSM_HEADS, 1), 0)
        drow_s[...] = jnp.zeros_like(drow_s)

        def sums(g, carry):
            s1, s2 = carry
            y2 = yp_ref[g] * _silu(z_ref[g])
            g3 = da3_ref[g] * gn_ref[g]
            return (s1 + jnp.sum(y2 * y2, axis=1, keepdims=True), s2 + jnp.sum(g3 * y2, axis=1, keepdims=True))

        zcol = jnp.zeros((CHUNK, 1), f32)
        s1, s2 = lax.fori_loop(0, SSM_GROUPS, sums, (zcol, zcol))
        rinv = lax.rsqrt(s1 * (1.0 / SSM_INNER) + EPS)
        m2 = s2 * rinv * rinv * rinv * (1.0 / SSM_INNER)

        def group(g, carry):
            z = z_ref[g]
            sg = _sigmoid(z)
            sz = z * sg
            y = yp_ref[g]
            y2 = y * sz
            da3 = da3_ref[g]
            dgn_ref[g] += jnp.sum(da3 * y2 * rinv, axis=0, keepdims=True)
            dy2 = rinv * (da3 * gn_ref[g]) - y2 * m2
            dy = dy2 * sz
            dy_s[...] = dy
            du_ref[g] = dy2 * y * (sg * (1.0 + z * (1.0 - sg)))
            xs = x_ref[g]
            bm, cm = x_ref[8 + g, :, 0:SSM_N], x_ref[8 + g, :, SSM_N:2 * SSM_N]
            cb = _dot_nt(cm, bm)
            cbt = _dot_nt(bm, cm)
            dtx, acx = colx_s[0, g], colx_s[1, g]
            sc = scx_s[g]
            a_end = sc[1:2, :]
            ex = jnp.exp(acx)
            wdx = jnp.exp(a_end - acx)
            eend = jnp.exp(a_end)
            rows = pltpu.roll(acs_t, (SSM_HEADS - 4 * g) % SSM_HEADS, 0)
            hp = hst_ref[0, g]
            dhn = dh_s[g]
            xt = xs * dtx
            xt_s[...] = xt
            ch = _dot(cm, hp)
            gy = dy * ex
            dcm = _dot_nt(gy, hp)
            dh_s[g] = _dot_tn(cm, gy) + dhn * eend
            q = _dot(bm, dhn)
            dbm = _dot_nt(xt * wdx, dhn)
            qx = q * xt * wdx
            v_end = jnp.sum(dhn * hp, axis=0, keepdims=True) * eend + jnp.sum(qx, axis=0, keepdims=True)
            ac_st, ar_st = _stack_cols_rows(acx, rows)
            lam = jnp.exp(jnp.where(causal, ac_st - ar_st, -jnp.inf))
            lam_t = jnp.exp(jnp.where(causal_t, ar_st - ac_st, -jnp.inf))
            m = jnp.concatenate([cb] * 4, axis=0) * lam
            m_t = jnp.concatenate([cbt] * 4, axis=0) * lam_t
            for r in range(4):
                hd = slice(SSM_P * r, SSM_P * (r + 1))
                rs = slice(CHUNK * r, CHUNK * (r + 1))
                dm_s[rs, :] = _dot_nt(dy_s[:, hd], xt_s[:, hd])
                dmt_s[rs, :] = _dot_nt(xt_s[:, hd], dy_s[:, hd])
                dxt_s[:, hd] = _dot(m_t[rs, :], dy_s[:, hd])
            dm = dm_s[...]
            dl = dm * lam
            dseg = dm * m
            dseg_t = dmt_s[...] * m_t
            dcb = dl[0:CHUNK] + dl[CHUNK:2 * CHUNK] + dl[2 * CHUNK:3 * CHUNK] + dl[3 * CHUNK:4 * CHUNK]
            drows = jnp.zeros((SSM_HEADS, CHUNK), f32)
            for r in range(4):
                rs = slice(CHUNK * r, CHUNK * (r + 1))
                in_head = (lane >= SSM_P * r) & (lane < SSM_P * (r + 1))
                d_ac = jnp.sum(dseg_t[rs, :], axis=0, keepdims=True)
                d_ar = jnp.sum(dseg[rs, :], axis=0, keepdims=True)
                d_aend = jnp.sum(jnp.where(in_head, v_end, 0.0), axis=1, keepdims=True)
                drows = drows + jnp.where(sub32 == r, d_ac - d_ar + last * d_aend, 0.0)
            dxt = dxt_s[...] + q * wdx
            du_ref[8 + g] = sc[0:1, :] * dy + dxt * dtx
            ddtx_s[g] = dxt * xs
            dacx_s[g] = dy * ch * ex - qx
            ddx_s[g] = jnp.broadcast_to(jnp.sum(dy * xs, axis=0, keepdims=True), (8, TILE))
            du_ref[16 + g, :, 0:SSM_N] = dbm + _dot_tn(dcb, cm)
            du_ref[16 + g, :, SSM_N:2 * SSM_N] = dcm + _dot(dcb, bm)
            drow_s[...] += pltpu.roll(drows, (4 * g) % SSM_HEADS, 0)
            return carry

        lax.fori_loop(0, SSM_GROUPS, group, 0)
        ddt = jnp.zeros((CHUNK, SSM_HEADS), f32)
        dacs = jnp.zeros((CHUNK, SSM_HEADS), f32)
        ddsk = jnp.zeros((8, SSM_HEADS), f32)
        for g in range(SSM_GROUPS):
            col_g = _collapse_matrix(g)
            ddt = ddt + _dot_exact(ddtx_s[g], col_g, ((1,), (1,)), True, 2)
            dacs = dacs + _dot_exact(dacx_s[g], col_g, ((1,), (1,)), True, 2)
            ddsk = ddsk + _dot_exact(ddx_s[g], col_g, ((1,), (1,)), True, 2)
        upper = _tri(False)
        da = _dot_exact(upper, dacs, ((1,), (0,)), False) + _dot_exact(upper, drow_s[...], ((1,), (1,)), False)
        ddt = ddt + da * a_neg
        dalog_ref[...] += jnp.sum(da * dt, axis=0, keepdims=True) * a_neg
        ddtr = ddt * _sigmoid(dtr)
        ddtb_ref[...] += jnp.sum(ddtr, axis=0, keepdims=True)
        ddsk_ref[...] += ddsk[0:1, :]
        du_ref[SSM_NT - 1] = jnp.zeros((CHUNK, TILE), f32)
        du_ref[SSM_NT - 1, :, 0:SSM_HEADS] = ddtr

    def rc(c):
        return nc - 1 - c

    vec = pl.BlockSpec((1, SSM_HEADS), lambda c: (0, 0))
    return _pc(
        body, name, grid=(nc,),
        in_specs=_ssm_core_specs(nc, True) + [
            pl.BlockSpec((SSM_KT, CHUNK, TILE), lambda c: (0, rc(c), 0)),
            pl.BlockSpec((1, SSM_GROUPS, SSM_N, TILE), lambda c: (rc(c), 0, 0, 0)),
            pl.BlockSpec((SSM_KT, CHUNK, TILE), lambda c: (0, rc(c), 0))],
        out_specs=[pl.BlockSpec((SSM_NT, CHUNK, TILE), lambda c: (0, rc(c), 0)), vec, vec, vec,
                   pl.BlockSpec((SSM_KT, 1, TILE), lambda c: (0, 0, 0))],
        out_shape=[SDS((SSM_NT, t_len, TILE), f32), SDS((1, SSM_HEADS), f32), SDS((1, SSM_HEADS), f32),
                   SDS((1, SSM_HEADS), f32), SDS((SSM_KT, 1, TILE), f32)],
        scratch_shapes=[pltpu.VMEM((SSM_GROUPS, SSM_N, TILE), f32), pltpu.VMEM((2, SSM_GROUPS, CHUNK, TILE), f32),
                        pltpu.VMEM((SSM_GROUPS, 8, TILE), f32), pltpu.VMEM((CHUNK, TILE), f32),
                        pltpu.VMEM((CHUNK, TILE), f32), pltpu.VMEM((CHUNK, TILE), f32),
                        pltpu.VMEM((SSM_GROUPS, CHUNK, TILE), f32), pltpu.VMEM((SSM_GROUPS, CHUNK, TILE), f32),
                        pltpu.VMEM((SSM_GROUPS, 8, TILE), f32), pltpu.VMEM((SSM_HEADS, CHUNK), f32),
                        pltpu.VMEM((4 * CHUNK, CHUNK), f32), pltpu.VMEM((4 * CHUNK, CHUNK), f32)],
        compiler_params=_params(("arbitrary",)),
    )(u, u, xbc, dtb, alog, dsk, gn, yp, hst, da3)


def _swap16(t):
    lane = lax.broadcasted_iota(jnp.int32, t.shape, 1) % 64
    return jnp.where(lane < 8, pltpu.roll(t, TILE - 8, 1), jnp.where(lane < 16, pltpu.roll(t, 8, 1), 0.0))


def _rope(t, cos_t, sin_t):
    return t * cos_t + _swap16(t) * sin_t


def _rope_bwd(g, cos_t, sin_t):
    return g * cos_t + _swap16(g * sin_t)


def _att_head(qh, kp, kc, vp, vc, sink, mask_p, mask_c):
    sp = jnp.where(mask_p, _dot_nt(qh, kp) * 0.125, -jnp.inf)
    sc = jnp.where(mask_c, _dot_nt(qh, kc) * 0.125, -jnp.inf)
    m = jnp.maximum(jnp.maximum(jnp.max(sp, axis=-1, keepdims=True), jnp.max(sc, axis=-1, keepdims=True)), sink)
    m = lax.stop_gradient(m)
    pp = jnp.exp(sp - m)
    pc = jnp.exp(sc - m)
    den = jnp.sum(pp, axis=-1, keepdims=True) + jnp.sum(pc, axis=-1, keepdims=True) + jnp.exp(sink - m)
    inv = 1.0 / den
    return _dot(pp * inv, vp) + _dot(pc * inv, vc)


def _att_masks(has_prev):
    row = lax.broadcasted_iota(jnp.int32, (CHUNK, CHUNK), 0)
    col = lax.broadcasted_iota(jnp.int32, (CHUNK, CHUNK), 1)
    return (col > row) & has_prev, col <= row


def _att_in_specs(nb, rev):
    def bidx(n):
        return (nb - 1 - n) if rev else n
    return [
        pl.BlockSpec((ATT_NT, CHUNK, TILE), lambda n: (0, bidx(n), 0)),
        pl.BlockSpec((2, CHUNK, TILE), lambda n: (2, jnp.maximum(bidx(n) - 1, 0), 0)),
        pl.BlockSpec((CHUNK, TILE), lambda n: (bidx(n), 0)),
        pl.BlockSpec((CHUNK, TILE), lambda n: (bidx(n), 0)),
        pl.BlockSpec((CHUNK, TILE), lambda n: (jnp.maximum(bidx(n) - 1, 0), 0)),
        pl.BlockSpec((CHUNK, TILE), lambda n: (jnp.maximum(bidx(n) - 1, 0), 0)),
        pl.BlockSpec((1, 16), lambda n: (0, 0)),
    ]


def _att_fwd(u, cos_t, sin_t, sinks, name):
    t_len = u.shape[1]
    nb = t_len // CHUNK

    def body(u_ref, prev_ref, cc_ref, sc_ref, cp_ref, sp_ref, sink_ref, a_ref,
             q_s, kp_s, kc_s, vp_s, vc_s, o_s):
        n = pl.program_id(0)
        mask_p, mask_c = _att_masks(n > 0)
        cos_c, sin_c = cc_ref[...], sc_ref[...]
        kc_s[...] = _rope(u_ref[4], cos_c, sin_c)
        kp_s[...] = _rope(prev_ref[0], cp_ref[...], sp_ref[...])
        vc_s[...] = u_ref[5]
        vp_s[...] = prev_ref[1]
        sinks = sink_ref[...]
        for g in range(4):
            q_s[...] = _rope(u_ref[g], cos_c, sin_c)
            kv = slice(64 * g, 64 * (g + 1))
            for r in range(4):
                hd = slice(64 * r, 64 * (r + 1))
                h = 4 * g + r
                o_s[:, hd] = _att_head(q_s[:, hd], kp_s[:, kv], kc_s[:, kv], vp_s[:, kv], vc_s[:, kv],
                                       sinks[:, h:h + 1], mask_p, mask_c)
            a_ref[g] = (o_s[...] * _silu(u_ref[6 + g])).astype(a_ref.dtype)

    return _pc(
        body, name, grid=(nb,),
        in_specs=_att_in_specs(nb, False),
        out_specs=pl.BlockSpec((ATT_KT, CHUNK, TILE), lambda n: (0, n, 0)),
        out_shape=SDS((ATT_KT, t_len, TILE), MXU_DTYPE),
        scratch_shapes=[pltpu.VMEM((CHUNK, TILE), f32)] * 6,
        compiler_params=_params(("arbitrary",)),
    )(u, u, cos_t, sin_t, cos_t, sin_t, sinks)


def _att_bwd(u, cos_t, sin_t, sinks, da, name):
    t_len = u.shape[1]
    nb = t_len // CHUNK

    def body(u_ref, prev_ref, cc_ref, sc_ref, cp_ref, sp_ref, sink_ref, da_ref, du_ref, dsink_ref,
             ck_s, cv_s, q_s, kp_s, kc_s, vp_s, vc_s, o_s, do_s, dq_s, dkp_s, dkc_s, dvp_s, dvc_s):
        step = pl.program_id(0)
        nn = nb - 1 - step

        @pl.when(step == 0)
        def _():
            ck_s[...] = jnp.zeros_like(ck_s)
            cv_s[...] = jnp.zeros_like(cv_s)
            dsink_ref[...] = jnp.zeros_like(dsink_ref)
        mask_p, mask_c = _att_masks(nn > 0)
        cos_c, sin_c = cc_ref[...], sc_ref[...]
        cos_p, sin_p = cp_ref[...], sp_ref[...]
        kc_s[...] = _rope(u_ref[4], cos_c, sin_c)
        kp_s[...] = _rope(prev_ref[0], cos_p, sin_p)
        vc_s[...] = u_ref[5]
        vp_s[...] = prev_ref[1]
        sinks = sink_ref[...]
        lane16 = lax.broadcasted_iota(jnp.int32, (1, 16), 1)
        dsink = jnp.zeros((1, 16), f32)
        att = functools.partial(_att_head, mask_p=mask_p, mask_c=mask_c)
        for g in range(4):
            q_s[...] = _rope(u_ref[g], cos_c, sin_c)
            gate = u_ref[6 + g]
            sg = _sigmoid(gate)
            dav = da_ref[g]
            do_s[...] = dav * (gate * sg)
            kv = slice(64 * g, 64 * (g + 1))
            dkp = jnp.zeros((CHUNK, 64), f32)
            dkc = jnp.zeros((CHUNK, 64), f32)
            dvp = jnp.zeros((CHUNK, 64), f32)
            dvc = jnp.zeros((CHUNK, 64), f32)
            for r in range(4):
                hd = slice(64 * r, 64 * (r + 1))
                h = 4 * g + r
                o_h, vjp_fn = jax.vjp(att, q_s[:, hd], kp_s[:, kv], kc_s[:, kv], vp_s[:, kv], vc_s[:, kv],
                                      sinks[:, h:h + 1])
                dq_h, dkp_h, dkc_h, dvp_h, dvc_h, ds_h = vjp_fn(do_s[:, hd])
                o_s[:, hd] = o_h
                dq_s[:, hd] = dq_h
                dkp, dkc, dvp, dvc = dkp + dkp_h, dkc + dkc_h, dvp + dvp_h, dvc + dvc_h
                dsink = dsink + ds_h * (lane16 == h).astype(f32)
            du_ref[6 + g] = dav * o_s[...] * (sg * (1.0 + gate * (1.0 - sg)))
            du_ref[g] = _rope_bwd(dq_s[...], cos_c, sin_c)
            dkp_s[:, kv] = dkp
            dkc_s[:, kv] = dkc
            dvp_s[:, kv] = dvp
            dvc_s[:, kv] = dvc
        du_ref[4] = _rope_bwd(dkc_s[...], cos_c, sin_c) + ck_s[...]
        du_ref[5] = dvc_s[...] + cv_s[...]
        ck_s[...] = _rope_bwd(dkp_s[...], cos_p, sin_p)
        cv_s[...] = dvp_s[...]
        dsink_ref[...] += dsink

    def rb(n):
        return nb - 1 - n

    return _pc(
        body, name, grid=(nb,),
        in_specs=_att_in_specs(nb, True) + [pl.BlockSpec((ATT_KT, CHUNK, TILE), lambda n: (0, rb(n), 0))],
        out_specs=[pl.BlockSpec((ATT_NT, CHUNK, TILE), lambda n: (0, rb(n), 0)),
                   pl.BlockSpec((1, 16), lambda n: (0, 0))],
        out_shape=[SDS((ATT_NT, t_len, TILE), f32), SDS((1, 16), f32)],
        scratch_shapes=[pltpu.VMEM((CHUNK, TILE), f32)] * 14,
        compiler_params=_params(("arbitrary",)),
    )(u, u, cos_t, sin_t, cos_t, sin_t, sinks, da)


ATT_SCALE = 0.125
ATT_ROWS = 4 * CHUNK


def _band_mask(has_prev):
    row = lax.broadcasted_iota(jnp.int32, (CHUNK, 2 * CHUNK), 0)
    col = lax.broadcasted_iota(jnp.int32, (CHUNK, 2 * CHUNK), 1)
    return ((col < CHUNK) & (col > row) & has_prev) | ((col >= CHUNK) & (col - CHUNK <= row))


def _tri4():
    row = lax.broadcasted_iota(jnp.int32, (ATT_ROWS, CHUNK), 0) % CHUNK
    col = lax.broadcasted_iota(jnp.int32, (ATT_ROWS, CHUNK), 1)
    return col <= row


def _stack_heads(ref):
    return jnp.concatenate([ref[:, 64 * r:64 * (r + 1)] for r in range(4)], axis=0)


def _sink_col(sinks, g):
    return [sinks[:, 4 * g + r:4 * g + r + 1] for r in range(4)]


def _softmax_rows(s_s, pn_s, pc_s, sink, tri, has_prev):
    sub = lax.broadcasted_iota(jnp.int32, (ATT_ROWS, 1), 0)
    sk = jnp.where(sub < CHUNK, sink[0], jnp.where(sub < 2 * CHUNK, sink[1], jnp.where(sub < 3 * CHUNK, sink[2], sink[3])))
    s = jnp.where(tri, s_s[:, CHUNK:2 * CHUNK], jnp.where(has_prev, s_s[:, 0:CHUNK], -jnp.inf)) * ATT_SCALE
    m = jnp.maximum(jnp.max(s, axis=-1, keepdims=True), sk)
    p = jnp.exp(s - m)
    e_sink = jnp.exp(sk - m)
    inv = 1.0 / (jnp.sum(p, axis=-1, keepdims=True) + e_sink)
    pn = p * inv
    pc_s[...] = pn
    pn_s[:, 0:CHUNK] = jnp.where(tri, 0.0, pn)
    pn_s[:, CHUNK:2 * CHUNK] = jnp.where(tri, pn, 0.0)
    return e_sink * inv


def _att2_fwd(u, cos_t, sin_t, sinks, name):
    t_len = u.shape[1]
    nb = t_len // CHUNK

    def body(u_ref, prev_ref, cc_ref, sc_ref, cp_ref, sp_ref, sink_ref, a_ref,
             q_s, kp_s, kc_s, vp_s, vc_s, o_s, s_s, pn_s, pc_s):
        n = pl.program_id(0)
        tri = _tri4()
        cos_c, sin_c = cc_ref[...], sc_ref[...]
        kc_s[...] = _rope(u_ref[4], cos_c, sin_c)
        kp_s[...] = _rope(prev_ref[0], cp_ref[...], sp_ref[...])
        vc_s[...] = u_ref[5]
        vp_s[...] = prev_ref[1]
        sinks = sink_ref[...]
        for g in range(4):
            q_g, o_g, s_g, pn_g = q_s.at[g], o_s.at[g], s_s.at[g], pn_s.at[g]
            q_g[...] = _rope(u_ref[g], cos_c, sin_c)
            kv = slice(64 * g, 64 * (g + 1))
            kb = jnp.concatenate([kp_s[:, kv], kc_s[:, kv]], axis=0)
            vb = jnp.concatenate([vp_s[:, kv], vc_s[:, kv]], axis=0)
            s_g[...] = _dot_nt(_stack_heads(q_g), kb)
            _softmax_rows(s_g, pn_g, pc_s.at[g], _sink_col(sinks, g), tri, n > 0)
            o = _dot(pn_g[...], vb)
            for r in range(4):
                o_g[:, 64 * r:64 * (r + 1)] = o[CHUNK * r:CHUNK * (r + 1), :]
            a_ref[g] = (o_g[...] * _silu(u_ref[6 + g])).astype(a_ref.dtype)

    return _pc(
        body, name, grid=(nb,),
        in_specs=_att_in_specs(nb, False),
        out_specs=pl.BlockSpec((ATT_KT, CHUNK, TILE), lambda n: (0, n, 0)),
        out_shape=SDS((ATT_KT, t_len, TILE), MXU_DTYPE),
        scratch_shapes=[pltpu.VMEM((4, CHUNK, TILE), f32)] + [pltpu.VMEM((CHUNK, TILE), f32)] * 4
                       + [pltpu.VMEM((4, CHUNK, TILE), f32)] + [pltpu.VMEM((4, ATT_ROWS, 2 * CHUNK), f32)] * 2
                       + [pltpu.VMEM((4, ATT_ROWS, CHUNK), f32)],
        compiler_params=_params(("arbitrary",)),
    )(u, u, cos_t, sin_t, cos_t, sin_t, sinks)


def _att2_bwd(u, cos_t, sin_t, sinks, da, name):
    t_len = u.shape[1]
    nb = t_len // CHUNK

    def body(u_ref, prev_ref, cc_ref, sc_ref, cp_ref, sp_ref, sink_ref, da_ref, du_ref, dsink_ref,
             ck_s, cv_s, kp_s, kc_s, vp_s, vc_s, q_s, o_s, do_s, dq_s, s_s, pn_s, dp_s, dkt_s, dvt_s, pc_s):
        step = pl.program_id(0)
        nn = nb - 1 - step

        @pl.when(step == 0)
        def _():
            ck_s[...] = jnp.zeros_like(ck_s)
            cv_s[...] = jnp.zeros_like(cv_s)
            dsink_ref[...] = jnp.zeros_like(dsink_ref)
        tri = _tri4()
        cos_c, sin_c = cc_ref[...], sc_ref[...]
        cos_p, sin_p = cp_ref[...], sp_ref[...]
        kc_s[...] = _rope(u_ref[4], cos_c, sin_c)
        kp_s[...] = _rope(prev_ref[0], cos_p, sin_p)
        vc_s[...] = u_ref[5]
        vp_s[...] = prev_ref[1]
        sinks = sink_ref[...]
        lane16 = lax.broadcasted_iota(jnp.int32, (1, 16), 1)
        dsink = jnp.zeros((1, 16), f32)
        for g in range(4):
            q_g, o_g, do_g, dq_g = q_s.at[g], o_s.at[g], do_s.at[g], dq_s.at[g]
            s_g, pn_g, dp_g, pc_g = s_s.at[g], pn_s.at[g], dp_s.at[g], pc_s.at[g]
            q_g[...] = _rope(u_ref[g], cos_c, sin_c)
            gate = u_ref[6 + g]
            sg = _sigmoid(gate)
            dav = da_ref[g]
            do_g[...] = dav * (gate * sg)
            kv = slice(64 * g, 64 * (g + 1))
            kb = jnp.concatenate([kp_s[:, kv], kc_s[:, kv]], axis=0)
            vb = jnp.concatenate([vp_s[:, kv], vc_s[:, kv]], axis=0)
            q_st = _stack_heads(q_g)
            do_st = _stack_heads(do_g)
            s_g[...] = _dot_nt(q_st, kb)
            p_sink = _softmax_rows(s_g, pn_g, pc_g, _sink_col(sinks, g), tri, nn > 0)
            o = _dot(pn_g[...], vb)
            dvt_s[64 * g:64 * (g + 1), :] = _dot_tn(do_st, pn_g[...])
            dp_g[...] = _dot_nt(do_st, vb)
            delta = jnp.sum(do_st * o, axis=-1, keepdims=True)
            dpc = jnp.where(tri, dp_g[:, CHUNK:2 * CHUNK], dp_g[:, 0:CHUNK])
            dsc = pc_g[...] * (dpc - delta) * ATT_SCALE
            dp_g[:, 0:CHUNK] = jnp.where(tri, 0.0, dsc)
            dp_g[:, CHUNK:2 * CHUNK] = jnp.where(tri, dsc, 0.0)
            sd = p_sink * delta
            for r in range(4):
                rs = slice(CHUNK * r, CHUNK * (r + 1))
                o_g[:, 64 * r:64 * (r + 1)] = o[rs, :]
                ds_h = -jnp.sum(sd[rs, :], axis=0, keepdims=True)
                dsink = dsink + ds_h * (lane16 == 4 * g + r).astype(f32)
            ds = dp_g[...]
            dq = _dot(ds, kb)
            for r in range(4):
                dq_g[:, 64 * r:64 * (r + 1)] = dq[CHUNK * r:CHUNK * (r + 1), :]
            dkt_s[64 * g:64 * (g + 1), :] = _dot_tn(q_st, ds)
            du_ref[6 + g] = dav * o_g[...] * (sg * (1.0 + gate * (1.0 - sg)))
            du_ref[g] = _rope_bwd(dq_g[...], cos_c, sin_c)
        dk = dkt_s[...].T
        dv = dvt_s[...].T
        du_ref[4] = _rope_bwd(dk[CHUNK:2 * CHUNK, :], cos_c, sin_c) + ck_s[...]
        du_ref[5] = dv[CHUNK:2 * CHUNK, :] + cv_s[...]
        ck_s[...] = _rope_bwd(dk[0:CHUNK, :], cos_p, sin_p)
        cv_s[...] = dv[0:CHUNK, :]
        dsink_ref[...] += dsink

    def rb(n):
        return nb - 1 - n

    return _pc(
        body, name, grid=(nb,),
        in_specs=_att_in_specs(nb, True) + [pl.BlockSpec((ATT_KT, CHUNK, TILE), lambda n: (0, rb(n), 0))],
        out_specs=[pl.BlockSpec((ATT_NT, CHUNK, TILE), lambda n: (0, rb(n), 0)),
                   pl.BlockSpec((1, 16), lambda n: (0, 0))],
        out_shape=[SDS((ATT_NT, t_len, TILE), f32), SDS((1, 16), f32)],
        scratch_shapes=[pltpu.VMEM((CHUNK, TILE), f32)] * 6 + [pltpu.VMEM((4, CHUNK, TILE), f32)] * 4
                       + [pltpu.VMEM((4, ATT_ROWS, 2 * CHUNK), f32)] * 3
                       + [pltpu.VMEM((2 * CHUNK, 2 * CHUNK), f32)] * 2 + [pltpu.VMEM((4, ATT_ROWS, CHUNK), f32)],
        compiler_params=_params(("arbitrary",)),
    )(u, u, cos_t, sin_t, cos_t, sin_t, sinks, da)


_HBM = pl.BlockSpec(memory_space=pltpu.HBM)


def _all_gather_big(shards):
    n = len(shards)

    def body(*refs):
        x_refs, out_refs = refs[:n], refs[n:2 * n]
        send_sems, recv_sems, local_sems = refs[2 * n:]
        x, y, c = lax.axis_index("x"), lax.axis_index("y"), lax.axis_index("c")
        me, sibling = (x, y, c), (x, y, 1 - c)
        chips = [(1 - x, y), (x, 1 - y), (1 - x, 1 - y)]

        def slot(i, px, py, pc):
            return out_refs[i].at[4 * px + 2 * py + pc]

        def copy(i, k, block, to, src=None):
            return pltpu.make_async_remote_copy(
                src_ref=slot(i, *block) if src is None else src, dst_ref=slot(i, *block),
                send_sem=send_sems.at[7 * i + k], recv_sem=recv_sems.at[7 * i + k], device_id=to, device_id_type=MESH)

        mine = [pltpu.make_async_copy(x_refs[i], slot(i, *me), local_sems.at[i]) for i in range(n)]
        for cp in mine:
            cp.start()
        first = []
        for i in range(n):
            first.append(copy(i, 0, me, sibling, src=x_refs[i]))
            first += [copy(i, 1 + j, me, (*chip, c), src=x_refs[i]) for j, chip in enumerate(chips)]
        for cp in first:
            cp.start()
        passed = []
        for j, chip in enumerate(chips):
            for i in range(n):
                copy(i, 1 + j, (*chip, c), me).wait_recv()
                fwd = copy(i, 4 + j, (*chip, c), sibling)
                fwd.start()
                passed.append(fwd)
        for i in range(n):
            copy(i, 0, sibling, me).wait_recv()
            for j, chip in enumerate(chips):
                copy(i, 4 + j, (*chip, 1 - c), me).wait_recv()
        for cp in first + passed:
            cp.wait_send()
        for cp in mine:
            cp.wait()

    return _pc(
        body, "all_gather_big",
        in_specs=[_HBM] * n, out_specs=[_HBM] * n,
        out_shape=[SDS((N_DEV,) + s.shape, s.dtype) for s in shards],
        scratch_shapes=[pltpu.SemaphoreType.DMA((7 * n,)), pltpu.SemaphoreType.DMA((7 * n,)),
                        pltpu.SemaphoreType.DMA((n,))],
    )(*shards)


def _all_gather_direct(block, name):
    rows, width = block.shape

    def body(x_ref, out_ref, send_sems, recv_sems, local_sem):
        x, y, c = lax.axis_index("x"), lax.axis_index("y"), lax.axis_index("c")
        my_slot = 4 * x + 2 * y + c

        def peer(k):
            return (1 - x if k & 4 else x, 1 - y if k & 2 else y, 1 - c if k & 1 else c)

        def copy(k):
            px, py, pc = peer(k)
            return pltpu.make_async_remote_copy(
                src_ref=x_ref, dst_ref=out_ref.at[my_slot], send_sem=send_sems.at[k - 1], recv_sem=recv_sems.at[k - 1],
                device_id=(px, py, pc), device_id_type=MESH)

        def arrival(k):
            px, py, pc = peer(k)
            return pltpu.make_async_remote_copy(
                src_ref=x_ref, dst_ref=out_ref.at[4 * px + 2 * py + pc], send_sem=send_sems.at[k - 1],
                recv_sem=recv_sems.at[k - 1], device_id=(px, py, pc), device_id_type=MESH)

        mine = pltpu.make_async_copy(x_ref, out_ref.at[my_slot], local_sem)
        mine.start()
        for k in range(1, N_DEV):
            copy(k).start()
        for k in range(1, N_DEV):
            arrival(k).wait_recv()
        for k in range(1, N_DEV):
            copy(k).wait_send()
        mine.wait()

    return _pc(
        body, name,
        in_specs=[_HBM], out_specs=_HBM,
        out_shape=SDS((N_DEV, rows, width), block.dtype),
        scratch_shapes=[pltpu.SemaphoreType.DMA((7,)), pltpu.SemaphoreType.DMA((7,)), pltpu.SemaphoreType.DMA],
    )(block)


N_CHIP = N_DEV // 2


def _exchange_sibling(gs):
    n = len(gs)

    def body(*refs):
        g_refs, out_refs = refs[:n], refs[n:2 * n]
        send_sems, recv_sems = refs[2 * n:]
        x, y, c = lax.axis_index("x"), lax.axis_index("y"), lax.axis_index("c")
        cps = [pltpu.make_async_remote_copy(
            src_ref=g_refs[i].at[2 * k + 1 - c], dst_ref=out_refs[i].at[k], send_sem=send_sems.at[N_CHIP * i + k],
            recv_sem=recv_sems.at[N_CHIP * i + k], device_id=(x, y, 1 - c), device_id_type=MESH)
            for i in range(n) for k in range(N_CHIP)]
        for cp in cps:
            cp.start()
        for cp in cps:
            cp.wait()

    return _pc(
        body, "rs_sibling",
        in_specs=[_HBM] * n, out_specs=[_HBM] * n,
        out_shape=[SDS((N_CHIP,) + g.shape[1:], g.dtype) for g in gs],
        scratch_shapes=[pltpu.SemaphoreType.DMA((N_CHIP * n,)), pltpu.SemaphoreType.DMA((N_CHIP * n,))],
    )(*gs)


def _pair_sum(g, r1, cidx, tr, name):
    _, rows, width = g.shape

    def body(c_ref, g_ref, r_ref, o_ref):
        o_ref[...] = (g_ref[...].astype(f32) + r_ref[...].astype(f32)).astype(o_ref.dtype)

    return pl.pallas_call(
        body, name=name,
        grid_spec=pltpu.PrefetchScalarGridSpec(
            num_scalar_prefetch=1, grid=(N_CHIP, rows // tr),
            in_specs=[pl.BlockSpec((1, tr, width), lambda k, i, c_ref: (2 * k + c_ref[0], i, 0)),
                      pl.BlockSpec((1, tr, width), lambda k, i, c_ref: (k, i, 0))],
            out_specs=pl.BlockSpec((1, tr, width), lambda k, i, c_ref: (k, i, 0))),
        out_shape=SDS((N_CHIP, rows, width), g.dtype),
        compiler_params=_params(("arbitrary", "arbitrary")),
    )(cidx, g, r1)


def _exchange_chips(ps):
    n = len(ps)

    def body(*refs):
        p_refs, out_refs = refs[:n], refs[n:2 * n]
        send_sems, recv_sems, local_sems = refs[2 * n:]
        x, y, c = lax.axis_index("x"), lax.axis_index("y"), lax.axis_index("c")
        my_chip = 2 * x + y
        chips = [(1 - x, y), (x, 1 - y), (1 - x, 1 - y)]

        def copy(i, j):
            px, py = chips[j]
            return pltpu.make_async_remote_copy(
                src_ref=p_refs[i].at[2 * px + py], dst_ref=out_refs[i].at[my_chip], send_sem=send_sems.at[3 * i + j],
                recv_sem=recv_sems.at[3 * i + j], device_id=(px, py, c), device_id_type=MESH)

        def arrival(i, j):
            px, py = chips[j]
            return pltpu.make_async_remote_copy(
                src_ref=p_refs[i].at[my_chip], dst_ref=out_refs[i].at[2 * px + py], send_sem=send_sems.at[3 * i + j],
                recv_sem=recv_sems.at[3 * i + j], device_id=(px, py, c), device_id_type=MESH)

        mine = [pltpu.make_async_copy(p_refs[i].at[my_chip], out_refs[i].at[my_chip], local_sems.at[i])
                for i in range(n)]
        for cp in mine:
            cp.start()
        for i in range(n):
            for j in range(3):
                copy(i, j).start()
        for i in range(n):
            for j in range(3):
                arrival(i, j).wait_recv()
        for i in range(n):
            for j in range(3):
                copy(i, j).wait_send()
        for cp in mine:
            cp.wait()

    return _pc(
        body, "rs_chips",
        in_specs=[_HBM] * n, out_specs=[_HBM] * n,
        out_shape=[SDS(p.shape, p.dtype) for p in ps],
        scratch_shapes=[pltpu.SemaphoreType.DMA((3 * n,)), pltpu.SemaphoreType.DMA((3 * n,)),
                        pltpu.SemaphoreType.DMA((n,))],
    )(*ps)


def _adamw(parts, w, m, v, tr, name):
    n, rows, width = parts.shape
    c1 = 1.0 / (1.0 - ADAM_B1 ** ADAM_STEP)
    c2 = 1.0 / (1.0 - ADAM_B2 ** ADAM_STEP)

    def body(p_ref, w_ref, m_ref, v_ref, g_ref, d_ref, mo_ref, vo_ref):
        g = p_ref[0].astype(f32)
        for k in range(1, n):
            g = g + p_ref[k].astype(f32)
        mn = ADAM_B1 * m_ref[...] + (1.0 - ADAM_B1) * g
        vn = ADAM_B2 * v_ref[...] + (1.0 - ADAM_B2) * (g * g)
        g_ref[...] = g
        mo_ref[...] = mn
        vo_ref[...] = vn
        d_ref[...] = -ADAM_LR * ((mn * c1) / (jnp.sqrt(vn * c2) + ADAM_EPS) + ADAM_WD * w_ref[...])

    blk = pl.BlockSpec((tr, width), lambda i: (i, 0))
    return _pc(
        body, name, grid=(rows // tr,),
        in_specs=[pl.BlockSpec((n, tr, width), lambda i: (0, i, 0)), blk, blk, blk],
        out_specs=[blk, blk, blk, blk],
        out_shape=[SDS((rows, width), f32)] * 4,
        compiler_params=_params(("arbitrary",)),
    )(parts, w, m, v)


ROWS_REST = ROWS_SSM_OUT + ROWS_ATT_OUT + 16


def _pack_rest(ssm_w_out, att_w_out, conv_w):
    conv = jnp.pad(conv_w.reshape(4, 1024), ((0, 12), (0, 0)))
    return jnp.concatenate([ssm_w_out.reshape(ROWS_SSM_OUT, 1024), att_w_out.reshape(ROWS_ATT_OUT, 1024), conv], axis=0)


def _unpack_rest(p):
    o = ROWS_SSM_OUT + ROWS_ATT_OUT
    return (p[0:ROWS_SSM_OUT].reshape(2, 256, 1024), p[ROWS_SSM_OUT:o].reshape(2, 128, 1024),
            p[o:o + 4].reshape(2, 4, 512))


def _pack_grads(d_ssm_w_in, d_ssm_w_out, d_att_w_in, d_att_w_out, d_conv_w):
    wire = lambda t: t.astype(MXU_DTYPE)
    a = jnp.transpose(wire(d_ssm_w_in).reshape(2, 1024, 8, 772), (2, 0, 1, 3)).reshape(8, 2048, 772)
    c = jnp.transpose(wire(d_att_w_in).reshape(2, 1024, 8, 320), (2, 0, 1, 3)).reshape(8, 2048, 320)
    b = jnp.transpose(wire(d_ssm_w_out).reshape(2, 8, 256, 1024), (1, 0, 2, 3)).reshape(8, ROWS_SSM_OUT, 1024)
    d = jnp.transpose(wire(d_att_w_out).reshape(2, 8, 128, 1024), (1, 0, 2, 3)).reshape(8, ROWS_ATT_OUT, 1024)
    e = jnp.transpose(wire(d_conv_w).reshape(2, 4, 8, 512), (2, 0, 1, 3)).reshape(8, 4, 1024)
    e = jnp.pad(e, ((0, 0), (0, 12), (0, 0)))
    return a, c, jnp.concatenate([b, d, e], axis=1)


def _pad8(a):
    return jnp.pad(a, ((0, 8 - a.shape[0]), (0, 0)))


def _pack_small(pre_norm, post_norm, conv_b, gate_norm, dt_bias, a_log, d_skip, sinks, extra=None):
    row = jnp.concatenate([dt_bias.reshape(1, 64), a_log.reshape(1, 64), d_skip.reshape(1, 64), sinks.reshape(1, 32),
                           jnp.zeros((1, 1024 - 224), f32)], axis=1)
    if extra is not None:
        row = row + jnp.pad(extra.reshape(1, 1), ((0, 0), (224, 1024 - 225)))
    return jnp.concatenate([_pad8(pre_norm.reshape(4, 1024)), _pad8(post_norm.reshape(4, 1024)),
                            conv_b.reshape(8, 1024), _pad8(gate_norm.reshape(4, 1024)), _pad8(row)], axis=0)


def _unpack_small(p):
    row = p[32]
    return (p[0:4], p[8:12], p[16:24].reshape(2, 4096), row[0:64].reshape(2, 32), row[64:128].reshape(2, 32),
            row[128:192].reshape(2, 32), p[24:28].reshape(2, 2048), row[192:224].reshape(2, 16))


def _ssm_w_in_tiles(w):
    wb = w[:, 4096:5120].reshape(1024, 8, 128)
    wc = w[:, 5120:6144].reshape(1024, 8, 128)
    wbc = jnp.concatenate([wb, wc], axis=2).reshape(1024, 2048)
    return jnp.concatenate([w[:, 0:4096], wbc, w[:, 6144:6176], jnp.zeros((1024, 224), w.dtype)], axis=1)


def _ssm_w_in_untile(dw):
    dbc = dw[:, 4096:6144].reshape(1024, 8, 256)
    return jnp.concatenate([dw[:, 0:4096], dbc[:, :, 0:128].reshape(1024, 1024), dbc[:, :, 128:256].reshape(1024, 1024),
                            dw[:, 6144:6176]], axis=1)


def _conv_tiles(cw):
    k = cw.shape[0]
    xs = jnp.transpose(cw[:, 0:2048].reshape(k, 8, 256), (1, 0, 2))
    b = cw[:, 2048:3072].reshape(k, 8, 128)
    c = cw[:, 3072:4096].reshape(k, 8, 128)
    bc = jnp.transpose(jnp.concatenate([b, c], axis=2), (1, 0, 2))
    return jnp.concatenate([xs, bc], axis=0)


def _conv_untile(t):
    k = t.shape[1]
    xs = jnp.transpose(t[0:8], (1, 0, 2)).reshape(k, 2048)
    bc = jnp.transpose(t[8:16], (1, 0, 2))
    return jnp.concatenate([xs, bc[:, :, 0:128].reshape(k, 1024), bc[:, :, 128:256].reshape(k, 1024)], axis=1)


def _rope_tables(positions):
    inv = ROPE_THETA ** (-jnp.arange(0, 16, 2, dtype=f32) / 16)
    ang = positions.astype(f32).reshape(-1, 1) * inv
    cos, sin = jnp.cos(ang), jnp.sin(ang)
    t_len = ang.shape[0]
    cos64 = jnp.concatenate([cos, cos, jnp.ones((t_len, 48), f32)], axis=1)
    sin64 = jnp.concatenate([-sin, sin, jnp.zeros((t_len, 48), f32)], axis=1)
    return jnp.tile(cos64, (1, 4)), jnp.tile(sin64, (1, 4))


def kernel(x, positions, pre_norm, post_norm, ssm_w_in, ssm_conv_w, ssm_conv_b, ssm_dt_bias, ssm_a_log, ssm_d, ssm_gate_norm, ssm_w_out, att_w_in, att_sinks, att_w_out, loss_target, m_pre_norm, m_post_norm, m_ssm_w_in, m_ssm_conv_w, m_ssm_conv_b, m_ssm_dt_bias, m_ssm_a_log, m_ssm_d, m_ssm_gate_norm, m_ssm_w_out, m_att_w_in, m_att_sinks, m_att_w_out, v_pre_norm, v_post_norm, v_ssm_w_in, v_ssm_conv_w, v_ssm_conv_b, v_ssm_dt_bias, v_ssm_a_log, v_ssm_d, v_ssm_gate_norm, v_ssm_w_out, v_att_w_in, v_att_sinks, v_att_w_out):
    t_len = x.shape[1]
    tm = min(1024, t_len)
    xin = x.reshape(t_len, D_MODEL)
    tgt = loss_target.reshape(t_len, D_MODEL)
    cidx = lax.axis_index("c").astype(jnp.int32).reshape(1)

    g_ssm_in, g_att_in, g_ssm_out, g_att_out = _all_gather_big(
        [ssm_w_in.astype(MXU_DTYPE), att_w_in.astype(MXU_DTYPE), ssm_w_out.astype(MXU_DTYPE),
         att_w_out.astype(MXU_DTYPE)])
    conv_local = jnp.concatenate([ssm_conv_w.reshape(4, 1024), jnp.zeros((4, 1024), f32)], axis=0)
    conv_all = _all_gather_direct(conv_local, "all_gather_conv")[:, 0:4]
    w_ssm_in = jnp.transpose(g_ssm_in, (1, 2, 0, 3)).reshape(2, 1024, SSM_IN)
    w_ssm_out = jnp.transpose(g_ssm_out, (1, 0, 2, 3)).reshape(2, SSM_INNER, 1024)
    w_att_in = jnp.transpose(g_att_in, (1, 2, 0, 3)).reshape(2, 1024, ATT_IN)
    w_att_out = jnp.transpose(g_att_out, (1, 0, 2, 3)).reshape(2, 1024, 1024)
    conv_w = jnp.transpose(conv_all.reshape(8, 2, 4, 512), (1, 2, 0, 3)).reshape(2, 4, 4096)
    cos_t, sin_t = _rope_tables(positions)

    saved = []
    xc = xin
    for i in range(4):
        j = i // 2
        wn_pre, wn_post = pre_norm[i].reshape(1, D_MODEL), post_norm[i].reshape(1, D_MODEL)
        if i % 2 == 0:
            w_in = _ssm_w_in_tiles(w_ssm_in[j])
            cw, cb = _conv_tiles(conv_w[j]), _conv_tiles(ssm_conv_b[j].reshape(1, 4096))
            dtb, alog, dsk = ssm_dt_bias[j].reshape(1, 32), ssm_a_log[j].reshape(1, 32), ssm_d[j].reshape(1, 32)
            gn = ssm_gate_norm[j].reshape(SSM_KT, 1, TILE)
            u, h = _mm_in(xc, wn_pre, w_in, 5, tm, f"ssm_in_{j}")
            xbc = _conv_fwd(u, cw, cb, f"ssm_conv_{j}")
            a3, yp, hst = _ssm2_fwd(u, xbc, dtb, alog, dsk, gn, f"ssm_core_{j}")
            y, xn = _mm_out(a3, w_ssm_out[j], xc, wn_post, 4, tm, f"ssm_out_{j}")
            saved.append(dict(x=xc, u=u, h=h, a=a3, yp=yp, hst=hst, y=y, w_in=w_in, cw=cw, cb=cb, dtb=dtb, alog=alog,
                              dsk=dsk, gn=gn, xbc=xbc))
        else:
            sinks = att_sinks[j].reshape(1, 16)
            u, h = _mm_in(xc, wn_pre, w_att_in[j], 5, tm, f"att_in_{j}")
            a = _att2_fwd(u, cos_t, sin_t, sinks, f"att_core_{j}")
            y, xn = _mm_out(a, w_att_out[j], xc, wn_post, 4, tm, f"att_out_{j}")
            saved.append(dict(x=xc, u=u, h=h, a=a, y=y, sinks=sinks))
        xc = xn

    dx, loss_part = _loss_grad(xc, tgt, tm)

    d_pre, d_post = [None] * 4, [None] * 4
    d_ssm_in, d_ssm_out, d_att_in, d_att_out = [None] * 2, [None] * 2, [None] * 2, [None] * 2
    d_cw, d_cb, d_dtb, d_alog, d_dsk, d_gn, d_sinks = ([None] * 2 for _ in range(7))
    for i in reversed(range(4)):
        j = i // 2
        s = saved[i]
        wn_pre, wn_post = pre_norm[i].reshape(1, D_MODEL), post_norm[i].reshape(1, D_MODEL)
        if i % 2 == 0:
            da3, dy, d_post[i] = _mm_dout(s["y"], dx, wn_post, w_ssm_out[j], 4, tm, f"ssm_dout_{j}")
            d_ssm_out[j] = _dw_rows(s["a"], dy, 4, tm, f"ssm_dwout_{j}")
            du, d_dtb[j], d_alog[j], d_dsk[j], dgn = _ssm2_bwd(
                s["u"], s["xbc"], s["yp"], s["hst"], da3, s["dtb"], s["alog"], s["dsk"], s["gn"],
                f"ssm_core_bwd_{j}")
            du, dcw, dcb = _conv_bwd(s["u"], du, s["cw"], s["cb"], f"ssm_conv_bwd_{j}")
            d_cw[j], d_cb[j], d_gn[j] = _conv_untile(dcw), _conv_untile(dcb), dgn.reshape(1, SSM_INNER)
            d_ssm_in[j] = _ssm_w_in_untile(_dw_cols(s["h"], du, 5, tm, f"ssm_dwin_{j}"))
            dx, d_pre[i] = _mm_dh(du, s["w_in"], s["x"], dx, wn_pre, 5, tm, f"ssm_dh_{j}")
        else:
            da, dy, d_post[i] = _mm_dout(s["y"], dx, wn_post, w_att_out[j], 4, tm, f"att_dout_{j}")
            d_att_out[j] = _dw_rows(s["a"], dy, 4, tm, f"att_dwout_{j}")
            du, d_sinks[j] = _att2_bwd(s["u"], cos_t, sin_t, s["sinks"], da, f"att_core_bwd_{j}")
            d_att_in[j] = _dw_cols(s["h"], du, 5, tm, f"att_dwin_{j}")
            dx, d_pre[i] = _mm_dh(du, w_att_in[j], s["x"], dx, wn_pre, 5, tm, f"att_dh_{j}")

    gs = _pack_grads(jnp.stack(d_ssm_in), jnp.stack(d_ssm_out), jnp.stack(d_att_in), jnp.stack(d_att_out),
                     jnp.stack(d_cw))
    r1 = _exchange_sibling(gs)
    tiles = (256, 256, ROWS_REST // 7)
    pairs = [_pair_sum(g, r, cidx, tr, f"rs_pair_sum_{k}") for k, (g, r, tr) in enumerate(zip(gs, r1, tiles))]
    parts = _exchange_chips(pairs)
    flat = lambda t: t.reshape(2048, t.shape[-1])
    a4 = _adamw(parts[0], flat(ssm_w_in), flat(m_ssm_w_in), flat(v_ssm_w_in), tiles[0], "adamw_ssm_in")
    b4 = _adamw(parts[1], flat(att_w_in), flat(m_att_w_in), flat(v_att_w_in), tiles[1], "adamw_att_in")
    c4 = _adamw(parts[2], _pack_rest(ssm_w_out, att_w_out, ssm_conv_w), _pack_rest(m_ssm_w_out, m_att_w_out, m_ssm_conv_w),
                _pack_rest(v_ssm_w_out, v_att_w_out, v_ssm_conv_w), tiles[2], "adamw_rest")
    big = []
    for k in range(4):
        o_ssm_out, o_att_out, o_conv = _unpack_rest(c4[k])
        big.append((a4[k].reshape(2, 1024, 772), o_ssm_out, b4[k].reshape(2, 1024, 320), o_att_out, o_conv))

    small_local = _pack_small(jnp.concatenate(d_pre, axis=0), jnp.concatenate(d_post, axis=0),
                              jnp.concatenate(d_cb, axis=0), jnp.concatenate(d_gn, axis=0),
                              jnp.concatenate(d_dtb, axis=0), jnp.concatenate(d_alog, axis=0),
                              jnp.concatenate(d_dsk, axis=0), jnp.concatenate(d_sinks, axis=0), loss_part[0, 0])
    small_all = _all_gather_direct(small_local, "all_gather_small")
    ws = _pack_small(pre_norm, post_norm, ssm_conv_b, ssm_gate_norm, ssm_dt_bias, ssm_a_log, ssm_d, att_sinks)
    ms = _pack_small(m_pre_norm, m_post_norm, m_ssm_conv_b, m_ssm_gate_norm, m_ssm_dt_bias, m_ssm_a_log, m_ssm_d,
                     m_att_sinks)
    vs = _pack_small(v_pre_norm, v_post_norm, v_ssm_conv_b, v_ssm_gate_norm, v_ssm_dt_bias, v_ssm_a_log, v_ssm_d,
                     v_att_sinks)
    small4 = _adamw(small_all, ws, ms, vs, ROWS_SMALL, "adamw_small")
    loss = small4[0][32, 224]
    small = [_unpack_small(t) for t in small4]

    outs = [loss, dx.reshape(1, t_len, D_MODEL)]
    for k in range(4):
        b_ssm_in, b_ssm_out, b_att_in, b_att_out, b_conv = big[k]
        s_pre, s_post, s_cb, s_dtb, s_alog, s_d, s_gn, s_sinks = small[k]
        outs += [s_pre, s_post, b_ssm_in, b_conv, s_cb, s_dtb, s_alog, s_d, s_gn, b_ssm_out, b_att_in, s_sinks,
                 b_att_out]
    return tuple(outs)
```

```python
import jax
import jax.numpy as jnp
from jax import lax
from jax.experimental import pallas as pl
from jax.experimental.pallas import tpu as pltpu

f32 = jnp.float32
MXU_DTYPE = jnp.bfloat16
SDS = jax.ShapeDtypeStruct
MESH = pl.DeviceIdType.MESH

D_MODEL = 1024
EPS = 1e-6
TILE = 256
CHUNK = 128
ATT_ROWS = 4 * CHUNK
SSM_HEADS = 32
SSM_GROUPS = 8
SSM_P = 64
SSM_N = 128
SSM_INNER = 2048
SSM_IN = 6176
SSM_NT = 25
SSM_KT = 8
ATT_NT = 10
ATT_KT = 4
ATT_IN = 2560
ROPE_THETA = 500000.0
N_DEV = 8
VMEM_LIMIT = 56 * 1024 * 1024

ADAM_LR = 0.001
ADAM_B1 = 0.9
ADAM_B2 = 0.999
ADAM_EPS = 1e-08
ADAM_WD = 0.01
ADAM_STEP = 10

ROWS_SSM_OUT = 2 * 256
ROWS_ATT_OUT = 2 * 128
ROWS_SMALL = 40


def _pc(body, name, **kw):
    return pl.pallas_call(body, name=name, **kw)


def _params(sem):
    return pltpu.CompilerParams(dimension_semantics=sem, vmem_limit_bytes=VMEM_LIMIT)


def _sigmoid(x):
    return 1.0 / (1.0 + jnp.exp(-x))


def _silu(x):
    return x * _sigmoid(x)


def _softplus(x):
    return jnp.maximum(x, 0.0) + jnp.log(1.0 + jnp.exp(-jnp.abs(x)))


def _mx(x):
    return x.astype(MXU_DTYPE)


def _dot(a, b):
    return jnp.dot(_mx(a), _mx(b), preferred_element_type=f32)


def _dot_nt(a, b):
    return lax.dot_general(_mx(a), _mx(b), (((1,), (1,)), ((), ())), preferred_element_type=f32)


def _dot_tn(a, b):
    return lax.dot_general(_mx(a), _mx(b), (((0,), (0,)), ((), ())), preferred_element_type=f32)


def _rms_fwd(x, w):
    r = lax.rsqrt(jnp.mean(x * x, axis=-1, keepdims=True) + EPS)
    return x * r * w


def _rms_bwd(x, w, dy):
    r = lax.rsqrt(jnp.mean(x * x, axis=-1, keepdims=True) + EPS)
    xh = x * r
    dw = jnp.sum(dy * xh, axis=0, keepdims=True)
    g = dy * w
    dx = r * (g - xh * jnp.mean(g * xh, axis=-1, keepdims=True))
    return dx, dw


def _mm_in(x, wn, w, ntb, tm, name):
    t_len, d = x.shape
    nt = w.shape[1] // TILE

    def body(x_ref, wn_ref, w_ref, u_ref, h_ref):
        @pl.when(pl.program_id(1) == 0)
        def _():
            h_ref[...] = _rms_fwd(x_ref[...], wn_ref[...]).astype(h_ref.dtype)
        h = h_ref[...]
        for t in range(ntb):
            u_ref[t] = jnp.dot(h, w_ref[:, TILE * t:TILE * (t + 1)], preferred_element_type=f32)

    return _pc(
        body, name, grid=(t_len // tm, nt // ntb),
        in_specs=[pl.BlockSpec((tm, d), lambda i, j: (i, 0)),
                  pl.BlockSpec((1, d), lambda i, j: (0, 0)),
                  pl.BlockSpec((d, ntb * TILE), lambda i, j: (0, j))],
        out_specs=[pl.BlockSpec((ntb, tm, TILE), lambda i, j: (j, i, 0)),
                   pl.BlockSpec((tm, d), lambda i, j: (i, 0))],
        out_shape=[SDS((nt, t_len, TILE), f32), SDS((t_len, d), MXU_DTYPE)],
        compiler_params=_params(("arbitrary", "arbitrary")),
    )(x, wn, w)


def _mm_dout(y, dxn, wn, w, ntb, tm, name):
    t_len, d = y.shape
    nt = w.shape[0] // TILE

    def body(y_ref, dxn_ref, wn_ref, w_ref, da_ref, dy_ref, dwn_ref):
        i, j = pl.program_id(0), pl.program_id(1)

        @pl.when((i == 0) & (j == 0))
        def _():
            dwn_ref[...] = jnp.zeros_like(dwn_ref)

        @pl.when(j == 0)
        def _():
            dy, dw = _rms_bwd(y_ref[...], wn_ref[...], dxn_ref[...])
            dy_ref[...] = dy.astype(dy_ref.dtype)
            dwn_ref[...] += dw
        dy = dy_ref[...]
        for t in range(ntb):
            da_ref[t] = _dot_nt(dy, w_ref[TILE * t:TILE * (t + 1), :])

    return _pc(
        body, name, grid=(t_len // tm, nt // ntb),
        in_specs=[pl.BlockSpec((tm, d), lambda i, j: (i, 0)),
                  pl.BlockSpec((tm, d), lambda i, j: (i, 0)),
                  pl.BlockSpec((1, d), lambda i, j: (0, 0)),
                  pl.BlockSpec((ntb * TILE, d), lambda i, j: (j, 0))],
        out_specs=[pl.BlockSpec((ntb, tm, TILE), lambda i, j: (j, i, 0)),
                   pl.BlockSpec((tm, d), lambda i, j: (i, 0)),
                   pl.BlockSpec((1, d), lambda i, j: (0, 0))],
        out_shape=[SDS((nt, t_len, TILE), f32), SDS((t_len, d), MXU_DTYPE), SDS((1, d), f32)],
        compiler_params=_params(("arbitrary", "arbitrary")),
    )(y, dxn, wn, w)


def _mm_out(a, w, x, wn, ktb, tm, name):
    kt, t_len, _ = a.shape
    d = w.shape[1]
    nk = kt // ktb

    def body(a_ref, w_ref, x_ref, wn_ref, y_ref, xn_ref, acc):
        k = pl.program_id(1)

        @pl.when(k == 0)
        def _():
            acc[...] = jnp.zeros_like(acc)
        s = acc[...]
        for t in range(ktb):
            s = s + jnp.dot(a_ref[t], w_ref[TILE * t:TILE * (t + 1), :], preferred_element_type=f32)
        acc[...] = s

        @pl.when(k == nk - 1)
        def _():
            y = acc[...]
            y_ref[...] = y
            xn_ref[...] = x_ref[...] + _rms_fwd(y, wn_ref[...])

    return _pc(
        body, name, grid=(t_len // tm, nk),
        in_specs=[pl.BlockSpec((ktb, tm, TILE), lambda i, k: (k, i, 0)),
                  pl.BlockSpec((ktb * TILE, d), lambda i, k: (k, 0)),
                  pl.BlockSpec((tm, d), lambda i, k: (i, 0)),
                  pl.BlockSpec((1, d), lambda i, k: (0, 0))],
        out_specs=[pl.BlockSpec((tm, d), lambda i, k: (i, 0)),
                   pl.BlockSpec((tm, d), lambda i, k: (i, 0))],
        out_shape=[SDS((t_len, d), f32), SDS((t_len, d), f32)],
        scratch_shapes=[pltpu.VMEM((tm, d), f32)],
        compiler_params=_params(("arbitrary", "arbitrary")),
    )(a, w, x, wn)


def _mm_dh(du, w, x, dxn, wn, ktb, tm, name):
    kt, t_len, _ = du.shape
    d = w.shape[0]
    nk = kt // ktb

    def body(du_ref, w_ref, x_ref, dxn_ref, wn_ref, dx_ref, dwn_ref, acc):
        i, k = pl.program_id(0), pl.program_id(1)

        @pl.when((i == 0) & (k == 0))
        def _():
            dwn_ref[...] = jnp.zeros_like(dwn_ref)

        @pl.when(k == 0)
        def _():
            acc[...] = jnp.zeros_like(acc)
        s = acc[...]
        for t in range(ktb):
            s = s + _dot_nt(du_ref[t], w_ref[:, TILE * t:TILE * (t + 1)])
        acc[...] = s

        @pl.when(k == nk - 1)
        def _():
            dxp, dw = _rms_bwd(x_ref[...], wn_ref[...], acc[...])
            dx_ref[...] = dxn_ref[...] + dxp
            dwn_ref[...] += dw

    return _pc(
        body, name, grid=(t_len // tm, nk),
        in_specs=[pl.BlockSpec((ktb, tm, TILE), lambda i, k: (k, i, 0)),
                  pl.BlockSpec((d, ktb * TILE), lambda i, k: (0, k)),
                  pl.BlockSpec((tm, d), lambda i, k: (i, 0)),
                  pl.BlockSpec((tm, d), lambda i, k: (i, 0)),
                  pl.BlockSpec((1, d), lambda i, k: (0, 0))],
        out_specs=[pl.BlockSpec((tm, d), lambda i, k: (i, 0)),
                   pl.BlockSpec((1, d), lambda i, k: (0, 0))],
        out_shape=[SDS((t_len, d), f32), SDS((1, d), f32)],
        scratch_shapes=[pltpu.VMEM((tm, d), f32)],
        compiler_params=_params(("arbitrary", "arbitrary")),
    )(du, w, x, dxn, wn)


def _dw_cols(a, b, ntb, tk, name):
    t_len, kdim = a.shape
    nt = b.shape[0]

    def body(a_ref, b_ref, o_ref):
        @pl.when(pl.program_id(1) == 0)
        def _():
            o_ref[...] = jnp.zeros_like(o_ref)
        av = a_ref[...]
        for s in range(ntb):
            o_ref[:, TILE * s:TILE * (s + 1)] += _dot_tn(av, b_ref[s])

    return _pc(
        body, name, grid=(nt // ntb, t_len // tk),
        in_specs=[pl.BlockSpec((tk, kdim), lambda j, t: (t, 0)),
                  pl.BlockSpec((ntb, tk, TILE), lambda j, t: (j, t, 0))],
        out_specs=pl.BlockSpec((kdim, ntb * TILE), lambda j, t: (0, j)),
        out_shape=SDS((kdim, nt * TILE), f32),
        compiler_params=_params(("arbitrary", "arbitrary")),
    )(a, b)


def _dw_rows(a, b, ktb, tk, name):
    kt, t_len, _ = a.shape
    d = b.shape[1]

    def body(a_ref, b_ref, o_ref):
        @pl.when(pl.program_id(1) == 0)
        def _():
            o_ref[...] = jnp.zeros_like(o_ref)
        bv = b_ref[...]
        for s in range(ktb):
            o_ref[TILE * s:TILE * (s + 1), :] += _dot_tn(a_ref[s], bv)

    return _pc(
        body, name, grid=(kt // ktb, t_len // tk),
        in_specs=[pl.BlockSpec((ktb, tk, TILE), lambda k, t: (k, t, 0)),
                  pl.BlockSpec((tk, d), lambda k, t: (t, 0))],
        out_specs=pl.BlockSpec((ktb * TILE, d), lambda k, t: (k, 0)),
        out_shape=SDS((kt * TILE, d), f32),
        compiler_params=_params(("arbitrary", "arbitrary")),
    )(a, b)


def _loss_grad(x, tgt, tm):
    t_len, d = x.shape

    def body(x_ref, t_ref, dx_ref, l_ref):
        @pl.when(pl.program_id(0) == 0)
        def _():
            l_ref[...] = jnp.zeros_like(l_ref)
        e = x_ref[...] - t_ref[...]
        dx_ref[...] = e * (1.0 / d)
        row = jnp.mean(e * e, axis=-1, keepdims=True)
        l_ref[...] += 0.5 * jnp.sum(row, axis=0, keepdims=True)

    return _pc(
        body, "loss_grad", grid=(t_len // tm,),
        in_specs=[pl.BlockSpec((tm, d), lambda i: (i, 0)), pl.BlockSpec((tm, d), lambda i: (i, 0))],
        out_specs=[pl.BlockSpec((tm, d), lambda i: (i, 0)), pl.BlockSpec((1, 128), lambda i: (0, 0))],
        out_shape=[SDS((t_len, d), f32), SDS((1, 128), f32)],
        compiler_params=_params(("arbitrary",)),
    )(x, tgt)


def _tri(lower):
    r = lax.broadcasted_iota(jnp.int32, (CHUNK, CHUNK), 0)
    c = lax.broadcasted_iota(jnp.int32, (CHUNK, CHUNK), 1)
    return ((c <= r) if lower else (c >= r)).astype(f32)


def _split(x, n):
    parts = []
    for _ in range(n):
        p = x.astype(jnp.bfloat16)
        parts.append(p)
        x = x - p.astype(f32)
    return parts


def _dot_exact(a, b, dims, split_a, n=3):
    out = None
    if split_a:
        b = b.astype(jnp.bfloat16)
        for p in _split(a, n):
            t = lax.dot_general(p, b, (dims, ((), ())), preferred_element_type=f32)
            out = t if out is None else out + t
    else:
        a = a.astype(jnp.bfloat16)
        for p in _split(b, n):
            t = lax.dot_general(a, p, (dims, ((), ())), preferred_element_type=f32)
            out = t if out is None else out + t
    return out


def _dt_path(dt_raw, dtb, alog):
    dtr = dt_raw + dtb
    dt = _softplus(dtr)
    a_neg = -jnp.exp(alog)
    a = dt * a_neg
    acs = _dot_exact(_tri(True), a, ((1,), (0,)), False)
    acs_t = _dot_exact(a, _tri(False), ((0,), (0,)), True)
    return dtr, dt, a_neg, acs, acs_t


CONV_ROWS = 1024
CONV_SUB = 32


def _conv_specs(nb, rows, rev):
    def ridx(i):
        return (nb - 1 - i) if rev else i
    return [
        pl.BlockSpec((1, rows, TILE), lambda p, i: (8 + p, ridx(i), 0)),
        pl.BlockSpec((1, 8, TILE), lambda p, i: (8 + p, jnp.maximum(ridx(i) * (rows // 8) - 1, 0), 0)),
        pl.BlockSpec((1, 4, TILE), lambda p, i: (p, 0, 0)),
        pl.BlockSpec((1, 1, TILE), lambda p, i: (p, 0, 0)),
    ]


def _conv_fwd(u, cw, cb, name):
    t_len = u.shape[1]
    rows = min(CONV_ROWS, t_len)
    nb = t_len // rows

    def body(u_ref, halo_ref, cw_ref, cb_ref, o_ref, win_s):
        i = pl.program_id(1)
        win_s[0:8, :] = jnp.where(i > 0, halo_ref[0], 0.0)
        win_s[8:8 + rows, :] = u_ref[0]
        w = [cw_ref[0, k:k + 1, :] for k in range(4)]
        b = cb_ref[0]
        for s in range(rows // CONV_SUB):
            o = CONV_SUB * s
            acc = b
            for k in range(4):
                acc = acc + w[k] * win_s[5 + k + o:5 + k + o + CONV_SUB, :]
            o_ref[0, o:o + CONV_SUB, :] = _silu(acc)

    return _pc(
        body, name, grid=(16, nb),
        in_specs=_conv_specs(nb, rows, False),
        out_specs=pl.BlockSpec((1, rows, TILE), lambda p, i: (p, i, 0)),
        out_shape=SDS((16, t_len, TILE), f32),
        scratch_shapes=[pltpu.VMEM((8 + rows, TILE), f32)],
        compiler_params=_params(("arbitrary", "arbitrary")),
    )(u, u, cw, cb)


def _conv_bwd(u, du, cw, cb, name):
    t_len = u.shape[1]
    rows = min(CONV_ROWS, t_len)
    nb = t_len // rows

    def body(u_ref, halo_ref, cw_ref, cb_ref, d_ref, o_ref, dcw_ref, dcb_ref, carry_s, win_s, dp_s):
        i = pl.program_id(1)
        ri = nb - 1 - i

        @pl.when(i == 0)
        def _():
            carry_s[...] = jnp.zeros_like(carry_s)
            dcw_ref[...] = jnp.zeros_like(dcw_ref)
            dcb_ref[...] = jnp.zeros_like(dcb_ref)
        win_s[0:8, :] = jnp.where(ri > 0, halo_ref[0], 0.0)
        win_s[8:8 + rows, :] = u_ref[0]
        w = [cw_ref[0, k:k + 1, :] for k in range(4)]
        b = cb_ref[0]
        dw = [jnp.zeros((1, TILE), f32)] * 4
        db = jnp.zeros((1, TILE), f32)
        for s in range(rows // CONV_SUB):
            o = CONV_SUB * s
            xk = [win_s[5 + k + o:5 + k + o + CONV_SUB, :] for k in range(4)]
            pre = b
            for k in range(4):
                pre = pre + w[k] * xk[k]
            sg = _sigmoid(pre)
            dpre = d_ref[0, o:o + CONV_SUB, :] * (sg * (1.0 + pre * (1.0 - sg)))
            dp_s[o:o + CONV_SUB, :] = dpre
            dw = [dw[k] + jnp.sum(dpre * xk[k], axis=0, keepdims=True) for k in range(4)]
            db = db + jnp.sum(dpre, axis=0, keepdims=True)
        dp_s[rows:rows + 8, :] = carry_s[...]
        for s in range(rows // CONV_SUB):
            o = CONV_SUB * s
            acc = w[0] * dp_s[3 + o:3 + o + CONV_SUB, :]
            for k in range(1, 4):
                acc = acc + w[k] * dp_s[3 - k + o:3 - k + o + CONV_SUB, :]
            o_ref[0, o:o + CONV_SUB, :] = acc
        carry_s[...] = dp_s[0:8, :]
        for k in range(4):
            dcw_ref[0, k:k + 1, :] += dw[k]
        dcb_ref[0] += db

    return _pc(
        body, name, grid=(16, nb),
        in_specs=_conv_specs(nb, rows, True) + [pl.BlockSpec((1, rows, TILE), lambda p, i: (8 + p, nb - 1 - i, 0))],
        out_specs=[pl.BlockSpec((1, rows, TILE), lambda p, i: (8 + p, nb - 1 - i, 0)),
                   pl.BlockSpec((1, 4, TILE), lambda p, i: (p, 0, 0)),
                   pl.BlockSpec((1, 1, TILE), lambda p, i: (p, 0, 0))],
        out_shape=[SDS(du.shape, f32), SDS((16, 4, TILE), f32), SDS((16, 1, TILE), f32)],
        input_output_aliases={4: 0},
        scratch_shapes=[pltpu.VMEM((8, TILE), f32), pltpu.VMEM((8 + rows, TILE), f32),
                        pltpu.VMEM((rows + 8, TILE), f32)],
        compiler_params=_params(("arbitrary", "arbitrary")),
    )(u, u, cw, cb, du)


def _collapse_matrix(g):
    r = lax.broadcasted_iota(jnp.int32, (SSM_HEADS, TILE), 0)
    c = lax.broadcasted_iota(jnp.int32, (SSM_HEADS, TILE), 1)
    return ((c // SSM_P) + 4 * g == r).astype(jnp.bfloat16)


def _ssd_prelude(dt_raw, dtb, alog, dsk, colx_s, scx_s):
    dtr, dt, a_neg, acs, acs_t = _dt_path(dt_raw, dtb, alog)
    a_end = acs[CHUNK - 1:CHUNK, :]
    lane = lax.broadcasted_iota(jnp.int32, (1, 2 * SSM_P), 1)
    lane4 = lax.broadcasted_iota(jnp.int32, (1, TILE), 1)
    sub8 = lax.broadcasted_iota(jnp.int32, (8, 1), 0)

    def row4(v, g):
        e = [v[:, 4 * g + r:4 * g + r + 1] for r in range(4)]
        return jnp.where(lane4 < 64, e[0], jnp.where(lane4 < 128, e[1], jnp.where(lane4 < 192, e[2], e[3])))

    for g in range(SSM_GROUPS):
        for k, arr in enumerate((dt, acs)):
            for half in range(2):
                h0 = 4 * g + 2 * half
                colx_s[k, g, :, 128 * half:128 * (half + 1)] = jnp.where(
                    lane < SSM_P, arr[:, h0:h0 + 1], arr[:, h0 + 1:h0 + 2])
        scx_s[g] = jnp.where(sub8 == 0, row4(dsk, g), jnp.where(sub8 == 1, row4(a_end, g), 0.0))
    return dtr, dt, a_neg, acs_t


def _ssm_core_specs(nc, rev):
    def cidx(c):
        return (nc - 1 - c) if rev else c
    return [
        pl.BlockSpec((SSM_KT, CHUNK, TILE), lambda c: (0, cidx(c), 0)),
        pl.BlockSpec((1, CHUNK, TILE), lambda c: (SSM_NT - 1, cidx(c), 0)),
        pl.BlockSpec((16, CHUNK, TILE), lambda c: (0, cidx(c), 0)),
        pl.BlockSpec((1, SSM_HEADS), lambda c: (0, 0)),
        pl.BlockSpec((1, SSM_HEADS), lambda c: (0, 0)),
        pl.BlockSpec((1, SSM_HEADS), lambda c: (0, 0)),
        pl.BlockSpec((SSM_KT, 1, TILE), lambda c: (0, 0, 0)),
    ]


def _stack_cols_rows(acx, rows):
    ac = jnp.concatenate([acx[:, SSM_P * r:SSM_P * r + 1] for r in range(4)], axis=0)
    ar = jnp.concatenate([jnp.broadcast_to(rows[r:r + 1, :], (CHUNK, CHUNK)) for r in range(4)], axis=0)
    return ac, ar


def _ssm2_fwd(u, xbc, dtb, alog, dsk, gn, name):
    t_len = u.shape[1]
    nc = t_len // CHUNK

    def body(z_ref, dt_ref, x_ref, dtb_ref, alog_ref, dsk_ref, gn_ref, a3_ref, yp_ref, hst_ref,
             h_s, colx_s, scx_s, xt_s, yd_s):
        c = pl.program_id(0)

        @pl.when(c == 0)
        def _():
            h_s[...] = jnp.zeros_like(h_s)
        _, _, _, acs_t = _ssd_prelude(dt_ref[0, :, 0:SSM_HEADS], dtb_ref[...], alog_ref[...], dsk_ref[...],
                                      colx_s, scx_s)
        causal = _tri4()

        def group(g, s1):
            xs = x_ref[g]
            bm, cm = x_ref[8 + g, :, 0:SSM_N], x_ref[8 + g, :, SSM_N:2 * SSM_N]
            cb = _dot_nt(cm, bm)
            sc = scx_s[g]
            a_end = sc[1:2, :]
            rows = pltpu.roll(acs_t, (SSM_HEADS - 4 * g) % SSM_HEADS, 0)
            hp = h_s[g]
            xt = xs * colx_s[0, g]
            xt_s[...] = xt
            acx = colx_s[1, g]
            ac_st, ar_st = _stack_cols_rows(acx, rows)
            m = jnp.concatenate([cb] * 4, axis=0) * jnp.exp(jnp.where(causal, ac_st - ar_st, -jnp.inf))
            for r in range(4):
                hd = slice(SSM_P * r, SSM_P * (r + 1))
                yd_s[:, hd] = _dot(m[CHUNK * r:CHUNK * (r + 1), :], xt_s[:, hd])
            yp_ref[g] = yd_s[...] + _dot(cm, hp) * jnp.exp(acx) + sc[0:1, :] * xs
            hst_ref[0, g] = hp
            h_s[g] = hp * jnp.exp(a_end) + _dot_tn(bm, xt * jnp.exp(a_end - acx))
            y2 = yp_ref[g] * _silu(z_ref[g])
            return s1 + jnp.sum(y2 * y2, axis=1, keepdims=True)

        s1 = lax.fori_loop(0, SSM_GROUPS // 2, lambda i, c: group(2 * i + 1, group(2 * i, c)),
                           jnp.zeros((CHUNK, 1), f32))
        rinv = lax.rsqrt(s1 * (1.0 / SSM_INNER) + EPS)

        def gate(g, carry):
            y2 = yp_ref[g] * _silu(z_ref[g])
            a3_ref[g] = (y2 * rinv * gn_ref[g]).astype(a3_ref.dtype)
            return carry

        lax.fori_loop(0, SSM_GROUPS, gate, 0)

    return _pc(
        body, name, grid=(nc,),
        in_specs=_ssm_core_specs(nc, False),
        out_specs=[pl.BlockSpec((SSM_KT, CHUNK, TILE), lambda c: (0, c, 0)),
                   pl.BlockSpec((SSM_KT, CHUNK, TILE), lambda c: (0, c, 0)),
                   pl.BlockSpec((1, SSM_GROUPS, SSM_N, TILE), lambda c: (c, 0, 0, 0))],
        out_shape=[SDS((SSM_KT, t_len, TILE), MXU_DTYPE), SDS((SSM_KT, t_len, TILE), f32),
                   SDS((nc, SSM_GROUPS, SSM_N, TILE), f32)],
        scratch_shapes=[pltpu.VMEM((SSM_GROUPS, SSM_N, TILE), f32), pltpu.VMEM((2, SSM_GROUPS, CHUNK, TILE), f32),
                        pltpu.VMEM((SSM_GROUPS, 8, TILE), f32), pltpu.VMEM((CHUNK, TILE), f32),
                        pltpu.VMEM((CHUNK, TILE), f32)],
        compiler_params=_params(("arbitrary",)),
    )(u, u, xbc, dtb, alog, dsk, gn)


def _ssm2_bwd(u, xbc, yp, hst, da3, dtb, alog, dsk, gn, name):
    t_len = u.shape[1]
    nc = t_len // CHUNK

    def body(z_ref, dt_ref, x_ref, dtb_ref, alog_ref, dsk_ref, gn_ref, yp_ref, hst_ref, da3_ref,
             du_ref, ddtb_ref, dalog_ref, ddsk_ref, dgn_ref,
             dh_s, colx_s, scx_s, xt_s, dy_s, dxt_s, ddtx_s, dacx_s, ddx_s, drow_s, dm_s, dmt_s):
        step = pl.program_id(0)

        @pl.when(step == 0)
        def _():
            dh_s[...] = jnp.zeros_like(dh_s)
            ddtb_ref[...] = jnp.zeros_like(ddtb_ref)
            dalog_ref[...] = jnp.zeros_like(dalog_ref)
            ddsk_ref[...] = jnp.zeros_like(ddsk_ref)
            dgn_ref[...] = jnp.zeros_like(dgn_ref)
        dtr, dt, a_neg, acs_t = _ssd_prelude(dt_ref[0, :, 0:SSM_HEADS], dtb_ref[...], alog_ref[...], dsk_ref[...],
                                             colx_s, scx_s)
        causal = _tri4()
        causal_t = (lax.broadcasted_iota(jnp.int32, (ATT_ROWS, CHUNK), 1)
                    >= lax.broadcasted_iota(jnp.int32, (ATT_ROWS, CHUNK), 0) % CHUNK)
        last = (lax.broadcasted_iota(jnp.int32, (1, CHUNK), 1) == CHUNK - 1).astype(f32)
        lane = lax.broadcasted_iota(jnp.int32, (1, TILE), 1)
        sub32 = lax.broadcasted_iota(jnp.int32, (SSM_HEADS, 1), 0)
        drow_s[...] = jnp.zeros_like(drow_s)

        def sums(g, carry):
            s1, s2 = carry
            y2 = yp_ref[g] * _silu(z_ref[g])
            g3 = da3_ref[g] * gn_ref[g]
            return (s1 + jnp.sum(y2 * y2, axis=1, keepdims=True), s2 + jnp.sum(g3 * y2, axis=1, keepdims=True))

        zcol = jnp.zeros((CHUNK, 1), f32)
        carry = (zcol, zcol)
        for g in range(SSM_GROUPS):
            carry = sums(g, carry)
        s1, s2 = carry
        rinv = lax.rsqrt(s1 * (1.0 / SSM_INNER) + EPS)
        m2 = s2 * rinv * rinv * rinv * (1.0 / SSM_INNER)

        def group(g, carry):
            z = z_ref[g]
            sg = _sigmoid(z)
            sz = z * sg
            y = yp_ref[g]
            y2 = y * sz
            da3 = da3_ref[g]
            dgn_ref[g] += jnp.sum(da3 * y2 * rinv, axis=0, keepdims=True)
            dy2 = rinv * (da3 * gn_ref[g]) - y2 * m2
            dy = dy2 * sz
            dy_s[...] = dy
            du_ref[g] = dy2 * y * (sg * (1.0 + z * (1.0 - sg)))
            xs = x_ref[g]
            bm, cm = x_ref[8 + g, :, 0:SSM_N], x_ref[8 + g, :, SSM_N:2 * SSM_N]
            cb = _dot_nt(cm, bm)
            cbt = _dot_nt(bm, cm)
            dtx, acx = colx_s[0, g], colx_s[1, g]
            sc = scx_s[g]
            a_end = sc[1:2, :]
            ex = jnp.exp(acx)
            wdx = jnp.exp(a_end - acx)
            eend = jnp.exp(a_end)
            rows = pltpu.roll(acs_t, (SSM_HEADS - 4 * g) % SSM_HEADS, 0)
            hp = hst_ref[0, g]
            dhn = dh_s[g]
            xt = xs * dtx
            xt_s[...] = xt
            ch = _dot(cm, hp)
            gy = dy * ex
            dcm = _dot_nt(gy, hp)
            dh_s[g] = _dot_tn(cm, gy) + dhn * eend
            q = _dot(bm, dhn)
            dbm = _dot_nt(xt * wdx, dhn)
            qx = q * xt * wdx
            v_end = jnp.sum(dhn * hp, axis=0, keepdims=True) * eend + jnp.sum(qx, axis=0, keepdims=True)
            ac_st, ar_st = _stack_cols_rows(acx, rows)
            lam = jnp.exp(jnp.where(causal, ac_st - ar_st, -jnp.inf))
            lam_t = jnp.exp(jnp.where(causal_t, ar_st - ac_st, -jnp.inf))
            m = jnp.concatenate([cb] * 4, axis=0) * lam
            m_t = jnp.concatenate([cbt] * 4, axis=0) * lam_t
            for r in range(4):
                hd = slice(SSM_P * r, SSM_P * (r + 1))
                rs = slice(CHUNK * r, CHUNK * (r + 1))
                dm_s[rs, :] = _dot_nt(dy_s[:, hd], xt_s[:, hd])
                dmt_s[rs, :] = _dot_nt(xt_s[:, hd], dy_s[:, hd])
                dxt_s[:, hd] = _dot(m_t[rs, :], dy_s[:, hd])
            dm = dm_s[...]
            dl = dm * lam
            dseg = dm * m
            dseg_t = dmt_s[...] * m_t
            dcb = dl[0:CHUNK] + dl[CHUNK:2 * CHUNK] + dl[2 * CHUNK:3 * CHUNK] + dl[3 * CHUNK:4 * CHUNK]
            drows = jnp.zeros((SSM_HEADS, CHUNK), f32)
            for r in range(4):
                rs = slice(CHUNK * r, CHUNK * (r + 1))
                in_head = (lane >= SSM_P * r) & (lane < SSM_P * (r + 1))
                d_ac = jnp.sum(dseg_t[rs, :], axis=0, keepdims=True)
                d_ar = jnp.sum(dseg[rs, :], axis=0, keepdims=True)
                d_aend = jnp.sum(jnp.where(in_head, v_end, 0.0), axis=1, keepdims=True)
                drows = drows + jnp.where(sub32 == r, d_ac - d_ar + last * d_aend, 0.0)
            dxt = dxt_s[...] + q * wdx
            du_ref[8 + g] = sc[0:1, :] * dy + dxt * dtx
            ddtx_s[g] = dxt * xs
            dacx_s[g] = dy * ch * ex - qx
            ddx_s[g] = jnp.broadcast_to(jnp.sum(dy * xs, axis=0, keepdims=True), (8, TILE))
            du_ref[16 + g, :, 0:SSM_N] = dbm + _dot_tn(dcb, cm)
            du_ref[16 + g, :, SSM_N:2 * SSM_N] = dcm + _dot(dcb, bm)
            drow_s[...] += pltpu.roll(drows, (4 * g) % SSM_HEADS, 0)
            return carry

        lax.fori_loop(0, SSM_GROUPS, group, 0, unroll=2)
        ddt = jnp.zeros((CHUNK, SSM_HEADS), f32)
        dacs = jnp.zeros((CHUNK, SSM_HEADS), f32)
        ddsk = jnp.zeros((8, SSM_HEADS), f32)
        for g in range(SSM_GROUPS):
            col_g = _collapse_matrix(g)
            ddt = ddt + _dot_exact(ddtx_s[g], col_g, ((1,), (1,)), True, 2)
            dacs = dacs + _dot_exact(dacx_s[g], col_g, ((1,), (1,)), True, 2)
            ddsk = ddsk + _dot_exact(ddx_s[g], col_g, ((1,), (1,)), True, 2)
        upper = _tri(False)
        da = _dot_exact(upper, dacs, ((1,), (0,)), False) + _dot_exact(upper, drow_s[...], ((1,), (1,)), False)
        ddt = ddt + da * a_neg
        dalog_ref[...] += jnp.sum(da * dt, axis=0, keepdims=True) * a_neg
        ddtr = ddt * _sigmoid(dtr)
        ddtb_ref[...] += jnp.sum(ddtr, axis=0, keepdims=True)
        ddsk_ref[...] += ddsk[0:1, :]
        du_ref[SSM_NT - 1] = jnp.zeros((CHUNK, TILE), f32)
        du_ref[SSM_NT - 1, :, 0:SSM_HEADS] = ddtr

    def rc(c):
        return nc - 1 - c

    vec = pl.BlockSpec((1, SSM_HEADS), lambda c: (0, 0))
    return _pc(
        body, name, grid=(nc,),
        in_specs=_ssm_core_specs(nc, True) + [
            pl.BlockSpec((SSM_KT, CHUNK, TILE), lambda c: (0, rc(c), 0)),
            pl.BlockSpec((1, SSM_GROUPS, SSM_N, TILE), lambda c: (rc(c), 0, 0, 0)),
            pl.BlockSpec((SSM_KT, CHUNK, TILE), lambda c: (0, rc(c), 0))],
        out_specs=[pl.BlockSpec((SSM_NT, CHUNK, TILE), lambda c: (0, rc(c), 0)), vec, vec, vec,
                   pl.BlockSpec((SSM_KT, 1, TILE), lambda c: (0, 0, 0))],
        out_shape=[SDS((SSM_NT, t_len, TILE), f32), SDS((1, SSM_HEADS), f32), SDS((1, SSM_HEADS), f32),
                   SDS((1, SSM_HEADS), f32), SDS((SSM_KT, 1, TILE), f32)],
        scratch_shapes=[pltpu.VMEM((SSM_GROUPS, SSM_N, TILE), f32), pltpu.VMEM((2, SSM_GROUPS, CHUNK, TILE), f32),
                        pltpu.VMEM((SSM_GROUPS, 8, TILE), f32), pltpu.VMEM((CHUNK, TILE), f32),
                        pltpu.VMEM((CHUNK, TILE), f32), pltpu.VMEM((CHUNK, TILE), f32),
                        pltpu.VMEM((SSM_GROUPS, CHUNK, TILE), f32), pltpu.VMEM((SSM_GROUPS, CHUNK, TILE), f32),
                        pltpu.VMEM((SSM_GROUPS, 8, TILE), f32), pltpu.VMEM((SSM_HEADS, CHUNK), f32),
                        pltpu.VMEM((4 * CHUNK, CHUNK), f32), pltpu.VMEM((4 * CHUNK, CHUNK), f32)],
        compiler_params=_params(("arbitrary",)),
    )(u, u, xbc, dtb, alog, dsk, gn, yp, hst, da3)


def _swap16(t):
    lane = lax.broadcasted_iota(jnp.int32, t.shape, 1) % 64
    return jnp.where(lane < 8, pltpu.roll(t, TILE - 8, 1), jnp.where(lane < 16, pltpu.roll(t, 8, 1), 0.0))


def _rope(t, cos_t, sin_t):
    return t * cos_t + _swap16(t) * sin_t


def _rope_bwd(g, cos_t, sin_t):
    return g * cos_t + _swap16(g * sin_t)


def _att_in_specs(nb, rev):
    def bidx(n):
        return (nb - 1 - n) if rev else n
    return [
        pl.BlockSpec((ATT_NT, CHUNK, TILE), lambda n: (0, bidx(n), 0)),
        pl.BlockSpec((2, CHUNK, TILE), lambda n: (2, jnp.maximum(bidx(n) - 1, 0), 0)),
        pl.BlockSpec((CHUNK, TILE), lambda n: (bidx(n), 0)),
        pl.BlockSpec((CHUNK, TILE), lambda n: (bidx(n), 0)),
        pl.BlockSpec((CHUNK, TILE), lambda n: (jnp.maximum(bidx(n) - 1, 0), 0)),
        pl.BlockSpec((CHUNK, TILE), lambda n: (jnp.maximum(bidx(n) - 1, 0), 0)),
        pl.BlockSpec((1, 16), lambda n: (0, 0)),
    ]


ATT_SCALE = 0.125


def _tri4():
    row = lax.broadcasted_iota(jnp.int32, (ATT_ROWS, CHUNK), 0) % CHUNK
    col = lax.broadcasted_iota(jnp.int32, (ATT_ROWS, CHUNK), 1)
    return col <= row


def _stack_heads(ref):
    return jnp.concatenate([ref[:, 64 * r:64 * (r + 1)] for r in range(4)], axis=0)


def _sink_col(sinks, g):
    return [sinks[:, 4 * g + r:4 * g + r + 1] for r in range(4)]


def _softmax_rows(s_s, pn_s, pc_s, sink, tri, has_prev):
    sub = lax.broadcasted_iota(jnp.int32, (ATT_ROWS, 1), 0)
    sk = jnp.where(sub < CHUNK, sink[0], jnp.where(sub < 2 * CHUNK, sink[1], jnp.where(sub < 3 * CHUNK, sink[2], sink[3])))
    s = jnp.where(tri, s_s[:, CHUNK:2 * CHUNK], jnp.where(has_prev, s_s[:, 0:CHUNK], -jnp.inf)) * ATT_SCALE
    m = jnp.maximum(jnp.max(s, axis=-1, keepdims=True), sk)
    p = jnp.exp(s - m)
    e_sink = jnp.exp(sk - m)
    inv = 1.0 / (jnp.sum(p, axis=-1, keepdims=True) + e_sink)
    pn = p * inv
    pc_s[...] = pn
    pn_s[:, 0:CHUNK] = jnp.where(tri, 0.0, pn)
    pn_s[:, CHUNK:2 * CHUNK] = jnp.where(tri, pn, 0.0)
    return e_sink * inv


def _att2_fwd(u, cos_t, sin_t, sinks, name):
    t_len = u.shape[1]
    nb = t_len // CHUNK

    def body(u_ref, prev_ref, cc_ref, sc_ref, cp_ref, sp_ref, sink_ref, a_ref,
             q_s, kp_s, kc_s, vp_s, vc_s, o_s, s_s, pn_s, pc_s):
        n = pl.program_id(0)
        tri = _tri4()
        cos_c, sin_c = cc_ref[...], sc_ref[...]
        kc_s[...] = _rope(u_ref[4], cos_c, sin_c)
        kp_s[...] = _rope(prev_ref[0], cp_ref[...], sp_ref[...])
        vc_s[...] = u_ref[5]
        vp_s[...] = prev_ref[1]
        sinks = sink_ref[...]
        for g in range(4):
            q_g, o_g, s_g, pn_g = q_s.at[g], o_s.at[g], s_s.at[g], pn_s.at[g]
            q_g[...] = _rope(u_ref[g], cos_c, sin_c)
            kv = slice(64 * g, 64 * (g + 1))
            kb = jnp.concatenate([kp_s[:, kv], kc_s[:, kv]], axis=0)
            vb = jnp.concatenate([vp_s[:, kv], vc_s[:, kv]], axis=0)
            s_g[...] = _dot_nt(_stack_heads(q_g), kb)
            _softmax_rows(s_g, pn_g, pc_s.at[g], _sink_col(sinks, g), tri, n > 0)
            o = _dot(pn_g[...], vb)
            for r in range(4):
                o_g[:, 64 * r:64 * (r + 1)] = o[CHUNK * r:CHUNK * (r + 1), :]
            a_ref[g] = (o_g[...] * _silu(u_ref[6 + g])).astype(a_ref.dtype)

    return _pc(
        body, name, grid=(nb,),
        in_specs=_att_in_specs(nb, False),
        out_specs=pl.BlockSpec((ATT_KT, CHUNK, TILE), lambda n: (0, n, 0)),
        out_shape=SDS((ATT_KT, t_len, TILE), MXU_DTYPE),
        scratch_shapes=[pltpu.VMEM((4, CHUNK, TILE), f32)] + [pltpu.VMEM((CHUNK, TILE), f32)] * 4
                       + [pltpu.VMEM((4, CHUNK, TILE), f32)] + [pltpu.VMEM((4, ATT_ROWS, 2 * CHUNK), f32)] * 2
                       + [pltpu.VMEM((4, ATT_ROWS, CHUNK), f32)],
        compiler_params=_params(("arbitrary",)),
    )(u, u, cos_t, sin_t, cos_t, sin_t, sinks)


def _att2_bwd(u, cos_t, sin_t, sinks, da, name):
    t_len = u.shape[1]
    nb = t_len // CHUNK

    def body(u_ref, prev_ref, cc_ref, sc_ref, cp_ref, sp_ref, sink_ref, da_ref, du_ref, dsink_ref,
             ck_s, cv_s, kp_s, kc_s, vp_s, vc_s, q_s, o_s, do_s, dq_s, s_s, pn_s, dp_s, dkt_s, dvt_s, pc_s):
        step = pl.program_id(0)
        nn = nb - 1 - step

        @pl.when(step == 0)
        def _():
            ck_s[...] = jnp.zeros_like(ck_s)
            cv_s[...] = jnp.zeros_like(cv_s)
            dsink_ref[...] = jnp.zeros_like(dsink_ref)
        tri = _tri4()
        cos_c, sin_c = cc_ref[...], sc_ref[...]
        cos_p, sin_p = cp_ref[...], sp_ref[...]
        kc_s[...] = _rope(u_ref[4], cos_c, sin_c)
        kp_s[...] = _rope(prev_ref[0], cos_p, sin_p)
        vc_s[...] = u_ref[5]
        vp_s[...] = prev_ref[1]
        sinks = sink_ref[...]
        lane16 = lax.broadcasted_iota(jnp.int32, (1, 16), 1)
        dsink = jnp.zeros((1, 16), f32)
        for g in range(4):
            q_g, o_g, do_g, dq_g = q_s.at[g], o_s.at[g], do_s.at[g], dq_s.at[g]
            s_g, pn_g, dp_g, pc_g = s_s.at[g], pn_s.at[g], dp_s.at[g], pc_s.at[g]
            q_g[...] = _rope(u_ref[g], cos_c, sin_c)
            gate = u_ref[6 + g]
            sg = _sigmoid(gate)
            dav = da_ref[g]
            do_g[...] = dav * (gate * sg)
            kv = slice(64 * g, 64 * (g + 1))
            kb = jnp.concatenate([kp_s[:, kv], kc_s[:, kv]], axis=0)
            vb = jnp.concatenate([vp_s[:, kv], vc_s[:, kv]], axis=0)
            q_st = _stack_heads(q_g)
            do_st = _stack_heads(do_g)
            s_g[...] = _dot_nt(q_st, kb)
            p_sink = _softmax_rows(s_g, pn_g, pc_g, _sink_col(sinks, g), tri, nn > 0)
            o = _dot(pn_g[...], vb)
            dvt_s[64 * g:64 * (g + 1), :] = _dot_tn(do_st, pn_g[...])
            dp_g[...] = _dot_nt(do_st, vb)
            delta = jnp.sum(do_st * o, axis=-1, keepdims=True)
            dpc = jnp.where(tri, dp_g[:, CHUNK:2 * CHUNK], dp_g[:, 0:CHUNK])
            dsc = pc_g[...] * (dpc - delta) * ATT_SCALE
            dp_g[:, 0:CHUNK] = jnp.where(tri, 0.0, dsc)
            dp_g[:, CHUNK:2 * CHUNK] = jnp.where(tri, dsc, 0.0)
            sd = p_sink * delta
            for r in range(4):
                rs = slice(CHUNK * r, CHUNK * (r + 1))
                o_g[:, 64 * r:64 * (r + 1)] = o[rs, :]
                ds_h = -jnp.sum(sd[rs, :], axis=0, keepdims=True)
                dsink = dsink + ds_h * (lane16 == 4 * g + r).astype(f32)
            ds = dp_g[...]
            dq = _dot(ds, kb)
            for r in range(4):
                dq_g[:, 64 * r:64 * (r + 1)] = dq[CHUNK * r:CHUNK * (r + 1), :]
            dkt_s[64 * g:64 * (g + 1), :] = _dot_tn(q_st, ds)
            du_ref[6 + g] = dav * o_g[...] * (sg * (1.0 + gate * (1.0 - sg)))
            du_ref[g] = _rope_bwd(dq_g[...], cos_c, sin_c)
        dk = dkt_s[...].T
        dv = dvt_s[...].T
        du_ref[4] = _rope_bwd(dk[CHUNK:2 * CHUNK, :], cos_c, sin_c) + ck_s[...]
        du_ref[5] = dv[CHUNK:2 * CHUNK, :] + cv_s[...]
        ck_s[...] = _rope_bwd(dk[0:CHUNK, :], cos_p, sin_p)
        cv_s[...] = dv[0:CHUNK, :]
        dsink_ref[...] += dsink

    def rb(n):
        return nb - 1 - n

    return _pc(
        body, name, grid=(nb,),
        in_specs=_att_in_specs(nb, True) + [pl.BlockSpec((ATT_KT, CHUNK, TILE), lambda n: (0, rb(n), 0))],
        out_specs=[pl.BlockSpec((ATT_NT, CHUNK, TILE), lambda n: (0, rb(n), 0)),
                   pl.BlockSpec((1, 16), lambda n: (0, 0))],
        out_shape=[SDS((ATT_NT, t_len, TILE), f32), SDS((1, 16), f32)],
        scratch_shapes=[pltpu.VMEM((CHUNK, TILE), f32)] * 6 + [pltpu.VMEM((4, CHUNK, TILE), f32)] * 4
                       + [pltpu.VMEM((4, ATT_ROWS, 2 * CHUNK), f32)] * 3
                       + [pltpu.VMEM((2 * CHUNK, 2 * CHUNK), f32)] * 2 + [pltpu.VMEM((4, ATT_ROWS, CHUNK), f32)],
        compiler_params=_params(("arbitrary",)),
    )(u, u, cos_t, sin_t, cos_t, sin_t, sinks, da)


_HBM = pl.BlockSpec(memory_space=pltpu.HBM)


def _all_gather_big(shards):
    n = len(shards)

    def body(*refs):
        x_refs, out_refs = refs[:n], refs[n:2 * n]
        send_sems, recv_sems, local_sems = refs[2 * n:]
        x, y, c = lax.axis_index("x"), lax.axis_index("y"), lax.axis_index("c")
        me, sibling = (x, y, c), (x, y, 1 - c)
        chips = [(1 - x, y), (x, 1 - y), (1 - x, 1 - y)]

        def slot(i, px, py, pc):
            return out_refs[i].at[4 * px + 2 * py + pc]

        def copy(i, k, block, to, src=None):
            return pltpu.make_async_remote_copy(
                src_ref=slot(i, *block) if src is None else src, dst_ref=slot(i, *block),
                send_sem=send_sems.at[7 * i + k], recv_sem=recv_sems.at[7 * i + k], device_id=to, device_id_type=MESH)

        mine = [pltpu.make_async_copy(x_refs[i], slot(i, *me), local_sems.at[i]) for i in range(n)]
        for cp in mine:
            cp.start()
        first = []
        for i in range(n):
            first.append(copy(i, 0, me, sibling, src=x_refs[i]))
            first += [copy(i, 1 + j, me, (*chip, c), src=x_refs[i]) for j, chip in enumerate(chips)]
        for cp in first:
            cp.start()
        passed = []
        for j, chip in enumerate(chips):
            for i in range(n):
                copy(i, 1 + j, (*chip, c), me).wait_recv()
                fwd = copy(i, 4 + j, (*chip, c), sibling)
                fwd.start()
                passed.append(fwd)
        for i in range(n):
            copy(i, 0, sibling, me).wait_recv()
            for j, chip in enumerate(chips):
                copy(i, 4 + j, (*chip, 1 - c), me).wait_recv()
        for cp in first + passed:
            cp.wait_send()
        for cp in mine:
            cp.wait()

    return _pc(
        body, "all_gather_big",
        in_specs=[_HBM] * n, out_specs=[_HBM] * n,
        out_shape=[SDS((N_DEV,) + s.shape, s.dtype) for s in shards],
        scratch_shapes=[pltpu.SemaphoreType.DMA((7 * n,)), pltpu.SemaphoreType.DMA((7 * n,)),
                        pltpu.SemaphoreType.DMA((n,))],
    )(*shards)


def _all_gather_direct(block, name):
    rows, width = block.shape

    def body(x_ref, out_ref, send_sems, recv_sems, local_sem):
        x, y, c = lax.axis_index("x"), lax.axis_index("y"), lax.axis_index("c")
        my_slot = 4 * x + 2 * y + c

        def peer(k):
            return (1 - x if k & 4 else x, 1 - y if k & 2 else y, 1 - c if k & 1 else c)

        def copy(k):
            px, py, pc = peer(k)
            return pltpu.make_async_remote_copy(
                src_ref=x_ref, dst_ref=out_ref.at[my_slot], send_sem=send_sems.at[k - 1], recv_sem=recv_sems.at[k - 1],
                device_id=(px, py, pc), device_id_type=MESH)

        def arrival(k):
            px, py, pc = peer(k)
            return pltpu.make_async_remote_copy(
                src_ref=x_ref, dst_ref=out_ref.at[4 * px + 2 * py + pc], send_sem=send_sems.at[k - 1],
                recv_sem=recv_sems.at[k - 1], device_id=(px, py, pc), device_id_type=MESH)

        mine = pltpu.make_async_copy(x_ref, out_ref.at[my_slot], local_sem)
        mine.start()
        for k in range(1, N_DEV):
            copy(k).start()
        for k in range(1, N_DEV):
            arrival(k).wait_recv()
        for k in range(1, N_DEV):
            copy(k).wait_send()
        mine.wait()

    return _pc(
        body, name,
        in_specs=[_HBM], out_specs=_HBM,
        out_shape=SDS((N_DEV, rows, width), block.dtype),
        scratch_shapes=[pltpu.SemaphoreType.DMA((7,)), pltpu.SemaphoreType.DMA((7,)), pltpu.SemaphoreType.DMA],
    )(block)


N_CHIP = N_DEV // 2


def _exchange_sibling(gs):
    n = len(gs)

    def body(*refs):
        g_refs, out_refs = refs[:n], refs[n:2 * n]
        send_sems, recv_sems = refs[2 * n:]
        x, y, c = lax.axis_index("x"), lax.axis_index("y"), lax.axis_index("c")
        cps = [pltpu.make_async_remote_copy(
            src_ref=g_refs[i].at[2 * k + 1 - c], dst_ref=out_refs[i].at[k], send_sem=send_sems.at[N_CHIP * i + k],
            recv_sem=recv_sems.at[N_CHIP * i + k], device_id=(x, y, 1 - c), device_id_type=MESH)
            for i in range(n) for k in range(N_CHIP)]
        for cp in cps:
            cp.start()
        for cp in cps:
            cp.wait()

    return _pc(
        body, "rs_sibling",
        in_specs=[_HBM] * n, out_specs=[_HBM] * n,
        out_shape=[SDS((N_CHIP,) + g.shape[1:], g.dtype) for g in gs],
        scratch_shapes=[pltpu.SemaphoreType.DMA((N_CHIP * n,)), pltpu.SemaphoreType.DMA((N_CHIP * n,))],
    )(*gs)


def _pair_sum(g, r1, cidx, tr, name):
    _, rows, width = g.shape

    def body(c_ref, g_ref, r_ref, o_ref):
        o_ref[...] = (g_ref[...].astype(f32) + r_ref[...].astype(f32)).astype(o_ref.dtype)

    return pl.pallas_call(
        body, name=name,
        grid_spec=pltpu.PrefetchScalarGridSpec(
            num_scalar_prefetch=1, grid=(N_CHIP, rows // tr),
            in_specs=[pl.BlockSpec((1, tr, width), lambda k, i, c_ref: (2 * k + c_ref[0], i, 0)),
                      pl.BlockSpec((1, tr, width), lambda k, i, c_ref: (k, i, 0))],
            out_specs=pl.BlockSpec((1, tr, width), lambda k, i, c_ref: (k, i, 0))),
        out_shape=SDS((N_CHIP, rows, width), g.dtype),
        compiler_params=_params(("arbitrary", "arbitrary")),
    )(cidx, g, r1)


def _exchange_chips(ps):
    n = len(ps)

    def body(*refs):
        p_refs, out_refs = refs[:n], refs[n:2 * n]
        send_sems, recv_sems, local_sems = refs[2 * n:]
        x, y, c = lax.axis_index("x"), lax.axis_index("y"), lax.axis_index("c")
        my_chip = 2 * x + y
        chips = [(1 - x, y), (x, 1 - y), (1 - x, 1 - y)]

        def copy(i, j):
            px, py = chips[j]
            return pltpu.make_async_remote_copy(
                src_ref=p_refs[i].at[2 * px + py], dst_ref=out_refs[i].at[my_chip], send_sem=send_sems.at[3 * i + j],
                recv_sem=recv_sems.at[3 * i + j], device_id=(px, py, c), device_id_type=MESH)

        def arrival(i, j):
            px, py = chips[j]
            return pltpu.make_async_remote_copy(
                src_ref=p_refs[i].at[my_chip], dst_ref=out_refs[i].at[2 * px + py], send_sem=send_sems.at[3 * i + j],
                recv_sem=recv_sems.at[3 * i + j], device_id=(px, py, c), device_id_type=MESH)

        mine = [pltpu.make_async_copy(p_refs[i].at[my_chip], out_refs[i].at[my_chip], local_sems.at[i])
                for i in range(n)]
        for cp in mine:
            cp.start()
        for i in range(n):
            for j in range(3):
                copy(i, j).start()
        for i in range(n):
            for j in range(3):
                arrival(i, j).wait_recv()
        for i in range(n):
            for j in range(3):
                copy(i, j).wait_send()
        for cp in mine:
            cp.wait()

    return _pc(
        body, "rs_chips",
        in_specs=[_HBM] * n, out_specs=[_HBM] * n,
        out_shape=[SDS(p.shape, p.dtype) for p in ps],
        scratch_shapes=[pltpu.SemaphoreType.DMA((3 * n,)), pltpu.SemaphoreType.DMA((3 * n,)),
                        pltpu.SemaphoreType.DMA((n,))],
    )(*ps)


def _adamw(parts, w, m, v, tr, name):
    n, rows, width = parts.shape
    c1 = 1.0 / (1.0 - ADAM_B1 ** ADAM_STEP)
    c2 = 1.0 / (1.0 - ADAM_B2 ** ADAM_STEP)

    def body(p_ref, w_ref, m_ref, v_ref, g_ref, d_ref, mo_ref, vo_ref):
        g = p_ref[0].astype(f32)
        for k in range(1, n):
            g = g + p_ref[k].astype(f32)
        mn = ADAM_B1 * m_ref[...] + (1.0 - ADAM_B1) * g
        vn = ADAM_B2 * v_ref[...] + (1.0 - ADAM_B2) * (g * g)
        g_ref[...] = g
        mo_ref[...] = mn
        vo_ref[...] = vn
        d_ref[...] = -ADAM_LR * ((mn * c1) / (jnp.sqrt(vn * c2) + ADAM_EPS) + ADAM_WD * w_ref[...])

    blk = pl.BlockSpec((tr, width), lambda i: (i, 0))
    return _pc(
        body, name, grid=(rows // tr,),
        in_specs=[pl.BlockSpec((n, tr, width), lambda i: (0, i, 0)), blk, blk, blk],
        out_specs=[blk, blk, blk, blk],
        out_shape=[SDS((rows, width), f32)] * 4,
        compiler_params=_params(("arbitrary",)),
    )(parts, w, m, v)


ROWS_REST = ROWS_SSM_OUT + ROWS_ATT_OUT + 16


def _pack_rest(ssm_w_out, att_w_out, conv_w):
    conv = jnp.pad(conv_w.reshape(4, 1024), ((0, 12), (0, 0)))
    return jnp.concatenate([ssm_w_out.reshape(ROWS_SSM_OUT, 1024), att_w_out.reshape(ROWS_ATT_OUT, 1024), conv], axis=0)


def _unpack_rest(p):
    o = ROWS_SSM_OUT + ROWS_ATT_OUT
    return (p[0:ROWS_SSM_OUT].reshape(2, 256, 1024), p[ROWS_SSM_OUT:o].reshape(2, 128, 1024),
            p[o:o + 4].reshape(2, 4, 512))


def _pack_grads(d_ssm_w_in, d_ssm_w_out, d_att_w_in, d_att_w_out, d_conv_w):
    wire = lambda t: t.astype(MXU_DTYPE)
    a = jnp.transpose(wire(d_ssm_w_in).reshape(2, 1024, 8, 772), (2, 0, 1, 3)).reshape(8, 2048, 772)
    c = jnp.transpose(wire(d_att_w_in).reshape(2, 1024, 8, 320), (2, 0, 1, 3)).reshape(8, 2048, 320)
    b = jnp.transpose(wire(d_ssm_w_out).reshape(2, 8, 256, 1024), (1, 0, 2, 3)).reshape(8, ROWS_SSM_OUT, 1024)
    d = jnp.transpose(wire(d_att_w_out).reshape(2, 8, 128, 1024), (1, 0, 2, 3)).reshape(8, ROWS_ATT_OUT, 1024)
    e = jnp.transpose(wire(d_conv_w).reshape(2, 4, 8, 512), (2, 0, 1, 3)).reshape(8, 4, 1024)
    e = jnp.pad(e, ((0, 0), (0, 12), (0, 0)))
    return a, c, jnp.concatenate([b, d, e], axis=1)


def _pad8(a):
    return jnp.pad(a, ((0, 8 - a.shape[0]), (0, 0)))


def _pack_small(pre_norm, post_norm, conv_b, gate_norm, dt_bias, a_log, d_skip, sinks, extra=None):
    row = jnp.concatenate([dt_bias.reshape(1, 64), a_log.reshape(1, 64), d_skip.reshape(1, 64), sinks.reshape(1, 32),
                           jnp.zeros((1, 1024 - 224), f32)], axis=1)
    if extra is not None:
        row = row + jnp.pad(extra.reshape(1, 1), ((0, 0), (224, 1024 - 225)))
    return jnp.concatenate([_pad8(pre_norm.reshape(4, 1024)), _pad8(post_norm.reshape(4, 1024)),
                            conv_b.reshape(8, 1024), _pad8(gate_norm.reshape(4, 1024)), _pad8(row)], axis=0)


def _unpack_small(p):
    row = p[32]
    return (p[0:4], p[8:12], p[16:24].reshape(2, 4096), row[0:64].reshape(2, 32), row[64:128].reshape(2, 32),
            row[128:192].reshape(2, 32), p[24:28].reshape(2, 2048), row[192:224].reshape(2, 16))


def _ssm_w_in_tiles(w):
    wb = w[:, 4096:5120].reshape(1024, 8, 128)
    wc = w[:, 5120:6144].reshape(1024, 8, 128)
    wbc = jnp.concatenate([wb, wc], axis=2).reshape(1024, 2048)
    return jnp.concatenate([w[:, 0:4096], wbc, w[:, 6144:6176], jnp.zeros((1024, 224), w.dtype)], axis=1)


def _ssm_w_in_untile(dw):
    dbc = dw[:, 4096:6144].reshape(1024, 8, 256)
    return jnp.concatenate([dw[:, 0:4096], dbc[:, :, 0:128].reshape(1024, 1024), dbc[:, :, 128:256].reshape(1024, 1024),
                            dw[:, 6144:6176]], axis=1)


def _conv_tiles(cw):
    k = cw.shape[0]
    xs = jnp.transpose(cw[:, 0:2048].reshape(k, 8, 256), (1, 0, 2))
    b = cw[:, 2048:3072].reshape(k, 8, 128)
    c = cw[:, 3072:4096].reshape(k, 8, 128)
    bc = jnp.transpose(jnp.concatenate([b, c], axis=2), (1, 0, 2))
    return jnp.concatenate([xs, bc], axis=0)


def _conv_untile(t):
    k = t.shape[1]
    xs = jnp.transpose(t[0:8], (1, 0, 2)).reshape(k, 2048)
    bc = jnp.transpose(t[8:16], (1, 0, 2))
    return jnp.concatenate([xs, bc[:, :, 0:128].reshape(k, 1024), bc[:, :, 128:256].reshape(k, 1024)], axis=1)


def _rope_tables(positions):
    inv = ROPE_THETA ** (-jnp.arange(0, 16, 2, dtype=f32) / 16)
    ang = positions.astype(f32).reshape(-1, 1) * inv
    cos, sin = jnp.cos(ang), jnp.sin(ang)
    t_len = ang.shape[0]
    cos64 = jnp.concatenate([cos, cos, jnp.ones((t_len, 48), f32)], axis=1)
    sin64 = jnp.concatenate([-sin, sin, jnp.zeros((t_len, 48), f32)], axis=1)
    return jnp.tile(cos64, (1, 4)), jnp.tile(sin64, (1, 4))


def kernel(x, positions, pre_norm, post_norm, ssm_w_in, ssm_conv_w, ssm_conv_b, ssm_dt_bias, ssm_a_log, ssm_d, ssm_gate_norm, ssm_w_out, att_w_in, att_sinks, att_w_out, loss_target, m_pre_norm, m_post_norm, m_ssm_w_in, m_ssm_conv_w, m_ssm_conv_b, m_ssm_dt_bias, m_ssm_a_log, m_ssm_d, m_ssm_gate_norm, m_ssm_w_out, m_att_w_in, m_att_sinks, m_att_w_out, v_pre_norm, v_post_norm, v_ssm_w_in, v_ssm_conv_w, v_ssm_conv_b, v_ssm_dt_bias, v_ssm_a_log, v_ssm_d, v_ssm_gate_norm, v_ssm_w_out, v_att_w_in, v_att_sinks, v_att_w_out):
    t_len = x.shape[1]
    tm = min(1024, t_len)
    xin = x.reshape(t_len, D_MODEL)
    tgt = loss_target.reshape(t_len, D_MODEL)
    cidx = lax.axis_index("c").astype(jnp.int32).reshape(1)

    g_ssm_in, g_att_in, g_ssm_out, g_att_out = _all_gather_big(
        [ssm_w_in.astype(MXU_DTYPE), att_w_in.astype(MXU_DTYPE), ssm_w_out.astype(MXU_DTYPE),
         att_w_out.astype(MXU_DTYPE)])
    conv_local = jnp.concatenate([ssm_conv_w.reshape(4, 1024), jnp.zeros((4, 1024), f32)], axis=0)
    conv_all = _all_gather_direct(conv_local, "all_gather_conv")[:, 0:4]
    w_ssm_in = jnp.transpose(g_ssm_in, (1, 2, 0, 3)).reshape(2, 1024, SSM_IN)
    w_ssm_out = jnp.transpose(g_ssm_out, (1, 0, 2, 3)).reshape(2, SSM_INNER, 1024)
    w_att_in = jnp.transpose(g_att_in, (1, 2, 0, 3)).reshape(2, 1024, ATT_IN)
    w_att_out = jnp.transpose(g_att_out, (1, 0, 2, 3)).reshape(2, 1024, 1024)
    conv_w = jnp.transpose(conv_all.reshape(8, 2, 4, 512), (1, 2, 0, 3)).reshape(2, 4, 4096)
    cos_t, sin_t = _rope_tables(positions)

    saved = []
    xc = xin
    for i in range(4):
        j = i // 2
        wn_pre, wn_post = pre_norm[i].reshape(1, D_MODEL), post_norm[i].reshape(1, D_MODEL)
        if i % 2 == 0:
            w_in = _ssm_w_in_tiles(w_ssm_in[j])
            cw, cb = _conv_tiles(conv_w[j]), _conv_tiles(ssm_conv_b[j].reshape(1, 4096))
            dtb, alog, dsk = ssm_dt_bias[j].reshape(1, 32), ssm_a_log[j].reshape(1, 32), ssm_d[j].reshape(1, 32)
            gn = ssm_gate_norm[j].reshape(SSM_KT, 1, TILE)
            u, h = _mm_in(xc, wn_pre, w_in, 5, tm, f"ssm_in_{j}")
            xbc = _conv_fwd(u, cw, cb, f"ssm_conv_{j}")
            a3, yp, hst = _ssm2_fwd(u, xbc, dtb, alog, dsk, gn, f"ssm_core_{j}")
            y, xn = _mm_out(a3, w_ssm_out[j], xc, wn_post, 4, tm, f"ssm_out_{j}")
            saved.append(dict(x=xc, u=u, h=h, a=a3, yp=yp, hst=hst, y=y, w_in=w_in, cw=cw, cb=cb, dtb=dtb, alog=alog,
                              dsk=dsk, gn=gn, xbc=xbc))
        else:
            sinks = att_sinks[j].reshape(1, 16)
            u, h = _mm_in(xc, wn_pre, w_att_in[j], 5, tm, f"att_in_{j}")
            a = _att2_fwd(u, cos_t, sin_t, sinks, f"att_core_{j}")
            y, xn = _mm_out(a, w_att_out[j], xc, wn_post, 4, tm, f"att_out_{j}")
            saved.append(dict(x=xc, u=u, h=h, a=a, y=y, sinks=sinks))
        xc = xn

    dx, loss_part = _loss_grad(xc, tgt, tm)

    d_pre, d_post = [None] * 4, [None] * 4
    d_ssm_in, d_ssm_out, d_att_in, d_att_out = [None] * 2, [None] * 2, [None] * 2, [None] * 2
    d_cw, d_cb, d_dtb, d_alog, d_dsk, d_gn, d_sinks = ([None] * 2 for _ in range(7))
    for i in reversed(range(4)):
        j = i // 2
        s = saved[i]
        wn_pre, wn_post = pre_norm[i].reshape(1, D_MODEL), post_norm[i].reshape(1, D_MODEL)
        if i % 2 == 0:
            da3, dy, d_post[i] = _mm_dout(s["y"], dx, wn_post, w_ssm_out[j], 4, tm, f"ssm_dout_{j}")
            d_ssm_out[j] = _dw_rows(s["a"], dy, 4, tm, f"ssm_dwout_{j}")
            du, d_dtb[j], d_alog[j], d_dsk[j], dgn = _ssm2_bwd(
                s["u"], s["xbc"], s["yp"], s["hst"], da3, s["dtb"], s["alog"], s["dsk"], s["gn"],
                f"ssm_core_bwd_{j}")
            du, dcw, dcb = _conv_bwd(s["u"], du, s["cw"], s["cb"], f"ssm_conv_bwd_{j}")
            d_cw[j], d_cb[j], d_gn[j] = _conv_untile(dcw), _conv_untile(dcb), dgn.reshape(1, SSM_INNER)
            d_ssm_in[j] = _ssm_w_in_untile(_dw_cols(s["h"], du, 5, tm, f"ssm_dwin_{j}"))
            dx, d_pre[i] = _mm_dh(du, s["w_in"], s["x"], dx, wn_pre, 5, tm, f"ssm_dh_{j}")
        else:
            da, dy, d_post[i] = _mm_dout(s["y"], dx, wn_post, w_att_out[j], 4, tm, f"att_dout_{j}")
            d_att_out[j] = _dw_rows(s["a"], dy, 4, tm, f"att_dwout_{j}")
            du, d_sinks[j] = _att2_bwd(s["u"], cos_t, sin_t, s["sinks"], da, f"att_core_bwd_{j}")
            d_att_in[j] = _dw_cols(s["h"], du, 5, tm, f"att_dwin_{j}")
            dx, d_pre[i] = _mm_dh(du, w_att_in[j], s["x"], dx, wn_pre, 5, tm, f"att_dh_{j}")

    gs = _pack_grads(jnp.stack(d_ssm_in), jnp.stack(d_ssm_out), jnp.stack(d_att_in), jnp.stack(d_att_out),
                     jnp.stack(d_cw))
    r1 = _exchange_sibling(gs)
    tiles = (256, 256, ROWS_REST // 7)
    pairs = [_pair_sum(g, r, cidx, tr, f"rs_pair_sum_{k}") for k, (g, r, tr) in enumerate(zip(gs, r1, tiles))]
    parts = _exchange_chips(pairs)
    flat = lambda t: t.reshape(2048, t.shape[-1])
    a4 = _adamw(parts[0], flat(ssm_w_in), flat(m_ssm_w_in), flat(v_ssm_w_in), tiles[0], "adamw_ssm_in")
    b4 = _adamw(parts[1], flat(att_w_in), flat(m_att_w_in), flat(v_att_w_in), tiles[1], "adamw_att_in")
    c4 = _adamw(parts[2], _pack_rest(ssm_w_out, att_w_out, ssm_conv_w), _pack_rest(m_ssm_w_out, m_att_w_out, m_ssm_conv_w),
                _pack_rest(v_ssm_w_out, v_att_w_out, v_ssm_conv_w), tiles[2], "adamw_rest")
    big = []
    for k in range(4):
        o_ssm_out, o_att_out, o_conv = _unpack_rest(c4[k])
        big.append((a4[k].reshape(2, 1024, 772), o_ssm_out, b4[k].reshape(2, 1024, 320), o_att_out, o_conv))

    small_local = _pack_small(jnp.concatenate(d_pre, axis=0), jnp.concatenate(d_post, axis=0),
                              jnp.concatenate(d_cb, axis=0), jnp.concatenate(d_gn, axis=0),
                              jnp.concatenate(d_dtb, axis=0), jnp.concatenate(d_alog, axis=0),
                              jnp.concatenate(d_dsk, axis=0), jnp.concatenate(d_sinks, axis=0), loss_part[0, 0])
    small_all = _all_gather_direct(small_local, "all_gather_small")
    ws = _pack_small(pre_norm, post_norm, ssm_conv_b, ssm_gate_norm, ssm_dt_bias, ssm_a_log, ssm_d, att_sinks)
    ms = _pack_small(m_pre_norm, m_post_norm, m_ssm_conv_b, m_ssm_gate_norm, m_ssm_dt_bias, m_ssm_a_log, m_ssm_d,
                     m_att_sinks)
    vs = _pack_small(v_pre_norm, v_post_norm, v_ssm_conv_b, v_ssm_gate_norm, v_ssm_dt_bias, v_ssm_a_log, v_ssm_d,
                     v_att_sinks)
    small4 = _adamw(small_all, ws, ms, vs, ROWS_SMALL, "adamw_small")
    loss = small4[0][32, 224]
    small = [_unpack_small(t) for t in small4]

    outs = [loss, dx.reshape(1, t_len, D_MODEL)]
    for k in range(4):
        b_ssm_in, b_ssm_out, b_att_in, b_att_out, b_conv = big[k]
        s_pre, s_post, s_cb, s_dtb, s_alog, s_d, s_gn, s_sinks = small[k]
        outs += [s_pre, s_post, b_ssm_in, b_conv, s_cb, s_dtb, s_alog, s_d, s_gn, b_ssm_out, b_att_in, s_sinks,
                 b_att_out]
    return tuple(outs)
```

```python
import jax
import jax.numpy as jnp
from jax import lax
from jax.experimental import pallas as pl
from jax.experimental.pallas import tpu as pltpu

f32 = jnp.float32
MXU_DTYPE = jnp.bfloat16
SDS = jax.ShapeDtypeStruct
MESH = pl.DeviceIdType.MESH

D_MODEL = 1024
EPS = 1e-6
TILE = 256
CHUNK = 128
ATT_ROWS = 4 * CHUNK
SSM_HEADS = 32
SSM_GROUPS = 8
SSM_P = 64
SSM_N = 128
SSM_INNER = 2048
SSM_IN = 6176
SSM_NT = 25
SSM_KT = 8
ATT_NT = 10
ATT_KT = 4
ATT_IN = 2560
ROPE_THETA = 500000.0
N_DEV = 8
VMEM_LIMIT = 56 * 1024 * 1024

ADAM_LR = 0.001
ADAM_B1 = 0.9
ADAM_B2 = 0.999
ADAM_EPS = 1e-08
ADAM_WD = 0.01
ADAM_STEP = 10

ROWS_SSM_OUT = 2 * 256
ROWS_ATT_OUT = 2 * 128
ROWS_SMALL = 40


def _pc(body, name, **kw):
    return pl.pallas_call(body, name=name, **kw)


def _params(sem):
    return pltpu.CompilerParams(dimension_semantics=sem, vmem_limit_bytes=VMEM_LIMIT)


def _sigmoid(x):
    return 1.0 / (1.0 + jnp.exp(-x))


def _silu(x):
    return x * _sigmoid(x)


def _softplus(x):
    return jnp.maximum(x, 0.0) + jnp.log(1.0 + jnp.exp(-jnp.abs(x)))


def _mx(x):
    return x.astype(MXU_DTYPE)


def _dot(a, b):
    return jnp.dot(_mx(a), _mx(b), preferred_element_type=f32)


def _dot_nt(a, b):
    return lax.dot_general(_mx(a), _mx(b), (((1,), (1,)), ((), ())), preferred_element_type=f32)


def _dot_tn(a, b):
    return lax.dot_general(_mx(a), _mx(b), (((0,), (0,)), ((), ())), preferred_element_type=f32)


def _rms_fwd(x, w):
    r = lax.rsqrt(jnp.mean(x * x, axis=-1, keepdims=True) + EPS)
    return x * r * w


def _rms_bwd(x, w, dy):
    r = lax.rsqrt(jnp.mean(x * x, axis=-1, keepdims=True) + EPS)
    xh = x * r
    dw = jnp.sum(dy * xh, axis=0, keepdims=True)
    g = dy * w
    dx = r * (g - xh * jnp.mean(g * xh, axis=-1, keepdims=True))
    return dx, dw


def _mm_in(x, wn, w, ntb, tm, name):
    t_len, d = x.shape
    nt = w.shape[1] // TILE

    def body(x_ref, wn_ref, w_ref, u_ref, h_ref):
        @pl.when(pl.program_id(1) == 0)
        def _():
            h_ref[...] = _rms_fwd(x_ref[...], wn_ref[...]).astype(h_ref.dtype)
        h = h_ref[...]
        for t in range(ntb):
            u_ref[t] = jnp.dot(h, w_ref[:, TILE * t:TILE * (t + 1)], preferred_element_type=f32)

    return _pc(
        body, name, grid=(t_len // tm, nt // ntb),
        in_specs=[pl.BlockSpec((tm, d), lambda i, j: (i, 0)),
                  pl.BlockSpec((1, d), lambda i, j: (0, 0)),
                  pl.BlockSpec((d, ntb * TILE), lambda i, j: (0, j))],
        out_specs=[pl.BlockSpec((ntb, tm, TILE), lambda i, j: (j, i, 0)),
                   pl.BlockSpec((tm, d), lambda i, j: (i, 0))],
        out_shape=[SDS((nt, t_len, TILE), f32), SDS((t_len, d), MXU_DTYPE)],
        compiler_params=_params(("arbitrary", "arbitrary")),
    )(x, wn, w)


def _mm_dout(y, dxn, wn, w, ntb, tm, name):
    t_len, d = y.shape
    nt = w.shape[0] // TILE

    def body(y_ref, dxn_ref, wn_ref, w_ref, da_ref, dy_ref, dwn_ref):
        i, j = pl.program_id(0), pl.program_id(1)

        @pl.when((i == 0) & (j == 0))
        def _():
            dwn_ref[...] = jnp.zeros_like(dwn_ref)

        @pl.when(j == 0)
        def _():
            dy, dw = _rms_bwd(y_ref[...], wn_ref[...], dxn_ref[...])
            dy_ref[...] = dy.astype(dy_ref.dtype)
            dwn_ref[...] += dw
        dy = dy_ref[...]
        for t in range(ntb):
            da_ref[t] = _dot_nt(dy, w_ref[TILE * t:TILE * (t + 1), :])

    return _pc(
        body, name, grid=(t_len // tm, nt // ntb),
        in_specs=[pl.BlockSpec((tm, d), lambda i, j: (i, 0)),
                  pl.BlockSpec((tm, d), lambda i, j: (i, 0)),
                  pl.BlockSpec((1, d), lambda i, j: (0, 0)),
                  pl.BlockSpec((ntb * TILE, d), lambda i, j: (j, 0))],
        out_specs=[pl.BlockSpec((ntb, tm, TILE), lambda i, j: (j, i, 0)),
                   pl.BlockSpec((tm, d), lambda i, j: (i, 0)),
                   pl.BlockSpec((1, d), lambda i, j: (0, 0))],
        out_shape=[SDS((nt, t_len, TILE), f32), SDS((t_len, d), MXU_DTYPE), SDS((1, d), f32)],
        compiler_params=_params(("arbitrary", "arbitrary")),
    )(y, dxn, wn, w)


def _mm_out(a, w, x, wn, ktb, tm, name):
    kt, t_len, _ = a.shape
    d = w.shape[1]
    nk = kt // ktb

    def body(a_ref, w_ref, x_ref, wn_ref, y_ref, xn_ref, acc):
        k = pl.program_id(1)

        @pl.when(k == 0)
        def _():
            acc[...] = jnp.zeros_like(acc)
        s = acc[...]
        for t in range(ktb):
            s = s + jnp.dot(a_ref[t], w_ref[TILE * t:TILE * (t + 1), :], preferred_element_type=f32)
        acc[...] = s

        @pl.when(k == nk - 1)
        def _():
            y = acc[...]
            y_ref[...] = y
            xn_ref[...] = x_ref[...] + _rms_fwd(y, wn_ref[...])

    return _pc(
        body, name, grid=(t_len // tm, nk),
        in_specs=[pl.BlockSpec((ktb, tm, TILE), lambda i, k: (k, i, 0)),
                  pl.BlockSpec((ktb * TILE, d), lambda i, k: (k, 0)),
                  pl.BlockSpec((tm, d), lambda i, k: (i, 0)),
                  pl.BlockSpec((1, d), lambda i, k: (0, 0))],
        out_specs=[pl.BlockSpec((tm, d), lambda i, k: (i, 0)),
                   pl.BlockSpec((tm, d), lambda i, k: (i, 0))],
        out_shape=[SDS((t_len, d), f32), SDS((t_len, d), f32)],
        scratch_shapes=[pltpu.VMEM((tm, d), f32)],
        compiler_params=_params(("arbitrary", "arbitrary")),
    )(a, w, x, wn)


def _mm_dh(du, w, x, dxn, wn, ktb, tm, name):
    kt, t_len, _ = du.shape
    d = w.shape[0]
    nk = kt // ktb

    def body(du_ref, w_ref, x_ref, dxn_ref, wn_ref, dx_ref, dwn_ref, acc):
        i, k = pl.program_id(0), pl.program_id(1)

        @pl.when((i == 0) & (k == 0))
        def _():
            dwn_ref[...] = jnp.zeros_like(dwn_ref)

        @pl.when(k == 0)
        def _():
            acc[...] = jnp.zeros_like(acc)
        s = acc[...]
        for t in range(ktb):
            s = s + _dot_nt(du_ref[t], w_ref[:, TILE * t:TILE * (t + 1)])
        acc[...] = s

        @pl.when(k == nk - 1)
        def _():
            dxp, dw = _rms_bwd(x_ref[...], wn_ref[...], acc[...])
            dx_ref[...] = dxn_ref[...] + dxp
            dwn_ref[...] += dw

    return _pc(
        body, name, grid=(t_len // tm, nk),
        in_specs=[pl.BlockSpec((ktb, tm, TILE), lambda i, k: (k, i, 0)),
                  pl.BlockSpec((d, ktb * TILE), lambda i, k: (0, k)),
                  pl.BlockSpec((tm, d), lambda i, k: (i, 0)),
                  pl.BlockSpec((tm, d), lambda i, k: (i, 0)),
                  pl.BlockSpec((1, d), lambda i, k: (0, 0))],
        out_specs=[pl.BlockSpec((tm, d), lambda i, k: (i, 0)),
                   pl.BlockSpec((1, d), lambda i, k: (0, 0))],
        out_shape=[SDS((t_len, d), f32), SDS((1, d), f32)],
        scratch_shapes=[pltpu.VMEM((tm, d), f32)],
        compiler_params=_params(("arbitrary", "arbitrary")),
    )(du, w, x, dxn, wn)


def _dw_cols(a, b, ntb, tk, name):
    t_len, kdim = a.shape
    nt = b.shape[0]

    def body(a_ref, b_ref, o_ref):
        @pl.when(pl.program_id(1) == 0)
        def _():
            o_ref[...] = jnp.zeros_like(o_ref)
        av = a_ref[...]
        for s in range(ntb):
            o_ref[:, TILE * s:TILE * (s + 1)] += _dot_tn(av, b_ref[s])

    return _pc(
        body, name, grid=(nt // ntb, t_len // tk),
        in_specs=[pl.BlockSpec((tk, kdim), lambda j, t: (t, 0)),
                  pl.BlockSpec((ntb, tk, TILE), lambda j, t: (j, t, 0))],
        out_specs=pl.BlockSpec((kdim, ntb * TILE), lambda j, t: (0, j)),
        out_shape=SDS((kdim, nt * TILE), f32),
        compiler_params=_params(("arbitrary", "arbitrary")),
    )(a, b)


def _dw_rows(a, b, ktb, tk, name):
    kt, t_len, _ = a.shape
    d = b.shape[1]

    def body(a_ref, b_ref, o_ref):
        @pl.when(pl.program_id(1) == 0)
        def _():
            o_ref[...] = jnp.zeros_like(o_ref)
        bv = b_ref[...]
        for s in range(ktb):
            o_ref[TILE * s:TILE * (s + 1), :] += _dot_tn(a_ref[s], bv)

    return _pc(
        body, name, grid=(kt // ktb, t_len // tk),
        in_specs=[pl.BlockSpec((ktb, tk, TILE), lambda k, t: (k, t, 0)),
                  pl.BlockSpec((tk, d), lambda k, t: (t, 0))],
        out_specs=pl.BlockSpec((ktb * TILE, d), lambda k, t: (k, 0)),
        out_shape=SDS((kt * TILE, d), f32),
        compiler_params=_params(("arbitrary", "arbitrary")),
    )(a, b)


def _loss_grad(x, tgt, tm):
    t_len, d = x.shape

    def body(x_ref, t_ref, dx_ref, l_ref):
        @pl.when(pl.program_id(0) == 0)
        def _():
            l_ref[...] = jnp.zeros_like(l_ref)
        e = x_ref[...] - t_ref[...]
        dx_ref[...] = e * (1.0 / d)
        row = jnp.mean(e * e, axis=-1, keepdims=True)
        l_ref[...] += 0.5 * jnp.sum(row, axis=0, keepdims=True)

    return _pc(
        body, "loss_grad", grid=(t_len // tm,),
        in_specs=[pl.BlockSpec((tm, d), lambda i: (i, 0)), pl.BlockSpec((tm, d), lambda i: (i, 0))],
        out_specs=[pl.BlockSpec((tm, d), lambda i: (i, 0)), pl.BlockSpec((1, 128), lambda i: (0, 0))],
        out_shape=[SDS((t_len, d), f32), SDS((1, 128), f32)],
        compiler_params=_params(("arbitrary",)),
    )(x, tgt)


def _tri(lower):
    r = lax.broadcasted_iota(jnp.int32, (CHUNK, CHUNK), 0)
    c = lax.broadcasted_iota(jnp.int32, (CHUNK, CHUNK), 1)
    return ((c <= r) if lower else (c >= r)).astype(f32)


def _split(x, n):
    parts = []
    for _ in range(n):
        p = x.astype(jnp.bfloat16)
        parts.append(p)
        x = x - p.astype(f32)
    return parts


def _dot_exact(a, b, dims, split_a, n=3):
    out = None
    if split_a:
        b = b.astype(jnp.bfloat16)
        for p in _split(a, n):
            t = lax.dot_general(p, b, (dims, ((), ())), preferred_element_type=f32)
            out = t if out is None else out + t
    else:
        a = a.astype(jnp.bfloat16)
        for p in _split(b, n):
            t = lax.dot_general(a, p, (dims, ((), ())), preferred_element_type=f32)
            out = t if out is None else out + t
    return out


def _dt_path(dt_raw, dtb, alog):
    dtr = dt_raw + dtb
    dt = _softplus(dtr)
    a_neg = -jnp.exp(alog)
    a = dt * a_neg
    acs = _dot_exact(_tri(True), a, ((1,), (0,)), False)
    acs_t = _dot_exact(a, _tri(False), ((0,), (0,)), True)
    return dtr, dt, a_neg, acs, acs_t


CONV_ROWS = 1024
CONV_SUB = 32


def _conv_specs(nb, rows, rev):
    def ridx(i):
        return (nb - 1 - i) if rev else i
    return [
        pl.BlockSpec((1, rows, TILE), lambda p, i: (8 + p, ridx(i), 0)),
        pl.BlockSpec((1, 8, TILE), lambda p, i: (8 + p, jnp.maximum(ridx(i) * (rows // 8) - 1, 0), 0)),
        pl.BlockSpec((1, 4, TILE), lambda p, i: (p, 0, 0)),
        pl.BlockSpec((1, 1, TILE), lambda p, i: (p, 0, 0)),
    ]


def _conv_fwd(u, cw, cb, name):
    t_len = u.shape[1]
    rows = min(CONV_ROWS, t_len)
    nb = t_len // rows

    def body(u_ref, halo_ref, cw_ref, cb_ref, o_ref, win_s):
        i = pl.program_id(1)
        win_s[0:8, :] = jnp.where(i > 0, halo_ref[0], 0.0)
        win_s[8:8 + rows, :] = u_ref[0]
        w = [cw_ref[0, k:k + 1, :] for k in range(4)]
        b = cb_ref[0]
        for s in range(rows // CONV_SUB):
            o = CONV_SUB * s
            acc = b
            for k in range(4):
                acc = acc + w[k] * win_s[5 + k + o:5 + k + o + CONV_SUB, :]
            o_ref[0, o:o + CONV_SUB, :] = _silu(acc)

    return _pc(
        body, name, grid=(16, nb),
        in_specs=_conv_specs(nb, rows, False),
        out_specs=pl.BlockSpec((1, rows, TILE), lambda p, i: (p, i, 0)),
        out_shape=SDS((16, t_len, TILE), f32),
        scratch_shapes=[pltpu.VMEM((8 + rows, TILE), f32)],
        compiler_params=_params(("arbitrary", "arbitrary")),
    )(u, u, cw, cb)


def _conv_bwd(u, du, cw, cb, name):
    t_len = u.shape[1]
    rows = min(CONV_ROWS, t_len)
    nb = t_len // rows

    def body(u_ref, halo_ref, cw_ref, cb_ref, d_ref, o_ref, dcw_ref, dcb_ref, carry_s, win_s, dp_s):
        i = pl.program_id(1)
        ri = nb - 1 - i

        @pl.when(i == 0)
        def _():
            carry_s[...] = jnp.zeros_like(carry_s)
            dcw_ref[...] = jnp.zeros_like(dcw_ref)
            dcb_ref[...] = jnp.zeros_like(dcb_ref)
        win_s[0:8, :] = jnp.where(ri > 0, halo_ref[0], 0.0)
        win_s[8:8 + rows, :] = u_ref[0]
        w = [cw_ref[0, k:k + 1, :] for k in range(4)]
        b = cb_ref[0]
        dw = [jnp.zeros((1, TILE), f32)] * 4
        db = jnp.zeros((1, TILE), f32)
        for s in range(rows // CONV_SUB):
            o = CONV_SUB * s
            xk = [win_s[5 + k + o:5 + k + o + CONV_SUB, :] for k in range(4)]
            pre = b
            for k in range(4):
                pre = pre + w[k] * xk[k]
            sg = _sigmoid(pre)
            dpre = d_ref[0, o:o + CONV_SUB, :] * (sg * (1.0 + pre * (1.0 - sg)))
            dp_s[o:o + CONV_SUB, :] = dpre
            dw = [dw[k] + jnp.sum(dpre * xk[k], axis=0, keepdims=True) for k in range(4)]
            db = db + jnp.sum(dpre, axis=0, keepdims=True)
        dp_s[rows:rows + 8, :] = carry_s[...]
        for s in range(rows // CONV_SUB):
            o = CONV_SUB * s
            acc = w[0] * dp_s[3 + o:3 + o + CONV_SUB, :]
            for k in range(1, 4):
                acc = acc + w[k] * dp_s[3 - k + o:3 - k + o + CONV_SUB, :]
            o_ref[0, o:o + CONV_SUB, :] = acc
        carry_s[...] = dp_s[0:8, :]
        for k in range(4):
            dcw_ref[0, k:k + 1, :] += dw[k]
        dcb_ref[0] += db

    return _pc(
        body, name, grid=(16, nb),
        in_specs=_conv_specs(nb, rows, True) + [pl.BlockSpec((1, rows, TILE), lambda p, i: (8 + p, nb - 1 - i, 0))],
        out_specs=[pl.BlockSpec((1, rows, TILE), lambda p, i: (8 + p, nb - 1 - i, 0)),
                   pl.BlockSpec((1, 4, TILE), lambda p, i: (p, 0, 0)),
                   pl.BlockSpec((1, 1, TILE), lambda p, i: (p, 0, 0))],
        out_shape=[SDS(du.shape, f32), SDS((16, 4, TILE), f32), SDS((16, 1, TILE), f32)],
        input_output_aliases={4: 0},
        scratch_shapes=[pltpu.VMEM((8, TILE), f32), pltpu.VMEM((8 + rows, TILE), f32),
                        pltpu.VMEM((rows + 8, TILE), f32)],
        compiler_params=_params(("arbitrary", "arbitrary")),
    )(u, u, cw, cb, du)


def _collapse_matrix(g):
    r = lax.broadcasted_iota(jnp.int32, (SSM_HEADS, TILE), 0)
    c = lax.broadcasted_iota(jnp.int32, (SSM_HEADS, TILE), 1)
    return ((c // SSM_P) + 4 * g == r).astype(jnp.bfloat16)


def _ssd_prelude(dt_raw, dtb, alog, dsk, colx_s, scx_s):
    dtr, dt, a_neg, acs, acs_t = _dt_path(dt_raw, dtb, alog)
    a_end = acs[CHUNK - 1:CHUNK, :]
    lane = lax.broadcasted_iota(jnp.int32, (1, 2 * SSM_P), 1)
    lane4 = lax.broadcasted_iota(jnp.int32, (1, TILE), 1)
    sub8 = lax.broadcasted_iota(jnp.int32, (8, 1), 0)

    def row4(v, g):
        e = [v[:, 4 * g + r:4 * g + r + 1] for r in range(4)]
        return jnp.where(lane4 < 64, e[0], jnp.where(lane4 < 128, e[1], jnp.where(lane4 < 192, e[2], e[3])))

    for g in range(SSM_GROUPS):
        for k, arr in enumerate((dt, acs)):
            for half in range(2):
                h0 = 4 * g + 2 * half
                colx_s[k, g, :, 128 * half:128 * (half + 1)] = jnp.where(
                    lane < SSM_P, arr[:, h0:h0 + 1], arr[:, h0 + 1:h0 + 2])
        scx_s[g] = jnp.where(sub8 == 0, row4(dsk, g), jnp.where(sub8 == 1, row4(a_end, g), 0.0))
    return dtr, dt, a_neg, acs_t


def _ssm_core_specs(nc, rev):
    def cidx(c):
        return (nc - 1 - c) if rev else c
    return [
        pl.BlockSpec((SSM_KT, CHUNK, TILE), lambda c: (0, cidx(c), 0)),
        pl.BlockSpec((1, CHUNK, TILE), lambda c: (SSM_NT - 1, cidx(c), 0)),
        pl.BlockSpec((16, CHUNK, TILE), lambda c: (0, cidx(c), 0)),
        pl.BlockSpec((1, SSM_HEADS), lambda c: (0, 0)),
        pl.BlockSpec((1, SSM_HEADS), lambda c: (0, 0)),
        pl.BlockSpec((1, SSM_HEADS), lambda c: (0, 0)),
        pl.BlockSpec((SSM_KT, 1, TILE), lambda c: (0, 0, 0)),
        pl.BlockSpec((2, SSM_GROUPS, CHUNK, TILE), lambda c: (0, 0, cidx(c), 0)),
        pl.BlockSpec((1, SSM_GROUPS, 8, TILE), lambda c: (cidx(c), 0, 0, 0)),
        pl.BlockSpec((1, SSM_HEADS, CHUNK), lambda c: (cidx(c), 0, 0)),
    ]


PRE_CHUNKS = 4


def _ssm_pre(u, dtb, alog, dsk, name):
    t_len = u.shape[1]
    nc = t_len // CHUNK
    per = min(PRE_CHUNKS, nc)

    def body(dt_ref, dtb_ref, alog_ref, dsk_ref, colx_ref, scx_ref, acst_ref):
        for c in range(per):
            rows = pl.ds(CHUNK * c, CHUNK)
            _, _, _, acs_t = _ssd_prelude(dt_ref[0, rows, 0:SSM_HEADS], dtb_ref[...], alog_ref[...], dsk_ref[...],
                                          colx_ref.at[:, :, rows, :], scx_ref.at[c])
            acst_ref[c] = acs_t

    vec = pl.BlockSpec((1, SSM_HEADS), lambda i: (0, 0))
    return _pc(
        body, name, grid=(nc // per,),
        in_specs=[pl.BlockSpec((1, per * CHUNK, TILE), lambda i: (SSM_NT - 1, i, 0)), vec, vec, vec],
        out_specs=[pl.BlockSpec((2, SSM_GROUPS, per * CHUNK, TILE), lambda i: (0, 0, i, 0)),
                   pl.BlockSpec((per, SSM_GROUPS, 8, TILE), lambda i: (i, 0, 0, 0)),
                   pl.BlockSpec((per, SSM_HEADS, CHUNK), lambda i: (i, 0, 0))],
        out_shape=[SDS((2, SSM_GROUPS, t_len, TILE), f32), SDS((nc, SSM_GROUPS, 8, TILE), f32),
                   SDS((nc, SSM_HEADS, CHUNK), f32)],
        compiler_params=_params(("arbitrary",)),
    )(u, dtb, alog, dsk)


def _stack_cols_rows(acx, rows):
    ac = jnp.concatenate([acx[:, SSM_P * r:SSM_P * r + 1] for r in range(4)], axis=0)
    ar = jnp.concatenate([jnp.broadcast_to(rows[r:r + 1, :], (CHUNK, CHUNK)) for r in range(4)], axis=0)
    return ac, ar


def _ssm2_fwd(u, xbc, pre, dtb, alog, dsk, gn, name):
    t_len = u.shape[1]
    nc = t_len // CHUNK

    def body(z_ref, dt_ref, x_ref, dtb_ref, alog_ref, dsk_ref, gn_ref, colx_ref, scx_ref, acst_ref,
             a3_ref, yp_ref, hst_ref, h_s, xt_s, yd_s):
        c = pl.program_id(0)

        @pl.when(c == 0)
        def _():
            h_s[...] = jnp.zeros_like(h_s)
        acs_t = acst_ref[0]
        causal = _tri4()

        def group(g, s1):
            xs = x_ref[g]
            bm, cm = x_ref[8 + g, :, 0:SSM_N], x_ref[8 + g, :, SSM_N:2 * SSM_N]
            cb = _dot_nt(cm, bm)
            sc = scx_ref[0, g]
            a_end = sc[1:2, :]
            rows = pltpu.roll(acs_t, (SSM_HEADS - 4 * g) % SSM_HEADS, 0)
            hp = h_s[g]
            xt = xs * colx_ref[0, g]
            xt_s[...] = xt
            acx = colx_ref[1, g]
            ac_st, ar_st = _stack_cols_rows(acx, rows)
            m = jnp.concatenate([cb] * 4, axis=0) * jnp.exp(jnp.where(causal, ac_st - ar_st, -jnp.inf))
            for r in range(4):
                hd = slice(SSM_P * r, SSM_P * (r + 1))
                yd_s[:, hd] = _dot(m[CHUNK * r:CHUNK * (r + 1), :], xt_s[:, hd])
            yp_ref[g] = yd_s[...] + _dot(cm, hp) * jnp.exp(acx) + sc[0:1, :] * xs
            hst_ref[0, g] = hp
            h_s[g] = hp * jnp.exp(a_end) + _dot_tn(bm, xt * jnp.exp(a_end - acx))
            y2 = yp_ref[g] * _silu(z_ref[g])
            return s1 + jnp.sum(y2 * y2, axis=1, keepdims=True)

        s1 = lax.fori_loop(0, SSM_GROUPS // 2, lambda i, c: group(2 * i + 1, group(2 * i, c)),
                           jnp.zeros((CHUNK, 1), f32))
        rinv = lax.rsqrt(s1 * (1.0 / SSM_INNER) + EPS)

        def gate(g, carry):
            y2 = yp_ref[g] * _silu(z_ref[g])
            a3_ref[g] = (y2 * rinv * gn_ref[g]).astype(a3_ref.dtype)
            return carry

        lax.fori_loop(0, SSM_GROUPS, gate, 0)

    return _pc(
        body, name, grid=(nc,),
        in_specs=_ssm_core_specs(nc, False),
        out_specs=[pl.BlockSpec((SSM_KT, CHUNK, TILE), lambda c: (0, c, 0)),
                   pl.BlockSpec((SSM_KT, CHUNK, TILE), lambda c: (0, c, 0)),
                   pl.BlockSpec((1, SSM_GROUPS, SSM_N, TILE), lambda c: (c, 0, 0, 0))],
        out_shape=[SDS((SSM_KT, t_len, TILE), MXU_DTYPE), SDS((SSM_KT, t_len, TILE), f32),
                   SDS((nc, SSM_GROUPS, SSM_N, TILE), f32)],
        scratch_shapes=[pltpu.VMEM((SSM_GROUPS, SSM_N, TILE), f32), pltpu.VMEM((CHUNK, TILE), f32),
                        pltpu.VMEM((CHUNK, TILE), f32)],
        compiler_params=_params(("arbitrary",)),
    )(u, u, xbc, dtb, alog, dsk, gn, *pre)


def _ssm2_bwd(u, xbc, pre, yp, hst, da3, dtb, alog, dsk, gn, name):
    t_len = u.shape[1]
    nc = t_len // CHUNK

    def body(z_ref, dt_ref, x_ref, dtb_ref, alog_ref, dsk_ref, gn_ref, colx_s, scx_ref, acst_ref,
             yp_ref, hst_ref, da3_ref, du_ref, ddtb_ref, dalog_ref, ddsk_ref, dgn_ref,
             dh_s, xt_s, dy_s, dxt_s, ddtx_s, dacx_s, ddx_s, drow_s, dm_s, dmt_s):
        step = pl.program_id(0)

        @pl.when(step == 0)
        def _():
            dh_s[...] = jnp.zeros_like(dh_s)
            ddtb_ref[...] = jnp.zeros_like(ddtb_ref)
            dalog_ref[...] = jnp.zeros_like(dalog_ref)
            ddsk_ref[...] = jnp.zeros_like(ddsk_ref)
            dgn_ref[...] = jnp.zeros_like(dgn_ref)
        dtr = dt_ref[0, :, 0:SSM_HEADS] + dtb_ref[...]
        dt = _softplus(dtr)
        a_neg = -jnp.exp(alog_ref[...])
        acs_t = acst_ref[0]
        causal = _tri4()
        causal_t = (lax.broadcasted_iota(jnp.int32, (ATT_ROWS, CHUNK), 1)
                    >= lax.broadcasted_iota(jnp.int32, (ATT_ROWS, CHUNK), 0) % CHUNK)
        last = (lax.broadcasted_iota(jnp.int32, (1, CHUNK), 1) == CHUNK - 1).astype(f32)
        lane = lax.broadcasted_iota(jnp.int32, (1, TILE), 1)
        sub32 = lax.broadcasted_iota(jnp.int32, (SSM_HEADS, 1), 0)
        drow_s[...] = jnp.zeros_like(drow_s)

        def sums(g, carry):
            s1, s2 = carry
            y2 = yp_ref[g] * _silu(z_ref[g])
            g3 = da3_ref[g] * gn_ref[g]
            return (s1 + jnp.sum(y2 * y2, axis=1, keepdims=True), s2 + jnp.sum(g3 * y2, axis=1, keepdims=True))

        zcol = jnp.zeros((CHUNK, 1), f32)
        carry = (zcol, zcol)
        for g in range(SSM_GROUPS):
            carry = sums(g, carry)
        s1, s2 = carry
        rinv = lax.rsqrt(s1 * (1.0 / SSM_INNER) + EPS)
        m2 = s2 * rinv * rinv * rinv * (1.0 / SSM_INNER)

        def group(g, carry):
            z = z_ref[g]
            sg = _sigmoid(z)
            sz = z * sg
            y = yp_ref[g]
            y2 = y * sz
            da3 = da3_ref[g]
            dgn_ref[g] += jnp.sum(da3 * y2 * rinv, axis=0, keepdims=True)
            dy2 = rinv * (da3 * gn_ref[g]) - y2 * m2
            dy = dy2 * sz
            dy_s[...] = dy
            du_ref[g] = dy2 * y * (sg * (1.0 + z * (1.0 - sg)))
            xs = x_ref[g]
            bm, cm = x_ref[8 + g, :, 0:SSM_N], x_ref[8 + g, :, SSM_N:2 * SSM_N]
            cb = _dot_nt(cm, bm)
            cbt = _dot_nt(bm, cm)
            dtx, acx = colx_s[0, g], colx_s[1, g]
            sc = scx_ref[0, g]
            a_end = sc[1:2, :]
            ex = jnp.exp(acx)
            wdx = jnp.exp(a_end - acx)
            eend = jnp.exp(a_end)
            rows = pltpu.roll(acs_t, (SSM_HEADS - 4 * g) % SSM_HEADS, 0)
            hp = hst_ref[0, g]
            dhn = dh_s[g]
            xt = xs * dtx
            xt_s[...] = xt
            ch = _dot(cm, hp)
            gy = dy * ex
            dcm = _dot_nt(gy, hp)
            dh_s[g] = _dot_tn(cm, gy) + dhn * eend
            q = _dot(bm, dhn)
            dbm = _dot_nt(xt * wdx, dhn)
            qx = q * xt * wdx
            v_end = jnp.sum(dhn * hp, axis=0, keepdims=True) * eend + jnp.sum(qx, axis=0, keepdims=True)
            ac_st, ar_st = _stack_cols_rows(acx, rows)
            lam = jnp.exp(jnp.where(causal, ac_st - ar_st, -jnp.inf))
            lam_t = jnp.exp(jnp.where(causal_t, ar_st - ac_st, -jnp.inf))
            m = jnp.concatenate([cb] * 4, axis=0) * lam
            m_t = jnp.concatenate([cbt] * 4, axis=0) * lam_t
            for r in range(4):
                hd = slice(SSM_P * r, SSM_P * (r + 1))
                rs = slice(CHUNK * r, CHUNK * (r + 1))
                dm_s[rs, :] = _dot_nt(dy_s[:, hd], xt_s[:, hd])
                dmt_s[rs, :] = _dot_nt(xt_s[:, hd], dy_s[:, hd])
                dxt_s[:, hd] = _dot(m_t[rs, :], dy_s[:, hd])
            dm = dm_s[...]
            dl = dm * lam
            dseg = dm * m
            dseg_t = dmt_s[...] * m_t
            dcb = dl[0:CHUNK] + dl[CHUNK:2 * CHUNK] + dl[2 * CHUNK:3 * CHUNK] + dl[3 * CHUNK:4 * CHUNK]
            drows = jnp.zeros((SSM_HEADS, CHUNK), f32)
            for r in range(4):
                rs = slice(CHUNK * r, CHUNK * (r + 1))
                in_head = (lane >= SSM_P * r) & (lane < SSM_P * (r + 1))
                d_ac = jnp.sum(dseg_t[rs, :], axis=0, keepdims=True)
                d_ar = jnp.sum(dseg[rs, :], axis=0, keepdims=True)
                d_aend = jnp.sum(jnp.where(in_head, v_end, 0.0), axis=1, keepdims=True)
                drows = drows + jnp.where(sub32 == r, d_ac - d_ar + last * d_aend, 0.0)
            dxt = dxt_s[...] + q * wdx
            du_ref[8 + g] = sc[0:1, :] * dy + dxt * dtx
            ddtx_s[g] = dxt * xs
            dacx_s[g] = dy * ch * ex - qx
            ddx_s[g] = jnp.broadcast_to(jnp.sum(dy * xs, axis=0, keepdims=True), (8, TILE))
            du_ref[16 + g, :, 0:SSM_N] = dbm + _dot_tn(dcb, cm)
            du_ref[16 + g, :, SSM_N:2 * SSM_N] = dcm + _dot(dcb, bm)
            drow_s[...] += pltpu.roll(drows, (4 * g) % SSM_HEADS, 0)
            return carry

        lax.fori_loop(0, SSM_GROUPS, group, 0, unroll=4)
        ddt = jnp.zeros((CHUNK, SSM_HEADS), f32)
        dacs = jnp.zeros((CHUNK, SSM_HEADS), f32)
        ddsk = jnp.zeros((8, SSM_HEADS), f32)
        for g in range(SSM_GROUPS):
            col_g = _collapse_matrix(g)
            ddt = ddt + _dot_exact(ddtx_s[g], col_g, ((1,), (1,)), True, 2)
            dacs = dacs + _dot_exact(dacx_s[g], col_g, ((1,), (1,)), True, 2)
            ddsk = ddsk + _dot_exact(ddx_s[g], col_g, ((1,), (1,)), True, 2)
        upper = _tri(False)
        da = _dot_exact(upper, dacs, ((1,), (0,)), False) + _dot_exact(upper, drow_s[...], ((1,), (1,)), False)
        ddt = ddt + da * a_neg
        dalog_ref[...] += jnp.sum(da * dt, axis=0, keepdims=True) * a_neg
        ddtr = ddt * _sigmoid(dtr)
        ddtb_ref[...] += jnp.sum(ddtr, axis=0, keepdims=True)
        ddsk_ref[...] += ddsk[0:1, :]
        du_ref[SSM_NT - 1] = jnp.zeros((CHUNK, TILE), f32)
        du_ref[SSM_NT - 1, :, 0:SSM_HEADS] = ddtr

    def rc(c):
        return nc - 1 - c

    vec = pl.BlockSpec((1, SSM_HEADS), lambda c: (0, 0))
    return _pc(
        body, name, grid=(nc,),
        in_specs=_ssm_core_specs(nc, True) + [
            pl.BlockSpec((SSM_KT, CHUNK, TILE), lambda c: (0, rc(c), 0)),
            pl.BlockSpec((1, SSM_GROUPS, SSM_N, TILE), lambda c: (rc(c), 0, 0, 0)),
            pl.BlockSpec((SSM_KT, CHUNK, TILE), lambda c: (0, rc(c), 0))],
        out_specs=[pl.BlockSpec((SSM_NT, CHUNK, TILE), lambda c: (0, rc(c), 0)), vec, vec, vec,
                   pl.BlockSpec((SSM_KT, 1, TILE), lambda c: (0, 0, 0))],
        out_shape=[SDS((SSM_NT, t_len, TILE), f32), SDS((1, SSM_HEADS), f32), SDS((1, SSM_HEADS), f32),
                   SDS((1, SSM_HEADS), f32), SDS((SSM_KT, 1, TILE), f32)],
        scratch_shapes=[pltpu.VMEM((SSM_GROUPS, SSM_N, TILE), f32), pltpu.VMEM((CHUNK, TILE), f32),
                        pltpu.VMEM((CHUNK, TILE), f32), pltpu.VMEM((CHUNK, TILE), f32),
                        pltpu.VMEM((SSM_GROUPS, CHUNK, TILE), f32), pltpu.VMEM((SSM_GROUPS, CHUNK, TILE), f32),
                        pltpu.VMEM((SSM_GROUPS, 8, TILE), f32), pltpu.VMEM((SSM_HEADS, CHUNK), f32),
                        pltpu.VMEM((4 * CHUNK, CHUNK), f32), pltpu.VMEM((4 * CHUNK, CHUNK), f32)],
        compiler_params=_params(("arbitrary",)),
    )(u, u, xbc, dtb, alog, dsk, gn, *pre, yp, hst, da3)


def _swap16(t):
    lane = lax.broadcasted_iota(jnp.int32, t.shape, 1) % 64
    return jnp.where(lane < 8, pltpu.roll(t, TILE - 8, 1), jnp.where(lane < 16, pltpu.roll(t, 8, 1), 0.0))


def _rope(t, cos_t, sin_t):
    return t * cos_t + _swap16(t) * sin_t


def _rope_bwd(g, cos_t, sin_t):
    return g * cos_t + _swap16(g * sin_t)


def _att_in_specs(nb, rev):
    def bidx(n):
        return (nb - 1 - n) if rev else n
    return [
        pl.BlockSpec((ATT_NT, CHUNK, TILE), lambda n: (0, bidx(n), 0)),
        pl.BlockSpec((2, CHUNK, TILE), lambda n: (2, jnp.maximum(bidx(n) - 1, 0), 0)),
        pl.BlockSpec((CHUNK, TILE), lambda n: (bidx(n), 0)),
        pl.BlockSpec((CHUNK, TILE), lambda n: (bidx(n), 0)),
        pl.BlockSpec((CHUNK, TILE), lambda n: (jnp.maximum(bidx(n) - 1, 0), 0)),
        pl.BlockSpec((CHUNK, TILE), lambda n: (jnp.maximum(bidx(n) - 1, 0), 0)),
        pl.BlockSpec((1, 16), lambda n: (0, 0)),
    ]


ATT_SCALE = 0.125


def _tri4():
    row = lax.broadcasted_iota(jnp.int32, (ATT_ROWS, CHUNK), 0) % CHUNK
    col = lax.broadcasted_iota(jnp.int32, (ATT_ROWS, CHUNK), 1)
    return col <= row


def _stack_heads(ref):
    return jnp.concatenate([ref[:, 64 * r:64 * (r + 1)] for r in range(4)], axis=0)


def _sink_col(sinks, g):
    return [sinks[:, 4 * g + r:4 * g + r + 1] for r in range(4)]


def _softmax_rows(s_s, pn_s, pc_s, sink, tri, has_prev):
    sub = lax.broadcasted_iota(jnp.int32, (ATT_ROWS, 1), 0)
    sk = jnp.where(sub < CHUNK, sink[0], jnp.where(sub < 2 * CHUNK, sink[1], jnp.where(sub < 3 * CHUNK, sink[2], sink[3])))
    s = jnp.where(tri, s_s[:, CHUNK:2 * CHUNK], jnp.where(has_prev, s_s[:, 0:CHUNK], -jnp.inf)) * ATT_SCALE
    m = jnp.maximum(jnp.max(s, axis=-1, keepdims=True), sk)
    p = jnp.exp(s - m)
    e_sink = jnp.exp(sk - m)
    inv = 1.0 / (jnp.sum(p, axis=-1, keepdims=True) + e_sink)
    pn = p * inv
    pc_s[...] = pn
    pn_s[:, 0:CHUNK] = jnp.where(tri, 0.0, pn)
    pn_s[:, CHUNK:2 * CHUNK] = jnp.where(tri, pn, 0.0)
    return e_sink * inv


def _att2_fwd(u, cos_t, sin_t, sinks, name):
    t_len = u.shape[1]
    nb = t_len // CHUNK

    def body(u_ref, prev_ref, cc_ref, sc_ref, cp_ref, sp_ref, sink_ref, a_ref,
             q_s, kp_s, kc_s, vp_s, vc_s, o_s, s_s, pn_s, pc_s):
        n = pl.program_id(0)
        tri = _tri4()
        cos_c, sin_c = cc_ref[...], sc_ref[...]
        kc_s[...] = _rope(u_ref[4], cos_c, sin_c)
        kp_s[...] = _rope(prev_ref[0], cp_ref[...], sp_ref[...])
        vc_s[...] = u_ref[5]
        vp_s[...] = prev_ref[1]
        sinks = sink_ref[...]
        for g in range(4):
            q_g, o_g, s_g, pn_g = q_s.at[g], o_s.at[g], s_s.at[g], pn_s.at[g]
            q_g[...] = _rope(u_ref[g], cos_c, sin_c)
            kv = slice(64 * g, 64 * (g + 1))
            kb = jnp.concatenate([kp_s[:, kv], kc_s[:, kv]], axis=0)
            vb = jnp.concatenate([vp_s[:, kv], vc_s[:, kv]], axis=0)
            s_g[...] = _dot_nt(_stack_heads(q_g), kb)
            _softmax_rows(s_g, pn_g, pc_s.at[g], _sink_col(sinks, g), tri, n > 0)
            o = _dot(pn_g[...], vb)
            for r in range(4):
                o_g[:, 64 * r:64 * (r + 1)] = o[CHUNK * r:CHUNK * (r + 1), :]
            a_ref[g] = (o_g[...] * _silu(u_ref[6 + g])).astype(a_ref.dtype)

    return _pc(
        body, name, grid=(nb,),
        in_specs=_att_in_specs(nb, False),
        out_specs=pl.BlockSpec((ATT_KT, CHUNK, TILE), lambda n: (0, n, 0)),
        out_shape=SDS((ATT_KT, t_len, TILE), MXU_DTYPE),
        scratch_shapes=[pltpu.VMEM((4, CHUNK, TILE), f32)] + [pltpu.VMEM((CHUNK, TILE), f32)] * 4
                       + [pltpu.VMEM((4, CHUNK, TILE), f32)] + [pltpu.VMEM((4, ATT_ROWS, 2 * CHUNK), f32)] * 2
                       + [pltpu.VMEM((4, ATT_ROWS, CHUNK), f32)],
        compiler_params=_params(("arbitrary",)),
    )(u, u, cos_t, sin_t, cos_t, sin_t, sinks)


def _att2_bwd(u, cos_t, sin_t, sinks, da, name):
    t_len = u.shape[1]
    nb = t_len // CHUNK

    def body(u_ref, prev_ref, cc_ref, sc_ref, cp_ref, sp_ref, sink_ref, da_ref, du_ref, dsink_ref,
             ck_s, cv_s, kp_s, kc_s, vp_s, vc_s, q_s, o_s, do_s, dq_s, s_s, pn_s, dp_s, dkt_s, dvt_s, pc_s):
        step = pl.program_id(0)
        nn = nb - 1 - step

        @pl.when(step == 0)
        def _():
            ck_s[...] = jnp.zeros_like(ck_s)
            cv_s[...] = jnp.zeros_like(cv_s)
            dsink_ref[...] = jnp.zeros_like(dsink_ref)
        tri = _tri4()
        cos_c, sin_c = cc_ref[...], sc_ref[...]
        cos_p, sin_p = cp_ref[...], sp_ref[...]
        kc_s[...] = _rope(u_ref[4], cos_c, sin_c)
        kp_s[...] = _rope(prev_ref[0], cos_p, sin_p)
        vc_s[...] = u_ref[5]
        vp_s[...] = prev_ref[1]
        sinks = sink_ref[...]
        lane16 = lax.broadcasted_iota(jnp.int32, (1, 16), 1)
        dsink = jnp.zeros((1, 16), f32)
        for g in range(4):
            q_g, o_g, do_g, dq_g = q_s.at[g], o_s.at[g], do_s.at[g], dq_s.at[g]
            s_g, pn_g, dp_g, pc_g = s_s.at[g], pn_s.at[g], dp_s.at[g], pc_s.at[g]
            q_g[...] = _rope(u_ref[g], cos_c, sin_c)
            gate = u_ref[6 + g]
            sg = _sigmoid(gate)
            dav = da_ref[g]
            do_g[...] = dav * (gate * sg)
            kv = slice(64 * g, 64 * (g + 1))
            kb = jnp.concatenate([kp_s[:, kv], kc_s[:, kv]], axis=0)
            vb = jnp.concatenate([vp_s[:, kv], vc_s[:, kv]], axis=0)
            q_st = _stack_heads(q_g)
            do_st = _stack_heads(do_g)
            s_g[...] = _dot_nt(q_st, kb)
            p_sink = _softmax_rows(s_g, pn_g, pc_g, _sink_col(sinks, g), tri, nn > 0)
            o = _dot(pn_g[...], vb)
            dvt_s[64 * g:64 * (g + 1), :] = _dot_tn(do_st, pn_g[...])
            dp_g[...] = _dot_nt(do_st, vb)
            delta = jnp.sum(do_st * o, axis=-1, keepdims=True)
            dpc = jnp.where(tri, dp_g[:, CHUNK:2 * CHUNK], dp_g[:, 0:CHUNK])
            dsc = pc_g[...] * (dpc - delta) * ATT_SCALE
            dp_g[:, 0:CHUNK] = jnp.where(tri, 0.0, dsc)
            dp_g[:, CHUNK:2 * CHUNK] = jnp.where(tri, dsc, 0.0)
            sd = p_sink * delta
            for r in range(4):
                rs = slice(CHUNK * r, CHUNK * (r + 1))
                o_g[:, 64 * r:64 * (r + 1)] = o[rs, :]
                ds_h = -jnp.sum(sd[rs, :], axis=0, keepdims=True)
                dsink = dsink + ds_h * (lane16 == 4 * g + r).astype(f32)
            ds = dp_g[...]
            dq = _dot(ds, kb)
            for r in range(4):
                dq_g[:, 64 * r:64 * (r + 1)] = dq[CHUNK * r:CHUNK * (r + 1), :]
            dkt_s[64 * g:64 * (g + 1), :] = _dot_tn(q_st, ds)
            du_ref[6 + g] = dav * o_g[...] * (sg * (1.0 + gate * (1.0 - sg)))
            du_ref[g] = _rope_bwd(dq_g[...], cos_c, sin_c)
        dk = dkt_s[...].T
        dv = dvt_s[...].T
        du_ref[4] = _rope_bwd(dk[CHUNK:2 * CHUNK, :], cos_c, sin_c) + ck_s[...]
        du_ref[5] = dv[CHUNK:2 * CHUNK, :] + cv_s[...]
        ck_s[...] = _rope_bwd(dk[0:CHUNK, :], cos_p, sin_p)
        cv_s[...] = dv[0:CHUNK, :]
        dsink_ref[...] += dsink

    def rb(n):
        return nb - 1 - n

    return _pc(
        body, name, grid=(nb,),
        in_specs=_att_in_specs(nb, True) + [pl.BlockSpec((ATT_KT, CHUNK, TILE), lambda n: (0, rb(n), 0))],
        out_specs=[pl.BlockSpec((ATT_NT, CHUNK, TILE), lambda n: (0, rb(n), 0)),
                   pl.BlockSpec((1, 16), lambda n: (0, 0))],
        out_shape=[SDS((ATT_NT, t_len, TILE), f32), SDS((1, 16), f32)],
        scratch_shapes=[pltpu.VMEM((CHUNK, TILE), f32)] * 6 + [pltpu.VMEM((4, CHUNK, TILE), f32)] * 4
                       + [pltpu.VMEM((4, ATT_ROWS, 2 * CHUNK), f32)] * 3
                       + [pltpu.VMEM((2 * CHUNK, 2 * CHUNK), f32)] * 2 + [pltpu.VMEM((4, ATT_ROWS, CHUNK), f32)],
        compiler_params=_params(("arbitrary",)),
    )(u, u, cos_t, sin_t, cos_t, sin_t, sinks, da)


_HBM = pl.BlockSpec(memory_space=pltpu.HBM)


def _all_gather_big(shards):
    n = len(shards)

    def body(*refs):
        x_refs, out_refs = refs[:n], refs[n:2 * n]
        send_sems, recv_sems, local_sems = refs[2 * n:]
        x, y, c = lax.axis_index("x"), lax.axis_index("y"), lax.axis_index("c")
        me, sibling = (x, y, c), (x, y, 1 - c)
        chips = [(1 - x, y), (x, 1 - y), (1 - x, 1 - y)]

        def slot(i, px, py, pc):
            return out_refs[i].at[4 * px + 2 * py + pc]

        def copy(i, k, block, to, src=None):
            return pltpu.make_async_remote_copy(
                src_ref=slot(i, *block) if src is None else src, dst_ref=slot(i, *block),
                send_sem=send_sems.at[7 * i + k], recv_sem=recv_sems.at[7 * i + k], device_id=to, device_id_type=MESH)

        mine = [pltpu.make_async_copy(x_refs[i], slot(i, *me), local_sems.at[i]) for i in range(n)]
        for cp in mine:
            cp.start()
        first = []
        for i in range(n):
            first.append(copy(i, 0, me, sibling, src=x_refs[i]))
            first += [copy(i, 1 + j, me, (*chip, c), src=x_refs[i]) for j, chip in enumerate(chips)]
        for cp in first:
            cp.start()
        passed = []
        for j, chip in enumerate(chips):
            for i in range(n):
                copy(i, 1 + j, (*chip, c), me).wait_recv()
                fwd = copy(i, 4 + j, (*chip, c), sibling)
                fwd.start()
                passed.append(fwd)
        for i in range(n):
            copy(i, 0, sibling, me).wait_recv()
            for j, chip in enumerate(chips):
                copy(i, 4 + j, (*chip, 1 - c), me).wait_recv()
        for cp in first + passed:
            cp.wait_send()
        for cp in mine:
            cp.wait()

    return _pc(
        body, "all_gather_big",
        in_specs=[_HBM] * n, out_specs=[_HBM] * n,
        out_shape=[SDS((N_DEV,) + s.shape, s.dtype) for s in shards],
        scratch_shapes=[pltpu.SemaphoreType.DMA((7 * n,)), pltpu.SemaphoreType.DMA((7 * n,)),
                        pltpu.SemaphoreType.DMA((n,))],
    )(*shards)


def _all_gather_direct(block, name):
    rows, width = block.shape

    def body(x_ref, out_ref, send_sems, recv_sems, local_sem):
        x, y, c = lax.axis_index("x"), lax.axis_index("y"), lax.axis_index("c")
        my_slot = 4 * x + 2 * y + c

        def peer(k):
            return (1 - x if k & 4 else x, 1 - y if k & 2 else y, 1 - c if k & 1 else c)

        def copy(k):
            px, py, pc = peer(k)
            return pltpu.make_async_remote_copy(
                src_ref=x_ref, dst_ref=out_ref.at[my_slot], send_sem=send_sems.at[k - 1], recv_sem=recv_sems.at[k - 1],
                device_id=(px, py, pc), device_id_type=MESH)

        def arrival(k):
            px, py, pc = peer(k)
            return pltpu.make_async_remote_copy(
                src_ref=x_ref, dst_ref=out_ref.at[4 * px + 2 * py + pc], send_sem=send_sems.at[k - 1],
                recv_sem=recv_sems.at[k - 1], device_id=(px, py, pc), device_id_type=MESH)

        mine = pltpu.make_async_copy(x_ref, out_ref.at[my_slot], local_sem)
        mine.start()
        for k in range(1, N_DEV):
            copy(k).start()
        for k in range(1, N_DEV):
            arrival(k).wait_recv()
        for k in range(1, N_DEV):
            copy(k).wait_send()
        mine.wait()

    return _pc(
        body, name,
        in_specs=[_HBM], out_specs=_HBM,
        out_shape=SDS((N_DEV, rows, width), block.dtype),
        scratch_shapes=[pltpu.SemaphoreType.DMA((7,)), pltpu.SemaphoreType.DMA((7,)), pltpu.SemaphoreType.DMA],
    )(block)


N_CHIP = N_DEV // 2


def _exchange_sibling(gs):
    n = len(gs)

    def body(*refs):
        g_refs, out_refs = refs[:n], refs[n:2 * n]
        send_sems, recv_sems = refs[2 * n:]
        x, y, c = lax.axis_index("x"), lax.axis_index("y"), lax.axis_index("c")
        cps = [pltpu.make_async_remote_copy(
            src_ref=g_refs[i].at[2 * k + 1 - c], dst_ref=out_refs[i].at[k], send_sem=send_sems.at[N_CHIP * i + k],
            recv_sem=recv_sems.at[N_CHIP * i + k], device_id=(x, y, 1 - c), device_id_type=MESH)
            for i in range(n) for k in range(N_CHIP)]
        for cp in cps:
            cp.start()
        for cp in cps:
            cp.wait()

    return _pc(
        body, "rs_sibling",
        in_specs=[_HBM] * n, out_specs=[_HBM] * n,
        out_shape=[SDS((N_CHIP,) + g.shape[1:], g.dtype) for g in gs],
        scratch_shapes=[pltpu.SemaphoreType.DMA((N_CHIP * n,)), pltpu.SemaphoreType.DMA((N_CHIP * n,))],
    )(*gs)


def _pair_sum(g, r1, cidx, tr, name):
    _, rows, width = g.shape

    def body(c_ref, g_ref, r_ref, o_ref):
        o_ref[...] = (g_ref[...].astype(f32) + r_ref[...].astype(f32)).astype(o_ref.dtype)

    return pl.pallas_call(
        body, name=name,
        grid_spec=pltpu.PrefetchScalarGridSpec(
            num_scalar_prefetch=1, grid=(N_CHIP, rows // tr),
            in_specs=[pl.BlockSpec((1, tr, width), lambda k, i, c_ref: (2 * k + c_ref[0], i, 0)),
                      pl.BlockSpec((1, tr, width), lambda k, i, c_ref: (k, i, 0))],
            out_specs=pl.BlockSpec((1, tr, width), lambda k, i, c_ref: (k, i, 0))),
        out_shape=SDS((N_CHIP, rows, width), g.dtype),
        compiler_params=_params(("arbitrary", "arbitrary")),
    )(cidx, g, r1)


def _exchange_chips(ps):
    n = len(ps)

    def body(*refs):
        p_refs, out_refs = refs[:n], refs[n:2 * n]
        send_sems, recv_sems, local_sems = refs[2 * n:]
        x, y, c = lax.axis_index("x"), lax.axis_index("y"), lax.axis_index("c")
        my_chip = 2 * x + y
        chips = [(1 - x, y), (x, 1 - y), (1 - x, 1 - y)]

        def copy(i, j):
            px, py = chips[j]
            return pltpu.make_async_remote_copy(
                src_ref=p_refs[i].at[2 * px + py], dst_ref=out_refs[i].at[my_chip], send_sem=send_sems.at[3 * i + j],
                recv_sem=recv_sems.at[3 * i + j], device_id=(px, py, c), device_id_type=MESH)

        def arrival(i, j):
            px, py = chips[j]
            return pltpu.make_async_remote_copy(
                src_ref=p_refs[i].at[my_chip], dst_ref=out_refs[i].at[2 * px + py], send_sem=send_sems.at[3 * i + j],
                recv_sem=recv_sems.at[3 * i + j], device_id=(px, py, c), device_id_type=MESH)

        mine = [pltpu.make_async_copy(p_refs[i].at[my_chip], out_refs[i].at[my_chip], local_sems.at[i])
                for i in range(n)]
        for cp in mine:
            cp.start()
        for i in range(n):
            for j in range(3):
                copy(i, j).start()
        for i in range(n):
            for j in range(3):
                arrival(i, j).wait_recv()
        for i in range(n):
            for j in range(3):
                copy(i, j).wait_send()
        for cp in mine:
            cp.wait()

    return _pc(
        body, "rs_chips",
        in_specs=[_HBM] * n, out_specs=[_HBM] * n,
        out_shape=[SDS(p.shape, p.dtype) for p in ps],
        scratch_shapes=[pltpu.SemaphoreType.DMA((3 * n,)), pltpu.SemaphoreType.DMA((3 * n,)),
                        pltpu.SemaphoreType.DMA((n,))],
    )(*ps)


def _adamw(parts, w, m, v, tr, name):
    n, rows, width = parts.shape
    c1 = 1.0 / (1.0 - ADAM_B1 ** ADAM_STEP)
    c2 = 1.0 / (1.0 - ADAM_B2 ** ADAM_STEP)

    def body(p_ref, w_ref, m_ref, v_ref, g_ref, d_ref, mo_ref, vo_ref):
        g = p_ref[0].astype(f32)
        for k in range(1, n):
            g = g + p_ref[k].astype(f32)
        mn = ADAM_B1 * m_ref[...] + (1.0 - ADAM_B1) * g
        vn = ADAM_B2 * v_ref[...] + (1.0 - ADAM_B2) * (g * g)
        g_ref[...] = g
        mo_ref[...] = mn
        vo_ref[...] = vn
        d_ref[...] = -ADAM_LR * ((mn * c1) / (jnp.sqrt(vn * c2) + ADAM_EPS) + ADAM_WD * w_ref[...])

    blk = pl.BlockSpec((tr, width), lambda i: (i, 0))
    return _pc(
        body, name, grid=(rows // tr,),
        in_specs=[pl.BlockSpec((n, tr, width), lambda i: (0, i, 0)), blk, blk, blk],
        out_specs=[blk, blk, blk, blk],
        out_shape=[SDS((rows, width), f32)] * 4,
        compiler_params=_params(("arbitrary",)),
    )(parts, w, m, v)


ROWS_REST = ROWS_SSM_OUT + ROWS_ATT_OUT + 16


def _pack_rest(ssm_w_out, att_w_out, conv_w):
    conv = jnp.pad(conv_w.reshape(4, 1024), ((0, 12), (0, 0)))
    return jnp.concatenate([ssm_w_out.reshape(ROWS_SSM_OUT, 1024), att_w_out.reshape(ROWS_ATT_OUT, 1024), conv], axis=0)


def _unpack_rest(p):
    o = ROWS_SSM_OUT + ROWS_ATT_OUT
    return (p[0:ROWS_SSM_OUT].reshape(2, 256, 1024), p[ROWS_SSM_OUT:o].reshape(2, 128, 1024),
            p[o:o + 4].reshape(2, 4, 512))


def _pack_grads(d_ssm_w_in, d_ssm_w_out, d_att_w_in, d_att_w_out, d_conv_w):
    wire = lambda t: t.astype(MXU_DTYPE)
    a = jnp.transpose(wire(d_ssm_w_in).reshape(2, 1024, 8, 772), (2, 0, 1, 3)).reshape(8, 2048, 772)
    c = jnp.transpose(wire(d_att_w_in).reshape(2, 1024, 8, 320), (2, 0, 1, 3)).reshape(8, 2048, 320)
    b = jnp.transpose(wire(d_ssm_w_out).reshape(2, 8, 256, 1024), (1, 0, 2, 3)).reshape(8, ROWS_SSM_OUT, 1024)
    d = jnp.transpose(wire(d_att_w_out).reshape(2, 8, 128, 1024), (1, 0, 2, 3)).reshape(8, ROWS_ATT_OUT, 1024)
    e = jnp.transpose(wire(d_conv_w).reshape(2, 4, 8, 512), (2, 0, 1, 3)).reshape(8, 4, 1024)
    e = jnp.pad(e, ((0, 0), (0, 12), (0, 0)))
    return a, c, jnp.concatenate([b, d, e], axis=1)


def _pad8(a):
    return jnp.pad(a, ((0, 8 - a.shape[0]), (0, 0)))


def _pack_small(pre_norm, post_norm, conv_b, gate_norm, dt_bias, a_log, d_skip, sinks, extra=None):
    row = jnp.concatenate([dt_bias.reshape(1, 64), a_log.reshape(1, 64), d_skip.reshape(1, 64), sinks.reshape(1, 32),
                           jnp.zeros((1, 1024 - 224), f32)], axis=1)
    if extra is not None:
        row = row + jnp.pad(extra.reshape(1, 1), ((0, 0), (224, 1024 - 225)))
    return jnp.concatenate([_pad8(pre_norm.reshape(4, 1024)), _pad8(post_norm.reshape(4, 1024)),
                            conv_b.reshape(8, 1024), _pad8(gate_norm.reshape(4, 1024)), _pad8(row)], axis=0)


def _unpack_small(p):
    row = p[32]
    return (p[0:4], p[8:12], p[16:24].reshape(2, 4096), row[0:64].reshape(2, 32), row[64:128].reshape(2, 32),
            row[128:192].reshape(2, 32), p[24:28].reshape(2, 2048), row[192:224].reshape(2, 16))


def _ssm_w_in_tiles(w):
    wb = w[:, 4096:5120].reshape(1024, 8, 128)
    wc = w[:, 5120:6144].reshape(1024, 8, 128)
    wbc = jnp.concatenate([wb, wc], axis=2).reshape(1024, 2048)
    return jnp.concatenate([w[:, 0:4096], wbc, w[:, 6144:6176], jnp.zeros((1024, 224), w.dtype)], axis=1)


def _ssm_w_in_untile(dw):
    dbc = dw[:, 4096:6144].reshape(1024, 8, 256)
    return jnp.concatenate([dw[:, 0:4096], dbc[:, :, 0:128].reshape(1024, 1024), dbc[:, :, 128:256].reshape(1024, 1024),
                            dw[:, 6144:6176]], axis=1)


def _conv_tiles(cw):
    k = cw.shape[0]
    xs = jnp.transpose(cw[:, 0:2048].reshape(k, 8, 256), (1, 0, 2))
    b = cw[:, 2048:3072].reshape(k, 8, 128)
    c = cw[:, 3072:4096].reshape(k, 8, 128)
    bc = jnp.transpose(jnp.concatenate([b, c], axis=2), (1, 0, 2))
    return jnp.concatenate([xs, bc], axis=0)


def _conv_untile(t):
    k = t.shape[1]
    xs = jnp.transpose(t[0:8], (1, 0, 2)).reshape(k, 2048)
    bc = jnp.transpose(t[8:16], (1, 0, 2))
    return jnp.concatenate([xs, bc[:, :, 0:128].reshape(k, 1024), bc[:, :, 128:256].reshape(k, 1024)], axis=1)


def _rope_tables(positions):
    inv = ROPE_THETA ** (-jnp.arange(0, 16, 2, dtype=f32) / 16)
    ang = positions.astype(f32).reshape(-1, 1) * inv
    cos, sin = jnp.cos(ang), jnp.sin(ang)
    t_len = ang.shape[0]
    cos64 = jnp.concatenate([cos, cos, jnp.ones((t_len, 48), f32)], axis=1)
    sin64 = jnp.concatenate([-sin, sin, jnp.zeros((t_len, 48), f32)], axis=1)
    return jnp.tile(cos64, (1, 4)), jnp.tile(sin64, (1, 4))


def kernel(x, positions, pre_norm, post_norm, ssm_w_in, ssm_conv_w, ssm_conv_b, ssm_dt_bias, ssm_a_log, ssm_d, ssm_gate_norm, ssm_w_out, att_w_in, att_sinks, att_w_out, loss_target, m_pre_norm, m_post_norm, m_ssm_w_in, m_ssm_conv_w, m_ssm_conv_b, m_ssm_dt_bias, m_ssm_a_log, m_ssm_d, m_ssm_gate_norm, m_ssm_w_out, m_att_w_in, m_att_sinks, m_att_w_out, v_pre_norm, v_post_norm, v_ssm_w_in, v_ssm_conv_w, v_ssm_conv_b, v_ssm_dt_bias, v_ssm_a_log, v_ssm_d, v_ssm_gate_norm, v_ssm_w_out, v_att_w_in, v_att_sinks, v_att_w_out):
    t_len = x.shape[1]
    tm = min(1024, t_len)
    xin = x.reshape(t_len, D_MODEL)
    tgt = loss_target.reshape(t_len, D_MODEL)
    cidx = lax.axis_index("c").astype(jnp.int32).reshape(1)

    g_ssm_in, g_att_in, g_ssm_out, g_att_out = _all_gather_big(
        [ssm_w_in.astype(MXU_DTYPE), att_w_in.astype(MXU_DTYPE), ssm_w_out.astype(MXU_DTYPE),
         att_w_out.astype(MXU_DTYPE)])
    conv_local = jnp.concatenate([ssm_conv_w.reshape(4, 1024), jnp.zeros((4, 1024), f32)], axis=0)
    conv_all = _all_gather_direct(conv_local, "all_gather_conv")[:, 0:4]
    w_ssm_in = jnp.transpose(g_ssm_in, (1, 2, 0, 3)).reshape(2, 1024, SSM_IN)
    w_ssm_out = jnp.transpose(g_ssm_out, (1, 0, 2, 3)).reshape(2, SSM_INNER, 1024)
    w_att_in = jnp.transpose(g_att_in, (1, 2, 0, 3)).reshape(2, 1024, ATT_IN)
    w_att_out = jnp.transpose(g_att_out, (1, 0, 2, 3)).reshape(2, 1024, 1024)
    conv_w = jnp.transpose(conv_all.reshape(8, 2, 4, 512), (1, 2, 0, 3)).reshape(2, 4, 4096)
    cos_t, sin_t = _rope_tables(positions)

    saved = []
    xc = xin
    for i in range(4):
        j = i // 2
        wn_pre, wn_post = pre_norm[i].reshape(1, D_MODEL), post_norm[i].reshape(1, D_MODEL)
        if i % 2 == 0:
            w_in = _ssm_w_in_tiles(w_ssm_in[j])
            cw, cb = _conv_tiles(conv_w[j]), _conv_tiles(ssm_conv_b[j].reshape(1, 4096))
            dtb, alog, dsk = ssm_dt_bias[j].reshape(1, 32), ssm_a_log[j].reshape(1, 32), ssm_d[j].reshape(1, 32)
            gn = ssm_gate_norm[j].reshape(SSM_KT, 1, TILE)
            u, h = _mm_in(xc, wn_pre, w_in, 5, tm, f"ssm_in_{j}")
            xbc = _conv_fwd(u, cw, cb, f"ssm_conv_{j}")
            pre = _ssm_pre(u, dtb, alog, dsk, f"ssm_pre_{j}")
            a3, yp, hst = _ssm2_fwd(u, xbc, pre, dtb, alog, dsk, gn, f"ssm_core_{j}")
            y, xn = _mm_out(a3, w_ssm_out[j], xc, wn_post, 4, tm, f"ssm_out_{j}")
            saved.append(dict(x=xc, u=u, h=h, a=a3, yp=yp, hst=hst, y=y, w_in=w_in, cw=cw, cb=cb, dtb=dtb, alog=alog,
                              dsk=dsk, gn=gn, xbc=xbc, pre=pre))
        else:
            sinks = att_sinks[j].reshape(1, 16)
            u, h = _mm_in(xc, wn_pre, w_att_in[j], 5, tm, f"att_in_{j}")
            a = _att2_fwd(u, cos_t, sin_t, sinks, f"att_core_{j}")
            y, xn = _mm_out(a, w_att_out[j], xc, wn_post, 4, tm, f"att_out_{j}")
            saved.append(dict(x=xc, u=u, h=h, a=a, y=y, sinks=sinks))
        xc = xn

    dx, loss_part = _loss_grad(xc, tgt, tm)

    d_pre, d_post = [None] * 4, [None] * 4
    d_ssm_in, d_ssm_out, d_att_in, d_att_out = [None] * 2, [None] * 2, [None] * 2, [None] * 2
    d_cw, d_cb, d_dtb, d_alog, d_dsk, d_gn, d_sinks = ([None] * 2 for _ in range(7))
    for i in reversed(range(4)):
        j = i // 2
        s = saved[i]
        wn_pre, wn_post = pre_norm[i].reshape(1, D_MODEL), post_norm[i].reshape(1, D_MODEL)
        if i % 2 == 0:
            da3, dy, d_post[i] = _mm_dout(s["y"], dx, wn_post, w_ssm_out[j], 4, tm, f"ssm_dout_{j}")
            d_ssm_out[j] = _dw_rows(s["a"], dy, 4, tm, f"ssm_dwout_{j}")
            du, d_dtb[j], d_alog[j], d_dsk[j], dgn = _ssm2_bwd(
                s["u"], s["xbc"], s["pre"], s["yp"], s["hst"], da3, s["dtb"], s["alog"], s["dsk"], s["gn"],
                f"ssm_core_bwd_{j}")
            du, dcw, dcb = _conv_bwd(s["u"], du, s["cw"], s["cb"], f"ssm_conv_bwd_{j}")
            d_cw[j], d_cb[j], d_gn[j] = _conv_untile(dcw), _conv_untile(dcb), dgn.reshape(1, SSM_INNER)
            d_ssm_in[j] = _ssm_w_in_untile(_dw_cols(s["h"], du, 5, tm, f"ssm_dwin_{j}"))
            dx, d_pre[i] = _mm_dh(du, s["w_in"], s["x"], dx, wn_pre, 5, tm, f"ssm_dh_{j}")
        else:
            da, dy, d_post[i] = _mm_dout(s["y"], dx, wn_post, w_att_out[j], 4, tm, f"att_dout_{j}")
            d_att_out[j] = _dw_rows(s["a"], dy, 4, tm, f"att_dwout_{j}")
            du, d_sinks[j] = _att2_bwd(s["u"], cos_t, sin_t, s["sinks"], da, f"att_core_bwd_{j}")
            d_att_in[j] = _dw_cols(s["h"], du, 5, tm, f"att_dwin_{j}")
            dx, d_pre[i] = _mm_dh(du, w_att_in[j], s["x"], dx, wn_pre, 5, tm, f"att_dh_{j}")

    gs = _pack_grads(jnp.stack(d_ssm_in), jnp.stack(d_ssm_out), jnp.stack(d_att_in), jnp.stack(d_att_out),
                     jnp.stack(d_cw))
    r1 = _exchange_sibling(gs)
    tiles = (256, 256, ROWS_REST // 7)
    pairs = [_pair_sum(g, r, cidx, tr, f"rs_pair_sum_{k}") for k, (g, r, tr) in enumerate(zip(gs, r1, tiles))]
    parts = _exchange_chips(pairs)
    flat = lambda t: t.reshape(2048, t.shape[-1])
    a4 = _adamw(parts[0], flat(ssm_w_in), flat(m_ssm_w_in), flat(v_ssm_w_in), tiles[0], "adamw_ssm_in")
    b4 = _adamw(parts[1], flat(att_w_in), flat(m_att_w_in), flat(v_att_w_in), tiles[1], "adamw_att_in")
    c4 = _adamw(parts[2], _pack_rest(ssm_w_out, att_w_out, ssm_conv_w), _pack_rest(m_ssm_w_out, m_att_w_out, m_ssm_conv_w),
                _pack_rest(v_ssm_w_out, v_att_w_out, v_ssm_conv_w), tiles[2], "adamw_rest")
    big = []
    for k in range(4):
        o_ssm_out, o_att_out, o_conv = _unpack_rest(c4[k])
        big.append((a4[k].reshape(2, 1024, 772), o_ssm_out, b4[k].reshape(2, 1024, 320), o_att_out, o_conv))

    small_local = _pack_small(jnp.concatenate(d_pre, axis=0), jnp.concatenate(d_post, axis=0),
                              jnp.concatenate(d_cb, axis=0), jnp.concatenate(d_gn, axis=0),
                              jnp.concatenate(d_dtb, axis=0), jnp.concatenate(d_alog, axis=0),
                              jnp.concatenate(d_dsk, axis=0), jnp.concatenate(d_sinks, axis=0), loss_part[0, 0])
    small_all = _all_gather_direct(small_local, "all_gather_small")
    ws = _pack_small(pre_norm, post_norm, ssm_conv_b, ssm_gate_norm, ssm_dt_bias, ssm_a_log, ssm_d, att_sinks)
    ms = _pack_small(m_pre_norm, m_post_norm, m_ssm_conv_b, m_ssm_gate_norm, m_ssm_dt_bias, m_ssm_a_log, m_ssm_d,
                     m_att_sinks)
    vs = _pack_small(v_pre_norm, v_post_norm, v_ssm_conv_b, v_ssm_gate_norm, v_ssm_dt_bias, v_ssm_a_log, v_ssm_d,
                     v_att_sinks)
    small4 = _adamw(small_all, ws, ms, vs, ROWS_SMALL, "adamw_small")
    loss = small4[0][32, 224]
    small = [_unpack_small(t) for t in small4]

    outs = [loss, dx.reshape(1, t_len, D_MODEL)]
    for k in range(4):
        b_ssm_in, b_ssm_out, b_att_in, b_att_out, b_conv = big[k]
        s_pre, s_post, s_cb, s_dtb, s_alog, s_d, s_gn, s_sinks = small[k]
        outs += [s_pre, s_post, b_ssm_in, b_conv, s_cb, s_dtb, s_alog, s_d, s_gn, b_ssm_out, b_att_in, s_sinks,
                 b_att_out]
    return tuple(outs)
```

```python
import jax
import jax.numpy as jnp
from jax import lax
from jax.experimental import pallas as pl
from jax.experimental.pallas import tpu as pltpu

f32 = jnp.float32
MXU_DTYPE = jnp.bfloat16
SDS = jax.ShapeDtypeStruct
MESH = pl.DeviceIdType.MESH

D_MODEL = 1024
EPS = 1e-6
TILE = 256
CHUNK = 128
ATT_ROWS = 4 * CHUNK
SSM_HEADS = 32
SSM_GROUPS = 8
SSM_P = 64
SSM_N = 128
SSM_INNER = 2048
SSM_IN = 6176
SSM_NT = 25
SSM_KT = 8
ATT_NT = 10
ATT_KT = 4
ATT_IN = 2560
ROPE_THETA = 500000.0
N_DEV = 8
VMEM_LIMIT = 56 * 1024 * 1024

ADAM_LR = 0.001
ADAM_B1 = 0.9
ADAM_B2 = 0.999
ADAM_EPS = 1e-08
ADAM_WD = 0.01
ADAM_STEP = 10

ROWS_SSM_OUT = 2 * 256
ROWS_ATT_OUT = 2 * 128
ROWS_SMALL = 40


def _pc(body, name, **kw):
    return pl.pallas_call(body, name=name, **kw)


def _params(sem):
    return pltpu.CompilerParams(dimension_semantics=sem, vmem_limit_bytes=VMEM_LIMIT)


def _sigmoid(x):
    return 1.0 / (1.0 + jnp.exp(-x))


def _silu(x):
    return x * _sigmoid(x)


def _softplus(x):
    return jnp.maximum(x, 0.0) + jnp.log(1.0 + jnp.exp(-jnp.abs(x)))


def _mx(x):
    return x.astype(MXU_DTYPE)


def _dot(a, b):
    return jnp.dot(_mx(a), _mx(b), preferred_element_type=f32)


def _dot_nt(a, b):
    return lax.dot_general(_mx(a), _mx(b), (((1,), (1,)), ((), ())), preferred_element_type=f32)


def _dot_tn(a, b):
    return lax.dot_general(_mx(a), _mx(b), (((0,), (0,)), ((), ())), preferred_element_type=f32)


def _rms_fwd(x, w):
    r = lax.rsqrt(jnp.mean(x * x, axis=-1, keepdims=True) + EPS)
    return x * r * w


def _rms_bwd(x, w, dy):
    r = lax.rsqrt(jnp.mean(x * x, axis=-1, keepdims=True) + EPS)
    xh = x * r
    dw = jnp.sum(dy * xh, axis=0, keepdims=True)
    g = dy * w
    dx = r * (g - xh * jnp.mean(g * xh, axis=-1, keepdims=True))
    return dx, dw


def _mm_in(x, wn, w, ntb, tm, name):
    t_len, d = x.shape
    nt = w.shape[1] // TILE

    def body(x_ref, wn_ref, w_ref, u_ref, h_ref):
        @pl.when(pl.program_id(1) == 0)
        def _():
            h_ref[...] = _rms_fwd(x_ref[...], wn_ref[...]).astype(h_ref.dtype)
        h = h_ref[...]
        for t in range(ntb):
            u_ref[t] = jnp.dot(h, w_ref[:, TILE * t:TILE * (t + 1)], preferred_element_type=f32)

    return _pc(
        body, name, grid=(t_len // tm, nt // ntb),
        in_specs=[pl.BlockSpec((tm, d), lambda i, j: (i, 0)),
                  pl.BlockSpec((1, d), lambda i, j: (0, 0)),
                  pl.BlockSpec((d, ntb * TILE), lambda i, j: (0, j))],
        out_specs=[pl.BlockSpec((ntb, tm, TILE), lambda i, j: (j, i, 0)),
                   pl.BlockSpec((tm, d), lambda i, j: (i, 0))],
        out_shape=[SDS((nt, t_len, TILE), f32), SDS((t_len, d), MXU_DTYPE)],
        compiler_params=_params(("arbitrary", "arbitrary")),
    )(x, wn, w)


def _mm_dout(y, dxn, wn, w, ntb, tm, name):
    t_len, d = y.shape
    nt = w.shape[0] // TILE

    def body(y_ref, dxn_ref, wn_ref, w_ref, da_ref, dy_ref, dwn_ref):
        i, j = pl.program_id(0), pl.program_id(1)

        @pl.when((i == 0) & (j == 0))
        def _():
            dwn_ref[...] = jnp.zeros_like(dwn_ref)

        @pl.when(j == 0)
        def _():
            dy, dw = _rms_bwd(y_ref[...], wn_ref[...], dxn_ref[...])
            dy_ref[...] = dy.astype(dy_ref.dtype)
            dwn_ref[...] += dw
        dy = dy_ref[...]
        for t in range(ntb):
            da_ref[t] = _dot_nt(dy, w_ref[TILE * t:TILE * (t + 1), :])

    return _pc(
        body, name, grid=(t_len // tm, nt // ntb),
        in_specs=[pl.BlockSpec((tm, d), lambda i, j: (i, 0)),
                  pl.BlockSpec((tm, d), lambda i, j: (i, 0)),
                  pl.BlockSpec((1, d), lambda i, j: (0, 0)),
                  pl.BlockSpec((ntb * TILE, d), lambda i, j: (j, 0))],
        out_specs=[pl.BlockSpec((ntb, tm, TILE), lambda i, j: (j, i, 0)),
                   pl.BlockSpec((tm, d), lambda i, j: (i, 0)),
                   pl.BlockSpec((1, d), lambda i, j: (0, 0))],
        out_shape=[SDS((nt, t_len, TILE), f32), SDS((t_len, d), MXU_DTYPE), SDS((1, d), f32)],
        compiler_params=_params(("arbitrary", "arbitrary")),
    )(y, dxn, wn, w)


def _mm_out(a, w, x, wn, ktb, tm, name):
    kt, t_len, _ = a.shape
    d = w.shape[1]
    nk = kt // ktb

    def body(a_ref, w_ref, x_ref, wn_ref, y_ref, xn_ref, acc):
        k = pl.program_id(1)

        @pl.when(k == 0)
        def _():
            acc[...] = jnp.zeros_like(acc)
        s = acc[...]
        for t in range(ktb):
            s = s + jnp.dot(a_ref[t], w_ref[TILE * t:TILE * (t + 1), :], preferred_element_type=f32)
        acc[...] = s

        @pl.when(k == nk - 1)
        def _():
            y = acc[...]
            y_ref[...] = y
            xn_ref[...] = x_ref[...] + _rms_fwd(y, wn_ref[...])

    return _pc(
        body, name, grid=(t_len // tm, nk),
        in_specs=[pl.BlockSpec((ktb, tm, TILE), lambda i, k: (k, i, 0)),
                  pl.BlockSpec((ktb * TILE, d), lambda i, k: (k, 0)),
                  pl.BlockSpec((tm, d), lambda i, k: (i, 0)),
                  pl.BlockSpec((1, d), lambda i, k: (0, 0))],
        out_specs=[pl.BlockSpec((tm, d), lambda i, k: (i, 0)),
                   pl.BlockSpec((tm, d), lambda i, k: (i, 0))],
        out_shape=[SDS((t_len, d), f32), SDS((t_len, d), f32)],
        scratch_shapes=[pltpu.VMEM((tm, d), f32)],
        compiler_params=_params(("arbitrary", "arbitrary")),
    )(a, w, x, wn)


def _mm_dh(du, w, x, dxn, wn, ktb, tm, name):
    kt, t_len, _ = du.shape
    d = w.shape[0]
    nk = kt // ktb

    def body(du_ref, w_ref, x_ref, dxn_ref, wn_ref, dx_ref, dwn_ref, acc):
        i, k = pl.program_id(0), pl.program_id(1)

        @pl.when((i == 0) & (k == 0))
        def _():
            dwn_ref[...] = jnp.zeros_like(dwn_ref)

        @pl.when(k == 0)
        def _():
            acc[...] = jnp.zeros_like(acc)
        s = acc[...]
        for t in range(ktb):
            s = s + _dot_nt(du_ref[t], w_ref[:, TILE * t:TILE * (t + 1)])
        acc[...] = s

        @pl.when(k == nk - 1)
        def _():
            dxp, dw = _rms_bwd(x_ref[...], wn_ref[...], acc[...])
            dx_ref[...] = dxn_ref[...] + dxp
            dwn_ref[...] += dw

    return _pc(
        body, name, grid=(t_len // tm, nk),
        in_specs=[pl.BlockSpec((ktb, tm, TILE), lambda i, k: (k, i, 0)),
                  pl.BlockSpec((d, ktb * TILE), lambda i, k: (0, k)),
                  pl.BlockSpec((tm, d), lambda i, k: (i, 0)),
                  pl.BlockSpec((tm, d), lambda i, k: (i, 0)),
                  pl.BlockSpec((1, d), lambda i, k: (0, 0))],
        out_specs=[pl.BlockSpec((tm, d), lambda i, k: (i, 0)),
                   pl.BlockSpec((1, d), lambda i, k: (0, 0))],
        out_shape=[SDS((t_len, d), f32), SDS((1, d), f32)],
        scratch_shapes=[pltpu.VMEM((tm, d), f32)],
        compiler_params=_params(("arbitrary", "arbitrary")),
    )(du, w, x, dxn, wn)


def _dw_cols(a, b, ntb, tk, name):
    t_len, kdim = a.shape
    nt = b.shape[0]

    def body(a_ref, b_ref, o_ref):
        @pl.when(pl.program_id(1) == 0)
        def _():
            o_ref[...] = jnp.zeros_like(o_ref)
        av = a_ref[...]
        for s in range(ntb):
            o_ref[:, TILE * s:TILE * (s + 1)] += _dot_tn(av, b_ref[s])

    return _pc(
        body, name, grid=(nt // ntb, t_len // tk),
        in_specs=[pl.BlockSpec((tk, kdim), lambda j, t: (t, 0)),
                  pl.BlockSpec((ntb, tk, TILE), lambda j, t: (j, t, 0))],
        out_specs=pl.BlockSpec((kdim, ntb * TILE), lambda j, t: (0, j)),
        out_shape=SDS((kdim, nt * TILE), f32),
        compiler_params=_params(("arbitrary", "arbitrary")),
    )(a, b)


def _dw_rows(a, b, ktb, tk, name):
    kt, t_len, _ = a.shape
    d = b.shape[1]

    def body(a_ref, b_ref, o_ref):
        @pl.when(pl.program_id(1) == 0)
        def _():
            o_ref[...] = jnp.zeros_like(o_ref)
        bv = b_ref[...]
        for s in range(ktb):
            o_ref[TILE * s:TILE * (s + 1), :] += _dot_tn(a_ref[s], bv)

    return _pc(
        body, name, grid=(kt // ktb, t_len // tk),
        in_specs=[pl.BlockSpec((ktb, tk, TILE), lambda k, t: (k, t, 0)),
                  pl.BlockSpec((tk, d), lambda k, t: (t, 0))],
        out_specs=pl.BlockSpec((ktb * TILE, d), lambda k, t: (k, 0)),
        out_shape=SDS((kt * TILE, d), f32),
        compiler_params=_params(("arbitrary", "arbitrary")),
    )(a, b)


def _loss_grad(x, tgt, tm):
    t_len, d = x.shape

    def body(x_ref, t_ref, dx_ref, l_ref):
        @pl.when(pl.program_id(0) == 0)
        def _():
            l_ref[...] = jnp.zeros_like(l_ref)
        e = x_ref[...] - t_ref[...]
        dx_ref[...] = e * (1.0 / d)
        row = jnp.mean(e * e, axis=-1, keepdims=True)
        l_ref[...] += 0.5 * jnp.sum(row, axis=0, keepdims=True)

    return _pc(
        body, "loss_grad", grid=(t_len // tm,),
        in_specs=[pl.BlockSpec((tm, d), lambda i: (i, 0)), pl.BlockSpec((tm, d), lambda i: (i, 0))],
        out_specs=[pl.BlockSpec((tm, d), lambda i: (i, 0)), pl.BlockSpec((1, 128), lambda i: (0, 0))],
        out_shape=[SDS((t_len, d), f32), SDS((1, 128), f32)],
        compiler_params=_params(("arbitrary",)),
    )(x, tgt)


def _tri(lower):
    r = lax.broadcasted_iota(jnp.int32, (CHUNK, CHUNK), 0)
    c = lax.broadcasted_iota(jnp.int32, (CHUNK, CHUNK), 1)
    return ((c <= r) if lower else (c >= r)).astype(f32)


def _split(x, n):
    parts = []
    for _ in range(n):
        p = x.astype(jnp.bfloat16)
        parts.append(p)
        x = x - p.astype(f32)
    return parts


def _dot_exact(a, b, dims, split_a, n=3):
    out = None
    if split_a:
        b = b.astype(jnp.bfloat16)
        for p in _split(a, n):
            t = lax.dot_general(p, b, (dims, ((), ())), preferred_element_type=f32)
            out = t if out is None else out + t
    else:
        a = a.astype(jnp.bfloat16)
        for p in _split(b, n):
            t = lax.dot_general(a, p, (dims, ((), ())), preferred_element_type=f32)
            out = t if out is None else out + t
    return out


def _dt_path(dt_raw, dtb, alog):
    dtr = dt_raw + dtb
    dt = _softplus(dtr)
    a_neg = -jnp.exp(alog)
    a = dt * a_neg
    acs = _dot_exact(_tri(True), a, ((1,), (0,)), False)
    acs_t = _dot_exact(a, _tri(False), ((0,), (0,)), True)
    return dtr, dt, a_neg, acs, acs_t


CONV_ROWS = 1024
CONV_SUB = 32


def _conv_specs(nb, rows, rev):
    def ridx(i):
        return (nb - 1 - i) if rev else i
    return [
        pl.BlockSpec((1, rows, TILE), lambda p, i: (p, ridx(i), 0)),
        pl.BlockSpec((1, 8, TILE), lambda p, i: (p, jnp.maximum(ridx(i) * (rows // 8) - 1, 0), 0)),
        pl.BlockSpec((1, 4, TILE), lambda p, i: (p, 0, 0)),
        pl.BlockSpec((1, 1, TILE), lambda p, i: (p, 0, 0)),
    ]


CONV_NTB = 4


def _mm_in_conv(x, wn, w, cw, cb, tm, name):
    t_len, d = x.shape
    nt = w.shape[1] // TILE

    def body(x_ref, wn_ref, w_ref, cw_ref, cb_ref, u_ref, o_ref, h_s, carry_s, win_s):
        i, j = pl.program_id(0), pl.program_id(1)

        @pl.when(j == 0)
        def _():
            h_s[...] = _rms_fwd(x_ref[...], wn_ref[...]).astype(h_s.dtype)
        h = h_s[...]
        for t in range(CONV_NTB):
            p = CONV_NTB * j + t
            ut = jnp.dot(h, w_ref[:, TILE * t:TILE * (t + 1)], preferred_element_type=f32)
            u_ref[t] = ut
            win_s[t, 0:8, :] = jnp.where(i > 0, carry_s[p], 0.0)
            win_s[t, 8:8 + tm, :] = ut
            carry_s[p] = ut[tm - 8:tm, :]
            wk = [cw_ref[t, k:k + 1, :] for k in range(4)]
            b = cb_ref[t]
            for s in range(tm // CONV_SUB):
                o = CONV_SUB * s
                acc = b
                for k in range(4):
                    acc = acc + wk[k] * win_s[t, 5 + k + o:5 + k + o + CONV_SUB, :]
                o_ref[t, o:o + CONV_SUB, :] = _silu(acc)

    return _pc(
        body, name, grid=(t_len // tm, nt // CONV_NTB),
        in_specs=[pl.BlockSpec((tm, d), lambda i, j: (i, 0)),
                  pl.BlockSpec((1, d), lambda i, j: (0, 0)),
                  pl.BlockSpec((d, CONV_NTB * TILE), lambda i, j: (0, j)),
                  pl.BlockSpec((CONV_NTB, 4, TILE), lambda i, j: (j, 0, 0)),
                  pl.BlockSpec((CONV_NTB, 1, TILE), lambda i, j: (j, 0, 0))],
        out_specs=[pl.BlockSpec((CONV_NTB, tm, TILE), lambda i, j: (j, i, 0)),
                   pl.BlockSpec((CONV_NTB, tm, TILE), lambda i, j: (j, i, 0))],
        out_shape=[SDS((nt, t_len, TILE), f32), SDS((nt, t_len, TILE), f32)],
        scratch_shapes=[pltpu.VMEM((tm, d), MXU_DTYPE), pltpu.VMEM((nt, 8, TILE), f32),
                        pltpu.VMEM((CONV_NTB, 8 + tm, TILE), f32)],
        compiler_params=_params(("arbitrary", "arbitrary")),
    )(x, wn, w, cw, cb)


def _conv_bwd(u, du, cw, cb, name):
    t_len = u.shape[1]
    rows = min(CONV_ROWS, t_len)
    nb = t_len // rows

    def body(u_ref, halo_ref, cw_ref, cb_ref, d_ref, o_ref, dcw_ref, dcb_ref, carry_s, win_s, dp_s):
        i = pl.program_id(1)
        ri = nb - 1 - i

        @pl.when(i == 0)
        def _():
            carry_s[...] = jnp.zeros_like(carry_s)
            dcw_ref[...] = jnp.zeros_like(dcw_ref)
            dcb_ref[...] = jnp.zeros_like(dcb_ref)
        win_s[0:8, :] = jnp.where(ri > 0, halo_ref[0], 0.0)
        win_s[8:8 + rows, :] = u_ref[0]
        w = [cw_ref[0, k:k + 1, :] for k in range(4)]
        b = cb_ref[0]
        dw = [jnp.zeros((1, TILE), f32)] * 4
        db = jnp.zeros((1, TILE), f32)
        for s in range(rows // CONV_SUB):
            o = CONV_SUB * s
            xk = [win_s[5 + k + o:5 + k + o + CONV_SUB, :] for k in range(4)]
            pre = b
            for k in range(4):
                pre = pre + w[k] * xk[k]
            sg = _sigmoid(pre)
            dpre = d_ref[0, o:o + CONV_SUB, :] * (sg * (1.0 + pre * (1.0 - sg)))
            dp_s[o:o + CONV_SUB, :] = dpre
            dw = [dw[k] + jnp.sum(dpre * xk[k], axis=0, keepdims=True) for k in range(4)]
            db = db + jnp.sum(dpre, axis=0, keepdims=True)
        dp_s[rows:rows + 8, :] = carry_s[...]
        for s in range(rows // CONV_SUB):
            o = CONV_SUB * s
            acc = w[0] * dp_s[3 + o:3 + o + CONV_SUB, :]
            for k in range(1, 4):
                acc = acc + w[k] * dp_s[3 - k + o:3 - k + o + CONV_SUB, :]
            o_ref[0, o:o + CONV_SUB, :] = acc
        carry_s[...] = dp_s[0:8, :]
        for k in range(4):
            dcw_ref[0, k:k + 1, :] += dw[k]
        dcb_ref[0] += db

    return _pc(
        body, name, grid=(16, nb),
        in_specs=_conv_specs(nb, rows, True) + [pl.BlockSpec((1, rows, TILE), lambda p, i: (8 + p, nb - 1 - i, 0))],
        out_specs=[pl.BlockSpec((1, rows, TILE), lambda p, i: (8 + p, nb - 1 - i, 0)),
                   pl.BlockSpec((1, 4, TILE), lambda p, i: (p, 0, 0)),
                   pl.BlockSpec((1, 1, TILE), lambda p, i: (p, 0, 0))],
        out_shape=[SDS(du.shape, f32), SDS((16, 4, TILE), f32), SDS((16, 1, TILE), f32)],
        input_output_aliases={4: 0},
        scratch_shapes=[pltpu.VMEM((8, TILE), f32), pltpu.VMEM((8 + rows, TILE), f32),
                        pltpu.VMEM((rows + 8, TILE), f32)],
        compiler_params=_params(("arbitrary", "arbitrary")),
    )(u, u, cw, cb, du)


def _collapse_matrix(g):
    r = lax.broadcasted_iota(jnp.int32, (SSM_HEADS, TILE), 0)
    c = lax.broadcasted_iota(jnp.int32, (SSM_HEADS, TILE), 1)
    return ((c // SSM_P) + 4 * g == r).astype(jnp.bfloat16)


def _ssd_prelude(dt_raw, dtb, alog, dsk, colx_s, scx_s):
    dtr, dt, a_neg, acs, acs_t = _dt_path(dt_raw, dtb, alog)
    a_end = acs[CHUNK - 1:CHUNK, :]
    lane = lax.broadcasted_iota(jnp.int32, (1, 2 * SSM_P), 1)
    lane4 = lax.broadcasted_iota(jnp.int32, (1, TILE), 1)
    sub8 = lax.broadcasted_iota(jnp.int32, (8, 1), 0)

    def row4(v, g):
        e = [v[:, 4 * g + r:4 * g + r + 1] for r in range(4)]
        return jnp.where(lane4 < 64, e[0], jnp.where(lane4 < 128, e[1], jnp.where(lane4 < 192, e[2], e[3])))

    for g in range(SSM_GROUPS):
        for k, arr in enumerate((dt, acs)):
            for half in range(2):
                h0 = 4 * g + 2 * half
                colx_s[k, g, :, 128 * half:128 * (half + 1)] = jnp.where(
                    lane < SSM_P, arr[:, h0:h0 + 1], arr[:, h0 + 1:h0 + 2])
        scx_s[g] = jnp.where(sub8 == 0, row4(dsk, g), jnp.where(sub8 == 1, row4(a_end, g), 0.0))
    return dtr, dt, a_neg, acs_t


def _ssm_core_specs(nc, rev):
    def cidx(c):
        return (nc - 1 - c) if rev else c
    return [
        pl.BlockSpec((SSM_KT, CHUNK, TILE), lambda c: (0, cidx(c), 0)),
        pl.BlockSpec((1, CHUNK, TILE), lambda c: (SSM_KT, cidx(c), 0)),
        pl.BlockSpec((16, CHUNK, TILE), lambda c: (0, cidx(c), 0)),
        pl.BlockSpec((1, SSM_HEADS), lambda c: (0, 0)),
        pl.BlockSpec((1, SSM_HEADS), lambda c: (0, 0)),
        pl.BlockSpec((1, SSM_HEADS), lambda c: (0, 0)),
        pl.BlockSpec((SSM_KT, 1, TILE), lambda c: (0, 0, 0)),
        pl.BlockSpec((2, SSM_GROUPS, CHUNK, TILE), lambda c: (0, 0, cidx(c), 0)),
        pl.BlockSpec((1, SSM_GROUPS, 8, TILE), lambda c: (cidx(c), 0, 0, 0)),
        pl.BlockSpec((1, SSM_HEADS, CHUNK), lambda c: (cidx(c), 0, 0)),
    ]


PRE_CHUNKS = 4


def _ssm_pre(u, dtb, alog, dsk, name):
    t_len = u.shape[1]
    nc = t_len // CHUNK
    per = min(PRE_CHUNKS, nc)

    def body(dt_ref, dtb_ref, alog_ref, dsk_ref, colx_ref, scx_ref, acst_ref):
        for c in range(per):
            rows = pl.ds(CHUNK * c, CHUNK)
            _, _, _, acs_t = _ssd_prelude(dt_ref[0, rows, 0:SSM_HEADS], dtb_ref[...], alog_ref[...], dsk_ref[...],
                                          colx_ref.at[:, :, rows, :], scx_ref.at[c])
            acst_ref[c] = acs_t

    vec = pl.BlockSpec((1, SSM_HEADS), lambda i: (0, 0))
    return _pc(
        body, name, grid=(nc // per,),
        in_specs=[pl.BlockSpec((1, per * CHUNK, TILE), lambda i: (SSM_KT, i, 0)), vec, vec, vec],
        out_specs=[pl.BlockSpec((2, SSM_GROUPS, per * CHUNK, TILE), lambda i: (0, 0, i, 0)),
                   pl.BlockSpec((per, SSM_GROUPS, 8, TILE), lambda i: (i, 0, 0, 0)),
                   pl.BlockSpec((per, SSM_HEADS, CHUNK), lambda i: (i, 0, 0))],
        out_shape=[SDS((2, SSM_GROUPS, t_len, TILE), f32), SDS((nc, SSM_GROUPS, 8, TILE), f32),
                   SDS((nc, SSM_HEADS, CHUNK), f32)],
        compiler_params=_params(("arbitrary",)),
    )(u, dtb, alog, dsk)


def _stack_cols_rows(acx, rows):
    ac = jnp.concatenate([acx[:, SSM_P * r:SSM_P * r + 1] for r in range(4)], axis=0)
    ar = jnp.concatenate([jnp.broadcast_to(rows[r:r + 1, :], (CHUNK, CHUNK)) for r in range(4)], axis=0)
    return ac, ar


def _ssm2_fwd(u, xbc, pre, dtb, alog, dsk, gn, name):
    t_len = u.shape[1]
    nc = t_len // CHUNK

    def body(z_ref, dt_ref, x_ref, dtb_ref, alog_ref, dsk_ref, gn_ref, colx_ref, scx_ref, acst_ref,
             a3_ref, yp_ref, hst_ref, h_s, xt_s, yd_s):
        c = pl.program_id(0)

        @pl.when(c == 0)
        def _():
            h_s[...] = jnp.zeros_like(h_s)
        acs_t = acst_ref[0]
        causal = _tri4()

        def group(g, s1):
            xs = x_ref[g]
            bm, cm = x_ref[8 + g, :, 0:SSM_N], x_ref[8 + g, :, SSM_N:2 * SSM_N]
            cb = _dot_nt(cm, bm)
            sc = scx_ref[0, g]
            a_end = sc[1:2, :]
            rows = pltpu.roll(acs_t, (SSM_HEADS - 4 * g) % SSM_HEADS, 0)
            hp = h_s[g]
            xt = xs * colx_ref[0, g]
            xt_s[...] = xt
            acx = colx_ref[1, g]
            ac_st, ar_st = _stack_cols_rows(acx, rows)
            m = jnp.concatenate([cb] * 4, axis=0) * jnp.exp(jnp.where(causal, ac_st - ar_st, -jnp.inf))
            for r in range(4):
                hd = slice(SSM_P * r, SSM_P * (r + 1))
                yd_s[:, hd] = _dot(m[CHUNK * r:CHUNK * (r + 1), :], xt_s[:, hd])
            yp_ref[g] = yd_s[...] + _dot(cm, hp) * jnp.exp(acx) + sc[0:1, :] * xs
            hst_ref[0, g] = hp
            h_s[g] = hp * jnp.exp(a_end) + _dot_tn(bm, xt * jnp.exp(a_end - acx))
            y2 = yp_ref[g] * _silu(z_ref[g])
            return s1 + jnp.sum(y2 * y2, axis=1, keepdims=True)

        s1 = lax.fori_loop(0, SSM_GROUPS // 2, lambda i, c: group(2 * i + 1, group(2 * i, c)),
                           jnp.zeros((CHUNK, 1), f32))
        rinv = lax.rsqrt(s1 * (1.0 / SSM_INNER) + EPS)

        def gate(g, carry):
            y2 = yp_ref[g] * _silu(z_ref[g])
            a3_ref[g] = (y2 * rinv * gn_ref[g]).astype(a3_ref.dtype)
            return carry

        lax.fori_loop(0, SSM_GROUPS, gate, 0)

    return _pc(
        body, name, grid=(nc,),
        in_specs=_ssm_core_specs(nc, False),
        out_specs=[pl.BlockSpec((SSM_KT, CHUNK, TILE), lambda c: (0, c, 0)),
                   pl.BlockSpec((SSM_KT, CHUNK, TILE), lambda c: (0, c, 0)),
                   pl.BlockSpec((1, SSM_GROUPS, SSM_N, TILE), lambda c: (c, 0, 0, 0))],
        out_shape=[SDS((SSM_KT, t_len, TILE), MXU_DTYPE), SDS((SSM_KT, t_len, TILE), f32),
                   SDS((nc, SSM_GROUPS, SSM_N, TILE), f32)],
        scratch_shapes=[pltpu.VMEM((SSM_GROUPS, SSM_N, TILE), f32), pltpu.VMEM((CHUNK, TILE), f32),
                        pltpu.VMEM((CHUNK, TILE), f32)],
        compiler_params=_params(("arbitrary",)),
    )(u, u, xbc, dtb, alog, dsk, gn, *pre)


def _ssm2_bwd(u, xbc, pre, yp, hst, da3, dtb, alog, dsk, gn, name):
    t_len = u.shape[1]
    nc = t_len // CHUNK

    def body(z_ref, dt_ref, x_ref, dtb_ref, alog_ref, dsk_ref, gn_ref, colx_s, scx_ref, acst_ref,
             yp_ref, hst_ref, da3_ref, du_ref, ddtb_ref, dalog_ref, ddsk_ref, dgn_ref,
             dh_s, xt_s, dy_s, dxt_s, ddtx_s, dacx_s, ddx_s, drow_s, dm_s, dmt_s):
        step = pl.program_id(0)

        @pl.when(step == 0)
        def _():
            dh_s[...] = jnp.zeros_like(dh_s)
            ddtb_ref[...] = jnp.zeros_like(ddtb_ref)
            dalog_ref[...] = jnp.zeros_like(dalog_ref)
            ddsk_ref[...] = jnp.zeros_like(ddsk_ref)
            dgn_ref[...] = jnp.zeros_like(dgn_ref)
        dtr = dt_ref[0, :, 0:SSM_HEADS] + dtb_ref[...]
        dt = _softplus(dtr)
        a_neg = -jnp.exp(alog_ref[...])
        acs_t = acst_ref[0]
        causal = _tri4()
        causal_t = (lax.broadcasted_iota(jnp.int32, (ATT_ROWS, CHUNK), 1)
                    >= lax.broadcasted_iota(jnp.int32, (ATT_ROWS, CHUNK), 0) % CHUNK)
        last = (lax.broadcasted_iota(jnp.int32, (1, CHUNK), 1) == CHUNK - 1).astype(f32)
        lane = lax.broadcasted_iota(jnp.int32, (1, TILE), 1)
        sub32 = lax.broadcasted_iota(jnp.int32, (SSM_HEADS, 1), 0)
        drow_s[...] = jnp.zeros_like(drow_s)

        def sums(g, carry):
            s1, s2 = carry
            y2 = yp_ref[g] * _silu(z_ref[g])
            g3 = da3_ref[g] * gn_ref[g]
            return (s1 + jnp.sum(y2 * y2, axis=1, keepdims=True), s2 + jnp.sum(g3 * y2, axis=1, keepdims=True))

        zcol = jnp.zeros((CHUNK, 1), f32)
        carry = (zcol, zcol)
        for g in range(SSM_GROUPS):
            carry = sums(g, carry)
        s1, s2 = carry
        rinv = lax.rsqrt(s1 * (1.0 / SSM_INNER) + EPS)
        m2 = s2 * rinv * rinv * rinv * (1.0 / SSM_INNER)

        def group(g, carry):
            z = z_ref[g]
            sg = _sigmoid(z)
            sz = z * sg
            y = yp_ref[g]
            y2 = y * sz
            da3 = da3_ref[g]
            dgn_ref[g] += jnp.sum(da3 * y2 * rinv, axis=0, keepdims=True)
            dy2 = rinv * (da3 * gn_ref[g]) - y2 * m2
            dy = dy2 * sz
            dy_s[...] = dy
            du_ref[g] = dy2 * y * (sg * (1.0 + z * (1.0 - sg)))
            xs = x_ref[g]
            bm, cm = x_ref[8 + g, :, 0:SSM_N], x_ref[8 + g, :, SSM_N:2 * SSM_N]
            cb = _dot_nt(cm, bm)
            cbt = _dot_nt(bm, cm)
            dtx, acx = colx_s[0, g], colx_s[1, g]
            sc = scx_ref[0, g]
            a_end = sc[1:2, :]
            ex = jnp.exp(acx)
            wdx = jnp.exp(a_end - acx)
            eend = jnp.exp(a_end)
            rows = pltpu.roll(acs_t, (SSM_HEADS - 4 * g) % SSM_HEADS, 0)
            hp = hst_ref[0, g]
            dhn = dh_s[g]
            xt = xs * dtx
            xt_s[...] = xt
            ch = _dot(cm, hp)
            gy = dy * ex
            dcm = _dot_nt(gy, hp)
            dh_s[g] = _dot_tn(cm, gy) + dhn * eend
            q = _dot(bm, dhn)
            dbm = _dot_nt(xt * wdx, dhn)
            qx = q * xt * wdx
            v_end = jnp.sum(dhn * hp, axis=0, keepdims=True) * eend + jnp.sum(qx, axis=0, keepdims=True)
            ac_st, ar_st = _stack_cols_rows(acx, rows)
            lam = jnp.exp(jnp.where(causal, ac_st - ar_st, -jnp.inf))
            lam_t = jnp.exp(jnp.where(causal_t, ar_st - ac_st, -jnp.inf))
            m = jnp.concatenate([cb] * 4, axis=0) * lam
            m_t = jnp.concatenate([cbt] * 4, axis=0) * lam_t
            for r in range(4):
                hd = slice(SSM_P * r, SSM_P * (r + 1))
                rs = slice(CHUNK * r, CHUNK * (r + 1))
                dm_s[rs, :] = _dot_nt(dy_s[:, hd], xt_s[:, hd])
                dmt_s[rs, :] = _dot_nt(xt_s[:, hd], dy_s[:, hd])
                dxt_s[:, hd] = _dot(m_t[rs, :], dy_s[:, hd])
            dm = dm_s[...]
            dl = dm * lam
            dseg = dm * m
            dseg_t = dmt_s[...] * m_t
            dcb = dl[0:CHUNK] + dl[CHUNK:2 * CHUNK] + dl[2 * CHUNK:3 * CHUNK] + dl[3 * CHUNK:4 * CHUNK]
            drows = jnp.zeros((SSM_HEADS, CHUNK), f32)
            for r in range(4):
                rs = slice(CHUNK * r, CHUNK * (r + 1))
                in_head = (lane >= SSM_P * r) & (lane < SSM_P * (r + 1))
                d_ac = jnp.sum(dseg_t[rs, :], axis=0, keepdims=True)
                d_ar = jnp.sum(dseg[rs, :], axis=0, keepdims=True)
                d_aend = jnp.sum(jnp.where(in_head, v_end, 0.0), axis=1, keepdims=True)
                drows = drows + jnp.where(sub32 == r, d_ac - d_ar + last * d_aend, 0.0)
            dxt = dxt_s[...] + q * wdx
            du_ref[8 + g] = sc[0:1, :] * dy + dxt * dtx
            ddtx_s[g] = dxt * xs
            dacx_s[g] = dy * ch * ex - qx
            ddx_s[g] = jnp.broadcast_to(jnp.sum(dy * xs, axis=0, keepdims=True), (8, TILE))
            du_ref[16 + g, :, 0:SSM_N] = dbm + _dot_tn(dcb, cm)
            du_ref[16 + g, :, SSM_N:2 * SSM_N] = dcm + _dot(dcb, bm)
            drow_s[...] += pltpu.roll(drows, (4 * g) % SSM_HEADS, 0)
            return carry

        lax.fori_loop(0, SSM_GROUPS, group, 0, unroll=4)
        ddt = jnp.zeros((CHUNK, SSM_HEADS), f32)
        dacs = jnp.zeros((CHUNK, SSM_HEADS), f32)
        ddsk = jnp.zeros((8, SSM_HEADS), f32)
        for g in range(SSM_GROUPS):
            col_g = _collapse_matrix(g)
            ddt = ddt + _dot_exact(ddtx_s[g], col_g, ((1,), (1,)), True, 2)
            dacs = dacs + _dot_exact(dacx_s[g], col_g, ((1,), (1,)), True, 2)
            ddsk = ddsk + _dot_exact(ddx_s[g], col_g, ((1,), (1,)), True, 2)
        upper = _tri(False)
        da = _dot_exact(upper, dacs, ((1,), (0,)), False) + _dot_exact(upper, drow_s[...], ((1,), (1,)), False)
        ddt = ddt + da * a_neg
        dalog_ref[...] += jnp.sum(da * dt, axis=0, keepdims=True) * a_neg
        ddtr = ddt * _sigmoid(dtr)
        ddtb_ref[...] += jnp.sum(ddtr, axis=0, keepdims=True)
        ddsk_ref[...] += ddsk[0:1, :]
        du_ref[SSM_NT - 1] = jnp.zeros((CHUNK, TILE), f32)
        du_ref[SSM_NT - 1, :, 0:SSM_HEADS] = ddtr

    def rc(c):
        return nc - 1 - c

    vec = pl.BlockSpec((1, SSM_HEADS), lambda c: (0, 0))
    return _pc(
        body, name, grid=(nc,),
        in_specs=_ssm_core_specs(nc, True) + [
            pl.BlockSpec((SSM_KT, CHUNK, TILE), lambda c: (0, rc(c), 0)),
            pl.BlockSpec((1, SSM_GROUPS, SSM_N, TILE), lambda c: (rc(c), 0, 0, 0)),
            pl.BlockSpec((SSM_KT, CHUNK, TILE), lambda c: (0, rc(c), 0))],
        out_specs=[pl.BlockSpec((SSM_NT, CHUNK, TILE), lambda c: (0, rc(c), 0)), vec, vec, vec,
                   pl.BlockSpec((SSM_KT, 1, TILE), lambda c: (0, 0, 0))],
        out_shape=[SDS((SSM_NT, t_len, TILE), f32), SDS((1, SSM_HEADS), f32), SDS((1, SSM_HEADS), f32),
                   SDS((1, SSM_HEADS), f32), SDS((SSM_KT, 1, TILE), f32)],
        scratch_shapes=[pltpu.VMEM((SSM_GROUPS, SSM_N, TILE), f32), pltpu.VMEM((CHUNK, TILE), f32),
                        pltpu.VMEM((CHUNK, TILE), f32), pltpu.VMEM((CHUNK, TILE), f32),
                        pltpu.VMEM((SSM_GROUPS, CHUNK, TILE), f32), pltpu.VMEM((SSM_GROUPS, CHUNK, TILE), f32),
                        pltpu.VMEM((SSM_GROUPS, 8, TILE), f32), pltpu.VMEM((SSM_HEADS, CHUNK), f32),
                        pltpu.VMEM((4 * CHUNK, CHUNK), f32), pltpu.VMEM((4 * CHUNK, CHUNK), f32)],
        compiler_params=_params(("arbitrary",)),
    )(u, u, xbc, dtb, alog, dsk, gn, *pre, yp, hst, da3)


def _swap16(t):
    lane = lax.broadcasted_iota(jnp.int32, t.shape, 1) % 64
    return jnp.where(lane < 8, pltpu.roll(t, TILE - 8, 1), jnp.where(lane < 16, pltpu.roll(t, 8, 1), 0.0))


def _rope(t, cos_t, sin_t):
    return t * cos_t + _swap16(t) * sin_t


def _rope_bwd(g, cos_t, sin_t):
    return g * cos_t + _swap16(g * sin_t)


def _att_in_specs(nb, rev):
    def bidx(n):
        return (nb - 1 - n) if rev else n
    return [
        pl.BlockSpec((ATT_NT, CHUNK, TILE), lambda n: (0, bidx(n), 0)),
        pl.BlockSpec((2, CHUNK, TILE), lambda n: (2, jnp.maximum(bidx(n) - 1, 0), 0)),
        pl.BlockSpec((CHUNK, TILE), lambda n: (bidx(n), 0)),
        pl.BlockSpec((CHUNK, TILE), lambda n: (bidx(n), 0)),
        pl.BlockSpec((CHUNK, TILE), lambda n: (jnp.maximum(bidx(n) - 1, 0), 0)),
        pl.BlockSpec((CHUNK, TILE), lambda n: (jnp.maximum(bidx(n) - 1, 0), 0)),
        pl.BlockSpec((1, 16), lambda n: (0, 0)),
    ]


ATT_SCALE = 0.125


def _tri4():
    row = lax.broadcasted_iota(jnp.int32, (ATT_ROWS, CHUNK), 0) % CHUNK
    col = lax.broadcasted_iota(jnp.int32, (ATT_ROWS, CHUNK), 1)
    return col <= row


def _stack_heads(ref):
    return jnp.concatenate([ref[:, 64 * r:64 * (r + 1)] for r in range(4)], axis=0)


def _sink_col(sinks, g):
    return [sinks[:, 4 * g + r:4 * g + r + 1] for r in range(4)]


def _softmax_rows(s_s, pn_s, pc_s, sink, tri, has_prev):
    sub = lax.broadcasted_iota(jnp.int32, (ATT_ROWS, 1), 0)
    sk = jnp.where(sub < CHUNK, sink[0], jnp.where(sub < 2 * CHUNK, sink[1], jnp.where(sub < 3 * CHUNK, sink[2], sink[3])))
    s = jnp.where(tri, s_s[:, CHUNK:2 * CHUNK], jnp.where(has_prev, s_s[:, 0:CHUNK], -jnp.inf)) * ATT_SCALE
    m = jnp.maximum(jnp.max(s, axis=-1, keepdims=True), sk)
    p = jnp.exp(s - m)
    e_sink = jnp.exp(sk - m)
    inv = 1.0 / (jnp.sum(p, axis=-1, keepdims=True) + e_sink)
    pn = p * inv
    pc_s[...] = pn
    pn_s[:, 0:CHUNK] = jnp.where(tri, 0.0, pn)
    pn_s[:, CHUNK:2 * CHUNK] = jnp.where(tri, pn, 0.0)
    return e_sink * inv


def _att2_fwd(u, cos_t, sin_t, sinks, name):
    t_len = u.shape[1]
    nb = t_len // CHUNK

    def body(u_ref, prev_ref, cc_ref, sc_ref, cp_ref, sp_ref, sink_ref, a_ref,
             q_s, kp_s, kc_s, vp_s, vc_s, o_s, s_s, pn_s, pc_s):
        n = pl.program_id(0)
        tri = _tri4()
        cos_c, sin_c = cc_ref[...], sc_ref[...]
        kc_s[...] = _rope(u_ref[4], cos_c, sin_c)
        kp_s[...] = _rope(prev_ref[0], cp_ref[...], sp_ref[...])
        vc_s[...] = u_ref[5]
        vp_s[...] = prev_ref[1]
        sinks = sink_ref[...]
        for g in range(4):
            q_g, o_g, s_g, pn_g = q_s.at[g], o_s.at[g], s_s.at[g], pn_s.at[g]
            q_g[...] = _rope(u_ref[g], cos_c, sin_c)
            kv = slice(64 * g, 64 * (g + 1))
            kb = jnp.concatenate([kp_s[:, kv], kc_s[:, kv]], axis=0)
            vb = jnp.concatenate([vp_s[:, kv], vc_s[:, kv]], axis=0)
            s_g[...] = _dot_nt(_stack_heads(q_g), kb)
            _softmax_rows(s_g, pn_g, pc_s.at[g], _sink_col(sinks, g), tri, n > 0)
            o = _dot(pn_g[...], vb)
            for r in range(4):
                o_g[:, 64 * r:64 * (r + 1)] = o[CHUNK * r:CHUNK * (r + 1), :]
            a_ref[g] = (o_g[...] * _silu(u_ref[6 + g])).astype(a_ref.dtype)

    return _pc(
        body, name, grid=(nb,),
        in_specs=_att_in_specs(nb, False),
        out_specs=pl.BlockSpec((ATT_KT, CHUNK, TILE), lambda n: (0, n, 0)),
        out_shape=SDS((ATT_KT, t_len, TILE), MXU_DTYPE),
        scratch_shapes=[pltpu.VMEM((4, CHUNK, TILE), f32)] + [pltpu.VMEM((CHUNK, TILE), f32)] * 4
                       + [pltpu.VMEM((4, CHUNK, TILE), f32)] + [pltpu.VMEM((4, ATT_ROWS, 2 * CHUNK), f32)] * 2
                       + [pltpu.VMEM((4, ATT_ROWS, CHUNK), f32)],
        compiler_params=_params(("arbitrary",)),
    )(u, u, cos_t, sin_t, cos_t, sin_t, sinks)


def _att2_bwd(u, cos_t, sin_t, sinks, da, name):
    t_len = u.shape[1]
    nb = t_len // CHUNK

    def body(u_ref, prev_ref, cc_ref, sc_ref, cp_ref, sp_ref, sink_ref, da_ref, du_ref, dsink_ref,
             ck_s, cv_s, kp_s, kc_s, vp_s, vc_s, q_s, o_s, do_s, dq_s, s_s, pn_s, dp_s, dkt_s, dvt_s, pc_s):
        step = pl.program_id(0)
        nn = nb - 1 - step

        @pl.when(step == 0)
        def _():
            ck_s[...] = jnp.zeros_like(ck_s)
            cv_s[...] = jnp.zeros_like(cv_s)
            dsink_ref[...] = jnp.zeros_like(dsink_ref)
        tri = _tri4()
        cos_c, sin_c = cc_ref[...], sc_ref[...]
        cos_p, sin_p = cp_ref[...], sp_ref[...]
        kc_s[...] = _rope(u_ref[4], cos_c, sin_c)
        kp_s[...] = _rope(prev_ref[0], cos_p, sin_p)
        vc_s[...] = u_ref[5]
        vp_s[...] = prev_ref[1]
        sinks = sink_ref[...]
        lane16 = lax.broadcasted_iota(jnp.int32, (1, 16), 1)
        dsink = jnp.zeros((1, 16), f32)
        for g in range(4):
            q_g, o_g, do_g, dq_g = q_s.at[g], o_s.at[g], do_s.at[g], dq_s.at[g]
            s_g, pn_g, dp_g, pc_g = s_s.at[g], pn_s.at[g], dp_s.at[g], pc_s.at[g]
            q_g[...] = _rope(u_ref[g], cos_c, sin_c)
            gate = u_ref[6 + g]
            sg = _sigmoid(gate)
            dav = da_ref[g]
            do_g[...] = dav * (gate * sg)
            kv = slice(64 * g, 64 * (g + 1))
            kb = jnp.concatenate([kp_s[:, kv], kc_s[:, kv]], axis=0)
            vb = jnp.concatenate([vp_s[:, kv], vc_s[:, kv]], axis=0)
            q_st = _stack_heads(q_g)
            do_st = _stack_heads(do_g)
            s_g[...] = _dot_nt(q_st, kb)
            p_sink = _softmax_rows(s_g, pn_g, pc_g, _sink_col(sinks, g), tri, nn > 0)
            o = _dot(pn_g[...], vb)
            dvt_s[64 * g:64 * (g + 1), :] = _dot_tn(do_st, pn_g[...])
            dp_g[...] = _dot_nt(do_st, vb)
            delta = jnp.sum(do_st * o, axis=-1, keepdims=True)
            dpc = jnp.where(tri, dp_g[:, CHUNK:2 * CHUNK], dp_g[:, 0:CHUNK])
            dsc = pc_g[...] * (dpc - delta) * ATT_SCALE
            dp_g[:, 0:CHUNK] = jnp.where(tri, 0.0, dsc)
            dp_g[:, CHUNK:2 * CHUNK] = jnp.where(tri, dsc, 0.0)
            sd = p_sink * delta
            for r in range(4):
                rs = slice(CHUNK * r, CHUNK * (r + 1))
                o_g[:, 64 * r:64 * (r + 1)] = o[rs, :]
                ds_h = -jnp.sum(sd[rs, :], axis=0, keepdims=True)
                dsink = dsink + ds_h * (lane16 == 4 * g + r).astype(f32)
            ds = dp_g[...]
            dq = _dot(ds, kb)
            for r in range(4):
                dq_g[:, 64 * r:64 * (r + 1)] = dq[CHUNK * r:CHUNK * (r + 1), :]
            dkt_s[64 * g:64 * (g + 1), :] = _dot_tn(q_st, ds)
            du_ref[6 + g] = dav * o_g[...] * (sg * (1.0 + gate * (1.0 - sg)))
            du_ref[g] = _rope_bwd(dq_g[...], cos_c, sin_c)
        dk = dkt_s[...].T
        dv = dvt_s[...].T
        du_ref[4] = _rope_bwd(dk[CHUNK:2 * CHUNK, :], cos_c, sin_c) + ck_s[...]
        du_ref[5] = dv[CHUNK:2 * CHUNK, :] + cv_s[...]
        ck_s[...] = _rope_bwd(dk[0:CHUNK, :], cos_p, sin_p)
        cv_s[...] = dv[0:CHUNK, :]
        dsink_ref[...] += dsink

    def rb(n):
        return nb - 1 - n

    return _pc(
        body, name, grid=(nb,),
        in_specs=_att_in_specs(nb, True) + [pl.BlockSpec((ATT_KT, CHUNK, TILE), lambda n: (0, rb(n), 0))],
        out_specs=[pl.BlockSpec((ATT_NT, CHUNK, TILE), lambda n: (0, rb(n), 0)),
                   pl.BlockSpec((1, 16), lambda n: (0, 0))],
        out_shape=[SDS((ATT_NT, t_len, TILE), f32), SDS((1, 16), f32)],
        scratch_shapes=[pltpu.VMEM((CHUNK, TILE), f32)] * 6 + [pltpu.VMEM((4, CHUNK, TILE), f32)] * 4
                       + [pltpu.VMEM((4, ATT_ROWS, 2 * CHUNK), f32)] * 3
                       + [pltpu.VMEM((2 * CHUNK, 2 * CHUNK), f32)] * 2 + [pltpu.VMEM((4, ATT_ROWS, CHUNK), f32)],
        compiler_params=_params(("arbitrary",)),
    )(u, u, cos_t, sin_t, cos_t, sin_t, sinks, da)


_HBM = pl.BlockSpec(memory_space=pltpu.HBM)


def _all_gather_big(shards):
    n = len(shards)

    def body(*refs):
        x_refs, out_refs = refs[:n], refs[n:2 * n]
        send_sems, recv_sems, local_sems = refs[2 * n:]
        x, y, c = lax.axis_index("x"), lax.axis_index("y"), lax.axis_index("c")
        me, sibling = (x, y, c), (x, y, 1 - c)
        chips = [(1 - x, y), (x, 1 - y), (1 - x, 1 - y)]

        def slot(i, px, py, pc):
            return out_refs[i].at[4 * px + 2 * py + pc]

        def copy(i, k, block, to, src=None):
            return pltpu.make_async_remote_copy(
                src_ref=slot(i, *block) if src is None else src, dst_ref=slot(i, *block),
                send_sem=send_sems.at[7 * i + k], recv_sem=recv_sems.at[7 * i + k], device_id=to, device_id_type=MESH)

        mine = [pltpu.make_async_copy(x_refs[i], slot(i, *me), local_sems.at[i]) for i in range(n)]
        for cp in mine:
            cp.start()
        first = []
        for i in range(n):
            first.append(copy(i, 0, me, sibling, src=x_refs[i]))
            first += [copy(i, 1 + j, me, (*chip, c), src=x_refs[i]) for j, chip in enumerate(chips)]
        for cp in first:
            cp.start()
        passed = []
        for j, chip in enumerate(chips):
            for i in range(n):
                copy(i, 1 + j, (*chip, c), me).wait_recv()
                fwd = copy(i, 4 + j, (*chip, c), sibling)
                fwd.start()
                passed.append(fwd)
        for i in range(n):
            copy(i, 0, sibling, me).wait_recv()
            for j, chip in enumerate(chips):
                copy(i, 4 + j, (*chip, 1 - c), me).wait_recv()
        for cp in first + passed:
            cp.wait_send()
        for cp in mine:
            cp.wait()

    return _pc(
        body, "all_gather_big",
        in_specs=[_HBM] * n, out_specs=[_HBM] * n,
        out_shape=[SDS((N_DEV,) + s.shape, s.dtype) for s in shards],
        scratch_shapes=[pltpu.SemaphoreType.DMA((7 * n,)), pltpu.SemaphoreType.DMA((7 * n,)),
                        pltpu.SemaphoreType.DMA((n,))],
    )(*shards)


def _all_gather_direct(block, name):
    rows, width = block.shape

    def body(x_ref, out_ref, send_sems, recv_sems, local_sem):
        x, y, c = lax.axis_index("x"), lax.axis_index("y"), lax.axis_index("c")
        my_slot = 4 * x + 2 * y + c

        def peer(k):
            return (1 - x if k & 4 else x, 1 - y if k & 2 else y, 1 - c if k & 1 else c)

        def copy(k):
            px, py, pc = peer(k)
            return pltpu.make_async_remote_copy(
                src_ref=x_ref, dst_ref=out_ref.at[my_slot], send_sem=send_sems.at[k - 1], recv_sem=recv_sems.at[k - 1],
                device_id=(px, py, pc), device_id_type=MESH)

        def arrival(k):
            px, py, pc = peer(k)
            return pltpu.make_async_remote_copy(
                src_ref=x_ref, dst_ref=out_ref.at[4 * px + 2 * py + pc], send_sem=send_sems.at[k - 1],
                recv_sem=recv_sems.at[k - 1], device_id=(px, py, pc), device_id_type=MESH)

        mine = pltpu.make_async_copy(x_ref, out_ref.at[my_slot], local_sem)
        mine.start()
        for k in range(1, N_DEV):
            copy(k).start()
        for k in range(1, N_DEV):
            arrival(k).wait_recv()
        for k in range(1, N_DEV):
            copy(k).wait_send()
        mine.wait()

    return _pc(
        body, name,
        in_specs=[_HBM], out_specs=_HBM,
        out_shape=SDS((N_DEV, rows, width), block.dtype),
        scratch_shapes=[pltpu.SemaphoreType.DMA((7,)), pltpu.SemaphoreType.DMA((7,)), pltpu.SemaphoreType.DMA],
    )(block)


N_CHIP = N_DEV // 2


def _exchange_sibling(gs):
    n = len(gs)

    def body(*refs):
        g_refs, out_refs = refs[:n], refs[n:2 * n]
        send_sems, recv_sems = refs[2 * n:]
        x, y, c = lax.axis_index("x"), lax.axis_index("y"), lax.axis_index("c")
        cps = [pltpu.make_async_remote_copy(
            src_ref=g_refs[i].at[2 * k + 1 - c], dst_ref=out_refs[i].at[k], send_sem=send_sems.at[N_CHIP * i + k],
            recv_sem=recv_sems.at[N_CHIP * i + k], device_id=(x, y, 1 - c), device_id_type=MESH)
            for i in range(n) for k in range(N_CHIP)]
        for cp in cps:
            cp.start()
        for cp in cps:
            cp.wait()

    return _pc(
        body, "rs_sibling",
        in_specs=[_HBM] * n, out_specs=[_HBM] * n,
        out_shape=[SDS((N_CHIP,) + g.shape[1:], g.dtype) for g in gs],
        scratch_shapes=[pltpu.SemaphoreType.DMA((N_CHIP * n,)), pltpu.SemaphoreType.DMA((N_CHIP * n,))],
    )(*gs)


def _pair_sum(g, r1, cidx, tr, name):
    _, rows, width = g.shape

    def body(c_ref, g_ref, r_ref, o_ref):
        o_ref[...] = (g_ref[...].astype(f32) + r_ref[...].astype(f32)).astype(o_ref.dtype)

    return pl.pallas_call(
        body, name=name,
        grid_spec=pltpu.PrefetchScalarGridSpec(
            num_scalar_prefetch=1, grid=(N_CHIP, rows // tr),
            in_specs=[pl.BlockSpec((1, tr, width), lambda k, i, c_ref: (2 * k + c_ref[0], i, 0)),
                      pl.BlockSpec((1, tr, width), lambda k, i, c_ref: (k, i, 0))],
            out_specs=pl.BlockSpec((1, tr, width), lambda k, i, c_ref: (k, i, 0))),
        out_shape=SDS((N_CHIP, rows, width), g.dtype),
        compiler_params=_params(("arbitrary", "arbitrary")),
    )(cidx, g, r1)


def _exchange_chips(ps):
    n = len(ps)

    def body(*refs):
        p_refs, out_refs = refs[:n], refs[n:2 * n]
        send_sems, recv_sems, local_sems = refs[2 * n:]
        x, y, c = lax.axis_index("x"), lax.axis_index("y"), lax.axis_index("c")
        my_chip = 2 * x + y
        chips = [(1 - x, y), (x, 1 - y), (1 - x, 1 - y)]

        def copy(i, j):
            px, py = chips[j]
            return pltpu.make_async_remote_copy(
                src_ref=p_refs[i].at[2 * px + py], dst_ref=out_refs[i].at[my_chip], send_sem=send_sems.at[3 * i + j],
                recv_sem=recv_sems.at[3 * i + j], device_id=(px, py, c), device_id_type=MESH)

        def arrival(i, j):
            px, py = chips[j]
            return pltpu.make_async_remote_copy(
                src_ref=p_refs[i].at[my_chip], dst_ref=out_refs[i].at[2 * px + py], send_sem=send_sems.at[3 * i + j],
                recv_sem=recv_sems.at[3 * i + j], device_id=(px, py, c), device_id_type=MESH)

        mine = [pltpu.make_async_copy(p_refs[i].at[my_chip], out_refs[i].at[my_chip], local_sems.at[i])
                for i in range(n)]
        for cp in mine:
            cp.start()
        for i in range(n):
            for j in range(3):
                copy(i, j).start()
        for i in range(n):
            for j in range(3):
                arrival(i, j).wait_recv()
        for i in range(n):
            for j in range(3):
                copy(i, j).wait_send()
        for cp in mine:
            cp.wait()

    return _pc(
        body, "rs_chips",
        in_specs=[_HBM] * n, out_specs=[_HBM] * n,
        out_shape=[SDS(p.shape, p.dtype) for p in ps],
        scratch_shapes=[pltpu.SemaphoreType.DMA((3 * n,)), pltpu.SemaphoreType.DMA((3 * n,)),
                        pltpu.SemaphoreType.DMA((n,))],
    )(*ps)


def _adamw(parts, w, m, v, tr, name):
    n, rows, width = parts.shape
    c1 = 1.0 / (1.0 - ADAM_B1 ** ADAM_STEP)
    c2 = 1.0 / (1.0 - ADAM_B2 ** ADAM_STEP)

    def body(p_ref, w_ref, m_ref, v_ref, g_ref, d_ref, mo_ref, vo_ref):
        g = p_ref[0].astype(f32)
        for k in range(1, n):
            g = g + p_ref[k].astype(f32)
        mn = ADAM_B1 * m_ref[...] + (1.0 - ADAM_B1) * g
        vn = ADAM_B2 * v_ref[...] + (1.0 - ADAM_B2) * (g * g)
        g_ref[...] = g
        mo_ref[...] = mn
        vo_ref[...] = vn
        d_ref[...] = -ADAM_LR * ((mn * c1) / (jnp.sqrt(vn * c2) + ADAM_EPS) + ADAM_WD * w_ref[...])

    blk = pl.BlockSpec((tr, width), lambda i: (i, 0))
    return _pc(
        body, name, grid=(rows // tr,),
        in_specs=[pl.BlockSpec((n, tr, width), lambda i: (0, i, 0)), blk, blk, blk],
        out_specs=[blk, blk, blk, blk],
        out_shape=[SDS((rows, width), f32)] * 4,
        compiler_params=_params(("arbitrary",)),
    )(parts, w, m, v)


ROWS_REST = ROWS_SSM_OUT + ROWS_ATT_OUT + 16


def _pack_rest(ssm_w_out, att_w_out, conv_w):
    conv = jnp.pad(conv_w.reshape(4, 1024), ((0, 12), (0, 0)))
    return jnp.concatenate([ssm_w_out.reshape(ROWS_SSM_OUT, 1024), att_w_out.reshape(ROWS_ATT_OUT, 1024), conv], axis=0)


def _unpack_rest(p):
    o = ROWS_SSM_OUT + ROWS_ATT_OUT
    return (p[0:ROWS_SSM_OUT].reshape(2, 256, 1024), p[ROWS_SSM_OUT:o].reshape(2, 128, 1024),
            p[o:o + 4].reshape(2, 4, 512))


def _pack_grads(d_ssm_w_in, d_ssm_w_out, d_att_w_in, d_att_w_out, d_conv_w):
    wire = lambda t: t.astype(MXU_DTYPE)
    a = jnp.transpose(wire(d_ssm_w_in).reshape(2, 1024, 8, 772), (2, 0, 1, 3)).reshape(8, 2048, 772)
    c = jnp.transpose(wire(d_att_w_in).reshape(2, 1024, 8, 320), (2, 0, 1, 3)).reshape(8, 2048, 320)
    b = jnp.transpose(wire(d_ssm_w_out).reshape(2, 8, 256, 1024), (1, 0, 2, 3)).reshape(8, ROWS_SSM_OUT, 1024)
    d = jnp.transpose(wire(d_att_w_out).reshape(2, 8, 128, 1024), (1, 0, 2, 3)).reshape(8, ROWS_ATT_OUT, 1024)
    e = jnp.transpose(wire(d_conv_w).reshape(2, 4, 8, 512), (2, 0, 1, 3)).reshape(8, 4, 1024)
    e = jnp.pad(e, ((0, 0), (0, 12), (0, 0)))
    return a, c, jnp.concatenate([b, d, e], axis=1)


def _pad8(a):
    return jnp.pad(a, ((0, 8 - a.shape[0]), (0, 0)))


def _pack_small(pre_norm, post_norm, conv_b, gate_norm, dt_bias, a_log, d_skip, sinks, extra=None):
    row = jnp.concatenate([dt_bias.reshape(1, 64), a_log.reshape(1, 64), d_skip.reshape(1, 64), sinks.reshape(1, 32),
                           jnp.zeros((1, 1024 - 224), f32)], axis=1)
    if extra is not None:
        row = row + jnp.pad(extra.reshape(1, 1), ((0, 0), (224, 1024 - 225)))
    return jnp.concatenate([_pad8(pre_norm.reshape(4, 1024)), _pad8(post_norm.reshape(4, 1024)),
                            conv_b.reshape(8, 1024), _pad8(gate_norm.reshape(4, 1024)), _pad8(row)], axis=0)


def _unpack_small(p):
    row = p[32]
    return (p[0:4], p[8:12], p[16:24].reshape(2, 4096), row[0:64].reshape(2, 32), row[64:128].reshape(2, 32),
            row[128:192].reshape(2, 32), p[24:28].reshape(2, 2048), row[192:224].reshape(2, 16))


def _ssm_w_in_tiles(w):
    wb = w[:, 4096:5120].reshape(1024, 8, 128)
    wc = w[:, 5120:6144].reshape(1024, 8, 128)
    wbc = jnp.concatenate([wb, wc], axis=2).reshape(1024, 2048)
    return jnp.concatenate([w[:, 0:4096], wbc, w[:, 6144:6176], jnp.zeros((1024, 224), w.dtype)], axis=1)


def _ssm_w_in_untile(dw):
    dbc = dw[:, 4096:6144].reshape(1024, 8, 256)
    return jnp.concatenate([dw[:, 0:4096], dbc[:, :, 0:128].reshape(1024, 1024), dbc[:, :, 128:256].reshape(1024, 1024),
                            dw[:, 6144:6176]], axis=1)


def _conv_tiles(cw):
    k = cw.shape[0]
    xs = jnp.transpose(cw[:, 0:2048].reshape(k, 8, 256), (1, 0, 2))
    b = cw[:, 2048:3072].reshape(k, 8, 128)
    c = cw[:, 3072:4096].reshape(k, 8, 128)
    bc = jnp.transpose(jnp.concatenate([b, c], axis=2), (1, 0, 2))
    return jnp.concatenate([xs, bc], axis=0)


def _conv_untile(t):
    k = t.shape[1]
    xs = jnp.transpose(t[0:8], (1, 0, 2)).reshape(k, 2048)
    bc = jnp.transpose(t[8:16], (1, 0, 2))
    return jnp.concatenate([xs, bc[:, :, 0:128].reshape(k, 1024), bc[:, :, 128:256].reshape(k, 1024)], axis=1)


def _rope_tables(positions):
    inv = ROPE_THETA ** (-jnp.arange(0, 16, 2, dtype=f32) / 16)
    ang = positions.astype(f32).reshape(-1, 1) * inv
    cos, sin = jnp.cos(ang), jnp.sin(ang)
    t_len = ang.shape[0]
    cos64 = jnp.concatenate([cos, cos, jnp.ones((t_len, 48), f32)], axis=1)
    sin64 = jnp.concatenate([-sin, sin, jnp.zeros((t_len, 48), f32)], axis=1)
    return jnp.tile(cos64, (1, 4)), jnp.tile(sin64, (1, 4))


def kernel(x, positions, pre_norm, post_norm, ssm_w_in, ssm_conv_w, ssm_conv_b, ssm_dt_bias, ssm_a_log, ssm_d, ssm_gate_norm, ssm_w_out, att_w_in, att_sinks, att_w_out, loss_target, m_pre_norm, m_post_norm, m_ssm_w_in, m_ssm_conv_w, m_ssm_conv_b, m_ssm_dt_bias, m_ssm_a_log, m_ssm_d, m_ssm_gate_norm, m_ssm_w_out, m_att_w_in, m_att_sinks, m_att_w_out, v_pre_norm, v_post_norm, v_ssm_w_in, v_ssm_conv_w, v_ssm_conv_b, v_ssm_dt_bias, v_ssm_a_log, v_ssm_d, v_ssm_gate_norm, v_ssm_w_out, v_att_w_in, v_att_sinks, v_att_w_out):
    t_len = x.shape[1]
    tm = min(1024, t_len)
    xin = x.reshape(t_len, D_MODEL)
    tgt = loss_target.reshape(t_len, D_MODEL)
    cidx = lax.axis_index("c").astype(jnp.int32).reshape(1)

    g_ssm_in, g_att_in, g_ssm_out, g_att_out = _all_gather_big(
        [ssm_w_in.astype(MXU_DTYPE), att_w_in.astype(MXU_DTYPE), ssm_w_out.astype(MXU_DTYPE),
         att_w_out.astype(MXU_DTYPE)])
    conv_local = jnp.concatenate([ssm_conv_w.reshape(4, 1024), jnp.zeros((4, 1024), f32)], axis=0)
    conv_all = _all_gather_direct(conv_local, "all_gather_conv")[:, 0:4]
    w_ssm_in = jnp.transpose(g_ssm_in, (1, 2, 0, 3)).reshape(2, 1024, SSM_IN)
    w_ssm_out = jnp.transpose(g_ssm_out, (1, 0, 2, 3)).reshape(2, SSM_INNER, 1024)
    w_att_in = jnp.transpose(g_att_in, (1, 2, 0, 3)).reshape(2, 1024, ATT_IN)
    w_att_out = jnp.transpose(g_att_out, (1, 0, 2, 3)).reshape(2, 1024, 1024)
    conv_w = jnp.transpose(conv_all.reshape(8, 2, 4, 512), (1, 2, 0, 3)).reshape(2, 4, 4096)
    cos_t, sin_t = _rope_tables(positions)

    saved = []
    xc = xin
    for i in range(4):
        j = i // 2
        wn_pre, wn_post = pre_norm[i].reshape(1, D_MODEL), post_norm[i].reshape(1, D_MODEL)
        if i % 2 == 0:
            w_in = _ssm_w_in_tiles(w_ssm_in[j])
            cw, cb = _conv_tiles(conv_w[j]), _conv_tiles(ssm_conv_b[j].reshape(1, 4096))
            dtb, alog, dsk = ssm_dt_bias[j].reshape(1, 32), ssm_a_log[j].reshape(1, 32), ssm_d[j].reshape(1, 32)
            gn = ssm_gate_norm[j].reshape(SSM_KT, 1, TILE)
            w_zdt = jnp.concatenate([w_in[:, 0:SSM_INNER], w_in[:, 24 * TILE:25 * TILE]], axis=1)
            u, h = _mm_in(xc, wn_pre, w_zdt, 3, tm, f"ssm_in_{j}")
            uc, xbc = _mm_in_conv(xc, wn_pre, w_in[:, SSM_INNER:24 * TILE], cw, cb, tm, f"ssm_inconv_{j}")
            pre = _ssm_pre(u, dtb, alog, dsk, f"ssm_pre_{j}")
            a3, yp, hst = _ssm2_fwd(u, xbc, pre, dtb, alog, dsk, gn, f"ssm_core_{j}")
            y, xn = _mm_out(a3, w_ssm_out[j], xc, wn_post, 4, tm, f"ssm_out_{j}")
            saved.append(dict(x=xc, u=u, uc=uc, h=h, a=a3, yp=yp, hst=hst, y=y, w_in=w_in, cw=cw, cb=cb, dtb=dtb,
                              alog=alog, dsk=dsk, gn=gn, xbc=xbc, pre=pre))
        else:
            sinks = att_sinks[j].reshape(1, 16)
            u, h = _mm_in(xc, wn_pre, w_att_in[j], 5, tm, f"att_in_{j}")
            a = _att2_fwd(u, cos_t, sin_t, sinks, f"att_core_{j}")
            y, xn = _mm_out(a, w_att_out[j], xc, wn_post, 4, tm, f"att_out_{j}")
            saved.append(dict(x=xc, u=u, h=h, a=a, y=y, sinks=sinks))
        xc = xn

    dx, loss_part = _loss_grad(xc, tgt, tm)

    d_pre, d_post = [None] * 4, [None] * 4
    d_ssm_in, d_ssm_out, d_att_in, d_att_out = [None] * 2, [None] * 2, [None] * 2, [None] * 2
    d_cw, d_cb, d_dtb, d_alog, d_dsk, d_gn, d_sinks = ([None] * 2 for _ in range(7))
    for i in reversed(range(4)):
        j = i // 2
        s = saved[i]
        wn_pre, wn_post = pre_norm[i].reshape(1, D_MODEL), post_norm[i].reshape(1, D_MODEL)
        if i % 2 == 0:
            da3, dy, d_post[i] = _mm_dout(s["y"], dx, wn_post, w_ssm_out[j], 4, tm, f"ssm_dout_{j}")
            d_ssm_out[j] = _dw_rows(s["a"], dy, 4, tm, f"ssm_dwout_{j}")
            du, d_dtb[j], d_alog[j], d_dsk[j], dgn = _ssm2_bwd(
                s["u"], s["xbc"], s["pre"], s["yp"], s["hst"], da3, s["dtb"], s["alog"], s["dsk"], s["gn"],
                f"ssm_core_bwd_{j}")
            du, dcw, dcb = _conv_bwd(s["uc"], du, s["cw"], s["cb"], f"ssm_conv_bwd_{j}")
            d_cw[j], d_cb[j], d_gn[j] = _conv_untile(dcw), _conv_untile(dcb), dgn.reshape(1, SSM_INNER)
            d_ssm_in[j] = _ssm_w_in_untile(_dw_cols(s["h"], du, 5, tm, f"ssm_dwin_{j}"))
            dx, d_pre[i] = _mm_dh(du, s["w_in"], s["x"], dx, wn_pre, 5, tm, f"ssm_dh_{j}")
        else:
            da, dy, d_post[i] = _mm_dout(s["y"], dx, wn_post, w_att_out[j], 4, tm, f"att_dout_{j}")
            d_att_out[j] = _dw_rows(s["a"], dy, 4, tm, f"att_dwout_{j}")
            du, d_sinks[j] = _att2_bwd(s["u"], cos_t, sin_t, s["sinks"], da, f"att_core_bwd_{j}")
            d_att_in[j] = _dw_cols(s["h"], du, 5, tm, f"att_dwin_{j}")
            dx, d_pre[i] = _mm_dh(du, w_att_in[j], s["x"], dx, wn_pre, 5, tm, f"att_dh_{j}")

    gs = _pack_grads(jnp.stack(d_ssm_in), jnp.stack(d_ssm_out), jnp.stack(d_att_in), jnp.stack(d_att_out),
                     jnp.stack(d_cw))
    r1 = _exchange_sibling(gs)
    tiles = (256, 256, ROWS_REST // 7)
    pairs = [_pair_sum(g, r, cidx, tr, f"rs_pair_sum_{k}") for k, (g, r, tr) in enumerate(zip(gs, r1, tiles))]
    parts = _exchange_chips(pairs)
    flat = lambda t: t.reshape(2048, t.shape[-1])
    a4 = _adamw(parts[0], flat(ssm_w_in), flat(m_ssm_w_in), flat(v_ssm_w_in), tiles[0], "adamw_ssm_in")
    b4 = _adamw(parts[1], flat(att_w_in), flat(m_att_w_in), flat(v_att_w_in), tiles[1], "adamw_att_in")
    c4 = _adamw(parts[2], _pack_rest(ssm_w_out, att_w_out, ssm_conv_w), _pack_rest(m_ssm_w_out, m_att_w_out, m_ssm_conv_w),
                _pack_rest(v_ssm_w_out, v_att_w_out, v_ssm_conv_w), tiles[2], "adamw_rest")
    big = []
    for k in range(4):
        o_ssm_out, o_att_out, o_conv = _unpack_rest(c4[k])
        big.append((a4[k].reshape(2, 1024, 772), o_ssm_out, b4[k].reshape(2, 1024, 320), o_att_out, o_conv))

    small_local = _pack_small(jnp.concatenate(d_pre, axis=0), jnp.concatenate(d_post, axis=0),
                              jnp.concatenate(d_cb, axis=0), jnp.concatenate(d_gn, axis=0),
                              jnp.concatenate(d_dtb, axis=0), jnp.concatenate(d_alog, axis=0),
                              jnp.concatenate(d_dsk, axis=0), jnp.concatenate(d_sinks, axis=0), loss_part[0, 0])
    small_all = _all_gather_direct(small_local, "all_gather_small")
    ws = _pack_small(pre_norm, post_norm, ssm_conv_b, ssm_gate_norm, ssm_dt_bias, ssm_a_log, ssm_d, att_sinks)
    ms = _pack_small(m_pre_norm, m_post_norm, m_ssm_conv_b, m_ssm_gate_norm, m_ssm_dt_bias, m_ssm_a_log, m_ssm_d,
                     m_att_sinks)
    vs = _pack_small(v_pre_norm, v_post_norm, v_ssm_conv_b, v_ssm_gate_norm, v_ssm_dt_bias, v_ssm_a_log, v_ssm_d,
                     v_att_sinks)
    small4 = _adamw(small_all, ws, ms, vs, ROWS_SMALL, "adamw_small")
    loss = small4[0][32, 224]
    small = [_unpack_small(t) for t in small4]

    outs = [loss, dx.reshape(1, t_len, D_MODEL)]
    for k in range(4):
        b_ssm_in, b_ssm_out, b_att_in, b_att_out, b_conv = big[k]
        s_pre, s_post, s_cb, s_dtb, s_alog, s_d, s_gn, s_sinks = small[k]
        outs += [s_pre, s_post, b_ssm_in, b_conv, s_cb, s_dtb, s_alog, s_d, s_gn, b_ssm_out, b_att_in, s_sinks,
                 b_att_out]
    return tuple(outs)
```

```python
import jax
import jax.numpy as jnp
from jax import lax
from jax.experimental import pallas as pl
from jax.experimental.pallas import tpu as pltpu

f32 = jnp.float32
MXU_DTYPE = jnp.bfloat16
SDS = jax.ShapeDtypeStruct
MESH = pl.DeviceIdType.MESH

D_MODEL = 1024
EPS = 1e-6
TILE = 256
CHUNK = 128
ATT_ROWS = 4 * CHUNK
SSM_HEADS = 32
SSM_GROUPS = 8
SSM_P = 64
SSM_N = 128
SSM_INNER = 2048
SSM_IN = 6176
SSM_NT = 25
SSM_KT = 8
ATT_NT = 10
ATT_KT = 4
ATT_IN = 2560
ROPE_THETA = 500000.0
N_DEV = 8
VMEM_LIMIT = 56 * 1024 * 1024

ADAM_LR = 0.001
ADAM_B1 = 0.9
ADAM_B2 = 0.999
ADAM_EPS = 1e-08
ADAM_WD = 0.01
ADAM_STEP = 10

ROWS_SSM_OUT = 2 * 256
ROWS_ATT_OUT = 2 * 128
ROWS_SMALL = 40


def _pc(body, name, **kw):
    return pl.pallas_call(body, name=name, **kw)


def _params(sem):
    return pltpu.CompilerParams(dimension_semantics=sem, vmem_limit_bytes=VMEM_LIMIT)


def _sigmoid(x):
    return 1.0 / (1.0 + jnp.exp(-x))


def _silu(x):
    return x * _sigmoid(x)


def _softplus(x):
    return jnp.maximum(x, 0.0) + jnp.log(1.0 + jnp.exp(-jnp.abs(x)))


def _mx(x):
    return x.astype(MXU_DTYPE)


def _dot(a, b):
    return jnp.dot(_mx(a), _mx(b), preferred_element_type=f32)


def _dot_nt(a, b):
    return lax.dot_general(_mx(a), _mx(b), (((1,), (1,)), ((), ())), preferred_element_type=f32)


def _dot_tn(a, b):
    return lax.dot_general(_mx(a), _mx(b), (((0,), (0,)), ((), ())), preferred_element_type=f32)


def _rms_fwd(x, w):
    r = lax.rsqrt(jnp.mean(x * x, axis=-1, keepdims=True) + EPS)
    return x * r * w


def _rms_bwd(x, w, dy):
    r = lax.rsqrt(jnp.mean(x * x, axis=-1, keepdims=True) + EPS)
    xh = x * r
    dw = jnp.sum(dy * xh, axis=0, keepdims=True)
    g = dy * w
    dx = r * (g - xh * jnp.mean(g * xh, axis=-1, keepdims=True))
    return dx, dw


def _mm_in(x, wn, w, ntb, tm, name):
    t_len, d = x.shape
    nt = w.shape[1] // TILE

    def body(x_ref, wn_ref, w_ref, u_ref, h_ref):
        @pl.when(pl.program_id(1) == 0)
        def _():
            h_ref[...] = _rms_fwd(x_ref[...], wn_ref[...]).astype(h_ref.dtype)
        h = h_ref[...]
        for t in range(ntb):
            u_ref[t] = jnp.dot(h, w_ref[:, TILE * t:TILE * (t + 1)], preferred_element_type=f32)

    return _pc(
        body, name, grid=(t_len // tm, nt // ntb),
        in_specs=[pl.BlockSpec((tm, d), lambda i, j: (i, 0)),
                  pl.BlockSpec((1, d), lambda i, j: (0, 0)),
                  pl.BlockSpec((d, ntb * TILE), lambda i, j: (0, j))],
        out_specs=[pl.BlockSpec((ntb, tm, TILE), lambda i, j: (j, i, 0)),
                   pl.BlockSpec((tm, d), lambda i, j: (i, 0))],
        out_shape=[SDS((nt, t_len, TILE), f32), SDS((t_len, d), MXU_DTYPE)],
        compiler_params=_params(("arbitrary", "arbitrary")),
    )(x, wn, w)


def _mm_dout(y, dxn, wn, w, ntb, tm, name):
    t_len, d = y.shape
    nt = w.shape[0] // TILE

    def body(y_ref, dxn_ref, wn_ref, w_ref, da_ref, dy_ref, dwn_ref):
        i, j = pl.program_id(0), pl.program_id(1)

        @pl.when((i == 0) & (j == 0))
        def _():
            dwn_ref[...] = jnp.zeros_like(dwn_ref)

        @pl.when(j == 0)
        def _():
            dy, dw = _rms_bwd(y_ref[...], wn_ref[...], dxn_ref[...])
            dy_ref[...] = dy.astype(dy_ref.dtype)
            dwn_ref[...] += dw
        dy = dy_ref[...]
        for t in range(ntb):
            da_ref[t] = _dot_nt(dy, w_ref[TILE * t:TILE * (t + 1), :])

    return _pc(
        body, name, grid=(t_len // tm, nt // ntb),
        in_specs=[pl.BlockSpec((tm, d), lambda i, j: (i, 0)),
                  pl.BlockSpec((tm, d), lambda i, j: (i, 0)),
                  pl.BlockSpec((1, d), lambda i, j: (0, 0)),
                  pl.BlockSpec((ntb * TILE, d), lambda i, j: (j, 0))],
        out_specs=[pl.BlockSpec((ntb, tm, TILE), lambda i, j: (j, i, 0)),
                   pl.BlockSpec((tm, d), lambda i, j: (i, 0)),
                   pl.BlockSpec((1, d), lambda i, j: (0, 0))],
        out_shape=[SDS((nt, t_len, TILE), f32), SDS((t_len, d), MXU_DTYPE), SDS((1, d), f32)],
        compiler_params=_params(("arbitrary", "arbitrary")),
    )(y, dxn, wn, w)


def _mm_out(a, w, x, wn, ktb, tm, name):
    kt, t_len, _ = a.shape
    d = w.shape[1]
    nk = kt // ktb

    def body(a_ref, w_ref, x_ref, wn_ref, y_ref, xn_ref, acc):
        k = pl.program_id(1)

        @pl.when(k == 0)
        def _():
            acc[...] = jnp.zeros_like(acc)
        s = acc[...]
        for t in range(ktb):
            s = s + jnp.dot(a_ref[t], w_ref[TILE * t:TILE * (t + 1), :], preferred_element_type=f32)
        acc[...] = s

        @pl.when(k == nk - 1)
        def _():
            y = acc[...]
            y_ref[...] = y
            xn_ref[...] = x_ref[...] + _rms_fwd(y, wn_ref[...])

    return _pc(
        body, name, grid=(t_len // tm, nk),
        in_specs=[pl.BlockSpec((ktb, tm, TILE), lambda i, k: (k, i, 0)),
                  pl.BlockSpec((ktb * TILE, d), lambda i, k: (k, 0)),
                  pl.BlockSpec((tm, d), lambda i, k: (i, 0)),
                  pl.BlockSpec((1, d), lambda i, k: (0, 0))],
        out_specs=[pl.BlockSpec((tm, d), lambda i, k: (i, 0)),
                   pl.BlockSpec((tm, d), lambda i, k: (i, 0))],
        out_shape=[SDS((t_len, d), f32), SDS((t_len, d), f32)],
        scratch_shapes=[pltpu.VMEM((tm, d), f32)],
        compiler_params=_params(("arbitrary", "arbitrary")),
    )(a, w, x, wn)


def _mm_dh(du, w, x, dxn, wn, ktb, tm, name):
    kt, t_len, _ = du.shape
    d = w.shape[0]
    nk = kt // ktb

    def body(du_ref, w_ref, x_ref, dxn_ref, wn_ref, dx_ref, dwn_ref, acc):
        i, k = pl.program_id(0), pl.program_id(1)

        @pl.when((i == 0) & (k == 0))
        def _():
            dwn_ref[...] = jnp.zeros_like(dwn_ref)

        @pl.when(k == 0)
        def _():
            acc[...] = jnp.zeros_like(acc)
        s = acc[...]
        for t in range(ktb):
            s = s + _dot_nt(du_ref[t], w_ref[:, TILE * t:TILE * (t + 1)])
        acc[...] = s

        @pl.when(k == nk - 1)
        def _():
            dxp, dw = _rms_bwd(x_ref[...], wn_ref[...], acc[...])
            dx_ref[...] = dxn_ref[...] + dxp
            dwn_ref[...] += dw

    return _pc(
        body, name, grid=(t_len // tm, nk),
        in_specs=[pl.BlockSpec((ktb, tm, TILE), lambda i, k: (k, i, 0)),
                  pl.BlockSpec((d, ktb * TILE), lambda i, k: (0, k)),
                  pl.BlockSpec((tm, d), lambda i, k: (i, 0)),
                  pl.BlockSpec((tm, d), lambda i, k: (i, 0)),
                  pl.BlockSpec((1, d), lambda i, k: (0, 0))],
        out_specs=[pl.BlockSpec((tm, d), lambda i, k: (i, 0)),
                   pl.BlockSpec((1, d), lambda i, k: (0, 0))],
        out_shape=[SDS((t_len, d), f32), SDS((1, d), f32)],
        scratch_shapes=[pltpu.VMEM((tm, d), f32)],
        compiler_params=_params(("arbitrary", "arbitrary")),
    )(du, w, x, dxn, wn)


def _dw_cols(a, b, ntb, tk, name):
    t_len, kdim = a.shape
    nt = b.shape[0]

    def body(a_ref, b_ref, o_ref):
        @pl.when(pl.program_id(1) == 0)
        def _():
            o_ref[...] = jnp.zeros_like(o_ref)
        av = a_ref[...]
        for s in range(ntb):
            o_ref[:, TILE * s:TILE * (s + 1)] += _dot_tn(av, b_ref[s])

    return _pc(
        body, name, grid=(nt // ntb, t_len // tk),
        in_specs=[pl.BlockSpec((tk, kdim), lambda j, t: (t, 0)),
                  pl.BlockSpec((ntb, tk, TILE), lambda j, t: (j, t, 0))],
        out_specs=pl.BlockSpec((kdim, ntb * TILE), lambda j, t: (0, j)),
        out_shape=SDS((kdim, nt * TILE), f32),
        compiler_params=_params(("arbitrary", "arbitrary")),
    )(a, b)


def _dw_rows(a, b, ktb, tk, name):
    kt, t_len, _ = a.shape
    d = b.shape[1]

    def body(a_ref, b_ref, o_ref):
        @pl.when(pl.program_id(1) == 0)
        def _():
            o_ref[...] = jnp.zeros_like(o_ref)
        bv = b_ref[...]
        for s in range(ktb):
            o_ref[TILE * s:TILE * (s + 1), :] += _dot_tn(a_ref[s], bv)

    return _pc(
        body, name, grid=(kt // ktb, t_len // tk),
        in_specs=[pl.BlockSpec((ktb, tk, TILE), lambda k, t: (k, t, 0)),
                  pl.BlockSpec((tk, d), lambda k, t: (t, 0))],
        out_specs=pl.BlockSpec((ktb * TILE, d), lambda k, t: (k, 0)),
        out_shape=SDS((kt * TILE, d), f32),
        compiler_params=_params(("arbitrary", "arbitrary")),
    )(a, b)


def _loss_grad(x, tgt, tm):
    t_len, d = x.shape

    def body(x_ref, t_ref, dx_ref, l_ref):
        @pl.when(pl.program_id(0) == 0)
        def _():
            l_ref[...] = jnp.zeros_like(l_ref)
        e = x_ref[...] - t_ref[...]
        dx_ref[...] = e * (1.0 / d)
        row = jnp.mean(e * e, axis=-1, keepdims=True)
        l_ref[...] += 0.5 * jnp.sum(row, axis=0, keepdims=True)

    return _pc(
        body, "loss_grad", grid=(t_len // tm,),
        in_specs=[pl.BlockSpec((tm, d), lambda i: (i, 0)), pl.BlockSpec((tm, d), lambda i: (i, 0))],
        out_specs=[pl.BlockSpec((tm, d), lambda i: (i, 0)), pl.BlockSpec((1, 128), lambda i: (0, 0))],
        out_shape=[SDS((t_len, d), f32), SDS((1, 128), f32)],
        compiler_params=_params(("arbitrary",)),
    )(x, tgt)


def _tri(lower):
    r = lax.broadcasted_iota(jnp.int32, (CHUNK, CHUNK), 0)
    c = lax.broadcasted_iota(jnp.int32, (CHUNK, CHUNK), 1)
    return ((c <= r) if lower else (c >= r)).astype(f32)


def _split(x, n):
    parts = []
    for _ in range(n):
        p = x.astype(jnp.bfloat16)
        parts.append(p)
        x = x - p.astype(f32)
    return parts


def _dot_exact(a, b, dims, split_a, n=3):
    out = None
    if split_a:
        b = b.astype(jnp.bfloat16)
        for p in _split(a, n):
            t = lax.dot_general(p, b, (dims, ((), ())), preferred_element_type=f32)
            out = t if out is None else out + t
    else:
        a = a.astype(jnp.bfloat16)
        for p in _split(b, n):
            t = lax.dot_general(a, p, (dims, ((), ())), preferred_element_type=f32)
            out = t if out is None else out + t
    return out


def _dt_path(dt_raw, dtb, alog):
    dtr = dt_raw + dtb
    dt = _softplus(dtr)
    a_neg = -jnp.exp(alog)
    a = dt * a_neg
    acs = _dot_exact(_tri(True), a, ((1,), (0,)), False)
    acs_t = _dot_exact(a, _tri(False), ((0,), (0,)), True)
    return dtr, dt, a_neg, acs, acs_t


CONV_ROWS = 1024
CONV_SUB = 32


def _conv_specs(nb, rows, rev):
    def ridx(i):
        return (nb - 1 - i) if rev else i
    return [
        pl.BlockSpec((1, rows, TILE), lambda p, i: (p, ridx(i), 0)),
        pl.BlockSpec((1, 8, TILE), lambda p, i: (p, jnp.maximum(ridx(i) * (rows // 8) - 1, 0), 0)),
        pl.BlockSpec((1, 4, TILE), lambda p, i: (p, 0, 0)),
        pl.BlockSpec((1, 1, TILE), lambda p, i: (p, 0, 0)),
    ]


CONV_NTB = 4


def _mm_in_conv(x, wn, w, cw, cb, tm, name):
    t_len, d = x.shape
    nt = w.shape[1] // TILE

    def body(x_ref, wn_ref, w_ref, cw_ref, cb_ref, u_ref, o_ref, h_s, carry_s, win_s):
        i, j = pl.program_id(0), pl.program_id(1)

        @pl.when(j == 0)
        def _():
            h_s[...] = _rms_fwd(x_ref[...], wn_ref[...]).astype(h_s.dtype)
        h = h_s[...]
        for t in range(CONV_NTB):
            p = CONV_NTB * j + t
            ut = jnp.dot(h, w_ref[:, TILE * t:TILE * (t + 1)], preferred_element_type=f32)
            u_ref[t] = ut
            win_s[t, 0:8, :] = jnp.where(i > 0, carry_s[p], 0.0)
            win_s[t, 8:8 + tm, :] = ut
            carry_s[p] = ut[tm - 8:tm, :]
            wk = [cw_ref[t, k:k + 1, :] for k in range(4)]
            b = cb_ref[t]
            for s in range(tm // CONV_SUB):
                o = CONV_SUB * s
                acc = b
                for k in range(4):
                    acc = acc + wk[k] * win_s[t, 5 + k + o:5 + k + o + CONV_SUB, :]
                o_ref[t, o:o + CONV_SUB, :] = _silu(acc)

    return _pc(
        body, name, grid=(t_len // tm, nt // CONV_NTB),
        in_specs=[pl.BlockSpec((tm, d), lambda i, j: (i, 0)),
                  pl.BlockSpec((1, d), lambda i, j: (0, 0)),
                  pl.BlockSpec((d, CONV_NTB * TILE), lambda i, j: (0, j)),
                  pl.BlockSpec((CONV_NTB, 4, TILE), lambda i, j: (j, 0, 0)),
                  pl.BlockSpec((CONV_NTB, 1, TILE), lambda i, j: (j, 0, 0))],
        out_specs=[pl.BlockSpec((CONV_NTB, tm, TILE), lambda i, j: (j, i, 0)),
                   pl.BlockSpec((CONV_NTB, tm, TILE), lambda i, j: (j, i, 0))],
        out_shape=[SDS((nt, t_len, TILE), f32), SDS((nt, t_len, TILE), f32)],
        scratch_shapes=[pltpu.VMEM((tm, d), MXU_DTYPE), pltpu.VMEM((nt, 8, TILE), f32),
                        pltpu.VMEM((CONV_NTB, 8 + tm, TILE), f32)],
        compiler_params=_params(("arbitrary", "arbitrary")),
    )(x, wn, w, cw, cb)


def _conv_bwd(u, du, cw, cb, name):
    t_len = u.shape[1]
    rows = min(CONV_ROWS, t_len)
    nb = t_len // rows

    def body(u_ref, halo_ref, cw_ref, cb_ref, d_ref, o_ref, dcw_ref, dcb_ref, carry_s, win_s, dp_s):
        i = pl.program_id(1)
        ri = nb - 1 - i

        @pl.when(i == 0)
        def _():
            carry_s[...] = jnp.zeros_like(carry_s)
            dcw_ref[...] = jnp.zeros_like(dcw_ref)
            dcb_ref[...] = jnp.zeros_like(dcb_ref)
        win_s[0:8, :] = jnp.where(ri > 0, halo_ref[0], 0.0)
        win_s[8:8 + rows, :] = u_ref[0]
        w = [cw_ref[0, k:k + 1, :] for k in range(4)]
        b = cb_ref[0]
        def fold(v):
            return jnp.sum(v.reshape(CONV_SUB // 8, 8, TILE), axis=0)

        dw = [jnp.zeros((8, TILE), f32)] * 4
        db = jnp.zeros((8, TILE), f32)
        for s in range(rows // CONV_SUB):
            o = CONV_SUB * s
            xk = [win_s[5 + k + o:5 + k + o + CONV_SUB, :] for k in range(4)]
            pre = b
            for k in range(4):
                pre = pre + w[k] * xk[k]
            sg = _sigmoid(pre)
            dpre = d_ref[0, o:o + CONV_SUB, :] * (sg * (1.0 + pre * (1.0 - sg)))
            dp_s[o:o + CONV_SUB, :] = dpre
            dw = [dw[k] + fold(dpre * xk[k]) for k in range(4)]
            db = db + fold(dpre)
        dw = [jnp.sum(t, axis=0, keepdims=True) for t in dw]
        db = jnp.sum(db, axis=0, keepdims=True)
        dp_s[rows:rows + 8, :] = carry_s[...]
        for s in range(rows // CONV_SUB):
            o = CONV_SUB * s
            acc = w[0] * dp_s[3 + o:3 + o + CONV_SUB, :]
            for k in range(1, 4):
                acc = acc + w[k] * dp_s[3 - k + o:3 - k + o + CONV_SUB, :]
            o_ref[0, o:o + CONV_SUB, :] = acc
        carry_s[...] = dp_s[0:8, :]
        for k in range(4):
            dcw_ref[0, k:k + 1, :] += dw[k]
        dcb_ref[0] += db

    return _pc(
        body, name, grid=(16, nb),
        in_specs=_conv_specs(nb, rows, True) + [pl.BlockSpec((1, rows, TILE), lambda p, i: (8 + p, nb - 1 - i, 0))],
        out_specs=[pl.BlockSpec((1, rows, TILE), lambda p, i: (8 + p, nb - 1 - i, 0)),
                   pl.BlockSpec((1, 4, TILE), lambda p, i: (p, 0, 0)),
                   pl.BlockSpec((1, 1, TILE), lambda p, i: (p, 0, 0))],
        out_shape=[SDS(du.shape, f32), SDS((16, 4, TILE), f32), SDS((16, 1, TILE), f32)],
        input_output_aliases={4: 0},
        scratch_shapes=[pltpu.VMEM((8, TILE), f32), pltpu.VMEM((8 + rows, TILE), f32),
                        pltpu.VMEM((rows + 8, TILE), f32)],
        compiler_params=_params(("arbitrary", "arbitrary")),
    )(u, u, cw, cb, du)


def _collapse_matrix(g):
    r = lax.broadcasted_iota(jnp.int32, (SSM_HEADS, TILE), 0)
    c = lax.broadcasted_iota(jnp.int32, (SSM_HEADS, TILE), 1)
    return ((c // SSM_P) + 4 * g == r).astype(jnp.bfloat16)


def _ssd_prelude(dt_raw, dtb, alog, dsk, colx_s, scx_s):
    dtr, dt, a_neg, acs, acs_t = _dt_path(dt_raw, dtb, alog)
    a_end = acs[CHUNK - 1:CHUNK, :]
    lane = lax.broadcasted_iota(jnp.int32, (1, 2 * SSM_P), 1)
    lane4 = lax.broadcasted_iota(jnp.int32, (1, TILE), 1)
    sub8 = lax.broadcasted_iota(jnp.int32, (8, 1), 0)

    def row4(v, g):
        e = [v[:, 4 * g + r:4 * g + r + 1] for r in range(4)]
        return jnp.where(lane4 < 64, e[0], jnp.where(lane4 < 128, e[1], jnp.where(lane4 < 192, e[2], e[3])))

    for g in range(SSM_GROUPS):
        for k, arr in enumerate((dt, acs)):
            for half in range(2):
                h0 = 4 * g + 2 * half
                colx_s[k, g, :, 128 * half:128 * (half + 1)] = jnp.where(
                    lane < SSM_P, arr[:, h0:h0 + 1], arr[:, h0 + 1:h0 + 2])
        scx_s[g] = jnp.where(sub8 == 0, row4(dsk, g), jnp.where(sub8 == 1, row4(a_end, g), 0.0))
    return dtr, dt, a_neg, acs_t


def _ssm_core_specs(nc, rev):
    def cidx(c):
        return (nc - 1 - c) if rev else c
    return [
        pl.BlockSpec((SSM_KT, CHUNK, TILE), lambda c: (0, cidx(c), 0)),
        pl.BlockSpec((1, CHUNK, TILE), lambda c: (SSM_KT, cidx(c), 0)),
        pl.BlockSpec((16, CHUNK, TILE), lambda c: (0, cidx(c), 0)),
        pl.BlockSpec((1, SSM_HEADS), lambda c: (0, 0)),
        pl.BlockSpec((1, SSM_HEADS), lambda c: (0, 0)),
        pl.BlockSpec((1, SSM_HEADS), lambda c: (0, 0)),
        pl.BlockSpec((SSM_KT, 1, TILE), lambda c: (0, 0, 0)),
        pl.BlockSpec((2, SSM_GROUPS, CHUNK, TILE), lambda c: (0, 0, cidx(c), 0)),
        pl.BlockSpec((1, SSM_GROUPS, 8, TILE), lambda c: (cidx(c), 0, 0, 0)),
        pl.BlockSpec((1, SSM_HEADS, CHUNK), lambda c: (cidx(c), 0, 0)),
    ]


PRE_CHUNKS = 4


def _ssm_pre(u, dtb, alog, dsk, name):
    t_len = u.shape[1]
    nc = t_len // CHUNK
    per = min(PRE_CHUNKS, nc)

    def body(dt_ref, dtb_ref, alog_ref, dsk_ref, colx_ref, scx_ref, acst_ref):
        for c in range(per):
            rows = pl.ds(CHUNK * c, CHUNK)
            _, _, _, acs_t = _ssd_prelude(dt_ref[0, rows, 0:SSM_HEADS], dtb_ref[...], alog_ref[...], dsk_ref[...],
                                          colx_ref.at[:, :, rows, :], scx_ref.at[c])
            acst_ref[c] = acs_t

    vec = pl.BlockSpec((1, SSM_HEADS), lambda i: (0, 0))
    return _pc(
        body, name, grid=(nc // per,),
        in_specs=[pl.BlockSpec((1, per * CHUNK, TILE), lambda i: (SSM_KT, i, 0)), vec, vec, vec],
        out_specs=[pl.BlockSpec((2, SSM_GROUPS, per * CHUNK, TILE), lambda i: (0, 0, i, 0)),
                   pl.BlockSpec((per, SSM_GROUPS, 8, TILE), lambda i: (i, 0, 0, 0)),
                   pl.BlockSpec((per, SSM_HEADS, CHUNK), lambda i: (i, 0, 0))],
        out_shape=[SDS((2, SSM_GROUPS, t_len, TILE), f32), SDS((nc, SSM_GROUPS, 8, TILE), f32),
                   SDS((nc, SSM_HEADS, CHUNK), f32)],
        compiler_params=_params(("arbitrary",)),
    )(u, dtb, alog, dsk)


def _stack_cols_rows(acx, rows):
    ac = jnp.concatenate([acx[:, SSM_P * r:SSM_P * r + 1] for r in range(4)], axis=0)
    ar = jnp.concatenate([jnp.broadcast_to(rows[r:r + 1, :], (CHUNK, CHUNK)) for r in range(4)], axis=0)
    return ac, ar


def _ssm2_fwd(u, xbc, pre, dtb, alog, dsk, gn, name):
    t_len = u.shape[1]
    nc = t_len // CHUNK

    def body(z_ref, dt_ref, x_ref, dtb_ref, alog_ref, dsk_ref, gn_ref, colx_ref, scx_ref, acst_ref,
             a3_ref, yp_ref, hst_ref, h_s, xt_s, yd_s):
        c = pl.program_id(0)

        @pl.when(c == 0)
        def _():
            h_s[...] = jnp.zeros_like(h_s)
        acs_t = acst_ref[0]
        causal = _tri4()

        def group(g, s1):
            xs = x_ref[g]
            bm, cm = x_ref[8 + g, :, 0:SSM_N], x_ref[8 + g, :, SSM_N:2 * SSM_N]
            cb = _dot_nt(cm, bm)
            sc = scx_ref[0, g]
            a_end = sc[1:2, :]
            rows = pltpu.roll(acs_t, (SSM_HEADS - 4 * g) % SSM_HEADS, 0)
            hp = h_s[g]
            xt = xs * colx_ref[0, g]
            xt_s[...] = xt
            acx = colx_ref[1, g]
            ac_st, ar_st = _stack_cols_rows(acx, rows)
            m = jnp.concatenate([cb] * 4, axis=0) * jnp.exp(jnp.where(causal, ac_st - ar_st, -jnp.inf))
            for r in range(4):
                hd = slice(SSM_P * r, SSM_P * (r + 1))
                yd_s[:, hd] = _dot(m[CHUNK * r:CHUNK * (r + 1), :], xt_s[:, hd])
            yp_ref[g] = yd_s[...] + _dot(cm, hp) * jnp.exp(acx) + sc[0:1, :] * xs
            hst_ref[0, g] = hp
            h_s[g] = hp * jnp.exp(a_end) + _dot_tn(bm, xt * jnp.exp(a_end - acx))
            y2 = yp_ref[g] * _silu(z_ref[g])
            return s1 + jnp.sum(y2 * y2, axis=1, keepdims=True)

        s1 = lax.fori_loop(0, SSM_GROUPS // 2, lambda i, c: group(2 * i + 1, group(2 * i, c)),
                           jnp.zeros((CHUNK, 1), f32))
        rinv = lax.rsqrt(s1 * (1.0 / SSM_INNER) + EPS)

        def gate(g, carry):
            y2 = yp_ref[g] * _silu(z_ref[g])
            a3_ref[g] = (y2 * rinv * gn_ref[g]).astype(a3_ref.dtype)
            return carry

        lax.fori_loop(0, SSM_GROUPS, gate, 0)

    return _pc(
        body, name, grid=(nc,),
        in_specs=_ssm_core_specs(nc, False),
        out_specs=[pl.BlockSpec((SSM_KT, CHUNK, TILE), lambda c: (0, c, 0)),
                   pl.BlockSpec((SSM_KT, CHUNK, TILE), lambda c: (0, c, 0)),
                   pl.BlockSpec((1, SSM_GROUPS, SSM_N, TILE), lambda c: (c, 0, 0, 0))],
        out_shape=[SDS((SSM_KT, t_len, TILE), MXU_DTYPE), SDS((SSM_KT, t_len, TILE), f32),
                   SDS((nc, SSM_GROUPS, SSM_N, TILE), f32)],
        scratch_shapes=[pltpu.VMEM((SSM_GROUPS, SSM_N, TILE), f32), pltpu.VMEM((CHUNK, TILE), f32),
                        pltpu.VMEM((CHUNK, TILE), f32)],
        compiler_params=_params(("arbitrary",)),
    )(u, u, xbc, dtb, alog, dsk, gn, *pre)


def _ssm2_bwd(u, xbc, pre, yp, hst, da3, dtb, alog, dsk, gn, name):
    t_len = u.shape[1]
    nc = t_len // CHUNK

    def body(z_ref, dt_ref, x_ref, dtb_ref, alog_ref, dsk_ref, gn_ref, colx_s, scx_ref, acst_ref,
             yp_ref, hst_ref, da3_ref, du_ref, ddtb_ref, dalog_ref, ddsk_ref, dgn_ref,
             dh_s, xt_s, dy_s, dxt_s, ddtx_s, dacx_s, ddx_s, drow_s, dm_s, dmt_s):
        step = pl.program_id(0)

        @pl.when(step == 0)
        def _():
            dh_s[...] = jnp.zeros_like(dh_s)
            ddtb_ref[...] = jnp.zeros_like(ddtb_ref)
            dalog_ref[...] = jnp.zeros_like(dalog_ref)
            ddsk_ref[...] = jnp.zeros_like(ddsk_ref)
            dgn_ref[...] = jnp.zeros_like(dgn_ref)
        dtr = dt_ref[0, :, 0:SSM_HEADS] + dtb_ref[...]
        dt = _softplus(dtr)
        a_neg = -jnp.exp(alog_ref[...])
        acs_t = acst_ref[0]
        causal = _tri4()
        causal_t = (lax.broadcasted_iota(jnp.int32, (ATT_ROWS, CHUNK), 1)
                    >= lax.broadcasted_iota(jnp.int32, (ATT_ROWS, CHUNK), 0) % CHUNK)
        last = (lax.broadcasted_iota(jnp.int32, (1, CHUNK), 1) == CHUNK - 1).astype(f32)
        lane = lax.broadcasted_iota(jnp.int32, (1, TILE), 1)
        sub32 = lax.broadcasted_iota(jnp.int32, (SSM_HEADS, 1), 0)
        drow_s[...] = jnp.zeros_like(drow_s)

        def sums(g, carry):
            s1, s2 = carry
            y2 = yp_ref[g] * _silu(z_ref[g])
            g3 = da3_ref[g] * gn_ref[g]
            return (s1 + jnp.sum(y2 * y2, axis=1, keepdims=True), s2 + jnp.sum(g3 * y2, axis=1, keepdims=True))

        zcol = jnp.zeros((CHUNK, 1), f32)
        carry = (zcol, zcol)
        for g in range(SSM_GROUPS):
            carry = sums(g, carry)
        s1, s2 = carry
        rinv = lax.rsqrt(s1 * (1.0 / SSM_INNER) + EPS)
        m2 = s2 * rinv * rinv * rinv * (1.0 / SSM_INNER)

        def group(g, carry):
            z = z_ref[g]
            sg = _sigmoid(z)
            sz = z * sg
            y = yp_ref[g]
            y2 = y * sz
            da3 = da3_ref[g]
            dgn_ref[g] += jnp.sum(da3 * y2 * rinv, axis=0, keepdims=True)
            dy2 = rinv * (da3 * gn_ref[g]) - y2 * m2
            dy = dy2 * sz
            dy_s[...] = dy
            du_ref[g] = dy2 * y * (sg * (1.0 + z * (1.0 - sg)))
            xs = x_ref[g]
            bm, cm = x_ref[8 + g, :, 0:SSM_N], x_ref[8 + g, :, SSM_N:2 * SSM_N]
            cb = _dot_nt(cm, bm)
            cbt = _dot_nt(bm, cm)
            dtx, acx = colx_s[0, g], colx_s[1, g]
            sc = scx_ref[0, g]
            a_end = sc[1:2, :]
            ex = jnp.exp(acx)
            wdx = jnp.exp(a_end - acx)
            eend = jnp.exp(a_end)
            rows = pltpu.roll(acs_t, (SSM_HEADS - 4 * g) % SSM_HEADS, 0)
            hp = hst_ref[0, g]
            dhn = dh_s[g]
            xt = xs * dtx
            xt_s[...] = xt
            ch = _dot(cm, hp)
            gy = dy * ex
            dcm = _dot_nt(gy, hp)
            dh_s[g] = _dot_tn(cm, gy) + dhn * eend
            q = _dot(bm, dhn)
            dbm = _dot_nt(xt * wdx, dhn)
            qx = q * xt * wdx
            v_end = jnp.sum(dhn * hp, axis=0, keepdims=True) * eend + jnp.sum(qx, axis=0, keepdims=True)
            ac_st, ar_st = _stack_cols_rows(acx, rows)
            lam = jnp.exp(jnp.where(causal, ac_st - ar_st, -jnp.inf))
            lam_t = jnp.exp(jnp.where(causal_t, ar_st - ac_st, -jnp.inf))
            m = jnp.concatenate([cb] * 4, axis=0) * lam
            m_t = jnp.concatenate([cbt] * 4, axis=0) * lam_t
            for r in range(4):
                hd = slice(SSM_P * r, SSM_P * (r + 1))
                rs = slice(CHUNK * r, CHUNK * (r + 1))
                dm_s[rs, :] = _dot_nt(dy_s[:, hd], xt_s[:, hd])
                dmt_s[rs, :] = _dot_nt(xt_s[:, hd], dy_s[:, hd])
                dxt_s[:, hd] = _dot(m_t[rs, :], dy_s[:, hd])
            dm = dm_s[...]
            dl = dm * lam
            dseg = dm * m
            dseg_t = dmt_s[...] * m_t
            dcb = dl[0:CHUNK] + dl[CHUNK:2 * CHUNK] + dl[2 * CHUNK:3 * CHUNK] + dl[3 * CHUNK:4 * CHUNK]
            drows = jnp.zeros((SSM_HEADS, CHUNK), f32)
            for r in range(4):
                rs = slice(CHUNK * r, CHUNK * (r + 1))
                in_head = (lane >= SSM_P * r) & (lane < SSM_P * (r + 1))
                d_ac = jnp.sum(dseg_t[rs, :], axis=0, keepdims=True)
                d_ar = jnp.sum(dseg[rs, :], axis=0, keepdims=True)
                d_aend = jnp.sum(jnp.where(in_head, v_end, 0.0), axis=1, keepdims=True)
                drows = drows + jnp.where(sub32 == r, d_ac - d_ar + last * d_aend, 0.0)
            dxt = dxt_s[...] + q * wdx
            du_ref[8 + g] = sc[0:1, :] * dy + dxt * dtx
            ddtx_s[g] = dxt * xs
            dacx_s[g] = dy * ch * ex - qx
            ddx_s[g] = jnp.broadcast_to(jnp.sum(dy * xs, axis=0, keepdims=True), (8, TILE))
            du_ref[16 + g, :, 0:SSM_N] = dbm + _dot_tn(dcb, cm)
            du_ref[16 + g, :, SSM_N:2 * SSM_N] = dcm + _dot(dcb, bm)
            drow_s[...] += pltpu.roll(drows, (4 * g) % SSM_HEADS, 0)
            return carry

        lax.fori_loop(0, SSM_GROUPS, group, 0, unroll=4)
        ddt = jnp.zeros((CHUNK, SSM_HEADS), f32)
        dacs = jnp.zeros((CHUNK, SSM_HEADS), f32)
        ddsk = jnp.zeros((8, SSM_HEADS), f32)
        for g in range(SSM_GROUPS):
            col_g = _collapse_matrix(g)
            ddt = ddt + _dot_exact(ddtx_s[g], col_g, ((1,), (1,)), True, 2)
            dacs = dacs + _dot_exact(dacx_s[g], col_g, ((1,), (1,)), True, 2)
            ddsk = ddsk + _dot_exact(ddx_s[g], col_g, ((1,), (1,)), True, 2)
        upper = _tri(False)
        da = _dot_exact(upper, dacs, ((1,), (0,)), False) + _dot_exact(upper, drow_s[...], ((1,), (1,)), False)
        ddt = ddt + da * a_neg
        dalog_ref[...] += jnp.sum(da * dt, axis=0, keepdims=True) * a_neg
        ddtr = ddt * _sigmoid(dtr)
        ddtb_ref[...] += jnp.sum(ddtr, axis=0, keepdims=True)
        ddsk_ref[...] += ddsk[0:1, :]
        du_ref[SSM_NT - 1] = jnp.zeros((CHUNK, TILE), f32)
        du_ref[SSM_NT - 1, :, 0:SSM_HEADS] = ddtr

    def rc(c):
        return nc - 1 - c

    vec = pl.BlockSpec((1, SSM_HEADS), lambda c: (0, 0))
    return _pc(
        body, name, grid=(nc,),
        in_specs=_ssm_core_specs(nc, True) + [
            pl.BlockSpec((SSM_KT, CHUNK, TILE), lambda c: (0, rc(c), 0)),
            pl.BlockSpec((1, SSM_GROUPS, SSM_N, TILE), lambda c: (rc(c), 0, 0, 0)),
            pl.BlockSpec((SSM_KT, CHUNK, TILE), lambda c: (0, rc(c), 0))],
        out_specs=[pl.BlockSpec((SSM_NT, CHUNK, TILE), lambda c: (0, rc(c), 0)), vec, vec, vec,
                   pl.BlockSpec((SSM_KT, 1, TILE), lambda c: (0, 0, 0))],
        out_shape=[SDS((SSM_NT, t_len, TILE), f32), SDS((1, SSM_HEADS), f32), SDS((1, SSM_HEADS), f32),
                   SDS((1, SSM_HEADS), f32), SDS((SSM_KT, 1, TILE), f32)],
        scratch_shapes=[pltpu.VMEM((SSM_GROUPS, SSM_N, TILE), f32), pltpu.VMEM((CHUNK, TILE), f32),
                        pltpu.VMEM((CHUNK, TILE), f32), pltpu.VMEM((CHUNK, TILE), f32),
                        pltpu.VMEM((SSM_GROUPS, CHUNK, TILE), f32), pltpu.VMEM((SSM_GROUPS, CHUNK, TILE), f32),
                        pltpu.VMEM((SSM_GROUPS, 8, TILE), f32), pltpu.VMEM((SSM_HEADS, CHUNK), f32),
                        pltpu.VMEM((4 * CHUNK, CHUNK), f32), pltpu.VMEM((4 * CHUNK, CHUNK), f32)],
        compiler_params=_params(("arbitrary",)),
    )(u, u, xbc, dtb, alog, dsk, gn, *pre, yp, hst, da3)


def _swap16(t):
    lane = lax.broadcasted_iota(jnp.int32, t.shape, 1) % 64
    return jnp.where(lane < 8, pltpu.roll(t, TILE - 8, 1), jnp.where(lane < 16, pltpu.roll(t, 8, 1), 0.0))


def _rope(t, cos_t, sin_t):
    return t * cos_t + _swap16(t) * sin_t


def _rope_bwd(g, cos_t, sin_t):
    return g * cos_t + _swap16(g * sin_t)


def _att_in_specs(nb, rev):
    def bidx(n):
        return (nb - 1 - n) if rev else n
    return [
        pl.BlockSpec((ATT_NT, CHUNK, TILE), lambda n: (0, bidx(n), 0)),
        pl.BlockSpec((2, CHUNK, TILE), lambda n: (2, jnp.maximum(bidx(n) - 1, 0), 0)),
        pl.BlockSpec((CHUNK, TILE), lambda n: (bidx(n), 0)),
        pl.BlockSpec((CHUNK, TILE), lambda n: (bidx(n), 0)),
        pl.BlockSpec((CHUNK, TILE), lambda n: (jnp.maximum(bidx(n) - 1, 0), 0)),
        pl.BlockSpec((CHUNK, TILE), lambda n: (jnp.maximum(bidx(n) - 1, 0), 0)),
        pl.BlockSpec((1, 16), lambda n: (0, 0)),
    ]


ATT_SCALE = 0.125


def _tri4():
    row = lax.broadcasted_iota(jnp.int32, (ATT_ROWS, CHUNK), 0) % CHUNK
    col = lax.broadcasted_iota(jnp.int32, (ATT_ROWS, CHUNK), 1)
    return col <= row


def _stack_heads(ref):
    return jnp.concatenate([ref[:, 64 * r:64 * (r + 1)] for r in range(4)], axis=0)


def _sink_col(sinks, g):
    return [sinks[:, 4 * g + r:4 * g + r + 1] for r in range(4)]


def _softmax_rows(s_s, pn_s, pc_s, sink, tri, has_prev):
    sub = lax.broadcasted_iota(jnp.int32, (ATT_ROWS, 1), 0)
    sk = jnp.where(sub < CHUNK, sink[0], jnp.where(sub < 2 * CHUNK, sink[1], jnp.where(sub < 3 * CHUNK, sink[2], sink[3])))
    s = jnp.where(tri, s_s[:, CHUNK:2 * CHUNK], jnp.where(has_prev, s_s[:, 0:CHUNK], -jnp.inf)) * ATT_SCALE
    m = jnp.maximum(jnp.max(s, axis=-1, keepdims=True), sk)
    p = jnp.exp(s - m)
    e_sink = jnp.exp(sk - m)
    inv = 1.0 / (jnp.sum(p, axis=-1, keepdims=True) + e_sink)
    pn = p * inv
    pc_s[...] = pn
    pn_s[:, 0:CHUNK] = jnp.where(tri, 0.0, pn)
    pn_s[:, CHUNK:2 * CHUNK] = jnp.where(tri, pn, 0.0)
    return e_sink * inv


def _att2_fwd(u, cos_t, sin_t, sinks, name):
    t_len = u.shape[1]
    nb = t_len // CHUNK

    def body(u_ref, prev_ref, cc_ref, sc_ref, cp_ref, sp_ref, sink_ref, a_ref,
             q_s, kp_s, kc_s, vp_s, vc_s, o_s, s_s, pn_s, pc_s):
        n = pl.program_id(0)
        tri = _tri4()
        cos_c, sin_c = cc_ref[...], sc_ref[...]
        kc_s[...] = _rope(u_ref[4], cos_c, sin_c)
        kp_s[...] = _rope(prev_ref[0], cp_ref[...], sp_ref[...])
        vc_s[...] = u_ref[5]
        vp_s[...] = prev_ref[1]
        sinks = sink_ref[...]
        for g in range(4):
            q_s[g] = _rope(u_ref[g], cos_c, sin_c)
        for g in range(4):
            kv = slice(64 * g, 64 * (g + 1))
            kb = jnp.concatenate([kp_s[:, kv], kc_s[:, kv]], axis=0)
            s_s[g] = _dot_nt(_stack_heads(q_s.at[g]), kb)
        for g in range(4):
            _softmax_rows(s_s.at[g], pn_s.at[g], pc_s.at[g], _sink_col(sinks, g), tri, n > 0)
        for g in range(4):
            kv = slice(64 * g, 64 * (g + 1))
            vb = jnp.concatenate([vp_s[:, kv], vc_s[:, kv]], axis=0)
            o = _dot(pn_s[g], vb)
            for r in range(4):
                o_s[g, :, 64 * r:64 * (r + 1)] = o[CHUNK * r:CHUNK * (r + 1), :]
        for g in range(4):
            a_ref[g] = (o_s[g] * _silu(u_ref[6 + g])).astype(a_ref.dtype)

    return _pc(
        body, name, grid=(nb,),
        in_specs=_att_in_specs(nb, False),
        out_specs=pl.BlockSpec((ATT_KT, CHUNK, TILE), lambda n: (0, n, 0)),
        out_shape=SDS((ATT_KT, t_len, TILE), MXU_DTYPE),
        scratch_shapes=[pltpu.VMEM((4, CHUNK, TILE), f32)] + [pltpu.VMEM((CHUNK, TILE), f32)] * 4
                       + [pltpu.VMEM((4, CHUNK, TILE), f32)] + [pltpu.VMEM((4, ATT_ROWS, 2 * CHUNK), f32)] * 2
                       + [pltpu.VMEM((4, ATT_ROWS, CHUNK), f32)],
        compiler_params=_params(("arbitrary",)),
    )(u, u, cos_t, sin_t, cos_t, sin_t, sinks)


def _att2_bwd(u, cos_t, sin_t, sinks, da, name):
    t_len = u.shape[1]
    nb = t_len // CHUNK

    def body(u_ref, prev_ref, cc_ref, sc_ref, cp_ref, sp_ref, sink_ref, da_ref, du_ref, dsink_ref,
             ck_s, cv_s, kp_s, kc_s, vp_s, vc_s, q_s, o_s, do_s, dq_s, s_s, pn_s, dp_s, dkt_s, dvt_s, pc_s):
        step = pl.program_id(0)
        nn = nb - 1 - step

        @pl.when(step == 0)
        def _():
            ck_s[...] = jnp.zeros_like(ck_s)
            cv_s[...] = jnp.zeros_like(cv_s)
            dsink_ref[...] = jnp.zeros_like(dsink_ref)
        tri = _tri4()
        cos_c, sin_c = cc_ref[...], sc_ref[...]
        cos_p, sin_p = cp_ref[...], sp_ref[...]
        kc_s[...] = _rope(u_ref[4], cos_c, sin_c)
        kp_s[...] = _rope(prev_ref[0], cos_p, sin_p)
        vc_s[...] = u_ref[5]
        vp_s[...] = prev_ref[1]
        sinks = sink_ref[...]
        lane16 = lax.broadcasted_iota(jnp.int32, (1, 16), 1)
        dsink = jnp.zeros((1, 16), f32)
        for g in range(4):
            q_g, o_g, do_g, dq_g = q_s.at[g], o_s.at[g], do_s.at[g], dq_s.at[g]
            s_g, pn_g, dp_g, pc_g = s_s.at[g], pn_s.at[g], dp_s.at[g], pc_s.at[g]
            q_g[...] = _rope(u_ref[g], cos_c, sin_c)
            gate = u_ref[6 + g]
            sg = _sigmoid(gate)
            dav = da_ref[g]
            do_g[...] = dav * (gate * sg)
            kv = slice(64 * g, 64 * (g + 1))
            kb = jnp.concatenate([kp_s[:, kv], kc_s[:, kv]], axis=0)
            vb = jnp.concatenate([vp_s[:, kv], vc_s[:, kv]], axis=0)
            q_st = _stack_heads(q_g)
            do_st = _stack_heads(do_g)
            s_g[...] = _dot_nt(q_st, kb)
            p_sink = _softmax_rows(s_g, pn_g, pc_g, _sink_col(sinks, g), tri, nn > 0)
            o = _dot(pn_g[...], vb)
            dvt_s[64 * g:64 * (g + 1), :] = _dot_tn(do_st, pn_g[...])
            dp_g[...] = _dot_nt(do_st, vb)
            delta = jnp.sum(do_st * o, axis=-1, keepdims=True)
            dpc = jnp.where(tri, dp_g[:, CHUNK:2 * CHUNK], dp_g[:, 0:CHUNK])
            dsc = pc_g[...] * (dpc - delta) * ATT_SCALE
            dp_g[:, 0:CHUNK] = jnp.where(tri, 0.0, dsc)
            dp_g[:, CHUNK:2 * CHUNK] = jnp.where(tri, dsc, 0.0)
            sd = p_sink * delta
            for r in range(4):
                rs = slice(CHUNK * r, CHUNK * (r + 1))
                o_g[:, 64 * r:64 * (r + 1)] = o[rs, :]
                ds_h = -jnp.sum(sd[rs, :], axis=0, keepdims=True)
                dsink = dsink + ds_h * (lane16 == 4 * g + r).astype(f32)
            ds = dp_g[...]
            dq = _dot(ds, kb)
            for r in range(4):
                dq_g[:, 64 * r:64 * (r + 1)] = dq[CHUNK * r:CHUNK * (r + 1), :]
            dkt_s[64 * g:64 * (g + 1), :] = _dot_tn(q_st, ds)
            du_ref[6 + g] = dav * o_g[...] * (sg * (1.0 + gate * (1.0 - sg)))
            du_ref[g] = _rope_bwd(dq_g[...], cos_c, sin_c)
        dk = dkt_s[...].T
        dv = dvt_s[...].T
        du_ref[4] = _rope_bwd(dk[CHUNK:2 * CHUNK, :], cos_c, sin_c) + ck_s[...]
        du_ref[5] = dv[CHUNK:2 * CHUNK, :] + cv_s[...]
        ck_s[...] = _rope_bwd(dk[0:CHUNK, :], cos_p, sin_p)
        cv_s[...] = dv[0:CHUNK, :]
        dsink_ref[...] += dsink

    def rb(n):
        return nb - 1 - n

    return _pc(
        body, name, grid=(nb,),
        in_specs=_att_in_specs(nb, True) + [pl.BlockSpec((ATT_KT, CHUNK, TILE), lambda n: (0, rb(n), 0))],
        out_specs=[pl.BlockSpec((ATT_NT, CHUNK, TILE), lambda n: (0, rb(n), 0)),
                   pl.BlockSpec((1, 16), lambda n: (0, 0))],
        out_shape=[SDS((ATT_NT, t_len, TILE), f32), SDS((1, 16), f32)],
        scratch_shapes=[pltpu.VMEM((CHUNK, TILE), f32)] * 6 + [pltpu.VMEM((4, CHUNK, TILE), f32)] * 4
                       + [pltpu.VMEM((4, ATT_ROWS, 2 * CHUNK), f32)] * 3
                       + [pltpu.VMEM((2 * CHUNK, 2 * CHUNK), f32)] * 2 + [pltpu.VMEM((4, ATT_ROWS, CHUNK), f32)],
        compiler_params=_params(("arbitrary",)),
    )(u, u, cos_t, sin_t, cos_t, sin_t, sinks, da)


_HBM = pl.BlockSpec(memory_space=pltpu.HBM)


def _all_gather_big(shards):
    n = len(shards)

    def body(*refs):
        x_refs, out_refs = refs[:n], refs[n:2 * n]
        send_sems, recv_sems, local_sems = refs[2 * n:]
        x, y, c = lax.axis_index("x"), lax.axis_index("y"), lax.axis_index("c")
        me, sibling = (x, y, c), (x, y, 1 - c)
        chips = [(1 - x, y), (x, 1 - y), (1 - x, 1 - y)]

        def slot(i, px, py, pc):
            return out_refs[i].at[4 * px + 2 * py + pc]

        def copy(i, k, block, to, src=None):
            return pltpu.make_async_remote_copy(
                src_ref=slot(i, *block) if src is None else src, dst_ref=slot(i, *block),
                send_sem=send_sems.at[7 * i + k], recv_sem=recv_sems.at[7 * i + k], device_id=to, device_id_type=MESH)

        mine = [pltpu.make_async_copy(x_refs[i], slot(i, *me), local_sems.at[i]) for i in range(n)]
        for cp in mine:
            cp.start()
        first = []
        for i in range(n):
            first.append(copy(i, 0, me, sibling, src=x_refs[i]))
            first += [copy(i, 1 + j, me, (*chip, c), src=x_refs[i]) for j, chip in enumerate(chips)]
        for cp in first:
            cp.start()
        passed = []
        for j, chip in enumerate(chips):
            for i in range(n):
                copy(i, 1 + j, (*chip, c), me).wait_recv()
                fwd = copy(i, 4 + j, (*chip, c), sibling)
                fwd.start()
                passed.append(fwd)
        for i in range(n):
            copy(i, 0, sibling, me).wait_recv()
            for j, chip in enumerate(chips):
                copy(i, 4 + j, (*chip, 1 - c), me).wait_recv()
        for cp in first + passed:
            cp.wait_send()
        for cp in mine:
            cp.wait()

    return _pc(
        body, "all_gather_big",
        in_specs=[_HBM] * n, out_specs=[_HBM] * n,
        out_shape=[SDS((N_DEV,) + s.shape, s.dtype) for s in shards],
        scratch_shapes=[pltpu.SemaphoreType.DMA((7 * n,)), pltpu.SemaphoreType.DMA((7 * n,)),
                        pltpu.SemaphoreType.DMA((n,))],
    )(*shards)


def _all_gather_direct(block, name):
    rows, width = block.shape

    def body(x_ref, out_ref, send_sems, recv_sems, local_sem):
        x, y, c = lax.axis_index("x"), lax.axis_index("y"), lax.axis_index("c")
        my_slot = 4 * x + 2 * y + c

        def peer(k):
            return (1 - x if k & 4 else x, 1 - y if k & 2 else y, 1 - c if k & 1 else c)

        def copy(k):
            px, py, pc = peer(k)
            return pltpu.make_async_remote_copy(
                src_ref=x_ref, dst_ref=out_ref.at[my_slot], send_sem=send_sems.at[k - 1], recv_sem=recv_sems.at[k - 1],
                device_id=(px, py, pc), device_id_type=MESH)

        def arrival(k):
            px, py, pc = peer(k)
            return pltpu.make_async_remote_copy(
                src_ref=x_ref, dst_ref=out_ref.at[4 * px + 2 * py + pc], send_sem=send_sems.at[k - 1],
                recv_sem=recv_sems.at[k - 1], device_id=(px, py, pc), device_id_type=MESH)

        mine = pltpu.make_async_copy(x_ref, out_ref.at[my_slot], local_sem)
        mine.start()
        for k in range(1, N_DEV):
            copy(k).start()
        for k in range(1, N_DEV):
            arrival(k).wait_recv()
        for k in range(1, N_DEV):
            copy(k).wait_send()
        mine.wait()

    return _pc(
        body, name,
        in_specs=[_HBM], out_specs=_HBM,
        out_shape=SDS((N_DEV, rows, width), block.dtype),
        scratch_shapes=[pltpu.SemaphoreType.DMA((7,)), pltpu.SemaphoreType.DMA((7,)), pltpu.SemaphoreType.DMA],
    )(block)


N_CHIP = N_DEV // 2


def _exchange_sibling(gs):
    n = len(gs)

    def body(*refs):
        g_refs, out_refs = refs[:n], refs[n:2 * n]
        send_sems, recv_sems = refs[2 * n:]
        x, y, c = lax.axis_index("x"), lax.axis_index("y"), lax.axis_index("c")
        cps = [pltpu.make_async_remote_copy(
            src_ref=g_refs[i].at[2 * k + 1 - c], dst_ref=out_refs[i].at[k], send_sem=send_sems.at[N_CHIP * i + k],
            recv_sem=recv_sems.at[N_CHIP * i + k], device_id=(x, y, 1 - c), device_id_type=MESH)
            for i in range(n) for k in range(N_CHIP)]
        for cp in cps:
            cp.start()
        for cp in cps:
            cp.wait()

    return _pc(
        body, "rs_sibling",
        in_specs=[_HBM] * n, out_specs=[_HBM] * n,
        out_shape=[SDS((N_CHIP,) + g.shape[1:], g.dtype) for g in gs],
        scratch_shapes=[pltpu.SemaphoreType.DMA((N_CHIP * n,)), pltpu.SemaphoreType.DMA((N_CHIP * n,))],
    )(*gs)


def _pair_sum(g, r1, cidx, tr, name):
    _, rows, width = g.shape

    def body(c_ref, g_ref, r_ref, o_ref):
        o_ref[...] = (g_ref[...].astype(f32) + r_ref[...].astype(f32)).astype(o_ref.dtype)

    return pl.pallas_call(
        body, name=name,
        grid_spec=pltpu.PrefetchScalarGridSpec(
            num_scalar_prefetch=1, grid=(N_CHIP, rows // tr),
            in_specs=[pl.BlockSpec((1, tr, width), lambda k, i, c_ref: (2 * k + c_ref[0], i, 0)),
                      pl.BlockSpec((1, tr, width), lambda k, i, c_ref: (k, i, 0))],
            out_specs=pl.BlockSpec((1, tr, width), lambda k, i, c_ref: (k, i, 0))),
        out_shape=SDS((N_CHIP, rows, width), g.dtype),
        compiler_params=_params(("arbitrary", "arbitrary")),
    )(cidx, g, r1)


def _exchange_chips(ps):
    n = len(ps)

    def body(*refs):
        p_refs, out_refs = refs[:n], refs[n:2 * n]
        send_sems, recv_sems, local_sems = refs[2 * n:]
        x, y, c = lax.axis_index("x"), lax.axis_index("y"), lax.axis_index("c")
        my_chip = 2 * x + y
        chips = [(1 - x, y), (x, 1 - y), (1 - x, 1 - y)]

        def copy(i, j):
            px, py = chips[j]
            return pltpu.make_async_remote_copy(
                src_ref=p_refs[i].at[2 * px + py], dst_ref=out_refs[i].at[my_chip], send_sem=send_sems.at[3 * i + j],
                recv_sem=recv_sems.at[3 * i + j], device_id=(px, py, c), device_id_type=MESH)

        def arrival(i, j):
            px, py = chips[j]
            return pltpu.make_async_remote_copy(
                src_ref=p_refs[i].at[my_chip], dst_ref=out_refs[i].at[2 * px + py], send_sem=send_sems.at[3 * i + j],
                recv_sem=recv_sems.at[3 * i + j], device_id=(px, py, c), device_id_type=MESH)

        mine = [pltpu.make_async_copy(p_refs[i].at[my_chip], out_refs[i].at[my_chip], local_sems.at[i])
                for i in range(n)]
        for cp in mine:
            cp.start()
        for i in range(n):
            for j in range(3):
                copy(i, j).start()
        for i in range(n):
            for j in range(3):
                arrival(i, j).wait_recv()
        for i in range(n):
            for j in range(3):
                copy(i, j).wait_send()
        for cp in mine:
            cp.wait()

    return _pc(
        body, "rs_chips",
        in_specs=[_HBM] * n, out_specs=[_HBM] * n,
        out_shape=[SDS(p.shape, p.dtype) for p in ps],
        scratch_shapes=[pltpu.SemaphoreType.DMA((3 * n,)), pltpu.SemaphoreType.DMA((3 * n,)),
                        pltpu.SemaphoreType.DMA((n,))],
    )(*ps)


def _adamw(parts, w, m, v, tr, name):
    n, rows, width = parts.shape
    c1 = 1.0 / (1.0 - ADAM_B1 ** ADAM_STEP)
    c2 = 1.0 / (1.0 - ADAM_B2 ** ADAM_STEP)

    def body(p_ref, w_ref, m_ref, v_ref, g_ref, d_ref, mo_ref, vo_ref):
        g = p_ref[0].astype(f32)
        for k in range(1, n):
            g = g + p_ref[k].astype(f32)
        mn = ADAM_B1 * m_ref[...] + (1.0 - ADAM_B1) * g
        vn = ADAM_B2 * v_ref[...] + (1.0 - ADAM_B2) * (g * g)
        g_ref[...] = g
        mo_ref[...] = mn
        vo_ref[...] = vn
        d_ref[...] = -ADAM_LR * ((mn * c1) / (jnp.sqrt(vn * c2) + ADAM_EPS) + ADAM_WD * w_ref[...])

    blk = pl.BlockSpec((tr, width), lambda i: (i, 0))
    return _pc(
        body, name, grid=(rows // tr,),
        in_specs=[pl.BlockSpec((n, tr, width), lambda i: (0, i, 0)), blk, blk, blk],
        out_specs=[blk, blk, blk, blk],
        out_shape=[SDS((rows, width), f32)] * 4,
        compiler_params=_params(("arbitrary",)),
    )(parts, w, m, v)


ROWS_REST = ROWS_SSM_OUT + ROWS_ATT_OUT + 16


def _pack_rest(ssm_w_out, att_w_out, conv_w):
    conv = jnp.pad(conv_w.reshape(4, 1024), ((0, 12), (0, 0)))
    return jnp.concatenate([ssm_w_out.reshape(ROWS_SSM_OUT, 1024), att_w_out.reshape(ROWS_ATT_OUT, 1024), conv], axis=0)


def _unpack_rest(p):
    o = ROWS_SSM_OUT + ROWS_ATT_OUT
    return (p[0:ROWS_SSM_OUT].reshape(2, 256, 1024), p[ROWS_SSM_OUT:o].reshape(2, 128, 1024),
            p[o:o + 4].reshape(2, 4, 512))


def _pack_grads(d_ssm_w_in, d_ssm_w_out, d_att_w_in, d_att_w_out, d_conv_w):
    wire = lambda t: t.astype(MXU_DTYPE)
    a = jnp.transpose(wire(d_ssm_w_in).reshape(2, 1024, 8, 772), (2, 0, 1, 3)).reshape(8, 2048, 772)
    c = jnp.transpose(wire(d_att_w_in).reshape(2, 1024, 8, 320), (2, 0, 1, 3)).reshape(8, 2048, 320)
    b = jnp.transpose(wire(d_ssm_w_out).reshape(2, 8, 256, 1024), (1, 0, 2, 3)).reshape(8, ROWS_SSM_OUT, 1024)
    d = jnp.transpose(wire(d_att_w_out).reshape(2, 8, 128, 1024), (1, 0, 2, 3)).reshape(8, ROWS_ATT_OUT, 1024)
    e = jnp.transpose(wire(d_conv_w).reshape(2, 4, 8, 512), (2, 0, 1, 3)).reshape(8, 4, 1024)
    e = jnp.pad(e, ((0, 0), (0, 12), (0, 0)))
    return a, c, jnp.concatenate([b, d, e], axis=1)


def _pad8(a):
    return jnp.pad(a, ((0, 8 - a.shape[0]), (0, 0)))


def _pack_small(pre_norm, post_norm, conv_b, gate_norm, dt_bias, a_log, d_skip, sinks, extra=None):
    row = jnp.concatenate([dt_bias.reshape(1, 64), a_log.reshape(1, 64), d_skip.reshape(1, 64), sinks.reshape(1, 32),
                           jnp.zeros((1, 1024 - 224), f32)], axis=1)
    if extra is not None:
        row = row + jnp.pad(extra.reshape(1, 1), ((0, 0), (224, 1024 - 225)))
    return jnp.concatenate([_pad8(pre_norm.reshape(4, 1024)), _pad8(post_norm.reshape(4, 1024)),
                            conv_b.reshape(8, 1024), _pad8(gate_norm.reshape(4, 1024)), _pad8(row)], axis=0)


def _unpack_small(p):
    row = p[32]
    return (p[0:4], p[8:12], p[16:24].reshape(2, 4096), row[0:64].reshape(2, 32), row[64:128].reshape(2, 32),
            row[128:192].reshape(2, 32), p[24:28].reshape(2, 2048), row[192:224].reshape(2, 16))


def _ssm_w_in_tiles(w):
    wb = w[:, 4096:5120].reshape(1024, 8, 128)
    wc = w[:, 5120:6144].reshape(1024, 8, 128)
    wbc = jnp.concatenate([wb, wc], axis=2).reshape(1024, 2048)
    return jnp.concatenate([w[:, 0:4096], wbc, w[:, 6144:6176], jnp.zeros((1024, 224), w.dtype)], axis=1)


def _ssm_w_in_untile(dw):
    dbc = dw[:, 4096:6144].reshape(1024, 8, 256)
    return jnp.concatenate([dw[:, 0:4096], dbc[:, :, 0:128].reshape(1024, 1024), dbc[:, :, 128:256].reshape(1024, 1024),
                            dw[:, 6144:6176]], axis=1)


def _conv_tiles(cw):
    k = cw.shape[0]
    xs = jnp.transpose(cw[:, 0:2048].reshape(k, 8, 256), (1, 0, 2))
    b = cw[:, 2048:3072].reshape(k, 8, 128)
    c = cw[:, 3072:4096].reshape(k, 8, 128)
    bc = jnp.transpose(jnp.concatenate([b, c], axis=2), (1, 0, 2))
    return jnp.concatenate([xs, bc], axis=0)


def _conv_untile(t):
    k = t.shape[1]
    xs = jnp.transpose(t[0:8], (1, 0, 2)).reshape(k, 2048)
    bc = jnp.transpose(t[8:16], (1, 0, 2))
    return jnp.concatenate([xs, bc[:, :, 0:128].reshape(k, 1024), bc[:, :, 128:256].reshape(k, 1024)], axis=1)


def _rope_tables(positions):
    inv = ROPE_THETA ** (-jnp.arange(0, 16, 2, dtype=f32) / 16)
    ang = positions.astype(f32).reshape(-1, 1) * inv
    cos, sin = jnp.cos(ang), jnp.sin(ang)
    t_len = ang.shape[0]
    cos64 = jnp.concatenate([cos, cos, jnp.ones((t_len, 48), f32)], axis=1)
    sin64 = jnp.concatenate([-sin, sin, jnp.zeros((t_len, 48), f32)], axis=1)
    return jnp.tile(cos64, (1, 4)), jnp.tile(sin64, (1, 4))


def kernel(x, positions, pre_norm, post_norm, ssm_w_in, ssm_conv_w, ssm_conv_b, ssm_dt_bias, ssm_a_log, ssm_d, ssm_gate_norm, ssm_w_out, att_w_in, att_sinks, att_w_out, loss_target, m_pre_norm, m_post_norm, m_ssm_w_in, m_ssm_conv_w, m_ssm_conv_b, m_ssm_dt_bias, m_ssm_a_log, m_ssm_d, m_ssm_gate_norm, m_ssm_w_out, m_att_w_in, m_att_sinks, m_att_w_out, v_pre_norm, v_post_norm, v_ssm_w_in, v_ssm_conv_w, v_ssm_conv_b, v_ssm_dt_bias, v_ssm_a_log, v_ssm_d, v_ssm_gate_norm, v_ssm_w_out, v_att_w_in, v_att_sinks, v_att_w_out):
    t_len = x.shape[1]
    tm = min(1024, t_len)
    xin = x.reshape(t_len, D_MODEL)
    tgt = loss_target.reshape(t_len, D_MODEL)
    cidx = lax.axis_index("c").astype(jnp.int32).reshape(1)

    g_ssm_in, g_att_in, g_ssm_out, g_att_out = _all_gather_big(
        [ssm_w_in.astype(MXU_DTYPE), att_w_in.astype(MXU_DTYPE), ssm_w_out.astype(MXU_DTYPE),
         att_w_out.astype(MXU_DTYPE)])
    conv_local = jnp.concatenate([ssm_conv_w.reshape(4, 1024), jnp.zeros((4, 1024), f32)], axis=0)
    conv_all = _all_gather_direct(conv_local, "all_gather_conv")[:, 0:4]
    w_ssm_in = jnp.transpose(g_ssm_in, (1, 2, 0, 3)).reshape(2, 1024, SSM_IN)
    w_ssm_out = jnp.transpose(g_ssm_out, (1, 0, 2, 3)).reshape(2, SSM_INNER, 1024)
    w_att_in = jnp.transpose(g_att_in, (1, 2, 0, 3)).reshape(2, 1024, ATT_IN)
    w_att_out = jnp.transpose(g_att_out, (1, 0, 2, 3)).reshape(2, 1024, 1024)
    conv_w = jnp.transpose(conv_all.reshape(8, 2, 4, 512), (1, 2, 0, 3)).reshape(2, 4, 4096)
    cos_t, sin_t = _rope_tables(positions)

    saved = []
    xc = xin
    for i in range(4):
        j = i // 2
        wn_pre, wn_post = pre_norm[i].reshape(1, D_MODEL), post_norm[i].reshape(1, D_MODEL)
        if i % 2 == 0:
            w_in = _ssm_w_in_tiles(w_ssm_in[j])
            cw, cb = _conv_tiles(conv_w[j]), _conv_tiles(ssm_conv_b[j].reshape(1, 4096))
            dtb, alog, dsk = ssm_dt_bias[j].reshape(1, 32), ssm_a_log[j].reshape(1, 32), ssm_d[j].reshape(1, 32)
            gn = ssm_gate_norm[j].reshape(SSM_KT, 1, TILE)
            w_zdt = jnp.concatenate([w_in[:, 0:SSM_INNER], w_in[:, 24 * TILE:25 * TILE]], axis=1)
            u, h = _mm_in(xc, wn_pre, w_zdt, 3, tm, f"ssm_in_{j}")
            uc, xbc = _mm_in_conv(xc, wn_pre, w_in[:, SSM_INNER:24 * TILE], cw, cb, tm, f"ssm_inconv_{j}")
            pre = _ssm_pre(u, dtb, alog, dsk, f"ssm_pre_{j}")
            a3, yp, hst = _ssm2_fwd(u, xbc, pre, dtb, alog, dsk, gn, f"ssm_core_{j}")
            y, xn = _mm_out(a3, w_ssm_out[j], xc, wn_post, 4, tm, f"ssm_out_{j}")
            saved.append(dict(x=xc, u=u, uc=uc, h=h, a=a3, yp=yp, hst=hst, y=y, w_in=w_in, cw=cw, cb=cb, dtb=dtb,
                              alog=alog, dsk=dsk, gn=gn, xbc=xbc, pre=pre))
        else:
            sinks = att_sinks[j].reshape(1, 16)
            u, h = _mm_in(xc, wn_pre, w_att_in[j], 5, tm, f"att_in_{j}")
            a = _att2_fwd(u, cos_t, sin_t, sinks, f"att_core_{j}")
            y, xn = _mm_out(a, w_att_out[j], xc, wn_post, 4, tm, f"att_out_{j}")
            saved.append(dict(x=xc, u=u, h=h, a=a, y=y, sinks=sinks))
        xc = xn

    dx, loss_part = _loss_grad(xc, tgt, tm)

    d_pre, d_post = [None] * 4, [None] * 4
    d_ssm_in, d_ssm_out, d_att_in, d_att_out = [None] * 2, [None] * 2, [None] * 2, [None] * 2
    d_cw, d_cb, d_dtb, d_alog, d_dsk, d_gn, d_sinks = ([None] * 2 for _ in range(7))
    for i in reversed(range(4)):
        j = i // 2
        s = saved[i]
        wn_pre, wn_post = pre_norm[i].reshape(1, D_MODEL), post_norm[i].reshape(1, D_MODEL)
        if i % 2 == 0:
            da3, dy, d_post[i] = _mm_dout(s["y"], dx, wn_post, w_ssm_out[j], 4, tm, f"ssm_dout_{j}")
            d_ssm_out[j] = _dw_rows(s["a"], dy, 4, tm, f"ssm_dwout_{j}")
            du, d_dtb[j], d_alog[j], d_dsk[j], dgn = _ssm2_bwd(
                s["u"], s["xbc"], s["pre"], s["yp"], s["hst"], da3, s["dtb"], s["alog"], s["dsk"], s["gn"],
                f"ssm_core_bwd_{j}")
            du, dcw, dcb = _conv_bwd(s["uc"], du, s["cw"], s["cb"], f"ssm_conv_bwd_{j}")
            d_cw[j], d_cb[j], d_gn[j] = _conv_untile(dcw), _conv_untile(dcb), dgn.reshape(1, SSM_INNER)
            d_ssm_in[j] = _ssm_w_in_untile(_dw_cols(s["h"], du, 5, tm, f"ssm_dwin_{j}"))
            dx, d_pre[i] = _mm_dh(du, s["w_in"], s["x"], dx, wn_pre, 5, tm, f"ssm_dh_{j}")
        else:
            da, dy, d_post[i] = _mm_dout(s["y"], dx, wn_post, w_att_out[j], 4, tm, f"att_dout_{j}")
            d_att_out[j] = _dw_rows(s["a"], dy, 4, tm, f"att_dwout_{j}")
            du, d_sinks[j] = _att2_bwd(s["u"], cos_t, sin_t, s["sinks"], da, f"att_core_bwd_{j}")
            d_att_in[j] = _dw_cols(s["h"], du, 5, tm, f"att_dwin_{j}")
            dx, d_pre[i] = _mm_dh(du, w_att_in[j], s["x"], dx, wn_pre, 5, tm, f"att_dh_{j}")

    gs = _pack_grads(jnp.stack(d_ssm_in), jnp.stack(d_ssm_out), jnp.stack(d_att_in), jnp.stack(d_att_out),
                     jnp.stack(d_cw))
    r1 = _exchange_sibling(gs)
    tiles = (256, 256, ROWS_REST // 7)
    pairs = [_pair_sum(g, r, cidx, tr, f"rs_pair_sum_{k}") for k, (g, r, tr) in enumerate(zip(gs, r1, tiles))]
    parts = _exchange_chips(pairs)
    flat = lambda t: t.reshape(2048, t.shape[-1])
    a4 = _adamw(parts[0], flat(ssm_w_in), flat(m_ssm_w_in), flat(v_ssm_w_in), tiles[0], "adamw_ssm_in")
    b4 = _adamw(parts[1], flat(att_w_in), flat(m_att_w_in), flat(v_att_w_in), tiles[1], "adamw_att_in")
    c4 = _adamw(parts[2], _pack_rest(ssm_w_out, att_w_out, ssm_conv_w), _pack_rest(m_ssm_w_out, m_att_w_out, m_ssm_conv_w),
                _pack_rest(v_ssm_w_out, v_att_w_out, v_ssm_conv_w), tiles[2], "adamw_rest")
    big = []
    for k in range(4):
        o_ssm_out, o_att_out, o_conv = _unpack_rest(c4[k])
        big.append((a4[k].reshape(2, 1024, 772), o_ssm_out, b4[k].reshape(2, 1024, 320), o_att_out, o_conv))

    small_local = _pack_small(jnp.concatenate(d_pre, axis=0), jnp.concatenate(d_post, axis=0),
                              jnp.concatenate(d_cb, axis=0), jnp.concatenate(d_gn, axis=0),
                              jnp.concatenate(d_dtb, axis=0), jnp.concatenate(d_alog, axis=0),
                              jnp.concatenate(d_dsk, axis=0), jnp.concatenate(d_sinks, axis=0), loss_part[0, 0])
    small_all = _all_gather_direct(small_local, "all_gather_small")
    ws = _pack_small(pre_norm, post_norm, ssm_conv_b, ssm_gate_norm, ssm_dt_bias, ssm_a_log, ssm_d, att_sinks)
    ms = _pack_small(m_pre_norm, m_post_norm, m_ssm_conv_b, m_ssm_gate_norm, m_ssm_dt_bias, m_ssm_a_log, m_ssm_d,
                     m_att_sinks)
    vs = _pack_small(v_pre_norm, v_post_norm, v_ssm_conv_b, v_ssm_gate_norm, v_ssm_dt_bias, v_ssm_a_log, v_ssm_d,
                     v_att_sinks)
    small4 = _adamw(small_all, ws, ms, vs, ROWS_SMALL, "adamw_small")
    loss = small4[0][32, 224]
    small = [_unpack_small(t) for t in small4]

    outs = [loss, dx.reshape(1, t_len, D_MODEL)]
    for k in range(4):
        b_ssm_in, b_ssm_out, b_att_in, b_att_out, b_conv = big[k]
        s_pre, s_post, s_cb, s_dtb, s_alog, s_d, s_gn, s_sinks = small[k]
        outs += [s_pre, s_post, b_ssm_in, b_conv, s_cb, s_dtb, s_alog, s_d, s_gn, b_ssm_out, b_att_in, s_sinks,
                 b_att_out]
    return tuple(outs)
```

```python
import jax
import jax.numpy as jnp
from jax import lax
from jax.experimental import pallas as pl
from jax.experimental.pallas import tpu as pltpu

f32 = jnp.float32
MXU_DTYPE = jnp.bfloat16
SDS = jax.ShapeDtypeStruct
MESH = pl.DeviceIdType.MESH

D_MODEL = 1024
EPS = 1e-6
TILE = 256
CHUNK = 128
ATT_ROWS = 4 * CHUNK
SSM_HEADS = 32
SSM_GROUPS = 8
SSM_P = 64
SSM_N = 128
SSM_INNER = 2048
SSM_IN = 6176
SSM_NT = 25
SSM_KT = 8
ATT_NT = 10
ATT_KT = 4
ATT_IN = 2560
ROPE_THETA = 500000.0
N_DEV = 8
VMEM_LIMIT = 56 * 1024 * 1024

ADAM_LR = 0.001
ADAM_B1 = 0.9
ADAM_B2 = 0.999
ADAM_EPS = 1e-08
ADAM_WD = 0.01
ADAM_STEP = 10

ROWS_SSM_OUT = 2 * 256
ROWS_ATT_OUT = 2 * 128
ROWS_SMALL = 40


def _pc(body, name, **kw):
    return pl.pallas_call(body, name=name, **kw)


def _params(sem):
    return pltpu.CompilerParams(dimension_semantics=sem, vmem_limit_bytes=VMEM_LIMIT)


def _sigmoid(x):
    return 0.5 * jnp.tanh(0.5 * x) + 0.5


def _silu(x):
    return x * _sigmoid(x)


def _softplus(x):
    return jnp.maximum(x, 0.0) + jnp.log(1.0 + jnp.exp(-jnp.abs(x)))


def _mx(x):
    return x.astype(MXU_DTYPE)


def _dot(a, b):
    return jnp.dot(_mx(a), _mx(b), preferred_element_type=f32)


def _dot_nt(a, b):
    return lax.dot_general(_mx(a), _mx(b), (((1,), (1,)), ((), ())), preferred_element_type=f32)


def _dot_tn(a, b):
    return lax.dot_general(_mx(a), _mx(b), (((0,), (0,)), ((), ())), preferred_element_type=f32)


def _rms_fwd(x, w):
    r = lax.rsqrt(jnp.mean(x * x, axis=-1, keepdims=True) + EPS)
    return x * r * w


def _rms_bwd(x, w, dy):
    r = lax.rsqrt(jnp.mean(x * x, axis=-1, keepdims=True) + EPS)
    xh = x * r
    dw = jnp.sum(dy * xh, axis=0, keepdims=True)
    g = dy * w
    dx = r * (g - xh * jnp.mean(g * xh, axis=-1, keepdims=True))
    return dx, dw


def _mm_in(x, wn, w, ntb, tm, name):
    t_len, d = x.shape
    nt = w.shape[1] // TILE

    def body(x_ref, wn_ref, w_ref, u_ref, h_ref):
        @pl.when(pl.program_id(1) == 0)
        def _():
            h_ref[...] = _rms_fwd(x_ref[...], wn_ref[...]).astype(h_ref.dtype)
        h = h_ref[...]
        for t in range(ntb):
            u_ref[t] = jnp.dot(h, w_ref[:, TILE * t:TILE * (t + 1)], preferred_element_type=f32)

    return _pc(
        body, name, grid=(t_len // tm, nt // ntb),
        in_specs=[pl.BlockSpec((tm, d), lambda i, j: (i, 0)),
                  pl.BlockSpec((1, d), lambda i, j: (0, 0)),
                  pl.BlockSpec((d, ntb * TILE), lambda i, j: (0, j))],
        out_specs=[pl.BlockSpec((ntb, tm, TILE), lambda i, j: (j, i, 0)),
                   pl.BlockSpec((tm, d), lambda i, j: (i, 0))],
        out_shape=[SDS((nt, t_len, TILE), f32), SDS((t_len, d), MXU_DTYPE)],
        compiler_params=_params(("arbitrary", "arbitrary")),
    )(x, wn, w)


def _mm_dout(y, dxn, wn, w, ntb, tm, name):
    t_len, d = y.shape
    nt = w.shape[0] // TILE

    def body(y_ref, dxn_ref, wn_ref, w_ref, da_ref, dy_ref, dwn_ref):
        i, j = pl.program_id(0), pl.program_id(1)

        @pl.when((i == 0) & (j == 0))
        def _():
            dwn_ref[...] = jnp.zeros_like(dwn_ref)

        @pl.when(j == 0)
        def _():
            dy, dw = _rms_bwd(y_ref[...], wn_ref[...], dxn_ref[...])
            dy_ref[...] = dy.astype(dy_ref.dtype)
            dwn_ref[...] += dw
        dy = dy_ref[...]
        for t in range(ntb):
            da_ref[t] = _dot_nt(dy, w_ref[TILE * t:TILE * (t + 1), :])

    return _pc(
        body, name, grid=(t_len // tm, nt // ntb),
        in_specs=[pl.BlockSpec((tm, d), lambda i, j: (i, 0)),
                  pl.BlockSpec((tm, d), lambda i, j: (i, 0)),
                  pl.BlockSpec((1, d), lambda i, j: (0, 0)),
                  pl.BlockSpec((ntb * TILE, d), lambda i, j: (j, 0))],
        out_specs=[pl.BlockSpec((ntb, tm, TILE), lambda i, j: (j, i, 0)),
                   pl.BlockSpec((tm, d), lambda i, j: (i, 0)),
                   pl.BlockSpec((1, d), lambda i, j: (0, 0))],
        out_shape=[SDS((nt, t_len, TILE), f32), SDS((t_len, d), MXU_DTYPE), SDS((1, d), f32)],
        compiler_params=_params(("arbitrary", "arbitrary")),
    )(y, dxn, wn, w)


def _mm_out(a, w, x, wn, ktb, tm, name):
    kt, t_len, _ = a.shape
    d = w.shape[1]
    nk = kt // ktb

    def body(a_ref, w_ref, x_ref, wn_ref, y_ref, xn_ref, acc):
        k = pl.program_id(1)

        @pl.when(k == 0)
        def _():
            acc[...] = jnp.zeros_like(acc)
        s = acc[...]
        for t in range(ktb):
            s = s + jnp.dot(a_ref[t], w_ref[TILE * t:TILE * (t + 1), :], preferred_element_type=f32)
        acc[...] = s

        @pl.when(k == nk - 1)
        def _():
            y = acc[...]
            y_ref[...] = y
            xn_ref[...] = x_ref[...] + _rms_fwd(y, wn_ref[...])

    return _pc(
        body, name, grid=(t_len // tm, nk),
        in_specs=[pl.BlockSpec((ktb, tm, TILE), lambda i, k: (k, i, 0)),
                  pl.BlockSpec((ktb * TILE, d), lambda i, k: (k, 0)),
                  pl.BlockSpec((tm, d), lambda i, k: (i, 0)),
                  pl.BlockSpec((1, d), lambda i, k: (0, 0))],
        out_specs=[pl.BlockSpec((tm, d), lambda i, k: (i, 0)),
                   pl.BlockSpec((tm, d), lambda i, k: (i, 0))],
        out_shape=[SDS((t_len, d), f32), SDS((t_len, d), f32)],
        scratch_shapes=[pltpu.VMEM((tm, d), f32)],
        compiler_params=_params(("arbitrary", "arbitrary")),
    )(a, w, x, wn)


def _mm_dh(du, w, x, dxn, wn, ktb, tm, name):
    kt, t_len, _ = du.shape
    d = w.shape[0]
    nk = kt // ktb

    def body(du_ref, w_ref, x_ref, dxn_ref, wn_ref, dx_ref, dwn_ref, acc):
        i, k = pl.program_id(0), pl.program_id(1)

        @pl.when((i == 0) & (k == 0))
        def _():
            dwn_ref[...] = jnp.zeros_like(dwn_ref)

        @pl.when(k == 0)
        def _():
            acc[...] = jnp.zeros_like(acc)
        s = acc[...]
        for t in range(ktb):
            s = s + _dot_nt(du_ref[t], w_ref[:, TILE * t:TILE * (t + 1)])
        acc[...] = s

        @pl.when(k == nk - 1)
        def _():
            dxp, dw = _rms_bwd(x_ref[...], wn_ref[...], acc[...])
            dx_ref[...] = dxn_ref[...] + dxp
            dwn_ref[...] += dw

    return _pc(
        body, name, grid=(t_len // tm, nk),
        in_specs=[pl.BlockSpec((ktb, tm, TILE), lambda i, k: (k, i, 0)),
                  pl.BlockSpec((d, ktb * TILE), lambda i, k: (0, k)),
                  pl.BlockSpec((tm, d), lambda i, k: (i, 0)),
                  pl.BlockSpec((tm, d), lambda i, k: (i, 0)),
                  pl.BlockSpec((1, d), lambda i, k: (0, 0))],
        out_specs=[pl.BlockSpec((tm, d), lambda i, k: (i, 0)),
                   pl.BlockSpec((1, d), lambda i, k: (0, 0))],
        out_shape=[SDS((t_len, d), f32), SDS((1, d), f32)],
        scratch_shapes=[pltpu.VMEM((tm, d), f32)],
        compiler_params=_params(("arbitrary", "arbitrary")),
    )(du, w, x, dxn, wn)


def _dw_cols(a, b, ntb, tk, name):
    t_len, kdim = a.shape
    nt = b.shape[0]

    def body(a_ref, b_ref, o_ref):
        @pl.when(pl.program_id(1) == 0)
        def _():
            o_ref[...] = jnp.zeros_like(o_ref)
        av = a_ref[...]
        for s in range(ntb):
            o_ref[:, TILE * s:TILE * (s + 1)] += _dot_tn(av, b_ref[s])

    return _pc(
        body, name, grid=(nt // ntb, t_len // tk),
        in_specs=[pl.BlockSpec((tk, kdim), lambda j, t: (t, 0)),
                  pl.BlockSpec((ntb, tk, TILE), lambda j, t: (j, t, 0))],
        out_specs=pl.BlockSpec((kdim, ntb * TILE), lambda j, t: (0, j)),
        out_shape=SDS((kdim, nt * TILE), f32),
        compiler_params=_params(("arbitrary", "arbitrary")),
    )(a, b)


def _dw_rows(a, b, ktb, tk, name):
    kt, t_len, _ = a.shape
    d = b.shape[1]

    def body(a_ref, b_ref, o_ref):
        @pl.when(pl.program_id(1) == 0)
        def _():
            o_ref[...] = jnp.zeros_like(o_ref)
        bv = b_ref[...]
        for s in range(ktb):
            o_ref[TILE * s:TILE * (s + 1), :] += _dot_tn(a_ref[s], bv)

    return _pc(
        body, name, grid=(kt // ktb, t_len // tk),
        in_specs=[pl.BlockSpec((ktb, tk, TILE), lambda k, t: (k, t, 0)),
                  pl.BlockSpec((tk, d), lambda k, t: (t, 0))],
        out_specs=pl.BlockSpec((ktb * TILE, d), lambda k, t: (k, 0)),
        out_shape=SDS((kt * TILE, d), f32),
        compiler_params=_params(("arbitrary", "arbitrary")),
    )(a, b)


def _loss_grad(x, tgt, tm):
    t_len, d = x.shape

    def body(x_ref, t_ref, dx_ref, l_ref):
        @pl.when(pl.program_id(0) == 0)
        def _():
            l_ref[...] = jnp.zeros_like(l_ref)
        e = x_ref[...] - t_ref[...]
        dx_ref[...] = e * (1.0 / d)
        row = jnp.mean(e * e, axis=-1, keepdims=True)
        l_ref[...] += 0.5 * jnp.sum(row, axis=0, keepdims=True)

    return _pc(
        body, "loss_grad", grid=(t_len // tm,),
        in_specs=[pl.BlockSpec((tm, d), lambda i: (i, 0)), pl.BlockSpec((tm, d), lambda i: (i, 0))],
        out_specs=[pl.BlockSpec((tm, d), lambda i: (i, 0)), pl.BlockSpec((1, 128), lambda i: (0, 0))],
        out_shape=[SDS((t_len, d), f32), SDS((1, 128), f32)],
        compiler_params=_params(("arbitrary",)),
    )(x, tgt)


def _tri(lower):
    r = lax.broadcasted_iota(jnp.int32, (CHUNK, CHUNK), 0)
    c = lax.broadcasted_iota(jnp.int32, (CHUNK, CHUNK), 1)
    return ((c <= r) if lower else (c >= r)).astype(f32)


def _split(x, n):
    parts = []
    for _ in range(n):
        p = x.astype(jnp.bfloat16)
        parts.append(p)
        x = x - p.astype(f32)
    return parts


def _dot_exact(a, b, dims, split_a, n=3):
    out = None
    if split_a:
        b = b.astype(jnp.bfloat16)
        for p in _split(a, n):
            t = lax.dot_general(p, b, (dims, ((), ())), preferred_element_type=f32)
            out = t if out is None else out + t
    else:
        a = a.astype(jnp.bfloat16)
        for p in _split(b, n):
            t = lax.dot_general(a, p, (dims, ((), ())), preferred_element_type=f32)
            out = t if out is None else out + t
    return out


def _dt_path(dt_raw, dtb, alog):
    dtr = dt_raw + dtb
    dt = _softplus(dtr)
    a_neg = -jnp.exp(alog)
    a = dt * a_neg
    acs = _dot_exact(_tri(True), a, ((1,), (0,)), False)
    acs_t = _dot_exact(a, _tri(False), ((0,), (0,)), True)
    return dtr, dt, a_neg, acs, acs_t


CONV_ROWS = 1024
CONV_SUB = 32


def _conv_specs(nb, rows, rev):
    def ridx(i):
        return (nb - 1 - i) if rev else i
    return [
        pl.BlockSpec((1, rows, TILE), lambda p, i: (p, ridx(i), 0)),
        pl.BlockSpec((1, 8, TILE), lambda p, i: (p, jnp.maximum(ridx(i) * (rows // 8) - 1, 0), 0)),
        pl.BlockSpec((1, 4, TILE), lambda p, i: (p, 0, 0)),
        pl.BlockSpec((1, 1, TILE), lambda p, i: (p, 0, 0)),
    ]


CONV_NTB = 4


def _mm_in_conv(x, wn, w, cw, cb, tm, name):
    t_len, d = x.shape
    nt = w.shape[1] // TILE

    def body(x_ref, wn_ref, w_ref, cw_ref, cb_ref, u_ref, o_ref, h_s, carry_s, win_s):
        i, j = pl.program_id(0), pl.program_id(1)

        @pl.when(j == 0)
        def _():
            h_s[...] = _rms_fwd(x_ref[...], wn_ref[...]).astype(h_s.dtype)
        h = h_s[...]
        for t in range(CONV_NTB):
            p = CONV_NTB * j + t
            ut = jnp.dot(h, w_ref[:, TILE * t:TILE * (t + 1)], preferred_element_type=f32)
            u_ref[t] = ut
            win_s[t, 0:8, :] = jnp.where(i > 0, carry_s[p], 0.0)
            win_s[t, 8:8 + tm, :] = ut
            carry_s[p] = ut[tm - 8:tm, :]
            wk = [cw_ref[t, k:k + 1, :] for k in range(4)]
            b = cb_ref[t]
            for s in range(tm // CONV_SUB):
                o = CONV_SUB * s
                acc = b
                for k in range(4):
                    acc = acc + wk[k] * win_s[t, 5 + k + o:5 + k + o + CONV_SUB, :]
                o_ref[t, o:o + CONV_SUB, :] = _silu(acc)

    return _pc(
        body, name, grid=(t_len // tm, nt // CONV_NTB),
        in_specs=[pl.BlockSpec((tm, d), lambda i, j: (i, 0)),
                  pl.BlockSpec((1, d), lambda i, j: (0, 0)),
                  pl.BlockSpec((d, CONV_NTB * TILE), lambda i, j: (0, j)),
                  pl.BlockSpec((CONV_NTB, 4, TILE), lambda i, j: (j, 0, 0)),
                  pl.BlockSpec((CONV_NTB, 1, TILE), lambda i, j: (j, 0, 0))],
        out_specs=[pl.BlockSpec((CONV_NTB, tm, TILE), lambda i, j: (j, i, 0)),
                   pl.BlockSpec((CONV_NTB, tm, TILE), lambda i, j: (j, i, 0))],
        out_shape=[SDS((nt, t_len, TILE), f32), SDS((nt, t_len, TILE), f32)],
        scratch_shapes=[pltpu.VMEM((tm, d), MXU_DTYPE), pltpu.VMEM((nt, 8, TILE), f32),
                        pltpu.VMEM((CONV_NTB, 8 + tm, TILE), f32)],
        compiler_params=_params(("arbitrary", "arbitrary")),
    )(x, wn, w, cw, cb)


def _conv_bwd(u, du, cw, cb, name):
    t_len = u.shape[1]
    rows = min(CONV_ROWS, t_len)
    nb = t_len // rows

    def body(u_ref, halo_ref, cw_ref, cb_ref, d_ref, o_ref, dcw_ref, dcb_ref, carry_s, win_s, dp_s):
        i = pl.program_id(1)
        ri = nb - 1 - i

        @pl.when(i == 0)
        def _():
            carry_s[...] = jnp.zeros_like(carry_s)
            dcw_ref[...] = jnp.zeros_like(dcw_ref)
            dcb_ref[...] = jnp.zeros_like(dcb_ref)
        win_s[0:8, :] = jnp.where(ri > 0, halo_ref[0], 0.0)
        win_s[8:8 + rows, :] = u_ref[0]
        w = [cw_ref[0, k:k + 1, :] for k in range(4)]
        b = cb_ref[0]
        def fold(v):
            return jnp.sum(v.reshape(CONV_SUB // 8, 8, TILE), axis=0)

        dw = [jnp.zeros((8, TILE), f32)] * 4
        db = jnp.zeros((8, TILE), f32)
        for s in range(rows // CONV_SUB):
            o = CONV_SUB * s
            xk = [win_s[5 + k + o:5 + k + o + CONV_SUB, :] for k in range(4)]
            pre = b
            for k in range(4):
                pre = pre + w[k] * xk[k]
            sg = _sigmoid(pre)
            dpre = d_ref[0, o:o + CONV_SUB, :] * (sg * (1.0 + pre * (1.0 - sg)))
            dp_s[o:o + CONV_SUB, :] = dpre
            dw = [dw[k] + fold(dpre * xk[k]) for k in range(4)]
            db = db + fold(dpre)
        dw = [jnp.sum(t, axis=0, keepdims=True) for t in dw]
        db = jnp.sum(db, axis=0, keepdims=True)
        dp_s[rows:rows + 8, :] = carry_s[...]
        for s in range(rows // CONV_SUB):
            o = CONV_SUB * s
            acc = w[0] * dp_s[3 + o:3 + o + CONV_SUB, :]
            for k in range(1, 4):
                acc = acc + w[k] * dp_s[3 - k + o:3 - k + o + CONV_SUB, :]
            o_ref[0, o:o + CONV_SUB, :] = acc
        carry_s[...] = dp_s[0:8, :]
        for k in range(4):
            dcw_ref[0, k:k + 1, :] += dw[k]
        dcb_ref[0] += db

    return _pc(
        body, name, grid=(16, nb),
        in_specs=_conv_specs(nb, rows, True) + [pl.BlockSpec((1, rows, TILE), lambda p, i: (8 + p, nb - 1 - i, 0))],
        out_specs=[pl.BlockSpec((1, rows, TILE), lambda p, i: (8 + p, nb - 1 - i, 0)),
                   pl.BlockSpec((1, 4, TILE), lambda p, i: (p, 0, 0)),
                   pl.BlockSpec((1, 1, TILE), lambda p, i: (p, 0, 0))],
        out_shape=[SDS(du.shape, f32), SDS((16, 4, TILE), f32), SDS((16, 1, TILE), f32)],
        input_output_aliases={4: 0},
        scratch_shapes=[pltpu.VMEM((8, TILE), f32), pltpu.VMEM((8 + rows, TILE), f32),
                        pltpu.VMEM((rows + 8, TILE), f32)],
        compiler_params=_params(("arbitrary", "arbitrary")),
    )(u, u, cw, cb, du)


def _collapse_matrix(g):
    r = lax.broadcasted_iota(jnp.int32, (SSM_HEADS, TILE), 0)
    c = lax.broadcasted_iota(jnp.int32, (SSM_HEADS, TILE), 1)
    return ((c // SSM_P) + 4 * g == r).astype(jnp.bfloat16)


def _ssd_prelude(dt_raw, dtb, alog, dsk, colx_s, scx_s):
    dtr, dt, a_neg, acs, acs_t = _dt_path(dt_raw, dtb, alog)
    a_end = acs[CHUNK - 1:CHUNK, :]
    lane = lax.broadcasted_iota(jnp.int32, (1, 2 * SSM_P), 1)
    lane4 = lax.broadcasted_iota(jnp.int32, (1, TILE), 1)
    sub8 = lax.broadcasted_iota(jnp.int32, (8, 1), 0)

    def row4(v, g):
        e = [v[:, 4 * g + r:4 * g + r + 1] for r in range(4)]
        return jnp.where(lane4 < 64, e[0], jnp.where(lane4 < 128, e[1], jnp.where(lane4 < 192, e[2], e[3])))

    for g in range(SSM_GROUPS):
        for k, arr in enumerate((dt, acs)):
            for half in range(2):
                h0 = 4 * g + 2 * half
                colx_s[k, g, :, 128 * half:128 * (half + 1)] = jnp.where(
                    lane < SSM_P, arr[:, h0:h0 + 1], arr[:, h0 + 1:h0 + 2])
        scx_s[g] = jnp.where(sub8 == 0, row4(dsk, g), jnp.where(sub8 == 1, row4(a_end, g), 0.0))
    return dtr, dt, a_neg, acs_t


def _ssm_core_specs(nc, rev):
    def cidx(c):
        return (nc - 1 - c) if rev else c
    return [
        pl.BlockSpec((SSM_KT, CHUNK, TILE), lambda c: (0, cidx(c), 0)),
        pl.BlockSpec((1, CHUNK, TILE), lambda c: (SSM_KT, cidx(c), 0)),
        pl.BlockSpec((16, CHUNK, TILE), lambda c: (0, cidx(c), 0)),
        pl.BlockSpec((1, SSM_HEADS), lambda c: (0, 0)),
        pl.BlockSpec((1, SSM_HEADS), lambda c: (0, 0)),
        pl.BlockSpec((1, SSM_HEADS), lambda c: (0, 0)),
        pl.BlockSpec((SSM_KT, 1, TILE), lambda c: (0, 0, 0)),
        pl.BlockSpec((2, SSM_GROUPS, CHUNK, TILE), lambda c: (0, 0, cidx(c), 0)),
        pl.BlockSpec((1, SSM_GROUPS, 8, TILE), lambda c: (cidx(c), 0, 0, 0)),
        pl.BlockSpec((1, SSM_HEADS, CHUNK), lambda c: (cidx(c), 0, 0)),
    ]


PRE_CHUNKS = 4


def _ssm_pre(u, dtb, alog, dsk, name):
    t_len = u.shape[1]
    nc = t_len // CHUNK
    per = min(PRE_CHUNKS, nc)

    def body(dt_ref, dtb_ref, alog_ref, dsk_ref, colx_ref, scx_ref, acst_ref):
        for c in range(per):
            rows = pl.ds(CHUNK * c, CHUNK)
            _, _, _, acs_t = _ssd_prelude(dt_ref[0, rows, 0:SSM_HEADS], dtb_ref[...], alog_ref[...], dsk_ref[...],
                                          colx_ref.at[:, :, rows, :], scx_ref.at[c])
            acst_ref[c] = acs_t

    vec = pl.BlockSpec((1, SSM_HEADS), lambda i: (0, 0))
    return _pc(
        body, name, grid=(nc // per,),
        in_specs=[pl.BlockSpec((1, per * CHUNK, TILE), lambda i: (SSM_KT, i, 0)), vec, vec, vec],
        out_specs=[pl.BlockSpec((2, SSM_GROUPS, per * CHUNK, TILE), lambda i: (0, 0, i, 0)),
                   pl.BlockSpec((per, SSM_GROUPS, 8, TILE), lambda i: (i, 0, 0, 0)),
                   pl.BlockSpec((per, SSM_HEADS, CHUNK), lambda i: (i, 0, 0))],
        out_shape=[SDS((2, SSM_GROUPS, t_len, TILE), f32), SDS((nc, SSM_GROUPS, 8, TILE), f32),
                   SDS((nc, SSM_HEADS, CHUNK), f32)],
        compiler_params=_params(("arbitrary",)),
    )(u, dtb, alog, dsk)


def _stack_cols_rows(acx, rows):
    ac = jnp.concatenate([acx[:, SSM_P * r:SSM_P * r + 1] for r in range(4)], axis=0)
    ar = jnp.concatenate([jnp.broadcast_to(rows[r:r + 1, :], (CHUNK, CHUNK)) for r in range(4)], axis=0)
    return ac, ar


def _ssm2_fwd(u, xbc, pre, dtb, alog, dsk, gn, name):
    t_len = u.shape[1]
    nc = t_len // CHUNK

    def body(z_ref, dt_ref, x_ref, dtb_ref, alog_ref, dsk_ref, gn_ref, colx_ref, scx_ref, acst_ref,
             a3_ref, yp_ref, hst_ref, h_s, xt_s, yd_s):
        c = pl.program_id(0)

        @pl.when(c == 0)
        def _():
            h_s[...] = jnp.zeros_like(h_s)
        acs_t = acst_ref[0]
        causal = _tri4()

        def group(g, s1):
            xs = x_ref[g]
            bm, cm = x_ref[8 + g, :, 0:SSM_N], x_ref[8 + g, :, SSM_N:2 * SSM_N]
            cb = _dot_nt(cm, bm)
            sc = scx_ref[0, g]
            a_end = sc[1:2, :]
            rows = pltpu.roll(acs_t, (SSM_HEADS - 4 * g) % SSM_HEADS, 0)
            hp = h_s[g]
            xt = xs * colx_ref[0, g]
            xt_s[...] = xt
            acx = colx_ref[1, g]
            ac_st, ar_st = _stack_cols_rows(acx, rows)
            m = jnp.concatenate([cb] * 4, axis=0) * jnp.exp(jnp.where(causal, ac_st - ar_st, -jnp.inf))
            for r in range(4):
                hd = slice(SSM_P * r, SSM_P * (r + 1))
                yd_s[:, hd] = _dot(m[CHUNK * r:CHUNK * (r + 1), :], xt_s[:, hd])
            yp_ref[g] = yd_s[...] + _dot(cm, hp) * jnp.exp(acx) + sc[0:1, :] * xs
            hst_ref[0, g] = hp
            h_s[g] = hp * jnp.exp(a_end) + _dot_tn(bm, xt * jnp.exp(a_end - acx))
            y2 = yp_ref[g] * _silu(z_ref[g])
            return s1 + jnp.sum(y2 * y2, axis=1, keepdims=True)

        s1 = lax.fori_loop(0, SSM_GROUPS // 2, lambda i, c: group(2 * i + 1, group(2 * i, c)),
                           jnp.zeros((CHUNK, 1), f32))
        rinv = lax.rsqrt(s1 * (1.0 / SSM_INNER) + EPS)

        def gate(g, carry):
            y2 = yp_ref[g] * _silu(z_ref[g])
            a3_ref[g] = (y2 * rinv * gn_ref[g]).astype(a3_ref.dtype)
            return carry

        lax.fori_loop(0, SSM_GROUPS, gate, 0)

    return _pc(
        body, name, grid=(nc,),
        in_specs=_ssm_core_specs(nc, False),
        out_specs=[pl.BlockSpec((SSM_KT, CHUNK, TILE), lambda c: (0, c, 0)),
                   pl.BlockSpec((SSM_KT, CHUNK, TILE), lambda c: (0, c, 0)),
                   pl.BlockSpec((1, SSM_GROUPS, SSM_N, TILE), lambda c: (c, 0, 0, 0))],
        out_shape=[SDS((SSM_KT, t_len, TILE), MXU_DTYPE), SDS((SSM_KT, t_len, TILE), f32),
                   SDS((nc, SSM_GROUPS, SSM_N, TILE), f32)],
        scratch_shapes=[pltpu.VMEM((SSM_GROUPS, SSM_N, TILE), f32), pltpu.VMEM((CHUNK, TILE), f32),
                        pltpu.VMEM((CHUNK, TILE), f32)],
        compiler_params=_params(("arbitrary",)),
    )(u, u, xbc, dtb, alog, dsk, gn, *pre)


def _ssm2_bwd(u, xbc, pre, yp, hst, da3, dtb, alog, dsk, gn, name):
    t_len = u.shape[1]
    nc = t_len // CHUNK

    def body(z_ref, dt_ref, x_ref, dtb_ref, alog_ref, dsk_ref, gn_ref, colx_s, scx_ref, acst_ref,
             yp_ref, hst_ref, da3_ref, du_ref, ddtb_ref, dalog_ref, ddsk_ref, dgn_ref,
             dh_s, xt_s, dy_s, dxt_s, ddtx_s, dacx_s, ddx_s, drow_s, dm_s, dmt_s):
        step = pl.program_id(0)

        @pl.when(step == 0)
        def _():
            dh_s[...] = jnp.zeros_like(dh_s)
            ddtb_ref[...] = jnp.zeros_like(ddtb_ref)
            dalog_ref[...] = jnp.zeros_like(dalog_ref)
            ddsk_ref[...] = jnp.zeros_like(ddsk_ref)
            dgn_ref[...] = jnp.zeros_like(dgn_ref)
        dtr = dt_ref[0, :, 0:SSM_HEADS] + dtb_ref[...]
        dt = _softplus(dtr)
        a_neg = -jnp.exp(alog_ref[...])
        acs_t = acst_ref[0]
        causal = _tri4()
        causal_t = (lax.broadcasted_iota(jnp.int32, (ATT_ROWS, CHUNK), 1)
                    >= lax.broadcasted_iota(jnp.int32, (ATT_ROWS, CHUNK), 0) % CHUNK)
        last = (lax.broadcasted_iota(jnp.int32, (1, CHUNK), 1) == CHUNK - 1).astype(f32)
        lane = lax.broadcasted_iota(jnp.int32, (1, TILE), 1)
        sub32 = lax.broadcasted_iota(jnp.int32, (SSM_HEADS, 1), 0)
        drow_s[...] = jnp.zeros_like(drow_s)

        def sums(g, carry):
            s1, s2 = carry
            y2 = yp_ref[g] * _silu(z_ref[g])
            g3 = da3_ref[g] * gn_ref[g]
            return (s1 + jnp.sum(y2 * y2, axis=1, keepdims=True), s2 + jnp.sum(g3 * y2, axis=1, keepdims=True))

        zcol = jnp.zeros((CHUNK, 1), f32)
        carry = (zcol, zcol)
        for g in range(SSM_GROUPS):
            carry = sums(g, carry)
        s1, s2 = carry
        rinv = lax.rsqrt(s1 * (1.0 / SSM_INNER) + EPS)
        m2 = s2 * rinv * rinv * rinv * (1.0 / SSM_INNER)

        def group(g, carry):
            z = z_ref[g]
            sg = _sigmoid(z)
            sz = z * sg
            y = yp_ref[g]
            y2 = y * sz
            da3 = da3_ref[g]
            dgn_ref[g] += jnp.sum(da3 * y2 * rinv, axis=0, keepdims=True)
            dy2 = rinv * (da3 * gn_ref[g]) - y2 * m2
            dy = dy2 * sz
            dy_s[...] = dy
            du_ref[g] = dy2 * y * (sg * (1.0 + z * (1.0 - sg)))
            xs = x_ref[g]
            bm, cm = x_ref[8 + g, :, 0:SSM_N], x_ref[8 + g, :, SSM_N:2 * SSM_N]
            cb = _dot_nt(cm, bm)
            cbt = _dot_nt(bm, cm)
            dtx, acx = colx_s[0, g], colx_s[1, g]
            sc = scx_ref[0, g]
            a_end = sc[1:2, :]
            ex = jnp.exp(acx)
            wdx = jnp.exp(a_end - acx)
            eend = jnp.exp(a_end)
            rows = pltpu.roll(acs_t, (SSM_HEADS - 4 * g) % SSM_HEADS, 0)
            hp = hst_ref[0, g]
            dhn = dh_s[g]
            xt = xs * dtx
            xt_s[...] = xt
            ch = _dot(cm, hp)
            gy = dy * ex
            dcm = _dot_nt(gy, hp)
            dh_s[g] = _dot_tn(cm, gy) + dhn * eend
            q = _dot(bm, dhn)
            dbm = _dot_nt(xt * wdx, dhn)
            qx = q * xt * wdx
            v_end = jnp.sum(dhn * hp, axis=0, keepdims=True) * eend + jnp.sum(qx, axis=0, keepdims=True)
            ac_st, ar_st = _stack_cols_rows(acx, rows)
            lam = jnp.exp(jnp.where(causal, ac_st - ar_st, -jnp.inf))
            lam_t = jnp.exp(jnp.where(causal_t, ar_st - ac_st, -jnp.inf))
            m = jnp.concatenate([cb] * 4, axis=0) * lam
            m_t = jnp.concatenate([cbt] * 4, axis=0) * lam_t
            for r in range(4):
                hd = slice(SSM_P * r, SSM_P * (r + 1))
                rs = slice(CHUNK * r, CHUNK * (r + 1))
                dm_s[rs, :] = _dot_nt(dy_s[:, hd], xt_s[:, hd])
                dmt_s[rs, :] = _dot_nt(xt_s[:, hd], dy_s[:, hd])
                dxt_s[:, hd] = _dot(m_t[rs, :], dy_s[:, hd])
            dm = dm_s[...]
            dl = dm * lam
            dseg = dm * m
            dseg_t = dmt_s[...] * m_t
            dcb = dl[0:CHUNK] + dl[CHUNK:2 * CHUNK] + dl[2 * CHUNK:3 * CHUNK] + dl[3 * CHUNK:4 * CHUNK]
            drows = jnp.zeros((SSM_HEADS, CHUNK), f32)
            for r in range(4):
                rs = slice(CHUNK * r, CHUNK * (r + 1))
                in_head = (lane >= SSM_P * r) & (lane < SSM_P * (r + 1))
                d_ac = jnp.sum(dseg_t[rs, :], axis=0, keepdims=True)
                d_ar = jnp.sum(dseg[rs, :], axis=0, keepdims=True)
                d_aend = jnp.sum(jnp.where(in_head, v_end, 0.0), axis=1, keepdims=True)
                drows = drows + jnp.where(sub32 == r, d_ac - d_ar + last * d_aend, 0.0)
            dxt = dxt_s[...] + q * wdx
            du_ref[8 + g] = sc[0:1, :] * dy + dxt * dtx
            ddtx_s[g] = dxt * xs
            dacx_s[g] = dy * ch * ex - qx
            ddx_s[g] = jnp.broadcast_to(jnp.sum(dy * xs, axis=0, keepdims=True), (8, TILE))
            du_ref[16 + g, :, 0:SSM_N] = dbm + _dot_tn(dcb, cm)
            du_ref[16 + g, :, SSM_N:2 * SSM_N] = dcm + _dot(dcb, bm)
            drow_s[...] += pltpu.roll(drows, (4 * g) % SSM_HEADS, 0)
            return carry

        lax.fori_loop(0, SSM_GROUPS, group, 0, unroll=8)
        ddt = jnp.zeros((CHUNK, SSM_HEADS), f32)
        dacs = jnp.zeros((CHUNK, SSM_HEADS), f32)
        ddsk = jnp.zeros((8, SSM_HEADS), f32)
        for g in range(SSM_GROUPS):
            col_g = _collapse_matrix(g)
            ddt = ddt + _dot_exact(ddtx_s[g], col_g, ((1,), (1,)), True, 2)
            dacs = dacs + _dot_exact(dacx_s[g], col_g, ((1,), (1,)), True, 2)
            ddsk = ddsk + _dot_exact(ddx_s[g], col_g, ((1,), (1,)), True, 2)
        upper = _tri(False)
        da = _dot_exact(upper, dacs, ((1,), (0,)), False) + _dot_exact(upper, drow_s[...], ((1,), (1,)), False)
        ddt = ddt + da * a_neg
        dalog_ref[...] += jnp.sum(da * dt, axis=0, keepdims=True) * a_neg
        ddtr = ddt * _sigmoid(dtr)
        ddtb_ref[...] += jnp.sum(ddtr, axis=0, keepdims=True)
        ddsk_ref[...] += ddsk[0:1, :]
        du_ref[SSM_NT - 1] = jnp.zeros((CHUNK, TILE), f32)
        du_ref[SSM_NT - 1, :, 0:SSM_HEADS] = ddtr

    def rc(c):
        return nc - 1 - c

    vec = pl.BlockSpec((1, SSM_HEADS), lambda c: (0, 0))
    return _pc(
        body, name, grid=(nc,),
        in_specs=_ssm_core_specs(nc, True) + [
            pl.BlockSpec((SSM_KT, CHUNK, TILE), lambda c: (0, rc(c), 0)),
            pl.BlockSpec((1, SSM_GROUPS, SSM_N, TILE), lambda c: (rc(c), 0, 0, 0)),
            pl.BlockSpec((SSM_KT, CHUNK, TILE), lambda c: (0, rc(c), 0))],
        out_specs=[pl.BlockSpec((SSM_NT, CHUNK, TILE), lambda c: (0, rc(c), 0)), vec, vec, vec,
                   pl.BlockSpec((SSM_KT, 1, TILE), lambda c: (0, 0, 0))],
        out_shape=[SDS((SSM_NT, t_len, TILE), f32), SDS((1, SSM_HEADS), f32), SDS((1, SSM_HEADS), f32),
                   SDS((1, SSM_HEADS), f32), SDS((SSM_KT, 1, TILE), f32)],
        scratch_shapes=[pltpu.VMEM((SSM_GROUPS, SSM_N, TILE), f32), pltpu.VMEM((CHUNK, TILE), f32),
                        pltpu.VMEM((CHUNK, TILE), f32), pltpu.VMEM((CHUNK, TILE), f32),
                        pltpu.VMEM((SSM_GROUPS, CHUNK, TILE), f32), pltpu.VMEM((SSM_GROUPS, CHUNK, TILE), f32),
                        pltpu.VMEM((SSM_GROUPS, 8, TILE), f32), pltpu.VMEM((SSM_HEADS, CHUNK), f32),
                        pltpu.VMEM((4 * CHUNK, CHUNK), f32), pltpu.VMEM((4 * CHUNK, CHUNK), f32)],
        compiler_params=_params(("arbitrary",)),
    )(u, u, xbc, dtb, alog, dsk, gn, *pre, yp, hst, da3)


def _swap16(t):
    lane = lax.broadcasted_iota(jnp.int32, t.shape, 1) % 64
    return jnp.where(lane < 8, pltpu.roll(t, TILE - 8, 1), jnp.where(lane < 16, pltpu.roll(t, 8, 1), 0.0))


def _rope(t, cos_t, sin_t):
    return t * cos_t + _swap16(t) * sin_t


def _rope_bwd(g, cos_t, sin_t):
    return g * cos_t + _swap16(g * sin_t)


def _att_in_specs(nb, rev):
    def bidx(n):
        return (nb - 1 - n) if rev else n
    return [
        pl.BlockSpec((ATT_NT, CHUNK, TILE), lambda n: (0, bidx(n), 0)),
        pl.BlockSpec((2, CHUNK, TILE), lambda n: (2, jnp.maximum(bidx(n) - 1, 0), 0)),
        pl.BlockSpec((CHUNK, TILE), lambda n: (bidx(n), 0)),
        pl.BlockSpec((CHUNK, TILE), lambda n: (bidx(n), 0)),
        pl.BlockSpec((CHUNK, TILE), lambda n: (jnp.maximum(bidx(n) - 1, 0), 0)),
        pl.BlockSpec((CHUNK, TILE), lambda n: (jnp.maximum(bidx(n) - 1, 0), 0)),
        pl.BlockSpec((1, 16), lambda n: (0, 0)),
    ]


ATT_SCALE = 0.125


def _tri4():
    row = lax.broadcasted_iota(jnp.int32, (ATT_ROWS, CHUNK), 0) % CHUNK
    col = lax.broadcasted_iota(jnp.int32, (ATT_ROWS, CHUNK), 1)
    return col <= row


def _stack_heads(ref):
    return jnp.concatenate([ref[:, 64 * r:64 * (r + 1)] for r in range(4)], axis=0)


def _sink_col(sinks, g):
    return [sinks[:, 4 * g + r:4 * g + r + 1] for r in range(4)]


def _softmax_rows(s_s, pn_s, pc_s, sink, tri, has_prev):
    sub = lax.broadcasted_iota(jnp.int32, (ATT_ROWS, 1), 0)
    sk = jnp.where(sub < CHUNK, sink[0], jnp.where(sub < 2 * CHUNK, sink[1], jnp.where(sub < 3 * CHUNK, sink[2], sink[3])))
    s = jnp.where(tri, s_s[:, CHUNK:2 * CHUNK], jnp.where(has_prev, s_s[:, 0:CHUNK], -jnp.inf)) * ATT_SCALE
    m = jnp.maximum(jnp.max(s, axis=-1, keepdims=True), sk)
    p = jnp.exp(s - m)
    e_sink = jnp.exp(sk - m)
    inv = 1.0 / (jnp.sum(p, axis=-1, keepdims=True) + e_sink)
    pn = p * inv
    pc_s[...] = pn
    pn_s[:, 0:CHUNK] = jnp.where(tri, 0.0, pn)
    pn_s[:, CHUNK:2 * CHUNK] = jnp.where(tri, pn, 0.0)
    return e_sink * inv


def _att2_fwd(u, cos_t, sin_t, sinks, name):
    t_len = u.shape[1]
    nb = t_len // CHUNK

    def body(u_ref, prev_ref, cc_ref, sc_ref, cp_ref, sp_ref, sink_ref, a_ref,
             q_s, kp_s, kc_s, vp_s, vc_s, o_s, s_s, pn_s, pc_s):
        n = pl.program_id(0)
        tri = _tri4()
        cos_c, sin_c = cc_ref[...], sc_ref[...]
        kc_s[...] = _rope(u_ref[4], cos_c, sin_c)
        kp_s[...] = _rope(prev_ref[0], cp_ref[...], sp_ref[...])
        vc_s[...] = u_ref[5]
        vp_s[...] = prev_ref[1]
        sinks = sink_ref[...]
        for g in range(4):
            q_s[g] = _rope(u_ref[g], cos_c, sin_c)
        for g in range(4):
            kv = slice(64 * g, 64 * (g + 1))
            kb = jnp.concatenate([kp_s[:, kv], kc_s[:, kv]], axis=0)
            s_s[g] = _dot_nt(_stack_heads(q_s.at[g]), kb)
        for g in range(4):
            _softmax_rows(s_s.at[g], pn_s.at[g], pc_s.at[g], _sink_col(sinks, g), tri, n > 0)
        for g in range(4):
            kv = slice(64 * g, 64 * (g + 1))
            vb = jnp.concatenate([vp_s[:, kv], vc_s[:, kv]], axis=0)
            o = _dot(pn_s[g], vb)
            for r in range(4):
                o_s[g, :, 64 * r:64 * (r + 1)] = o[CHUNK * r:CHUNK * (r + 1), :]
        for g in range(4):
            a_ref[g] = (o_s[g] * _silu(u_ref[6 + g])).astype(a_ref.dtype)

    return _pc(
        body, name, grid=(nb,),
        in_specs=_att_in_specs(nb, False),
        out_specs=pl.BlockSpec((ATT_KT, CHUNK, TILE), lambda n: (0, n, 0)),
        out_shape=SDS((ATT_KT, t_len, TILE), MXU_DTYPE),
        scratch_shapes=[pltpu.VMEM((4, CHUNK, TILE), f32)] + [pltpu.VMEM((CHUNK, TILE), f32)] * 4
                       + [pltpu.VMEM((4, CHUNK, TILE), f32)] + [pltpu.VMEM((4, ATT_ROWS, 2 * CHUNK), f32)] * 2
                       + [pltpu.VMEM((4, ATT_ROWS, CHUNK), f32)],
        compiler_params=_params(("arbitrary",)),
    )(u, u, cos_t, sin_t, cos_t, sin_t, sinks)


def _att2_bwd(u, cos_t, sin_t, sinks, da, name):
    t_len = u.shape[1]
    nb = t_len // CHUNK

    def body(u_ref, prev_ref, cc_ref, sc_ref, cp_ref, sp_ref, sink_ref, da_ref, du_ref, dsink_ref,
             ck_s, cv_s, kp_s, kc_s, vp_s, vc_s, q_s, o_s, do_s, dq_s, s_s, pn_s, dp_s, dkt_s, dvt_s, pc_s):
        step = pl.program_id(0)
        nn = nb - 1 - step

        @pl.when(step == 0)
        def _():
            ck_s[...] = jnp.zeros_like(ck_s)
            cv_s[...] = jnp.zeros_like(cv_s)
            dsink_ref[...] = jnp.zeros_like(dsink_ref)
        tri = _tri4()
        cos_c, sin_c = cc_ref[...], sc_ref[...]
        cos_p, sin_p = cp_ref[...], sp_ref[...]
        kc_s[...] = _rope(u_ref[4], cos_c, sin_c)
        kp_s[...] = _rope(prev_ref[0], cos_p, sin_p)
        vc_s[...] = u_ref[5]
        vp_s[...] = prev_ref[1]
        sinks = sink_ref[...]
        lane16 = lax.broadcasted_iota(jnp.int32, (1, 16), 1)
        dsink = jnp.zeros((1, 16), f32)
        for g in range(4):
            q_g, o_g, do_g, dq_g = q_s.at[g], o_s.at[g], do_s.at[g], dq_s.at[g]
            s_g, pn_g, dp_g, pc_g = s_s.at[g], pn_s.at[g], dp_s.at[g], pc_s.at[g]
            q_g[...] = _rope(u_ref[g], cos_c, sin_c)
            gate = u_ref[6 + g]
            sg = _sigmoid(gate)
            dav = da_ref[g]
            do_g[...] = dav * (gate * sg)
            kv = slice(64 * g, 64 * (g + 1))
            kb = jnp.concatenate([kp_s[:, kv], kc_s[:, kv]], axis=0)
            vb = jnp.concatenate([vp_s[:, kv], vc_s[:, kv]], axis=0)
            q_st = _stack_heads(q_g)
            do_st = _stack_heads(do_g)
            s_g[...] = _dot_nt(q_st, kb)
            p_sink = _softmax_rows(s_g, pn_g, pc_g, _sink_col(sinks, g), tri, nn > 0)
            o = _dot(pn_g[...], vb)
            dvt_s[64 * g:64 * (g + 1), :] = _dot_tn(do_st, pn_g[...])
            dp_g[...] = _dot_nt(do_st, vb)
            delta = jnp.sum(do_st * o, axis=-1, keepdims=True)
            dpc = jnp.where(tri, dp_g[:, CHUNK:2 * CHUNK], dp_g[:, 0:CHUNK])
            dsc = pc_g[...] * (dpc - delta) * ATT_SCALE
            dp_g[:, 0:CHUNK] = jnp.where(tri, 0.0, dsc)
            dp_g[:, CHUNK:2 * CHUNK] = jnp.where(tri, dsc, 0.0)
            sd = p_sink * delta
            for r in range(4):
                rs = slice(CHUNK * r, CHUNK * (r + 1))
                o_g[:, 64 * r:64 * (r + 1)] = o[rs, :]
                ds_h = -jnp.sum(sd[rs, :], axis=0, keepdims=True)
                dsink = dsink + ds_h * (lane16 == 4 * g + r).astype(f32)
            ds = dp_g[...]
            dq = _dot(ds, kb)
            for r in range(4):
                dq_g[:, 64 * r:64 * (r + 1)] = dq[CHUNK * r:CHUNK * (r + 1), :]
            dkt_s[64 * g:64 * (g + 1), :] = _dot_tn(q_st, ds)
            du_ref[6 + g] = dav * o_g[...] * (sg * (1.0 + gate * (1.0 - sg)))
            du_ref[g] = _rope_bwd(dq_g[...], cos_c, sin_c)
        dk = dkt_s[...].T
        dv = dvt_s[...].T
        du_ref[4] = _rope_bwd(dk[CHUNK:2 * CHUNK, :], cos_c, sin_c) + ck_s[...]
        du_ref[5] = dv[CHUNK:2 * CHUNK, :] + cv_s[...]
        ck_s[...] = _rope_bwd(dk[0:CHUNK, :], cos_p, sin_p)
        cv_s[...] = dv[0:CHUNK, :]
        dsink_ref[...] += dsink

    def rb(n):
        return nb - 1 - n

    return _pc(
        body, name, grid=(nb,),
        in_specs=_att_in_specs(nb, True) + [pl.BlockSpec((ATT_KT, CHUNK, TILE), lambda n: (0, rb(n), 0))],
        out_specs=[pl.BlockSpec((ATT_NT, CHUNK, TILE), lambda n: (0, rb(n), 0)),
                   pl.BlockSpec((1, 16), lambda n: (0, 0))],
        out_shape=[SDS((ATT_NT, t_len, TILE), f32), SDS((1, 16), f32)],
        scratch_shapes=[pltpu.VMEM((CHUNK, TILE), f32)] * 6 + [pltpu.VMEM((4, CHUNK, TILE), f32)] * 4
                       + [pltpu.VMEM((4, ATT_ROWS, 2 * CHUNK), f32)] * 3
                       + [pltpu.VMEM((2 * CHUNK, 2 * CHUNK), f32)] * 2 + [pltpu.VMEM((4, ATT_ROWS, CHUNK), f32)],
        compiler_params=_params(("arbitrary",)),
    )(u, u, cos_t, sin_t, cos_t, sin_t, sinks, da)


_HBM = pl.BlockSpec(memory_space=pltpu.HBM)


def _all_gather_big(shards):
    n = len(shards)

    def body(*refs):
        x_refs, out_refs = refs[:n], refs[n:2 * n]
        send_sems, recv_sems, local_sems = refs[2 * n:]
        x, y, c = lax.axis_index("x"), lax.axis_index("y"), lax.axis_index("c")
        me, sibling = (x, y, c), (x, y, 1 - c)
        chips = [(1 - x, y), (x, 1 - y), (1 - x, 1 - y)]

        def slot(i, px, py, pc):
            return out_refs[i].at[4 * px + 2 * py + pc]

        def copy(i, k, block, to, src=None):
            return pltpu.make_async_remote_copy(
                src_ref=slot(i, *block) if src is None else src, dst_ref=slot(i, *block),
                send_sem=send_sems.at[7 * i + k], recv_sem=recv_sems.at[7 * i + k], device_id=to, device_id_type=MESH)

        mine = [pltpu.make_async_copy(x_refs[i], slot(i, *me), local_sems.at[i]) for i in range(n)]
        for cp in mine:
            cp.start()
        first = []
        for i in range(n):
            first.append(copy(i, 0, me, sibling, src=x_refs[i]))
            first += [copy(i, 1 + j, me, (*chip, c), src=x_refs[i]) for j, chip in enumerate(chips)]
        for cp in first:
            cp.start()
        passed = []
        for j, chip in enumerate(chips):
            for i in range(n):
                copy(i, 1 + j, (*chip, c), me).wait_recv()
                fwd = copy(i, 4 + j, (*chip, c), sibling)
                fwd.start()
                passed.append(fwd)
        for i in range(n):
            copy(i, 0, sibling, me).wait_recv()
            for j, chip in enumerate(chips):
                copy(i, 4 + j, (*chip, 1 - c), me).wait_recv()
        for cp in first + passed:
            cp.wait_send()
        for cp in mine:
            cp.wait()

    return _pc(
        body, "all_gather_big",
        in_specs=[_HBM] * n, out_specs=[_HBM] * n,
        out_shape=[SDS((N_DEV,) + s.shape, s.dtype) for s in shards],
        scratch_shapes=[pltpu.SemaphoreType.DMA((7 * n,)), pltpu.SemaphoreType.DMA((7 * n,)),
                        pltpu.SemaphoreType.DMA((n,))],
    )(*shards)


def _all_gather_direct(block, name):
    rows, width = block.shape

    def body(x_ref, out_ref, send_sems, recv_sems, local_sem):
        x, y, c = lax.axis_index("x"), lax.axis_index("y"), lax.axis_index("c")
        my_slot = 4 * x + 2 * y + c

        def peer(k):
            return (1 - x if k & 4 else x, 1 - y if k & 2 else y, 1 - c if k & 1 else c)

        def copy(k):
            px, py, pc = peer(k)
            return pltpu.make_async_remote_copy(
                src_ref=x_ref, dst_ref=out_ref.at[my_slot], send_sem=send_sems.at[k - 1], recv_sem=recv_sems.at[k - 1],
                device_id=(px, py, pc), device_id_type=MESH)

        def arrival(k):
            px, py, pc = peer(k)
            return pltpu.make_async_remote_copy(
                src_ref=x_ref, dst_ref=out_ref.at[4 * px + 2 * py + pc], send_sem=send_sems.at[k - 1],
                recv_sem=recv_sems.at[k - 1], device_id=(px, py, pc), device_id_type=MESH)

        mine = pltpu.make_async_copy(x_ref, out_ref.at[my_slot], local_sem)
        mine.start()
        for k in range(1, N_DEV):
            copy(k).start()
        for k in range(1, N_DEV):
            arrival(k).wait_recv()
        for k in range(1, N_DEV):
            copy(k).wait_send()
        mine.wait()

    return _pc(
        body, name,
        in_specs=[_HBM], out_specs=_HBM,
        out_shape=SDS((N_DEV, rows, width), block.dtype),
        scratch_shapes=[pltpu.SemaphoreType.DMA((7,)), pltpu.SemaphoreType.DMA((7,)), pltpu.SemaphoreType.DMA],
    )(block)


N_CHIP = N_DEV // 2


def _exchange_sibling(gs):
    n = len(gs)

    def body(*refs):
        g_refs, out_refs = refs[:n], refs[n:2 * n]
        send_sems, recv_sems = refs[2 * n:]
        x, y, c = lax.axis_index("x"), lax.axis_index("y"), lax.axis_index("c")
        cps = [pltpu.make_async_remote_copy(
            src_ref=g_refs[i].at[2 * k + 1 - c], dst_ref=out_refs[i].at[k], send_sem=send_sems.at[N_CHIP * i + k],
            recv_sem=recv_sems.at[N_CHIP * i + k], device_id=(x, y, 1 - c), device_id_type=MESH)
            for i in range(n) for k in range(N_CHIP)]
        for cp in cps:
            cp.start()
        for cp in cps:
            cp.wait()

    return _pc(
        body, "rs_sibling",
        in_specs=[_HBM] * n, out_specs=[_HBM] * n,
        out_shape=[SDS((N_CHIP,) + g.shape[1:], g.dtype) for g in gs],
        scratch_shapes=[pltpu.SemaphoreType.DMA((N_CHIP * n,)), pltpu.SemaphoreType.DMA((N_CHIP * n,))],
    )(*gs)


def _pair_sum(g, r1, cidx, tr, name):
    _, rows, width = g.shape

    def body(c_ref, g_ref, r_ref, o_ref):
        o_ref[...] = (g_ref[...].astype(f32) + r_ref[...].astype(f32)).astype(o_ref.dtype)

    return pl.pallas_call(
        body, name=name,
        grid_spec=pltpu.PrefetchScalarGridSpec(
            num_scalar_prefetch=1, grid=(N_CHIP, rows // tr),
            in_specs=[pl.BlockSpec((1, tr, width), lambda k, i, c_ref: (2 * k + c_ref[0], i, 0)),
                      pl.BlockSpec((1, tr, width), lambda k, i, c_ref: (k, i, 0))],
            out_specs=pl.BlockSpec((1, tr, width), lambda k, i, c_ref: (k, i, 0))),
        out_shape=SDS((N_CHIP, rows, width), g.dtype),
        compiler_params=_params(("arbitrary", "arbitrary")),
    )(cidx, g, r1)


def _exchange_chips(ps):
    n = len(ps)

    def body(*refs):
        p_refs, out_refs = refs[:n], refs[n:2 * n]
        send_sems, recv_sems, local_sems = refs[2 * n:]
        x, y, c = lax.axis_index("x"), lax.axis_index("y"), lax.axis_index("c")
        my_chip = 2 * x + y
        chips = [(1 - x, y), (x, 1 - y), (1 - x, 1 - y)]

        def copy(i, j):
            px, py = chips[j]
            return pltpu.make_async_remote_copy(
                src_ref=p_refs[i].at[2 * px + py], dst_ref=out_refs[i].at[my_chip], send_sem=send_sems.at[3 * i + j],
                recv_sem=recv_sems.at[3 * i + j], device_id=(px, py, c), device_id_type=MESH)

        def arrival(i, j):
            px, py = chips[j]
            return pltpu.make_async_remote_copy(
                src_ref=p_refs[i].at[my_chip], dst_ref=out_refs[i].at[2 * px + py], send_sem=send_sems.at[3 * i + j],
                recv_sem=recv_sems.at[3 * i + j], device_id=(px, py, c), device_id_type=MESH)

        mine = [pltpu.make_async_copy(p_refs[i].at[my_chip], out_refs[i].at[my_chip], local_sems.at[i])
                for i in range(n)]
        for cp in mine:
            cp.start()
        for i in range(n):
            for j in range(3):
                copy(i, j).start()
        for i in range(n):
            for j in range(3):
                arrival(i, j).wait_recv()
        for i in range(n):
            for j in range(3):
                copy(i, j).wait_send()
        for cp in mine:
            cp.wait()

    return _pc(
        body, "rs_chips",
        in_specs=[_HBM] * n, out_specs=[_HBM] * n,
        out_shape=[SDS(p.shape, p.dtype) for p in ps],
        scratch_shapes=[pltpu.SemaphoreType.DMA((3 * n,)), pltpu.SemaphoreType.DMA((3 * n,)),
                        pltpu.SemaphoreType.DMA((n,))],
    )(*ps)


def _adamw(parts, w, m, v, tr, name):
    n, rows, width = parts.shape
    c1 = 1.0 / (1.0 - ADAM_B1 ** ADAM_STEP)
    c2 = 1.0 / (1.0 - ADAM_B2 ** ADAM_STEP)

    def body(p_ref, w_ref, m_ref, v_ref, g_ref, d_ref, mo_ref, vo_ref):
        g = p_ref[0].astype(f32)
        for k in range(1, n):
            g = g + p_ref[k].astype(f32)
        mn = ADAM_B1 * m_ref[...] + (1.0 - ADAM_B1) * g
        vn = ADAM_B2 * v_ref[...] + (1.0 - ADAM_B2) * (g * g)
        g_ref[...] = g
        mo_ref[...] = mn
        vo_ref[...] = vn
        d_ref[...] = -ADAM_LR * ((mn * c1) / (jnp.sqrt(vn * c2) + ADAM_EPS) + ADAM_WD * w_ref[...])

    blk = pl.BlockSpec((tr, width), lambda i: (i, 0))
    return _pc(
        body, name, grid=(rows // tr,),
        in_specs=[pl.BlockSpec((n, tr, width), lambda i: (0, i, 0)), blk, blk, blk],
        out_specs=[blk, blk, blk, blk],
        out_shape=[SDS((rows, width), f32)] * 4,
        compiler_params=_params(("arbitrary",)),
    )(parts, w, m, v)


ROWS_REST = ROWS_SSM_OUT + ROWS_ATT_OUT + 16


def _pack_rest(ssm_w_out, att_w_out, conv_w):
    conv = jnp.pad(conv_w.reshape(4, 1024), ((0, 12), (0, 0)))
    return jnp.concatenate([ssm_w_out.reshape(ROWS_SSM_OUT, 1024), att_w_out.reshape(ROWS_ATT_OUT, 1024), conv], axis=0)


def _unpack_rest(p):
    o = ROWS_SSM_OUT + ROWS_ATT_OUT
    return (p[0:ROWS_SSM_OUT].reshape(2, 256, 1024), p[ROWS_SSM_OUT:o].reshape(2, 128, 1024),
            p[o:o + 4].reshape(2, 4, 512))


def _pack_grads(d_ssm_w_in, d_ssm_w_out, d_att_w_in, d_att_w_out, d_conv_w):
    wire = lambda t: t.astype(MXU_DTYPE)
    a = jnp.transpose(wire(d_ssm_w_in).reshape(2, 1024, 8, 772), (2, 0, 1, 3)).reshape(8, 2048, 772)
    c = jnp.transpose(wire(d_att_w_in).reshape(2, 1024, 8, 320), (2, 0, 1, 3)).reshape(8, 2048, 320)
    b = jnp.transpose(wire(d_ssm_w_out).reshape(2, 8, 256, 1024), (1, 0, 2, 3)).reshape(8, ROWS_SSM_OUT, 1024)
    d = jnp.transpose(wire(d_att_w_out).reshape(2, 8, 128, 1024), (1, 0, 2, 3)).reshape(8, ROWS_ATT_OUT, 1024)
    e = jnp.transpose(wire(d_conv_w).reshape(2, 4, 8, 512), (2, 0, 1, 3)).reshape(8, 4, 1024)
    e = jnp.pad(e, ((0, 0), (0, 12), (0, 0)))
    return a, c, jnp.concatenate([b, d, e], axis=1)


def _pad8(a):
    return jnp.pad(a, ((0, 8 - a.shape[0]), (0, 0)))


def _pack_small(pre_norm, post_norm, conv_b, gate_norm, dt_bias, a_log, d_skip, sinks, extra=None):
    row = jnp.concatenate([dt_bias.reshape(1, 64), a_log.reshape(1, 64), d_skip.reshape(1, 64), sinks.reshape(1, 32),
                           jnp.zeros((1, 1024 - 224), f32)], axis=1)
    if extra is not None:
        row = row + jnp.pad(extra.reshape(1, 1), ((0, 0), (224, 1024 - 225)))
    return jnp.concatenate([_pad8(pre_norm.reshape(4, 1024)), _pad8(post_norm.reshape(4, 1024)),
                            conv_b.reshape(8, 1024), _pad8(gate_norm.reshape(4, 1024)), _pad8(row)], axis=0)


def _unpack_small(p):
    row = p[32]
    return (p[0:4], p[8:12], p[16:24].reshape(2, 4096), row[0:64].reshape(2, 32), row[64:128].reshape(2, 32),
            row[128:192].reshape(2, 32), p[24:28].reshape(2, 2048), row[192:224].reshape(2, 16))


def _ssm_w_in_tiles(w):
    wb = w[:, 4096:5120].reshape(1024, 8, 128)
    wc = w[:, 5120:6144].reshape(1024, 8, 128)
    wbc = jnp.concatenate([wb, wc], axis=2).reshape(1024, 2048)
    return jnp.concatenate([w[:, 0:4096], wbc, w[:, 6144:6176], jnp.zeros((1024, 224), w.dtype)], axis=1)


def _ssm_w_in_untile(dw):
    dbc = dw[:, 4096:6144].reshape(1024, 8, 256)
    return jnp.concatenate([dw[:, 0:4096], dbc[:, :, 0:128].reshape(1024, 1024), dbc[:, :, 128:256].reshape(1024, 1024),
                            dw[:, 6144:6176]], axis=1)


def _conv_tiles(cw):
    k = cw.shape[0]
    xs = jnp.transpose(cw[:, 0:2048].reshape(k, 8, 256), (1, 0, 2))
    b = cw[:, 2048:3072].reshape(k, 8, 128)
    c = cw[:, 3072:4096].reshape(k, 8, 128)
    bc = jnp.transpose(jnp.concatenate([b, c], axis=2), (1, 0, 2))
    return jnp.concatenate([xs, bc], axis=0)


def _conv_untile(t):
    k = t.shape[1]
    xs = jnp.transpose(t[0:8], (1, 0, 2)).reshape(k, 2048)
    bc = jnp.transpose(t[8:16], (1, 0, 2))
    return jnp.concatenate([xs, bc[:, :, 0:128].reshape(k, 1024), bc[:, :, 128:256].reshape(k, 1024)], axis=1)


def _rope_tables(positions):
    inv = ROPE_THETA ** (-jnp.arange(0, 16, 2, dtype=f32) / 16)
    ang = positions.astype(f32).reshape(-1, 1) * inv
    cos, sin = jnp.cos(ang), jnp.sin(ang)
    t_len = ang.shape[0]
    cos64 = jnp.concatenate([cos, cos, jnp.ones((t_len, 48), f32)], axis=1)
    sin64 = jnp.concatenate([-sin, sin, jnp.zeros((t_len, 48), f32)], axis=1)
    return jnp.tile(cos64, (1, 4)), jnp.tile(sin64, (1, 4))


def kernel(x, positions, pre_norm, post_norm, ssm_w_in, ssm_conv_w, ssm_conv_b, ssm_dt_bias, ssm_a_log, ssm_d, ssm_gate_norm, ssm_w_out, att_w_in, att_sinks, att_w_out, loss_target, m_pre_norm, m_post_norm, m_ssm_w_in, m_ssm_conv_w, m_ssm_conv_b, m_ssm_dt_bias, m_ssm_a_log, m_ssm_d, m_ssm_gate_norm, m_ssm_w_out, m_att_w_in, m_att_sinks, m_att_w_out, v_pre_norm, v_post_norm, v_ssm_w_in, v_ssm_conv_w, v_ssm_conv_b, v_ssm_dt_bias, v_ssm_a_log, v_ssm_d, v_ssm_gate_norm, v_ssm_w_out, v_att_w_in, v_att_sinks, v_att_w_out):
    t_len = x.shape[1]
    tm = min(1024, t_len)
    xin = x.reshape(t_len, D_MODEL)
    tgt = loss_target.reshape(t_len, D_MODEL)
    cidx = lax.axis_index("c").astype(jnp.int32).reshape(1)

    g_ssm_in, g_att_in, g_ssm_out, g_att_out = _all_gather_big(
        [ssm_w_in.astype(MXU_DTYPE), att_w_in.astype(MXU_DTYPE), ssm_w_out.astype(MXU_DTYPE),
         att_w_out.astype(MXU_DTYPE)])
    conv_local = jnp.concatenate([ssm_conv_w.reshape(4, 1024), jnp.zeros((4, 1024), f32)], axis=0)
    conv_all = _all_gather_direct(conv_local, "all_gather_conv")[:, 0:4]
    w_ssm_in = jnp.transpose(g_ssm_in, (1, 2, 0, 3)).reshape(2, 1024, SSM_IN)
    w_ssm_out = jnp.transpose(g_ssm_out, (1, 0, 2, 3)).reshape(2, SSM_INNER, 1024)
    w_att_in = jnp.transpose(g_att_in, (1, 2, 0, 3)).reshape(2, 1024, ATT_IN)
    w_att_out = jnp.transpose(g_att_out, (1, 0, 2, 3)).reshape(2, 1024, 1024)
    conv_w = jnp.transpose(conv_all.reshape(8, 2, 4, 512), (1, 2, 0, 3)).reshape(2, 4, 4096)
    cos_t, sin_t = _rope_tables(positions)

    saved = []
    xc = xin
    for i in range(4):
        j = i // 2
        wn_pre, wn_post = pre_norm[i].reshape(1, D_MODEL), post_norm[i].reshape(1, D_MODEL)
        if i % 2 == 0:
            w_in = _ssm_w_in_tiles(w_ssm_in[j])
            cw, cb = _conv_tiles(conv_w[j]), _conv_tiles(ssm_conv_b[j].reshape(1, 4096))
            dtb, alog, dsk = ssm_dt_bias[j].reshape(1, 32), ssm_a_log[j].reshape(1, 32), ssm_d[j].reshape(1, 32)
            gn = ssm_gate_norm[j].reshape(SSM_KT, 1, TILE)
            w_zdt = jnp.concatenate([w_in[:, 0:SSM_INNER], w_in[:, 24 * TILE:25 * TILE]], axis=1)
            u, h = _mm_in(xc, wn_pre, w_zdt, 3, tm, f"ssm_in_{j}")
            uc, xbc = _mm_in_conv(xc, wn_pre, w_in[:, SSM_INNER:24 * TILE], cw, cb, tm, f"ssm_inconv_{j}")
            pre = _ssm_pre(u, dtb, alog, dsk, f"ssm_pre_{j}")
            a3, yp, hst = _ssm2_fwd(u, xbc, pre, dtb, alog, dsk, gn, f"ssm_core_{j}")
            y, xn = _mm_out(a3, w_ssm_out[j], xc, wn_post, 4, tm, f"ssm_out_{j}")
            saved.append(dict(x=xc, u=u, uc=uc, h=h, a=a3, yp=yp, hst=hst, y=y, w_in=w_in, cw=cw, cb=cb, dtb=dtb,
                              alog=alog, dsk=dsk, gn=gn, xbc=xbc, pre=pre))
        else:
            sinks = att_sinks[j].reshape(1, 16)
            u, h = _mm_in(xc, wn_pre, w_att_in[j], 5, tm, f"att_in_{j}")
            a = _att2_fwd(u, cos_t, sin_t, sinks, f"att_core_{j}")
            y, xn = _mm_out(a, w_att_out[j], xc, wn_post, 4, tm, f"att_out_{j}")
            saved.append(dict(x=xc, u=u, h=h, a=a, y=y, sinks=sinks))
        xc = xn

    dx, loss_part = _loss_grad(xc, tgt, tm)

    d_pre, d_post = [None] * 4, [None] * 4
    d_ssm_in, d_ssm_out, d_att_in, d_att_out = [None] * 2, [None] * 2, [None] * 2, [None] * 2
    d_cw, d_cb, d_dtb, d_alog, d_dsk, d_gn, d_sinks = ([None] * 2 for _ in range(7))
    for i in reversed(range(4)):
        j = i // 2
        s = saved[i]
        wn_pre, wn_post = pre_norm[i].reshape(1, D_MODEL), post_norm[i].reshape(1, D_MODEL)
        if i % 2 == 0:
            da3, dy, d_post[i] = _mm_dout(s["y"], dx, wn_post, w_ssm_out[j], 4, tm, f"ssm_dout_{j}")
            d_ssm_out[j] = _dw_rows(s["a"], dy, 4, tm, f"ssm_dwout_{j}")
            du, d_dtb[j], d_alog[j], d_dsk[j], dgn = _ssm2_bwd(
                s["u"], s["xbc"], s["pre"], s["yp"], s["hst"], da3, s["dtb"], s["alog"], s["dsk"], s["gn"],
                f"ssm_core_bwd_{j}")
            du, dcw, dcb = _conv_bwd(s["uc"], du, s["cw"], s["cb"], f"ssm_conv_bwd_{j}")
            d_cw[j], d_cb[j], d_gn[j] = _conv_untile(dcw), _conv_untile(dcb), dgn.reshape(1, SSM_INNER)
            d_ssm_in[j] = _ssm_w_in_untile(_dw_cols(s["h"], du, 5, tm, f"ssm_dwin_{j}"))
            dx, d_pre[i] = _mm_dh(du, s["w_in"], s["x"], dx, wn_pre, 5, tm, f"ssm_dh_{j}")
        else:
            da, dy, d_post[i] = _mm_dout(s["y"], dx, wn_post, w_att_out[j], 4, tm, f"att_dout_{j}")
            d_att_out[j] = _dw_rows(s["a"], dy, 4, tm, f"att_dwout_{j}")
            du, d_sinks[j] = _att2_bwd(s["u"], cos_t, sin_t, s["sinks"], da, f"att_core_bwd_{j}")
            d_att_in[j] = _dw_cols(s["h"], du, 5, tm, f"att_dwin_{j}")
            dx, d_pre[i] = _mm_dh(du, w_att_in[j], s["x"], dx, wn_pre, 5, tm, f"att_dh_{j}")

    gs = _pack_grads(jnp.stack(d_ssm_in), jnp.stack(d_ssm_out), jnp.stack(d_att_in), jnp.stack(d_att_out),
                     jnp.stack(d_cw))
    r1 = _exchange_sibling(gs)
    tiles = (256, 256, ROWS_REST // 7)
    pairs = [_pair_sum(g, r, cidx, tr, f"rs_pair_sum_{k}") for k, (g, r, tr) in enumerate(zip(gs, r1, tiles))]
    parts = _exchange_chips(pairs)
    flat = lambda t: t.reshape(2048, t.shape[-1])
    a4 = _adamw(parts[0], flat(ssm_w_in), flat(m_ssm_w_in), flat(v_ssm_w_in), tiles[0], "adamw_ssm_in")
    b4 = _adamw(parts[1], flat(att_w_in), flat(m_att_w_in), flat(v_att_w_in), tiles[1], "adamw_att_in")
    c4 = _adamw(parts[2], _pack_rest(ssm_w_out, att_w_out, ssm_conv_w), _pack_rest(m_ssm_w_out, m_att_w_out, m_ssm_conv_w),
                _pack_rest(v_ssm_w_out, v_att_w_out, v_ssm_conv_w), tiles[2], "adamw_rest")
    big = []
    for k in range(4):
        o_ssm_out, o_att_out, o_conv = _unpack_rest(c4[k])
        big.append((a4[k].reshape(2, 1024, 772), o_ssm_out, b4[k].reshape(2, 1024, 320), o_att_out, o_conv))

    small_local = _pack_small(jnp.concatenate(d_pre, axis=0), jnp.concatenate(d_post, axis=0),
                              jnp.concatenate(d_cb, axis=0), jnp.concatenate(d_gn, axis=0),
                              jnp.concatenate(d_dtb, axis=0), jnp.concatenate(d_alog, axis=0),
                              jnp.concatenate(d_dsk, axis=0), jnp.concatenate(d_sinks, axis=0), loss_part[0, 0])
    small_all = _all_gather_direct(small_local, "all_gather_small")
    ws = _pack_small(pre_norm, post_norm, ssm_conv_b, ssm_gate_norm, ssm_dt_bias, ssm_a_log, ssm_d, att_sinks)
    ms = _pack_small(m_pre_norm, m_post_norm, m_ssm_conv_b, m_ssm_gate_norm, m_ssm_dt_bias, m_ssm_a_log, m_ssm_d,
                     m_att_sinks)
    vs = _pack_small(v_pre_norm, v_post_norm, v_ssm_conv_b, v_ssm_gate_norm, v_ssm_dt_bias, v_ssm_a_log, v_ssm_d,
                     v_att_sinks)
    small4 = _adamw(small_all, ws, ms, vs, ROWS_SMALL, "adamw_small")
    loss = small4[0][32, 224]
    small = [_unpack_small(t) for t in small4]

    outs = [loss, dx.reshape(1, t_len, D_MODEL)]
    for k in range(4):
        b_ssm_in, b_ssm_out, b_att_in, b_att_out, b_conv = big[k]
        s_pre, s_post, s_cb, s_dtb, s_alog, s_d, s_gn, s_sinks = small[k]
        outs += [s_pre, s_post, b_ssm_in, b_conv, s_cb, s_dtb, s_alog, s_d, s_gn, b_ssm_out, b_att_in, s_sinks,
                 b_att_out]
    return tuple(outs)
```

```python
import jax
import jax.numpy as jnp
from jax import lax
from jax.experimental import pallas as pl
from jax.experimental.pallas import tpu as pltpu

f32 = jnp.float32
MXU_DTYPE = jnp.bfloat16
SDS = jax.ShapeDtypeStruct
MESH = pl.DeviceIdType.MESH

D_MODEL = 1024
EPS = 1e-6
TILE = 256
CHUNK = 128
ATT_ROWS = 4 * CHUNK
SSM_HEADS = 32
SSM_GROUPS = 8
SSM_P = 64
SSM_N = 128
SSM_INNER = 2048
SSM_IN = 6176
SSM_NT = 25
SSM_KT = 8
ATT_NT = 10
ATT_KT = 4
ATT_IN = 2560
ROPE_THETA = 500000.0
N_DEV = 8
VMEM_LIMIT = 56 * 1024 * 1024

ADAM_LR = 0.001
ADAM_B1 = 0.9
ADAM_B2 = 0.999
ADAM_EPS = 1e-08
ADAM_WD = 0.01
ADAM_STEP = 10

ROWS_SSM_OUT = 2 * 256
ROWS_ATT_OUT = 2 * 128
ROWS_SMALL = 40


def _pc(body, name, **kw):
    return pl.pallas_call(body, name=name, **kw)


def _params(sem):
    return pltpu.CompilerParams(dimension_semantics=sem, vmem_limit_bytes=VMEM_LIMIT)


def _sigmoid(x):
    return 0.5 * jnp.tanh(0.5 * x) + 0.5


def _silu(x):
    return x * _sigmoid(x)


def _softplus(x):
    return jnp.maximum(x, 0.0) + jnp.log(1.0 + jnp.exp(-jnp.abs(x)))


def _mx(x):
    return x.astype(MXU_DTYPE)


def _dot(a, b):
    return jnp.dot(_mx(a), _mx(b), preferred_element_type=f32)


def _dot_nt(a, b):
    return lax.dot_general(_mx(a), _mx(b), (((1,), (1,)), ((), ())), preferred_element_type=f32)


def _dot_tn(a, b):
    return lax.dot_general(_mx(a), _mx(b), (((0,), (0,)), ((), ())), preferred_element_type=f32)


def _rms_fwd(x, w):
    r = lax.rsqrt(jnp.mean(x * x, axis=-1, keepdims=True) + EPS)
    return x * r * w


def _rms_bwd(x, w, dy):
    r = lax.rsqrt(jnp.mean(x * x, axis=-1, keepdims=True) + EPS)
    xh = x * r
    dw = jnp.sum(dy * xh, axis=0, keepdims=True)
    g = dy * w
    dx = r * (g - xh * jnp.mean(g * xh, axis=-1, keepdims=True))
    return dx, dw


def _mm_in(x, wn, w, ntb, tm, name):
    t_len, d = x.shape
    nt = w.shape[1] // TILE

    def body(x_ref, wn_ref, w_ref, u_ref, h_ref):
        @pl.when(pl.program_id(1) == 0)
        def _():
            h_ref[...] = _rms_fwd(x_ref[...], wn_ref[...]).astype(h_ref.dtype)
        h = h_ref[...]
        for t in range(ntb):
            u_ref[t] = jnp.dot(h, w_ref[:, TILE * t:TILE * (t + 1)], preferred_element_type=f32)

    return _pc(
        body, name, grid=(t_len // tm, nt // ntb),
        in_specs=[pl.BlockSpec((tm, d), lambda i, j: (i, 0)),
                  pl.BlockSpec((1, d), lambda i, j: (0, 0)),
                  pl.BlockSpec((d, ntb * TILE), lambda i, j: (0, j))],
        out_specs=[pl.BlockSpec((ntb, tm, TILE), lambda i, j: (j, i, 0)),
                   pl.BlockSpec((tm, d), lambda i, j: (i, 0))],
        out_shape=[SDS((nt, t_len, TILE), f32), SDS((t_len, d), MXU_DTYPE)],
        compiler_params=_params(("arbitrary", "arbitrary")),
    )(x, wn, w)


def _mm_dout(y, dxn, wn, w, ntb, tm, name):
    t_len, d = y.shape
    nt = w.shape[0] // TILE

    def body(y_ref, dxn_ref, wn_ref, w_ref, da_ref, dy_ref, dwn_ref):
        i, j = pl.program_id(0), pl.program_id(1)

        @pl.when((i == 0) & (j == 0))
        def _():
            dwn_ref[...] = jnp.zeros_like(dwn_ref)

        @pl.when(j == 0)
        def _():
            dy, dw = _rms_bwd(y_ref[...], wn_ref[...], dxn_ref[...])
            dy_ref[...] = dy.astype(dy_ref.dtype)
            dwn_ref[...] += dw
        dy = dy_ref[...]
        for t in range(ntb):
            da_ref[t] = _dot_nt(dy, w_ref[TILE * t:TILE * (t + 1), :])

    return _pc(
        body, name, grid=(t_len // tm, nt // ntb),
        in_specs=[pl.BlockSpec((tm, d), lambda i, j: (i, 0)),
                  pl.BlockSpec((tm, d), lambda i, j: (i, 0)),
                  pl.BlockSpec((1, d), lambda i, j: (0, 0)),
                  pl.BlockSpec((ntb * TILE, d), lambda i, j: (j, 0))],
        out_specs=[pl.BlockSpec((ntb, tm, TILE), lambda i, j: (j, i, 0)),
                   pl.BlockSpec((tm, d), lambda i, j: (i, 0)),
                   pl.BlockSpec((1, d), lambda i, j: (0, 0))],
        out_shape=[SDS((nt, t_len, TILE), f32), SDS((t_len, d), MXU_DTYPE), SDS((1, d), f32)],
        compiler_params=_params(("arbitrary", "arbitrary")),
    )(y, dxn, wn, w)


def _mm_out(a, w, x, wn, ktb, tm, name):
    kt, t_len, _ = a.shape
    d = w.shape[1]
    nk = kt // ktb

    def body(a_ref, w_ref, x_ref, wn_ref, y_ref, xn_ref, acc):
        k = pl.program_id(1)

        @pl.when(k == 0)
        def _():
            acc[...] = jnp.zeros_like(acc)
        s = acc[...]
        for t in range(ktb):
            s = s + jnp.dot(a_ref[t], w_ref[TILE * t:TILE * (t + 1), :], preferred_element_type=f32)
        acc[...] = s

        @pl.when(k == nk - 1)
        def _():
            y = acc[...]
            y_ref[...] = y
            xn_ref[...] = x_ref[...] + _rms_fwd(y, wn_ref[...])

    return _pc(
        body, name, grid=(t_len // tm, nk),
        in_specs=[pl.BlockSpec((ktb, tm, TILE), lambda i, k: (k, i, 0)),
                  pl.BlockSpec((ktb * TILE, d), lambda i, k: (k, 0)),
                  pl.BlockSpec((tm, d), lambda i, k: (i, 0)),
                  pl.BlockSpec((1, d), lambda i, k: (0, 0))],
        out_specs=[pl.BlockSpec((tm, d), lambda i, k: (i, 0)),
                   pl.BlockSpec((tm, d), lambda i, k: (i, 0))],
        out_shape=[SDS((t_len, d), f32), SDS((t_len, d), f32)],
        scratch_shapes=[pltpu.VMEM((tm, d), f32)],
        compiler_params=_params(("arbitrary", "arbitrary")),
    )(a, w, x, wn)


def _mm_dh(du, w, x, dxn, wn, ktb, tm, name):
    kt, t_len, _ = du.shape
    d = w.shape[0]
    nk = kt // ktb

    def body(du_ref, w_ref, x_ref, dxn_ref, wn_ref, dx_ref, dwn_ref, acc):
        i, k = pl.program_id(0), pl.program_id(1)

        @pl.when((i == 0) & (k == 0))
        def _():
            dwn_ref[...] = jnp.zeros_like(dwn_ref)

        @pl.when(k == 0)
        def _():
            acc[...] = jnp.zeros_like(acc)
        s = acc[...]
        for t in range(ktb):
            s = s + _dot_nt(du_ref[t], w_ref[:, TILE * t:TILE * (t + 1)])
        acc[...] = s

        @pl.when(k == nk - 1)
        def _():
            dxp, dw = _rms_bwd(x_ref[...], wn_ref[...], acc[...])
            dx_ref[...] = dxn_ref[...] + dxp
            dwn_ref[...] += dw

    return _pc(
        body, name, grid=(t_len // tm, nk),
        in_specs=[pl.BlockSpec((ktb, tm, TILE), lambda i, k: (k, i, 0)),
                  pl.BlockSpec((d, ktb * TILE), lambda i, k: (0, k)),
                  pl.BlockSpec((tm, d), lambda i, k: (i, 0)),
                  pl.BlockSpec((tm, d), lambda i, k: (i, 0)),
                  pl.BlockSpec((1, d), lambda i, k: (0, 0))],
        out_specs=[pl.BlockSpec((tm, d), lambda i, k: (i, 0)),
                   pl.BlockSpec((1, d), lambda i, k: (0, 0))],
        out_shape=[SDS((t_len, d), f32), SDS((1, d), f32)],
        scratch_shapes=[pltpu.VMEM((tm, d), f32)],
        compiler_params=_params(("arbitrary", "arbitrary")),
    )(du, w, x, dxn, wn)


def _dw_cols(a, b, ntb, tk, name):
    t_len, kdim = a.shape
    nt = b.shape[0]

    def body(a_ref, b_ref, o_ref):
        @pl.when(pl.program_id(1) == 0)
        def _():
            o_ref[...] = jnp.zeros_like(o_ref)
        av = a_ref[...]
        for s in range(ntb):
            o_ref[:, TILE * s:TILE * (s + 1)] += _dot_tn(av, b_ref[s])

    return _pc(
        body, name, grid=(nt // ntb, t_len // tk),
        in_specs=[pl.BlockSpec((tk, kdim), lambda j, t: (t, 0)),
                  pl.BlockSpec((ntb, tk, TILE), lambda j, t: (j, t, 0))],
        out_specs=pl.BlockSpec((kdim, ntb * TILE), lambda j, t: (0, j)),
        out_shape=SDS((kdim, nt * TILE), f32),
        compiler_params=_params(("arbitrary", "arbitrary")),
    )(a, b)


def _dw_rows(a, b, ktb, tk, name):
    kt, t_len, _ = a.shape
    d = b.shape[1]

    def body(a_ref, b_ref, o_ref):
        @pl.when(pl.program_id(1) == 0)
        def _():
            o_ref[...] = jnp.zeros_like(o_ref)
        bv = b_ref[...]
        for s in range(ktb):
            o_ref[TILE * s:TILE * (s + 1), :] += _dot_tn(a_ref[s], bv)

    return _pc(
        body, name, grid=(kt // ktb, t_len // tk),
        in_specs=[pl.BlockSpec((ktb, tk, TILE), lambda k, t: (k, t, 0)),
                  pl.BlockSpec((tk, d), lambda k, t: (t, 0))],
        out_specs=pl.BlockSpec((ktb * TILE, d), lambda k, t: (k, 0)),
        out_shape=SDS((kt * TILE, d), f32),
        compiler_params=_params(("arbitrary", "arbitrary")),
    )(a, b)


def _loss_grad(x, tgt, tm):
    t_len, d = x.shape

    def body(x_ref, t_ref, dx_ref, l_ref):
        @pl.when(pl.program_id(0) == 0)
        def _():
            l_ref[...] = jnp.zeros_like(l_ref)
        e = x_ref[...] - t_ref[...]
        dx_ref[...] = e * (1.0 / d)
        row = jnp.mean(e * e, axis=-1, keepdims=True)
        l_ref[...] += 0.5 * jnp.sum(row, axis=0, keepdims=True)

    return _pc(
        body, "loss_grad", grid=(t_len // tm,),
        in_specs=[pl.BlockSpec((tm, d), lambda i: (i, 0)), pl.BlockSpec((tm, d), lambda i: (i, 0))],
        out_specs=[pl.BlockSpec((tm, d), lambda i: (i, 0)), pl.BlockSpec((1, 128), lambda i: (0, 0))],
        out_shape=[SDS((t_len, d), f32), SDS((1, 128), f32)],
        compiler_params=_params(("arbitrary",)),
    )(x, tgt)


def _tri(lower):
    r = lax.broadcasted_iota(jnp.int32, (CHUNK, CHUNK), 0)
    c = lax.broadcasted_iota(jnp.int32, (CHUNK, CHUNK), 1)
    return ((c <= r) if lower else (c >= r)).astype(f32)


def _split(x, n):
    parts = []
    for _ in range(n):
        p = x.astype(jnp.bfloat16)
        parts.append(p)
        x = x - p.astype(f32)
    return parts


def _dot_exact(a, b, dims, split_a, n=3):
    out = None
    if split_a:
        b = b.astype(jnp.bfloat16)
        for p in _split(a, n):
            t = lax.dot_general(p, b, (dims, ((), ())), preferred_element_type=f32)
            out = t if out is None else out + t
    else:
        a = a.astype(jnp.bfloat16)
        for p in _split(b, n):
            t = lax.dot_general(a, p, (dims, ((), ())), preferred_element_type=f32)
            out = t if out is None else out + t
    return out


def _dt_path(dt_raw, dtb, alog):
    dtr = dt_raw + dtb
    dt = _softplus(dtr)
    a_neg = -jnp.exp(alog)
    a = dt * a_neg
    acs = _dot_exact(_tri(True), a, ((1,), (0,)), False)
    acs_t = _dot_exact(a, _tri(False), ((0,), (0,)), True)
    return dtr, dt, a_neg, acs, acs_t


CONV_ROWS = 1024
CONV_SUB = 32


CONV_NTB = 4


def _mm_in_conv(x, wn, w, cw, cb, tm, name):
    t_len, d = x.shape
    nt = w.shape[1] // TILE

    def body(x_ref, wn_ref, w_ref, cw_ref, cb_ref, u_ref, o_ref, ds_ref, h_s, carry_s, win_s):
        i, j = pl.program_id(0), pl.program_id(1)

        @pl.when(j == 0)
        def _():
            h_s[...] = _rms_fwd(x_ref[...], wn_ref[...]).astype(h_s.dtype)
        h = h_s[...]
        for t in range(CONV_NTB):
            p = CONV_NTB * j + t
            ut = jnp.dot(h, w_ref[:, TILE * t:TILE * (t + 1)], preferred_element_type=f32)
            u_ref[t] = ut
            win_s[t, 0:8, :] = jnp.where(i > 0, carry_s[p], 0.0)
            win_s[t, 8:8 + tm, :] = ut
            carry_s[p] = ut[tm - 8:tm, :]
            wk = [cw_ref[t, k:k + 1, :] for k in range(4)]
            b = cb_ref[t]
            for s in range(tm // CONV_SUB):
                o = CONV_SUB * s
                acc = b
                for k in range(4):
                    acc = acc + wk[k] * win_s[t, 5 + k + o:5 + k + o + CONV_SUB, :]
                sg = _sigmoid(acc)
                o_ref[t, o:o + CONV_SUB, :] = acc * sg
                ds_ref[t, o:o + CONV_SUB, :] = sg * (1.0 + acc * (1.0 - sg))

    return _pc(
        body, name, grid=(t_len // tm, nt // CONV_NTB),
        in_specs=[pl.BlockSpec((tm, d), lambda i, j: (i, 0)),
                  pl.BlockSpec((1, d), lambda i, j: (0, 0)),
                  pl.BlockSpec((d, CONV_NTB * TILE), lambda i, j: (0, j)),
                  pl.BlockSpec((CONV_NTB, 4, TILE), lambda i, j: (j, 0, 0)),
                  pl.BlockSpec((CONV_NTB, 1, TILE), lambda i, j: (j, 0, 0))],
        out_specs=[pl.BlockSpec((CONV_NTB, tm, TILE), lambda i, j: (j, i, 0))] * 3,
        out_shape=[SDS((nt, t_len, TILE), f32)] * 3,
        scratch_shapes=[pltpu.VMEM((tm, d), MXU_DTYPE), pltpu.VMEM((nt, 8, TILE), f32),
                        pltpu.VMEM((CONV_NTB, 8 + tm, TILE), f32)],
        compiler_params=_params(("arbitrary", "arbitrary")),
    )(x, wn, w, cw, cb)


def _conv_bwd(u, dsl, du, cw, name):
    t_len = u.shape[1]
    rows = min(CONV_ROWS, t_len)
    nb = t_len // rows

    def body(u_ref, ds_ref, cw_ref, d_ref, o_ref, dcw_ref, dcb_ref, carry_s, dp_s):
        i = pl.program_id(1)

        @pl.when(i == 0)
        def _():
            carry_s[...] = jnp.zeros_like(carry_s)
            dcw_ref[...] = jnp.zeros_like(dcw_ref)
            dcb_ref[...] = jnp.zeros_like(dcb_ref)
        w = [cw_ref[0, k:k + 1, :] for k in range(4)]

        def fold(v):
            return jnp.sum(v.reshape(CONV_SUB // 8, 8, TILE), axis=0)

        dw = [jnp.zeros((8, TILE), f32)] * 4
        db = jnp.zeros((8, TILE), f32)
        for s in range(rows // CONV_SUB):
            o = CONV_SUB * s
            dpre = d_ref[0, o:o + CONV_SUB, :] * ds_ref[0, o:o + CONV_SUB, :]
            dp_s[o:o + CONV_SUB, :] = dpre
            db = db + fold(dpre)
        dp_s[rows:rows + 8, :] = carry_s[...]
        for s in range(rows // CONV_SUB):
            o = CONV_SUB * s
            xs = u_ref[0, o:o + CONV_SUB, :]
            acc = jnp.zeros((CONV_SUB, TILE), f32)
            for k in range(4):
                win = dp_s[3 - k + o:3 - k + o + CONV_SUB, :]
                acc = acc + w[k] * win
                dw[k] = dw[k] + fold(win * xs)
            o_ref[0, o:o + CONV_SUB, :] = acc
        carry_s[...] = dp_s[0:8, :]
        for k in range(4):
            dcw_ref[0, k:k + 1, :] += jnp.sum(dw[k], axis=0, keepdims=True)
        dcb_ref[0] += jnp.sum(db, axis=0, keepdims=True)

    def blk(off):
        return pl.BlockSpec((1, rows, TILE), lambda p, i: (off + p, nb - 1 - i, 0))

    return _pc(
        body, name, grid=(16, nb),
        in_specs=[blk(0), blk(0), pl.BlockSpec((1, 4, TILE), lambda p, i: (p, 0, 0)), blk(8)],
        out_specs=[blk(8),
                   pl.BlockSpec((1, 4, TILE), lambda p, i: (p, 0, 0)),
                   pl.BlockSpec((1, 1, TILE), lambda p, i: (p, 0, 0))],
        out_shape=[SDS(du.shape, f32), SDS((16, 4, TILE), f32), SDS((16, 1, TILE), f32)],
        input_output_aliases={3: 0},
        scratch_shapes=[pltpu.VMEM((8, TILE), f32), pltpu.VMEM((rows + 8, TILE), f32)],
        compiler_params=_params(("arbitrary", "arbitrary")),
    )(u, dsl, cw, du)


def _collapse_matrix(g):
    r = lax.broadcasted_iota(jnp.int32, (SSM_HEADS, TILE), 0)
    c = lax.broadcasted_iota(jnp.int32, (SSM_HEADS, TILE), 1)
    return ((c // SSM_P) + 4 * g == r).astype(jnp.bfloat16)


def _ssd_prelude(dt_raw, dtb, alog, dsk, colx_s, scx_s):
    dtr, dt, a_neg, acs, acs_t = _dt_path(dt_raw, dtb, alog)
    a_end = acs[CHUNK - 1:CHUNK, :]
    lane = lax.broadcasted_iota(jnp.int32, (1, 2 * SSM_P), 1)
    lane4 = lax.broadcasted_iota(jnp.int32, (1, TILE), 1)
    sub8 = lax.broadcasted_iota(jnp.int32, (8, 1), 0)

    def row4(v, g):
        e = [v[:, 4 * g + r:4 * g + r + 1] for r in range(4)]
        return jnp.where(lane4 < 64, e[0], jnp.where(lane4 < 128, e[1], jnp.where(lane4 < 192, e[2], e[3])))

    for g in range(SSM_GROUPS):
        for k, arr in enumerate((dt, acs)):
            for half in range(2):
                h0 = 4 * g + 2 * half
                colx_s[k, g, :, 128 * half:128 * (half + 1)] = jnp.where(
                    lane < SSM_P, arr[:, h0:h0 + 1], arr[:, h0 + 1:h0 + 2])
        scx_s[g] = jnp.where(sub8 == 0, row4(dsk, g), jnp.where(sub8 == 1, row4(a_end, g), 0.0))
    return dtr, dt, a_neg, acs_t


def _ssm_core_specs(nc, rev):
    def cidx(c):
        return (nc - 1 - c) if rev else c
    return [
        pl.BlockSpec((SSM_KT, CHUNK, TILE), lambda c: (0, cidx(c), 0)),
        pl.BlockSpec((1, CHUNK, TILE), lambda c: (SSM_KT, cidx(c), 0)),
        pl.BlockSpec((16, CHUNK, TILE), lambda c: (0, cidx(c), 0)),
        pl.BlockSpec((1, SSM_HEADS), lambda c: (0, 0)),
        pl.BlockSpec((1, SSM_HEADS), lambda c: (0, 0)),
        pl.BlockSpec((1, SSM_HEADS), lambda c: (0, 0)),
        pl.BlockSpec((SSM_KT, 1, TILE), lambda c: (0, 0, 0)),
        pl.BlockSpec((2, SSM_GROUPS, CHUNK, TILE), lambda c: (0, 0, cidx(c), 0)),
        pl.BlockSpec((1, SSM_GROUPS, 8, TILE), lambda c: (cidx(c), 0, 0, 0)),
        pl.BlockSpec((1, SSM_HEADS, CHUNK), lambda c: (cidx(c), 0, 0)),
    ]


PRE_CHUNKS = 4


def _ssm_pre(u, dtb, alog, dsk, name):
    t_len = u.shape[1]
    nc = t_len // CHUNK
    per = min(PRE_CHUNKS, nc)

    def body(dt_ref, dtb_ref, alog_ref, dsk_ref, colx_ref, scx_ref, acst_ref):
        for c in range(per):
            rows = pl.ds(CHUNK * c, CHUNK)
            _, _, _, acs_t = _ssd_prelude(dt_ref[0, rows, 0:SSM_HEADS], dtb_ref[...], alog_ref[...], dsk_ref[...],
                                          colx_ref.at[:, :, rows, :], scx_ref.at[c])
            acst_ref[c] = acs_t

    vec = pl.BlockSpec((1, SSM_HEADS), lambda i: (0, 0))
    return _pc(
        body, name, grid=(nc // per,),
        in_specs=[pl.BlockSpec((1, per * CHUNK, TILE), lambda i: (SSM_KT, i, 0)), vec, vec, vec],
        out_specs=[pl.BlockSpec((2, SSM_GROUPS, per * CHUNK, TILE), lambda i: (0, 0, i, 0)),
                   pl.BlockSpec((per, SSM_GROUPS, 8, TILE), lambda i: (i, 0, 0, 0)),
                   pl.BlockSpec((per, SSM_HEADS, CHUNK), lambda i: (i, 0, 0))],
        out_shape=[SDS((2, SSM_GROUPS, t_len, TILE), f32), SDS((nc, SSM_GROUPS, 8, TILE), f32),
                   SDS((nc, SSM_HEADS, CHUNK), f32)],
        compiler_params=_params(("arbitrary",)),
    )(u, dtb, alog, dsk)


def _stack_cols_rows(acx, rows):
    ac = jnp.concatenate([acx[:, SSM_P * r:SSM_P * r + 1] for r in range(4)], axis=0)
    ar = jnp.concatenate([jnp.broadcast_to(rows[r:r + 1, :], (CHUNK, CHUNK)) for r in range(4)], axis=0)
    return ac, ar


def _ssm2_fwd(u, xbc, pre, dtb, alog, dsk, gn, name):
    t_len = u.shape[1]
    nc = t_len // CHUNK

    def body(z_ref, dt_ref, x_ref, dtb_ref, alog_ref, dsk_ref, gn_ref, colx_ref, scx_ref, acst_ref,
             a3_ref, yp_ref, hst_ref, h_s, xt_s, yd_s):
        c = pl.program_id(0)

        @pl.when(c == 0)
        def _():
            h_s[...] = jnp.zeros_like(h_s)
        acs_t = acst_ref[0]
        causal = _tri4()

        def group(g, s1):
            xs = x_ref[g]
            bm, cm = x_ref[8 + g, :, 0:SSM_N], x_ref[8 + g, :, SSM_N:2 * SSM_N]
            cb = _dot_nt(cm, bm)
            sc = scx_ref[0, g]
            a_end = sc[1:2, :]
            rows = pltpu.roll(acs_t, (SSM_HEADS - 4 * g) % SSM_HEADS, 0)
            hp = h_s[g]
            xt = xs * colx_ref[0, g]
            xt_s[...] = xt
            acx = colx_ref[1, g]
            ac_st, ar_st = _stack_cols_rows(acx, rows)
            m = jnp.concatenate([cb] * 4, axis=0) * jnp.exp(jnp.where(causal, ac_st - ar_st, -jnp.inf))
            for r in range(4):
                hd = slice(SSM_P * r, SSM_P * (r + 1))
                yd_s[:, hd] = _dot(m[CHUNK * r:CHUNK * (r + 1), :], xt_s[:, hd])
            yp_ref[g] = yd_s[...] + _dot(cm, hp) * jnp.exp(acx) + sc[0:1, :] * xs
            hst_ref[0, g] = hp
            h_s[g] = hp * jnp.exp(a_end) + _dot_tn(bm, xt * jnp.exp(a_end - acx))
            y2 = yp_ref[g] * _silu(z_ref[g])
            return s1 + jnp.sum(y2 * y2, axis=1, keepdims=True)

        s1 = lax.fori_loop(0, SSM_GROUPS // 2, lambda i, c: group(2 * i + 1, group(2 * i, c)),
                           jnp.zeros((CHUNK, 1), f32))
        rinv = lax.rsqrt(s1 * (1.0 / SSM_INNER) + EPS)

        def gate(g, carry):
            y2 = yp_ref[g] * _silu(z_ref[g])
            a3_ref[g] = (y2 * rinv * gn_ref[g]).astype(a3_ref.dtype)
            return carry

        lax.fori_loop(0, SSM_GROUPS, gate, 0)

    return _pc(
        body, name, grid=(nc,),
        in_specs=_ssm_core_specs(nc, False),
        out_specs=[pl.BlockSpec((SSM_KT, CHUNK, TILE), lambda c: (0, c, 0)),
                   pl.BlockSpec((SSM_KT, CHUNK, TILE), lambda c: (0, c, 0)),
                   pl.BlockSpec((1, SSM_GROUPS, SSM_N, TILE), lambda c: (c, 0, 0, 0))],
        out_shape=[SDS((SSM_KT, t_len, TILE), MXU_DTYPE), SDS((SSM_KT, t_len, TILE), f32),
                   SDS((nc, SSM_GROUPS, SSM_N, TILE), f32)],
        scratch_shapes=[pltpu.VMEM((SSM_GROUPS, SSM_N, TILE), f32), pltpu.VMEM((CHUNK, TILE), f32),
                        pltpu.VMEM((CHUNK, TILE), f32)],
        compiler_params=_params(("arbitrary",)),
    )(u, u, xbc, dtb, alog, dsk, gn, *pre)


def _ssm2_bwd(u, xbc, pre, yp, hst, da3, dtb, alog, dsk, gn, name):
    t_len = u.shape[1]
    nc = t_len // CHUNK

    def body(z_ref, dt_ref, x_ref, dtb_ref, alog_ref, dsk_ref, gn_ref, colx_s, scx_ref, acst_ref,
             yp_ref, hst_ref, da3_ref, du_ref, ddtb_ref, dalog_ref, ddsk_ref, dgn_ref,
             dh_s, xt_s, dy_s, dxt_s, ddtx_s, dacx_s, ddx_s, drow_s, dm_s, dmt_s):
        step = pl.program_id(0)

        @pl.when(step == 0)
        def _():
            dh_s[...] = jnp.zeros_like(dh_s)
            ddtb_ref[...] = jnp.zeros_like(ddtb_ref)
            dalog_ref[...] = jnp.zeros_like(dalog_ref)
            ddsk_ref[...] = jnp.zeros_like(ddsk_ref)
            dgn_ref[...] = jnp.zeros_like(dgn_ref)
        dtr = dt_ref[0, :, 0:SSM_HEADS] + dtb_ref[...]
        dt = _softplus(dtr)
        a_neg = -jnp.exp(alog_ref[...])
        acs_t = acst_ref[0]
        causal = _tri4()
        causal_t = (lax.broadcasted_iota(jnp.int32, (ATT_ROWS, CHUNK), 1)
                    >= lax.broadcasted_iota(jnp.int32, (ATT_ROWS, CHUNK), 0) % CHUNK)
        last = (lax.broadcasted_iota(jnp.int32, (1, CHUNK), 1) == CHUNK - 1).astype(f32)
        lane = lax.broadcasted_iota(jnp.int32, (1, TILE), 1)
        sub32 = lax.broadcasted_iota(jnp.int32, (SSM_HEADS, 1), 0)
        drow_s[...] = jnp.zeros_like(drow_s)

        def sums(g, carry):
            s1, s2 = carry
            y2 = yp_ref[g] * _silu(z_ref[g])
            g3 = da3_ref[g] * gn_ref[g]
            return (s1 + jnp.sum(y2 * y2, axis=1, keepdims=True), s2 + jnp.sum(g3 * y2, axis=1, keepdims=True))

        zcol = jnp.zeros((CHUNK, 1), f32)
        carry = (zcol, zcol)
        for g in range(SSM_GROUPS):
            carry = sums(g, carry)
        s1, s2 = carry
        rinv = lax.rsqrt(s1 * (1.0 / SSM_INNER) + EPS)
        m2 = s2 * rinv * rinv * rinv * (1.0 / SSM_INNER)

        def group(g, carry):
            z = z_ref[g]
            sg = _sigmoid(z)
            sz = z * sg
            y = yp_ref[g]
            y2 = y * sz
            da3 = da3_ref[g]
            dgn_ref[g] += jnp.sum(da3 * y2 * rinv, axis=0, keepdims=True)
            dy2 = rinv * (da3 * gn_ref[g]) - y2 * m2
            dy = dy2 * sz
            dy_s[...] = dy
            du_ref[g] = dy2 * y * (sg * (1.0 + z * (1.0 - sg)))
            xs = x_ref[g]
            bm, cm = x_ref[8 + g, :, 0:SSM_N], x_ref[8 + g, :, SSM_N:2 * SSM_N]
            cb = _dot_nt(cm, bm)
            cbt = _dot_nt(bm, cm)
            dtx, acx = colx_s[0, g], colx_s[1, g]
            sc = scx_ref[0, g]
            a_end = sc[1:2, :]
            ex = jnp.exp(acx)
            wdx = jnp.exp(a_end - acx)
            eend = jnp.exp(a_end)
            rows = pltpu.roll(acs_t, (SSM_HEADS - 4 * g) % SSM_HEADS, 0)
            hp = hst_ref[0, g]
            dhn = dh_s[g]
            xt = xs * dtx
            xt_s[...] = xt
            ch = _dot(cm, hp)
            gy = dy * ex
            dcm = _dot_nt(gy, hp)
            dh_s[g] = _dot_tn(cm, gy) + dhn * eend
            q = _dot(bm, dhn)
            dbm = _dot_nt(xt * wdx, dhn)
            qx = q * xt * wdx
            v_end = jnp.sum(dhn * hp, axis=0, keepdims=True) * eend + jnp.sum(qx, axis=0, keepdims=True)
            ac_st, ar_st = _stack_cols_rows(acx, rows)
            lam = jnp.exp(jnp.where(causal, ac_st - ar_st, -jnp.inf))
            lam_t = jnp.exp(jnp.where(causal_t, ar_st - ac_st, -jnp.inf))
            m = jnp.concatenate([cb] * 4, axis=0) * lam
            m_t = jnp.concatenate([cbt] * 4, axis=0) * lam_t
            for r in range(4):
                hd = slice(SSM_P * r, SSM_P * (r + 1))
                rs = slice(CHUNK * r, CHUNK * (r + 1))
                dm_s[rs, :] = _dot_nt(dy_s[:, hd], xt_s[:, hd])
                dmt_s[rs, :] = _dot_nt(xt_s[:, hd], dy_s[:, hd])
                dxt_s[:, hd] = _dot(m_t[rs, :], dy_s[:, hd])
            dm = dm_s[...]
            dl = dm * lam
            dseg = dm * m
            dseg_t = dmt_s[...] * m_t
            dcb = dl[0:CHUNK] + dl[CHUNK:2 * CHUNK] + dl[2 * CHUNK:3 * CHUNK] + dl[3 * CHUNK:4 * CHUNK]
            drows = jnp.zeros((SSM_HEADS, CHUNK), f32)
            for r in range(4):
                rs = slice(CHUNK * r, CHUNK * (r + 1))
                in_head = (lane >= SSM_P * r) & (lane < SSM_P * (r + 1))
                d_ac = jnp.sum(dseg_t[rs, :], axis=0, keepdims=True)
                d_ar = jnp.sum(dseg[rs, :], axis=0, keepdims=True)
                d_aend = jnp.sum(jnp.where(in_head, v_end, 0.0), axis=1, keepdims=True)
                drows = drows + jnp.where(sub32 == r, d_ac - d_ar + last * d_aend, 0.0)
            dxt = dxt_s[...] + q * wdx
            du_ref[8 + g] = sc[0:1, :] * dy + dxt * dtx
            ddtx_s[g] = dxt * xs
            dacx_s[g] = dy * ch * ex - qx
            ddx_s[g] = jnp.broadcast_to(jnp.sum(dy * xs, axis=0, keepdims=True), (8, TILE))
            du_ref[16 + g, :, 0:SSM_N] = dbm + _dot_tn(dcb, cm)
            du_ref[16 + g, :, SSM_N:2 * SSM_N] = dcm + _dot(dcb, bm)
            drow_s[...] += pltpu.roll(drows, (4 * g) % SSM_HEADS, 0)
            return carry

        lax.fori_loop(0, SSM_GROUPS, group, 0, unroll=8)
        ddt = jnp.zeros((CHUNK, SSM_HEADS), f32)
        dacs = jnp.zeros((CHUNK, SSM_HEADS), f32)
        ddsk = jnp.zeros((8, SSM_HEADS), f32)
        for g in range(SSM_GROUPS):
            col_g = _collapse_matrix(g)
            ddt = ddt + _dot_exact(ddtx_s[g], col_g, ((1,), (1,)), True, 2)
            dacs = dacs + _dot_exact(dacx_s[g], col_g, ((1,), (1,)), True, 2)
            ddsk = ddsk + _dot_exact(ddx_s[g], col_g, ((1,), (1,)), True, 2)
        upper = _tri(False)
        da = _dot_exact(upper, dacs, ((1,), (0,)), False) + _dot_exact(upper, drow_s[...], ((1,), (1,)), False)
        ddt = ddt + da * a_neg
        dalog_ref[...] += jnp.sum(da * dt, axis=0, keepdims=True) * a_neg
        ddtr = ddt * _sigmoid(dtr)
        ddtb_ref[...] += jnp.sum(ddtr, axis=0, keepdims=True)
        ddsk_ref[...] += ddsk[0:1, :]
        du_ref[SSM_NT - 1] = jnp.zeros((CHUNK, TILE), f32)
        du_ref[SSM_NT - 1, :, 0:SSM_HEADS] = ddtr

    def rc(c):
        return nc - 1 - c

    vec = pl.BlockSpec((1, SSM_HEADS), lambda c: (0, 0))
    return _pc(
        body, name, grid=(nc,),
        in_specs=_ssm_core_specs(nc, True) + [
            pl.BlockSpec((SSM_KT, CHUNK, TILE), lambda c: (0, rc(c), 0)),
            pl.BlockSpec((1, SSM_GROUPS, SSM_N, TILE), lambda c: (rc(c), 0, 0, 0)),
            pl.BlockSpec((SSM_KT, CHUNK, TILE), lambda c: (0, rc(c), 0))],
        out_specs=[pl.BlockSpec((SSM_NT, CHUNK, TILE), lambda c: (0, rc(c), 0)), vec, vec, vec,
                   pl.BlockSpec((SSM_KT, 1, TILE), lambda c: (0, 0, 0))],
        out_shape=[SDS((SSM_NT, t_len, TILE), f32), SDS((1, SSM_HEADS), f32), SDS((1, SSM_HEADS), f32),
                   SDS((1, SSM_HEADS), f32), SDS((SSM_KT, 1, TILE), f32)],
        scratch_shapes=[pltpu.VMEM((SSM_GROUPS, SSM_N, TILE), f32), pltpu.VMEM((CHUNK, TILE), f32),
                        pltpu.VMEM((CHUNK, TILE), f32), pltpu.VMEM((CHUNK, TILE), f32),
                        pltpu.VMEM((SSM_GROUPS, CHUNK, TILE), f32), pltpu.VMEM((SSM_GROUPS, CHUNK, TILE), f32),
                        pltpu.VMEM((SSM_GROUPS, 8, TILE), f32), pltpu.VMEM((SSM_HEADS, CHUNK), f32),
                        pltpu.VMEM((4 * CHUNK, CHUNK), f32), pltpu.VMEM((4 * CHUNK, CHUNK), f32)],
        compiler_params=_params(("arbitrary",)),
    )(u, u, xbc, dtb, alog, dsk, gn, *pre, yp, hst, da3)


def _swap16(t):
    lane = lax.broadcasted_iota(jnp.int32, t.shape, 1) % 64
    return jnp.where(lane < 8, pltpu.roll(t, TILE - 8, 1), jnp.where(lane < 16, pltpu.roll(t, 8, 1), 0.0))


def _rope(t, cos_t, sin_t):
    return t * cos_t + _swap16(t) * sin_t


def _rope_bwd(g, cos_t, sin_t):
    return g * cos_t + _swap16(g * sin_t)


def _att_in_specs(nb, rev):
    def bidx(n):
        return (nb - 1 - n) if rev else n
    return [
        pl.BlockSpec((ATT_NT, CHUNK, TILE), lambda n: (0, bidx(n), 0)),
        pl.BlockSpec((2, CHUNK, TILE), lambda n: (2, jnp.maximum(bidx(n) - 1, 0), 0)),
        pl.BlockSpec((CHUNK, TILE), lambda n: (bidx(n), 0)),
        pl.BlockSpec((CHUNK, TILE), lambda n: (bidx(n), 0)),
        pl.BlockSpec((CHUNK, TILE), lambda n: (jnp.maximum(bidx(n) - 1, 0), 0)),
        pl.BlockSpec((CHUNK, TILE), lambda n: (jnp.maximum(bidx(n) - 1, 0), 0)),
        pl.BlockSpec((1, 16), lambda n: (0, 0)),
    ]


ATT_SCALE = 0.125


def _tri4():
    row = lax.broadcasted_iota(jnp.int32, (ATT_ROWS, CHUNK), 0) % CHUNK
    col = lax.broadcasted_iota(jnp.int32, (ATT_ROWS, CHUNK), 1)
    return col <= row


def _stack_heads(ref):
    return jnp.concatenate([ref[:, 64 * r:64 * (r + 1)] for r in range(4)], axis=0)


def _sink_col(sinks, g):
    return [sinks[:, 4 * g + r:4 * g + r + 1] for r in range(4)]


def _softmax_rows(s_s, pn_s, pc_s, sink, tri, has_prev):
    sub = lax.broadcasted_iota(jnp.int32, (ATT_ROWS, 1), 0)
    sk = jnp.where(sub < CHUNK, sink[0], jnp.where(sub < 2 * CHUNK, sink[1], jnp.where(sub < 3 * CHUNK, sink[2], sink[3])))
    s = jnp.where(tri, s_s[:, CHUNK:2 * CHUNK], jnp.where(has_prev, s_s[:, 0:CHUNK], -jnp.inf)) * ATT_SCALE
    m = jnp.maximum(jnp.max(s, axis=-1, keepdims=True), sk)
    p = jnp.exp(s - m)
    e_sink = jnp.exp(sk - m)
    inv = 1.0 / (jnp.sum(p, axis=-1, keepdims=True) + e_sink)
    pn = p * inv
    pc_s[...] = pn
    pn_s[:, 0:CHUNK] = jnp.where(tri, 0.0, pn)
    pn_s[:, CHUNK:2 * CHUNK] = jnp.where(tri, pn, 0.0)
    return e_sink * inv


def _att2_fwd(u, cos_t, sin_t, sinks, name):
    t_len = u.shape[1]
    nb = t_len // CHUNK

    def body(u_ref, prev_ref, cc_ref, sc_ref, cp_ref, sp_ref, sink_ref, a_ref,
             q_s, kp_s, kc_s, vp_s, vc_s, o_s, s_s, pn_s, pc_s):
        n = pl.program_id(0)
        tri = _tri4()
        cos_c, sin_c = cc_ref[...], sc_ref[...]
        kc_s[...] = _rope(u_ref[4], cos_c, sin_c)
        kp_s[...] = _rope(prev_ref[0], cp_ref[...], sp_ref[...])
        vc_s[...] = u_ref[5]
        vp_s[...] = prev_ref[1]
        sinks = sink_ref[...]
        for g in range(4):
            q_s[g] = _rope(u_ref[g], cos_c, sin_c)
        for g in range(4):
            kv = slice(64 * g, 64 * (g + 1))
            kb = jnp.concatenate([kp_s[:, kv], kc_s[:, kv]], axis=0)
            s_s[g] = _dot_nt(_stack_heads(q_s.at[g]), kb)
        for g in range(4):
            _softmax_rows(s_s.at[g], pn_s.at[g], pc_s.at[g], _sink_col(sinks, g), tri, n > 0)
        for g in range(4):
            kv = slice(64 * g, 64 * (g + 1))
            vb = jnp.concatenate([vp_s[:, kv], vc_s[:, kv]], axis=0)
            o = _dot(pn_s[g], vb)
            for r in range(4):
                o_s[g, :, 64 * r:64 * (r + 1)] = o[CHUNK * r:CHUNK * (r + 1), :]
        for g in range(4):
            a_ref[g] = (o_s[g] * _silu(u_ref[6 + g])).astype(a_ref.dtype)

    return _pc(
        body, name, grid=(nb,),
        in_specs=_att_in_specs(nb, False),
        out_specs=pl.BlockSpec((ATT_KT, CHUNK, TILE), lambda n: (0, n, 0)),
        out_shape=SDS((ATT_KT, t_len, TILE), MXU_DTYPE),
        scratch_shapes=[pltpu.VMEM((4, CHUNK, TILE), f32)] + [pltpu.VMEM((CHUNK, TILE), f32)] * 4
                       + [pltpu.VMEM((4, CHUNK, TILE), f32)] + [pltpu.VMEM((4, ATT_ROWS, 2 * CHUNK), f32)] * 2
                       + [pltpu.VMEM((4, ATT_ROWS, CHUNK), f32)],
        compiler_params=_params(("arbitrary",)),
    )(u, u, cos_t, sin_t, cos_t, sin_t, sinks)


def _att2_bwd(u, cos_t, sin_t, sinks, da, name):
    t_len = u.shape[1]
    nb = t_len // CHUNK

    def body(u_ref, prev_ref, cc_ref, sc_ref, cp_ref, sp_ref, sink_ref, da_ref, du_ref, dsink_ref,
             ck_s, cv_s, kp_s, kc_s, vp_s, vc_s, q_s, o_s, do_s, dq_s, s_s, pn_s, dp_s, dkt_s, dvt_s, pc_s):
        step = pl.program_id(0)
        nn = nb - 1 - step

        @pl.when(step == 0)
        def _():
            ck_s[...] = jnp.zeros_like(ck_s)
            cv_s[...] = jnp.zeros_like(cv_s)
            dsink_ref[...] = jnp.zeros_like(dsink_ref)
        tri = _tri4()
        cos_c, sin_c = cc_ref[...], sc_ref[...]
        cos_p, sin_p = cp_ref[...], sp_ref[...]
        kc_s[...] = _rope(u_ref[4], cos_c, sin_c)
        kp_s[...] = _rope(prev_ref[0], cos_p, sin_p)
        vc_s[...] = u_ref[5]
        vp_s[...] = prev_ref[1]
        sinks = sink_ref[...]
        lane16 = lax.broadcasted_iota(jnp.int32, (1, 16), 1)
        dsink = jnp.zeros((1, 16), f32)
        for g in range(4):
            q_g, o_g, do_g, dq_g = q_s.at[g], o_s.at[g], do_s.at[g], dq_s.at[g]
            s_g, pn_g, dp_g, pc_g = s_s.at[g], pn_s.at[g], dp_s.at[g], pc_s.at[g]
            q_g[...] = _rope(u_ref[g], cos_c, sin_c)
            gate = u_ref[6 + g]
            sg = _sigmoid(gate)
            dav = da_ref[g]
            do_g[...] = dav * (gate * sg)
            kv = slice(64 * g, 64 * (g + 1))
            kb = jnp.concatenate([kp_s[:, kv], kc_s[:, kv]], axis=0)
            vb = jnp.concatenate([vp_s[:, kv], vc_s[:, kv]], axis=0)
            q_st = _stack_heads(q_g)
            do_st = _stack_heads(do_g)
            s_g[...] = _dot_nt(q_st, kb)
            p_sink = _softmax_rows(s_g, pn_g, pc_g, _sink_col(sinks, g), tri, nn > 0)
            o = _dot(pn_g[...], vb)
            dvt_s[64 * g:64 * (g + 1), :] = _dot_tn(do_st, pn_g[...])
            dp_g[...] = _dot_nt(do_st, vb)
            delta = jnp.sum(do_st * o, axis=-1, keepdims=True)
            dpc = jnp.where(tri, dp_g[:, CHUNK:2 * CHUNK], dp_g[:, 0:CHUNK])
            dsc = pc_g[...] * (dpc - delta) * ATT_SCALE
            dp_g[:, 0:CHUNK] = jnp.where(tri, 0.0, dsc)
            dp_g[:, CHUNK:2 * CHUNK] = jnp.where(tri, dsc, 0.0)
            sd = p_sink * delta
            for r in range(4):
                rs = slice(CHUNK * r, CHUNK * (r + 1))
                o_g[:, 64 * r:64 * (r + 1)] = o[rs, :]
                ds_h = -jnp.sum(sd[rs, :], axis=0, keepdims=True)
                dsink = dsink + ds_h * (lane16 == 4 * g + r).astype(f32)
            ds = dp_g[...]
            dq = _dot(ds, kb)
            for r in range(4):
                dq_g[:, 64 * r:64 * (r + 1)] = dq[CHUNK * r:CHUNK * (r + 1), :]
            dkt_s[64 * g:64 * (g + 1), :] = _dot_tn(q_st, ds)
            du_ref[6 + g] = dav * o_g[...] * (sg * (1.0 + gate * (1.0 - sg)))
            du_ref[g] = _rope_bwd(dq_g[...], cos_c, sin_c)
        dk = dkt_s[...].T
        dv = dvt_s[...].T
        du_ref[4] = _rope_bwd(dk[CHUNK:2 * CHUNK, :], cos_c, sin_c) + ck_s[...]
        du_ref[5] = dv[CHUNK:2 * CHUNK, :] + cv_s[...]
        ck_s[...] = _rope_bwd(dk[0:CHUNK, :], cos_p, sin_p)
        cv_s[...] = dv[0:CHUNK, :]
        dsink_ref[...] += dsink

    def rb(n):
        return nb - 1 - n

    return _pc(
        body, name, grid=(nb,),
        in_specs=_att_in_specs(nb, True) + [pl.BlockSpec((ATT_KT, CHUNK, TILE), lambda n: (0, rb(n), 0))],
        out_specs=[pl.BlockSpec((ATT_NT, CHUNK, TILE), lambda n: (0, rb(n), 0)),
                   pl.BlockSpec((1, 16), lambda n: (0, 0))],
        out_shape=[SDS((ATT_NT, t_len, TILE), f32), SDS((1, 16), f32)],
        scratch_shapes=[pltpu.VMEM((CHUNK, TILE), f32)] * 6 + [pltpu.VMEM((4, CHUNK, TILE), f32)] * 4
                       + [pltpu.VMEM((4, ATT_ROWS, 2 * CHUNK), f32)] * 3
                       + [pltpu.VMEM((2 * CHUNK, 2 * CHUNK), f32)] * 2 + [pltpu.VMEM((4, ATT_ROWS, CHUNK), f32)],
        compiler_params=_params(("arbitrary",)),
    )(u, u, cos_t, sin_t, cos_t, sin_t, sinks, da)


_HBM = pl.BlockSpec(memory_space=pltpu.HBM)


def _all_gather_big(shards):
    n = len(shards)

    def body(*refs):
        x_refs, out_refs = refs[:n], refs[n:2 * n]
        send_sems, recv_sems, local_sems = refs[2 * n:]
        x, y, c = lax.axis_index("x"), lax.axis_index("y"), lax.axis_index("c")
        me, sibling = (x, y, c), (x, y, 1 - c)
        chips = [(1 - x, y), (x, 1 - y), (1 - x, 1 - y)]

        def slot(i, px, py, pc):
            return out_refs[i].at[4 * px + 2 * py + pc]

        def copy(i, k, block, to, src=None):
            return pltpu.make_async_remote_copy(
                src_ref=slot(i, *block) if src is None else src, dst_ref=slot(i, *block),
                send_sem=send_sems.at[7 * i + k], recv_sem=recv_sems.at[7 * i + k], device_id=to, device_id_type=MESH)

        mine = [pltpu.make_async_copy(x_refs[i], slot(i, *me), local_sems.at[i]) for i in range(n)]
        for cp in mine:
            cp.start()
        first = []
        for i in range(n):
            first.append(copy(i, 0, me, sibling, src=x_refs[i]))
            first += [copy(i, 1 + j, me, (*chip, c), src=x_refs[i]) for j, chip in enumerate(chips)]
        for cp in first:
            cp.start()
        passed = []
        for j, chip in enumerate(chips):
            for i in range(n):
                copy(i, 1 + j, (*chip, c), me).wait_recv()
                fwd = copy(i, 4 + j, (*chip, c), sibling)
                fwd.start()
                passed.append(fwd)
        for i in range(n):
            copy(i, 0, sibling, me).wait_recv()
            for j, chip in enumerate(chips):
                copy(i, 4 + j, (*chip, 1 - c), me).wait_recv()
        for cp in first + passed:
            cp.wait_send()
        for cp in mine:
            cp.wait()

    return _pc(
        body, "all_gather_big",
        in_specs=[_HBM] * n, out_specs=[_HBM] * n,
        out_shape=[SDS((N_DEV,) + s.shape, s.dtype) for s in shards],
        scratch_shapes=[pltpu.SemaphoreType.DMA((7 * n,)), pltpu.SemaphoreType.DMA((7 * n,)),
                        pltpu.SemaphoreType.DMA((n,))],
    )(*shards)


def _all_gather_direct(block, name):
    rows, width = block.shape

    def body(x_ref, out_ref, send_sems, recv_sems, local_sem):
        x, y, c = lax.axis_index("x"), lax.axis_index("y"), lax.axis_index("c")
        my_slot = 4 * x + 2 * y + c

        def peer(k):
            return (1 - x if k & 4 else x, 1 - y if k & 2 else y, 1 - c if k & 1 else c)

        def copy(k):
            px, py, pc = peer(k)
            return pltpu.make_async_remote_copy(
                src_ref=x_ref, dst_ref=out_ref.at[my_slot], send_sem=send_sems.at[k - 1], recv_sem=recv_sems.at[k - 1],
                device_id=(px, py, pc), device_id_type=MESH)

        def arrival(k):
            px, py, pc = peer(k)
            return pltpu.make_async_remote_copy(
                src_ref=x_ref, dst_ref=out_ref.at[4 * px + 2 * py + pc], send_sem=send_sems.at[k - 1],
                recv_sem=recv_sems.at[k - 1], device_id=(px, py, pc), device_id_type=MESH)

        mine = pltpu.make_async_copy(x_ref, out_ref.at[my_slot], local_sem)
        mine.start()
        for k in range(1, N_DEV):
            copy(k).start()
        for k in range(1, N_DEV):
            arrival(k).wait_recv()
        for k in range(1, N_DEV):
            copy(k).wait_send()
        mine.wait()

    return _pc(
        body, name,
        in_specs=[_HBM], out_specs=_HBM,
        out_shape=SDS((N_DEV, rows, width), block.dtype),
        scratch_shapes=[pltpu.SemaphoreType.DMA((7,)), pltpu.SemaphoreType.DMA((7,)), pltpu.SemaphoreType.DMA],
    )(block)


N_CHIP = N_DEV // 2


def _exchange_sibling(gs):
    n = len(gs)

    def body(*refs):
        g_refs, out_refs = refs[:n], refs[n:2 * n]
        send_sems, recv_sems = refs[2 * n:]
        x, y, c = lax.axis_index("x"), lax.axis_index("y"), lax.axis_index("c")
        cps = [pltpu.make_async_remote_copy(
            src_ref=g_refs[i].at[2 * k + 1 - c], dst_ref=out_refs[i].at[k], send_sem=send_sems.at[N_CHIP * i + k],
            recv_sem=recv_sems.at[N_CHIP * i + k], device_id=(x, y, 1 - c), device_id_type=MESH)
            for i in range(n) for k in range(N_CHIP)]
        for cp in cps:
            cp.start()
        for cp in cps:
            cp.wait()

    return _pc(
        body, "rs_sibling",
        in_specs=[_HBM] * n, out_specs=[_HBM] * n,
        out_shape=[SDS((N_CHIP,) + g.shape[1:], g.dtype) for g in gs],
        scratch_shapes=[pltpu.SemaphoreType.DMA((N_CHIP * n,)), pltpu.SemaphoreType.DMA((N_CHIP * n,))],
    )(*gs)


def _pair_sum(g, r1, cidx, tr, name):
    _, rows, width = g.shape

    def body(c_ref, g_ref, r_ref, o_ref):
        o_ref[...] = (g_ref[...].astype(f32) + r_ref[...].astype(f32)).astype(o_ref.dtype)

    return pl.pallas_call(
        body, name=name,
        grid_spec=pltpu.PrefetchScalarGridSpec(
            num_scalar_prefetch=1, grid=(N_CHIP, rows // tr),
            in_specs=[pl.BlockSpec((1, tr, width), lambda k, i, c_ref: (2 * k + c_ref[0], i, 0)),
                      pl.BlockSpec((1, tr, width), lambda k, i, c_ref: (k, i, 0))],
            out_specs=pl.BlockSpec((1, tr, width), lambda k, i, c_ref: (k, i, 0))),
        out_shape=SDS((N_CHIP, rows, width), g.dtype),
        compiler_params=_params(("arbitrary", "arbitrary")),
    )(cidx, g, r1)


def _exchange_chips(ps):
    n = len(ps)

    def body(*refs):
        p_refs, out_refs = refs[:n], refs[n:2 * n]
        send_sems, recv_sems, local_sems = refs[2 * n:]
        x, y, c = lax.axis_index("x"), lax.axis_index("y"), lax.axis_index("c")
        my_chip = 2 * x + y
        chips = [(1 - x, y), (x, 1 - y), (1 - x, 1 - y)]

        def copy(i, j):
            px, py = chips[j]
            return pltpu.make_async_remote_copy(
                src_ref=p_refs[i].at[2 * px + py], dst_ref=out_refs[i].at[my_chip], send_sem=send_sems.at[3 * i + j],
                recv_sem=recv_sems.at[3 * i + j], device_id=(px, py, c), device_id_type=MESH)

        def arrival(i, j):
            px, py = chips[j]
            return pltpu.make_async_remote_copy(
                src_ref=p_refs[i].at[my_chip], dst_ref=out_refs[i].at[2 * px + py], send_sem=send_sems.at[3 * i + j],
                recv_sem=recv_sems.at[3 * i + j], device_id=(px, py, c), device_id_type=MESH)

        mine = [pltpu.make_async_copy(p_refs[i].at[my_chip], out_refs[i].at[my_chip], local_sems.at[i])
                for i in range(n)]
        for cp in mine:
            cp.start()
        for i in range(n):
            for j in range(3):
                copy(i, j).start()
        for i in range(n):
            for j in range(3):
                arrival(i, j).wait_recv()
        for i in range(n):
            for j in range(3):
                copy(i, j).wait_send()
        for cp in mine:
            cp.wait()

    return _pc(
        body, "rs_chips",
        in_specs=[_HBM] * n, out_specs=[_HBM] * n,
        out_shape=[SDS(p.shape, p.dtype) for p in ps],
        scratch_shapes=[pltpu.SemaphoreType.DMA((3 * n,)), pltpu.SemaphoreType.DMA((3 * n,)),
                        pltpu.SemaphoreType.DMA((n,))],
    )(*ps)


def _adamw(parts, w, m, v, tr, name):
    n, rows, width = parts.shape
    c1 = 1.0 / (1.0 - ADAM_B1 ** ADAM_STEP)
    c2 = 1.0 / (1.0 - ADAM_B2 ** ADAM_STEP)

    def body(p_ref, w_ref, m_ref, v_ref, g_ref, d_ref, mo_ref, vo_ref):
        g = p_ref[0].astype(f32)
        for k in range(1, n):
            g = g + p_ref[k].astype(f32)
        mn = ADAM_B1 * m_ref[...] + (1.0 - ADAM_B1) * g
        vn = ADAM_B2 * v_ref[...] + (1.0 - ADAM_B2) * (g * g)
        g_ref[...] = g
        mo_ref[...] = mn
        vo_ref[...] = vn
        d_ref[...] = -ADAM_LR * ((mn * c1) / (jnp.sqrt(vn * c2) + ADAM_EPS) + ADAM_WD * w_ref[...])

    blk = pl.BlockSpec((tr, width), lambda i: (i, 0))
    return _pc(
        body, name, grid=(rows // tr,),
        in_specs=[pl.BlockSpec((n, tr, width), lambda i: (0, i, 0)), blk, blk, blk],
        out_specs=[blk, blk, blk, blk],
        out_shape=[SDS((rows, width), f32)] * 4,
        compiler_params=_params(("arbitrary",)),
    )(parts, w, m, v)


ROWS_REST = ROWS_SSM_OUT + ROWS_ATT_OUT + 16


def _pack_rest(ssm_w_out, att_w_out, conv_w):
    conv = jnp.pad(conv_w.reshape(4, 1024), ((0, 12), (0, 0)))
    return jnp.concatenate([ssm_w_out.reshape(ROWS_SSM_OUT, 1024), att_w_out.reshape(ROWS_ATT_OUT, 1024), conv], axis=0)


def _unpack_rest(p):
    o = ROWS_SSM_OUT + ROWS_ATT_OUT
    return (p[0:ROWS_SSM_OUT].reshape(2, 256, 1024), p[ROWS_SSM_OUT:o].reshape(2, 128, 1024),
            p[o:o + 4].reshape(2, 4, 512))


def _pack_grads(d_ssm_w_in, d_ssm_w_out, d_att_w_in, d_att_w_out, d_conv_w):
    wire = lambda t: t.astype(MXU_DTYPE)
    a = jnp.transpose(wire(d_ssm_w_in).reshape(2, 1024, 8, 772), (2, 0, 1, 3)).reshape(8, 2048, 772)
    c = jnp.transpose(wire(d_att_w_in).reshape(2, 1024, 8, 320), (2, 0, 1, 3)).reshape(8, 2048, 320)
    b = jnp.transpose(wire(d_ssm_w_out).reshape(2, 8, 256, 1024), (1, 0, 2, 3)).reshape(8, ROWS_SSM_OUT, 1024)
    d = jnp.transpose(wire(d_att_w_out).reshape(2, 8, 128, 1024), (1, 0, 2, 3)).reshape(8, ROWS_ATT_OUT, 1024)
    e = jnp.transpose(wire(d_conv_w).reshape(2, 4, 8, 512), (2, 0, 1, 3)).reshape(8, 4, 1024)
    e = jnp.pad(e, ((0, 0), (0, 12), (0, 0)))
    return a, c, jnp.concatenate([b, d, e], axis=1)


def _pad8(a):
    return jnp.pad(a, ((0, 8 - a.shape[0]), (0, 0)))


def _pack_small(pre_norm, post_norm, conv_b, gate_norm, dt_bias, a_log, d_skip, sinks, extra=None):
    row = jnp.concatenate([dt_bias.reshape(1, 64), a_log.reshape(1, 64), d_skip.reshape(1, 64), sinks.reshape(1, 32),
                           jnp.zeros((1, 1024 - 224), f32)], axis=1)
    if extra is not None:
        row = row + jnp.pad(extra.reshape(1, 1), ((0, 0), (224, 1024 - 225)))
    return jnp.concatenate([_pad8(pre_norm.reshape(4, 1024)), _pad8(post_norm.reshape(4, 1024)),
                            conv_b.reshape(8, 1024), _pad8(gate_norm.reshape(4, 1024)), _pad8(row)], axis=0)


def _unpack_small(p):
    row = p[32]
    return (p[0:4], p[8:12], p[16:24].reshape(2, 4096), row[0:64].reshape(2, 32), row[64:128].reshape(2, 32),
            row[128:192].reshape(2, 32), p[24:28].reshape(2, 2048), row[192:224].reshape(2, 16))


def _ssm_w_in_tiles(w):
    wb = w[:, 4096:5120].reshape(1024, 8, 128)
    wc = w[:, 5120:6144].reshape(1024, 8, 128)
    wbc = jnp.concatenate([wb, wc], axis=2).reshape(1024, 2048)
    return jnp.concatenate([w[:, 0:4096], wbc, w[:, 6144:6176], jnp.zeros((1024, 224), w.dtype)], axis=1)


def _ssm_w_in_untile(dw):
    dbc = dw[:, 4096:6144].reshape(1024, 8, 256)
    return jnp.concatenate([dw[:, 0:4096], dbc[:, :, 0:128].reshape(1024, 1024), dbc[:, :, 128:256].reshape(1024, 1024),
                            dw[:, 6144:6176]], axis=1)


def _conv_tiles(cw):
    k = cw.shape[0]
    xs = jnp.transpose(cw[:, 0:2048].reshape(k, 8, 256), (1, 0, 2))
    b = cw[:, 2048:3072].reshape(k, 8, 128)
    c = cw[:, 3072:4096].reshape(k, 8, 128)
    bc = jnp.transpose(jnp.concatenate([b, c], axis=2), (1, 0, 2))
    return jnp.concatenate([xs, bc], axis=0)


def _conv_untile(t):
    k = t.shape[1]
    xs = jnp.transpose(t[0:8], (1, 0, 2)).reshape(k, 2048)
    bc = jnp.transpose(t[8:16], (1, 0, 2))
    return jnp.concatenate([xs, bc[:, :, 0:128].reshape(k, 1024), bc[:, :, 128:256].reshape(k, 1024)], axis=1)


def _rope_tables(positions):
    inv = ROPE_THETA ** (-jnp.arange(0, 16, 2, dtype=f32) / 16)
    ang = positions.astype(f32).reshape(-1, 1) * inv
    cos, sin = jnp.cos(ang), jnp.sin(ang)
    t_len = ang.shape[0]
    cos64 = jnp.concatenate([cos, cos, jnp.ones((t_len, 48), f32)], axis=1)
    sin64 = jnp.concatenate([-sin, sin, jnp.zeros((t_len, 48), f32)], axis=1)
    return jnp.tile(cos64, (1, 4)), jnp.tile(sin64, (1, 4))


def kernel(x, positions, pre_norm, post_norm, ssm_w_in, ssm_conv_w, ssm_conv_b, ssm_dt_bias, ssm_a_log, ssm_d, ssm_gate_norm, ssm_w_out, att_w_in, att_sinks, att_w_out, loss_target, m_pre_norm, m_post_norm, m_ssm_w_in, m_ssm_conv_w, m_ssm_conv_b, m_ssm_dt_bias, m_ssm_a_log, m_ssm_d, m_ssm_gate_norm, m_ssm_w_out, m_att_w_in, m_att_sinks, m_att_w_out, v_pre_norm, v_post_norm, v_ssm_w_in, v_ssm_conv_w, v_ssm_conv_b, v_ssm_dt_bias, v_ssm_a_log, v_ssm_d, v_ssm_gate_norm, v_ssm_w_out, v_att_w_in, v_att_sinks, v_att_w_out):
    t_len = x.shape[1]
    tm = min(1024, t_len)
    xin = x.reshape(t_len, D_MODEL)
    tgt = loss_target.reshape(t_len, D_MODEL)
    cidx = lax.axis_index("c").astype(jnp.int32).reshape(1)

    g_ssm_in, g_att_in, g_ssm_out, g_att_out = _all_gather_big(
        [ssm_w_in.astype(MXU_DTYPE), att_w_in.astype(MXU_DTYPE), ssm_w_out.astype(MXU_DTYPE),
         att_w_out.astype(MXU_DTYPE)])
    conv_local = jnp.concatenate([ssm_conv_w.reshape(4, 1024), jnp.zeros((4, 1024), f32)], axis=0)
    conv_all = _all_gather_direct(conv_local, "all_gather_conv")[:, 0:4]
    w_ssm_in = jnp.transpose(g_ssm_in, (1, 2, 0, 3)).reshape(2, 1024, SSM_IN)
    w_ssm_out = jnp.transpose(g_ssm_out, (1, 0, 2, 3)).reshape(2, SSM_INNER, 1024)
    w_att_in = jnp.transpose(g_att_in, (1, 2, 0, 3)).reshape(2, 1024, ATT_IN)
    w_att_out = jnp.transpose(g_att_out, (1, 0, 2, 3)).reshape(2, 1024, 1024)
    conv_w = jnp.transpose(conv_all.reshape(8, 2, 4, 512), (1, 2, 0, 3)).reshape(2, 4, 4096)
    cos_t, sin_t = _rope_tables(positions)

    saved = []
    xc = xin
    for i in range(4):
        j = i // 2
        wn_pre, wn_post = pre_norm[i].reshape(1, D_MODEL), post_norm[i].reshape(1, D_MODEL)
        if i % 2 == 0:
            w_in = _ssm_w_in_tiles(w_ssm_in[j])
            cw, cb = _conv_tiles(conv_w[j]), _conv_tiles(ssm_conv_b[j].reshape(1, 4096))
            dtb, alog, dsk = ssm_dt_bias[j].reshape(1, 32), ssm_a_log[j].reshape(1, 32), ssm_d[j].reshape(1, 32)
            gn = ssm_gate_norm[j].reshape(SSM_KT, 1, TILE)
            w_zdt = jnp.concatenate([w_in[:, 0:SSM_INNER], w_in[:, 24 * TILE:25 * TILE]], axis=1)
            u, h = _mm_in(xc, wn_pre, w_zdt, 3, tm, f"ssm_in_{j}")
            uc, xbc, dsl = _mm_in_conv(xc, wn_pre, w_in[:, SSM_INNER:24 * TILE], cw, cb, tm, f"ssm_inconv_{j}")
            pre = _ssm_pre(u, dtb, alog, dsk, f"ssm_pre_{j}")
            a3, yp, hst = _ssm2_fwd(u, xbc, pre, dtb, alog, dsk, gn, f"ssm_core_{j}")
            y, xn = _mm_out(a3, w_ssm_out[j], xc, wn_post, 4, tm, f"ssm_out_{j}")
            saved.append(dict(x=xc, u=u, uc=uc, dsl=dsl, h=h, a=a3, yp=yp, hst=hst, y=y, w_in=w_in, cw=cw, dtb=dtb,
                              alog=alog, dsk=dsk, gn=gn, xbc=xbc, pre=pre))
        else:
            sinks = att_sinks[j].reshape(1, 16)
            u, h = _mm_in(xc, wn_pre, w_att_in[j], 5, tm, f"att_in_{j}")
            a = _att2_fwd(u, cos_t, sin_t, sinks, f"att_core_{j}")
            y, xn = _mm_out(a, w_att_out[j], xc, wn_post, 4, tm, f"att_out_{j}")
            saved.append(dict(x=xc, u=u, h=h, a=a, y=y, sinks=sinks))
        xc = xn

    dx, loss_part = _loss_grad(xc, tgt, tm)

    d_pre, d_post = [None] * 4, [None] * 4
    d_ssm_in, d_ssm_out, d_att_in, d_att_out = [None] * 2, [None] * 2, [None] * 2, [None] * 2
    d_cw, d_cb, d_dtb, d_alog, d_dsk, d_gn, d_sinks = ([None] * 2 for _ in range(7))
    for i in reversed(range(4)):
        j = i // 2
        s = saved[i]
        wn_pre, wn_post = pre_norm[i].reshape(1, D_MODEL), post_norm[i].reshape(1, D_MODEL)
        if i % 2 == 0:
            da3, dy, d_post[i] = _mm_dout(s["y"], dx, wn_post, w_ssm_out[j], 4, tm, f"ssm_dout_{j}")
            d_ssm_out[j] = _dw_rows(s["a"], dy, 4, tm, f"ssm_dwout_{j}")
            du, d_dtb[j], d_alog[j], d_dsk[j], dgn = _ssm2_bwd(
                s["u"], s["xbc"], s["pre"], s["yp"], s["hst"], da3, s["dtb"], s["alog"], s["dsk"], s["gn"],
                f"ssm_core_bwd_{j}")
            du, dcw, dcb = _conv_bwd(s["uc"], s["dsl"], du, s["cw"], f"ssm_conv_bwd_{j}")
            d_cw[j], d_cb[j], d_gn[j] = _conv_untile(dcw), _conv_untile(dcb), dgn.reshape(1, SSM_INNER)
            d_ssm_in[j] = _ssm_w_in_untile(_dw_cols(s["h"], du, 5, tm, f"ssm_dwin_{j}"))
            dx, d_pre[i] = _mm_dh(du, s["w_in"], s["x"], dx, wn_pre, 5, tm, f"ssm_dh_{j}")
        else:
            da, dy, d_post[i] = _mm_dout(s["y"], dx, wn_post, w_att_out[j], 4, tm, f"att_dout_{j}")
            d_att_out[j] = _dw_rows(s["a"], dy, 4, tm, f"att_dwout_{j}")
            du, d_sinks[j] = _att2_bwd(s["u"], cos_t, sin_t, s["sinks"], da, f"att_core_bwd_{j}")
            d_att_in[j] = _dw_cols(s["h"], du, 5, tm, f"att_dwin_{j}")
            dx, d_pre[i] = _mm_dh(du, w_att_in[j], s["x"], dx, wn_pre, 5, tm, f"att_dh_{j}")

    gs = _pack_grads(jnp.stack(d_ssm_in), jnp.stack(d_ssm_out), jnp.stack(d_att_in), jnp.stack(d_att_out),
                     jnp.stack(d_cw))
    r1 = _exchange_sibling(gs)
    tiles = (256, 256, ROWS_REST // 7)
    pairs = [_pair_sum(g, r, cidx, tr, f"rs_pair_sum_{k}") for k, (g, r, tr) in enumerate(zip(gs, r1, tiles))]
    parts = _exchange_chips(pairs)
    flat = lambda t: t.reshape(2048, t.shape[-1])
    a4 = _adamw(parts[0], flat(ssm_w_in), flat(m_ssm_w_in), flat(v_ssm_w_in), tiles[0], "adamw_ssm_in")
    b4 = _adamw(parts[1], flat(att_w_in), flat(m_att_w_in), flat(v_att_w_in), tiles[1], "adamw_att_in")
    c4 = _adamw(parts[2], _pack_rest(ssm_w_out, att_w_out, ssm_conv_w), _pack_rest(m_ssm_w_out, m_att_w_out, m_ssm_conv_w),
                _pack_rest(v_ssm_w_out, v_att_w_out, v_ssm_conv_w), tiles[2], "adamw_rest")
    big = []
    for k in range(4):
        o_ssm_out, o_att_out, o_conv = _unpack_rest(c4[k])
        big.append((a4[k].reshape(2, 1024, 772), o_ssm_out, b4[k].reshape(2, 1024, 320), o_att_out, o_conv))

    small_local = _pack_small(jnp.concatenate(d_pre, axis=0), jnp.concatenate(d_post, axis=0),
                              jnp.concatenate(d_cb, axis=0), jnp.concatenate(d_gn, axis=0),
                              jnp.concatenate(d_dtb, axis=0), jnp.concatenate(d_alog, axis=0),
                              jnp.concatenate(d_dsk, axis=0), jnp.concatenate(d_sinks, axis=0), loss_part[0, 0])
    small_all = _all_gather_direct(small_local, "all_gather_small")
    ws = _pack_small(pre_norm, post_norm, ssm_conv_b, ssm_gate_norm, ssm_dt_bias, ssm_a_log, ssm_d, att_sinks)
    ms = _pack_small(m_pre_norm, m_post_norm, m_ssm_conv_b, m_ssm_gate_norm, m_ssm_dt_bias, m_ssm_a_log, m_ssm_d,
                     m_att_sinks)
    vs = _pack_small(v_pre_norm, v_post_norm, v_ssm_conv_b, v_ssm_gate_norm, v_ssm_dt_bias, v_ssm_a_log, v_ssm_d,
                     v_att_sinks)
    small4 = _adamw(small_all, ws, ms, vs, ROWS_SMALL, "adamw_small")
    loss = small4[0][32, 224]
    small = [_unpack_small(t) for t in small4]

    outs = [loss, dx.reshape(1, t_len, D_MODEL)]
    for k in range(4):
        b_ssm_in, b_ssm_out, b_att_in, b_att_out, b_conv = big[k]
        s_pre, s_post, s_cb, s_dtb, s_alog, s_d, s_gn, s_sinks = small[k]
        outs += [s_pre, s_post, b_ssm_in, b_conv, s_cb, s_dtb, s_alog, s_d, s_gn, b_ssm_out, b_att_in, s_sinks,
                 b_att_out]
    return tuple(outs)
```

```python
import jax
import jax.numpy as jnp
from jax import lax
from jax.experimental import pallas as pl
from jax.experimental.pallas import tpu as pltpu

f32 = jnp.float32
MXU_DTYPE = jnp.bfloat16
SDS = jax.ShapeDtypeStruct
MESH = pl.DeviceIdType.MESH

D_MODEL = 1024
EPS = 1e-6
TILE = 256
CHUNK = 128
ATT_ROWS = 4 * CHUNK
SSM_HEADS = 32
SSM_GROUPS = 8
SSM_P = 64
SSM_N = 128
SSM_INNER = 2048
SSM_IN = 6176
SSM_NT = 25
SSM_KT = 8
ATT_NT = 10
ATT_KT = 4
ATT_IN = 2560
ROPE_THETA = 500000.0
N_DEV = 8
VMEM_LIMIT = 56 * 1024 * 1024

ADAM_LR = 0.001
ADAM_B1 = 0.9
ADAM_B2 = 0.999
ADAM_EPS = 1e-08
ADAM_WD = 0.01
ADAM_STEP = 10

ROWS_SSM_OUT = 2 * 256
ROWS_ATT_OUT = 2 * 128
ROWS_SMALL = 40


def _pc(body, name, **kw):
    return pl.pallas_call(body, name=name, **kw)


def _params(sem):
    return pltpu.CompilerParams(dimension_semantics=sem, vmem_limit_bytes=VMEM_LIMIT)


def _sigmoid(x):
    return 0.5 * jnp.tanh(0.5 * x) + 0.5


def _silu(x):
    return x * _sigmoid(x)


def _softplus(x):
    return jnp.maximum(x, 0.0) + jnp.log(1.0 + jnp.exp(-jnp.abs(x)))


def _mx(x):
    return x.astype(MXU_DTYPE)


def _dot(a, b):
    return jnp.dot(_mx(a), _mx(b), preferred_element_type=f32)


def _dot_nt(a, b):
    return lax.dot_general(_mx(a), _mx(b), (((1,), (1,)), ((), ())), preferred_element_type=f32)


def _dot_tn(a, b):
    return lax.dot_general(_mx(a), _mx(b), (((0,), (0,)), ((), ())), preferred_element_type=f32)


def _rms_fwd(x, w):
    r = lax.rsqrt(jnp.mean(x * x, axis=-1, keepdims=True) + EPS)
    return x * r * w


def _rms_bwd(x, w, dy):
    r = lax.rsqrt(jnp.mean(x * x, axis=-1, keepdims=True) + EPS)
    xh = x * r
    dw = jnp.sum(dy * xh, axis=0, keepdims=True)
    g = dy * w
    dx = r * (g - xh * jnp.mean(g * xh, axis=-1, keepdims=True))
    return dx, dw


def _mm_in(x, wn, w, ntb, tm, name):
    t_len, d = x.shape
    nt = w.shape[1] // TILE

    def body(x_ref, wn_ref, w_ref, u_ref, h_ref):
        @pl.when(pl.program_id(1) == 0)
        def _():
            h_ref[...] = _rms_fwd(x_ref[...], wn_ref[...]).astype(h_ref.dtype)
        h = h_ref[...]
        for t in range(ntb):
            u_ref[t] = jnp.dot(h, w_ref[:, TILE * t:TILE * (t + 1)], preferred_element_type=f32)

    return _pc(
        body, name, grid=(t_len // tm, nt // ntb),
        in_specs=[pl.BlockSpec((tm, d), lambda i, j: (i, 0)),
                  pl.BlockSpec((1, d), lambda i, j: (0, 0)),
                  pl.BlockSpec((d, ntb * TILE), lambda i, j: (0, j))],
        out_specs=[pl.BlockSpec((ntb, tm, TILE), lambda i, j: (j, i, 0)),
                   pl.BlockSpec((tm, d), lambda i, j: (i, 0))],
        out_shape=[SDS((nt, t_len, TILE), f32), SDS((t_len, d), MXU_DTYPE)],
        compiler_params=_params(("arbitrary", "arbitrary")),
    )(x, wn, w)


def _mm_dout(y, dxn, wn, w, ntb, tm, name):
    t_len, d = y.shape
    nt = w.shape[0] // TILE

    def body(y_ref, dxn_ref, wn_ref, w_ref, da_ref, dy_ref, dwn_ref):
        i, j = pl.program_id(0), pl.program_id(1)

        @pl.when((i == 0) & (j == 0))
        def _():
            dwn_ref[...] = jnp.zeros_like(dwn_ref)

        @pl.when(j == 0)
        def _():
            dy, dw = _rms_bwd(y_ref[...], wn_ref[...], dxn_ref[...])
            dy_ref[...] = dy.astype(dy_ref.dtype)
            dwn_ref[...] += dw
        dy = dy_ref[...]
        for t in range(ntb):
            da_ref[t] = _dot_nt(dy, w_ref[TILE * t:TILE * (t + 1), :])

    return _pc(
        body, name, grid=(t_len // tm, nt // ntb),
        in_specs=[pl.BlockSpec((tm, d), lambda i, j: (i, 0)),
                  pl.BlockSpec((tm, d), lambda i, j: (i, 0)),
                  pl.BlockSpec((1, d), lambda i, j: (0, 0)),
                  pl.BlockSpec((ntb * TILE, d), lambda i, j: (j, 0))],
        out_specs=[pl.BlockSpec((ntb, tm, TILE), lambda i, j: (j, i, 0)),
                   pl.BlockSpec((tm, d), lambda i, j: (i, 0)),
                   pl.BlockSpec((1, d), lambda i, j: (0, 0))],
        out_shape=[SDS((nt, t_len, TILE), f32), SDS((t_len, d), MXU_DTYPE), SDS((1, d), f32)],
        compiler_params=_params(("arbitrary", "arbitrary")),
    )(y, dxn, wn, w)


def _mm_out(a, w, x, wn, ktb, tm, name):
    kt, t_len, _ = a.shape
    d = w.shape[1]
    nk = kt // ktb

    def body(a_ref, w_ref, x_ref, wn_ref, y_ref, xn_ref, acc):
        k = pl.program_id(1)

        @pl.when(k == 0)
        def _():
            acc[...] = jnp.zeros_like(acc)
        s = acc[...]
        for t in range(ktb):
            s = s + jnp.dot(a_ref[t], w_ref[TILE * t:TILE * (t + 1), :], preferred_element_type=f32)
        acc[...] = s

        @pl.when(k == nk - 1)
        def _():
            y = acc[...]
            y_ref[...] = y
            xn_ref[...] = x_ref[...] + _rms_fwd(y, wn_ref[...])

    return _pc(
        body, name, grid=(t_len // tm, nk),
        in_specs=[pl.BlockSpec((ktb, tm, TILE), lambda i, k: (k, i, 0)),
                  pl.BlockSpec((ktb * TILE, d), lambda i, k: (k, 0)),
                  pl.BlockSpec((tm, d), lambda i, k: (i, 0)),
                  pl.BlockSpec((1, d), lambda i, k: (0, 0))],
        out_specs=[pl.BlockSpec((tm, d), lambda i, k: (i, 0)),
                   pl.BlockSpec((tm, d), lambda i, k: (i, 0))],
        out_shape=[SDS((t_len, d), f32), SDS((t_len, d), f32)],
        scratch_shapes=[pltpu.VMEM((tm, d), f32)],
        compiler_params=_params(("arbitrary", "arbitrary")),
    )(a, w, x, wn)


def _mm_dh(du, w, x, dxn, wn, ktb, tm, name):
    kt, t_len, _ = du.shape
    d = w.shape[0]
    nk = kt // ktb

    def body(du_ref, w_ref, x_ref, dxn_ref, wn_ref, dx_ref, dwn_ref, acc):
        i, k = pl.program_id(0), pl.program_id(1)

        @pl.when((i == 0) & (k == 0))
        def _():
            dwn_ref[...] = jnp.zeros_like(dwn_ref)

        @pl.when(k == 0)
        def _():
            acc[...] = jnp.zeros_like(acc)
        s = acc[...]
        for t in range(ktb):
            s = s + _dot_nt(du_ref[t], w_ref[:, TILE * t:TILE * (t + 1)])
        acc[...] = s

        @pl.when(k == nk - 1)
        def _():
            dxp, dw = _rms_bwd(x_ref[...], wn_ref[...], acc[...])
            dx_ref[...] = dxn_ref[...] + dxp
            dwn_ref[...] += dw

    return _pc(
        body, name, grid=(t_len // tm, nk),
        in_specs=[pl.BlockSpec((ktb, tm, TILE), lambda i, k: (k, i, 0)),
                  pl.BlockSpec((d, ktb * TILE), lambda i, k: (0, k)),
                  pl.BlockSpec((tm, d), lambda i, k: (i, 0)),
                  pl.BlockSpec((tm, d), lambda i, k: (i, 0)),
                  pl.BlockSpec((1, d), lambda i, k: (0, 0))],
        out_specs=[pl.BlockSpec((tm, d), lambda i, k: (i, 0)),
                   pl.BlockSpec((1, d), lambda i, k: (0, 0))],
        out_shape=[SDS((t_len, d), f32), SDS((1, d), f32)],
        scratch_shapes=[pltpu.VMEM((tm, d), f32)],
        compiler_params=_params(("arbitrary", "arbitrary")),
    )(du, w, x, dxn, wn)


def _dw_cols(a, b, ntb, tk, name):
    t_len, kdim = a.shape
    nt = b.shape[0]

    def body(a_ref, b_ref, o_ref):
        @pl.when(pl.program_id(1) == 0)
        def _():
            o_ref[...] = jnp.zeros_like(o_ref)
        av = a_ref[...]
        for s in range(ntb):
            o_ref[:, TILE * s:TILE * (s + 1)] += _dot_tn(av, b_ref[s])

    return _pc(
        body, name, grid=(nt // ntb, t_len // tk),
        in_specs=[pl.BlockSpec((tk, kdim), lambda j, t: (t, 0)),
                  pl.BlockSpec((ntb, tk, TILE), lambda j, t: (j, t, 0))],
        out_specs=pl.BlockSpec((kdim, ntb * TILE), lambda j, t: (0, j)),
        out_shape=SDS((kdim, nt * TILE), f32),
        compiler_params=_params(("arbitrary", "arbitrary")),
    )(a, b)


def _dw_rows(a, b, ktb, tk, name):
    kt, t_len, _ = a.shape
    d = b.shape[1]

    def body(a_ref, b_ref, o_ref):
        @pl.when(pl.program_id(1) == 0)
        def _():
            o_ref[...] = jnp.zeros_like(o_ref)
        bv = b_ref[...]
        for s in range(ktb):
            o_ref[TILE * s:TILE * (s + 1), :] += _dot_tn(a_ref[s], bv)

    return _pc(
        body, name, grid=(kt // ktb, t_len // tk),
        in_specs=[pl.BlockSpec((ktb, tk, TILE), lambda k, t: (k, t, 0)),
                  pl.BlockSpec((tk, d), lambda k, t: (t, 0))],
        out_specs=pl.BlockSpec((ktb * TILE, d), lambda k, t: (k, 0)),
        out_shape=SDS((kt * TILE, d), f32),
        compiler_params=_params(("arbitrary", "arbitrary")),
    )(a, b)


def _loss_grad(x, tgt, tm):
    t_len, d = x.shape

    def body(x_ref, t_ref, dx_ref, l_ref):
        @pl.when(pl.program_id(0) == 0)
        def _():
            l_ref[...] = jnp.zeros_like(l_ref)
        e = x_ref[...] - t_ref[...]
        dx_ref[...] = e * (1.0 / d)
        row = jnp.mean(e * e, axis=-1, keepdims=True)
        l_ref[...] += 0.5 * jnp.sum(row, axis=0, keepdims=True)

    return _pc(
        body, "loss_grad", grid=(t_len // tm,),
        in_specs=[pl.BlockSpec((tm, d), lambda i: (i, 0)), pl.BlockSpec((tm, d), lambda i: (i, 0))],
        out_specs=[pl.BlockSpec((tm, d), lambda i: (i, 0)), pl.BlockSpec((1, 128), lambda i: (0, 0))],
        out_shape=[SDS((t_len, d), f32), SDS((1, 128), f32)],
        compiler_params=_params(("arbitrary",)),
    )(x, tgt)


def _tri(lower):
    r = lax.broadcasted_iota(jnp.int32, (CHUNK, CHUNK), 0)
    c = lax.broadcasted_iota(jnp.int32, (CHUNK, CHUNK), 1)
    return ((c <= r) if lower else (c >= r)).astype(f32)


def _split(x, n):
    parts = []
    for _ in range(n):
        p = x.astype(jnp.bfloat16)
        parts.append(p)
        x = x - p.astype(f32)
    return parts


def _dot_exact(a, b, dims, split_a, n=3):
    out = None
    if split_a:
        b = b.astype(jnp.bfloat16)
        for p in _split(a, n):
            t = lax.dot_general(p, b, (dims, ((), ())), preferred_element_type=f32)
            out = t if out is None else out + t
    else:
        a = a.astype(jnp.bfloat16)
        for p in _split(b, n):
            t = lax.dot_general(a, p, (dims, ((), ())), preferred_element_type=f32)
            out = t if out is None else out + t
    return out


def _dt_path(dt_raw, dtb, alog):
    dtr = dt_raw + dtb
    dt = _softplus(dtr)
    a_neg = -jnp.exp(alog)
    a = dt * a_neg
    acs = _dot_exact(_tri(True), a, ((1,), (0,)), False)
    acs_t = _dot_exact(a, _tri(False), ((0,), (0,)), True)
    return dtr, dt, a_neg, acs, acs_t


CONV_ROWS = 1024
CONV_SUB = 64


CONV_NTB = 4


def _mm_in_conv(x, wn, w, cw, cb, tm, name):
    t_len, d = x.shape
    nt = w.shape[1] // TILE

    def body(x_ref, wn_ref, w_ref, cw_ref, cb_ref, u_ref, o_ref, ds_ref, h_s, carry_s, win_s):
        i, j = pl.program_id(0), pl.program_id(1)

        @pl.when(j == 0)
        def _():
            h_s[...] = _rms_fwd(x_ref[...], wn_ref[...]).astype(h_s.dtype)
        h = h_s[...]
        for t in range(CONV_NTB):
            p = CONV_NTB * j + t
            ut = jnp.dot(h, w_ref[:, TILE * t:TILE * (t + 1)], preferred_element_type=f32)
            u_ref[t] = ut
            win_s[t, 0:8, :] = jnp.where(i > 0, carry_s[p], 0.0)
            win_s[t, 8:8 + tm, :] = ut
            carry_s[p] = ut[tm - 8:tm, :]
            wk = [cw_ref[t, k:k + 1, :] for k in range(4)]
            b = cb_ref[t]
            for s in range(tm // CONV_SUB):
                o = CONV_SUB * s
                acc = b
                for k in range(4):
                    acc = acc + wk[k] * win_s[t, 5 + k + o:5 + k + o + CONV_SUB, :]
                sg = _sigmoid(acc)
                o_ref[t, o:o + CONV_SUB, :] = acc * sg
                ds_ref[t, o:o + CONV_SUB, :] = sg * (1.0 + acc * (1.0 - sg))

    return _pc(
        body, name, grid=(t_len // tm, nt // CONV_NTB),
        in_specs=[pl.BlockSpec((tm, d), lambda i, j: (i, 0)),
                  pl.BlockSpec((1, d), lambda i, j: (0, 0)),
                  pl.BlockSpec((d, CONV_NTB * TILE), lambda i, j: (0, j)),
                  pl.BlockSpec((CONV_NTB, 4, TILE), lambda i, j: (j, 0, 0)),
                  pl.BlockSpec((CONV_NTB, 1, TILE), lambda i, j: (j, 0, 0))],
        out_specs=[pl.BlockSpec((CONV_NTB, tm, TILE), lambda i, j: (j, i, 0))] * 3,
        out_shape=[SDS((nt, t_len, TILE), f32)] * 3,
        scratch_shapes=[pltpu.VMEM((tm, d), MXU_DTYPE), pltpu.VMEM((nt, 8, TILE), f32),
                        pltpu.VMEM((CONV_NTB, 8 + tm, TILE), f32)],
        compiler_params=_params(("arbitrary", "arbitrary")),
    )(x, wn, w, cw, cb)


def _conv_bwd(u, dsl, du, cw, name):
    t_len = u.shape[1]
    rows = min(CONV_ROWS, t_len)
    nb = t_len // rows

    def body(u_ref, ds_ref, cw_ref, d_ref, o_ref, dcw_ref, dcb_ref, carry_s, dp_s):
        i = pl.program_id(1)

        @pl.when(i == 0)
        def _():
            carry_s[...] = jnp.zeros_like(carry_s)
            dcw_ref[...] = jnp.zeros_like(dcw_ref)
            dcb_ref[...] = jnp.zeros_like(dcb_ref)
        w = [cw_ref[0, k:k + 1, :] for k in range(4)]

        def fold(v):
            return jnp.sum(v.reshape(CONV_SUB // 8, 8, TILE), axis=0)

        dw = [jnp.zeros((8, TILE), f32)] * 4
        db = jnp.zeros((8, TILE), f32)
        for s in range(rows // CONV_SUB):
            o = CONV_SUB * s
            dpre = d_ref[0, o:o + CONV_SUB, :] * ds_ref[0, o:o + CONV_SUB, :]
            dp_s[o:o + CONV_SUB, :] = dpre
            db = db + fold(dpre)
        dp_s[rows:rows + 8, :] = carry_s[...]
        for s in range(rows // CONV_SUB):
            o = CONV_SUB * s
            xs = u_ref[0, o:o + CONV_SUB, :]
            acc = jnp.zeros((CONV_SUB, TILE), f32)
            for k in range(4):
                win = dp_s[3 - k + o:3 - k + o + CONV_SUB, :]
                acc = acc + w[k] * win
                dw[k] = dw[k] + fold(win * xs)
            o_ref[0, o:o + CONV_SUB, :] = acc
        carry_s[...] = dp_s[0:8, :]
        for k in range(4):
            dcw_ref[0, k:k + 1, :] += jnp.sum(dw[k], axis=0, keepdims=True)
        dcb_ref[0] += jnp.sum(db, axis=0, keepdims=True)

    def blk(off):
        return pl.BlockSpec((1, rows, TILE), lambda p, i: (off + p, nb - 1 - i, 0))

    return _pc(
        body, name, grid=(16, nb),
        in_specs=[blk(0), blk(0), pl.BlockSpec((1, 4, TILE), lambda p, i: (p, 0, 0)), blk(8)],
        out_specs=[blk(8),
                   pl.BlockSpec((1, 4, TILE), lambda p, i: (p, 0, 0)),
                   pl.BlockSpec((1, 1, TILE), lambda p, i: (p, 0, 0))],
        out_shape=[SDS(du.shape, f32), SDS((16, 4, TILE), f32), SDS((16, 1, TILE), f32)],
        input_output_aliases={3: 0},
        scratch_shapes=[pltpu.VMEM((8, TILE), f32), pltpu.VMEM((rows + 8, TILE), f32)],
        compiler_params=_params(("arbitrary", "arbitrary")),
    )(u, dsl, cw, du)


def _collapse_matrix(g):
    r = lax.broadcasted_iota(jnp.int32, (SSM_HEADS, TILE), 0)
    c = lax.broadcasted_iota(jnp.int32, (SSM_HEADS, TILE), 1)
    return ((c // SSM_P) + 4 * g == r).astype(jnp.bfloat16)


def _ssd_prelude(dt_raw, dtb, alog, dsk, colx_s, scx_s):
    dtr, dt, a_neg, acs, acs_t = _dt_path(dt_raw, dtb, alog)
    a_end = acs[CHUNK - 1:CHUNK, :]
    lane = lax.broadcasted_iota(jnp.int32, (1, 2 * SSM_P), 1)
    lane4 = lax.broadcasted_iota(jnp.int32, (1, TILE), 1)
    sub8 = lax.broadcasted_iota(jnp.int32, (8, 1), 0)

    def row4(v, g):
        e = [v[:, 4 * g + r:4 * g + r + 1] for r in range(4)]
        return jnp.where(lane4 < 64, e[0], jnp.where(lane4 < 128, e[1], jnp.where(lane4 < 192, e[2], e[3])))

    for g in range(SSM_GROUPS):
        for k, arr in enumerate((dt, acs)):
            for half in range(2):
                h0 = 4 * g + 2 * half
                colx_s[k, g, :, 128 * half:128 * (half + 1)] = jnp.where(
                    lane < SSM_P, arr[:, h0:h0 + 1], arr[:, h0 + 1:h0 + 2])
        scx_s[g] = jnp.where(sub8 == 0, row4(dsk, g), jnp.where(sub8 == 1, row4(a_end, g), 0.0))
    return dtr, dt, a_neg, acs_t


def _ssm_core_specs(nc, rev):
    def cidx(c):
        return (nc - 1 - c) if rev else c
    return [
        pl.BlockSpec((SSM_KT, CHUNK, TILE), lambda c: (0, cidx(c), 0)),
        pl.BlockSpec((1, CHUNK, TILE), lambda c: (SSM_KT, cidx(c), 0)),
        pl.BlockSpec((16, CHUNK, TILE), lambda c: (0, cidx(c), 0)),
        pl.BlockSpec((1, SSM_HEADS), lambda c: (0, 0)),
        pl.BlockSpec((1, SSM_HEADS), lambda c: (0, 0)),
        pl.BlockSpec((1, SSM_HEADS), lambda c: (0, 0)),
        pl.BlockSpec((SSM_KT, 1, TILE), lambda c: (0, 0, 0)),
        pl.BlockSpec((2, SSM_GROUPS, CHUNK, TILE), lambda c: (0, 0, cidx(c), 0)),
        pl.BlockSpec((1, SSM_GROUPS, 8, TILE), lambda c: (cidx(c), 0, 0, 0)),
        pl.BlockSpec((1, SSM_HEADS, CHUNK), lambda c: (cidx(c), 0, 0)),
    ]


PRE_CHUNKS = 4


def _ssm_pre(u, dtb, alog, dsk, name):
    t_len = u.shape[1]
    nc = t_len // CHUNK
    per = min(PRE_CHUNKS, nc)

    def body(dt_ref, dtb_ref, alog_ref, dsk_ref, colx_ref, scx_ref, acst_ref):
        for c in range(per):
            rows = pl.ds(CHUNK * c, CHUNK)
            _, _, _, acs_t = _ssd_prelude(dt_ref[0, rows, 0:SSM_HEADS], dtb_ref[...], alog_ref[...], dsk_ref[...],
                                          colx_ref.at[:, :, rows, :], scx_ref.at[c])
            acst_ref[c] = acs_t

    vec = pl.BlockSpec((1, SSM_HEADS), lambda i: (0, 0))
    return _pc(
        body, name, grid=(nc // per,),
        in_specs=[pl.BlockSpec((1, per * CHUNK, TILE), lambda i: (SSM_KT, i, 0)), vec, vec, vec],
        out_specs=[pl.BlockSpec((2, SSM_GROUPS, per * CHUNK, TILE), lambda i: (0, 0, i, 0)),
                   pl.BlockSpec((per, SSM_GROUPS, 8, TILE), lambda i: (i, 0, 0, 0)),
                   pl.BlockSpec((per, SSM_HEADS, CHUNK), lambda i: (i, 0, 0))],
        out_shape=[SDS((2, SSM_GROUPS, t_len, TILE), f32), SDS((nc, SSM_GROUPS, 8, TILE), f32),
                   SDS((nc, SSM_HEADS, CHUNK), f32)],
        compiler_params=_params(("arbitrary",)),
    )(u, dtb, alog, dsk)


def _stack_cols_rows(acx, rows):
    ac = jnp.concatenate([acx[:, SSM_P * r:SSM_P * r + 1] for r in range(4)], axis=0)
    ar = jnp.concatenate([jnp.broadcast_to(rows[r:r + 1, :], (CHUNK, CHUNK)) for r in range(4)], axis=0)
    return ac, ar


def _ssm2_fwd(u, xbc, pre, dtb, alog, dsk, gn, name):
    t_len = u.shape[1]
    nc = t_len // CHUNK

    def body(z_ref, dt_ref, x_ref, dtb_ref, alog_ref, dsk_ref, gn_ref, colx_ref, scx_ref, acst_ref,
             a3_ref, yp_ref, hst_ref, h_s, xt_s, yd_s):
        c = pl.program_id(0)

        @pl.when(c == 0)
        def _():
            h_s[...] = jnp.zeros_like(h_s)
        acs_t = acst_ref[0]
        causal = _tri4()

        def group(g, s1):
            xs = x_ref[g]
            bm, cm = x_ref[8 + g, :, 0:SSM_N], x_ref[8 + g, :, SSM_N:2 * SSM_N]
            cb = _dot_nt(cm, bm)
            sc = scx_ref[0, g]
            a_end = sc[1:2, :]
            rows = pltpu.roll(acs_t, (SSM_HEADS - 4 * g) % SSM_HEADS, 0)
            hp = h_s[g]
            xt = xs * colx_ref[0, g]
            xt_s[...] = xt
            acx = colx_ref[1, g]
            ac_st, ar_st = _stack_cols_rows(acx, rows)
            m = jnp.concatenate([cb] * 4, axis=0) * jnp.exp(jnp.where(causal, ac_st - ar_st, -jnp.inf))
            for r in range(4):
                hd = slice(SSM_P * r, SSM_P * (r + 1))
                yd_s[:, hd] = _dot(m[CHUNK * r:CHUNK * (r + 1), :], xt_s[:, hd])
            yp_ref[g] = yd_s[...] + _dot(cm, hp) * jnp.exp(acx) + sc[0:1, :] * xs
            hst_ref[0, g] = hp
            h_s[g] = hp * jnp.exp(a_end) + _dot_tn(bm, xt * jnp.exp(a_end - acx))
            y2 = yp_ref[g] * _silu(z_ref[g])
            return s1 + jnp.sum(y2 * y2, axis=1, keepdims=True)

        s1 = lax.fori_loop(0, SSM_GROUPS // 2, lambda i, c: group(2 * i + 1, group(2 * i, c)),
                           jnp.zeros((CHUNK, 1), f32))
        rinv = lax.rsqrt(s1 * (1.0 / SSM_INNER) + EPS)

        def gate(g, carry):
            y2 = yp_ref[g] * _silu(z_ref[g])
            a3_ref[g] = (y2 * rinv * gn_ref[g]).astype(a3_ref.dtype)
            return carry

        lax.fori_loop(0, SSM_GROUPS, gate, 0)

    return _pc(
        body, name, grid=(nc,),
        in_specs=_ssm_core_specs(nc, False),
        out_specs=[pl.BlockSpec((SSM_KT, CHUNK, TILE), lambda c: (0, c, 0)),
                   pl.BlockSpec((SSM_KT, CHUNK, TILE), lambda c: (0, c, 0)),
                   pl.BlockSpec((1, SSM_GROUPS, SSM_N, TILE), lambda c: (c, 0, 0, 0))],
        out_shape=[SDS((SSM_KT, t_len, TILE), MXU_DTYPE), SDS((SSM_KT, t_len, TILE), f32),
                   SDS((nc, SSM_GROUPS, SSM_N, TILE), f32)],
        scratch_shapes=[pltpu.VMEM((SSM_GROUPS, SSM_N, TILE), f32), pltpu.VMEM((CHUNK, TILE), f32),
                        pltpu.VMEM((CHUNK, TILE), f32)],
        compiler_params=_params(("arbitrary",)),
    )(u, u, xbc, dtb, alog, dsk, gn, *pre)


def _ssm2_bwd(u, xbc, pre, yp, hst, da3, dtb, alog, dsk, gn, name):
    t_len = u.shape[1]
    nc = t_len // CHUNK

    def body(z_ref, dt_ref, x_ref, dtb_ref, alog_ref, dsk_ref, gn_ref, colx_s, scx_ref, acst_ref,
             yp_ref, hst_ref, da3_ref, du_ref, ddtb_ref, dalog_ref, ddsk_ref, dgn_ref,
             dh_s, xt_s, dy_s, dxt_s, ddtx_s, dacx_s, ddx_s, drow_s, dm_s, dmt_s):
        step = pl.program_id(0)

        @pl.when(step == 0)
        def _():
            dh_s[...] = jnp.zeros_like(dh_s)
            ddtb_ref[...] = jnp.zeros_like(ddtb_ref)
            dalog_ref[...] = jnp.zeros_like(dalog_ref)
            ddsk_ref[...] = jnp.zeros_like(ddsk_ref)
            dgn_ref[...] = jnp.zeros_like(dgn_ref)
        dtr = dt_ref[0, :, 0:SSM_HEADS] + dtb_ref[...]
        dt = _softplus(dtr)
        a_neg = -jnp.exp(alog_ref[...])
        acs_t = acst_ref[0]
        causal = _tri4()
        causal_t = (lax.broadcasted_iota(jnp.int32, (ATT_ROWS, CHUNK), 1)
                    >= lax.broadcasted_iota(jnp.int32, (ATT_ROWS, CHUNK), 0) % CHUNK)
        last = (lax.broadcasted_iota(jnp.int32, (1, CHUNK), 1) == CHUNK - 1).astype(f32)
        lane = lax.broadcasted_iota(jnp.int32, (1, TILE), 1)
        sub32 = lax.broadcasted_iota(jnp.int32, (SSM_HEADS, 1), 0)
        drow_s[...] = jnp.zeros_like(drow_s)

        def sums(g, carry):
            s1, s2 = carry
            y2 = yp_ref[g] * _silu(z_ref[g])
            g3 = da3_ref[g] * gn_ref[g]
            return (s1 + jnp.sum(y2 * y2, axis=1, keepdims=True), s2 + jnp.sum(g3 * y2, axis=1, keepdims=True))

        zcol = jnp.zeros((CHUNK, 1), f32)
        carry = (zcol, zcol)
        for g in range(SSM_GROUPS):
            carry = sums(g, carry)
        s1, s2 = carry
        rinv = lax.rsqrt(s1 * (1.0 / SSM_INNER) + EPS)
        m2 = s2 * rinv * rinv * rinv * (1.0 / SSM_INNER)

        def group(g, carry):
            z = z_ref[g]
            sg = _sigmoid(z)
            sz = z * sg
            y = yp_ref[g]
            y2 = y * sz
            da3 = da3_ref[g]
            dgn_ref[g] += jnp.sum(da3 * y2 * rinv, axis=0, keepdims=True)
            dy2 = rinv * (da3 * gn_ref[g]) - y2 * m2
            dy = dy2 * sz
            dy_s[...] = dy
            du_ref[g] = dy2 * y * (sg * (1.0 + z * (1.0 - sg)))
            xs = x_ref[g]
            bm, cm = x_ref[8 + g, :, 0:SSM_N], x_ref[8 + g, :, SSM_N:2 * SSM_N]
            cb = _dot_nt(cm, bm)
            cbt = _dot_nt(bm, cm)
            dtx, acx = colx_s[0, g], colx_s[1, g]
            sc = scx_ref[0, g]
            a_end = sc[1:2, :]
            ex = jnp.exp(acx)
            wdx = jnp.exp(a_end - acx)
            eend = jnp.exp(a_end)
            rows = pltpu.roll(acs_t, (SSM_HEADS - 4 * g) % SSM_HEADS, 0)
            hp = hst_ref[0, g]
            dhn = dh_s[g]
            xt = xs * dtx
            xt_s[...] = xt
            ch = _dot(cm, hp)
            gy = dy * ex
            dcm = _dot_nt(gy, hp)
            dh_s[g] = _dot_tn(cm, gy) + dhn * eend
            q = _dot(bm, dhn)
            dbm = _dot_nt(xt * wdx, dhn)
            qx = q * xt * wdx
            v_end = jnp.sum(dhn * hp, axis=0, keepdims=True) * eend + jnp.sum(qx, axis=0, keepdims=True)
            ac_st, ar_st = _stack_cols_rows(acx, rows)
            lam = jnp.exp(jnp.where(causal, ac_st - ar_st, -jnp.inf))
            lam_t = jnp.exp(jnp.where(causal_t, ar_st - ac_st, -jnp.inf))
            m = jnp.concatenate([cb] * 4, axis=0) * lam
            m_t = jnp.concatenate([cbt] * 4, axis=0) * lam_t
            for r in range(4):
                hd = slice(SSM_P * r, SSM_P * (r + 1))
                rs = slice(CHUNK * r, CHUNK * (r + 1))
                dm_s[rs, :] = _dot_nt(dy_s[:, hd], xt_s[:, hd])
                dmt_s[rs, :] = _dot_nt(xt_s[:, hd], dy_s[:, hd])
                dxt_s[:, hd] = _dot(m_t[rs, :], dy_s[:, hd])
            dm = dm_s[...]
            dl = dm * lam
            dseg = dm * m
            dseg_t = dmt_s[...] * m_t
            dcb = dl[0:CHUNK] + dl[CHUNK:2 * CHUNK] + dl[2 * CHUNK:3 * CHUNK] + dl[3 * CHUNK:4 * CHUNK]
            drows = jnp.zeros((SSM_HEADS, CHUNK), f32)
            for r in range(4):
                rs = slice(CHUNK * r, CHUNK * (r + 1))
                in_head = (lane >= SSM_P * r) & (lane < SSM_P * (r + 1))
                d_ac = jnp.sum(dseg_t[rs, :], axis=0, keepdims=True)
                d_ar = jnp.sum(dseg[rs, :], axis=0, keepdims=True)
                d_aend = jnp.sum(jnp.where(in_head, v_end, 0.0), axis=1, keepdims=True)
                drows = drows + jnp.where(sub32 == r, d_ac - d_ar + last * d_aend, 0.0)
            dxt = dxt_s[...] + q * wdx
            du_ref[8 + g] = sc[0:1, :] * dy + dxt * dtx
            ddtx_s[g] = dxt * xs
            dacx_s[g] = dy * ch * ex - qx
            ddx_s[g] = jnp.broadcast_to(jnp.sum(dy * xs, axis=0, keepdims=True), (8, TILE))
            du_ref[16 + g, :, 0:SSM_N] = dbm + _dot_tn(dcb, cm)
            du_ref[16 + g, :, SSM_N:2 * SSM_N] = dcm + _dot(dcb, bm)
            drow_s[...] += pltpu.roll(drows, (4 * g) % SSM_HEADS, 0)
            return carry

        lax.fori_loop(0, SSM_GROUPS, group, 0, unroll=8)
        ddt = jnp.zeros((CHUNK, SSM_HEADS), f32)
        dacs = jnp.zeros((CHUNK, SSM_HEADS), f32)
        ddsk = jnp.zeros((8, SSM_HEADS), f32)
        for g in range(SSM_GROUPS):
            col_g = _collapse_matrix(g)
            ddt = ddt + _dot_exact(ddtx_s[g], col_g, ((1,), (1,)), True, 2)
            dacs = dacs + _dot_exact(dacx_s[g], col_g, ((1,), (1,)), True, 2)
            ddsk = ddsk + _dot_exact(ddx_s[g], col_g, ((1,), (1,)), True, 2)
        upper = _tri(False)
        da = _dot_exact(upper, dacs, ((1,), (0,)), False) + _dot_exact(upper, drow_s[...], ((1,), (1,)), False)
        ddt = ddt + da * a_neg
        dalog_ref[...] += jnp.sum(da * dt, axis=0, keepdims=True) * a_neg
        ddtr = ddt * _sigmoid(dtr)
        ddtb_ref[...] += jnp.sum(ddtr, axis=0, keepdims=True)
        ddsk_ref[...] += ddsk[0:1, :]
        du_ref[SSM_NT - 1] = jnp.zeros((CHUNK, TILE), f32)
        du_ref[SSM_NT - 1, :, 0:SSM_HEADS] = ddtr

    def rc(c):
        return nc - 1 - c

    vec = pl.BlockSpec((1, SSM_HEADS), lambda c: (0, 0))
    return _pc(
        body, name, grid=(nc,),
        in_specs=_ssm_core_specs(nc, True) + [
            pl.BlockSpec((SSM_KT, CHUNK, TILE), lambda c: (0, rc(c), 0)),
            pl.BlockSpec((1, SSM_GROUPS, SSM_N, TILE), lambda c: (rc(c), 0, 0, 0)),
            pl.BlockSpec((SSM_KT, CHUNK, TILE), lambda c: (0, rc(c), 0))],
        out_specs=[pl.BlockSpec((SSM_NT, CHUNK, TILE), lambda c: (0, rc(c), 0)), vec, vec, vec,
                   pl.BlockSpec((SSM_KT, 1, TILE), lambda c: (0, 0, 0))],
        out_shape=[SDS((SSM_NT, t_len, TILE), f32), SDS((1, SSM_HEADS), f32), SDS((1, SSM_HEADS), f32),
                   SDS((1, SSM_HEADS), f32), SDS((SSM_KT, 1, TILE), f32)],
        scratch_shapes=[pltpu.VMEM((SSM_GROUPS, SSM_N, TILE), f32), pltpu.VMEM((CHUNK, TILE), f32),
                        pltpu.VMEM((CHUNK, TILE), f32), pltpu.VMEM((CHUNK, TILE), f32),
                        pltpu.VMEM((SSM_GROUPS, CHUNK, TILE), f32), pltpu.VMEM((SSM_GROUPS, CHUNK, TILE), f32),
                        pltpu.VMEM((SSM_GROUPS, 8, TILE), f32), pltpu.VMEM((SSM_HEADS, CHUNK), f32),
                        pltpu.VMEM((4 * CHUNK, CHUNK), f32), pltpu.VMEM((4 * CHUNK, CHUNK), f32)],
        compiler_params=_params(("arbitrary",)),
    )(u, u, xbc, dtb, alog, dsk, gn, *pre, yp, hst, da3)


def _swap16(t):
    lane = lax.broadcasted_iota(jnp.int32, t.shape, 1) % 64
    return jnp.where(lane < 8, pltpu.roll(t, TILE - 8, 1), jnp.where(lane < 16, pltpu.roll(t, 8, 1), 0.0))


def _rope(t, cos_t, sin_t):
    return t * cos_t + _swap16(t) * sin_t


def _rope_bwd(g, cos_t, sin_t):
    return g * cos_t + _swap16(g * sin_t)


def _att_in_specs(nb, rev):
    def bidx(n):
        return (nb - 1 - n) if rev else n
    return [
        pl.BlockSpec((ATT_NT, CHUNK, TILE), lambda n: (0, bidx(n), 0)),
        pl.BlockSpec((2, CHUNK, TILE), lambda n: (2, jnp.maximum(bidx(n) - 1, 0), 0)),
        pl.BlockSpec((CHUNK, TILE), lambda n: (bidx(n), 0)),
        pl.BlockSpec((CHUNK, TILE), lambda n: (bidx(n), 0)),
        pl.BlockSpec((CHUNK, TILE), lambda n: (jnp.maximum(bidx(n) - 1, 0), 0)),
        pl.BlockSpec((CHUNK, TILE), lambda n: (jnp.maximum(bidx(n) - 1, 0), 0)),
        pl.BlockSpec((1, 16), lambda n: (0, 0)),
    ]


ATT_SCALE = 0.125


def _tri4():
    row = lax.broadcasted_iota(jnp.int32, (ATT_ROWS, CHUNK), 0) % CHUNK
    col = lax.broadcasted_iota(jnp.int32, (ATT_ROWS, CHUNK), 1)
    return col <= row


def _stack_heads(ref):
    return jnp.concatenate([ref[:, 64 * r:64 * (r + 1)] for r in range(4)], axis=0)


def _sink_col(sinks, g):
    return [sinks[:, 4 * g + r:4 * g + r + 1] for r in range(4)]


def _softmax_rows(s_s, pn_s, pc_s, sink, tri, has_prev):
    sub = lax.broadcasted_iota(jnp.int32, (ATT_ROWS, 1), 0)
    sk = jnp.where(sub < CHUNK, sink[0], jnp.where(sub < 2 * CHUNK, sink[1], jnp.where(sub < 3 * CHUNK, sink[2], sink[3])))
    s = jnp.where(tri, s_s[:, CHUNK:2 * CHUNK], jnp.where(has_prev, s_s[:, 0:CHUNK], -jnp.inf)) * ATT_SCALE
    m = jnp.maximum(jnp.max(s, axis=-1, keepdims=True), sk)
    p = jnp.exp(s - m)
    e_sink = jnp.exp(sk - m)
    inv = 1.0 / (jnp.sum(p, axis=-1, keepdims=True) + e_sink)
    pn = p * inv
    pc_s[...] = pn
    pn_s[:, 0:CHUNK] = jnp.where(tri, 0.0, pn)
    pn_s[:, CHUNK:2 * CHUNK] = jnp.where(tri, pn, 0.0)
    return e_sink * inv


def _att2_fwd(u, cos_t, sin_t, sinks, name):
    t_len = u.shape[1]
    nb = t_len // CHUNK

    def body(u_ref, prev_ref, cc_ref, sc_ref, cp_ref, sp_ref, sink_ref, a_ref,
             q_s, kp_s, kc_s, vp_s, vc_s, o_s, s_s, pn_s, pc_s):
        n = pl.program_id(0)
        tri = _tri4()
        cos_c, sin_c = cc_ref[...], sc_ref[...]
        kc_s[...] = _rope(u_ref[4], cos_c, sin_c)
        kp_s[...] = _rope(prev_ref[0], cp_ref[...], sp_ref[...])
        vc_s[...] = u_ref[5]
        vp_s[...] = prev_ref[1]
        sinks = sink_ref[...]
        for g in range(4):
            q_s[g] = _rope(u_ref[g], cos_c, sin_c)
        for g in range(4):
            kv = slice(64 * g, 64 * (g + 1))
            kb = jnp.concatenate([kp_s[:, kv], kc_s[:, kv]], axis=0)
            s_s[g] = _dot_nt(_stack_heads(q_s.at[g]), kb)
        for g in range(4):
            _softmax_rows(s_s.at[g], pn_s.at[g], pc_s.at[g], _sink_col(sinks, g), tri, n > 0)
        for g in range(4):
            kv = slice(64 * g, 64 * (g + 1))
            vb = jnp.concatenate([vp_s[:, kv], vc_s[:, kv]], axis=0)
            o = _dot(pn_s[g], vb)
            for r in range(4):
                o_s[g, :, 64 * r:64 * (r + 1)] = o[CHUNK * r:CHUNK * (r + 1), :]
        for g in range(4):
            a_ref[g] = (o_s[g] * _silu(u_ref[6 + g])).astype(a_ref.dtype)

    return _pc(
        body, name, grid=(nb,),
        in_specs=_att_in_specs(nb, False),
        out_specs=pl.BlockSpec((ATT_KT, CHUNK, TILE), lambda n: (0, n, 0)),
        out_shape=SDS((ATT_KT, t_len, TILE), MXU_DTYPE),
        scratch_shapes=[pltpu.VMEM((4, CHUNK, TILE), f32)] + [pltpu.VMEM((CHUNK, TILE), f32)] * 4
                       + [pltpu.VMEM((4, CHUNK, TILE), f32)] + [pltpu.VMEM((4, ATT_ROWS, 2 * CHUNK), f32)] * 2
                       + [pltpu.VMEM((4, ATT_ROWS, CHUNK), f32)],
        compiler_params=_params(("arbitrary",)),
    )(u, u, cos_t, sin_t, cos_t, sin_t, sinks)


def _att2_bwd(u, cos_t, sin_t, sinks, da, name):
    t_len = u.shape[1]
    nb = t_len // CHUNK

    def body(u_ref, prev_ref, cc_ref, sc_ref, cp_ref, sp_ref, sink_ref, da_ref, du_ref, dsink_ref,
             ck_s, cv_s, kp_s, kc_s, vp_s, vc_s, q_s, o_s, do_s, dq_s, s_s, pn_s, dp_s, dkt_s, dvt_s, pc_s):
        step = pl.program_id(0)
        nn = nb - 1 - step

        @pl.when(step == 0)
        def _():
            ck_s[...] = jnp.zeros_like(ck_s)
            cv_s[...] = jnp.zeros_like(cv_s)
            dsink_ref[...] = jnp.zeros_like(dsink_ref)
        tri = _tri4()
        cos_c, sin_c = cc_ref[...], sc_ref[...]
        cos_p, sin_p = cp_ref[...], sp_ref[...]
        kc_s[...] = _rope(u_ref[4], cos_c, sin_c)
        kp_s[...] = _rope(prev_ref[0], cos_p, sin_p)
        vc_s[...] = u_ref[5]
        vp_s[...] = prev_ref[1]
        sinks = sink_ref[...]
        lane16 = lax.broadcasted_iota(jnp.int32, (1, 16), 1)
        dsink = jnp.zeros((1, 16), f32)
        for g in range(4):
            q_g, o_g, do_g, dq_g = q_s.at[g], o_s.at[g], do_s.at[g], dq_s.at[g]
            s_g, pn_g, dp_g, pc_g = s_s.at[g], pn_s.at[g], dp_s.at[g], pc_s.at[g]
            q_g[...] = _rope(u_ref[g], cos_c, sin_c)
            gate = u_ref[6 + g]
            sg = _sigmoid(gate)
            dav = da_ref[g]
            do_g[...] = dav * (gate * sg)
            kv = slice(64 * g, 64 * (g + 1))
            kb = jnp.concatenate([kp_s[:, kv], kc_s[:, kv]], axis=0)
            vb = jnp.concatenate([vp_s[:, kv], vc_s[:, kv]], axis=0)
            q_st = _stack_heads(q_g)
            do_st = _stack_heads(do_g)
            s_g[...] = _dot_nt(q_st, kb)
            p_sink = _softmax_rows(s_g, pn_g, pc_g, _sink_col(sinks, g), tri, nn > 0)
            o = _dot(pn_g[...], vb)
            dvt_s[64 * g:64 * (g + 1), :] = _dot_tn(do_st, pn_g[...])
            dp_g[...] = _dot_nt(do_st, vb)
            delta = jnp.sum(do_st * o, axis=-1, keepdims=True)
            dpc = jnp.where(tri, dp_g[:, CHUNK:2 * CHUNK], dp_g[:, 0:CHUNK])
            dsc = pc_g[...] * (dpc - delta) * ATT_SCALE
            dp_g[:, 0:CHUNK] = jnp.where(tri, 0.0, dsc)
            dp_g[:, CHUNK:2 * CHUNK] = jnp.where(tri, dsc, 0.0)
            sd = p_sink * delta
            for r in range(4):
                rs = slice(CHUNK * r, CHUNK * (r + 1))
                o_g[:, 64 * r:64 * (r + 1)] = o[rs, :]
                ds_h = -jnp.sum(sd[rs, :], axis=0, keepdims=True)
                dsink = dsink + ds_h * (lane16 == 4 * g + r).astype(f32)
            ds = dp_g[...]
            dq = _dot(ds, kb)
            for r in range(4):
                dq_g[:, 64 * r:64 * (r + 1)] = dq[CHUNK * r:CHUNK * (r + 1), :]
            dkt_s[64 * g:64 * (g + 1), :] = _dot_tn(q_st, ds)
            du_ref[6 + g] = dav * o_g[...] * (sg * (1.0 + gate * (1.0 - sg)))
            du_ref[g] = _rope_bwd(dq_g[...], cos_c, sin_c)
        dk = dkt_s[...].T
        dv = dvt_s[...].T
        du_ref[4] = _rope_bwd(dk[CHUNK:2 * CHUNK, :], cos_c, sin_c) + ck_s[...]
        du_ref[5] = dv[CHUNK:2 * CHUNK, :] + cv_s[...]
        ck_s[...] = _rope_bwd(dk[0:CHUNK, :], cos_p, sin_p)
        cv_s[...] = dv[0:CHUNK, :]
        dsink_ref[...] += dsink

    def rb(n):
        return nb - 1 - n

    return _pc(
        body, name, grid=(nb,),
        in_specs=_att_in_specs(nb, True) + [pl.BlockSpec((ATT_KT, CHUNK, TILE), lambda n: (0, rb(n), 0))],
        out_specs=[pl.BlockSpec((ATT_NT, CHUNK, TILE), lambda n: (0, rb(n), 0)),
                   pl.BlockSpec((1, 16), lambda n: (0, 0))],
        out_shape=[SDS((ATT_NT, t_len, TILE), f32), SDS((1, 16), f32)],
        scratch_shapes=[pltpu.VMEM((CHUNK, TILE), f32)] * 6 + [pltpu.VMEM((4, CHUNK, TILE), f32)] * 4
                       + [pltpu.VMEM((4, ATT_ROWS, 2 * CHUNK), f32)] * 3
                       + [pltpu.VMEM((2 * CHUNK, 2 * CHUNK), f32)] * 2 + [pltpu.VMEM((4, ATT_ROWS, CHUNK), f32)],
        compiler_params=_params(("arbitrary",)),
    )(u, u, cos_t, sin_t, cos_t, sin_t, sinks, da)


_HBM = pl.BlockSpec(memory_space=pltpu.HBM)


def _all_gather_big(shards):
    n = len(shards)

    def body(*refs):
        x_refs, out_refs = refs[:n], refs[n:2 * n]
        send_sems, recv_sems, local_sems = refs[2 * n:]
        x, y, c = lax.axis_index("x"), lax.axis_index("y"), lax.axis_index("c")
        me, sibling = (x, y, c), (x, y, 1 - c)
        chips = [(1 - x, y), (x, 1 - y), (1 - x, 1 - y)]

        def slot(i, px, py, pc):
            return out_refs[i].at[4 * px + 2 * py + pc]

        def copy(i, k, block, to, src=None):
            return pltpu.make_async_remote_copy(
                src_ref=slot(i, *block) if src is None else src, dst_ref=slot(i, *block),
                send_sem=send_sems.at[7 * i + k], recv_sem=recv_sems.at[7 * i + k], device_id=to, device_id_type=MESH)

        mine = [pltpu.make_async_copy(x_refs[i], slot(i, *me), local_sems.at[i]) for i in range(n)]
        for cp in mine:
            cp.start()
        first = []
        for i in range(n):
            first.append(copy(i, 0, me, sibling, src=x_refs[i]))
            first += [copy(i, 1 + j, me, (*chip, c), src=x_refs[i]) for j, chip in enumerate(chips)]
        for cp in first:
            cp.start()
        passed = []
        for j, chip in enumerate(chips):
            for i in range(n):
                copy(i, 1 + j, (*chip, c), me).wait_recv()
                fwd = copy(i, 4 + j, (*chip, c), sibling)
                fwd.start()
                passed.append(fwd)
        for i in range(n):
            copy(i, 0, sibling, me).wait_recv()
            for j, chip in enumerate(chips):
                copy(i, 4 + j, (*chip, 1 - c), me).wait_recv()
        for cp in first + passed:
            cp.wait_send()
        for cp in mine:
            cp.wait()

    return _pc(
        body, "all_gather_big",
        in_specs=[_HBM] * n, out_specs=[_HBM] * n,
        out_shape=[SDS((N_DEV,) + s.shape, s.dtype) for s in shards],
        scratch_shapes=[pltpu.SemaphoreType.DMA((7 * n,)), pltpu.SemaphoreType.DMA((7 * n,)),
                        pltpu.SemaphoreType.DMA((n,))],
    )(*shards)


def _all_gather_direct(block, name):
    rows, width = block.shape

    def body(x_ref, out_ref, send_sems, recv_sems, local_sem):
        x, y, c = lax.axis_index("x"), lax.axis_index("y"), lax.axis_index("c")
        my_slot = 4 * x + 2 * y + c

        def peer(k):
            return (1 - x if k & 4 else x, 1 - y if k & 2 else y, 1 - c if k & 1 else c)

        def copy(k):
            px, py, pc = peer(k)
            return pltpu.make_async_remote_copy(
                src_ref=x_ref, dst_ref=out_ref.at[my_slot], send_sem=send_sems.at[k - 1], recv_sem=recv_sems.at[k - 1],
                device_id=(px, py, pc), device_id_type=MESH)

        def arrival(k):
            px, py, pc = peer(k)
            return pltpu.make_async_remote_copy(
                src_ref=x_ref, dst_ref=out_ref.at[4 * px + 2 * py + pc], send_sem=send_sems.at[k - 1],
                recv_sem=recv_sems.at[k - 1], device_id=(px, py, pc), device_id_type=MESH)

        mine = pltpu.make_async_copy(x_ref, out_ref.at[my_slot], local_sem)
        mine.start()
        for k in range(1, N_DEV):
            copy(k).start()
        for k in range(1, N_DEV):
            arrival(k).wait_recv()
        for k in range(1, N_DEV):
            copy(k).wait_send()
        mine.wait()

    return _pc(
        body, name,
        in_specs=[_HBM], out_specs=_HBM,
        out_shape=SDS((N_DEV, rows, width), block.dtype),
        scratch_shapes=[pltpu.SemaphoreType.DMA((7,)), pltpu.SemaphoreType.DMA((7,)), pltpu.SemaphoreType.DMA],
    )(block)


N_CHIP = N_DEV // 2


def _exchange_sibling(gs):
    n = len(gs)

    def body(*refs):
        g_refs, out_refs = refs[:n], refs[n:2 * n]
        send_sems, recv_sems = refs[2 * n:]
        x, y, c = lax.axis_index("x"), lax.axis_index("y"), lax.axis_index("c")
        cps = [pltpu.make_async_remote_copy(
            src_ref=g_refs[i].at[2 * k + 1 - c], dst_ref=out_refs[i].at[k], send_sem=send_sems.at[N_CHIP * i + k],
            recv_sem=recv_sems.at[N_CHIP * i + k], device_id=(x, y, 1 - c), device_id_type=MESH)
            for i in range(n) for k in range(N_CHIP)]
        for cp in cps:
            cp.start()
        for cp in cps:
            cp.wait()

    return _pc(
        body, "rs_sibling",
        in_specs=[_HBM] * n, out_specs=[_HBM] * n,
        out_shape=[SDS((N_CHIP,) + g.shape[1:], g.dtype) for g in gs],
        scratch_shapes=[pltpu.SemaphoreType.DMA((N_CHIP * n,)), pltpu.SemaphoreType.DMA((N_CHIP * n,))],
    )(*gs)


def _pair_sum(g, r1, cidx, tr, name):
    _, rows, width = g.shape

    def body(c_ref, g_ref, r_ref, o_ref):
        o_ref[...] = (g_ref[...].astype(f32) + r_ref[...].astype(f32)).astype(o_ref.dtype)

    return pl.pallas_call(
        body, name=name,
        grid_spec=pltpu.PrefetchScalarGridSpec(
            num_scalar_prefetch=1, grid=(N_CHIP, rows // tr),
            in_specs=[pl.BlockSpec((1, tr, width), lambda k, i, c_ref: (2 * k + c_ref[0], i, 0)),
                      pl.BlockSpec((1, tr, width), lambda k, i, c_ref: (k, i, 0))],
            out_specs=pl.BlockSpec((1, tr, width), lambda k, i, c_ref: (k, i, 0))),
        out_shape=SDS((N_CHIP, rows, width), g.dtype),
        compiler_params=_params(("arbitrary", "arbitrary")),
    )(cidx, g, r1)


def _exchange_chips(ps):
    n = len(ps)

    def body(*refs):
        p_refs, out_refs = refs[:n], refs[n:2 * n]
        send_sems, recv_sems, local_sems = refs[2 * n:]
        x, y, c = lax.axis_index("x"), lax.axis_index("y"), lax.axis_index("c")
        my_chip = 2 * x + y
        chips = [(1 - x, y), (x, 1 - y), (1 - x, 1 - y)]

        def copy(i, j):
            px, py = chips[j]
            return pltpu.make_async_remote_copy(
                src_ref=p_refs[i].at[2 * px + py], dst_ref=out_refs[i].at[my_chip], send_sem=send_sems.at[3 * i + j],
                recv_sem=recv_sems.at[3 * i + j], device_id=(px, py, c), device_id_type=MESH)

        def arrival(i, j):
            px, py = chips[j]
            return pltpu.make_async_remote_copy(
                src_ref=p_refs[i].at[my_chip], dst_ref=out_refs[i].at[2 * px + py], send_sem=send_sems.at[3 * i + j],
                recv_sem=recv_sems.at[3 * i + j], device_id=(px, py, c), device_id_type=MESH)

        mine = [pltpu.make_async_copy(p_refs[i].at[my_chip], out_refs[i].at[my_chip], local_sems.at[i])
                for i in range(n)]
        for cp in mine:
            cp.start()
        for i in range(n):
            for j in range(3):
                copy(i, j).start()
        for i in range(n):
            for j in range(3):
                arrival(i, j).wait_recv()
        for i in range(n):
            for j in range(3):
                copy(i, j).wait_send()
        for cp in mine:
            cp.wait()

    return _pc(
        body, "rs_chips",
        in_specs=[_HBM] * n, out_specs=[_HBM] * n,
        out_shape=[SDS(p.shape, p.dtype) for p in ps],
        scratch_shapes=[pltpu.SemaphoreType.DMA((3 * n,)), pltpu.SemaphoreType.DMA((3 * n,)),
                        pltpu.SemaphoreType.DMA((n,))],
    )(*ps)


def _adamw(parts, w, m, v, tr, name):
    n, rows, width = parts.shape
    c1 = 1.0 / (1.0 - ADAM_B1 ** ADAM_STEP)
    c2 = 1.0 / (1.0 - ADAM_B2 ** ADAM_STEP)

    def body(p_ref, w_ref, m_ref, v_ref, g_ref, d_ref, mo_ref, vo_ref):
        g = p_ref[0].astype(f32)
        for k in range(1, n):
            g = g + p_ref[k].astype(f32)
        mn = ADAM_B1 * m_ref[...] + (1.0 - ADAM_B1) * g
        vn = ADAM_B2 * v_ref[...] + (1.0 - ADAM_B2) * (g * g)
        g_ref[...] = g
        mo_ref[...] = mn
        vo_ref[...] = vn
        d_ref[...] = -ADAM_LR * ((mn * c1) / (jnp.sqrt(vn * c2) + ADAM_EPS) + ADAM_WD * w_ref[...])

    blk = pl.BlockSpec((tr, width), lambda i: (i, 0))
    return _pc(
        body, name, grid=(rows // tr,),
        in_specs=[pl.BlockSpec((n, tr, width), lambda i: (0, i, 0)), blk, blk, blk],
        out_specs=[blk, blk, blk, blk],
        out_shape=[SDS((rows, width), f32)] * 4,
        compiler_params=_params(("arbitrary",)),
    )(parts, w, m, v)


ROWS_REST = ROWS_SSM_OUT + ROWS_ATT_OUT + 16


def _pack_rest(ssm_w_out, att_w_out, conv_w):
    conv = jnp.pad(conv_w.reshape(4, 1024), ((0, 12), (0, 0)))
    return jnp.concatenate([ssm_w_out.reshape(ROWS_SSM_OUT, 1024), att_w_out.reshape(ROWS_ATT_OUT, 1024), conv], axis=0)


def _unpack_rest(p):
    o = ROWS_SSM_OUT + ROWS_ATT_OUT
    return (p[0:ROWS_SSM_OUT].reshape(2, 256, 1024), p[ROWS_SSM_OUT:o].reshape(2, 128, 1024),
            p[o:o + 4].reshape(2, 4, 512))


def _pack_grads(d_ssm_w_in, d_ssm_w_out, d_att_w_in, d_att_w_out, d_conv_w):
    wire = lambda t: t.astype(MXU_DTYPE)
    a = jnp.transpose(wire(d_ssm_w_in).reshape(2, 1024, 8, 772), (2, 0, 1, 3)).reshape(8, 2048, 772)
    c = jnp.transpose(wire(d_att_w_in).reshape(2, 1024, 8, 320), (2, 0, 1, 3)).reshape(8, 2048, 320)
    b = jnp.transpose(wire(d_ssm_w_out).reshape(2, 8, 256, 1024), (1, 0, 2, 3)).reshape(8, ROWS_SSM_OUT, 1024)
    d = jnp.transpose(wire(d_att_w_out).reshape(2, 8, 128, 1024), (1, 0, 2, 3)).reshape(8, ROWS_ATT_OUT, 1024)
    e = jnp.transpose(wire(d_conv_w).reshape(2, 4, 8, 512), (2, 0, 1, 3)).reshape(8, 4, 1024)
    e = jnp.pad(e, ((0, 0), (0, 12), (0, 0)))
    return a, c, jnp.concatenate([b, d, e], axis=1)


def _pad8(a):
    return jnp.pad(a, ((0, 8 - a.shape[0]), (0, 0)))


def _pack_small(pre_norm, post_norm, conv_b, gate_norm, dt_bias, a_log, d_skip, sinks, extra=None):
    row = jnp.concatenate([dt_bias.reshape(1, 64), a_log.reshape(1, 64), d_skip.reshape(1, 64), sinks.reshape(1, 32),
                           jnp.zeros((1, 1024 - 224), f32)], axis=1)
    if extra is not None:
        row = row + jnp.pad(extra.reshape(1, 1), ((0, 0), (224, 1024 - 225)))
    return jnp.concatenate([_pad8(pre_norm.reshape(4, 1024)), _pad8(post_norm.reshape(4, 1024)),
                            conv_b.reshape(8, 1024), _pad8(gate_norm.reshape(4, 1024)), _pad8(row)], axis=0)


def _unpack_small(p):
    row = p[32]
    return (p[0:4], p[8:12], p[16:24].reshape(2, 4096), row[0:64].reshape(2, 32), row[64:128].reshape(2, 32),
            row[128:192].reshape(2, 32), p[24:28].reshape(2, 2048), row[192:224].reshape(2, 16))


def _ssm_w_in_tiles(w):
    wb = w[:, 4096:5120].reshape(1024, 8, 128)
    wc = w[:, 5120:6144].reshape(1024, 8, 128)
    wbc = jnp.concatenate([wb, wc], axis=2).reshape(1024, 2048)
    return jnp.concatenate([w[:, 0:4096], wbc, w[:, 6144:6176], jnp.zeros((1024, 224), w.dtype)], axis=1)


def _ssm_w_in_untile(dw):
    dbc = dw[:, 4096:6144].reshape(1024, 8, 256)
    return jnp.concatenate([dw[:, 0:4096], dbc[:, :, 0:128].reshape(1024, 1024), dbc[:, :, 128:256].reshape(1024, 1024),
                            dw[:, 6144:6176]], axis=1)


def _conv_tiles(cw):
    k = cw.shape[0]
    xs = jnp.transpose(cw[:, 0:2048].reshape(k, 8, 256), (1, 0, 2))
    b = cw[:, 2048:3072].reshape(k, 8, 128)
    c = cw[:, 3072:4096].reshape(k, 8, 128)
    bc = jnp.transpose(jnp.concatenate([b, c], axis=2), (1, 0, 2))
    return jnp.concatenate([xs, bc], axis=0)


def _conv_untile(t):
    k = t.shape[1]
    xs = jnp.transpose(t[0:8], (1, 0, 2)).reshape(k, 2048)
    bc = jnp.transpose(t[8:16], (1, 0, 2))
    return jnp.concatenate([xs, bc[:, :, 0:128].reshape(k, 1024), bc[:, :, 128:256].reshape(k, 1024)], axis=1)


def _rope_tables(positions):
    inv = ROPE_THETA ** (-jnp.arange(0, 16, 2, dtype=f32) / 16)
    ang = positions.astype(f32).reshape(-1, 1) * inv
    cos, sin = jnp.cos(ang), jnp.sin(ang)
    t_len = ang.shape[0]
    cos64 = jnp.concatenate([cos, cos, jnp.ones((t_len, 48), f32)], axis=1)
    sin64 = jnp.concatenate([-sin, sin, jnp.zeros((t_len, 48), f32)], axis=1)
    return jnp.tile(cos64, (1, 4)), jnp.tile(sin64, (1, 4))


def kernel(x, positions, pre_norm, post_norm, ssm_w_in, ssm_conv_w, ssm_conv_b, ssm_dt_bias, ssm_a_log, ssm_d, ssm_gate_norm, ssm_w_out, att_w_in, att_sinks, att_w_out, loss_target, m_pre_norm, m_post_norm, m_ssm_w_in, m_ssm_conv_w, m_ssm_conv_b, m_ssm_dt_bias, m_ssm_a_log, m_ssm_d, m_ssm_gate_norm, m_ssm_w_out, m_att_w_in, m_att_sinks, m_att_w_out, v_pre_norm, v_post_norm, v_ssm_w_in, v_ssm_conv_w, v_ssm_conv_b, v_ssm_dt_bias, v_ssm_a_log, v_ssm_d, v_ssm_gate_norm, v_ssm_w_out, v_att_w_in, v_att_sinks, v_att_w_out):
    t_len = x.shape[1]
    tm = min(1024, t_len)
    xin = x.reshape(t_len, D_MODEL)
    tgt = loss_target.reshape(t_len, D_MODEL)
    cidx = lax.axis_index("c").astype(jnp.int32).reshape(1)

    g_ssm_in, g_att_in, g_ssm_out, g_att_out = _all_gather_big(
        [ssm_w_in.astype(MXU_DTYPE), att_w_in.astype(MXU_DTYPE), ssm_w_out.astype(MXU_DTYPE),
         att_w_out.astype(MXU_DTYPE)])
    conv_local = jnp.concatenate([ssm_conv_w.reshape(4, 1024), jnp.zeros((4, 1024), f32)], axis=0)
    conv_all = _all_gather_direct(conv_local, "all_gather_conv")[:, 0:4]
    w_ssm_in = jnp.transpose(g_ssm_in, (1, 2, 0, 3)).reshape(2, 1024, SSM_IN)
    w_ssm_out = jnp.transpose(g_ssm_out, (1, 0, 2, 3)).reshape(2, SSM_INNER, 1024)
    w_att_in = jnp.transpose(g_att_in, (1, 2, 0, 3)).reshape(2, 1024, ATT_IN)
    w_att_out = jnp.transpose(g_att_out, (1, 0, 2, 3)).reshape(2, 1024, 1024)
    conv_w = jnp.transpose(conv_all.reshape(8, 2, 4, 512), (1, 2, 0, 3)).reshape(2, 4, 4096)
    cos_t, sin_t = _rope_tables(positions)

    saved = []
    xc = xin
    for i in range(4):
        j = i // 2
        wn_pre, wn_post = pre_norm[i].reshape(1, D_MODEL), post_norm[i].reshape(1, D_MODEL)
        if i % 2 == 0:
            w_in = _ssm_w_in_tiles(w_ssm_in[j])
            cw, cb = _conv_tiles(conv_w[j]), _conv_tiles(ssm_conv_b[j].reshape(1, 4096))
            dtb, alog, dsk = ssm_dt_bias[j].reshape(1, 32), ssm_a_log[j].reshape(1, 32), ssm_d[j].reshape(1, 32)
            gn = ssm_gate_norm[j].reshape(SSM_KT, 1, TILE)
            w_zdt = jnp.concatenate([w_in[:, 0:SSM_INNER], w_in[:, 24 * TILE:25 * TILE]], axis=1)
            u, h = _mm_in(xc, wn_pre, w_zdt, 3, tm, f"ssm_in_{j}")
            uc, xbc, dsl = _mm_in_conv(xc, wn_pre, w_in[:, SSM_INNER:24 * TILE], cw, cb, tm, f"ssm_inconv_{j}")
            pre = _ssm_pre(u, dtb, alog, dsk, f"ssm_pre_{j}")
            a3, yp, hst = _ssm2_fwd(u, xbc, pre, dtb, alog, dsk, gn, f"ssm_core_{j}")
            y, xn = _mm_out(a3, w_ssm_out[j], xc, wn_post, 4, tm, f"ssm_out_{j}")
            saved.append(dict(x=xc, u=u, uc=uc, dsl=dsl, h=h, a=a3, yp=yp, hst=hst, y=y, w_in=w_in, cw=cw, dtb=dtb,
                              alog=alog, dsk=dsk, gn=gn, xbc=xbc, pre=pre))
        else:
            sinks = att_sinks[j].reshape(1, 16)
            u, h = _mm_in(xc, wn_pre, w_att_in[j], 5, tm, f"att_in_{j}")
            a = _att2_fwd(u, cos_t, sin_t, sinks, f"att_core_{j}")
            y, xn = _mm_out(a, w_att_out[j], xc, wn_post, 4, tm, f"att_out_{j}")
            saved.append(dict(x=xc, u=u, h=h, a=a, y=y, sinks=sinks))
        xc = xn

    dx, loss_part = _loss_grad(xc, tgt, tm)

    d_pre, d_post = [None] * 4, [None] * 4
    d_ssm_in, d_ssm_out, d_att_in, d_att_out = [None] * 2, [None] * 2, [None] * 2, [None] * 2
    d_cw, d_cb, d_dtb, d_alog, d_dsk, d_gn, d_sinks = ([None] * 2 for _ in range(7))
    for i in reversed(range(4)):
        j = i // 2
        s = saved[i]
        wn_pre, wn_post = pre_norm[i].reshape(1, D_MODEL), post_norm[i].reshape(1, D_MODEL)
        if i % 2 == 0:
            da3, dy, d_post[i] = _mm_dout(s["y"], dx, wn_post, w_ssm_out[j], 4, tm, f"ssm_dout_{j}")
            d_ssm_out[j] = _dw_rows(s["a"], dy, 4, tm, f"ssm_dwout_{j}")
            du, d_dtb[j], d_alog[j], d_dsk[j], dgn = _ssm2_bwd(
                s["u"], s["xbc"], s["pre"], s["yp"], s["hst"], da3, s["dtb"], s["alog"], s["dsk"], s["gn"],
                f"ssm_core_bwd_{j}")
            du, dcw, dcb = _conv_bwd(s["uc"], s["dsl"], du, s["cw"], f"ssm_conv_bwd_{j}")
            d_cw[j], d_cb[j], d_gn[j] = _conv_untile(dcw), _conv_untile(dcb), dgn.reshape(1, SSM_INNER)
            d_ssm_in[j] = _ssm_w_in_untile(_dw_cols(s["h"], du, 5, tm, f"ssm_dwin_{j}"))
            dx, d_pre[i] = _mm_dh(du, s["w_in"], s["x"], dx, wn_pre, 5, tm, f"ssm_dh_{j}")
        else:
            da, dy, d_post[i] = _mm_dout(s["y"], dx, wn_post, w_att_out[j], 4, tm, f"att_dout_{j}")
            d_att_out[j] = _dw_rows(s["a"], dy, 4, tm, f"att_dwout_{j}")
            du, d_sinks[j] = _att2_bwd(s["u"], cos_t, sin_t, s["sinks"], da, f"att_core_bwd_{j}")
            d_att_in[j] = _dw_cols(s["h"], du, 5, tm, f"att_dwin_{j}")
            dx, d_pre[i] = _mm_dh(du, w_att_in[j], s["x"], dx, wn_pre, 5, tm, f"att_dh_{j}")

    gs = _pack_grads(jnp.stack(d_ssm_in), jnp.stack(d_ssm_out), jnp.stack(d_att_in), jnp.stack(d_att_out),
                     jnp.stack(d_cw))
    r1 = _exchange_sibling(gs)
    tiles = (256, 256, ROWS_REST // 7)
    pairs = [_pair_sum(g, r, cidx, tr, f"rs_pair_sum_{k}") for k, (g, r, tr) in enumerate(zip(gs, r1, tiles))]
    parts = _exchange_chips(pairs)
    flat = lambda t: t.reshape(2048, t.shape[-1])
    a4 = _adamw(parts[0], flat(ssm_w_in), flat(m_ssm_w_in), flat(v_ssm_w_in), tiles[0], "adamw_ssm_in")
    b4 = _adamw(parts[1], flat(att_w_in), flat(m_att_w_in), flat(v_att_w_in), tiles[1], "adamw_att_in")
    c4 = _adamw(parts[2], _pack_rest(ssm_w_out, att_w_out, ssm_conv_w), _pack_rest(m_ssm_w_out, m_att_w_out, m_ssm_conv_w),
                _pack_rest(v_ssm_w_out, v_att_w_out, v_ssm_conv_w), tiles[2], "adamw_rest")
    big = []
    for k in range(4):
        o_ssm_out, o_att_out, o_conv = _unpack_rest(c4[k])
        big.append((a4[k].reshape(2, 1024, 772), o_ssm_out, b4[k].reshape(2, 1024, 320), o_att_out, o_conv))

    small_local = _pack_small(jnp.concatenate(d_pre, axis=0), jnp.concatenate(d_post, axis=0),
                              jnp.concatenate(d_cb, axis=0), jnp.concatenate(d_gn, axis=0),
                              jnp.concatenate(d_dtb, axis=0), jnp.concatenate(d_alog, axis=0),
                              jnp.concatenate(d_dsk, axis=0), jnp.concatenate(d_sinks, axis=0), loss_part[0, 0])
    small_all = _all_gather_direct(small_local, "all_gather_small")
    ws = _pack_small(pre_norm, post_norm, ssm_conv_b, ssm_gate_norm, ssm_dt_bias, ssm_a_log, ssm_d, att_sinks)
    ms = _pack_small(m_pre_norm, m_post_norm, m_ssm_conv_b, m_ssm_gate_norm, m_ssm_dt_bias, m_ssm_a_log, m_ssm_d,
                     m_att_sinks)
    vs = _pack_small(v_pre_norm, v_post_norm, v_ssm_conv_b, v_ssm_gate_norm, v_ssm_dt_bias, v_ssm_a_log, v_ssm_d,
                     v_att_sinks)
    small4 = _adamw(small_all, ws, ms, vs, ROWS_SMALL, "adamw_small")
    loss = small4[0][32, 224]
    small = [_unpack_small(t) for t in small4]

    outs = [loss, dx.reshape(1, t_len, D_MODEL)]
    for k in range(4):
        b_ssm_in, b_ssm_out, b_att_in, b_att_out, b_conv = big[k]
        s_pre, s_post, s_cb, s_dtb, s_alog, s_d, s_gn, s_sinks = small[k]
        outs += [s_pre, s_post, b_ssm_in, b_conv, s_cb, s_dtb, s_alog, s_d, s_gn, b_ssm_out, b_att_in, s_sinks,
                 b_att_out]
    return tuple(outs)
```

```python
import jax
import jax.numpy as jnp
from jax import lax
from jax.experimental import pallas as pl
from jax.experimental.pallas import tpu as pltpu

f32 = jnp.float32
MXU_DTYPE = jnp.bfloat16
SDS = jax.ShapeDtypeStruct
MESH = pl.DeviceIdType.MESH

D_MODEL = 1024
EPS = 1e-6
TILE = 256
CHUNK = 128
ATT_ROWS = 4 * CHUNK
SSM_HEADS = 32
SSM_GROUPS = 8
SSM_P = 64
SSM_N = 128
SSM_INNER = 2048
SSM_IN = 6176
SSM_NT = 25
SSM_KT = 8
ATT_NT = 10
ATT_KT = 4
ATT_IN = 2560
ROPE_THETA = 500000.0
N_DEV = 8
VMEM_LIMIT = 56 * 1024 * 1024

ADAM_LR = 0.001
ADAM_B1 = 0.9
ADAM_B2 = 0.999
ADAM_EPS = 1e-08
ADAM_WD = 0.01
ADAM_STEP = 10

ROWS_SSM_OUT = 2 * 256
ROWS_ATT_OUT = 2 * 128
ROWS_SMALL = 40


def _pc(body, name, **kw):
    return pl.pallas_call(body, name=name, **kw)


def _params(sem):
    return pltpu.CompilerParams(dimension_semantics=sem, vmem_limit_bytes=VMEM_LIMIT)


def _sigmoid(x):
    return 0.5 * jnp.tanh(0.5 * x) + 0.5


def _silu(x):
    return x * _sigmoid(x)


def _softplus(x):
    return jnp.maximum(x, 0.0) + jnp.log(1.0 + jnp.exp(-jnp.abs(x)))


def _mx(x):
    return x.astype(MXU_DTYPE)


def _dot(a, b):
    return jnp.dot(_mx(a), _mx(b), preferred_element_type=f32)


def _dot_nt(a, b):
    return lax.dot_general(_mx(a), _mx(b), (((1,), (1,)), ((), ())), preferred_element_type=f32)


def _dot_tn(a, b):
    return lax.dot_general(_mx(a), _mx(b), (((0,), (0,)), ((), ())), preferred_element_type=f32)


def _rms_fwd(x, w):
    r = lax.rsqrt(jnp.mean(x * x, axis=-1, keepdims=True) + EPS)
    return x * r * w


def _rms_bwd(x, w, dy):
    r = lax.rsqrt(jnp.mean(x * x, axis=-1, keepdims=True) + EPS)
    xh = x * r
    dw = jnp.sum(dy * xh, axis=0, keepdims=True)
    g = dy * w
    dx = r * (g - xh * jnp.mean(g * xh, axis=-1, keepdims=True))
    return dx, dw


def _mm_in(x, wn, w, ntb, tm, name):
    t_len, d = x.shape
    nt = w.shape[1] // TILE

    def body(x_ref, wn_ref, w_ref, u_ref, h_ref):
        @pl.when(pl.program_id(1) == 0)
        def _():
            h_ref[...] = _rms_fwd(x_ref[...], wn_ref[...]).astype(h_ref.dtype)
        h = h_ref[...]
        for t in range(ntb):
            u_ref[t] = jnp.dot(h, w_ref[:, TILE * t:TILE * (t + 1)], preferred_element_type=f32)

    return _pc(
        body, name, grid=(t_len // tm, nt // ntb),
        in_specs=[pl.BlockSpec((tm, d), lambda i, j: (i, 0)),
                  pl.BlockSpec((1, d), lambda i, j: (0, 0)),
                  pl.BlockSpec((d, ntb * TILE), lambda i, j: (0, j))],
        out_specs=[pl.BlockSpec((ntb, tm, TILE), lambda i, j: (j, i, 0)),
                   pl.BlockSpec((tm, d), lambda i, j: (i, 0))],
        out_shape=[SDS((nt, t_len, TILE), f32), SDS((t_len, d), MXU_DTYPE)],
        compiler_params=_params(("arbitrary", "arbitrary")),
    )(x, wn, w)


def _mm_dout(y, dxn, wn, w, ntb, tm, name):
    t_len, d = y.shape
    nt = w.shape[0] // TILE

    def body(y_ref, dxn_ref, wn_ref, w_ref, da_ref, dy_ref, dwn_ref):
        i, j = pl.program_id(0), pl.program_id(1)

        @pl.when((i == 0) & (j == 0))
        def _():
            dwn_ref[...] = jnp.zeros_like(dwn_ref)

        @pl.when(j == 0)
        def _():
            dy, dw = _rms_bwd(y_ref[...], wn_ref[...], dxn_ref[...])
            dy_ref[...] = dy.astype(dy_ref.dtype)
            dwn_ref[...] += dw
        dy = dy_ref[...]
        for t in range(ntb):
            da_ref[t] = _dot_nt(dy, w_ref[TILE * t:TILE * (t + 1), :])

    return _pc(
        body, name, grid=(t_len // tm, nt // ntb),
        in_specs=[pl.BlockSpec((tm, d), lambda i, j: (i, 0)),
                  pl.BlockSpec((tm, d), lambda i, j: (i, 0)),
                  pl.BlockSpec((1, d), lambda i, j: (0, 0)),
                  pl.BlockSpec((ntb * TILE, d), lambda i, j: (j, 0))],
        out_specs=[pl.BlockSpec((ntb, tm, TILE), lambda i, j: (j, i, 0)),
                   pl.BlockSpec((tm, d), lambda i, j: (i, 0)),
                   pl.BlockSpec((1, d), lambda i, j: (0, 0))],
        out_shape=[SDS((nt, t_len, TILE), f32), SDS((t_len, d), MXU_DTYPE), SDS((1, d), f32)],
        compiler_params=_params(("arbitrary", "arbitrary")),
    )(y, dxn, wn, w)


def _mm_out(a, w, x, wn, ktb, tm, name):
    kt, t_len, _ = a.shape
    d = w.shape[1]
    nk = kt // ktb

    def body(a_ref, w_ref, x_ref, wn_ref, y_ref, xn_ref, acc):
        k = pl.program_id(1)

        @pl.when(k == 0)
        def _():
            acc[...] = jnp.zeros_like(acc)
        s = acc[...]
        for t in range(ktb):
            s = s + jnp.dot(a_ref[t], w_ref[TILE * t:TILE * (t + 1), :], preferred_element_type=f32)
        acc[...] = s

        @pl.when(k == nk - 1)
        def _():
            y = acc[...]
            y_ref[...] = y
            xn_ref[...] = x_ref[...] + _rms_fwd(y, wn_ref[...])

    return _pc(
        body, name, grid=(t_len // tm, nk),
        in_specs=[pl.BlockSpec((ktb, tm, TILE), lambda i, k: (k, i, 0)),
                  pl.BlockSpec((ktb * TILE, d), lambda i, k: (k, 0)),
                  pl.BlockSpec((tm, d), lambda i, k: (i, 0)),
                  pl.BlockSpec((1, d), lambda i, k: (0, 0))],
        out_specs=[pl.BlockSpec((tm, d), lambda i, k: (i, 0)),
                   pl.BlockSpec((tm, d), lambda i, k: (i, 0))],
        out_shape=[SDS((t_len, d), f32), SDS((t_len, d), f32)],
        scratch_shapes=[pltpu.VMEM((tm, d), f32)],
        compiler_params=_params(("arbitrary", "arbitrary")),
    )(a, w, x, wn)


def _mm_dh(du, w, x, dxn, wn, ktb, tm, name):
    kt, t_len, _ = du.shape
    d = w.shape[0]
    nk = kt // ktb

    def body(du_ref, w_ref, x_ref, dxn_ref, wn_ref, dx_ref, dwn_ref, acc):
        i, k = pl.program_id(0), pl.program_id(1)

        @pl.when((i == 0) & (k == 0))
        def _():
            dwn_ref[...] = jnp.zeros_like(dwn_ref)

        @pl.when(k == 0)
        def _():
            acc[...] = jnp.zeros_like(acc)
        s = acc[...]
        for t in range(ktb):
            s = s + _dot_nt(du_ref[t], w_ref[:, TILE * t:TILE * (t + 1)])
        acc[...] = s

        @pl.when(k == nk - 1)
        def _():
            dxp, dw = _rms_bwd(x_ref[...], wn_ref[...], acc[...])
            dx_ref[...] = dxn_ref[...] + dxp
            dwn_ref[...] += dw

    return _pc(
        body, name, grid=(t_len // tm, nk),
        in_specs=[pl.BlockSpec((ktb, tm, TILE), lambda i, k: (k, i, 0)),
                  pl.BlockSpec((d, ktb * TILE), lambda i, k: (0, k)),
                  pl.BlockSpec((tm, d), lambda i, k: (i, 0)),
                  pl.BlockSpec((tm, d), lambda i, k: (i, 0)),
                  pl.BlockSpec((1, d), lambda i, k: (0, 0))],
        out_specs=[pl.BlockSpec((tm, d), lambda i, k: (i, 0)),
                   pl.BlockSpec((1, d), lambda i, k: (0, 0))],
        out_shape=[SDS((t_len, d), f32), SDS((1, d), f32)],
        scratch_shapes=[pltpu.VMEM((tm, d), f32)],
        compiler_params=_params(("arbitrary", "arbitrary")),
    )(du, w, x, dxn, wn)


def _dw_cols(a, b, ntb, tk, name):
    t_len, kdim = a.shape
    nt = b.shape[0]

    def body(a_ref, b_ref, o_ref):
        @pl.when(pl.program_id(1) == 0)
        def _():
            o_ref[...] = jnp.zeros_like(o_ref)
        av = a_ref[...]
        for s in range(ntb):
            o_ref[:, TILE * s:TILE * (s + 1)] += _dot_tn(av, b_ref[s])

    return _pc(
        body, name, grid=(nt // ntb, t_len // tk),
        in_specs=[pl.BlockSpec((tk, kdim), lambda j, t: (t, 0)),
                  pl.BlockSpec((ntb, tk, TILE), lambda j, t: (j, t, 0))],
        out_specs=pl.BlockSpec((kdim, ntb * TILE), lambda j, t: (0, j)),
        out_shape=SDS((kdim, nt * TILE), f32),
        compiler_params=_params(("arbitrary", "arbitrary")),
    )(a, b)


def _dw_rows(a, b, ktb, tk, name):
    kt, t_len, _ = a.shape
    d = b.shape[1]

    def body(a_ref, b_ref, o_ref):
        @pl.when(pl.program_id(1) == 0)
        def _():
            o_ref[...] = jnp.zeros_like(o_ref)
        bv = b_ref[...]
        for s in range(ktb):
            o_ref[TILE * s:TILE * (s + 1), :] += _dot_tn(a_ref[s], bv)

    return _pc(
        body, name, grid=(kt // ktb, t_len // tk),
        in_specs=[pl.BlockSpec((ktb, tk, TILE), lambda k, t: (k, t, 0)),
                  pl.BlockSpec((tk, d), lambda k, t: (t, 0))],
        out_specs=pl.BlockSpec((ktb * TILE, d), lambda k, t: (k, 0)),
        out_shape=SDS((kt * TILE, d), f32),
        compiler_params=_params(("arbitrary", "arbitrary")),
    )(a, b)


def _loss_grad(x, tgt, tm):
    t_len, d = x.shape

    def body(x_ref, t_ref, dx_ref, l_ref):
        @pl.when(pl.program_id(0) == 0)
        def _():
            l_ref[...] = jnp.zeros_like(l_ref)
        e = x_ref[...] - t_ref[...]
        dx_ref[...] = e * (1.0 / d)
        row = jnp.mean(e * e, axis=-1, keepdims=True)
        l_ref[...] += 0.5 * jnp.sum(row, axis=0, keepdims=True)

    return _pc(
        body, "loss_grad", grid=(t_len // tm,),
        in_specs=[pl.BlockSpec((tm, d), lambda i: (i, 0)), pl.BlockSpec((tm, d), lambda i: (i, 0))],
        out_specs=[pl.BlockSpec((tm, d), lambda i: (i, 0)), pl.BlockSpec((1, 128), lambda i: (0, 0))],
        out_shape=[SDS((t_len, d), f32), SDS((1, 128), f32)],
        compiler_params=_params(("arbitrary",)),
    )(x, tgt)


def _tri(lower):
    r = lax.broadcasted_iota(jnp.int32, (CHUNK, CHUNK), 0)
    c = lax.broadcasted_iota(jnp.int32, (CHUNK, CHUNK), 1)
    return ((c <= r) if lower else (c >= r)).astype(f32)


def _split(x, n):
    parts = []
    for _ in range(n):
        p = x.astype(jnp.bfloat16)
        parts.append(p)
        x = x - p.astype(f32)
    return parts


def _dot_exact(a, b, dims, split_a, n=3):
    out = None
    if split_a:
        b = b.astype(jnp.bfloat16)
        for p in _split(a, n):
            t = lax.dot_general(p, b, (dims, ((), ())), preferred_element_type=f32)
            out = t if out is None else out + t
    else:
        a = a.astype(jnp.bfloat16)
        for p in _split(b, n):
            t = lax.dot_general(a, p, (dims, ((), ())), preferred_element_type=f32)
            out = t if out is None else out + t
    return out


def _dt_path(dt_raw, dtb, alog):
    dtr = dt_raw + dtb
    dt = _softplus(dtr)
    a_neg = -jnp.exp(alog)
    a = dt * a_neg
    acs = _dot_exact(_tri(True), a, ((1,), (0,)), False)
    acs_t = _dot_exact(a, _tri(False), ((0,), (0,)), True)
    return dtr, dt, a_neg, acs, acs_t


CONV_ROWS = 1024
CONV_SUB = 64


CONV_NTB = 4


def _mm_in_conv(x, wn, w, cw, cb, tm, name):
    t_len, d = x.shape
    nt = w.shape[1] // TILE

    def body(x_ref, wn_ref, w_ref, cw_ref, cb_ref, u_ref, o_ref, ds_ref, h_s, carry_s, win_s):
        i, j = pl.program_id(0), pl.program_id(1)

        @pl.when(j == 0)
        def _():
            h_s[...] = _rms_fwd(x_ref[...], wn_ref[...]).astype(h_s.dtype)
        h = h_s[...]
        for t in range(CONV_NTB):
            p = CONV_NTB * j + t
            ut = jnp.dot(h, w_ref[:, TILE * t:TILE * (t + 1)], preferred_element_type=f32)
            u_ref[t] = ut
            win_s[t, 0:8, :] = jnp.where(i > 0, carry_s[p], 0.0)
            win_s[t, 8:8 + tm, :] = ut
            carry_s[p] = ut[tm - 8:tm, :]
            wk = [cw_ref[t, k:k + 1, :] for k in range(4)]
            b = cb_ref[t]
            for s in range(tm // CONV_SUB):
                o = CONV_SUB * s
                acc = b
                for k in range(4):
                    acc = acc + wk[k] * win_s[t, 5 + k + o:5 + k + o + CONV_SUB, :]
                sg = _sigmoid(acc)
                o_ref[t, o:o + CONV_SUB, :] = acc * sg
                ds_ref[t, o:o + CONV_SUB, :] = sg * (1.0 + acc * (1.0 - sg))

    return _pc(
        body, name, grid=(t_len // tm, nt // CONV_NTB),
        in_specs=[pl.BlockSpec((tm, d), lambda i, j: (i, 0)),
                  pl.BlockSpec((1, d), lambda i, j: (0, 0)),
                  pl.BlockSpec((d, CONV_NTB * TILE), lambda i, j: (0, j)),
                  pl.BlockSpec((CONV_NTB, 4, TILE), lambda i, j: (j, 0, 0)),
                  pl.BlockSpec((CONV_NTB, 1, TILE), lambda i, j: (j, 0, 0))],
        out_specs=[pl.BlockSpec((CONV_NTB, tm, TILE), lambda i, j: (j, i, 0))] * 3,
        out_shape=[SDS((nt, t_len, TILE), f32)] * 3,
        scratch_shapes=[pltpu.VMEM((tm, d), MXU_DTYPE), pltpu.VMEM((nt, 8, TILE), f32),
                        pltpu.VMEM((CONV_NTB, 8 + tm, TILE), f32)],
        compiler_params=_params(("arbitrary", "arbitrary")),
    )(x, wn, w, cw, cb)


def _conv_bwd(u, dsl, du, cw, name):
    t_len = u.shape[1]
    rows = min(CONV_ROWS, t_len)
    nb = t_len // rows

    def body(u_ref, ds_ref, cw_ref, d_ref, o_ref, dcw_ref, dcb_ref, carry_s, dp_s):
        i = pl.program_id(1)

        @pl.when(i == 0)
        def _():
            carry_s[...] = jnp.zeros_like(carry_s)
            dcw_ref[...] = jnp.zeros_like(dcw_ref)
            dcb_ref[...] = jnp.zeros_like(dcb_ref)
        w = [cw_ref[0, k:k + 1, :] for k in range(4)]

        def fold(v):
            return jnp.sum(v.reshape(CONV_SUB // 8, 8, TILE), axis=0)

        dw = [jnp.zeros((8, TILE), f32)] * 4
        db = jnp.zeros((8, TILE), f32)
        for s in range(rows // CONV_SUB):
            o = CONV_SUB * s
            dpre = d_ref[0, o:o + CONV_SUB, :] * ds_ref[0, o:o + CONV_SUB, :]
            dp_s[o:o + CONV_SUB, :] = dpre
            db = db + fold(dpre)
        dp_s[rows:rows + 8, :] = carry_s[...]
        for s in range(rows // CONV_SUB):
            o = CONV_SUB * s
            xs = u_ref[0, o:o + CONV_SUB, :]
            acc = jnp.zeros((CONV_SUB, TILE), f32)
            for k in range(4):
                win = dp_s[3 - k + o:3 - k + o + CONV_SUB, :]
                acc = acc + w[k] * win
                dw[k] = dw[k] + fold(win * xs)
            o_ref[0, o:o + CONV_SUB, :] = acc
        carry_s[...] = dp_s[0:8, :]
        for k in range(4):
            dcw_ref[0, k:k + 1, :] += jnp.sum(dw[k], axis=0, keepdims=True)
        dcb_ref[0] += jnp.sum(db, axis=0, keepdims=True)

    def blk(off):
        return pl.BlockSpec((1, rows, TILE), lambda p, i: (off + p, nb - 1 - i, 0))

    return _pc(
        body, name, grid=(16, nb),
        in_specs=[blk(0), blk(0), pl.BlockSpec((1, 4, TILE), lambda p, i: (p, 0, 0)), blk(8)],
        out_specs=[blk(8),
                   pl.BlockSpec((1, 4, TILE), lambda p, i: (p, 0, 0)),
                   pl.BlockSpec((1, 1, TILE), lambda p, i: (p, 0, 0))],
        out_shape=[SDS(du.shape, f32), SDS((16, 4, TILE), f32), SDS((16, 1, TILE), f32)],
        input_output_aliases={3: 0},
        scratch_shapes=[pltpu.VMEM((8, TILE), f32), pltpu.VMEM((rows + 8, TILE), f32)],
        compiler_params=_params(("arbitrary", "arbitrary")),
    )(u, dsl, cw, du)


def _collapse_matrix(g):
    r = lax.broadcasted_iota(jnp.int32, (SSM_HEADS, TILE), 0)
    c = lax.broadcasted_iota(jnp.int32, (SSM_HEADS, TILE), 1)
    return ((c // SSM_P) + 4 * g == r).astype(jnp.bfloat16)


def _ssd_prelude(dt_raw, dtb, alog, dsk, colx_s, scx_s):
    dtr, dt, a_neg, acs, acs_t = _dt_path(dt_raw, dtb, alog)
    a_end = acs[CHUNK - 1:CHUNK, :]
    lane = lax.broadcasted_iota(jnp.int32, (1, 2 * SSM_P), 1)
    lane4 = lax.broadcasted_iota(jnp.int32, (1, TILE), 1)
    sub8 = lax.broadcasted_iota(jnp.int32, (8, 1), 0)

    def row4(v, g):
        e = [v[:, 4 * g + r:4 * g + r + 1] for r in range(4)]
        return jnp.where(lane4 < 64, e[0], jnp.where(lane4 < 128, e[1], jnp.where(lane4 < 192, e[2], e[3])))

    for g in range(SSM_GROUPS):
        for k, arr in enumerate((dt, acs)):
            for half in range(2):
                h0 = 4 * g + 2 * half
                colx_s[k, g, :, 128 * half:128 * (half + 1)] = jnp.where(
                    lane < SSM_P, arr[:, h0:h0 + 1], arr[:, h0 + 1:h0 + 2])
        scx_s[g] = jnp.where(sub8 == 0, row4(dsk, g), jnp.where(sub8 == 1, row4(a_end, g), 0.0))
    return dtr, dt, a_neg, acs_t


def _ssm_core_specs(nc, rev):
    def cidx(c):
        return (nc - 1 - c) if rev else c
    return [
        pl.BlockSpec((SSM_KT, CHUNK, TILE), lambda c: (0, cidx(c), 0)),
        pl.BlockSpec((1, CHUNK, TILE), lambda c: (SSM_KT, cidx(c), 0)),
        pl.BlockSpec((16, CHUNK, TILE), lambda c: (0, cidx(c), 0)),
        pl.BlockSpec((1, SSM_HEADS), lambda c: (0, 0)),
        pl.BlockSpec((1, SSM_HEADS), lambda c: (0, 0)),
        pl.BlockSpec((1, SSM_HEADS), lambda c: (0, 0)),
        pl.BlockSpec((SSM_KT, 1, TILE), lambda c: (0, 0, 0)),
        pl.BlockSpec((2, SSM_GROUPS, CHUNK, TILE), lambda c: (0, 0, cidx(c), 0)),
        pl.BlockSpec((1, SSM_GROUPS, 8, TILE), lambda c: (cidx(c), 0, 0, 0)),
        pl.BlockSpec((1, SSM_HEADS, CHUNK), lambda c: (cidx(c), 0, 0)),
    ]


PRE_CHUNKS = 4


def _ssm_pre(u, dtb, alog, dsk, name):
    t_len = u.shape[1]
    nc = t_len // CHUNK
    per = min(PRE_CHUNKS, nc)

    def body(dt_ref, dtb_ref, alog_ref, dsk_ref, colx_ref, scx_ref, acst_ref):
        for c in range(per):
            rows = pl.ds(CHUNK * c, CHUNK)
            _, _, _, acs_t = _ssd_prelude(dt_ref[0, rows, 0:SSM_HEADS], dtb_ref[...], alog_ref[...], dsk_ref[...],
                                          colx_ref.at[:, :, rows, :], scx_ref.at[c])
            acst_ref[c] = acs_t

    vec = pl.BlockSpec((1, SSM_HEADS), lambda i: (0, 0))
    return _pc(
        body, name, grid=(nc // per,),
        in_specs=[pl.BlockSpec((1, per * CHUNK, TILE), lambda i: (SSM_KT, i, 0)), vec, vec, vec],
        out_specs=[pl.BlockSpec((2, SSM_GROUPS, per * CHUNK, TILE), lambda i: (0, 0, i, 0)),
                   pl.BlockSpec((per, SSM_GROUPS, 8, TILE), lambda i: (i, 0, 0, 0)),
                   pl.BlockSpec((per, SSM_HEADS, CHUNK), lambda i: (i, 0, 0))],
        out_shape=[SDS((2, SSM_GROUPS, t_len, TILE), f32), SDS((nc, SSM_GROUPS, 8, TILE), f32),
                   SDS((nc, SSM_HEADS, CHUNK), f32)],
        compiler_params=_params(("arbitrary",)),
    )(u, dtb, alog, dsk)


def _stack_cols_rows(acx, rows):
    ac = jnp.concatenate([acx[:, SSM_P * r:SSM_P * r + 1] for r in range(4)], axis=0)
    ar = jnp.concatenate([jnp.broadcast_to(rows[r:r + 1, :], (CHUNK, CHUNK)) for r in range(4)], axis=0)
    return ac, ar


def _ssm2_fwd(u, xbc, pre, dtb, alog, dsk, gn, name):
    t_len = u.shape[1]
    nc = t_len // CHUNK

    def body(z_ref, dt_ref, x_ref, dtb_ref, alog_ref, dsk_ref, gn_ref, colx_ref, scx_ref, acst_ref,
             a3_ref, yp_ref, hst_ref, h_s, xt_s, yd_s):
        c = pl.program_id(0)

        @pl.when(c == 0)
        def _():
            h_s[...] = jnp.zeros_like(h_s)
        acs_t = acst_ref[0]
        causal = _tri4()

        def group(g, s1):
            xs = x_ref[g]
            bm, cm = x_ref[8 + g, :, 0:SSM_N], x_ref[8 + g, :, SSM_N:2 * SSM_N]
            cb = _dot_nt(cm, bm)
            sc = scx_ref[0, g]
            a_end = sc[1:2, :]
            rows = pltpu.roll(acs_t, (SSM_HEADS - 4 * g) % SSM_HEADS, 0)
            hp = h_s[g]
            xt = xs * colx_ref[0, g]
            xt_s[...] = xt
            acx = colx_ref[1, g]
            ac_st, ar_st = _stack_cols_rows(acx, rows)
            m = jnp.concatenate([cb] * 4, axis=0) * jnp.exp(jnp.where(causal, ac_st - ar_st, -jnp.inf))
            for r in range(4):
                hd = slice(SSM_P * r, SSM_P * (r + 1))
                yd_s[:, hd] = _dot(m[CHUNK * r:CHUNK * (r + 1), :], xt_s[:, hd])
            yp_ref[g] = yd_s[...] + _dot(cm, hp) * jnp.exp(acx) + sc[0:1, :] * xs
            hst_ref[0, g] = hp
            h_s[g] = hp * jnp.exp(a_end) + _dot_tn(bm, xt * jnp.exp(a_end - acx))
            y2 = yp_ref[g] * _silu(z_ref[g])
            return s1 + jnp.sum(y2 * y2, axis=1, keepdims=True)

        s1 = lax.fori_loop(0, SSM_GROUPS // 2, lambda i, c: group(2 * i + 1, group(2 * i, c)),
                           jnp.zeros((CHUNK, 1), f32))
        rinv = lax.rsqrt(s1 * (1.0 / SSM_INNER) + EPS)

        def gate(g, carry):
            y2 = yp_ref[g] * _silu(z_ref[g])
            a3_ref[g] = (y2 * rinv * gn_ref[g]).astype(a3_ref.dtype)
            return carry

        lax.fori_loop(0, SSM_GROUPS, gate, 0)

    return _pc(
        body, name, grid=(nc,),
        in_specs=_ssm_core_specs(nc, False),
        out_specs=[pl.BlockSpec((SSM_KT, CHUNK, TILE), lambda c: (0, c, 0)),
                   pl.BlockSpec((SSM_KT, CHUNK, TILE), lambda c: (0, c, 0)),
                   pl.BlockSpec((1, SSM_GROUPS, SSM_N, TILE), lambda c: (c, 0, 0, 0))],
        out_shape=[SDS((SSM_KT, t_len, TILE), MXU_DTYPE), SDS((SSM_KT, t_len, TILE), f32),
                   SDS((nc, SSM_GROUPS, SSM_N, TILE), f32)],
        scratch_shapes=[pltpu.VMEM((SSM_GROUPS, SSM_N, TILE), f32), pltpu.VMEM((CHUNK, TILE), f32),
                        pltpu.VMEM((CHUNK, TILE), f32)],
        compiler_params=_params(("arbitrary",)),
    )(u, u, xbc, dtb, alog, dsk, gn, *pre)


def _ssm2_bwd(u, xbc, pre, yp, hst, da3, dtb, alog, dsk, gn, name):
    t_len = u.shape[1]
    nc = t_len // CHUNK

    def body(z_ref, dt_ref, x_ref, dtb_ref, alog_ref, dsk_ref, gn_ref, colx_s, scx_ref, acst_ref,
             yp_ref, hst_ref, da3_ref, du_ref, ddtb_ref, dalog_ref, ddsk_ref, dgn_ref,
             dh_s, xt_s, dy_s, dxt_s, ddtx_s, dacx_s, ddx_s, drow_s, dm_s, dmt_s):
        step = pl.program_id(0)

        @pl.when(step == 0)
        def _():
            dh_s[...] = jnp.zeros_like(dh_s)
            ddtb_ref[...] = jnp.zeros_like(ddtb_ref)
            dalog_ref[...] = jnp.zeros_like(dalog_ref)
            ddsk_ref[...] = jnp.zeros_like(ddsk_ref)
            dgn_ref[...] = jnp.zeros_like(dgn_ref)
        dtr = dt_ref[0, :, 0:SSM_HEADS] + dtb_ref[...]
        dt = _softplus(dtr)
        a_neg = -jnp.exp(alog_ref[...])
        acs_t = acst_ref[0]
        causal = _tri4()
        causal_t = (lax.broadcasted_iota(jnp.int32, (ATT_ROWS, CHUNK), 1)
                    >= lax.broadcasted_iota(jnp.int32, (ATT_ROWS, CHUNK), 0) % CHUNK)
        last = (lax.broadcasted_iota(jnp.int32, (1, CHUNK), 1) == CHUNK - 1).astype(f32)
        lane = lax.broadcasted_iota(jnp.int32, (1, TILE), 1)
        sub32 = lax.broadcasted_iota(jnp.int32, (SSM_HEADS, 1), 0)
        drow_s[...] = jnp.zeros_like(drow_s)

        def sums(g, carry):
            s1, s2 = carry
            y2 = yp_ref[g] * _silu(z_ref[g])
            g3 = da3_ref[g] * gn_ref[g]
            return (s1 + jnp.sum(y2 * y2, axis=1, keepdims=True), s2 + jnp.sum(g3 * y2, axis=1, keepdims=True))

        zcol = jnp.zeros((CHUNK, 1), f32)
        carry = (zcol, zcol)
        for g in range(SSM_GROUPS):
            carry = sums(g, carry)
        s1, s2 = carry
        rinv = lax.rsqrt(s1 * (1.0 / SSM_INNER) + EPS)
        m2 = s2 * rinv * rinv * rinv * (1.0 / SSM_INNER)

        def group(g, carry):
            z = z_ref[g]
            sg = _sigmoid(z)
            sz = z * sg
            y = yp_ref[g]
            y2 = y * sz
            da3 = da3_ref[g]
            dgn_ref[g] += jnp.sum(da3 * y2 * rinv, axis=0, keepdims=True)
            dy2 = rinv * (da3 * gn_ref[g]) - y2 * m2
            dy = dy2 * sz
            dy_s[...] = dy
            du_ref[g] = dy2 * y * (sg * (1.0 + z * (1.0 - sg)))
            xs = x_ref[g]
            bm, cm = x_ref[8 + g, :, 0:SSM_N], x_ref[8 + g, :, SSM_N:2 * SSM_N]
            cb = _dot_nt(cm, bm)
            cbt = _dot_nt(bm, cm)
            dtx, acx = colx_s[0, g], colx_s[1, g]
            sc = scx_ref[0, g]
            a_end = sc[1:2, :]
            ex = jnp.exp(acx)
            wdx = jnp.exp(a_end - acx)
            eend = jnp.exp(a_end)
            rows = pltpu.roll(acs_t, (SSM_HEADS - 4 * g) % SSM_HEADS, 0)
            hp = hst_ref[0, g]
            dhn = dh_s[g]
            xt = xs * dtx
            xt_s[...] = xt
            ch = _dot(cm, hp)
            gy = dy * ex
            dcm = _dot_nt(gy, hp)
            dh_s[g] = _dot_tn(cm, gy) + dhn * eend
            q = _dot(bm, dhn)
            dbm = _dot_nt(xt * wdx, dhn)
            qx = q * xt * wdx
            v_end = jnp.sum(dhn * hp, axis=0, keepdims=True) * eend + jnp.sum(qx, axis=0, keepdims=True)
            ac_st, ar_st = _stack_cols_rows(acx, rows)
            lam = jnp.exp(jnp.where(causal, ac_st - ar_st, -jnp.inf))
            lam_t = jnp.exp(jnp.where(causal_t, ar_st - ac_st, -jnp.inf))
            m = jnp.concatenate([cb] * 4, axis=0) * lam
            m_t = jnp.concatenate([cbt] * 4, axis=0) * lam_t
            for r in range(4):
                hd = slice(SSM_P * r, SSM_P * (r + 1))
                rs = slice(CHUNK * r, CHUNK * (r + 1))
                dm_s[rs, :] = _dot_nt(dy_s[:, hd], xt_s[:, hd])
                dmt_s[rs, :] = _dot_nt(xt_s[:, hd], dy_s[:, hd])
                dxt_s[:, hd] = _dot(m_t[rs, :], dy_s[:, hd])
            dm = dm_s[...]
            dl = dm * lam
            dseg = dm * m
            dseg_t = dmt_s[...] * m_t
            dcb = dl[0:CHUNK] + dl[CHUNK:2 * CHUNK] + dl[2 * CHUNK:3 * CHUNK] + dl[3 * CHUNK:4 * CHUNK]
            drows = jnp.zeros((SSM_HEADS, CHUNK), f32)
            for r in range(4):
                rs = slice(CHUNK * r, CHUNK * (r + 1))
                in_head = (lane >= SSM_P * r) & (lane < SSM_P * (r + 1))
                d_ac = jnp.sum(dseg_t[rs, :], axis=0, keepdims=True)
                d_ar = jnp.sum(dseg[rs, :], axis=0, keepdims=True)
                d_aend = jnp.sum(jnp.where(in_head, v_end, 0.0), axis=1, keepdims=True)
                drows = drows + jnp.where(sub32 == r, d_ac - d_ar + last * d_aend, 0.0)
            dxt = dxt_s[...] + q * wdx
            du_ref[8 + g] = sc[0:1, :] * dy + dxt * dtx
            ddtx_s[g] = dxt * xs
            dacx_s[g] = dy * ch * ex - qx
            ddx_s[g] = jnp.broadcast_to(jnp.sum(dy * xs, axis=0, keepdims=True), (8, TILE))
            du_ref[16 + g, :, 0:SSM_N] = dbm + _dot_tn(dcb, cm)
            du_ref[16 + g, :, SSM_N:2 * SSM_N] = dcm + _dot(dcb, bm)
            drow_s[...] += pltpu.roll(drows, (4 * g) % SSM_HEADS, 0)
            return carry

        lax.fori_loop(0, SSM_GROUPS, group, 0, unroll=8)
        ddt = jnp.zeros((CHUNK, SSM_HEADS), f32)
        dacs = jnp.zeros((CHUNK, SSM_HEADS), f32)
        ddsk = jnp.zeros((8, SSM_HEADS), f32)
        for g in range(SSM_GROUPS):
            col_g = _collapse_matrix(g)
            ddt = ddt + _dot_exact(ddtx_s[g], col_g, ((1,), (1,)), True, 2)
            dacs = dacs + _dot_exact(dacx_s[g], col_g, ((1,), (1,)), True, 2)
            ddsk = ddsk + _dot_exact(ddx_s[g], col_g, ((1,), (1,)), True, 2)
        upper = _tri(False)
        da = _dot_exact(upper, dacs, ((1,), (0,)), False) + _dot_exact(upper, drow_s[...], ((1,), (1,)), False)
        ddt = ddt + da * a_neg
        dalog_ref[...] += jnp.sum(da * dt, axis=0, keepdims=True) * a_neg
        ddtr = ddt * _sigmoid(dtr)
        ddtb_ref[...] += jnp.sum(ddtr, axis=0, keepdims=True)
        ddsk_ref[...] += ddsk[0:1, :]
        du_ref[SSM_NT - 1] = jnp.zeros((CHUNK, TILE), f32)
        du_ref[SSM_NT - 1, :, 0:SSM_HEADS] = ddtr

    def rc(c):
        return nc - 1 - c

    vec = pl.BlockSpec((1, SSM_HEADS), lambda c: (0, 0))
    return _pc(
        body, name, grid=(nc,),
        in_specs=_ssm_core_specs(nc, True) + [
            pl.BlockSpec((SSM_KT, CHUNK, TILE), lambda c: (0, rc(c), 0)),
            pl.BlockSpec((1, SSM_GROUPS, SSM_N, TILE), lambda c: (rc(c), 0, 0, 0)),
            pl.BlockSpec((SSM_KT, CHUNK, TILE), lambda c: (0, rc(c), 0))],
        out_specs=[pl.BlockSpec((SSM_NT, CHUNK, TILE), lambda c: (0, rc(c), 0)), vec, vec, vec,
                   pl.BlockSpec((SSM_KT, 1, TILE), lambda c: (0, 0, 0))],
        out_shape=[SDS((SSM_NT, t_len, TILE), f32), SDS((1, SSM_HEADS), f32), SDS((1, SSM_HEADS), f32),
                   SDS((1, SSM_HEADS), f32), SDS((SSM_KT, 1, TILE), f32)],
        scratch_shapes=[pltpu.VMEM((SSM_GROUPS, SSM_N, TILE), f32), pltpu.VMEM((CHUNK, TILE), f32),
                        pltpu.VMEM((CHUNK, TILE), f32), pltpu.VMEM((CHUNK, TILE), f32),
                        pltpu.VMEM((SSM_GROUPS, CHUNK, TILE), f32), pltpu.VMEM((SSM_GROUPS, CHUNK, TILE), f32),
                        pltpu.VMEM((SSM_GROUPS, 8, TILE), f32), pltpu.VMEM((SSM_HEADS, CHUNK), f32),
                        pltpu.VMEM((4 * CHUNK, CHUNK), f32), pltpu.VMEM((4 * CHUNK, CHUNK), f32)],
        compiler_params=_params(("arbitrary",)),
    )(u, u, xbc, dtb, alog, dsk, gn, *pre, yp, hst, da3)


def _swap16(t):
    lane = lax.broadcasted_iota(jnp.int32, t.shape, 1) % 64
    return jnp.where(lane < 8, pltpu.roll(t, TILE - 8, 1), jnp.where(lane < 16, pltpu.roll(t, 8, 1), 0.0))


def _rope(t, cos_t, sin_t):
    return t * cos_t + _swap16(t) * sin_t


def _rope_bwd(g, cos_t, sin_t):
    return g * cos_t + _swap16(g * sin_t)


def _att_in_specs(nb, rev):
    def bidx(n):
        return (nb - 1 - n) if rev else n
    return [
        pl.BlockSpec((ATT_NT, CHUNK, TILE), lambda n: (0, bidx(n), 0)),
        pl.BlockSpec((2, CHUNK, TILE), lambda n: (2, jnp.maximum(bidx(n) - 1, 0), 0)),
        pl.BlockSpec((CHUNK, TILE), lambda n: (bidx(n), 0)),
        pl.BlockSpec((CHUNK, TILE), lambda n: (bidx(n), 0)),
        pl.BlockSpec((CHUNK, TILE), lambda n: (jnp.maximum(bidx(n) - 1, 0), 0)),
        pl.BlockSpec((CHUNK, TILE), lambda n: (jnp.maximum(bidx(n) - 1, 0), 0)),
        pl.BlockSpec((1, 16), lambda n: (0, 0)),
    ]


ATT_SCALE = 0.125


def _tri4():
    row = lax.broadcasted_iota(jnp.int32, (ATT_ROWS, CHUNK), 0) % CHUNK
    col = lax.broadcasted_iota(jnp.int32, (ATT_ROWS, CHUNK), 1)
    return col <= row


def _stack_heads(ref):
    return jnp.concatenate([ref[:, 64 * r:64 * (r + 1)] for r in range(4)], axis=0)


def _sink_col(sinks, g):
    return [sinks[:, 4 * g + r:4 * g + r + 1] for r in range(4)]


def _softmax_rows(s_s, pn_s, pc_s, sink, tri, has_prev):
    sub = lax.broadcasted_iota(jnp.int32, (ATT_ROWS, 1), 0)
    sk = jnp.where(sub < CHUNK, sink[0], jnp.where(sub < 2 * CHUNK, sink[1], jnp.where(sub < 3 * CHUNK, sink[2], sink[3])))
    s = jnp.where(tri, s_s[:, CHUNK:2 * CHUNK], jnp.where(has_prev, s_s[:, 0:CHUNK], -jnp.inf)) * ATT_SCALE
    m = jnp.maximum(jnp.max(s, axis=-1, keepdims=True), sk)
    p = jnp.exp(s - m)
    e_sink = jnp.exp(sk - m)
    inv = 1.0 / (jnp.sum(p, axis=-1, keepdims=True) + e_sink)
    pn = p * inv
    pc_s[...] = pn
    pn_s[:, 0:CHUNK] = jnp.where(tri, 0.0, pn)
    pn_s[:, CHUNK:2 * CHUNK] = jnp.where(tri, pn, 0.0)
    return e_sink * inv


def _att2_fwd(u, cos_t, sin_t, sinks, name):
    t_len = u.shape[1]
    nb = t_len // CHUNK

    def body(u_ref, prev_ref, cc_ref, sc_ref, cp_ref, sp_ref, sink_ref, a_ref,
             q_s, kp_s, kc_s, vp_s, vc_s, o_s, s_s, pn_s, pc_s):
        n = pl.program_id(0)
        tri = _tri4()
        cos_c, sin_c = cc_ref[...], sc_ref[...]
        kc_s[...] = _rope(u_ref[4], cos_c, sin_c)
        kp_s[...] = _rope(prev_ref[0], cp_ref[...], sp_ref[...])
        vc_s[...] = u_ref[5]
        vp_s[...] = prev_ref[1]
        sinks = sink_ref[...]
        for g in range(4):
            q_s[g] = _rope(u_ref[g], cos_c, sin_c)
        for g in range(4):
            kv = slice(64 * g, 64 * (g + 1))
            kb = jnp.concatenate([kp_s[:, kv], kc_s[:, kv]], axis=0)
            s_s[g] = _dot_nt(_stack_heads(q_s.at[g]), kb)
        for g in range(4):
            _softmax_rows(s_s.at[g], pn_s.at[g], pc_s.at[g], _sink_col(sinks, g), tri, n > 0)
        for g in range(4):
            kv = slice(64 * g, 64 * (g + 1))
            vb = jnp.concatenate([vp_s[:, kv], vc_s[:, kv]], axis=0)
            o = _dot(pn_s[g], vb)
            for r in range(4):
                o_s[g, :, 64 * r:64 * (r + 1)] = o[CHUNK * r:CHUNK * (r + 1), :]
        for g in range(4):
            a_ref[g] = (o_s[g] * _silu(u_ref[6 + g])).astype(a_ref.dtype)

    return _pc(
        body, name, grid=(nb,),
        in_specs=_att_in_specs(nb, False),
        out_specs=pl.BlockSpec((ATT_KT, CHUNK, TILE), lambda n: (0, n, 0)),
        out_shape=SDS((ATT_KT, t_len, TILE), MXU_DTYPE),
        scratch_shapes=[pltpu.VMEM((4, CHUNK, TILE), f32)] + [pltpu.VMEM((CHUNK, TILE), f32)] * 4
                       + [pltpu.VMEM((4, CHUNK, TILE), f32)] + [pltpu.VMEM((4, ATT_ROWS, 2 * CHUNK), f32)] * 2
                       + [pltpu.VMEM((4, ATT_ROWS, CHUNK), f32)],
        compiler_params=_params(("arbitrary",)),
    )(u, u, cos_t, sin_t, cos_t, sin_t, sinks)


def _att2_bwd(u, cos_t, sin_t, sinks, da, name):
    t_len = u.shape[1]
    nb = t_len // CHUNK

    def body(u_ref, prev_ref, cc_ref, sc_ref, cp_ref, sp_ref, sink_ref, da_ref, du_ref, dsink_ref,
             ck_s, cv_s, kp_s, kc_s, vp_s, vc_s, q_s, o_s, do_s, dq_s, s_s, pn_s, dp_s, dkt_s, dvt_s, pc_s):
        step = pl.program_id(0)
        nn = nb - 1 - step

        @pl.when(step == 0)
        def _():
            ck_s[...] = jnp.zeros_like(ck_s)
            cv_s[...] = jnp.zeros_like(cv_s)
            dsink_ref[...] = jnp.zeros_like(dsink_ref)
        tri = _tri4()
        cos_c, sin_c = cc_ref[...], sc_ref[...]
        cos_p, sin_p = cp_ref[...], sp_ref[...]
        kc_s[...] = _rope(u_ref[4], cos_c, sin_c)
        kp_s[...] = _rope(prev_ref[0], cos_p, sin_p)
        vc_s[...] = u_ref[5]
        vp_s[...] = prev_ref[1]
        sinks = sink_ref[...]
        lane16 = lax.broadcasted_iota(jnp.int32, (1, 16), 1)
        dsink = jnp.zeros((1, 16), f32)
        for g in range(4):
            q_g, o_g, do_g, dq_g = q_s.at[g], o_s.at[g], do_s.at[g], dq_s.at[g]
            s_g, pn_g, dp_g, pc_g = s_s.at[g], pn_s.at[g], dp_s.at[g], pc_s.at[g]
            q_g[...] = _rope(u_ref[g], cos_c, sin_c)
            gate = u_ref[6 + g]
            sg = _sigmoid(gate)
            dav = da_ref[g]
            do_g[...] = dav * (gate * sg)
            kv = slice(64 * g, 64 * (g + 1))
            kb = jnp.concatenate([kp_s[:, kv], kc_s[:, kv]], axis=0)
            vb = jnp.concatenate([vp_s[:, kv], vc_s[:, kv]], axis=0)
            q_st = _stack_heads(q_g)
            do_st = _stack_heads(do_g)
            s_g[...] = _dot_nt(q_st, kb)
            p_sink = _softmax_rows(s_g, pn_g, pc_g, _sink_col(sinks, g), tri, nn > 0)
            o = _dot(pn_g[...], vb)
            dvt_s[64 * g:64 * (g + 1), :] = _dot_tn(do_st, pn_g[...])
            dp_g[...] = _dot_nt(do_st, vb)
            delta = jnp.sum(do_st * o, axis=-1, keepdims=True)
            dpc = jnp.where(tri, dp_g[:, CHUNK:2 * CHUNK], dp_g[:, 0:CHUNK])
            dsc = pc_g[...] * (dpc - delta) * ATT_SCALE
            dp_g[:, 0:CHUNK] = jnp.where(tri, 0.0, dsc)
            dp_g[:, CHUNK:2 * CHUNK] = jnp.where(tri, dsc, 0.0)
            sd = p_sink * delta
            for r in range(4):
                rs = slice(CHUNK * r, CHUNK * (r + 1))
                o_g[:, 64 * r:64 * (r + 1)] = o[rs, :]
                ds_h = -jnp.sum(sd[rs, :], axis=0, keepdims=True)
                dsink = dsink + ds_h * (lane16 == 4 * g + r).astype(f32)
            ds = dp_g[...]
            dq = _dot(ds, kb)
            for r in range(4):
                dq_g[:, 64 * r:64 * (r + 1)] = dq[CHUNK * r:CHUNK * (r + 1), :]
            dkt_s[64 * g:64 * (g + 1), :] = _dot_tn(q_st, ds)
            du_ref[6 + g] = dav * o_g[...] * (sg * (1.0 + gate * (1.0 - sg)))
            du_ref[g] = _rope_bwd(dq_g[...], cos_c, sin_c)
        dk = dkt_s[...].T
        dv = dvt_s[...].T
        du_ref[4] = _rope_bwd(dk[CHUNK:2 * CHUNK, :], cos_c, sin_c) + ck_s[...]
        du_ref[5] = dv[CHUNK:2 * CHUNK, :] + cv_s[...]
        ck_s[...] = _rope_bwd(dk[0:CHUNK, :], cos_p, sin_p)
        cv_s[...] = dv[0:CHUNK, :]
        dsink_ref[...] += dsink

    def rb(n):
        return nb - 1 - n

    return _pc(
        body, name, grid=(nb,),
        in_specs=_att_in_specs(nb, True) + [pl.BlockSpec((ATT_KT, CHUNK, TILE), lambda n: (0, rb(n), 0))],
        out_specs=[pl.BlockSpec((ATT_NT, CHUNK, TILE), lambda n: (0, rb(n), 0)),
                   pl.BlockSpec((1, 16), lambda n: (0, 0))],
        out_shape=[SDS((ATT_NT, t_len, TILE), f32), SDS((1, 16), f32)],
        scratch_shapes=[pltpu.VMEM((CHUNK, TILE), f32)] * 6 + [pltpu.VMEM((4, CHUNK, TILE), f32)] * 4
                       + [pltpu.VMEM((4, ATT_ROWS, 2 * CHUNK), f32)] * 3
                       + [pltpu.VMEM((2 * CHUNK, 2 * CHUNK), f32)] * 2 + [pltpu.VMEM((4, ATT_ROWS, CHUNK), f32)],
        compiler_params=_params(("arbitrary",)),
    )(u, u, cos_t, sin_t, cos_t, sin_t, sinks, da)


_HBM = pl.BlockSpec(memory_space=pltpu.HBM)


def _all_gather_big(shards):
    n = len(shards)

    def body(*refs):
        x_refs, out_refs = refs[:n], refs[n:2 * n]
        send_sems, recv_sems, local_sems = refs[2 * n:]
        x, y, c = lax.axis_index("x"), lax.axis_index("y"), lax.axis_index("c")
        me, sibling = (x, y, c), (x, y, 1 - c)
        chips = [(1 - x, y), (x, 1 - y), (1 - x, 1 - y)]

        def slot(i, px, py, pc):
            return out_refs[i].at[4 * px + 2 * py + pc]

        def copy(i, k, block, to, src=None):
            return pltpu.make_async_remote_copy(
                src_ref=slot(i, *block) if src is None else src, dst_ref=slot(i, *block),
                send_sem=send_sems.at[7 * i + k], recv_sem=recv_sems.at[7 * i + k], device_id=to, device_id_type=MESH)

        mine = [pltpu.make_async_copy(x_refs[i], slot(i, *me), local_sems.at[i]) for i in range(n)]
        for cp in mine:
            cp.start()
        first = []
        for i in range(n):
            first.append(copy(i, 0, me, sibling, src=x_refs[i]))
            first += [copy(i, 1 + j, me, (*chip, c), src=x_refs[i]) for j, chip in enumerate(chips)]
        for cp in first:
            cp.start()
        passed = []
        for j, chip in enumerate(chips):
            for i in range(n):
                copy(i, 1 + j, (*chip, c), me).wait_recv()
                fwd = copy(i, 4 + j, (*chip, c), sibling)
                fwd.start()
                passed.append(fwd)
        for i in range(n):
            copy(i, 0, sibling, me).wait_recv()
            for j, chip in enumerate(chips):
                copy(i, 4 + j, (*chip, 1 - c), me).wait_recv()
        for cp in first + passed:
            cp.wait_send()
        for cp in mine:
            cp.wait()

    return _pc(
        body, "all_gather_big",
        in_specs=[_HBM] * n, out_specs=[_HBM] * n,
        out_shape=[SDS((N_DEV,) + s.shape, s.dtype) for s in shards],
        scratch_shapes=[pltpu.SemaphoreType.DMA((7 * n,)), pltpu.SemaphoreType.DMA((7 * n,)),
                        pltpu.SemaphoreType.DMA((n,))],
    )(*shards)


def _all_gather_direct(block, name):
    rows, width = block.shape

    def body(x_ref, out_ref, send_sems, recv_sems, local_sem):
        x, y, c = lax.axis_index("x"), lax.axis_index("y"), lax.axis_index("c")
        my_slot = 4 * x + 2 * y + c

        def peer(k):
            return (1 - x if k & 4 else x, 1 - y if k & 2 else y, 1 - c if k & 1 else c)

        def copy(k):
            px, py, pc = peer(k)
            return pltpu.make_async_remote_copy(
                src_ref=x_ref, dst_ref=out_ref.at[my_slot], send_sem=send_sems.at[k - 1], recv_sem=recv_sems.at[k - 1],
                device_id=(px, py, pc), device_id_type=MESH)

        def arrival(k):
            px, py, pc = peer(k)
            return pltpu.make_async_remote_copy(
                src_ref=x_ref, dst_ref=out_ref.at[4 * px + 2 * py + pc], send_sem=send_sems.at[k - 1],
                recv_sem=recv_sems.at[k - 1], device_id=(px, py, pc), device_id_type=MESH)

        mine = pltpu.make_async_copy(x_ref, out_ref.at[my_slot], local_sem)
        mine.start()
        for k in range(1, N_DEV):
            copy(k).start()
        for k in range(1, N_DEV):
            arrival(k).wait_recv()
        for k in range(1, N_DEV):
            copy(k).wait_send()
        mine.wait()

    return _pc(
        body, name,
        in_specs=[_HBM], out_specs=_HBM,
        out_shape=SDS((N_DEV, rows, width), block.dtype),
        scratch_shapes=[pltpu.SemaphoreType.DMA((7,)), pltpu.SemaphoreType.DMA((7,)), pltpu.SemaphoreType.DMA],
    )(block)


N_CHIP = N_DEV // 2


def _exchange_sibling(gs):
    n = len(gs)

    def body(*refs):
        g_refs, out_refs = refs[:n], refs[n:2 * n]
        send_sems, recv_sems = refs[2 * n:]
        x, y, c = lax.axis_index("x"), lax.axis_index("y"), lax.axis_index("c")
        cps = [pltpu.make_async_remote_copy(
            src_ref=g_refs[i].at[2 * k + 1 - c], dst_ref=out_refs[i].at[k], send_sem=send_sems.at[N_CHIP * i + k],
            recv_sem=recv_sems.at[N_CHIP * i + k], device_id=(x, y, 1 - c), device_id_type=MESH)
            for i in range(n) for k in range(N_CHIP)]
        for cp in cps:
            cp.start()
        for cp in cps:
            cp.wait()

    return _pc(
        body, "rs_sibling",
        in_specs=[_HBM] * n, out_specs=[_HBM] * n,
        out_shape=[SDS((N_CHIP,) + g.shape[1:], g.dtype) for g in gs],
        scratch_shapes=[pltpu.SemaphoreType.DMA((N_CHIP * n,)), pltpu.SemaphoreType.DMA((N_CHIP * n,))],
    )(*gs)


def _pair_sum(g, r1, cidx, tr, name):
    _, rows, width = g.shape

    def body(c_ref, g_ref, r_ref, o_ref):
        o_ref[...] = (g_ref[...].astype(f32) + r_ref[...].astype(f32)).astype(o_ref.dtype)

    return pl.pallas_call(
        body, name=name,
        grid_spec=pltpu.PrefetchScalarGridSpec(
            num_scalar_prefetch=1, grid=(N_CHIP, rows // tr),
            in_specs=[pl.BlockSpec((1, tr, width), lambda k, i, c_ref: (2 * k + c_ref[0], i, 0)),
                      pl.BlockSpec((1, tr, width), lambda k, i, c_ref: (k, i, 0))],
            out_specs=pl.BlockSpec((1, tr, width), lambda k, i, c_ref: (k, i, 0))),
        out_shape=SDS((N_CHIP, rows, width), g.dtype),
        compiler_params=_params(("arbitrary", "arbitrary")),
    )(cidx, g, r1)


def _exchange_chips(ps):
    n = len(ps)

    def body(*refs):
        p_refs, out_refs = refs[:n], refs[n:2 * n]
        send_sems, recv_sems, local_sems = refs[2 * n:]
        x, y, c = lax.axis_index("x"), lax.axis_index("y"), lax.axis_index("c")
        my_chip = 2 * x + y
        chips = [(1 - x, y), (x, 1 - y), (1 - x, 1 - y)]

        def copy(i, j):
            px, py = chips[j]
            return pltpu.make_async_remote_copy(
                src_ref=p_refs[i].at[2 * px + py], dst_ref=out_refs[i].at[my_chip], send_sem=send_sems.at[3 * i + j],
                recv_sem=recv_sems.at[3 * i + j], device_id=(px, py, c), device_id_type=MESH)

        def arrival(i, j):
            px, py = chips[j]
            return pltpu.make_async_remote_copy(
                src_ref=p_refs[i].at[my_chip], dst_ref=out_refs[i].at[2 * px + py], send_sem=send_sems.at[3 * i + j],
                recv_sem=recv_sems.at[3 * i + j], device_id=(px, py, c), device_id_type=MESH)

        mine = [pltpu.make_async_copy(p_refs[i].at[my_chip], out_refs[i].at[my_chip], local_sems.at[i])
                for i in range(n)]
        for cp in mine:
            cp.start()
        for i in range(n):
            for j in range(3):
                copy(i, j).start()
        for i in range(n):
            for j in range(3):
                arrival(i, j).wait_recv()
        for i in range(n):
            for j in range(3):
                copy(i, j).wait_send()
        for cp in mine:
            cp.wait()

    return _pc(
        body, "rs_chips",
        in_specs=[_HBM] * n, out_specs=[_HBM] * n,
        out_shape=[SDS(p.shape, p.dtype) for p in ps],
        scratch_shapes=[pltpu.SemaphoreType.DMA((3 * n,)), pltpu.SemaphoreType.DMA((3 * n,)),
                        pltpu.SemaphoreType.DMA((n,))],
    )(*ps)


def _adamw(parts, w, m, v, tr, name):
    n, rows, width = parts.shape
    c1 = 1.0 / (1.0 - ADAM_B1 ** ADAM_STEP)
    c2 = 1.0 / (1.0 - ADAM_B2 ** ADAM_STEP)

    def body(p_ref, w_ref, m_ref, v_ref, g_ref, d_ref, mo_ref, vo_ref):
        g = p_ref[0].astype(f32)
        for k in range(1, n):
            g = g + p_ref[k].astype(f32)
        mn = ADAM_B1 * m_ref[...] + (1.0 - ADAM_B1) * g
        vn = ADAM_B2 * v_ref[...] + (1.0 - ADAM_B2) * (g * g)
        g_ref[...] = g
        mo_ref[...] = mn
        vo_ref[...] = vn
        d_ref[...] = -ADAM_LR * ((mn * c1) / (jnp.sqrt(vn * c2) + ADAM_EPS) + ADAM_WD * w_ref[...])

    blk = pl.BlockSpec((tr, width), lambda i: (i, 0))
    return _pc(
        body, name, grid=(rows // tr,),
        in_specs=[pl.BlockSpec((n, tr, width), lambda i: (0, i, 0)), blk, blk, blk],
        out_specs=[blk, blk, blk, blk],
        out_shape=[SDS((rows, width), f32)] * 4,
        compiler_params=_params(("arbitrary",)),
    )(parts, w, m, v)


ROWS_REST = ROWS_SSM_OUT + ROWS_ATT_OUT + 16


def _pack_rest(ssm_w_out, att_w_out, conv_w):
    conv = jnp.pad(conv_w.reshape(4, 1024), ((0, 12), (0, 0)))
    return jnp.concatenate([ssm_w_out.reshape(ROWS_SSM_OUT, 1024), att_w_out.reshape(ROWS_ATT_OUT, 1024), conv], axis=0)


def _unpack_rest(p):
    o = ROWS_SSM_OUT + ROWS_ATT_OUT
    return (p[0:ROWS_SSM_OUT].reshape(2, 256, 1024), p[ROWS_SSM_OUT:o].reshape(2, 128, 1024),
            p[o:o + 4].reshape(2, 4, 512))


def _pack_grads(d_ssm_w_in, d_ssm_w_out, d_att_w_in, d_att_w_out, d_conv_w):
    wire = lambda t: t.astype(MXU_DTYPE)
    a = jnp.transpose(wire(d_ssm_w_in).reshape(2, 1024, 8, 772), (2, 0, 1, 3)).reshape(8, 2048, 772)
    c = jnp.transpose(wire(d_att_w_in).reshape(2, 1024, 8, 320), (2, 0, 1, 3)).reshape(8, 2048, 320)
    b = jnp.transpose(wire(d_ssm_w_out).reshape(2, 8, 256, 1024), (1, 0, 2, 3)).reshape(8, ROWS_SSM_OUT, 1024)
    d = jnp.transpose(wire(d_att_w_out).reshape(2, 8, 128, 1024), (1, 0, 2, 3)).reshape(8, ROWS_ATT_OUT, 1024)
    e = jnp.transpose(wire(d_conv_w).reshape(2, 4, 8, 512), (2, 0, 1, 3)).reshape(8, 4, 1024)
    e = jnp.pad(e, ((0, 0), (0, 12), (0, 0)))
    return a, c, jnp.concatenate([b, d, e], axis=1)


def _pad8(a):
    return jnp.pad(a, ((0, 8 - a.shape[0]), (0, 0)))


def _pack_small(pre_norm, post_norm, conv_b, gate_norm, dt_bias, a_log, d_skip, sinks, extra=None):
    row = jnp.concatenate([dt_bias.reshape(1, 64), a_log.reshape(1, 64), d_skip.reshape(1, 64), sinks.reshape(1, 32),
                           jnp.zeros((1, 1024 - 224), f32)], axis=1)
    if extra is not None:
        row = row + jnp.pad(extra.reshape(1, 1), ((0, 0), (224, 1024 - 225)))
    return jnp.concatenate([_pad8(pre_norm.reshape(4, 1024)), _pad8(post_norm.reshape(4, 1024)),
                            conv_b.reshape(8, 1024), _pad8(gate_norm.reshape(4, 1024)), _pad8(row)], axis=0)


def _unpack_small(p):
    row = p[32]
    return (p[0:4], p[8:12], p[16:24].reshape(2, 4096), row[0:64].reshape(2, 32), row[64:128].reshape(2, 32),
            row[128:192].reshape(2, 32), p[24:28].reshape(2, 2048), row[192:224].reshape(2, 16))


def _ssm_w_in_tiles(w):
    wb = w[:, 4096:5120].reshape(1024, 8, 128)
    wc = w[:, 5120:6144].reshape(1024, 8, 128)
    wbc = jnp.concatenate([wb, wc], axis=2).reshape(1024, 2048)
    return jnp.concatenate([w[:, 0:4096], wbc, w[:, 6144:6176], jnp.zeros((1024, 224), w.dtype)], axis=1)


def _ssm_w_in_untile(dw):
    dbc = dw[:, 4096:6144].reshape(1024, 8, 256)
    return jnp.concatenate([dw[:, 0:4096], dbc[:, :, 0:128].reshape(1024, 1024), dbc[:, :, 128:256].reshape(1024, 1024),
                            dw[:, 6144:6176]], axis=1)


def _conv_tiles(cw):
    k = cw.shape[0]
    xs = jnp.transpose(cw[:, 0:2048].reshape(k, 8, 256), (1, 0, 2))
    b = cw[:, 2048:3072].reshape(k, 8, 128)
    c = cw[:, 3072:4096].reshape(k, 8, 128)
    bc = jnp.transpose(jnp.concatenate([b, c], axis=2), (1, 0, 2))
    return jnp.concatenate([xs, bc], axis=0)


def _conv_untile(t):
    k = t.shape[1]
    xs = jnp.transpose(t[0:8], (1, 0, 2)).reshape(k, 2048)
    bc = jnp.transpose(t[8:16], (1, 0, 2))
    return jnp.concatenate([xs, bc[:, :, 0:128].reshape(k, 1024), bc[:, :, 128:256].reshape(k, 1024)], axis=1)


def _rope_tables(positions):
    inv = ROPE_THETA ** (-jnp.arange(0, 16, 2, dtype=f32) / 16)
    ang = positions.astype(f32).reshape(-1, 1) * inv
    cos, sin = jnp.cos(ang), jnp.sin(ang)
    t_len = ang.shape[0]
    cos64 = jnp.concatenate([cos, cos, jnp.ones((t_len, 48), f32)], axis=1)
    sin64 = jnp.concatenate([-sin, sin, jnp.zeros((t_len, 48), f32)], axis=1)
    return jnp.tile(cos64, (1, 4)), jnp.tile(sin64, (1, 4))


def kernel(x, positions, pre_norm, post_norm, ssm_w_in, ssm_conv_w, ssm_conv_b, ssm_dt_bias, ssm_a_log, ssm_d, ssm_gate_norm, ssm_w_out, att_w_in, att_sinks, att_w_out, loss_target, m_pre_norm, m_post_norm, m_ssm_w_in, m_ssm_conv_w, m_ssm_conv_b, m_ssm_dt_bias, m_ssm_a_log, m_ssm_d, m_ssm_gate_norm, m_ssm_w_out, m_att_w_in, m_att_sinks, m_att_w_out, v_pre_norm, v_post_norm, v_ssm_w_in, v_ssm_conv_w, v_ssm_conv_b, v_ssm_dt_bias, v_ssm_a_log, v_ssm_d, v_ssm_gate_norm, v_ssm_w_out, v_att_w_in, v_att_sinks, v_att_w_out):
    t_len = x.shape[1]
    tm = min(1024, t_len)
    xin = x.reshape(t_len, D_MODEL)
    tgt = loss_target.reshape(t_len, D_MODEL)
    cidx = lax.axis_index("c").astype(jnp.int32).reshape(1)

    g_ssm_in, g_att_in, g_ssm_out, g_att_out = _all_gather_big(
        [ssm_w_in.astype(MXU_DTYPE), att_w_in.astype(MXU_DTYPE), ssm_w_out.astype(MXU_DTYPE),
         att_w_out.astype(MXU_DTYPE)])
    conv_local = jnp.concatenate([ssm_conv_w.reshape(4, 1024), jnp.zeros((4, 1024), f32)], axis=0)
    conv_all = _all_gather_direct(conv_local, "all_gather_conv")[:, 0:4]
    w_ssm_in = jnp.transpose(g_ssm_in, (1, 2, 0, 3)).reshape(2, 1024, SSM_IN)
    w_ssm_out = jnp.transpose(g_ssm_out, (1, 0, 2, 3)).reshape(2, SSM_INNER, 1024)
    w_att_in = jnp.transpose(g_att_in, (1, 2, 0, 3)).reshape(2, 1024, ATT_IN)
    w_att_out = jnp.transpose(g_att_out, (1, 0, 2, 3)).reshape(2, 1024, 1024)
    conv_w = jnp.transpose(conv_all.reshape(8, 2, 4, 512), (1, 2, 0, 3)).reshape(2, 4, 4096)
    cos_t, sin_t = _rope_tables(positions)

    saved = []
    xc = xin
    for i in range(4):
        j = i // 2
        wn_pre, wn_post = pre_norm[i].reshape(1, D_MODEL), post_norm[i].reshape(1, D_MODEL)
        if i % 2 == 0:
            w_in = _ssm_w_in_tiles(w_ssm_in[j])
            cw, cb = _conv_tiles(conv_w[j]), _conv_tiles(ssm_conv_b[j].reshape(1, 4096))
            dtb, alog, dsk = ssm_dt_bias[j].reshape(1, 32), ssm_a_log[j].reshape(1, 32), ssm_d[j].reshape(1, 32)
            gn = ssm_gate_norm[j].reshape(SSM_KT, 1, TILE)
            w_zdt = jnp.concatenate([w_in[:, 0:SSM_INNER], w_in[:, 24 * TILE:25 * TILE]], axis=1)
            u, h = _mm_in(xc, wn_pre, w_zdt, 9, tm, f"ssm_in_{j}")
            uc, xbc, dsl = _mm_in_conv(xc, wn_pre, w_in[:, SSM_INNER:24 * TILE], cw, cb, tm, f"ssm_inconv_{j}")
            pre = _ssm_pre(u, dtb, alog, dsk, f"ssm_pre_{j}")
            a3, yp, hst = _ssm2_fwd(u, xbc, pre, dtb, alog, dsk, gn, f"ssm_core_{j}")
            y, xn = _mm_out(a3, w_ssm_out[j], xc, wn_post, 4, tm, f"ssm_out_{j}")
            saved.append(dict(x=xc, u=u, uc=uc, dsl=dsl, h=h, a=a3, yp=yp, hst=hst, y=y, w_in=w_in, cw=cw, dtb=dtb,
                              alog=alog, dsk=dsk, gn=gn, xbc=xbc, pre=pre))
        else:
            sinks = att_sinks[j].reshape(1, 16)
            u, h = _mm_in(xc, wn_pre, w_att_in[j], 5, tm, f"att_in_{j}")
            a = _att2_fwd(u, cos_t, sin_t, sinks, f"att_core_{j}")
            y, xn = _mm_out(a, w_att_out[j], xc, wn_post, 4, tm, f"att_out_{j}")
            saved.append(dict(x=xc, u=u, h=h, a=a, y=y, sinks=sinks))
        xc = xn

    dx, loss_part = _loss_grad(xc, tgt, tm)

    d_pre, d_post = [None] * 4, [None] * 4
    d_ssm_in, d_ssm_out, d_att_in, d_att_out = [None] * 2, [None] * 2, [None] * 2, [None] * 2
    d_cw, d_cb, d_dtb, d_alog, d_dsk, d_gn, d_sinks = ([None] * 2 for _ in range(7))
    for i in reversed(range(4)):
        j = i // 2
        s = saved[i]
        wn_pre, wn_post = pre_norm[i].reshape(1, D_MODEL), post_norm[i].reshape(1, D_MODEL)
        if i % 2 == 0:
            da3, dy, d_post[i] = _mm_dout(s["y"], dx, wn_post, w_ssm_out[j], 4, tm, f"ssm_dout_{j}")
            d_ssm_out[j] = _dw_rows(s["a"], dy, 4, tm, f"ssm_dwout_{j}")
            du, d_dtb[j], d_alog[j], d_dsk[j], dgn = _ssm2_bwd(
                s["u"], s["xbc"], s["pre"], s["yp"], s["hst"], da3, s["dtb"], s["alog"], s["dsk"], s["gn"],
                f"ssm_core_bwd_{j}")
            du, dcw, dcb = _conv_bwd(s["uc"], s["dsl"], du, s["cw"], f"ssm_conv_bwd_{j}")
            d_cw[j], d_cb[j], d_gn[j] = _conv_untile(dcw), _conv_untile(dcb), dgn.reshape(1, SSM_INNER)
            d_ssm_in[j] = _ssm_w_in_untile(_dw_cols(s["h"], du, 5, tm, f"ssm_dwin_{j}"))
            dx, d_pre[i] = _mm_dh(du, s["w_in"], s["x"], dx, wn_pre, 5, tm, f"ssm_dh_{j}")
        else:
            da, dy, d_post[i] = _mm_dout(s["y"], dx, wn_post, w_att_out[j], 4, tm, f"att_dout_{j}")
            d_att_out[j] = _dw_rows(s["a"], dy, 4, tm, f"att_dwout_{j}")
            du, d_sinks[j] = _att2_bwd(s["u"], cos_t, sin_t, s["sinks"], da, f"att_core_bwd_{j}")
            d_att_in[j] = _dw_cols(s["h"], du, 5, tm, f"att_dwin_{j}")
            dx, d_pre[i] = _mm_dh(du, w_att_in[j], s["x"], dx, wn_pre, 5, tm, f"att_dh_{j}")

    gs = _pack_grads(jnp.stack(d_ssm_in), jnp.stack(d_ssm_out), jnp.stack(d_att_in), jnp.stack(d_att_out),
                     jnp.stack(d_cw))
    r1 = _exchange_sibling(gs)
    tiles = (256, 256, ROWS_REST // 7)
    pairs = [_pair_sum(g, r, cidx, tr, f"rs_pair_sum_{k}") for k, (g, r, tr) in enumerate(zip(gs, r1, tiles))]
    parts = _exchange_chips(pairs)
    flat = lambda t: t.reshape(2048, t.shape[-1])
    a4 = _adamw(parts[0], flat(ssm_w_in), flat(m_ssm_w_in), flat(v_ssm_w_in), tiles[0], "adamw_ssm_in")
    b4 = _adamw(parts[1], flat(att_w_in), flat(m_att_w_in), flat(v_att_w_in), tiles[1], "adamw_att_in")
    c4 = _adamw(parts[2], _pack_rest(ssm_w_out, att_w_out, ssm_conv_w), _pack_rest(m_ssm_w_out, m_att_w_out, m_ssm_conv_w),
                _pack_rest(v_ssm_w_out, v_att_w_out, v_ssm_conv_w), tiles[2], "adamw_rest")
    big = []
    for k in range(4):
        o_ssm_out, o_att_out, o_conv = _unpack_rest(c4[k])
        big.append((a4[k].reshape(2, 1024, 772), o_ssm_out, b4[k].reshape(2, 1024, 320), o_att_out, o_conv))

    small_local = _pack_small(jnp.concatenate(d_pre, axis=0), jnp.concatenate(d_post, axis=0),
                              jnp.concatenate(d_cb, axis=0), jnp.concatenate(d_gn, axis=0),
                              jnp.concatenate(d_dtb, axis=0), jnp.concatenate(d_alog, axis=0),
                              jnp.concatenate(d_dsk, axis=0), jnp.concatenate(d_sinks, axis=0), loss_part[0, 0])
    small_all = _all_gather_direct(small_local, "all_gather_small")
    ws = _pack_small(pre_norm, post_norm, ssm_conv_b, ssm_gate_norm, ssm_dt_bias, ssm_a_log, ssm_d, att_sinks)
    ms = _pack_small(m_pre_norm, m_post_norm, m_ssm_conv_b, m_ssm_gate_norm, m_ssm_dt_bias, m_ssm_a_log, m_ssm_d,
                     m_att_sinks)
    vs = _pack_small(v_pre_norm, v_post_norm, v_ssm_conv_b, v_ssm_gate_norm, v_ssm_dt_bias, v_ssm_a_log, v_ssm_d,
                     v_att_sinks)
    small4 = _adamw(small_all, ws, ms, vs, ROWS_SMALL, "adamw_small")
    loss = small4[0][32, 224]
    small = [_unpack_small(t) for t in small4]

    outs = [loss, dx.reshape(1, t_len, D_MODEL)]
    for k in range(4):
        b_ssm_in, b_ssm_out, b_att_in, b_att_out, b_conv = big[k]
        s_pre, s_post, s_cb, s_dtb, s_alog, s_d, s_gn, s_sinks = small[k]
        outs += [s_pre, s_post, b_ssm_in, b_conv, s_cb, s_dtb, s_alog, s_d, s_gn, b_ssm_out, b_att_in, s_sinks,
                 b_att_out]
    return tuple(outs)
```
